```python
import math
import jax, jax.numpy as jnp
from jax import lax
import numpy as np

D_MODEL = 1024
BATCH = 8
SEQ = 4096
DEPTH = 4

DN_HEADS = 8
DN_HEAD_DIM = 128
DN_WIDTH = DN_HEADS * DN_HEAD_DIM
CONV_K = 4
CHUNK = 64
SSM_WIDTH = D_MODEL
SSM_GROUP = 16
SSM_GROUPS = SSM_WIDTH // SSM_GROUP
SSM_STATE = 64
DT_MIN = 0.001
DT_MAX = 0.1
D_IN = 3 * DN_WIDTH + DN_WIDTH + 2 * DN_HEADS + 2 * SSM_WIDTH + 2 * D_MODEL
EPS = 1e-6

kernel_name = "hybrid_deltanet_s5_gated_merge"


def _in_split_points():
    sizes = (3 * DN_WIDTH, DN_WIDTH, DN_HEADS, DN_HEADS, SSM_WIDTH, SSM_WIDTH, D_MODEL, D_MODEL)
    return [int(p) for p in np.cumsum(sizes)[:-1]]


def rms_norm(x, gain):
    xf = x.astype(jnp.float32)
    xf = xf * lax.rsqrt(jnp.mean(xf * xf, axis=-1, keepdims=True) + EPS)
    return (xf * gain.astype(jnp.float32)).astype(x.dtype)


def l2_normalize(x):
    xf = x.astype(jnp.float32)
    return xf * lax.rsqrt(jnp.sum(xf * xf, axis=-1, keepdims=True) + EPS)


def causal_depthwise_conv(x, w):
    return lax.conv_general_dilated(
        x, w[:, None, :].astype(x.dtype), window_strides=(1,), padding=[(CONV_K - 1, 0)],
        dimension_numbers=("NWC", "WIO", "NWC"), feature_group_count=x.shape[-1])


def chunked_gated_delta_rule(q, k, v, g, beta):
    b, s, h, d = q.shape
    n = s // CHUNK

    def chunks(t):
        t = jnp.moveaxis(t, 2, 1)
        return t.reshape((b, h, n, CHUNK) + t.shape[3:])

    q = chunks(q) * (d ** -0.5)
    k = chunks(k)
    v = chunks(v)
    g = jnp.cumsum(chunks(g), axis=-1)
    beta = chunks(beta)[..., None]
    idx = jnp.arange(CHUNK)
    causal = idx[:, None] >= idx[None, :]
    strict = idx[:, None] > idx[None, :]
    decay = jnp.exp(jnp.where(causal, g[..., :, None] - g[..., None, :], -jnp.inf))
    kb = k * beta
    l_mat = jnp.where(strict, jnp.einsum("bhncd,bhnjd->bhncj", kb, k) * decay, 0.0)
    rhs = jnp.concatenate([v * beta, kb * jnp.exp(g)[..., None]], axis=-1)
    sol = lax.linalg.triangular_solve(l_mat, rhs, left_side=True, lower=True, unit_diagonal=True)
    u_val, w = sol[..., :d], sol[..., d:]
    a_qk = jnp.einsum("bhncd,bhnjd->bhncj", q, k) * decay
    q_dec = q * jnp.exp(g)[..., None]
    g_last = g[..., -1]
    k_dec = k * jnp.exp(g_last[..., None] - g)[..., None]

    def step(state, xs):
        u_i, w_i, q_i, k_i, a_i, gl_i = xs
        v_new = u_i - jnp.einsum("bhcd,bhde->bhce", w_i, state)
        o_i = jnp.einsum("bhcd,bhde->bhce", q_i, state) + jnp.einsum("bhcj,bhje->bhce", a_i, v_new)
        state = state * jnp.exp(gl_i)[..., None, None] + jnp.einsum("bhcd,bhce->bhde", k_i, v_new)
        return state, o_i

    xs = tuple(jnp.moveaxis(t, 2, 0) for t in (u_val, w, q_dec, k_dec, a_qk, g_last))
    state0 = jnp.zeros((b, h, d, v.shape[-1]), jnp.float32)
    _, o = lax.scan(step, state0, xs)
    o = jnp.moveaxis(o, 0, 2).reshape(b, h, s, -1)
    return jnp.moveaxis(o, 1, 2)


def s5_ssm(u, a_re, a_im, log_dt, b_re, b_im, c_re, c_im, d_skip):
    bsz, s, _ = u.shape
    uf = u.astype(jnp.float32).reshape(bsz, s, SSM_GROUPS, SSM_GROUP)
    dt = jnp.exp(log_dt.astype(jnp.float32))[:, None]
    ar = a_re.astype(jnp.float32)
    ai = a_im.astype(jnp.float32)
    mag = jnp.exp(ar * dt)
    lr, li = mag * jnp.cos(ai * dt), mag * jnp.sin(ai * dt)
    den = ar * ar + ai * ai
    fr = ((lr - 1.0) * ar + li * ai) / den
    fi = (li * ar - (lr - 1.0) * ai) / den
    br, bi = b_re.astype(jnp.float32), b_im.astype(jnp.float32)
    bbr = fr[..., None] * br - fi[..., None] * bi
    bbi = fr[..., None] * bi + fi[..., None] * br
    xr = jnp.einsum("bsgc,gnc->bsgn", uf, bbr)
    xi = jnp.einsum("bsgc,gnc->bsgn", uf, bbi)
    lam_r = jnp.broadcast_to(lr, (s,) + lr.shape)[None]
    lam_i = jnp.broadcast_to(li, (s,) + li.shape)[None]

    def combine(e1, e2):
        a1r, a1i, b1r, b1i = e1
        a2r, a2i, b2r, b2i = e2
        return (a2r * a1r - a2i * a1i, a2r * a1i + a2i * a1r,
                a2r * b1r - a2i * b1i + b2r, a2r * b1i + a2i * b1r + b2i)

    _, _, hr, hi = lax.associative_scan(combine, (lam_r, lam_i, xr, xi), axis=1)
    y = (jnp.einsum("bsgn,gcn->bsgc", hr, c_re.astype(jnp.float32))
         - jnp.einsum("bsgn,gcn->bsgc", hi, c_im.astype(jnp.float32)))
    return y.reshape(bsz, s, SSM_WIDTH) + d_skip.astype(jnp.float32) * u.astype(jnp.float32)


def hybrid_layer(x, norm_pre, w_in, conv_w, a_log, dt_bias, head_norm,
                 ssm_a_re, ssm_a_im, ssm_log_dt, ssm_b_re, ssm_b_im, ssm_c_re, ssm_c_im, ssm_d,
                 w_glu, b_glu, w_out, norm_post):
    bsz, s, _ = x.shape
    h = rms_norm(x, norm_pre)
    proj = h @ w_in
    qkv, z_a, beta_logit, decay_logit, u, z_b, r_a, r_b = jnp.split(proj, _in_split_points(), axis=-1)

    qkv = jax.nn.silu(causal_depthwise_conv(qkv, conv_w))
    q, k, v = jnp.split(qkv, 3, axis=-1)
    heads = lambda t: t.reshape(bsz, s, DN_HEADS, DN_HEAD_DIM)
    q, k = l2_normalize(heads(q)), l2_normalize(heads(k))
    v = heads(v).astype(jnp.float32)
    beta = jax.nn.sigmoid(beta_logit.astype(jnp.float32))
    g = -jnp.exp(a_log.astype(jnp.float32)) * jax.nn.softplus(
        decay_logit.astype(jnp.float32) + dt_bias.astype(jnp.float32))
    o = chunked_gated_delta_rule(q, k, v, g, beta)
    y_a = rms_norm(o, head_norm).reshape(bsz, s, DN_WIDTH).astype(x.dtype) * jax.nn.silu(z_a)

    y = jax.nn.gelu(s5_ssm(u, ssm_a_re, ssm_a_im, ssm_log_dt, ssm_b_re, ssm_b_im,
                           ssm_c_re, ssm_c_im, ssm_d))
    y = y * jax.nn.sigmoid(y @ w_glu.astype(jnp.float32) + b_glu.astype(jnp.float32))
    y_b = y.astype(x.dtype) * jax.nn.silu(z_b)

    merged = jax.nn.sigmoid(r_a) * y_a + jax.nn.sigmoid(r_b) * y_b
    return x + rms_norm(merged @ w_out, norm_post)


def _fwd_setup_inputs(seed: int = 0) -> dict:
    key = jax.random.key(seed)
    ks = jax.random.split(key, 20)
    f32 = jnp.float32
    nrm = lambda k, shape, scale: jax.random.normal(k, shape, f32) * scale
    gain = lambda k, shape: 1.0 + 0.02 * jax.random.normal(k, shape, f32)
    dt = jnp.exp(jax.random.uniform(ks[4], (DEPTH, DN_HEADS), f32, math.log(DT_MIN), math.log(DT_MAX)))
    return {
        "x": jax.random.normal(ks[0], (BATCH, SEQ, D_MODEL), f32),
        "norm_pre": gain(ks[1], (DEPTH, D_MODEL)),
        "w_in": nrm(ks[2], (DEPTH, D_MODEL, D_IN), D_MODEL ** -0.5),
        "conv_w": nrm(ks[3], (DEPTH, CONV_K, 3 * DN_WIDTH), CONV_K ** -0.5),
        "a_log": jnp.log(jax.random.uniform(ks[5], (DEPTH, DN_HEADS), f32, 1.0, 16.0)),
        "dt_bias": dt + jnp.log(-jnp.expm1(-dt)),
        "head_norm": gain(ks[6], (DEPTH, DN_HEAD_DIM)),
        "ssm_a_re": -0.5 + 0.01 * jax.random.normal(ks[7], (DEPTH, SSM_GROUPS, SSM_STATE), f32),
        "ssm_a_im": math.pi * jnp.arange(SSM_STATE, dtype=f32)
                    + 0.01 * jax.random.normal(ks[8], (DEPTH, SSM_GROUPS, SSM_STATE), f32),
        "ssm_log_dt": jax.random.uniform(ks[9], (DEPTH, SSM_GROUPS), f32, math.log(DT_MIN), math.log(DT_MAX)),
        "ssm_b_re": nrm(ks[10], (DEPTH, SSM_GROUPS, SSM_STATE, SSM_GROUP), (2 * SSM_GROUP) ** -0.5),
        "ssm_b_im": nrm(ks[11], (DEPTH, SSM_GROUPS, SSM_STATE, SSM_GROUP), (2 * SSM_GROUP) ** -0.5),
        "ssm_c_re": nrm(ks[12], (DEPTH, SSM_GROUPS, SSM_GROUP, SSM_STATE), (2 * SSM_STATE) ** -0.5),
        "ssm_c_im": nrm(ks[13], (DEPTH, SSM_GROUPS, SSM_GROUP, SSM_STATE), (2 * SSM_STATE) ** -0.5),
        "ssm_d": nrm(ks[14], (DEPTH, SSM_WIDTH), 1.0),
        "w_glu": nrm(ks[15], (DEPTH, SSM_WIDTH, SSM_WIDTH), SSM_WIDTH ** -0.5),
        "b_glu": nrm(ks[16], (DEPTH, SSM_WIDTH), 0.01),
        "w_out": nrm(ks[17], (DEPTH, D_MODEL, D_MODEL), D_MODEL ** -0.5),
        "norm_post": gain(ks[18], (DEPTH, D_MODEL)),
    }


def _fwd_reference(x, norm_pre, w_in, conv_w, a_log, dt_bias, head_norm,
              ssm_a_re, ssm_a_im, ssm_log_dt, ssm_b_re, ssm_b_im, ssm_c_re, ssm_c_im, ssm_d,
              w_glu, b_glu, w_out, norm_post):
    for i in range(DEPTH):
        x = hybrid_layer(x, norm_pre[i], w_in[i], conv_w[i], a_log[i], dt_bias[i], head_norm[i],
                         ssm_a_re[i], ssm_a_im[i], ssm_log_dt[i], ssm_b_re[i], ssm_b_im[i],
                         ssm_c_re[i], ssm_c_im[i], ssm_d[i], w_glu[i], b_glu[i], w_out[i], norm_post[i])
    return x


import jax as _jax
import jax.numpy as _jnp

TWIN_FORMAT = 'train_step'
FWD_PARAMS = ['x', 'norm_pre', 'w_in', 'conv_w', 'a_log', 'dt_bias', 'head_norm', 'ssm_a_re', 'ssm_a_im', 'ssm_log_dt', 'ssm_b_re', 'ssm_b_im', 'ssm_c_re', 'ssm_c_im', 'ssm_d', 'w_glu', 'b_glu', 'w_out', 'norm_post']
TWIN_WEIGHTS = ['norm_pre', 'w_in', 'conv_w', 'a_log', 'dt_bias', 'head_norm', 'ssm_a_re', 'ssm_a_im', 'ssm_log_dt', 'ssm_b_re', 'ssm_b_im', 'ssm_c_re', 'ssm_c_im', 'ssm_d', 'w_glu', 'b_glu', 'w_out', 'norm_post']
TWIN_DIFF_INPUT = 'x'
TWIN_INPUTS = ['x', 'norm_pre', 'w_in', 'conv_w', 'a_log', 'dt_bias', 'head_norm', 'ssm_a_re', 'ssm_a_im', 'ssm_log_dt', 'ssm_b_re', 'ssm_b_im', 'ssm_c_re', 'ssm_c_im', 'ssm_d', 'w_glu', 'b_glu', 'w_out', 'norm_post', 'loss_target', 'm_norm_pre', 'm_w_in', 'm_conv_w', 'm_a_log', 'm_dt_bias', 'm_head_norm', 'm_ssm_a_re', 'm_ssm_a_im', 'm_ssm_log_dt', 'm_ssm_b_re', 'm_ssm_b_im', 'm_ssm_c_re', 'm_ssm_c_im', 'm_ssm_d', 'm_w_glu', 'm_b_glu', 'm_w_out', 'm_norm_post', 'v_norm_pre', 'v_w_in', 'v_conv_w', 'v_a_log', 'v_dt_bias', 'v_head_norm', 'v_ssm_a_re', 'v_ssm_a_im', 'v_ssm_log_dt', 'v_ssm_b_re', 'v_ssm_b_im', 'v_ssm_c_re', 'v_ssm_c_im', 'v_ssm_d', 'v_w_glu', 'v_b_glu', 'v_w_out', 'v_norm_post']
TWIN_OUTPUTS = ['loss', 'grad_x', 'grad_norm_pre', 'grad_w_in', 'grad_conv_w', 'grad_a_log', 'grad_dt_bias', 'grad_head_norm', 'grad_ssm_a_re', 'grad_ssm_a_im', 'grad_ssm_log_dt', 'grad_ssm_b_re', 'grad_ssm_b_im', 'grad_ssm_c_re', 'grad_ssm_c_im', 'grad_ssm_d', 'grad_w_glu', 'grad_b_glu', 'grad_w_out', 'grad_norm_post', 'delta_norm_pre', 'delta_w_in', 'delta_conv_w', 'delta_a_log', 'delta_dt_bias', 'delta_head_norm', 'delta_ssm_a_re', 'delta_ssm_a_im', 'delta_ssm_log_dt', 'delta_ssm_b_re', 'delta_ssm_b_im', 'delta_ssm_c_re', 'delta_ssm_c_im', 'delta_ssm_d', 'delta_w_glu', 'delta_b_glu', 'delta_w_out', 'delta_norm_post', 'new_m_norm_pre', 'new_m_w_in', 'new_m_conv_w', 'new_m_a_log', 'new_m_dt_bias', 'new_m_head_norm', 'new_m_ssm_a_re', 'new_m_ssm_a_im', 'new_m_ssm_log_dt', 'new_m_ssm_b_re', 'new_m_ssm_b_im', 'new_m_ssm_c_re', 'new_m_ssm_c_im', 'new_m_ssm_d', 'new_m_w_glu', 'new_m_b_glu', 'new_m_w_out', 'new_m_norm_post', 'new_v_norm_pre', 'new_v_w_in', 'new_v_conv_w', 'new_v_a_log', 'new_v_dt_bias', 'new_v_head_norm', 'new_v_ssm_a_re', 'new_v_ssm_a_im', 'new_v_ssm_log_dt', 'new_v_ssm_b_re', 'new_v_ssm_b_im', 'new_v_ssm_c_re', 'new_v_ssm_c_im', 'new_v_ssm_d', 'new_v_w_glu', 'new_v_b_glu', 'new_v_w_out', 'new_v_norm_post']
TWIN_LEAF_KINDS = {'loss': 'loss', 'grad_x': 'grad_x', 'grad_norm_pre': 'grad_w', 'grad_w_in': 'grad_w', 'grad_conv_w': 'grad_w', 'grad_a_log': 'grad_w', 'grad_dt_bias': 'grad_w', 'grad_head_norm': 'grad_w', 'grad_ssm_a_re': 'grad_w', 'grad_ssm_a_im': 'grad_w', 'grad_ssm_log_dt': 'grad_w', 'grad_ssm_b_re': 'grad_w', 'grad_ssm_b_im': 'grad_w', 'grad_ssm_c_re': 'grad_w', 'grad_ssm_c_im': 'grad_w', 'grad_ssm_d': 'grad_w', 'grad_w_glu': 'grad_w', 'grad_b_glu': 'grad_w', 'grad_w_out': 'grad_w', 'grad_norm_post': 'grad_w', 'delta_norm_pre': 'delta_w', 'delta_w_in': 'delta_w', 'delta_conv_w': 'delta_w', 'delta_a_log': 'delta_w', 'delta_dt_bias': 'delta_w', 'delta_head_norm': 'delta_w', 'delta_ssm_a_re': 'delta_w', 'delta_ssm_a_im': 'delta_w', 'delta_ssm_log_dt': 'delta_w', 'delta_ssm_b_re': 'delta_w', 'delta_ssm_b_im': 'delta_w', 'delta_ssm_c_re': 'delta_w', 'delta_ssm_c_im': 'delta_w', 'delta_ssm_d': 'delta_w', 'delta_w_glu': 'delta_w', 'delta_b_glu': 'delta_w', 'delta_w_out': 'delta_w', 'delta_norm_post': 'delta_w', 'new_m_norm_pre': 'new_m', 'new_m_w_in': 'new_m', 'new_m_conv_w': 'new_m', 'new_m_a_log': 'new_m', 'new_m_dt_bias': 'new_m', 'new_m_head_norm': 'new_m', 'new_m_ssm_a_re': 'new_m', 'new_m_ssm_a_im': 'new_m', 'new_m_ssm_log_dt': 'new_m', 'new_m_ssm_b_re': 'new_m', 'new_m_ssm_b_im': 'new_m', 'new_m_ssm_c_re': 'new_m', 'new_m_ssm_c_im': 'new_m', 'new_m_ssm_d': 'new_m', 'new_m_w_glu': 'new_m', 'new_m_b_glu': 'new_m', 'new_m_w_out': 'new_m', 'new_m_norm_post': 'new_m', 'new_v_norm_pre': 'new_v', 'new_v_w_in': 'new_v', 'new_v_conv_w': 'new_v', 'new_v_a_log': 'new_v', 'new_v_dt_bias': 'new_v', 'new_v_head_norm': 'new_v', 'new_v_ssm_a_re': 'new_v', 'new_v_ssm_a_im': 'new_v', 'new_v_ssm_log_dt': 'new_v', 'new_v_ssm_b_re': 'new_v', 'new_v_ssm_b_im': 'new_v', 'new_v_ssm_c_re': 'new_v', 'new_v_ssm_c_im': 'new_v', 'new_v_ssm_d': 'new_v', 'new_v_w_glu': 'new_v', 'new_v_b_glu': 'new_v', 'new_v_w_out': 'new_v', 'new_v_norm_post': 'new_v'}


def _forward(args):
    return _fwd_reference(*[args[k] for k in FWD_PARAMS])


def _output_shape():
    def fwd():
        inp = _fwd_setup_inputs(0)
        return _fwd_reference(*[inp[k] for k in FWD_PARAMS])
    out = _jax.eval_shape(fwd)
    return out.shape, out.dtype

N_MICROBATCH = 1
ADAM_LR = 0.001
ADAM_B1 = 0.9
ADAM_B2 = 0.999
ADAM_EPS = 1e-08
ADAM_WD = 0.01
ADAM_STEP = 10
PER_EXAMPLE_BATCH_AXIS = {'x': 0, 'loss_target': 0}
SHARED_INPUTS = []
_WEIGHT_DTYPES = {'norm_pre': _jnp.float32, 'w_in': _jnp.float32, 'conv_w': _jnp.float32, 'a_log': _jnp.float32, 'dt_bias': _jnp.float32, 'head_norm': _jnp.float32, 'ssm_a_re': _jnp.float32, 'ssm_a_im': _jnp.float32, 'ssm_log_dt': _jnp.float32, 'ssm_b_re': _jnp.float32, 'ssm_b_im': _jnp.float32, 'ssm_c_re': _jnp.float32, 'ssm_c_im': _jnp.float32, 'ssm_d': _jnp.float32, 'w_glu': _jnp.float32, 'b_glu': _jnp.float32, 'w_out': _jnp.float32, 'norm_post': _jnp.float32}
MOMENT_SCALE = {'norm_pre': 1.914230e+00, 'w_in': 6.458941e-01, 'conv_w': 1.170816e+00, 'a_log': 3.258374e+00, 'dt_bias': 3.195520e+00, 'head_norm': 7.519144e+00, 'ssm_a_re': 2.677945e-02, 'ssm_a_im': 3.069973e-02, 'ssm_log_dt': 1.169104e+01, 'ssm_b_re': 1.462497e-02, 'ssm_b_im': 1.597457e-02, 'ssm_c_re': 2.927671e-02, 'ssm_c_im': 2.971829e-02, 'ssm_d': 8.121209e-01, 'w_glu': 1.233234e-01, 'b_glu': 3.284618e-01, 'w_out': 3.196557e+00, 'norm_post': 3.225332e+01}


def _to_microbatches(a, axis):
    t = _jnp.moveaxis(a, axis, 0)
    t = t.reshape((N_MICROBATCH, t.shape[0] // N_MICROBATCH) + t.shape[1:])
    return _jnp.moveaxis(t, 1, axis + 1)


def setup_inputs(seed: int = 0) -> dict:
    inp = _fwd_setup_inputs(seed)
    key = _jax.random.fold_in(_jax.random.key(seed), 7919)
    shape, _ = _output_shape()
    out = dict(inp)
    out["loss_target"] = _jax.random.normal(_jax.random.fold_in(key, 0), shape, _jnp.float32)
    for i, name in enumerate(TWIN_WEIGHTS):
        w = inp[name].astype(_jnp.float32)
        if MOMENT_SCALE is None:
            s = _jnp.sqrt(_jnp.mean(_jnp.square(w)) + 1e-30)
        else:
            s = MOMENT_SCALE[name]
        km, kv = _jax.random.split(_jax.random.fold_in(key, i + 1))
        out[name] = w
        out["m_" + name] = s * _jax.random.normal(km, w.shape, _jnp.float32)
        out["v_" + name] = (s * s) * _jax.random.uniform(kv, w.shape, _jnp.float32, 0.5, 1.5)
    if N_MICROBATCH > 1:
        for name, axis in PER_EXAMPLE_BATCH_AXIS.items():
            out[name] = _to_microbatches(out[name], axis)
    return {'x': out['x'], 'norm_pre': out['norm_pre'], 'w_in': out['w_in'], 'conv_w': out['conv_w'], 'a_log': out['a_log'], 'dt_bias': out['dt_bias'], 'head_norm': out['head_norm'], 'ssm_a_re': out['ssm_a_re'], 'ssm_a_im': out['ssm_a_im'], 'ssm_log_dt': out['ssm_log_dt'], 'ssm_b_re': out['ssm_b_re'], 'ssm_b_im': out['ssm_b_im'], 'ssm_c_re': out['ssm_c_re'], 'ssm_c_im': out['ssm_c_im'], 'ssm_d': out['ssm_d'], 'w_glu': out['w_glu'], 'b_glu': out['b_glu'], 'w_out': out['w_out'], 'norm_post': out['norm_post'], 'loss_target': out['loss_target'], 'm_norm_pre': out['m_norm_pre'], 'm_w_in': out['m_w_in'], 'm_conv_w': out['m_conv_w'], 'm_a_log': out['m_a_log'], 'm_dt_bias': out['m_dt_bias'], 'm_head_norm': out['m_head_norm'], 'm_ssm_a_re': out['m_ssm_a_re'], 'm_ssm_a_im': out['m_ssm_a_im'], 'm_ssm_log_dt': out['m_ssm_log_dt'], 'm_ssm_b_re': out['m_ssm_b_re'], 'm_ssm_b_im': out['m_ssm_b_im'], 'm_ssm_c_re': out['m_ssm_c_re'], 'm_ssm_c_im': out['m_ssm_c_im'], 'm_ssm_d': out['m_ssm_d'], 'm_w_glu': out['m_w_glu'], 'm_b_glu': out['m_b_glu'], 'm_w_out': out['m_w_out'], 'm_norm_post': out['m_norm_post'], 'v_norm_pre': out['v_norm_pre'], 'v_w_in': out['v_w_in'], 'v_conv_w': out['v_conv_w'], 'v_a_log': out['v_a_log'], 'v_dt_bias': out['v_dt_bias'], 'v_head_norm': out['v_head_norm'], 'v_ssm_a_re': out['v_ssm_a_re'], 'v_ssm_a_im': out['v_ssm_a_im'], 'v_ssm_log_dt': out['v_ssm_log_dt'], 'v_ssm_b_re': out['v_ssm_b_re'], 'v_ssm_b_im': out['v_ssm_b_im'], 'v_ssm_c_re': out['v_ssm_c_re'], 'v_ssm_c_im': out['v_ssm_c_im'], 'v_ssm_d': out['v_ssm_d'], 'v_w_glu': out['v_w_glu'], 'v_b_glu': out['v_b_glu'], 'v_w_out': out['v_w_out'], 'v_norm_post': out['v_norm_post']}


def _loss(weights, diff, rest, loss_target):
    with _jax.named_scope("forward"):
        args = {**rest, TWIN_DIFF_INPUT: diff, **{k: w.astype(_WEIGHT_DTYPES[k]) for k, w in weights.items()}}
        y = _forward(args)
    with _jax.named_scope("loss_head"):
        err = _jnp.square(y.astype(_jnp.float32) - loss_target)
        return 0.5 * _jnp.sum(_jnp.mean(err, axis=-1)) if err.ndim else 0.5 * err


def _adamw(w, g, m, v):
    m = ADAM_B1 * m + (1.0 - ADAM_B1) * g
    v = ADAM_B2 * v + (1.0 - ADAM_B2) * _jnp.square(g)
    m_hat = m / (1.0 - ADAM_B1 ** ADAM_STEP)
    v_hat = v / (1.0 - ADAM_B2 ** ADAM_STEP)
    delta = -ADAM_LR * (m_hat / (_jnp.sqrt(v_hat) + ADAM_EPS) + ADAM_WD * w)
    return delta, m, v


def reference(x, norm_pre, w_in, conv_w, a_log, dt_bias, head_norm, ssm_a_re, ssm_a_im, ssm_log_dt, ssm_b_re, ssm_b_im, ssm_c_re, ssm_c_im, ssm_d, w_glu, b_glu, w_out, norm_post, loss_target, m_norm_pre, m_w_in, m_conv_w, m_a_log, m_dt_bias, m_head_norm, m_ssm_a_re, m_ssm_a_im, m_ssm_log_dt, m_ssm_b_re, m_ssm_b_im, m_ssm_c_re, m_ssm_c_im, m_ssm_d, m_w_glu, m_b_glu, m_w_out, m_norm_post, v_norm_pre, v_w_in, v_conv_w, v_a_log, v_dt_bias, v_head_norm, v_ssm_a_re, v_ssm_a_im, v_ssm_log_dt, v_ssm_b_re, v_ssm_b_im, v_ssm_c_re, v_ssm_c_im, v_ssm_d, v_w_glu, v_b_glu, v_w_out, v_norm_post):
    given = dict(x=x, norm_pre=norm_pre, w_in=w_in, conv_w=conv_w, a_log=a_log, dt_bias=dt_bias, head_norm=head_norm, ssm_a_re=ssm_a_re, ssm_a_im=ssm_a_im, ssm_log_dt=ssm_log_dt, ssm_b_re=ssm_b_re, ssm_b_im=ssm_b_im, ssm_c_re=ssm_c_re, ssm_c_im=ssm_c_im, ssm_d=ssm_d, w_glu=w_glu, b_glu=b_glu, w_out=w_out, norm_post=norm_post, loss_target=loss_target, m_norm_pre=m_norm_pre, m_w_in=m_w_in, m_conv_w=m_conv_w, m_a_log=m_a_log, m_dt_bias=m_dt_bias, m_head_norm=m_head_norm, m_ssm_a_re=m_ssm_a_re, m_ssm_a_im=m_ssm_a_im, m_ssm_log_dt=m_ssm_log_dt, m_ssm_b_re=m_ssm_b_re, m_ssm_b_im=m_ssm_b_im, m_ssm_c_re=m_ssm_c_re, m_ssm_c_im=m_ssm_c_im, m_ssm_d=m_ssm_d, m_w_glu=m_w_glu, m_b_glu=m_b_glu, m_w_out=m_w_out, m_norm_post=m_norm_post, v_norm_pre=v_norm_pre, v_w_in=v_w_in, v_conv_w=v_conv_w, v_a_log=v_a_log, v_dt_bias=v_dt_bias, v_head_norm=v_head_norm, v_ssm_a_re=v_ssm_a_re, v_ssm_a_im=v_ssm_a_im, v_ssm_log_dt=v_ssm_log_dt, v_ssm_b_re=v_ssm_b_re, v_ssm_b_im=v_ssm_b_im, v_ssm_c_re=v_ssm_c_re, v_ssm_c_im=v_ssm_c_im, v_ssm_d=v_ssm_d, v_w_glu=v_w_glu, v_b_glu=v_b_glu, v_w_out=v_w_out, v_norm_post=v_norm_post)
    weights = {n: given[n] for n in TWIN_WEIGHTS}
    shared = {n: given[n] for n in SHARED_INPUTS}
    per_example = {n: given[n] for n in ['x']}
    grad_fn = _jax.value_and_grad(_loss, argnums=(0, 1))

    def one_microbatch(ex, loss_target):
        ex = dict(ex)
        diff = ex.pop(TWIN_DIFF_INPUT)
        return grad_fn(weights, diff, {**shared, **ex}, loss_target)

    if N_MICROBATCH == 1:
        loss, (grad_w, grad_x) = one_microbatch(per_example, given["loss_target"])
    else:
        def body(carry, xs):
            loss_sum, grad_sum = carry
            l_k, (gw_k, gx_k) = one_microbatch(xs[0], xs[1])
            with _jax.named_scope("update"):
                return (loss_sum + l_k, _jax.tree.map(_jnp.add, grad_sum, gw_k)), gx_k

        init = (_jnp.zeros((), _jnp.float32), _jax.tree.map(_jnp.zeros_like, weights))
        (loss, grad_w), grad_x = _jax.lax.scan(body, init, (per_example, given["loss_target"]))
    with _jax.named_scope("update"):
        delta_w, new_m, new_v = {}, {}, {}
        for n in TWIN_WEIGHTS:
            delta_w[n], new_m[n], new_v[n] = _adamw(weights[n], grad_w[n], given["m_" + n], given["v_" + n])
    return (loss, grad_x, *[grad_w[n] for n in TWIN_WEIGHTS], *[delta_w[n] for n in TWIN_WEIGHTS],
            *[new_m[n] for n in TWIN_WEIGHTS], *[new_v[n] for n in TWIN_WEIGHTS])
```

```python
import functools
import math

import jax
import jax.numpy as jnp
from jax import lax
from jax.experimental import pallas as pl
from jax.experimental.pallas import tpu as pltpu

F32 = jnp.float32
BF16 = jnp.bfloat16
HI = lax.Precision.HIGHEST

D = 1024
NH = 8
DH = 128
CH = 128
NG = 64
GS = 16
NS = 64
GPB = 8
NCB = NG // GPB
SW = GPB * NS
NCOL = 8320
BD0 = 8192
EPS = 1e-6
DEPTH = 4
NCHIP = 4
NDEV = 8
VMEM_LIMIT = 56 * 1024 * 1024

ADAM_LR = 0.001
ADAM_B1 = 0.9
ADAM_B2 = 0.999
ADAM_EPS = 1e-08
ADAM_WD = 0.01
ADAM_STEP = 10


def _cparams(sem=None):
    return pltpu.CompilerParams(dimension_semantics=sem, vmem_limit_bytes=VMEM_LIMIT)


def _full(shape):
    nd = len(shape)
    return pl.BlockSpec(shape, lambda *_: (0,) * nd)


def _rms(x, gain):
    ms = jnp.mean(x * x, axis=-1, keepdims=True)
    return x * lax.rsqrt(ms + EPS) * gain


def _sigmoid(x):
    return 1.0 / (1.0 + jnp.exp(-x))


def _silu(x):
    return x * _sigmoid(x)


def _softplus(x):
    return jnp.maximum(x, 0.0) + jnp.log(1.0 + jnp.exp(-jnp.abs(x)))


def _gelu(x):
    return 0.5 * x * (1.0 + jnp.tanh(math.sqrt(2.0 / math.pi) * (x + 0.044715 * (x * x * x))))


def _dot_bf16(a, b, dims):
    return lax.dot_general(a.astype(BF16), b.astype(BF16), (dims, ((), ())), preferred_element_type=F32)


def _mm_nt(a, b):
    return _dot_bf16(a, b, ((1,), (1,)))


def _mm_tn(a, b):
    return _dot_bf16(a, b, ((0,), (0,)))


@jax.custom_vjp
def _mm(a, b):
    return _dot_bf16(a, b, ((1,), (0,)))


def _mm_fwd(a, b):
    return _dot_bf16(a, b, ((1,), (0,))), (a, b)


def _mm_bwd(res, ct):
    a, b = res
    return _mm_nt(ct, b).astype(a.dtype), _mm_tn(a, ct).astype(b.dtype)


_mm.defvjp(_mm_fwd, _mm_bwd)


def _hmm(a, b):
    return jnp.dot(a, b, precision=HI, preferred_element_type=F32)


def _hmm_nt(a, b):
    return lax.dot_general(a, b, (((1,), (1,)), ((), ())), precision=HI, preferred_element_type=F32)


def _hmm_tn(a, b):
    return lax.dot_general(a, b, (((0,), (0,)), ((), ())), precision=HI, preferred_element_type=F32)


def _rows(shape):
    return lax.broadcasted_iota(jnp.int32, shape, 0)


def _cols(shape):
    return lax.broadcasted_iota(jnp.int32, shape, 1)


def _sd(x, s):
    return jnp.where(_rows(x.shape) >= s, pltpu.roll(x, s, axis=0), 0.0)


def _su(x, s):
    n = x.shape[0]
    return jnp.where(_rows(x.shape) < n - s, pltpu.roll(x, n - s, axis=0), 0.0)


@functools.partial(jax.custom_vjp, nondiff_argnums=(1,))
def _shift_down(x, s):
    return _sd(x, s)


def _shift_down_fwd(x, s):
    return _sd(x, s), None


def _shift_down_bwd(s, _, g):
    return (_su(g, s),)


_shift_down.defvjp(_shift_down_fwd, _shift_down_bwd)


def _last_row(x):
    n = x.shape[0]
    return jnp.sum(jnp.where(_rows(x.shape) == n - 1, x, 0.0), axis=0, keepdims=True)


def _prep_fn(p, w0, w1, w2, w3, qk):
    acc = w3 * p + w2 * _shift_down(p, 1) + w1 * _shift_down(p, 2) + w0 * _shift_down(p, 3)
    a = _silu(acc)
    nrm = lax.rsqrt(jnp.sum(a * a, axis=-1, keepdims=True) + EPS)
    return a * (nrm * qk + (1.0 - qk))


def _gates_fn(bd, av, bv):
    tm = bd.shape[0]
    beta_all = _sigmoid(bd)
    g_all = -jnp.exp(av) * _softplus(bd + bv)
    r, c = _rows((tm, tm)), _cols((tm, tm))
    tri = jnp.where((r // CH == c // CH) & (r >= c), 1.0, 0.0).astype(F32)
    gc_all = _hmm(tri, g_all)
    lane = _cols(bd.shape)
    outs = []
    for h in range(NH):
        b = jnp.sum(jnp.where(lane == h, beta_all, 0.0), axis=1, keepdims=True)
        outs.append(jnp.broadcast_to(b, bd.shape))
    for h in range(NH):
        g = jnp.sum(jnp.where(lane == NH + h, gc_all, 0.0), axis=1, keepdims=True)
        outs.append(jnp.broadcast_to(g, bd.shape))
    return tuple(outs)


def _unit_lower_inv(l_mat):
    n = l_mat.shape[0]
    eye = jnp.where(_rows((n, n)) == _cols((n, n)), 1.0, 0.0).astype(F32)
    p = -l_mat
    r = eye + p
    k = 1
    while 2 * k < n:
        p = _hmm(p, p)
        r = r + _hmm(r, p)
        k *= 2
    return r


def _delta_chunk(q, k, v, bb, gcb, state):
    qs = q * (DH ** -0.5)
    kb = k * bb
    eg = jnp.exp(gcb)
    ii, jj = _rows((CH, CH)), _cols((CH, CH))
    decay = jnp.exp(jnp.where(ii >= jj, gcb - gcb.T, -1e30))
    l_mat = jnp.where(ii > jj, _hmm_nt(kb, k) * decay, 0.0)
    t_inv = _unit_lower_inv(l_mat)
    u = _hmm(t_inv, v * bb)
    w = _hmm(t_inv, kb * eg)
    a_qk = _hmm_nt(qs, k) * decay
    g_last = _last_row(gcb)
    k_dec = k * jnp.exp(g_last - gcb)
    v_new = u - _hmm(w, state)
    o = _hmm(qs * eg, state) + _hmm(a_qk, v_new)
    new_state = state * jnp.exp(g_last) + _hmm_tn(k_dec, v_new)
    return o, new_state


def _s5_tile(u, cr, ci, lr, li, bbr, bbi, ccr, cci):
    xr = _mm(u, bbr)
    xi = _mm(u, bbi)
    first = _rows(xr.shape) == 0
    hr = xr + jnp.where(first, lr * cr - li * ci, 0.0)
    hi = xi + jnp.where(first, lr * ci + li * cr, 0.0)
    pr, pi = lr, li
    s = 1
    while s < u.shape[0]:
        sr, si = _shift_down(hr, s), _shift_down(hi, s)
        hr, hi = hr + pr * sr - pi * si, hi + pr * si + pi * sr
        pr, pi = pr * pr - pi * pi, 2.0 * pr * pi
        s *= 2
    y = _mm(hr, ccr) - _mm(hi, cci)
    return y, _last_row(hr), _last_row(hi)


def _s5_params_fn(ar, ai, ldt, br2, bi2):
    dt = jnp.exp(ldt)
    mag = jnp.exp(ar * dt)
    lr, li = mag * jnp.cos(ai * dt), mag * jnp.sin(ai * dt)
    den = ar * ar + ai * ai
    fr = ((lr - 1.0) * ar + li * ai) / den
    fi = (li * ar - (lr - 1.0) * ai) / den
    expand = jnp.where(_cols((NS, NS * GS)) // GS == _rows((NS, NS * GS)), 1.0, 0.0).astype(F32)
    fr2, fi2 = _hmm(fr, expand), _hmm(fi, expand)
    return lr, li, fr2 * br2 - fi2 * bi2, fr2 * bi2 + fi2 * br2


def _head_norm(o, hn):
    parts = []
    for h in range(NH):
        oh = o[:, h * DH:(h + 1) * DH]
        parts.append(oh * lax.rsqrt(jnp.mean(oh * oh, axis=-1, keepdims=True) + EPS) * hn)
    return jnp.concatenate(parts, axis=1)


def _mix_pre(s5y, u, dvec):
    return _gelu(s5y + dvec * u)


def _mix_mid(o, za, y0, gl, zb, ra, rb, hn):
    ya = _head_norm(o, hn) * _silu(za)
    yb = y0 * _sigmoid(gl) * _silu(zb)
    return _sigmoid(ra) * ya + _sigmoid(rb) * yb


def _mix_post(x, out, npost):
    return x + _rms(out, npost)


def _tile(t, want):
    return min(t, want)


def _row_tile(rows, want):
    return max(r for r in range(16, want + 1, 16) if rows % r == 0)


def _inproj_fwd(x, gain, wcat):
    t = x.shape[0]
    tm, tn = _tile(t, 512), 640

    def body(x_ref, g_ref, w_ref, o_ref, h_ref):
        @pl.when(pl.program_id(1) == 0)
        def _():
            h_ref[...] = _rms(x_ref[...], g_ref[...]).astype(h_ref.dtype)
        o_ref[...] = _dot_bf16(h_ref[...], w_ref[...], ((1,), (0,)))

    return pl.pallas_call(
        body, name="inproj_fwd", grid=(t // tm, NCOL // tn),
        in_specs=[pl.BlockSpec((tm, D), lambda i, j: (i, 0)), _full((1, D)),
                  pl.BlockSpec((D, tn), lambda i, j: (0, j))],
        out_specs=[pl.BlockSpec((tm, tn), lambda i, j: (i, j)), pl.BlockSpec((tm, D), lambda i, j: (i, 0))],
        out_shape=[jax.ShapeDtypeStruct((t, NCOL), F32), jax.ShapeDtypeStruct((t, D), wcat.dtype)],
        compiler_params=_cparams(("parallel", "arbitrary")),
    )(x, gain, wcat)


def _inproj_bwd_dx(dproj, wcat, x, gain, dxres):
    t = x.shape[0]
    tm, tk = _tile(t, 512), 640
    nk = NCOL // tk

    def body(dp_ref, w_ref, x_ref, g_ref, r_ref, dx_ref, dg_ref, acc_ref):
        i, k = pl.program_id(0), pl.program_id(1)

        @pl.when(k == 0)
        def _():
            acc_ref[...] = jnp.zeros_like(acc_ref)

        acc_ref[...] += _mm_nt(dp_ref[...], w_ref[...])

        @pl.when(k == nk - 1)
        def _():
            _, vjp = jax.vjp(_rms, x_ref[...], g_ref[...])
            dx, dg = vjp(acc_ref[...])
            dx_ref[...] = r_ref[...] + dx

            @pl.when(i == 0)
            def _():
                dg_ref[...] = dg

            @pl.when(i > 0)
            def _():
                dg_ref[...] += dg

    return pl.pallas_call(
        body, name="inproj_bwd_dx", grid=(t // tm, nk),
        in_specs=[pl.BlockSpec((tm, tk), lambda i, k: (i, k)), pl.BlockSpec((D, tk), lambda i, k: (0, k)),
                  pl.BlockSpec((tm, D), lambda i, k: (i, 0)), _full((1, D)),
                  pl.BlockSpec((tm, D), lambda i, k: (i, 0))],
        out_specs=[pl.BlockSpec((tm, D), lambda i, k: (i, 0)), _full((1, D))],
        out_shape=[jax.ShapeDtypeStruct((t, D), F32), jax.ShapeDtypeStruct((1, D), F32)],
        scratch_shapes=[pltpu.VMEM((tm, D), F32)],
        compiler_params=_cparams(("arbitrary", "arbitrary")),
    )(dproj, wcat, x, gain, dxres)


def _inproj_bwd_dw(h, dproj):
    t = h.shape[0]
    tm, tn = _tile(t, 512), 640

    def body(h_ref, dp_ref, o_ref):
        @pl.when(pl.program_id(1) == 0)
        def _():
            o_ref[...] = jnp.zeros_like(o_ref)

        o_ref[...] += _mm_tn(h_ref[...], dp_ref[...])

    return pl.pallas_call(
        body, name="inproj_bwd_dw", grid=(NCOL // tn, t // tm),
        in_specs=[pl.BlockSpec((tm, D), lambda j, i: (i, 0)), pl.BlockSpec((tm, tn), lambda j, i: (i, j))],
        out_specs=pl.BlockSpec((D, tn), lambda j, i: (0, j)),
        out_shape=jax.ShapeDtypeStruct((D, NCOL), F32),
        compiler_params=_cparams(("parallel", "arbitrary")),
    )(h, dproj)


def _prep_fwd(proj, cw):
    t = proj.shape[0]

    def body(p_ref, w_ref, o_ref):
        qk = (pl.program_id(0) < 2 * NH).astype(F32)
        o_ref[...] = _prep_fn(p_ref[...], w_ref[0:1, :], w_ref[1:2, :], w_ref[2:3, :], w_ref[3:4, :], qk)

    return pl.pallas_call(
        body, name="prep_fwd", grid=(3 * NH,),
        in_specs=[pl.BlockSpec((t, DH), lambda c: (0, c)), pl.BlockSpec((4, DH), lambda c: (0, c))],
        out_specs=pl.BlockSpec((None, t, DH), lambda c: (c, 0, 0)),
        out_shape=jax.ShapeDtypeStruct((3 * NH, t, DH), F32),
        compiler_params=_cparams(("parallel",)),
    )(proj, cw)


def _prep_bwd(proj, cw, dqkv):
    t = proj.shape[0]

    def body(p_ref, w_ref, d_ref, dp_ref, dw_ref):
        qk = (pl.program_id(0) < 2 * NH).astype(F32)
        _, vjp = jax.vjp(lambda p, w0, w1, w2, w3: _prep_fn(p, w0, w1, w2, w3, qk),
                         p_ref[...], w_ref[0:1, :], w_ref[1:2, :], w_ref[2:3, :], w_ref[3:4, :])
        dp, dw0, dw1, dw2, dw3 = vjp(d_ref[...])
        dp_ref[...] = dp
        dw_ref[0:1, :] = dw0
        dw_ref[1:2, :] = dw1
        dw_ref[2:3, :] = dw2
        dw_ref[3:4, :] = dw3

    return pl.pallas_call(
        body, name="prep_bwd", grid=(3 * NH,),
        in_specs=[pl.BlockSpec((t, DH), lambda c: (0, c)), pl.BlockSpec((4, DH), lambda c: (0, c)),
                  pl.BlockSpec((None, t, DH), lambda c: (c, 0, 0))],
        out_specs=[pl.BlockSpec((t, DH), lambda c: (0, c)), pl.BlockSpec((4, DH), lambda c: (0, c))],
        out_shape=[jax.ShapeDtypeStruct((t, 3 * D), F32), jax.ShapeDtypeStruct((4, 3 * D), F32)],
        compiler_params=_cparams(("parallel",)),
    )(proj, cw, dqkv)


def _gates_fwd(proj, gvec):
    t = proj.shape[0]
    tm = _tile(t, 512)

    def body(p_ref, gv_ref, b_ref, g_ref):
        outs = _gates_fn(p_ref[...], gv_ref[0:1, :], gv_ref[1:2, :])
        for h in range(NH):
            b_ref[h] = outs[h]
            g_ref[h] = outs[NH + h]

    spec = pl.BlockSpec((NH, tm, DH), lambda i: (0, i, 0))
    return pl.pallas_call(
        body, name="gates_fwd", grid=(t // tm,),
        in_specs=[pl.BlockSpec((tm, DH), lambda i: (i, BD0 // DH)), _full((8, DH))],
        out_specs=[spec, spec],
        out_shape=[jax.ShapeDtypeStruct((NH, t, DH), F32)] * 2,
        compiler_params=_cparams(("parallel",)),
    )(proj, gvec)


def _gates_bwd(proj, gvec, dbb, dgcb):
    t = proj.shape[0]
    tm = _tile(t, 512)

    def body(p_ref, gv_ref, db_ref, dg_ref, dp_ref, dgv_ref):
        _, vjp = jax.vjp(_gates_fn, p_ref[...], gv_ref[0:1, :], gv_ref[1:2, :])
        cts = tuple(db_ref[h] for h in range(NH)) + tuple(dg_ref[h] for h in range(NH))
        dp, da, db = vjp(cts)
        dp_ref[...] = dp

        @pl.when(pl.program_id(0) == 0)
        def _():
            dgv_ref[...] = jnp.zeros_like(dgv_ref)

        dgv_ref[0:1, :] += da
        dgv_ref[1:2, :] += db

    spec = pl.BlockSpec((NH, tm, DH), lambda i: (0, i, 0))
    return pl.pallas_call(
        body, name="gates_bwd", grid=(t // tm,),
        in_specs=[pl.BlockSpec((tm, DH), lambda i: (i, BD0 // DH)), _full((8, DH)), spec, spec],
        out_specs=[pl.BlockSpec((tm, DH), lambda i: (i, 0)), _full((8, DH))],
        out_shape=[jax.ShapeDtypeStruct((t, DH), F32), jax.ShapeDtypeStruct((8, DH), F32)],
        compiler_params=_cparams(("arbitrary",)),
    )(proj, gvec, dbb, dgcb)


def _delta_fwd(qkv, bb, gcb):
    t = qkv.shape[1]
    nch = t // CH

    def body(q_ref, k_ref, v_ref, b_ref, g_ref, o_ref, s_ref, st_ref):
        @pl.when(pl.program_id(1) == 0)
        def _():
            st_ref[...] = jnp.zeros_like(st_ref)

        s_ref[...] = st_ref[...]
        o, ns = _delta_chunk(q_ref[...], k_ref[...], v_ref[...], b_ref[...], g_ref[...], st_ref[...])
        o_ref[...] = o
        st_ref[...] = ns

    def blk(off):
        return pl.BlockSpec((None, CH, DH), lambda h, n: (h + off, n, 0))

    return pl.pallas_call(
        body, name="delta_fwd", grid=(NH, nch),
        in_specs=[blk(0), blk(NH), blk(2 * NH), blk(0), blk(0)],
        out_specs=[pl.BlockSpec((CH, DH), lambda h, n: (n, h)),
                   pl.BlockSpec((None, None, DH, DH), lambda h, n: (h, n, 0, 0))],
        out_shape=[jax.ShapeDtypeStruct((t, D), F32), jax.ShapeDtypeStruct((NH, nch, DH, DH), F32)],
        scratch_shapes=[pltpu.VMEM((DH, DH), F32)],
        compiler_params=_cparams(("parallel", "arbitrary")),
    )(qkv, qkv, qkv, bb, gcb)


def _delta_bwd(qkv, bb, gcb, states, do):
    t = qkv.shape[1]
    nch = t // CH

    def body(q_ref, k_ref, v_ref, b_ref, g_ref, s_ref, do_ref, dq_ref, dk_ref, dv_ref, db_ref, dg_ref, ds_ref):
        @pl.when(pl.program_id(1) == 0)
        def _():
            ds_ref[...] = jnp.zeros_like(ds_ref)

        _, vjp = jax.vjp(_delta_chunk, q_ref[...], k_ref[...], v_ref[...], b_ref[...], g_ref[...], s_ref[...])
        dq, dk, dv, db, dg, ds = vjp((do_ref[...], ds_ref[...]))
        dq_ref[...] = dq
        dk_ref[...] = dk
        dv_ref[...] = dv
        db_ref[...] = db
        dg_ref[...] = dg
        ds_ref[...] = ds

    def blk(off):
        return pl.BlockSpec((None, CH, DH), lambda h, n: (h + off, nch - 1 - n, 0))

    hb = jax.ShapeDtypeStruct((NH, t, DH), F32)
    return pl.pallas_call(
        body, name="delta_bwd", grid=(NH, nch),
        in_specs=[blk(0), blk(NH), blk(2 * NH), blk(0), blk(0),
                  pl.BlockSpec((None, None, DH, DH), lambda h, n: (h, nch - 1 - n, 0, 0)),
                  pl.BlockSpec((CH, DH), lambda h, n: (nch - 1 - n, h))],
        out_specs=[blk(0)] * 5,
        out_shape=[hb] * 5,
        scratch_shapes=[pltpu.VMEM((DH, DH), F32)],
        compiler_params=_cparams(("parallel", "arbitrary")),
    )(qkv, qkv, qkv, bb, gcb, states, do)


def _s5_params(ar, ai, ldt, br2, bi2):
    def body(ar_ref, ai_ref, ld_ref, br_ref, bi_ref, lr_ref, li_ref, bbr_ref, bbi_ref):
        lr, li, bbr, bbi = _s5_params_fn(ar_ref[...], ai_ref[...], ld_ref[...], br_ref[...], bi_ref[...])
        lr_ref[...] = lr
        li_ref[...] = li
        bbr_ref[...] = bbr
        bbi_ref[...] = bbi

    sq = pl.BlockSpec((None, NG, NS), lambda l: (l, 0, 0))
    wide = pl.BlockSpec((None, NG, NS * GS), lambda l: (l, 0, 0))
    return pl.pallas_call(
        body, name="s5_params", grid=(DEPTH,),
        in_specs=[sq, sq, pl.BlockSpec((None, NG, 1), lambda l: (l, 0, 0)), wide, wide],
        out_specs=[sq, sq, wide, wide],
        out_shape=[jax.ShapeDtypeStruct((DEPTH, NG, NS), F32)] * 2
        + [jax.ShapeDtypeStruct((DEPTH, NG, NS * GS), F32)] * 2,
        compiler_params=_cparams(("parallel",)),
    )(ar, ai, ldt, br2, bi2)


def _s5_params_bwd(ar, ai, ldt, br2, bi2, dlr, dli, dbbr, dbbi):
    def body(ar_ref, ai_ref, ld_ref, br_ref, bi_ref, a_ref, b_ref, c_ref, d_ref,
             dar_ref, dai_ref, dld_ref, dbr_ref, dbi_ref):
        _, vjp = jax.vjp(_s5_params_fn, ar_ref[...], ai_ref[...], ld_ref[...], br_ref[...], bi_ref[...])
        dar, dai, dld, dbr, dbi = vjp((a_ref[...], b_ref[...], c_ref[...], d_ref[...]))
        dar_ref[...] = dar
        dai_ref[...] = dai
        dld_ref[...] = dld
        dbr_ref[...] = dbr
        dbi_ref[...] = dbi

    sq = pl.BlockSpec((None, NG, NS), lambda l: (l, 0, 0))
    col = pl.BlockSpec((None, NG, 1), lambda l: (l, 0, 0))
    wide = pl.BlockSpec((None, NG, NS * GS), lambda l: (l, 0, 0))
    return pl.pallas_call(
        body, name="s5_params_bwd", grid=(DEPTH,),
        in_specs=[sq, sq, col, wide, wide, sq, sq, wide, wide],
        out_specs=[sq, sq, col, wide, wide],
        out_shape=[jax.ShapeDtypeStruct((DEPTH, NG, NS), F32)] * 2 + [jax.ShapeDtypeStruct((DEPTH, NG, 1), F32)]
        + [jax.ShapeDtypeStruct((DEPTH, NG, NS * GS), F32)] * 2,
        compiler_params=_cparams(("parallel",)),
    )(ar, ai, ldt, br2, bi2, dlr, dli, dbbr, dbbi)


def _s5_tile_rows(t):
    return _tile(t // 2, 256)


def _s5_fwd(proj, lam, bblk, cblk):
    t = proj.shape[0]
    r = _s5_tile_rows(t)
    nt = t // r
    u0 = 4 * D // DH

    def body(u_ref, lam_ref, b_ref, c_ref, y_ref, car_ref, st_ref):
        @pl.when(pl.program_id(1) == 0)
        def _():
            st_ref[...] = jnp.zeros_like(st_ref)

        car_ref[...] = st_ref[...]
        y, cr, ci = _s5_tile(u_ref[...], st_ref[0:1, :], st_ref[1:2, :], lam_ref[0], lam_ref[1],
                             b_ref[0], b_ref[1], c_ref[0], c_ref[1])
        y_ref[...] = y
        st_ref[0:1, :] = cr
        st_ref[1:2, :] = ci

    return pl.pallas_call(
        body, name="s5_fwd", grid=(NCB, nt),
        in_specs=[pl.BlockSpec((r, DH), lambda c, i: (i, u0 + c)),
                  pl.BlockSpec((2, 1, SW), lambda c, i: (0, 0, c)),
                  pl.BlockSpec((2, None, DH, SW), lambda c, i: (0, c, 0, 0)),
                  pl.BlockSpec((2, None, SW, DH), lambda c, i: (0, c, 0, 0))],
        out_specs=[pl.BlockSpec((r, DH), lambda c, i: (i, c)),
                   pl.BlockSpec((None, 8, SW), lambda c, i: (i, 0, c))],
        out_shape=[jax.ShapeDtypeStruct((t, D), F32), jax.ShapeDtypeStruct((nt, 8, NG * NS), F32)],
        scratch_shapes=[pltpu.VMEM((8, SW), F32)],
        compiler_params=_cparams(("parallel", "arbitrary")),
    )(proj, lam, bblk, cblk)


def _s5_bwd(proj, lam, bblk, cblk, carries, dy, du_skip):
    t = proj.shape[0]
    r = _s5_tile_rows(t)
    nt = t // r
    u0 = 4 * D // DH

    def body(u_ref, lam_ref, b_ref, c_ref, car_ref, dy_ref, dus_ref, du_ref, dlam_ref, db_ref, dc_ref, dst_ref):
        first = pl.program_id(1) == 0

        @pl.when(first)
        def _():
            dst_ref[...] = jnp.zeros_like(dst_ref)

        _, vjp = jax.vjp(_s5_tile, u_ref[...], car_ref[0:1, :], car_ref[1:2, :], lam_ref[0], lam_ref[1],
                         b_ref[0], b_ref[1], c_ref[0], c_ref[1])
        du, dcr, dci, dlr, dli, dbr, dbi, dcr2, dci2 = vjp((dy_ref[...], dst_ref[0:1, :], dst_ref[1:2, :]))
        du_ref[...] = du + dus_ref[...]
        dst_ref[0:1, :] = dcr
        dst_ref[1:2, :] = dci

        @pl.when(first)
        def _():
            dlam_ref[0] = dlr
            dlam_ref[1] = dli
            db_ref[0] = dbr
            db_ref[1] = dbi
            dc_ref[0] = dcr2
            dc_ref[1] = dci2

        @pl.when(jnp.logical_not(first))
        def _():
            dlam_ref[0] += dlr
            dlam_ref[1] += dli
            db_ref[0] += dbr
            db_ref[1] += dbi
            dc_ref[0] += dcr2
            dc_ref[1] += dci2

    return pl.pallas_call(
        body, name="s5_bwd", grid=(NCB, nt),
        in_specs=[pl.BlockSpec((r, DH), lambda c, i: (nt - 1 - i, u0 + c)),
                  pl.BlockSpec((2, 1, SW), lambda c, i: (0, 0, c)),
                  pl.BlockSpec((2, None, DH, SW), lambda c, i: (0, c, 0, 0)),
                  pl.BlockSpec((2, None, SW, DH), lambda c, i: (0, c, 0, 0)),
                  pl.BlockSpec((None, 8, SW), lambda c, i: (nt - 1 - i, 0, c)),
                  pl.BlockSpec((r, DH), lambda c, i: (nt - 1 - i, c)),
                  pl.BlockSpec((r, DH), lambda c, i: (nt - 1 - i, c))],
        out_specs=[pl.BlockSpec((r, DH), lambda c, i: (nt - 1 - i, c)),
                   pl.BlockSpec((2, 1, SW), lambda c, i: (0, 0, c)),
                   pl.BlockSpec((2, None, DH, SW), lambda c, i: (0, c, 0, 0)),
                   pl.BlockSpec((2, None, SW, DH), lambda c, i: (0, c, 0, 0))],
        out_shape=[jax.ShapeDtypeStruct((t, D), F32), jax.ShapeDtypeStruct((2, 1, NG * NS), F32),
                   jax.ShapeDtypeStruct((2, NCB, DH, SW), F32), jax.ShapeDtypeStruct((2, NCB, SW, DH), F32)],
        scratch_shapes=[pltpu.VMEM((8, SW), F32)],
        compiler_params=_cparams(("parallel", "arbitrary")),
    )(proj, lam, bblk, cblk, carries, dy, du_skip)


def _proj_spec(tm, col):
    return pl.BlockSpec((tm, D), lambda i: (i, col))


def _mix_fwd(proj, o, s5y, x, hn, dvec, wglu, bglu, wout, npost):
    t = x.shape[0]
    tm = _tile(t, 256)

    def body(za_ref, u_ref, zb_ref, ra_ref, rb_ref, o_ref, y_ref, x_ref, hn_ref, d_ref, wg_ref, bg_ref, wo_ref,
             np_ref, xn_ref):
        y0 = _mix_pre(y_ref[...], u_ref[...], d_ref[...])
        gl = _mm(y0, wg_ref[...]) + bg_ref[...]
        m = _mix_mid(o_ref[...], za_ref[...], y0, gl, zb_ref[...], ra_ref[...], rb_ref[...], hn_ref[...])
        out = _mm(m, wo_ref[...])
        xn_ref[...] = _mix_post(x_ref[...], out, np_ref[...])

    act = pl.BlockSpec((tm, D), lambda i: (i, 0))
    return pl.pallas_call(
        body, name="mix_fwd", grid=(t // tm,),
        in_specs=[_proj_spec(tm, 3), _proj_spec(tm, 4), _proj_spec(tm, 5), _proj_spec(tm, 6), _proj_spec(tm, 7),
                  act, act, act, _full((1, DH)), _full((1, D)), _full((D, D)), _full((1, D)), _full((D, D)),
                  _full((1, D))],
        out_specs=act,
        out_shape=jax.ShapeDtypeStruct((t, D), F32),
        compiler_params=_cparams(("parallel",)),
    )(proj, proj, proj, proj, proj, o, s5y, x, hn, dvec, wglu, bglu, wout, npost)


def _mix_bwd(proj, o, s5y, x, hn, dvec, wglu, bglu, wout, npost, dxn):
    t = x.shape[0]
    tm = _tile(t, 128)

    def body(za_ref, u_ref, zb_ref, ra_ref, rb_ref, o_ref, y_ref, x_ref, hn_ref, d_ref, wg_ref, bg_ref, wo_ref,
             np_ref, dxn_ref,
             dza_ref, du_ref, dzb_ref, dra_ref, drb_ref, do_ref, dy_ref, dx_ref,
             dwg_ref, dwo_ref, dvecs_ref, dhn_ref):
        y0, vjp_pre = jax.vjp(_mix_pre, y_ref[...], u_ref[...], d_ref[...])
        gl = _mm(y0, wg_ref[...]) + bg_ref[...]
        m, vjp_mid = jax.vjp(_mix_mid, o_ref[...], za_ref[...], y0, gl, zb_ref[...], ra_ref[...], rb_ref[...],
                             hn_ref[...])
        out = _mm(m, wo_ref[...])
        _, vjp_post = jax.vjp(_mix_post, x_ref[...], out, np_ref[...])
        dx, dout, dnp = vjp_post(dxn_ref[...])
        dm = _mm_nt(dout, wo_ref[...])
        dwo = _mm_tn(m, dout)
        do, dza, dy0, dgl, dzb, dra, drb, dhn = vjp_mid(dm)
        dwg = _mm_tn(y0, dgl)
        dbg = jnp.sum(dgl, axis=0, keepdims=True)
        dy0 = dy0 + _mm_nt(dgl, wg_ref[...])
        dy, du, dd = vjp_pre(dy0)
        dza_ref[...] = dza
        du_ref[...] = du
        dzb_ref[...] = dzb
        dra_ref[...] = dra
        drb_ref[...] = drb
        do_ref[...] = do
        dy_ref[...] = dy
        dx_ref[...] = dx
        first = pl.program_id(0) == 0

        @pl.when(first)
        def _():
            dwg_ref[...] = dwg
            dwo_ref[...] = dwo
            dvecs_ref[...] = jnp.zeros_like(dvecs_ref)
            dhn_ref[...] = jnp.zeros_like(dhn_ref)

        @pl.when(jnp.logical_not(first))
        def _():
            dwg_ref[...] += dwg
            dwo_ref[...] += dwo

        dvecs_ref[0:1, :] += dd
        dvecs_ref[1:2, :] += dbg
        dvecs_ref[2:3, :] += dnp
        dhn_ref[0:1, :] += dhn

    act = pl.BlockSpec((tm, D), lambda i: (i, 0))
    a = jax.ShapeDtypeStruct((t, D), F32)
    w = jax.ShapeDtypeStruct((D, D), F32)
    return pl.pallas_call(
        body, name="mix_bwd", grid=(t // tm,),
        in_specs=[_proj_spec(tm, 3), _proj_spec(tm, 4), _proj_spec(tm, 5), _proj_spec(tm, 6), _proj_spec(tm, 7),
                  act, act, act, _full((1, DH)), _full((1, D)), _full((D, D)), _full((1, D)), _full((D, D)),
                  _full((1, D)), act],
        out_specs=[act] * 8 + [_full((D, D)), _full((D, D)), _full((8, D)), _full((8, DH))],
        out_shape=[a] * 8 + [w, w, jax.ShapeDtypeStruct((8, D), F32), jax.ShapeDtypeStruct((8, DH), F32)],
        compiler_params=_cparams(("arbitrary",)),
    )(proj, proj, proj, proj, proj, o, s5y, x, hn, dvec, wglu, bglu, wout, npost, dxn)


def _loss_grad(y, target):
    t = y.shape[0]
    tm = _tile(t, 512)

    def body(y_ref, t_ref, dy_ref, l_ref):
        err = y_ref[...] - t_ref[...]
        dy_ref[...] = err * (1.0 / D)
        part = jnp.sum(jnp.sum(err * err, axis=1, keepdims=True), axis=0, keepdims=True) * (0.5 / D)
        part = jnp.broadcast_to(part, (8, DH))

        @pl.when(pl.program_id(0) == 0)
        def _():
            l_ref[...] = part

        @pl.when(pl.program_id(0) > 0)
        def _():
            l_ref[...] += part

    act = pl.BlockSpec((tm, D), lambda i: (i, 0))
    return pl.pallas_call(
        body, name="loss_grad", grid=(t // tm,),
        in_specs=[act, act], out_specs=[act, _full((8, DH))],
        out_shape=[jax.ShapeDtypeStruct((t, D), F32), jax.ShapeDtypeStruct((8, DH), F32)],
        compiler_params=_cparams(("arbitrary",)),
    )(y, target)


def _flips(rel):
    x, y, c = lax.axis_index("x"), lax.axis_index("y"), lax.axis_index("c")
    fx, fy, fc = rel
    return (x ^ fx if fx else x, y ^ fy if fy else y, c ^ fc if fc else c)


CHIP_RELS = ((1, 0, 0), (0, 1, 0), (1, 1, 0))
ALL_RELS = tuple((fx, fy, fc) for fx in (0, 1) for fy in (0, 1) for fc in (0, 1) if (fx, fy, fc) != (0, 0, 0))


def _slot_of(pos, by_chip):
    px, py, pc = pos
    return 2 * px + py if by_chip else 4 * px + 2 * py + pc


def _exchange(src, rels, by_chip, scatter, name):
    nslot = NCHIP if by_chip else NDEV
    shape = src.shape[-2:]
    nrel = len(rels)

    def body(src_ref, dst_ref, send_sems, recv_sems, local_sem):
        me = _flips((0, 0, 0))
        my_slot = _slot_of(me, by_chip)
        mine = pltpu.make_async_copy(src_ref.at[my_slot] if scatter else src_ref, dst_ref.at[my_slot], local_sem)
        mine.start()
        copies = []
        for k, rel in enumerate(rels):
            peer = _flips(rel)
            part = src_ref.at[_slot_of(peer, by_chip)] if scatter else src_ref
            copies.append(pltpu.make_async_remote_copy(
                src_ref=part, dst_ref=dst_ref.at[my_slot], send_sem=send_sems.at[k], recv_sem=recv_sems.at[k],
                device_id=peer, device_id_type=pl.DeviceIdType.MESH))
        for cp in copies:
            cp.start()
        for k, rel in enumerate(rels):
            peer = _flips(rel)
            part = src_ref.at[0] if scatter else src_ref
            pltpu.make_async_remote_copy(
                src_ref=part, dst_ref=dst_ref.at[_slot_of(peer, by_chip)], send_sem=send_sems.at[k],
                recv_sem=recv_sems.at[k], device_id=peer, device_id_type=pl.DeviceIdType.MESH).wait_recv()
        for cp in copies:
            cp.wait_send()
        mine.wait()

    return pl.pallas_call(
        body, name=name,
        in_specs=[pl.BlockSpec(memory_space=pl.ANY)],
        out_specs=pl.BlockSpec(memory_space=pl.ANY),
        out_shape=jax.ShapeDtypeStruct((nslot,) + shape, src.dtype),
        scratch_shapes=[pltpu.SemaphoreType.DMA((nrel,)), pltpu.SemaphoreType.DMA((nrel,)), pltpu.SemaphoreType.DMA],
    )(src)


def _sibling_swap(src, name):
    def body(src_ref, dst_ref, send_sem, recv_sem):
        peer = _flips((0, 0, 1))
        cp = pltpu.make_async_remote_copy(src_ref=src_ref, dst_ref=dst_ref, send_sem=send_sem, recv_sem=recv_sem,
                                          device_id=peer, device_id_type=pl.DeviceIdType.MESH)
        cp.start()
        cp.wait()

    return pl.pallas_call(
        body, name=name,
        in_specs=[pl.BlockSpec(memory_space=pl.ANY)],
        out_specs=pl.BlockSpec(memory_space=pl.ANY),
        out_shape=jax.ShapeDtypeStruct(src.shape, src.dtype),
        scratch_shapes=[pltpu.SemaphoreType.DMA, pltpu.SemaphoreType.DMA],
    )(src)


def _sum_slots(parts, name):
    ns, rows, cols = parts.shape
    tr = _row_tile(rows, 256)

    def body(p_ref, o_ref):
        acc = p_ref[0].astype(F32)
        for s in range(1, ns):
            acc = acc + p_ref[s].astype(F32)
        o_ref[...] = acc

    return pl.pallas_call(
        body, name=name, grid=(rows // tr,),
        in_specs=[pl.BlockSpec((ns, tr, cols), lambda i: (0, i, 0))],
        out_specs=pl.BlockSpec((tr, cols), lambda i: (i, 0)),
        out_shape=jax.ShapeDtypeStruct((rows, cols), F32),
        compiler_params=_cparams(("parallel",)),
    )(parts)


def _adamw(w, g_parts, m, v, name):
    ns, rows, cols = g_parts.shape
    tr = _row_tile(rows, 256)
    c1 = 1.0 / (1.0 - ADAM_B1 ** ADAM_STEP)
    c2 = 1.0 / (1.0 - ADAM_B2 ** ADAM_STEP)

    def body(w_ref, g_ref, m_ref, v_ref, go_ref, d_ref, nm_ref, nv_ref):
        g = g_ref[0]
        for s in range(1, ns):
            g = g + g_ref[s]
        nm = ADAM_B1 * m_ref[...] + (1.0 - ADAM_B1) * g
        nv = ADAM_B2 * v_ref[...] + (1.0 - ADAM_B2) * (g * g)
        d_ref[...] = -ADAM_LR * ((nm * c1) / (jnp.sqrt(nv * c2) + ADAM_EPS) + ADAM_WD * w_ref[...])
        go_ref[...] = g
        nm_ref[...] = nm
        nv_ref[...] = nv

    blk = pl.BlockSpec((tr, cols), lambda i: (i, 0))
    out = jax.ShapeDtypeStruct((rows, cols), F32)
    return pl.pallas_call(
        body, name=name, grid=(rows // tr,),
        in_specs=[blk, pl.BlockSpec((ns, tr, cols), lambda i: (0, i, 0)), blk, blk],
        out_specs=[blk] * 4, out_shape=[out] * 4,
        compiler_params=_cparams(("parallel",)),
    )(w, g_parts, m, v)


SHARD_ROWS = 10272
W_IN_ROWS = 8208
CONV_ROWS = 16
WIN_SHARD = 2052
CONV_SHARD = 768
ROW_SHARD = 256

SMALL = (("norm_pre", (DEPTH, D)), ("a_log", (DEPTH, NH)), ("dt_bias", (DEPTH, NH)), ("head_norm", (DEPTH, DH)),
         ("ssm_a_re", (DEPTH, NG, NS)), ("ssm_a_im", (DEPTH, NG, NS)), ("ssm_log_dt", (DEPTH, NG)),
         ("ssm_b_re", (DEPTH, NG, NS, GS)), ("ssm_b_im", (DEPTH, NG, NS, GS)),
         ("ssm_c_re", (DEPTH, NG, GS, NS)), ("ssm_c_im", (DEPTH, NG, GS, NS)), ("ssm_d", (DEPTH, D)),
         ("b_glu", (DEPTH, D)), ("norm_post", (DEPTH, D)))


def _pad_rows(flat, rows):
    return jnp.pad(flat, (0, rows * D - flat.shape[0])).reshape(rows, D)


def _pack_shard(w_in, conv_w, w_glu, w_out):
    return jnp.concatenate([w_in.reshape(W_IN_ROWS, D), _pad_rows(conv_w.reshape(-1), CONV_ROWS),
                            w_glu.reshape(-1, D), w_out.reshape(-1, D)], axis=0)


def _unpack_shard(flat):
    w_in = flat[:W_IN_ROWS].reshape(DEPTH, D, WIN_SHARD)
    conv = flat[W_IN_ROWS:W_IN_ROWS + CONV_ROWS].reshape(-1)[:DEPTH * 4 * CONV_SHARD].reshape(DEPTH, 4, CONV_SHARD)
    r0 = W_IN_ROWS + CONV_ROWS
    w_glu = flat[r0:r0 + DEPTH * ROW_SHARD].reshape(DEPTH, ROW_SHARD, D)
    w_out = flat[r0 + DEPTH * ROW_SHARD:].reshape(DEPTH, ROW_SHARD, D)
    return w_in, conv, w_glu, w_out


def _small_rows():
    return sum(-(-math.prod(s) // (8 * D)) * 8 for _, s in SMALL)


def _pack_small(vals):
    parts = []
    for (_, shape), val in zip(SMALL, vals):
        n = math.prod(shape)
        parts.append(_pad_rows(val.reshape(-1), -(-n // (8 * D)) * 8))
    return jnp.concatenate(parts, axis=0)


def _unpack_small(flat):
    outs, r0 = [], 0
    for _, shape in SMALL:
        n = math.prod(shape)
        rows = -(-n // (8 * D)) * 8
        outs.append(flat[r0:r0 + rows].reshape(-1)[:n].reshape(shape))
        r0 += rows
    return outs


def _rearrange_cols(w):
    pad = jnp.zeros(w.shape[:-1] + (NCOL - BD0 - 2 * NH,), w.dtype)
    return jnp.concatenate([w[..., :4 * D], w[..., 4 * D + 2 * NH:], w[..., 4 * D:4 * D + 2 * NH], pad], axis=-1)


def _restore_cols(w):
    return jnp.concatenate([w[..., :4 * D], w[..., BD0:BD0 + 2 * NH], w[..., 4 * D:BD0]], axis=-1)


def _block_diag_b(bb2):
    b = bb2.reshape(NCB, GPB, NS, GS)
    eye = jnp.eye(GPB, dtype=F32)
    return jnp.einsum("kgnc,gh->kgchn", b, eye).reshape(NCB, GPB * GS, SW)


def _block_diag_b_t(d):
    return jnp.einsum("kgchn,gh->kgnc", d.reshape(NCB, GPB, GS, GPB, NS), jnp.eye(GPB, dtype=F32)).reshape(NG, NS * GS)


def _block_diag_c(c):
    eye = jnp.eye(GPB, dtype=F32)
    return jnp.einsum("kgcn,gh->kgnhc", c.reshape(NCB, GPB, GS, NS), eye).reshape(NCB, SW, GPB * GS)


def _block_diag_c_t(d):
    return jnp.einsum("kgnhc,gh->kgcn", d.reshape(NCB, GPB, NS, GPB, GS), jnp.eye(GPB, dtype=F32)).reshape(NG, GS, NS)


def _local_step(x, target, wcat, conv, wglu, wout, small):
    ar, ai = small["ssm_a_re"], small["ssm_a_im"]
    ldt = small["ssm_log_dt"].reshape(DEPTH, NG, 1)
    br2 = small["ssm_b_re"].reshape(DEPTH, NG, NS * GS)
    bi2 = small["ssm_b_im"].reshape(DEPTH, NG, NS * GS)
    lr, li, bbr2, bbi2 = _s5_params(ar, ai, ldt, br2, bi2)

    def row(name, l, width):
        return small[name][l].reshape(1, width)

    saved = []
    for l in range(DEPTH):
        gvec = jnp.pad(jnp.stack([small["a_log"][l], small["dt_bias"][l]]), ((0, 6), (NH, DH - 2 * NH)))
        lam = jnp.stack([lr[l].reshape(1, NG * NS), li[l].reshape(1, NG * NS)])
        bblk = jnp.stack([_block_diag_b(bbr2[l]), _block_diag_b(bbi2[l])])
        cblk = jnp.stack([_block_diag_c(small["ssm_c_re"][l]), _block_diag_c(small["ssm_c_im"][l])])
        proj, h = _inproj_fwd(x, row("norm_pre", l, D), wcat[l])
        qkv = _prep_fwd(proj, conv[l])
        bb, gcb = _gates_fwd(proj, gvec)
        o, states = _delta_fwd(qkv, bb, gcb)
        s5y, carries = _s5_fwd(proj, lam, bblk, cblk)
        xn = _mix_fwd(proj, o, s5y, x, row("head_norm", l, DH), row("ssm_d", l, D), wglu[l], row("b_glu", l, D),
                      wout[l], row("norm_post", l, D))
        saved.append((x, proj, h, qkv, bb, gcb, o, states, s5y, carries, gvec, lam, bblk, cblk))
        x = xn

    dx, loss_part = _loss_grad(x, target)

    g = {k: [None] * DEPTH for k in ("wcat", "conv", "wglu", "wout", "norm_pre", "a_log", "dt_bias", "head_norm",
                                     "ssm_c_re", "ssm_c_im", "ssm_d", "b_glu", "norm_post", "lr", "li", "bbr", "bbi")}
    for l in reversed(range(DEPTH)):
        xl, proj, h, qkv, bb, gcb, o, states, s5y, carries, gvec, lam, bblk, cblk = saved[l]
        (dza, du_skip, dzb, dra, drb, do, ds5y, dxres, dwg, dwo, dvecs, dhn) = _mix_bwd(
            proj, o, s5y, xl, row("head_norm", l, DH), row("ssm_d", l, D), wglu[l], row("b_glu", l, D), wout[l],
            row("norm_post", l, D), dx)
        du, dlam, dbblk, dcblk = _s5_bwd(proj, lam, bblk, cblk, carries, ds5y, du_skip)
        dq, dk, dv, dbb, dgcb = _delta_bwd(qkv, bb, gcb, states, do)
        dbd, dgvec = _gates_bwd(proj, gvec, dbb, dgcb)
        dqkv = jnp.concatenate([dq, dk, dv], axis=0)
        dpre, dconv = _prep_bwd(proj, conv[l], dqkv)
        dproj = jnp.concatenate([dpre, dza, du, dzb, dra, drb, dbd], axis=1)
        dx, dgain = _inproj_bwd_dx(dproj, wcat[l], xl, row("norm_pre", l, D), dxres)
        g["wcat"][l] = _inproj_bwd_dw(h, dproj)
        g["conv"][l], g["wglu"][l], g["wout"][l] = dconv, dwg, dwo
        g["norm_pre"][l] = dgain[0]
        g["a_log"][l], g["dt_bias"][l] = dgvec[0, NH:2 * NH], dgvec[1, NH:2 * NH]
        g["head_norm"][l] = dhn[0]
        g["ssm_d"][l], g["b_glu"][l], g["norm_post"][l] = dvecs[0], dvecs[1], dvecs[2]
        g["ssm_c_re"][l], g["ssm_c_im"][l] = _block_diag_c_t(dcblk[0]), _block_diag_c_t(dcblk[1])
        g["lr"][l], g["li"][l] = dlam[0].reshape(NG, NS), dlam[1].reshape(NG, NS)
        g["bbr"][l], g["bbi"][l] = _block_diag_b_t(dbblk[0]), _block_diag_b_t(dbblk[1])
    g = {k: jnp.stack(v) for k, v in g.items()}
    dar, dai, dldt, dbr2, dbi2 = _s5_params_bwd(ar, ai, ldt, br2, bi2, g["lr"], g["li"], g["bbr"], g["bbi"])
    g["ssm_a_re"], g["ssm_a_im"], g["ssm_log_dt"] = dar, dai, dldt.reshape(DEPTH, NG)
    g["ssm_b_re"] = dbr2.reshape(DEPTH, NG, NS, GS)
    g["ssm_b_im"] = dbi2.reshape(DEPTH, NG, NS, GS)
    return loss_part[0, 0], dx, g


def kernel(x, norm_pre, w_in, conv_w, a_log, dt_bias, head_norm, ssm_a_re, ssm_a_im, ssm_log_dt, ssm_b_re, ssm_b_im, ssm_c_re, ssm_c_im, ssm_d, w_glu, b_glu, w_out, norm_post, loss_target, m_norm_pre, m_w_in, m_conv_w, m_a_log, m_dt_bias, m_head_norm, m_ssm_a_re, m_ssm_a_im, m_ssm_log_dt, m_ssm_b_re, m_ssm_b_im, m_ssm_c_re, m_ssm_c_im, m_ssm_d, m_w_glu, m_b_glu, m_w_out, m_norm_post, v_norm_pre, v_w_in, v_conv_w, v_a_log, v_dt_bias, v_head_norm, v_ssm_a_re, v_ssm_a_im, v_ssm_log_dt, v_ssm_b_re, v_ssm_b_im, v_ssm_c_re, v_ssm_c_im, v_ssm_d, v_w_glu, v_b_glu, v_w_out, v_norm_post):
    args = dict(locals())
    small = {n: args[n] for n, _ in SMALL}

    shard = _pack_shard(w_in, conv_w, w_glu, w_out)
    gathered = _exchange(shard.astype(BF16), CHIP_RELS, True, False, "gather_weights")
    conv_g = _exchange(_pad_rows(conv_w.reshape(-1), CONV_ROWS), CHIP_RELS, True, False, "gather_conv")
    pieces = [_unpack_shard(gathered[j]) for j in range(NCHIP)]
    wcat = _rearrange_cols(jnp.concatenate([p[0] for p in pieces], axis=2))
    wglu = jnp.concatenate([p[2] for p in pieces], axis=1)
    wout = jnp.concatenate([p[3] for p in pieces], axis=1)
    conv = jnp.concatenate([conv_g[j].reshape(-1)[:DEPTH * 4 * CONV_SHARD].reshape(DEPTH, 4, CONV_SHARD)
                            for j in range(NCHIP)], axis=2)

    loss_part, dx, g = _local_step(x[0], loss_target[0], wcat, conv, wglu, wout, small)
    loss = lax.psum(loss_part, ("x", "y", "c"))

    gw_in = _restore_cols(g["wcat"])
    per_chip = jnp.stack([
        _pack_shard(gw_in[:, :, j * WIN_SHARD:(j + 1) * WIN_SHARD], g["conv"][:, :, j * CONV_SHARD:(j + 1) * CONV_SHARD],
                    g["wglu"][:, j * ROW_SHARD:(j + 1) * ROW_SHARD], g["wout"][:, j * ROW_SHARD:(j + 1) * ROW_SHARD])
        for j in range(NCHIP)]).astype(BF16)
    from_chips = _exchange(per_chip, CHIP_RELS, True, True, "scatter_grads")
    core_sum = _sum_slots(from_chips, "sum_chips")
    other = _sibling_swap(core_sum, "swap_cores")
    gs, ds, nms, nvs = _adamw(shard, jnp.stack([core_sum, other]),
                              _pack_shard(m_w_in, m_conv_w, m_w_glu, m_w_out),
                              _pack_shard(v_w_in, v_conv_w, v_w_glu, v_w_out), "adamw_sharded")

    small_parts = _exchange(_pack_small([g[n] for n, _ in SMALL]), ALL_RELS, False, False, "gather_small")
    gr, dr, nmr, nvr = _adamw(_pack_small([args[n] for n, _ in SMALL]), small_parts,
                              _pack_small([args["m_" + n] for n, _ in SMALL]),
                              _pack_small([args["v_" + n] for n, _ in SMALL]), "adamw_small")

    names = ["norm_pre", "w_in", "conv_w", "a_log", "dt_bias", "head_norm", "ssm_a_re", "ssm_a_im", "ssm_log_dt",
             "ssm_b_re", "ssm_b_im", "ssm_c_re", "ssm_c_im", "ssm_d", "w_glu", "b_glu", "w_out", "norm_post"]
    outs = [loss, dx[None]]
    for flat_s, flat_r in ((gs, gr), (ds, dr), (nms, nmr), (nvs, nvr)):
        sh = dict(zip(("w_in", "conv_w", "w_glu", "w_out"), _unpack_shard(flat_s)))
        sm = dict(zip([n for n, _ in SMALL], _unpack_small(flat_r)))
        outs += [sh[n] if n in sh else sm[n] for n in names]
    return tuple(outs)
```

```python
import functools
import math

import jax
import jax.numpy as jnp
from jax import lax
from jax.experimental import pallas as pl
from jax.experimental.pallas import tpu as pltpu

F32 = jnp.float32
BF16 = jnp.bfloat16
HI = lax.Precision.HIGHEST

D = 1024
NH = 8
DH = 128
CH = 128
NG = 64
GS = 16
NS = 64
GPB = 8
NCB = NG // GPB
SW = GPB * NS
NCOL = 8320
BD0 = 8192
EPS = 1e-6
DEPTH = 4
NCHIP = 4
NDEV = 8
VMEM_LIMIT = 56 * 1024 * 1024

ADAM_LR = 0.001
ADAM_B1 = 0.9
ADAM_B2 = 0.999
ADAM_EPS = 1e-08
ADAM_WD = 0.01
ADAM_STEP = 10


def _cparams(sem=None):
    return pltpu.CompilerParams(dimension_semantics=sem, vmem_limit_bytes=VMEM_LIMIT)


def _full(shape):
    nd = len(shape)
    return pl.BlockSpec(shape, lambda *_: (0,) * nd)


def _rms(x, gain):
    ms = jnp.mean(x * x, axis=-1, keepdims=True)
    return x * lax.rsqrt(ms + EPS) * gain


def _sigmoid(x):
    return 1.0 / (1.0 + jnp.exp(-x))


def _silu(x):
    return x * _sigmoid(x)


def _softplus(x):
    return jnp.maximum(x, 0.0) + jnp.log(1.0 + jnp.exp(-jnp.abs(x)))


def _gelu(x):
    return 0.5 * x * (1.0 + jnp.tanh(math.sqrt(2.0 / math.pi) * (x + 0.044715 * (x * x * x))))


def _dot_bf16(a, b, dims):
    return lax.dot_general(a.astype(BF16), b.astype(BF16), (dims, ((), ())), preferred_element_type=F32)


def _mm_nt(a, b):
    return _dot_bf16(a, b, ((1,), (1,)))


def _mm_tn(a, b):
    return _dot_bf16(a, b, ((0,), (0,)))


@jax.custom_vjp
def _mm(a, b):
    return _dot_bf16(a, b, ((1,), (0,)))


def _mm_fwd(a, b):
    return _dot_bf16(a, b, ((1,), (0,))), (a, b)


def _mm_bwd(res, ct):
    a, b = res
    return _mm_nt(ct, b).astype(a.dtype), _mm_tn(a, ct).astype(b.dtype)


_mm.defvjp(_mm_fwd, _mm_bwd)


@jax.custom_vjp
def _mm_nt_d(a, b):
    return _mm_nt(a, b)


def _mm_nt_d_bwd(res, ct):
    a, b = res
    return _dot_bf16(ct, b, ((1,), (0,))), _mm_tn(ct, a)


_mm_nt_d.defvjp(lambda a, b: (_mm_nt(a, b), (a, b)), _mm_nt_d_bwd)


@jax.custom_vjp
def _mm_tn_d(a, b):
    return _mm_tn(a, b)


def _mm_tn_d_bwd(res, ct):
    a, b = res
    return _mm_nt(b, ct), _dot_bf16(a, ct, ((1,), (0,)))


_mm_tn_d.defvjp(lambda a, b: (_mm_tn(a, b), (a, b)), _mm_tn_d_bwd)


def _split_bf16(a):
    hi = a.astype(BF16)
    return hi, (a - hi.astype(F32)).astype(BF16)


def _dot3(a, b, dims):
    ah, al = _split_bf16(a)
    bh, bl = _split_bf16(b)

    def dot(x, y):
        return lax.dot_general(x, y, (dims, ((), ())), preferred_element_type=F32)

    return dot(ah, bh) + (dot(ah, bl) + dot(al, bh))


@jax.custom_vjp
def _imm(a, b):
    return _dot3(a, b, ((1,), (0,)))


def _imm_bwd(res, ct):
    a, b = res
    return _dot3(ct, b, ((1,), (1,))), _dot3(a, ct, ((0,), (0,)))


_imm.defvjp(lambda a, b: (_dot3(a, b, ((1,), (0,))), (a, b)), _imm_bwd)


def _hmm(a, b):
    return jnp.dot(a, b, precision=HI, preferred_element_type=F32)


def _hmm_nt(a, b):
    return lax.dot_general(a, b, (((1,), (1,)), ((), ())), precision=HI, preferred_element_type=F32)


def _hmm_tn(a, b):
    return lax.dot_general(a, b, (((0,), (0,)), ((), ())), precision=HI, preferred_element_type=F32)


def _rows(shape):
    return lax.broadcasted_iota(jnp.int32, shape, 0)


def _cols(shape):
    return lax.broadcasted_iota(jnp.int32, shape, 1)


def _sd(x, s):
    return jnp.where(_rows(x.shape) >= s, pltpu.roll(x, s, axis=0), 0.0)


def _su(x, s):
    n = x.shape[0]
    return jnp.where(_rows(x.shape) < n - s, pltpu.roll(x, n - s, axis=0), 0.0)


@functools.partial(jax.custom_vjp, nondiff_argnums=(1,))
def _shift_down(x, s):
    return _sd(x, s)


def _shift_down_fwd(x, s):
    return _sd(x, s), None


def _shift_down_bwd(s, _, g):
    return (_su(g, s),)


_shift_down.defvjp(_shift_down_fwd, _shift_down_bwd)


def _last_row(x):
    n = x.shape[0]
    return jnp.sum(jnp.where(_rows(x.shape) == n - 1, x, 0.0), axis=0, keepdims=True)


def _prep_fn(p, w0, w1, w2, w3, qk):
    acc = w3 * p + w2 * _shift_down(p, 1) + w1 * _shift_down(p, 2) + w0 * _shift_down(p, 3)
    a = _silu(acc)
    nrm = lax.rsqrt(jnp.sum(a * a, axis=-1, keepdims=True) + EPS)
    return a * (nrm * qk + (1.0 - qk))


def _gates_fn(bd, av, bv):
    tm = bd.shape[0]
    beta_all = _sigmoid(bd)
    g_all = -jnp.exp(av) * _softplus(bd + bv)
    r, c = _rows((tm, tm)), _cols((tm, tm))
    tri = jnp.where((r // CH == c // CH) & (r >= c), 1.0, 0.0).astype(F32)
    gc_all = _hmm(tri, g_all)
    lane = _cols(bd.shape)
    outs = []
    for h in range(NH):
        b = jnp.sum(jnp.where(lane == h, beta_all, 0.0), axis=1, keepdims=True)
        outs.append(jnp.broadcast_to(b, bd.shape))
    for h in range(NH):
        g = jnp.sum(jnp.where(lane == NH + h, gc_all, 0.0), axis=1, keepdims=True)
        outs.append(jnp.broadcast_to(g, bd.shape))
    return tuple(outs)


def _unit_lower_inv(l_mat):
    n = l_mat.shape[0]
    eye = jnp.where(_rows((n, n)) == _cols((n, n)), 1.0, 0.0).astype(F32)
    p = -l_mat
    r = eye + p
    k = 1
    while 2 * k < n:
        p = _imm(p, p)
        r = r + _imm(r, p)
        k *= 2
    return r


def _chunk_local(q, k, v, bb, gcb):
    qs = q * (DH ** -0.5)
    kb = k * bb
    eg = jnp.exp(gcb)
    ii, jj = _rows((CH, CH)), _cols((CH, CH))
    decay = jnp.exp(jnp.where(ii >= jj, gcb - gcb.T, -1e30))
    l_mat = jnp.where(ii > jj, _mm_nt_d(kb, k) * decay, 0.0)
    t_inv = _unit_lower_inv(l_mat)
    u = _mm(t_inv, v * bb)
    w = _mm(t_inv, kb * eg)
    a_qk = _mm_nt_d(qs, k) * decay
    k_dec = k * jnp.exp(_last_row(gcb) - gcb)
    return u, w, qs * eg, k_dec, a_qk


def _state_step(u, w, q_dec, k_dec, a_qk, gcb, state):
    v_new = u - _mm(w, state)
    o = _mm(q_dec, state) + _mm(a_qk, v_new)
    new_state = state * jnp.exp(_last_row(gcb)) + _mm_tn_d(k_dec, v_new)
    return o, new_state


def _s5_tile(u, cr, ci, lr, li, bbr, bbi, ccr, cci):
    xr = _mm(u, bbr)
    xi = _mm(u, bbi)
    first = _rows(xr.shape) == 0
    hr = xr + jnp.where(first, lr * cr - li * ci, 0.0)
    hi = xi + jnp.where(first, lr * ci + li * cr, 0.0)
    pr, pi = lr, li
    s = 1
    while s < u.shape[0]:
        sr, si = _shift_down(hr, s), _shift_down(hi, s)
        hr, hi = hr + pr * sr - pi * si, hi + pr * si + pi * sr
        pr, pi = pr * pr - pi * pi, 2.0 * pr * pi
        s *= 2
    y = _mm(hr, ccr) - _mm(hi, cci)
    return y, _last_row(hr), _last_row(hi)


def _s5_params_fn(ar, ai, ldt, br2, bi2):
    dt = jnp.exp(ldt)
    mag = jnp.exp(ar * dt)
    lr, li = mag * jnp.cos(ai * dt), mag * jnp.sin(ai * dt)
    den = ar * ar + ai * ai
    fr = ((lr - 1.0) * ar + li * ai) / den
    fi = (li * ar - (lr - 1.0) * ai) / den
    expand = jnp.where(_cols((NS, NS * GS)) // GS == _rows((NS, NS * GS)), 1.0, 0.0).astype(F32)
    fr2, fi2 = _hmm(fr, expand), _hmm(fi, expand)
    return lr, li, fr2 * br2 - fi2 * bi2, fr2 * bi2 + fi2 * br2


def _head_norm(o, hn):
    parts = []
    for h in range(NH):
        oh = o[:, h * DH:(h + 1) * DH]
        parts.append(oh * lax.rsqrt(jnp.mean(oh * oh, axis=-1, keepdims=True) + EPS) * hn)
    return jnp.concatenate(parts, axis=1)


def _mix_pre(s5y, u, dvec):
    return _gelu(s5y + dvec * u)


def _mix_mid(o, za, y0, gl, zb, ra, rb, hn):
    ya = _head_norm(o, hn) * _silu(za)
    yb = y0 * _sigmoid(gl) * _silu(zb)
    return _sigmoid(ra) * ya + _sigmoid(rb) * yb


def _mix_post(x, out, npost):
    return x + _rms(out, npost)


def _tile(t, want):
    return min(t, want)


def _row_tile(rows, want):
    return max(r for r in range(16, want + 1, 16) if rows % r == 0)


def _inproj_fwd(x, gain, wcat):
    t = x.shape[0]
    tm, tn = _tile(t, 512), 640

    def body(x_ref, g_ref, w_ref, o_ref, h_ref):
        @pl.when(pl.program_id(1) == 0)
        def _():
            h_ref[...] = _rms(x_ref[...], g_ref[...]).astype(h_ref.dtype)
        o_ref[...] = _dot_bf16(h_ref[...], w_ref[...], ((1,), (0,)))

    return pl.pallas_call(
        body, name="inproj_fwd", grid=(t // tm, NCOL // tn),
        in_specs=[pl.BlockSpec((tm, D), lambda i, j: (i, 0)), _full((1, D)),
                  pl.BlockSpec((D, tn), lambda i, j: (0, j))],
        out_specs=[pl.BlockSpec((tm, tn), lambda i, j: (i, j)), pl.BlockSpec((tm, D), lambda i, j: (i, 0))],
        out_shape=[jax.ShapeDtypeStruct((t, NCOL), F32), jax.ShapeDtypeStruct((t, D), wcat.dtype)],
        compiler_params=_cparams(("parallel", "arbitrary")),
    )(x, gain, wcat)


def _inproj_bwd_dx(dproj, wcat, x, gain, dxres):
    t = x.shape[0]
    tm, tk = _tile(t, 512), 640
    nk = NCOL // tk

    def body(dp_ref, w_ref, x_ref, g_ref, r_ref, dx_ref, dg_ref, acc_ref):
        i, k = pl.program_id(0), pl.program_id(1)

        @pl.when(k == 0)
        def _():
            acc_ref[...] = jnp.zeros_like(acc_ref)

        acc_ref[...] += _mm_nt(dp_ref[...], w_ref[...])

        @pl.when(k == nk - 1)
        def _():
            _, vjp = jax.vjp(_rms, x_ref[...], g_ref[...])
            dx, dg = vjp(acc_ref[...])
            dx_ref[...] = r_ref[...] + dx

            @pl.when(i == 0)
            def _():
                dg_ref[...] = dg

            @pl.when(i > 0)
            def _():
                dg_ref[...] += dg

    return pl.pallas_call(
        body, name="inproj_bwd_dx", grid=(t // tm, nk),
        in_specs=[pl.BlockSpec((tm, tk), lambda i, k: (i, k)), pl.BlockSpec((D, tk), lambda i, k: (0, k)),
                  pl.BlockSpec((tm, D), lambda i, k: (i, 0)), _full((1, D)),
                  pl.BlockSpec((tm, D), lambda i, k: (i, 0))],
        out_specs=[pl.BlockSpec((tm, D), lambda i, k: (i, 0)), _full((1, D))],
        out_shape=[jax.ShapeDtypeStruct((t, D), F32), jax.ShapeDtypeStruct((1, D), F32)],
        scratch_shapes=[pltpu.VMEM((tm, D), F32)],
        compiler_params=_cparams(("arbitrary", "arbitrary")),
    )(dproj, wcat, x, gain, dxres)


def _inproj_bwd_dw(h, dproj):
    t = h.shape[0]
    tm, tn = _tile(t, 512), 640

    def body(h_ref, dp_ref, o_ref):
        @pl.when(pl.program_id(1) == 0)
        def _():
            o_ref[...] = jnp.zeros_like(o_ref)

        o_ref[...] += _mm_tn(h_ref[...], dp_ref[...])

    return pl.pallas_call(
        body, name="inproj_bwd_dw", grid=(NCOL // tn, t // tm),
        in_specs=[pl.BlockSpec((tm, D), lambda j, i: (i, 0)), pl.BlockSpec((tm, tn), lambda j, i: (i, j))],
        out_specs=pl.BlockSpec((D, tn), lambda j, i: (0, j)),
        out_shape=jax.ShapeDtypeStruct((D, NCOL), F32),
        compiler_params=_cparams(("parallel", "arbitrary")),
    )(h, dproj)


def _prep_fwd(proj, cw):
    t = proj.shape[0]

    def body(p_ref, w_ref, o_ref):
        qk = (pl.program_id(0) < 2 * NH).astype(F32)
        o_ref[...] = _prep_fn(p_ref[...], w_ref[0:1, :], w_ref[1:2, :], w_ref[2:3, :], w_ref[3:4, :], qk)

    return pl.pallas_call(
        body, name="prep_fwd", grid=(3 * NH,),
        in_specs=[pl.BlockSpec((t, DH), lambda c: (0, c)), pl.BlockSpec((4, DH), lambda c: (0, c))],
        out_specs=pl.BlockSpec((None, t, DH), lambda c: (c, 0, 0)),
        out_shape=jax.ShapeDtypeStruct((3 * NH, t, DH), F32),
        compiler_params=_cparams(("parallel",)),
    )(proj, cw)


def _prep_bwd(proj, cw, dqkv):
    t = proj.shape[0]

    def body(p_ref, w_ref, d_ref, dp_ref, dw_ref):
        qk = (pl.program_id(0) < 2 * NH).astype(F32)
        _, vjp = jax.vjp(lambda p, w0, w1, w2, w3: _prep_fn(p, w0, w1, w2, w3, qk),
                         p_ref[...], w_ref[0:1, :], w_ref[1:2, :], w_ref[2:3, :], w_ref[3:4, :])
        dp, dw0, dw1, dw2, dw3 = vjp(d_ref[...])
        dp_ref[...] = dp
        dw_ref[0:1, :] = dw0
        dw_ref[1:2, :] = dw1
        dw_ref[2:3, :] = dw2
        dw_ref[3:4, :] = dw3

    return pl.pallas_call(
        body, name="prep_bwd", grid=(3 * NH,),
        in_specs=[pl.BlockSpec((t, DH), lambda c: (0, c)), pl.BlockSpec((4, DH), lambda c: (0, c)),
                  pl.BlockSpec((None, t, DH), lambda c: (c, 0, 0))],
        out_specs=[pl.BlockSpec((t, DH), lambda c: (0, c)), pl.BlockSpec((4, DH), lambda c: (0, c))],
        out_shape=[jax.ShapeDtypeStruct((t, 3 * D), F32), jax.ShapeDtypeStruct((4, 3 * D), F32)],
        compiler_params=_cparams(("parallel",)),
    )(proj, cw, dqkv)


def _gates_fwd(proj, gvec):
    t = proj.shape[0]
    tm = _tile(t, 512)

    def body(p_ref, gv_ref, b_ref, g_ref):
        outs = _gates_fn(p_ref[...], gv_ref[0:1, :], gv_ref[1:2, :])
        for h in range(NH):
            b_ref[h] = outs[h]
            g_ref[h] = outs[NH + h]

    spec = pl.BlockSpec((NH, tm, DH), lambda i: (0, i, 0))
    return pl.pallas_call(
        body, name="gates_fwd", grid=(t // tm,),
        in_specs=[pl.BlockSpec((tm, DH), lambda i: (i, BD0 // DH)), _full((8, DH))],
        out_specs=[spec, spec],
        out_shape=[jax.ShapeDtypeStruct((NH, t, DH), F32)] * 2,
        compiler_params=_cparams(("parallel",)),
    )(proj, gvec)


def _gates_bwd(proj, gvec, dbb, dgcb):
    t = proj.shape[0]
    tm = _tile(t, 512)

    def body(p_ref, gv_ref, db_ref, dg_ref, dp_ref, dgv_ref):
        _, vjp = jax.vjp(_gates_fn, p_ref[...], gv_ref[0:1, :], gv_ref[1:2, :])
        cts = tuple(db_ref[h] for h in range(NH)) + tuple(dg_ref[h] for h in range(NH))
        dp, da, db = vjp(cts)
        dp_ref[...] = dp

        @pl.when(pl.program_id(0) == 0)
        def _():
            dgv_ref[...] = jnp.zeros_like(dgv_ref)

        dgv_ref[0:1, :] += da
        dgv_ref[1:2, :] += db

    spec = pl.BlockSpec((NH, tm, DH), lambda i: (0, i, 0))
    return pl.pallas_call(
        body, name="gates_bwd", grid=(t // tm,),
        in_specs=[pl.BlockSpec((tm, DH), lambda i: (i, BD0 // DH)), _full((8, DH)), spec, spec],
        out_specs=[pl.BlockSpec((tm, DH), lambda i: (i, 0)), _full((8, DH))],
        out_shape=[jax.ShapeDtypeStruct((t, DH), F32), jax.ShapeDtypeStruct((8, DH), F32)],
        compiler_params=_cparams(("arbitrary",)),
    )(proj, gvec, dbb, dgcb)


def _chunks_per_step(nch):
    return 2 if nch % 2 == 0 else 1


def _delta_local_fwd(qkv, bb, gcb):
    t = qkv.shape[1]
    cps = _chunks_per_step(t // CH)
    rows = cps * CH

    def body(q_ref, k_ref, v_ref, b_ref, g_ref, *out_refs):
        for c in range(cps):
            sl = slice(c * CH, (c + 1) * CH)
            outs = _chunk_local(q_ref[sl, :], k_ref[sl, :], v_ref[sl, :], b_ref[sl, :], g_ref[sl, :])
            for ref, val in zip(out_refs, outs):
                ref[sl, :] = val

    def blk(off):
        return pl.BlockSpec((None, rows, DH), lambda h, n: (h + off, n, 0))

    return pl.pallas_call(
        body, name="delta_local_fwd", grid=(NH, t // rows),
        in_specs=[blk(0), blk(NH), blk(2 * NH), blk(0), blk(0)],
        out_specs=[blk(0)] * 5,
        out_shape=[jax.ShapeDtypeStruct((NH, t, DH), F32)] * 5,
        compiler_params=_cparams(("parallel", "parallel")),
    )(qkv, qkv, qkv, bb, gcb)


def _delta_local_bwd(qkv, bb, gcb, cts, dgcb_state):
    t = qkv.shape[1]
    cps = _chunks_per_step(t // CH)
    rows = cps * CH

    def body(q_ref, k_ref, v_ref, b_ref, g_ref, du_ref, dw_ref, dqd_ref, dkd_ref, da_ref, dgs_ref,
             dq_ref, dk_ref, dv_ref, db_ref, dg_ref):
        for c in range(cps):
            sl = slice(c * CH, (c + 1) * CH)
            _, vjp = jax.vjp(_chunk_local, q_ref[sl, :], k_ref[sl, :], v_ref[sl, :], b_ref[sl, :], g_ref[sl, :])
            dq, dk, dv, db, dg = vjp((du_ref[sl, :], dw_ref[sl, :], dqd_ref[sl, :], dkd_ref[sl, :], da_ref[sl, :]))
            dq_ref[sl, :] = dq
            dk_ref[sl, :] = dk
            dv_ref[sl, :] = dv
            db_ref[sl, :] = db
            dg_ref[sl, :] = dg + dgs_ref[sl, :]

    def blk(off):
        return pl.BlockSpec((None, rows, DH), lambda h, n: (h + off, n, 0))

    return pl.pallas_call(
        body, name="delta_local_bwd", grid=(NH, t // rows),
        in_specs=[blk(0), blk(NH), blk(2 * NH)] + [blk(0)] * 8,
        out_specs=[blk(0)] * 5,
        out_shape=[jax.ShapeDtypeStruct((NH, t, DH), F32)] * 5,
        compiler_params=_cparams(("parallel", "parallel")),
    )(qkv, qkv, qkv, bb, gcb, *cts, dgcb_state)


def _delta_state_fwd(local, gcb):
    t = gcb.shape[1]
    nch = t // CH

    def body(u_ref, w_ref, qd_ref, kd_ref, a_ref, g_ref, o_ref, s_ref, st_ref):
        @pl.when(pl.program_id(0) == 0)
        def _():
            st_ref[...] = jnp.zeros_like(st_ref)

        for h in range(NH):
            s_ref[h] = st_ref[h]
            o, ns = _state_step(u_ref[h], w_ref[h], qd_ref[h], kd_ref[h], a_ref[h], g_ref[h], st_ref[h])
            o_ref[:, h * DH:(h + 1) * DH] = o
            st_ref[h] = ns

    blk = pl.BlockSpec((NH, CH, DH), lambda n: (0, n, 0))
    return pl.pallas_call(
        body, name="delta_state_fwd", grid=(nch,),
        in_specs=[blk] * 6,
        out_specs=[pl.BlockSpec((CH, D), lambda n: (n, 0)),
                   pl.BlockSpec((NH, None, DH, DH), lambda n: (0, n, 0, 0))],
        out_shape=[jax.ShapeDtypeStruct((t, D), F32), jax.ShapeDtypeStruct((NH, nch, DH, DH), F32)],
        scratch_shapes=[pltpu.VMEM((NH, DH, DH), F32)],
        compiler_params=_cparams(("arbitrary",)),
    )(*local, gcb)


def _delta_state_bwd(local, gcb, states, do):
    t = gcb.shape[1]
    nch = t // CH

    def body(u_ref, w_ref, qd_ref, kd_ref, a_ref, g_ref, s_ref, do_ref,
             du_ref, dw_ref, dqd_ref, dkd_ref, da_ref, dg_ref, ds_ref):
        @pl.when(pl.program_id(0) == 0)
        def _():
            ds_ref[...] = jnp.zeros_like(ds_ref)

        for h in range(NH):
            _, vjp = jax.vjp(_state_step, u_ref[h], w_ref[h], qd_ref[h], kd_ref[h], a_ref[h], g_ref[h], s_ref[h])
            du, dw, dqd, dkd, da, dg, ds = vjp((do_ref[:, h * DH:(h + 1) * DH], ds_ref[h]))
            du_ref[h] = du
            dw_ref[h] = dw
            dqd_ref[h] = dqd
            dkd_ref[h] = dkd
            da_ref[h] = da
            dg_ref[h] = dg
            ds_ref[h] = ds

    blk = pl.BlockSpec((NH, CH, DH), lambda n: (0, nch - 1 - n, 0))
    return pl.pallas_call(
        body, name="delta_state_bwd", grid=(nch,),
        in_specs=[blk] * 6 + [pl.BlockSpec((NH, None, DH, DH), lambda n: (0, nch - 1 - n, 0, 0)),
                              pl.BlockSpec((CH, D), lambda n: (nch - 1 - n, 0))],
        out_specs=[blk] * 6,
        out_shape=[jax.ShapeDtypeStruct((NH, t, DH), F32)] * 6,
        scratch_shapes=[pltpu.VMEM((NH, DH, DH), F32)],
        compiler_params=_cparams(("arbitrary",)),
    )(*local, gcb, states, do)


def _s5_params(ar, ai, ldt, br2, bi2):
    def body(ar_ref, ai_ref, ld_ref, br_ref, bi_ref, lr_ref, li_ref, bbr_ref, bbi_ref):
        lr, li, bbr, bbi = _s5_params_fn(ar_ref[...], ai_ref[...], ld_ref[...], br_ref[...], bi_ref[...])
        lr_ref[...] = lr
        li_ref[...] = li
        bbr_ref[...] = bbr
        bbi_ref[...] = bbi

    sq = pl.BlockSpec((None, NG, NS), lambda l: (l, 0, 0))
    wide = pl.BlockSpec((None, NG, NS * GS), lambda l: (l, 0, 0))
    return pl.pallas_call(
        body, name="s5_params", grid=(DEPTH,),
        in_specs=[sq, sq, pl.BlockSpec((None, NG, 1), lambda l: (l, 0, 0)), wide, wide],
        out_specs=[sq, sq, wide, wide],
        out_shape=[jax.ShapeDtypeStruct((DEPTH, NG, NS), F32)] * 2
        + [jax.ShapeDtypeStruct((DEPTH, NG, NS * GS), F32)] * 2,
        compiler_params=_cparams(("parallel",)),
    )(ar, ai, ldt, br2, bi2)


def _s5_params_bwd(ar, ai, ldt, br2, bi2, dlr, dli, dbbr, dbbi):
    def body(ar_ref, ai_ref, ld_ref, br_ref, bi_ref, a_ref, b_ref, c_ref, d_ref,
             dar_ref, dai_ref, dld_ref, dbr_ref, dbi_ref):
        _, vjp = jax.vjp(_s5_params_fn, ar_ref[...], ai_ref[...], ld_ref[...], br_ref[...], bi_ref[...])
        dar, dai, dld, dbr, dbi = vjp((a_ref[...], b_ref[...], c_ref[...], d_ref[...]))
        dar_ref[...] = dar
        dai_ref[...] = dai
        dld_ref[...] = dld
        dbr_ref[...] = dbr
        dbi_ref[...] = dbi

    sq = pl.BlockSpec((None, NG, NS), lambda l: (l, 0, 0))
    col = pl.BlockSpec((None, NG, 1), lambda l: (l, 0, 0))
    wide = pl.BlockSpec((None, NG, NS * GS), lambda l: (l, 0, 0))
    return pl.pallas_call(
        body, name="s5_params_bwd", grid=(DEPTH,),
        in_specs=[sq, sq, col, wide, wide, sq, sq, wide, wide],
        out_specs=[sq, sq, col, wide, wide],
        out_shape=[jax.ShapeDtypeStruct((DEPTH, NG, NS), F32)] * 2 + [jax.ShapeDtypeStruct((DEPTH, NG, 1), F32)]
        + [jax.ShapeDtypeStruct((DEPTH, NG, NS * GS), F32)] * 2,
        compiler_params=_cparams(("parallel",)),
    )(ar, ai, ldt, br2, bi2, dlr, dli, dbbr, dbbi)


def _s5_tile_rows(t):
    return _tile(t // 2, 256)


def _s5_fwd(proj, lam, bblk, cblk):
    t = proj.shape[0]
    r = _s5_tile_rows(t)
    nt = t // r
    u0 = 4 * D // DH

    def body(u_ref, lam_ref, b_ref, c_ref, y_ref, car_ref, st_ref):
        @pl.when(pl.program_id(1) == 0)
        def _():
            st_ref[...] = jnp.zeros_like(st_ref)

        car_ref[...] = st_ref[...]
        y, cr, ci = _s5_tile(u_ref[...], st_ref[0:1, :], st_ref[1:2, :], lam_ref[0], lam_ref[1],
                             b_ref[0], b_ref[1], c_ref[0], c_ref[1])
        y_ref[...] = y
        st_ref[0:1, :] = cr
        st_ref[1:2, :] = ci

    return pl.pallas_call(
        body, name="s5_fwd", grid=(NCB, nt),
        in_specs=[pl.BlockSpec((r, DH), lambda c, i: (i, u0 + c)),
                  pl.BlockSpec((2, 1, SW), lambda c, i: (0, 0, c)),
                  pl.BlockSpec((2, None, DH, SW), lambda c, i: (0, c, 0, 0)),
                  pl.BlockSpec((2, None, SW, DH), lambda c, i: (0, c, 0, 0))],
        out_specs=[pl.BlockSpec((r, DH), lambda c, i: (i, c)),
                   pl.BlockSpec((None, 8, SW), lambda c, i: (i, 0, c))],
        out_shape=[jax.ShapeDtypeStruct((t, D), F32), jax.ShapeDtypeStruct((nt, 8, NG * NS), F32)],
        scratch_shapes=[pltpu.VMEM((8, SW), F32)],
        compiler_params=_cparams(("parallel", "arbitrary")),
    )(proj, lam, bblk, cblk)


def _s5_bwd(proj, lam, bblk, cblk, carries, dy, du_skip):
    t = proj.shape[0]
    r = _s5_tile_rows(t)
    nt = t // r
    u0 = 4 * D // DH

    def body(u_ref, lam_ref, b_ref, c_ref, car_ref, dy_ref, dus_ref, du_ref, dlam_ref, db_ref, dc_ref, dst_ref):
        first = pl.program_id(1) == 0

        @pl.when(first)
        def _():
            dst_ref[...] = jnp.zeros_like(dst_ref)

        _, vjp = jax.vjp(_s5_tile, u_ref[...], car_ref[0:1, :], car_ref[1:2, :], lam_ref[0], lam_ref[1],
                         b_ref[0], b_ref[1], c_ref[0], c_ref[1])
        du, dcr, dci, dlr, dli, dbr, dbi, dcr2, dci2 = vjp((dy_ref[...], dst_ref[0:1, :], dst_ref[1:2, :]))
        du_ref[...] = du + dus_ref[...]
        dst_ref[0:1, :] = dcr
        dst_ref[1:2, :] = dci

        @pl.when(first)
        def _():
            dlam_ref[0] = dlr
            dlam_ref[1] = dli
            db_ref[0] = dbr
            db_ref[1] = dbi
            dc_ref[0] = dcr2
            dc_ref[1] = dci2

        @pl.when(jnp.logical_not(first))
        def _():
            dlam_ref[0] += dlr
            dlam_ref[1] += dli
            db_ref[0] += dbr
            db_ref[1] += dbi
            dc_ref[0] += dcr2
            dc_ref[1] += dci2

    return pl.pallas_call(
        body, name="s5_bwd", grid=(NCB, nt),
        in_specs=[pl.BlockSpec((r, DH), lambda c, i: (nt - 1 - i, u0 + c)),
                  pl.BlockSpec((2, 1, SW), lambda c, i: (0, 0, c)),
                  pl.BlockSpec((2, None, DH, SW), lambda c, i: (0, c, 0, 0)),
                  pl.BlockSpec((2, None, SW, DH), lambda c, i: (0, c, 0, 0)),
                  pl.BlockSpec((None, 8, SW), lambda c, i: (nt - 1 - i, 0, c)),
                  pl.BlockSpec((r, DH), lambda c, i: (nt - 1 - i, c)),
                  pl.BlockSpec((r, DH), lambda c, i: (nt - 1 - i, c))],
        out_specs=[pl.BlockSpec((r, DH), lambda c, i: (nt - 1 - i, c)),
                   pl.BlockSpec((2, 1, SW), lambda c, i: (0, 0, c)),
                   pl.BlockSpec((2, None, DH, SW), lambda c, i: (0, c, 0, 0)),
                   pl.BlockSpec((2, None, SW, DH), lambda c, i: (0, c, 0, 0))],
        out_shape=[jax.ShapeDtypeStruct((t, D), F32), jax.ShapeDtypeStruct((2, 1, NG * NS), F32),
                   jax.ShapeDtypeStruct((2, NCB, DH, SW), F32), jax.ShapeDtypeStruct((2, NCB, SW, DH), F32)],
        scratch_shapes=[pltpu.VMEM((8, SW), F32)],
        compiler_params=_cparams(("parallel", "arbitrary")),
    )(proj, lam, bblk, cblk, carries, dy, du_skip)


def _proj_spec(tm, col):
    return pl.BlockSpec((tm, D), lambda i: (i, col))


def _mix_fwd(proj, o, s5y, x, hn, dvec, wglu, bglu, wout, npost):
    t = x.shape[0]
    tm = _tile(t, 256)

    def body(za_ref, u_ref, zb_ref, ra_ref, rb_ref, o_ref, y_ref, x_ref, hn_ref, d_ref, wg_ref, bg_ref, wo_ref,
             np_ref, xn_ref):
        y0 = _mix_pre(y_ref[...], u_ref[...], d_ref[...])
        gl = _mm(y0, wg_ref[...]) + bg_ref[...]
        m = _mix_mid(o_ref[...], za_ref[...], y0, gl, zb_ref[...], ra_ref[...], rb_ref[...], hn_ref[...])
        out = _mm(m, wo_ref[...])
        xn_ref[...] = _mix_post(x_ref[...], out, np_ref[...])

    act = pl.BlockSpec((tm, D), lambda i: (i, 0))
    return pl.pallas_call(
        body, name="mix_fwd", grid=(t // tm,),
        in_specs=[_proj_spec(tm, 3), _proj_spec(tm, 4), _proj_spec(tm, 5), _proj_spec(tm, 6), _proj_spec(tm, 7),
                  act, act, act, _full((1, DH)), _full((1, D)), _full((D, D)), _full((1, D)), _full((D, D)),
                  _full((1, D))],
        out_specs=act,
        out_shape=jax.ShapeDtypeStruct((t, D), F32),
        compiler_params=_cparams(("parallel",)),
    )(proj, proj, proj, proj, proj, o, s5y, x, hn, dvec, wglu, bglu, wout, npost)


def _mix_bwd(proj, o, s5y, x, hn, dvec, wglu, bglu, wout, npost, dxn):
    t = x.shape[0]
    tm = _tile(t, 128)

    def body(za_ref, u_ref, zb_ref, ra_ref, rb_ref, o_ref, y_ref, x_ref, hn_ref, d_ref, wg_ref, bg_ref, wo_ref,
             np_ref, dxn_ref,
             dza_ref, du_ref, dzb_ref, dra_ref, drb_ref, do_ref, dy_ref, dx_ref,
             dwg_ref, dwo_ref, dvecs_ref, dhn_ref):
        y0, vjp_pre = jax.vjp(_mix_pre, y_ref[...], u_ref[...], d_ref[...])
        gl = _mm(y0, wg_ref[...]) + bg_ref[...]
        m, vjp_mid = jax.vjp(_mix_mid, o_ref[...], za_ref[...], y0, gl, zb_ref[...], ra_ref[...], rb_ref[...],
                             hn_ref[...])
        out = _mm(m, wo_ref[...])
        _, vjp_post = jax.vjp(_mix_post, x_ref[...], out, np_ref[...])
        dx, dout, dnp = vjp_post(dxn_ref[...])
        dm = _mm_nt(dout, wo_ref[...])
        dwo = _mm_tn(m, dout)
        do, dza, dy0, dgl, dzb, dra, drb, dhn = vjp_mid(dm)
        dwg = _mm_tn(y0, dgl)
        dbg = jnp.sum(dgl, axis=0, keepdims=True)
        dy0 = dy0 + _mm_nt(dgl, wg_ref[...])
        dy, du, dd = vjp_pre(dy0)
        dza_ref[...] = dza
        du_ref[...] = du
        dzb_ref[...] = dzb
        dra_ref[...] = dra
        drb_ref[...] = drb
        do_ref[...] = do
        dy_ref[...] = dy
        dx_ref[...] = dx
        first = pl.program_id(0) == 0

        @pl.when(first)
        def _():
            dwg_ref[...] = dwg
            dwo_ref[...] = dwo
            dvecs_ref[...] = jnp.zeros_like(dvecs_ref)
            dhn_ref[...] = jnp.zeros_like(dhn_ref)

        @pl.when(jnp.logical_not(first))
        def _():
            dwg_ref[...] += dwg
            dwo_ref[...] += dwo

        dvecs_ref[0:1, :] += dd
        dvecs_ref[1:2, :] += dbg
        dvecs_ref[2:3, :] += dnp
        dhn_ref[0:1, :] += dhn

    act = pl.BlockSpec((tm, D), lambda i: (i, 0))
    a = jax.ShapeDtypeStruct((t, D), F32)
    w = jax.ShapeDtypeStruct((D, D), F32)
    return pl.pallas_call(
        body, name="mix_bwd", grid=(t // tm,),
        in_specs=[_proj_spec(tm, 3), _proj_spec(tm, 4), _proj_spec(tm, 5), _proj_spec(tm, 6), _proj_spec(tm, 7),
                  act, act, act, _full((1, DH)), _full((1, D)), _full((D, D)), _full((1, D)), _full((D, D)),
                  _full((1, D)), act],
        out_specs=[act] * 8 + [_full((D, D)), _full((D, D)), _full((8, D)), _full((8, DH))],
        out_shape=[a] * 8 + [w, w, jax.ShapeDtypeStruct((8, D), F32), jax.ShapeDtypeStruct((8, DH), F32)],
        compiler_params=_cparams(("arbitrary",)),
    )(proj, proj, proj, proj, proj, o, s5y, x, hn, dvec, wglu, bglu, wout, npost, dxn)


def _loss_grad(y, target):
    t = y.shape[0]
    tm = _tile(t, 512)

    def body(y_ref, t_ref, dy_ref, l_ref):
        err = y_ref[...] - t_ref[...]
        dy_ref[...] = err * (1.0 / D)
        part = jnp.sum(jnp.sum(err * err, axis=1, keepdims=True), axis=0, keepdims=True) * (0.5 / D)
        part = jnp.broadcast_to(part, (8, DH))

        @pl.when(pl.program_id(0) == 0)
        def _():
            l_ref[...] = part

        @pl.when(pl.program_id(0) > 0)
        def _():
            l_ref[...] += part

    act = pl.BlockSpec((tm, D), lambda i: (i, 0))
    return pl.pallas_call(
        body, name="loss_grad", grid=(t // tm,),
        in_specs=[act, act], out_specs=[act, _full((8, DH))],
        out_shape=[jax.ShapeDtypeStruct((t, D), F32), jax.ShapeDtypeStruct((8, DH), F32)],
        compiler_params=_cparams(("arbitrary",)),
    )(y, target)


def _flips(rel):
    x, y, c = lax.axis_index("x"), lax.axis_index("y"), lax.axis_index("c")
    fx, fy, fc = rel
    return (x ^ fx if fx else x, y ^ fy if fy else y, c ^ fc if fc else c)


CHIP_RELS = ((1, 0, 0), (0, 1, 0), (1, 1, 0))
ALL_RELS = tuple((fx, fy, fc) for fx in (0, 1) for fy in (0, 1) for fc in (0, 1) if (fx, fy, fc) != (0, 0, 0))


def _slot_of(pos, by_chip):
    px, py, pc = pos
    return 2 * px + py if by_chip else 4 * px + 2 * py + pc


def _exchange(src, rels, by_chip, scatter, name):
    nslot = NCHIP if by_chip else NDEV
    shape = src.shape[-2:]
    nrel = len(rels)

    def body(src_ref, dst_ref, send_sems, recv_sems, local_sem):
        me = _flips((0, 0, 0))
        my_slot = _slot_of(me, by_chip)
        mine = pltpu.make_async_copy(src_ref.at[my_slot] if scatter else src_ref, dst_ref.at[my_slot], local_sem)
        mine.start()
        copies = []
        for k, rel in enumerate(rels):
            peer = _flips(rel)
            part = src_ref.at[_slot_of(peer, by_chip)] if scatter else src_ref
            copies.append(pltpu.make_async_remote_copy(
                src_ref=part, dst_ref=dst_ref.at[my_slot], send_sem=send_sems.at[k], recv_sem=recv_sems.at[k],
                device_id=peer, device_id_type=pl.DeviceIdType.MESH))
        for cp in copies:
            cp.start()
        for k, rel in enumerate(rels):
            peer = _flips(rel)
            part = src_ref.at[0] if scatter else src_ref
            pltpu.make_async_remote_copy(
                src_ref=part, dst_ref=dst_ref.at[_slot_of(peer, by_chip)], send_sem=send_sems.at[k],
                recv_sem=recv_sems.at[k], device_id=peer, device_id_type=pl.DeviceIdType.MESH).wait_recv()
        for cp in copies:
            cp.wait_send()
        mine.wait()

    return pl.pallas_call(
        body, name=name,
        in_specs=[pl.BlockSpec(memory_space=pl.ANY)],
        out_specs=pl.BlockSpec(memory_space=pl.ANY),
        out_shape=jax.ShapeDtypeStruct((nslot,) + shape, src.dtype),
        scratch_shapes=[pltpu.SemaphoreType.DMA((nrel,)), pltpu.SemaphoreType.DMA((nrel,)), pltpu.SemaphoreType.DMA],
    )(src)


def _sibling_swap(src, name):
    def body(src_ref, dst_ref, send_sem, recv_sem):
        peer = _flips((0, 0, 1))
        cp = pltpu.make_async_remote_copy(src_ref=src_ref, dst_ref=dst_ref, send_sem=send_sem, recv_sem=recv_sem,
                                          device_id=peer, device_id_type=pl.DeviceIdType.MESH)
        cp.start()
        cp.wait()

    return pl.pallas_call(
        body, name=name,
        in_specs=[pl.BlockSpec(memory_space=pl.ANY)],
        out_specs=pl.BlockSpec(memory_space=pl.ANY),
        out_shape=jax.ShapeDtypeStruct(src.shape, src.dtype),
        scratch_shapes=[pltpu.SemaphoreType.DMA, pltpu.SemaphoreType.DMA],
    )(src)


def _sum_slots(parts, name):
    ns, rows, cols = parts.shape
    tr = _row_tile(rows, 256)

    def body(p_ref, o_ref):
        acc = p_ref[0].astype(F32)
        for s in range(1, ns):
            acc = acc + p_ref[s].astype(F32)
        o_ref[...] = acc

    return pl.pallas_call(
        body, name=name, grid=(rows // tr,),
        in_specs=[pl.BlockSpec((ns, tr, cols), lambda i: (0, i, 0))],
        out_specs=pl.BlockSpec((tr, cols), lambda i: (i, 0)),
        out_shape=jax.ShapeDtypeStruct((rows, cols), F32),
        compiler_params=_cparams(("parallel",)),
    )(parts)


def _adamw(w, g_parts, m, v, name):
    ns, rows, cols = g_parts.shape
    tr = _row_tile(rows, 256)
    c1 = 1.0 / (1.0 - ADAM_B1 ** ADAM_STEP)
    c2 = 1.0 / (1.0 - ADAM_B2 ** ADAM_STEP)

    def body(w_ref, g_ref, m_ref, v_ref, go_ref, d_ref, nm_ref, nv_ref):
        g = g_ref[0]
        for s in range(1, ns):
            g = g + g_ref[s]
        nm = ADAM_B1 * m_ref[...] + (1.0 - ADAM_B1) * g
        nv = ADAM_B2 * v_ref[...] + (1.0 - ADAM_B2) * (g * g)
        d_ref[...] = -ADAM_LR * ((nm * c1) / (jnp.sqrt(nv * c2) + ADAM_EPS) + ADAM_WD * w_ref[...])
        go_ref[...] = g
        nm_ref[...] = nm
        nv_ref[...] = nv

    blk = pl.BlockSpec((tr, cols), lambda i: (i, 0))
    out = jax.ShapeDtypeStruct((rows, cols), F32)
    return pl.pallas_call(
        body, name=name, grid=(rows // tr,),
        in_specs=[blk, pl.BlockSpec((ns, tr, cols), lambda i: (0, i, 0)), blk, blk],
        out_specs=[blk] * 4, out_shape=[out] * 4,
        compiler_params=_cparams(("parallel",)),
    )(w, g_parts, m, v)


SHARD_ROWS = 10272
W_IN_ROWS = 8208
CONV_ROWS = 16
WIN_SHARD = 2052
CONV_SHARD = 768
ROW_SHARD = 256

SMALL = (("norm_pre", (DEPTH, D)), ("a_log", (DEPTH, NH)), ("dt_bias", (DEPTH, NH)), ("head_norm", (DEPTH, DH)),
         ("ssm_a_re", (DEPTH, NG, NS)), ("ssm_a_im", (DEPTH, NG, NS)), ("ssm_log_dt", (DEPTH, NG)),
         ("ssm_b_re", (DEPTH, NG, NS, GS)), ("ssm_b_im", (DEPTH, NG, NS, GS)),
         ("ssm_c_re", (DEPTH, NG, GS, NS)), ("ssm_c_im", (DEPTH, NG, GS, NS)), ("ssm_d", (DEPTH, D)),
         ("b_glu", (DEPTH, D)), ("norm_post", (DEPTH, D)))


def _pad_rows(flat, rows):
    return jnp.pad(flat, (0, rows * D - flat.shape[0])).reshape(rows, D)


def _pack_shard(w_in, conv_w, w_glu, w_out):
    return jnp.concatenate([w_in.reshape(W_IN_ROWS, D), _pad_rows(conv_w.reshape(-1), CONV_ROWS),
                            w_glu.reshape(-1, D), w_out.reshape(-1, D)], axis=0)


def _unpack_shard(flat):
    w_in = flat[:W_IN_ROWS].reshape(DEPTH, D, WIN_SHARD)
    conv = flat[W_IN_ROWS:W_IN_ROWS + CONV_ROWS].reshape(-1)[:DEPTH * 4 * CONV_SHARD].reshape(DEPTH, 4, CONV_SHARD)
    r0 = W_IN_ROWS + CONV_ROWS
    w_glu = flat[r0:r0 + DEPTH * ROW_SHARD].reshape(DEPTH, ROW_SHARD, D)
    w_out = flat[r0 + DEPTH * ROW_SHARD:].reshape(DEPTH, ROW_SHARD, D)
    return w_in, conv, w_glu, w_out


def _small_rows():
    return sum(-(-math.prod(s) // (8 * D)) * 8 for _, s in SMALL)


def _pack_small(vals):
    parts = []
    for (_, shape), val in zip(SMALL, vals):
        n = math.prod(shape)
        parts.append(_pad_rows(val.reshape(-1), -(-n // (8 * D)) * 8))
    return jnp.concatenate(parts, axis=0)


def _unpack_small(flat):
    outs, r0 = [], 0
    for _, shape in SMALL:
        n = math.prod(shape)
        rows = -(-n // (8 * D)) * 8
        outs.append(flat[r0:r0 + rows].reshape(-1)[:n].reshape(shape))
        r0 += rows
    return outs


def _rearrange_cols(w):
    pad = jnp.zeros(w.shape[:-1] + (NCOL - BD0 - 2 * NH,), w.dtype)
    return jnp.concatenate([w[..., :4 * D], w[..., 4 * D + 2 * NH:], w[..., 4 * D:4 * D + 2 * NH], pad], axis=-1)


def _restore_cols(w):
    return jnp.concatenate([w[..., :4 * D], w[..., BD0:BD0 + 2 * NH], w[..., 4 * D:BD0]], axis=-1)


def _block_diag_b(bb2):
    b = bb2.reshape(NCB, GPB, NS, GS)
    eye = jnp.eye(GPB, dtype=F32)
    return jnp.einsum("kgnc,gh->kgchn", b, eye).reshape(NCB, GPB * GS, SW)


def _block_diag_b_t(d):
    return jnp.einsum("kgchn,gh->kgnc", d.reshape(NCB, GPB, GS, GPB, NS), jnp.eye(GPB, dtype=F32)).reshape(NG, NS * GS)


def _block_diag_c(c):
    eye = jnp.eye(GPB, dtype=F32)
    return jnp.einsum("kgcn,gh->kgnhc", c.reshape(NCB, GPB, GS, NS), eye).reshape(NCB, SW, GPB * GS)


def _block_diag_c_t(d):
    return jnp.einsum("kgnhc,gh->kgcn", d.reshape(NCB, GPB, NS, GPB, GS), jnp.eye(GPB, dtype=F32)).reshape(NG, GS, NS)


def _local_step(x, target, wcat, conv, wglu, wout, small):
    ar, ai = small["ssm_a_re"], small["ssm_a_im"]
    ldt = small["ssm_log_dt"].reshape(DEPTH, NG, 1)
    br2 = small["ssm_b_re"].reshape(DEPTH, NG, NS * GS)
    bi2 = small["ssm_b_im"].reshape(DEPTH, NG, NS * GS)
    lr, li, bbr2, bbi2 = _s5_params(ar, ai, ldt, br2, bi2)

    def row(name, l, width):
        return small[name][l].reshape(1, width)

    saved = []
    for l in range(DEPTH):
        gvec = jnp.pad(jnp.stack([small["a_log"][l], small["dt_bias"][l]]), ((0, 6), (NH, DH - 2 * NH)))
        lam = jnp.stack([lr[l].reshape(1, NG * NS), li[l].reshape(1, NG * NS)])
        bblk = jnp.stack([_block_diag_b(bbr2[l]), _block_diag_b(bbi2[l])])
        cblk = jnp.stack([_block_diag_c(small["ssm_c_re"][l]), _block_diag_c(small["ssm_c_im"][l])])
        proj, h = _inproj_fwd(x, row("norm_pre", l, D), wcat[l])
        qkv = _prep_fwd(proj, conv[l])
        bb, gcb = _gates_fwd(proj, gvec)
        local = _delta_local_fwd(qkv, bb, gcb)
        o, states = _delta_state_fwd(local, gcb)
        s5y, carries = _s5_fwd(proj, lam, bblk, cblk)
        xn = _mix_fwd(proj, o, s5y, x, row("head_norm", l, DH), row("ssm_d", l, D), wglu[l], row("b_glu", l, D),
                      wout[l], row("norm_post", l, D))
        saved.append((x, proj, h, qkv, bb, gcb, local, o, states, s5y, carries, gvec, lam, bblk, cblk))
        x = xn

    dx, loss_part = _loss_grad(x, target)

    g = {k: [None] * DEPTH for k in ("wcat", "conv", "wglu", "wout", "norm_pre", "a_log", "dt_bias", "head_norm",
                                     "ssm_c_re", "ssm_c_im", "ssm_d", "b_glu", "norm_post", "lr", "li", "bbr", "bbi")}
    for l in reversed(range(DEPTH)):
        xl, proj, h, qkv, bb, gcb, local, o, states, s5y, carries, gvec, lam, bblk, cblk = saved[l]
        (dza, du_skip, dzb, dra, drb, do, ds5y, dxres, dwg, dwo, dvecs, dhn) = _mix_bwd(
            proj, o, s5y, xl, row("head_norm", l, DH), row("ssm_d", l, D), wglu[l], row("b_glu", l, D), wout[l],
            row("norm_post", l, D), dx)
        du, dlam, dbblk, dcblk = _s5_bwd(proj, lam, bblk, cblk, carries, ds5y, du_skip)
        *dlocal, dgcb_state = _delta_state_bwd(local, gcb, states, do)
        dq, dk, dv, dbb, dgcb = _delta_local_bwd(qkv, bb, gcb, dlocal, dgcb_state)
        dbd, dgvec = _gates_bwd(proj, gvec, dbb, dgcb)
        dqkv = jnp.concatenate([dq, dk, dv], axis=0)
        dpre, dconv = _prep_bwd(proj, conv[l], dqkv)
        dproj = jnp.concatenate([dpre, dza, du, dzb, dra, drb, dbd], axis=1)
        dx, dgain = _inproj_bwd_dx(dproj, wcat[l], xl, row("norm_pre", l, D), dxres)
        g["wcat"][l] = _inproj_bwd_dw(h, dproj)
        g["conv"][l], g["wglu"][l], g["wout"][l] = dconv, dwg, dwo
        g["norm_pre"][l] = dgain[0]
        g["a_log"][l], g["dt_bias"][l] = dgvec[0, NH:2 * NH], dgvec[1, NH:2 * NH]
        g["head_norm"][l] = dhn[0]
        g["ssm_d"][l], g["b_glu"][l], g["norm_post"][l] = dvecs[0], dvecs[1], dvecs[2]
        g["ssm_c_re"][l], g["ssm_c_im"][l] = _block_diag_c_t(dcblk[0]), _block_diag_c_t(dcblk[1])
        g["lr"][l], g["li"][l] = dlam[0].reshape(NG, NS), dlam[1].reshape(NG, NS)
        g["bbr"][l], g["bbi"][l] = _block_diag_b_t(dbblk[0]), _block_diag_b_t(dbblk[1])
    g = {k: jnp.stack(v) for k, v in g.items()}
    dar, dai, dldt, dbr2, dbi2 = _s5_params_bwd(ar, ai, ldt, br2, bi2, g["lr"], g["li"], g["bbr"], g["bbi"])
    g["ssm_a_re"], g["ssm_a_im"], g["ssm_log_dt"] = dar, dai, dldt.reshape(DEPTH, NG)
    g["ssm_b_re"] = dbr2.reshape(DEPTH, NG, NS, GS)
    g["ssm_b_im"] = dbi2.reshape(DEPTH, NG, NS, GS)
    return loss_part[0, 0], dx, g


def kernel(x, norm_pre, w_in, conv_w, a_log, dt_bias, head_norm, ssm_a_re, ssm_a_im, ssm_log_dt, ssm_b_re, ssm_b_im, ssm_c_re, ssm_c_im, ssm_d, w_glu, b_glu, w_out, norm_post, loss_target, m_norm_pre, m_w_in, m_conv_w, m_a_log, m_dt_bias, m_head_norm, m_ssm_a_re, m_ssm_a_im, m_ssm_log_dt, m_ssm_b_re, m_ssm_b_im, m_ssm_c_re, m_ssm_c_im, m_ssm_d, m_w_glu, m_b_glu, m_w_out, m_norm_post, v_norm_pre, v_w_in, v_conv_w, v_a_log, v_dt_bias, v_head_norm, v_ssm_a_re, v_ssm_a_im, v_ssm_log_dt, v_ssm_b_re, v_ssm_b_im, v_ssm_c_re, v_ssm_c_im, v_ssm_d, v_w_glu, v_b_glu, v_w_out, v_norm_post):
    args = dict(locals())
    small = {n: args[n] for n, _ in SMALL}

    shard = _pack_shard(w_in, conv_w, w_glu, w_out)
    gathered = _exchange(shard.astype(BF16), CHIP_RELS, True, False, "gather_weights")
    conv_g = _exchange(_pad_rows(conv_w.reshape(-1), CONV_ROWS), CHIP_RELS, True, False, "gather_conv")
    pieces = [_unpack_shard(gathered[j]) for j in range(NCHIP)]
    wcat = _rearrange_cols(jnp.concatenate([p[0] for p in pieces], axis=2))
    wglu = jnp.concatenate([p[2] for p in pieces], axis=1)
    wout = jnp.concatenate([p[3] for p in pieces], axis=1)
    conv = jnp.concatenate([conv_g[j].reshape(-1)[:DEPTH * 4 * CONV_SHARD].reshape(DEPTH, 4, CONV_SHARD)
                            for j in range(NCHIP)], axis=2)

    loss_part, dx, g = _local_step(x[0], loss_target[0], wcat, conv, wglu, wout, small)
    loss = lax.psum(loss_part, ("x", "y", "c"))

    gw_in = _restore_cols(g["wcat"])
    per_chip = jnp.stack([
        _pack_shard(gw_in[:, :, j * WIN_SHARD:(j + 1) * WIN_SHARD], g["conv"][:, :, j * CONV_SHARD:(j + 1) * CONV_SHARD],
                    g["wglu"][:, j * ROW_SHARD:(j + 1) * ROW_SHARD], g["wout"][:, j * ROW_SHARD:(j + 1) * ROW_SHARD])
        for j in range(NCHIP)]).astype(BF16)
    from_chips = _exchange(per_chip, CHIP_RELS, True, True, "scatter_grads")
    core_sum = _sum_slots(from_chips, "sum_chips")
    other = _sibling_swap(core_sum, "swap_cores")
    gs, ds, nms, nvs = _adamw(shard, jnp.stack([core_sum, other]),
                              _pack_shard(m_w_in, m_conv_w, m_w_glu, m_w_out),
                              _pack_shard(v_w_in, v_conv_w, v_w_glu, v_w_out), "adamw_sharded")

    small_parts = _exchange(_pack_small([g[n] for n, _ in SMALL]), ALL_RELS, False, False, "gather_small")
    gr, dr, nmr, nvr = _adamw(_pack_small([args[n] for n, _ in SMALL]), small_parts,
                              _pack_small([args["m_" + n] for n, _ in SMALL]),
                              _pack_small([args["v_" + n] for n, _ in SMALL]), "adamw_small")

    names = ["norm_pre", "w_in", "conv_w", "a_log", "dt_bias", "head_norm", "ssm_a_re", "ssm_a_im", "ssm_log_dt",
             "ssm_b_re", "ssm_b_im", "ssm_c_re", "ssm_c_im", "ssm_d", "w_glu", "b_glu", "w_out", "norm_post"]
    outs = [loss, dx[None]]
    for flat_s, flat_r in ((gs, gr), (ds, dr), (nms, nmr), (nvs, nvr)):
        sh = dict(zip(("w_in", "conv_w", "w_glu", "w_out"), _unpack_shard(flat_s)))
        sm = dict(zip([n for n, _ in SMALL], _unpack_small(flat_r)))
        outs += [sh[n] if n in sh else sm[n] for n in names]
    return tuple(outs)
```

```python
import functools
import math

import jax
import jax.numpy as jnp
from jax import lax
from jax.experimental import pallas as pl
from jax.experimental.pallas import tpu as pltpu

F32 = jnp.float32
BF16 = jnp.bfloat16
HI = lax.Precision.HIGHEST

D = 1024
NH = 8
DH = 128
CH = 128
NG = 64
GS = 16
NS = 64
GPB = 8
NCB = NG // GPB
SW = GPB * NS
NCOL = 8320
BD0 = 8192
EPS = 1e-6
DEPTH = 4
NCHIP = 4
NDEV = 8
VMEM_LIMIT = 56 * 1024 * 1024

ADAM_LR = 0.001
ADAM_B1 = 0.9
ADAM_B2 = 0.999
ADAM_EPS = 1e-08
ADAM_WD = 0.01
ADAM_STEP = 10


def _cparams(sem=None):
    return pltpu.CompilerParams(dimension_semantics=sem, vmem_limit_bytes=VMEM_LIMIT)


def _full(shape):
    nd = len(shape)
    return pl.BlockSpec(shape, lambda *_: (0,) * nd)


def _rms(x, gain):
    ms = jnp.mean(x * x, axis=-1, keepdims=True)
    return x * lax.rsqrt(ms + EPS) * gain


def _sigmoid(x):
    return 1.0 / (1.0 + jnp.exp(-x))


def _silu(x):
    return x * _sigmoid(x)


def _softplus(x):
    return jnp.maximum(x, 0.0) + jnp.log(1.0 + jnp.exp(-jnp.abs(x)))


def _gelu(x):
    return 0.5 * x * (1.0 + jnp.tanh(math.sqrt(2.0 / math.pi) * (x + 0.044715 * (x * x * x))))


def _dot_bf16(a, b, dims):
    return lax.dot_general(a.astype(BF16), b.astype(BF16), (dims, ((), ())), preferred_element_type=F32)


def _mm_nt(a, b):
    return _dot_bf16(a, b, ((1,), (1,)))


def _mm_tn(a, b):
    return _dot_bf16(a, b, ((0,), (0,)))


@jax.custom_vjp
def _mm(a, b):
    return _dot_bf16(a, b, ((1,), (0,)))


def _mm_fwd(a, b):
    return _dot_bf16(a, b, ((1,), (0,))), (a, b)


def _mm_bwd(res, ct):
    a, b = res
    return _mm_nt(ct, b).astype(a.dtype), _mm_tn(a, ct).astype(b.dtype)


_mm.defvjp(_mm_fwd, _mm_bwd)


@jax.custom_vjp
def _mm_nt_d(a, b):
    return _mm_nt(a, b)


def _mm_nt_d_bwd(res, ct):
    a, b = res
    return _dot_bf16(ct, b, ((1,), (0,))), _mm_tn(ct, a)


_mm_nt_d.defvjp(lambda a, b: (_mm_nt(a, b), (a, b)), _mm_nt_d_bwd)


@jax.custom_vjp
def _mm_tn_d(a, b):
    return _mm_tn(a, b)


def _mm_tn_d_bwd(res, ct):
    a, b = res
    return _mm_nt(b, ct), _dot_bf16(a, ct, ((1,), (0,)))


_mm_tn_d.defvjp(lambda a, b: (_mm_tn(a, b), (a, b)), _mm_tn_d_bwd)


def _split_bf16(a):
    hi = a.astype(BF16)
    return hi, (a - hi.astype(F32)).astype(BF16)


def _dot3(a, b, dims):
    ah, al = _split_bf16(a)
    bh, bl = _split_bf16(b)

    def dot(x, y):
        return lax.dot_general(x, y, (dims, ((), ())), preferred_element_type=F32)

    return dot(ah, bh) + (dot(ah, bl) + dot(al, bh))


@jax.custom_vjp
def _imm(a, b):
    return _dot3(a, b, ((1,), (0,)))


def _imm_bwd(res, ct):
    a, b = res
    return _dot3(ct, b, ((1,), (1,))), _dot3(a, ct, ((0,), (0,)))


_imm.defvjp(lambda a, b: (_dot3(a, b, ((1,), (0,))), (a, b)), _imm_bwd)


def _hmm(a, b):
    return jnp.dot(a, b, precision=HI, preferred_element_type=F32)


def _hmm_nt(a, b):
    return lax.dot_general(a, b, (((1,), (1,)), ((), ())), precision=HI, preferred_element_type=F32)


def _hmm_tn(a, b):
    return lax.dot_general(a, b, (((0,), (0,)), ((), ())), precision=HI, preferred_element_type=F32)


def _rows(shape):
    return lax.broadcasted_iota(jnp.int32, shape, 0)


def _cols(shape):
    return lax.broadcasted_iota(jnp.int32, shape, 1)


def _sd(x, s):
    return jnp.where(_rows(x.shape) >= s, pltpu.roll(x, s, axis=0), 0.0)


def _su(x, s):
    n = x.shape[0]
    return jnp.where(_rows(x.shape) < n - s, pltpu.roll(x, n - s, axis=0), 0.0)


@functools.partial(jax.custom_vjp, nondiff_argnums=(1,))
def _shift_down(x, s):
    return _sd(x, s)


def _shift_down_fwd(x, s):
    return _sd(x, s), None


def _shift_down_bwd(s, _, g):
    return (_su(g, s),)


_shift_down.defvjp(_shift_down_fwd, _shift_down_bwd)


def _last_row(x):
    n = x.shape[0]
    return jnp.sum(jnp.where(_rows(x.shape) == n - 1, x, 0.0), axis=0, keepdims=True)


def _prep_fn(p, w0, w1, w2, w3, qk):
    acc = w3 * p + w2 * _shift_down(p, 1) + w1 * _shift_down(p, 2) + w0 * _shift_down(p, 3)
    a = _silu(acc)
    nrm = lax.rsqrt(jnp.sum(a * a, axis=-1, keepdims=True) + EPS)
    return a * (nrm * qk + (1.0 - qk))


def _gates_fn(bd, av, bv):
    tm = bd.shape[0]
    beta_all = _sigmoid(bd)
    g_all = -jnp.exp(av) * _softplus(bd + bv)
    r, c = _rows((tm, tm)), _cols((tm, tm))
    tri = jnp.where((r // CH == c // CH) & (r >= c), 1.0, 0.0).astype(F32)
    gc_all = _hmm(tri, g_all)
    lane = _cols(bd.shape)
    outs = []
    for h in range(NH):
        b = jnp.sum(jnp.where(lane == h, beta_all, 0.0), axis=1, keepdims=True)
        outs.append(jnp.broadcast_to(b, bd.shape))
    for h in range(NH):
        g = jnp.sum(jnp.where(lane == NH + h, gc_all, 0.0), axis=1, keepdims=True)
        outs.append(jnp.broadcast_to(g, bd.shape))
    return tuple(outs)


def _unit_lower_inv(l_mat):
    n = l_mat.shape[0]
    eye = jnp.where(_rows((n, n)) == _cols((n, n)), 1.0, 0.0).astype(F32)
    p = -l_mat
    r = eye + p
    k = 1
    while 2 * k < n:
        p = _imm(p, p)
        r = r + _imm(r, p)
        k *= 2
    return r


@jax.custom_vjp
def _known_inverse(l_mat, t_inv):
    return t_inv


def _known_inverse_bwd(t_inv, ct):
    d_l = -_dot3(_dot3(t_inv, ct, ((0,), (0,))), t_inv, ((1,), (1,)))
    return d_l, jnp.zeros_like(t_inv)


_known_inverse.defvjp(lambda l_mat, t_inv: (t_inv, t_inv), _known_inverse_bwd)


def _chunk_local(q, k, v, bb, gcb, t_inv=None):
    qs = q * (DH ** -0.5)
    kb = k * bb
    eg = jnp.exp(gcb)
    ii, jj = _rows((CH, CH)), _cols((CH, CH))
    decay = jnp.exp(jnp.where(ii >= jj, gcb - gcb.T, -1e30))
    l_mat = jnp.where(ii > jj, _mm_nt_d(kb, k) * decay, 0.0)
    t_inv = _unit_lower_inv(l_mat) if t_inv is None else _known_inverse(l_mat, t_inv)
    u = _mm(t_inv, v * bb)
    w = _mm(t_inv, kb * eg)
    a_qk = _mm_nt_d(qs, k) * decay
    k_dec = k * jnp.exp(_last_row(gcb) - gcb)
    return (u, w, qs * eg, k_dec, a_qk), t_inv


def _state_step(u, w, q_dec, k_dec, a_qk, gcb, state):
    v_new = u - _mm(w, state)
    o = _mm(q_dec, state) + _mm(a_qk, v_new)
    new_state = state * jnp.exp(_last_row(gcb)) + _mm_tn_d(k_dec, v_new)
    return o, new_state


def _s5_tile(u, cr, ci, lr, li, bbr, bbi, ccr, cci):
    xr = _mm(u, bbr)
    xi = _mm(u, bbi)
    first = _rows(xr.shape) == 0
    hr = xr + jnp.where(first, lr * cr - li * ci, 0.0)
    hi = xi + jnp.where(first, lr * ci + li * cr, 0.0)
    pr, pi = lr, li
    s = 1
    while s < u.shape[0]:
        sr, si = _shift_down(hr, s), _shift_down(hi, s)
        hr, hi = hr + pr * sr - pi * si, hi + pr * si + pi * sr
        pr, pi = pr * pr - pi * pi, 2.0 * pr * pi
        s *= 2
    y = _mm(hr, ccr) - _mm(hi, cci)
    return y, _last_row(hr), _last_row(hi)


def _s5_params_fn(ar, ai, ldt, br2, bi2):
    dt = jnp.exp(ldt)
    mag = jnp.exp(ar * dt)
    lr, li = mag * jnp.cos(ai * dt), mag * jnp.sin(ai * dt)
    den = ar * ar + ai * ai
    fr = ((lr - 1.0) * ar + li * ai) / den
    fi = (li * ar - (lr - 1.0) * ai) / den
    expand = jnp.where(_cols((NS, NS * GS)) // GS == _rows((NS, NS * GS)), 1.0, 0.0).astype(F32)
    fr2, fi2 = _hmm(fr, expand), _hmm(fi, expand)
    return lr, li, fr2 * br2 - fi2 * bi2, fr2 * bi2 + fi2 * br2


def _head_norm(o, hn):
    parts = []
    for h in range(NH):
        oh = o[:, h * DH:(h + 1) * DH]
        parts.append(oh * lax.rsqrt(jnp.mean(oh * oh, axis=-1, keepdims=True) + EPS) * hn)
    return jnp.concatenate(parts, axis=1)


def _mix_pre(s5y, u, dvec):
    return _gelu(s5y + dvec * u)


def _mix_mid(o, za, y0, gl, zb, ra, rb, hn):
    ya = _head_norm(o, hn) * _silu(za)
    yb = y0 * _sigmoid(gl) * _silu(zb)
    return _sigmoid(ra) * ya + _sigmoid(rb) * yb


def _mix_post(x, out, npost):
    return x + _rms(out, npost)


def _tile(t, want):
    return min(t, want)


def _row_tile(rows, want):
    return max(r for r in range(16, want + 1, 16) if rows % r == 0)


def _inproj_fwd(x, gain, wcat, l):
    t = x.shape[0]
    tm, tn = _tile(t, 512), 640

    def body(x_ref, g_ref, w_ref, o_ref, h_ref):
        @pl.when(pl.program_id(1) == 0)
        def _():
            h_ref[...] = _rms(x_ref[...], g_ref[...]).astype(h_ref.dtype)
        o_ref[...] = _dot_bf16(h_ref[...], w_ref[...], ((1,), (0,)))

    return pl.pallas_call(
        body, name="inproj_fwd", grid=(t // tm, NCOL // tn),
        in_specs=[pl.BlockSpec((tm, D), lambda i, j: (i, 0)), _full((1, D)),
                  pl.BlockSpec((None, D, tn), lambda i, j: (l, 0, j))],
        out_specs=[pl.BlockSpec((tm, tn), lambda i, j: (i, j)), pl.BlockSpec((tm, D), lambda i, j: (i, 0))],
        out_shape=[jax.ShapeDtypeStruct((t, NCOL), F32), jax.ShapeDtypeStruct((t, D), wcat.dtype)],
        compiler_params=_cparams(("parallel", "arbitrary")),
    )(x, gain, wcat)


def _inproj_bwd_dx(dproj, wcat, x, gain, dxres, l):
    t = x.shape[0]
    tm, tk = _tile(t, 512), 640
    nk = NCOL // tk

    def body(dp_ref, w_ref, x_ref, g_ref, r_ref, dx_ref, dg_ref, acc_ref):
        i, k = pl.program_id(0), pl.program_id(1)

        @pl.when(k == 0)
        def _():
            acc_ref[...] = jnp.zeros_like(acc_ref)

        acc_ref[...] += _mm_nt(dp_ref[...], w_ref[...])

        @pl.when(k == nk - 1)
        def _():
            _, vjp = jax.vjp(_rms, x_ref[...], g_ref[...])
            dx, dg = vjp(acc_ref[...])
            dx_ref[...] = r_ref[...] + dx

            @pl.when(i == 0)
            def _():
                dg_ref[...] = dg

            @pl.when(i > 0)
            def _():
                dg_ref[...] += dg

    return pl.pallas_call(
        body, name="inproj_bwd_dx", grid=(t // tm, nk),
        in_specs=[pl.BlockSpec((tm, tk), lambda i, k: (i, k)), pl.BlockSpec((None, D, tk), lambda i, k: (l, 0, k)),
                  pl.BlockSpec((tm, D), lambda i, k: (i, 0)), _full((1, D)),
                  pl.BlockSpec((tm, D), lambda i, k: (i, 0))],
        out_specs=[pl.BlockSpec((tm, D), lambda i, k: (i, 0)), _full((1, D))],
        out_shape=[jax.ShapeDtypeStruct((t, D), F32), jax.ShapeDtypeStruct((1, D), F32)],
        scratch_shapes=[pltpu.VMEM((tm, D), F32)],
        compiler_params=_cparams(("arbitrary", "arbitrary")),
    )(dproj, wcat, x, gain, dxres)


def _inproj_bwd_dw(h, dproj):
    t = h.shape[0]
    tm, tn = _tile(t, 512), 640

    def body(h_ref, dp_ref, o_ref):
        @pl.when(pl.program_id(1) == 0)
        def _():
            o_ref[...] = jnp.zeros_like(o_ref)

        o_ref[...] += _mm_tn(h_ref[...], dp_ref[...])

    return pl.pallas_call(
        body, name="inproj_bwd_dw", grid=(NCOL // tn, t // tm),
        in_specs=[pl.BlockSpec((tm, D), lambda j, i: (i, 0)), pl.BlockSpec((tm, tn), lambda j, i: (i, j))],
        out_specs=pl.BlockSpec((D, tn), lambda j, i: (0, j)),
        out_shape=jax.ShapeDtypeStruct((D, NCOL), F32),
        compiler_params=_cparams(("parallel", "arbitrary")),
    )(h, dproj)


def _prep_fwd(proj, cw):
    t = proj.shape[0]

    def body(p_ref, w_ref, o_ref):
        qk = (pl.program_id(0) < 2 * NH).astype(F32)
        o_ref[...] = _prep_fn(p_ref[...], w_ref[0:1, :], w_ref[1:2, :], w_ref[2:3, :], w_ref[3:4, :], qk)

    return pl.pallas_call(
        body, name="prep_fwd", grid=(3 * NH,),
        in_specs=[pl.BlockSpec((t, DH), lambda c: (0, c)), pl.BlockSpec((4, DH), lambda c: (0, c))],
        out_specs=pl.BlockSpec((None, t, DH), lambda c: (c, 0, 0)),
        out_shape=jax.ShapeDtypeStruct((3 * NH, t, DH), F32),
        compiler_params=_cparams(("parallel",)),
    )(proj, cw)


def _prep_bwd(proj, cw, dqkv):
    t = proj.shape[0]

    def body(p_ref, w_ref, d_ref, dp_ref, dw_ref):
        qk = (pl.program_id(0) < 2 * NH).astype(F32)
        _, vjp = jax.vjp(lambda p, w0, w1, w2, w3: _prep_fn(p, w0, w1, w2, w3, qk),
                         p_ref[...], w_ref[0:1, :], w_ref[1:2, :], w_ref[2:3, :], w_ref[3:4, :])
        dp, dw0, dw1, dw2, dw3 = vjp(d_ref[...])
        dp_ref[...] = dp
        dw_ref[0:1, :] = dw0
        dw_ref[1:2, :] = dw1
        dw_ref[2:3, :] = dw2
        dw_ref[3:4, :] = dw3

    return pl.pallas_call(
        body, name="prep_bwd", grid=(3 * NH,),
        in_specs=[pl.BlockSpec((t, DH), lambda c: (0, c)), pl.BlockSpec((4, DH), lambda c: (0, c)),
                  pl.BlockSpec((None, t, DH), lambda c: (c, 0, 0))],
        out_specs=[pl.BlockSpec((t, DH), lambda c: (0, c)), pl.BlockSpec((4, DH), lambda c: (0, c))],
        out_shape=[jax.ShapeDtypeStruct((t, 3 * D), F32), jax.ShapeDtypeStruct((4, 3 * D), F32)],
        compiler_params=_cparams(("parallel",)),
    )(proj, cw, dqkv)


def _gates_fwd(proj, gvec):
    t = proj.shape[0]
    tm = _tile(t, 512)

    def body(p_ref, gv_ref, b_ref, g_ref):
        outs = _gates_fn(p_ref[...], gv_ref[0:1, :], gv_ref[1:2, :])
        for h in range(NH):
            b_ref[h] = outs[h]
            g_ref[h] = outs[NH + h]

    spec = pl.BlockSpec((NH, tm, DH), lambda i: (0, i, 0))
    return pl.pallas_call(
        body, name="gates_fwd", grid=(t // tm,),
        in_specs=[pl.BlockSpec((tm, DH), lambda i: (i, BD0 // DH)), _full((8, DH))],
        out_specs=[spec, spec],
        out_shape=[jax.ShapeDtypeStruct((NH, t, DH), F32)] * 2,
        compiler_params=_cparams(("parallel",)),
    )(proj, gvec)


def _gates_bwd(proj, gvec, dbb, dgcb):
    t = proj.shape[0]
    tm = _tile(t, 512)

    def body(p_ref, gv_ref, db_ref, dg_ref, dp_ref, dgv_ref):
        _, vjp = jax.vjp(_gates_fn, p_ref[...], gv_ref[0:1, :], gv_ref[1:2, :])
        cts = tuple(db_ref[h] for h in range(NH)) + tuple(dg_ref[h] for h in range(NH))
        dp, da, db = vjp(cts)
        dp_ref[...] = dp

        @pl.when(pl.program_id(0) == 0)
        def _():
            dgv_ref[...] = jnp.zeros_like(dgv_ref)

        dgv_ref[0:1, :] += da
        dgv_ref[1:2, :] += db

    spec = pl.BlockSpec((NH, tm, DH), lambda i: (0, i, 0))
    return pl.pallas_call(
        body, name="gates_bwd", grid=(t // tm,),
        in_specs=[pl.BlockSpec((tm, DH), lambda i: (i, BD0 // DH)), _full((8, DH)), spec, spec],
        out_specs=[pl.BlockSpec((tm, DH), lambda i: (i, 0)), _full((8, DH))],
        out_shape=[jax.ShapeDtypeStruct((t, DH), F32), jax.ShapeDtypeStruct((8, DH), F32)],
        compiler_params=_cparams(("arbitrary",)),
    )(proj, gvec, dbb, dgcb)


def _chunks_per_step(nch):
    return 2 if nch % 2 == 0 else 1


def _delta_local_fwd(qkv, bb, gcb):
    t = qkv.shape[1]
    cps = _chunks_per_step(t // CH)
    rows = cps * CH

    def body(q_ref, k_ref, v_ref, b_ref, g_ref, *out_refs):
        for c in range(cps):
            sl = slice(c * CH, (c + 1) * CH)
            outs, t_inv = _chunk_local(q_ref[sl, :], k_ref[sl, :], v_ref[sl, :], b_ref[sl, :], g_ref[sl, :])
            for ref, val in zip(out_refs, outs + (t_inv,)):
                ref[sl, :] = val

    def blk(off):
        return pl.BlockSpec((None, rows, DH), lambda h, n: (h + off, n, 0))

    outs = pl.pallas_call(
        body, name="delta_local_fwd", grid=(NH, t // rows),
        in_specs=[blk(0), blk(NH), blk(2 * NH), blk(0), blk(0)],
        out_specs=[blk(0)] * 6,
        out_shape=[jax.ShapeDtypeStruct((NH, t, DH), F32)] * 6,
        compiler_params=_cparams(("parallel", "parallel")),
    )(qkv, qkv, qkv, bb, gcb)
    return outs[:5], outs[5]


def _delta_local_bwd(qkv, bb, gcb, t_inv, cts, dgcb_state):
    t = qkv.shape[1]
    cps = _chunks_per_step(t // CH)
    rows = cps * CH

    def body(q_ref, k_ref, v_ref, b_ref, g_ref, ti_ref, du_ref, dw_ref, dqd_ref, dkd_ref, da_ref, dgs_ref,
             dq_ref, dk_ref, dv_ref, db_ref, dg_ref):
        for c in range(cps):
            sl = slice(c * CH, (c + 1) * CH)
            t_inv_c = ti_ref[sl, :]
            _, vjp = jax.vjp(lambda *a: _chunk_local(*a, t_inv=t_inv_c)[0],
                             q_ref[sl, :], k_ref[sl, :], v_ref[sl, :], b_ref[sl, :], g_ref[sl, :])
            dq, dk, dv, db, dg = vjp((du_ref[sl, :], dw_ref[sl, :], dqd_ref[sl, :], dkd_ref[sl, :], da_ref[sl, :]))
            dq_ref[sl, :] = dq
            dk_ref[sl, :] = dk
            dv_ref[sl, :] = dv
            db_ref[sl, :] = db
            dg_ref[sl, :] = dg + dgs_ref[sl, :]

    def blk(off):
        return pl.BlockSpec((None, rows, DH), lambda h, n: (h + off, n, 0))

    return pl.pallas_call(
        body, name="delta_local_bwd", grid=(NH, t // rows),
        in_specs=[blk(0), blk(NH), blk(2 * NH)] + [blk(0)] * 9,
        out_specs=[blk(0)] * 5,
        out_shape=[jax.ShapeDtypeStruct((NH, t, DH), F32)] * 5,
        compiler_params=_cparams(("parallel", "parallel")),
    )(qkv, qkv, qkv, bb, gcb, t_inv, *cts, dgcb_state)


def _delta_state_fwd(local, gcb):
    t = gcb.shape[1]
    nch = t // CH

    def body(u_ref, w_ref, qd_ref, kd_ref, a_ref, g_ref, o_ref, s_ref, st_ref):
        @pl.when(pl.program_id(0) == 0)
        def _():
            st_ref[...] = jnp.zeros_like(st_ref)

        for h in range(NH):
            s_ref[h] = st_ref[h]
            o, ns = _state_step(u_ref[h], w_ref[h], qd_ref[h], kd_ref[h], a_ref[h], g_ref[h], st_ref[h])
            o_ref[:, h * DH:(h + 1) * DH] = o
            st_ref[h] = ns

    blk = pl.BlockSpec((NH, CH, DH), lambda n: (0, n, 0))
    return pl.pallas_call(
        body, name="delta_state_fwd", grid=(nch,),
        in_specs=[blk] * 6,
        out_specs=[pl.BlockSpec((CH, D), lambda n: (n, 0)),
                   pl.BlockSpec((NH, None, DH, DH), lambda n: (0, n, 0, 0))],
        out_shape=[jax.ShapeDtypeStruct((t, D), F32), jax.ShapeDtypeStruct((NH, nch, DH, DH), F32)],
        scratch_shapes=[pltpu.VMEM((NH, DH, DH), F32)],
        compiler_params=_cparams(("arbitrary",)),
    )(*local, gcb)


def _delta_state_bwd(local, gcb, states, do):
    t = gcb.shape[1]
    nch = t // CH

    def body(u_ref, w_ref, qd_ref, kd_ref, a_ref, g_ref, s_ref, do_ref,
             du_ref, dw_ref, dqd_ref, dkd_ref, da_ref, dg_ref, ds_ref):
        @pl.when(pl.program_id(0) == 0)
        def _():
            ds_ref[...] = jnp.zeros_like(ds_ref)

        for h in range(NH):
            _, vjp = jax.vjp(_state_step, u_ref[h], w_ref[h], qd_ref[h], kd_ref[h], a_ref[h], g_ref[h], s_ref[h])
            du, dw, dqd, dkd, da, dg, ds = vjp((do_ref[:, h * DH:(h + 1) * DH], ds_ref[h]))
            du_ref[h] = du
            dw_ref[h] = dw
            dqd_ref[h] = dqd
            dkd_ref[h] = dkd
            da_ref[h] = da
            dg_ref[h] = dg
            ds_ref[h] = ds

    blk = pl.BlockSpec((NH, CH, DH), lambda n: (0, nch - 1 - n, 0))
    return pl.pallas_call(
        body, name="delta_state_bwd", grid=(nch,),
        in_specs=[blk] * 6 + [pl.BlockSpec((NH, None, DH, DH), lambda n: (0, nch - 1 - n, 0, 0)),
                              pl.BlockSpec((CH, D), lambda n: (nch - 1 - n, 0))],
        out_specs=[blk] * 6,
        out_shape=[jax.ShapeDtypeStruct((NH, t, DH), F32)] * 6,
        scratch_shapes=[pltpu.VMEM((NH, DH, DH), F32)],
        compiler_params=_cparams(("arbitrary",)),
    )(*local, gcb, states, do)


def _s5_params(ar, ai, ldt, br2, bi2):
    def body(ar_ref, ai_ref, ld_ref, br_ref, bi_ref, lr_ref, li_ref, bbr_ref, bbi_ref):
        lr, li, bbr, bbi = _s5_params_fn(ar_ref[...], ai_ref[...], ld_ref[...], br_ref[...], bi_ref[...])
        lr_ref[...] = lr
        li_ref[...] = li
        bbr_ref[...] = bbr
        bbi_ref[...] = bbi

    sq = pl.BlockSpec((None, NG, NS), lambda l: (l, 0, 0))
    wide = pl.BlockSpec((None, NG, NS * GS), lambda l: (l, 0, 0))
    return pl.pallas_call(
        body, name="s5_params", grid=(DEPTH,),
        in_specs=[sq, sq, pl.BlockSpec((None, NG, 1), lambda l: (l, 0, 0)), wide, wide],
        out_specs=[sq, sq, wide, wide],
        out_shape=[jax.ShapeDtypeStruct((DEPTH, NG, NS), F32)] * 2
        + [jax.ShapeDtypeStruct((DEPTH, NG, NS * GS), F32)] * 2,
        compiler_params=_cparams(("parallel",)),
    )(ar, ai, ldt, br2, bi2)


def _s5_params_bwd(ar, ai, ldt, br2, bi2, dlr, dli, dbbr, dbbi):
    def body(ar_ref, ai_ref, ld_ref, br_ref, bi_ref, a_ref, b_ref, c_ref, d_ref,
             dar_ref, dai_ref, dld_ref, dbr_ref, dbi_ref):
        _, vjp = jax.vjp(_s5_params_fn, ar_ref[...], ai_ref[...], ld_ref[...], br_ref[...], bi_ref[...])
        dar, dai, dld, dbr, dbi = vjp((a_ref[...], b_ref[...], c_ref[...], d_ref[...]))
        dar_ref[...] = dar
        dai_ref[...] = dai
        dld_ref[...] = dld
        dbr_ref[...] = dbr
        dbi_ref[...] = dbi

    sq = pl.BlockSpec((None, NG, NS), lambda l: (l, 0, 0))
    col = pl.BlockSpec((None, NG, 1), lambda l: (l, 0, 0))
    wide = pl.BlockSpec((None, NG, NS * GS), lambda l: (l, 0, 0))
    return pl.pallas_call(
        body, name="s5_params_bwd", grid=(DEPTH,),
        in_specs=[sq, sq, col, wide, wide, sq, sq, wide, wide],
        out_specs=[sq, sq, col, wide, wide],
        out_shape=[jax.ShapeDtypeStruct((DEPTH, NG, NS), F32)] * 2 + [jax.ShapeDtypeStruct((DEPTH, NG, 1), F32)]
        + [jax.ShapeDtypeStruct((DEPTH, NG, NS * GS), F32)] * 2,
        compiler_params=_cparams(("parallel",)),
    )(ar, ai, ldt, br2, bi2, dlr, dli, dbbr, dbbi)


def _s5_tile_rows(t):
    return _tile(t // 2, 256)


def _s5_fwd(proj, lam, bblk, cblk):
    t = proj.shape[0]
    r = _s5_tile_rows(t)
    nt = t // r
    u0 = 4 * D // DH

    def body(u_ref, lam_ref, b_ref, c_ref, y_ref, car_ref, st_ref):
        @pl.when(pl.program_id(1) == 0)
        def _():
            st_ref[...] = jnp.zeros_like(st_ref)

        car_ref[...] = st_ref[...]
        y, cr, ci = _s5_tile(u_ref[...], st_ref[0:1, :], st_ref[1:2, :], lam_ref[0], lam_ref[1],
                             b_ref[0], b_ref[1], c_ref[0], c_ref[1])
        y_ref[...] = y
        st_ref[0:1, :] = cr
        st_ref[1:2, :] = ci

    return pl.pallas_call(
        body, name="s5_fwd", grid=(NCB, nt),
        in_specs=[pl.BlockSpec((r, DH), lambda c, i: (i, u0 + c)),
                  pl.BlockSpec((2, 1, SW), lambda c, i: (0, 0, c)),
                  pl.BlockSpec((2, None, DH, SW), lambda c, i: (0, c, 0, 0)),
                  pl.BlockSpec((2, None, SW, DH), lambda c, i: (0, c, 0, 0))],
        out_specs=[pl.BlockSpec((r, DH), lambda c, i: (i, c)),
                   pl.BlockSpec((None, 8, SW), lambda c, i: (i, 0, c))],
        out_shape=[jax.ShapeDtypeStruct((t, D), F32), jax.ShapeDtypeStruct((nt, 8, NG * NS), F32)],
        scratch_shapes=[pltpu.VMEM((8, SW), F32)],
        compiler_params=_cparams(("parallel", "arbitrary")),
    )(proj, lam, bblk, cblk)


def _s5_bwd(proj, lam, bblk, cblk, carries, dy, du_skip):
    t = proj.shape[0]
    r = _s5_tile_rows(t)
    nt = t // r
    u0 = 4 * D // DH

    def body(u_ref, lam_ref, b_ref, c_ref, car_ref, dy_ref, dus_ref, du_ref, dlam_ref, db_ref, dc_ref, dst_ref):
        first = pl.program_id(1) == 0

        @pl.when(first)
        def _():
            dst_ref[...] = jnp.zeros_like(dst_ref)

        _, vjp = jax.vjp(_s5_tile, u_ref[...], car_ref[0:1, :], car_ref[1:2, :], lam_ref[0], lam_ref[1],
                         b_ref[0], b_ref[1], c_ref[0], c_ref[1])
        du, dcr, dci, dlr, dli, dbr, dbi, dcr2, dci2 = vjp((dy_ref[...], dst_ref[0:1, :], dst_ref[1:2, :]))
        du_ref[...] = du + dus_ref[...]
        dst_ref[0:1, :] = dcr
        dst_ref[1:2, :] = dci

        @pl.when(first)
        def _():
            dlam_ref[0] = dlr
            dlam_ref[1] = dli
            db_ref[0] = dbr
            db_ref[1] = dbi
            dc_ref[0] = dcr2
            dc_ref[1] = dci2

        @pl.when(jnp.logical_not(first))
        def _():
            dlam_ref[0] += dlr
            dlam_ref[1] += dli
            db_ref[0] += dbr
            db_ref[1] += dbi
            dc_ref[0] += dcr2
            dc_ref[1] += dci2

    return pl.pallas_call(
        body, name="s5_bwd", grid=(NCB, nt),
        in_specs=[pl.BlockSpec((r, DH), lambda c, i: (nt - 1 - i, u0 + c)),
                  pl.BlockSpec((2, 1, SW), lambda c, i: (0, 0, c)),
                  pl.BlockSpec((2, None, DH, SW), lambda c, i: (0, c, 0, 0)),
                  pl.BlockSpec((2, None, SW, DH), lambda c, i: (0, c, 0, 0)),
                  pl.BlockSpec((None, 8, SW), lambda c, i: (nt - 1 - i, 0, c)),
                  pl.BlockSpec((r, DH), lambda c, i: (nt - 1 - i, c)),
                  pl.BlockSpec((r, DH), lambda c, i: (nt - 1 - i, c))],
        out_specs=[pl.BlockSpec((r, DH), lambda c, i: (nt - 1 - i, c)),
                   pl.BlockSpec((2, 1, SW), lambda c, i: (0, 0, c)),
                   pl.BlockSpec((2, None, DH, SW), lambda c, i: (0, c, 0, 0)),
                   pl.BlockSpec((2, None, SW, DH), lambda c, i: (0, c, 0, 0))],
        out_shape=[jax.ShapeDtypeStruct((t, D), F32), jax.ShapeDtypeStruct((2, 1, NG * NS), F32),
                   jax.ShapeDtypeStruct((2, NCB, DH, SW), F32), jax.ShapeDtypeStruct((2, NCB, SW, DH), F32)],
        scratch_shapes=[pltpu.VMEM((8, SW), F32)],
        compiler_params=_cparams(("parallel", "arbitrary")),
    )(proj, lam, bblk, cblk, carries, dy, du_skip)


def _proj_spec(tm, col):
    return pl.BlockSpec((tm, D), lambda i: (i, col))


def _layer_mat(l):
    return pl.BlockSpec((None, D, D), lambda i: (l, 0, 0))


def _mix_fwd(proj, o, s5y, x, hn, dvec, wglu, bglu, wout, npost, l):
    t = x.shape[0]
    tm = _tile(t, 256)

    def body(za_ref, u_ref, zb_ref, ra_ref, rb_ref, o_ref, y_ref, x_ref, hn_ref, d_ref, wg_ref, bg_ref, wo_ref,
             np_ref, xn_ref):
        y0 = _mix_pre(y_ref[...], u_ref[...], d_ref[...])
        gl = _mm(y0, wg_ref[...]) + bg_ref[...]
        m = _mix_mid(o_ref[...], za_ref[...], y0, gl, zb_ref[...], ra_ref[...], rb_ref[...], hn_ref[...])
        out = _mm(m, wo_ref[...])
        xn_ref[...] = _mix_post(x_ref[...], out, np_ref[...])

    act = pl.BlockSpec((tm, D), lambda i: (i, 0))
    return pl.pallas_call(
        body, name="mix_fwd", grid=(t // tm,),
        in_specs=[_proj_spec(tm, 3), _proj_spec(tm, 4), _proj_spec(tm, 5), _proj_spec(tm, 6), _proj_spec(tm, 7),
                  act, act, act, _full((1, DH)), _full((1, D)), _layer_mat(l), _full((1, D)), _layer_mat(l),
                  _full((1, D))],
        out_specs=act,
        out_shape=jax.ShapeDtypeStruct((t, D), F32),
        compiler_params=_cparams(("parallel",)),
    )(proj, proj, proj, proj, proj, o, s5y, x, hn, dvec, wglu, bglu, wout, npost)


def _mix_bwd(proj, o, s5y, x, hn, dvec, wglu, bglu, wout, npost, dxn, l):
    t = x.shape[0]
    tm = _tile(t, 128)

    def body(za_ref, u_ref, zb_ref, ra_ref, rb_ref, o_ref, y_ref, x_ref, hn_ref, d_ref, wg_ref, bg_ref, wo_ref,
             np_ref, dxn_ref,
             dza_ref, du_ref, dzb_ref, dra_ref, drb_ref, do_ref, dy_ref, dx_ref,
             dwg_ref, dwo_ref, dvecs_ref, dhn_ref):
        y0, vjp_pre = jax.vjp(_mix_pre, y_ref[...], u_ref[...], d_ref[...])
        gl = _mm(y0, wg_ref[...]) + bg_ref[...]
        m, vjp_mid = jax.vjp(_mix_mid, o_ref[...], za_ref[...], y0, gl, zb_ref[...], ra_ref[...], rb_ref[...],
                             hn_ref[...])
        out = _mm(m, wo_ref[...])
        _, vjp_post = jax.vjp(_mix_post, x_ref[...], out, np_ref[...])
        dx, dout, dnp = vjp_post(dxn_ref[...])
        dm = _mm_nt(dout, wo_ref[...])
        dwo = _mm_tn(m, dout)
        do, dza, dy0, dgl, dzb, dra, drb, dhn = vjp_mid(dm)
        dwg = _mm_tn(y0, dgl)
        dbg = jnp.sum(dgl, axis=0, keepdims=True)
        dy0 = dy0 + _mm_nt(dgl, wg_ref[...])
        dy, du, dd = vjp_pre(dy0)
        dza_ref[...] = dza
        du_ref[...] = du
        dzb_ref[...] = dzb
        dra_ref[...] = dra
        drb_ref[...] = drb
        do_ref[...] = do
        dy_ref[...] = dy
        dx_ref[...] = dx
        first = pl.program_id(0) == 0

        @pl.when(first)
        def _():
            dwg_ref[...] = dwg
            dwo_ref[...] = dwo
            dvecs_ref[...] = jnp.zeros_like(dvecs_ref)
            dhn_ref[...] = jnp.zeros_like(dhn_ref)

        @pl.when(jnp.logical_not(first))
        def _():
            dwg_ref[...] += dwg
            dwo_ref[...] += dwo

        dvecs_ref[0:1, :] += dd
        dvecs_ref[1:2, :] += dbg
        dvecs_ref[2:3, :] += dnp
        dhn_ref[0:1, :] += dhn

    act = pl.BlockSpec((tm, D), lambda i: (i, 0))
    a = jax.ShapeDtypeStruct((t, D), F32)
    w = jax.ShapeDtypeStruct((D, D), F32)
    return pl.pallas_call(
        body, name="mix_bwd", grid=(t // tm,),
        in_specs=[_proj_spec(tm, 3), _proj_spec(tm, 4), _proj_spec(tm, 5), _proj_spec(tm, 6), _proj_spec(tm, 7),
                  act, act, act, _full((1, DH)), _full((1, D)), _layer_mat(l), _full((1, D)), _layer_mat(l),
                  _full((1, D)), act],
        out_specs=[act] * 8 + [_full((D, D)), _full((D, D)), _full((8, D)), _full((8, DH))],
        out_shape=[a] * 8 + [w, w, jax.ShapeDtypeStruct((8, D), F32), jax.ShapeDtypeStruct((8, DH), F32)],
        compiler_params=_cparams(("arbitrary",)),
    )(proj, proj, proj, proj, proj, o, s5y, x, hn, dvec, wglu, bglu, wout, npost, dxn)


def _loss_grad(y, target):
    t = y.shape[0]
    tm = _tile(t, 512)

    def body(y_ref, t_ref, dy_ref, l_ref):
        err = y_ref[...] - t_ref[...]
        dy_ref[...] = err * (1.0 / D)
        part = jnp.sum(jnp.sum(err * err, axis=1, keepdims=True), axis=0, keepdims=True) * (0.5 / D)
        part = jnp.broadcast_to(part, (8, DH))

        @pl.when(pl.program_id(0) == 0)
        def _():
            l_ref[...] = part

        @pl.when(pl.program_id(0) > 0)
        def _():
            l_ref[...] += part

    act = pl.BlockSpec((tm, D), lambda i: (i, 0))
    return pl.pallas_call(
        body, name="loss_grad", grid=(t // tm,),
        in_specs=[act, act], out_specs=[act, _full((8, DH))],
        out_shape=[jax.ShapeDtypeStruct((t, D), F32), jax.ShapeDtypeStruct((8, DH), F32)],
        compiler_params=_cparams(("arbitrary",)),
    )(y, target)


def _flips(rel):
    x, y, c = lax.axis_index("x"), lax.axis_index("y"), lax.axis_index("c")
    fx, fy, fc = rel
    return (x ^ fx if fx else x, y ^ fy if fy else y, c ^ fc if fc else c)


CHIP_RELS = ((1, 0, 0), (0, 1, 0), (1, 1, 0))
ALL_RELS = tuple((fx, fy, fc) for fx in (0, 1) for fy in (0, 1) for fc in (0, 1) if (fx, fy, fc) != (0, 0, 0))


def _slot_of(pos, by_chip):
    px, py, pc = pos
    return 2 * px + py if by_chip else 4 * px + 2 * py + pc


def _exchange(srcs, rels, by_chip, scatter, name):
    nslot = NCHIP if by_chip else NDEV
    narr, nrel = len(srcs), len(rels)

    def body(*refs):
        src_refs, dst_refs = refs[:narr], refs[narr:2 * narr]
        send_sems, recv_sems, local_sems = refs[2 * narr:]
        me = _flips((0, 0, 0))
        my_slot = _slot_of(me, by_chip)
        started = []
        for a, (src_ref, dst_ref) in enumerate(zip(src_refs, dst_refs)):
            mine = pltpu.make_async_copy(src_ref.at[my_slot] if scatter else src_ref, dst_ref.at[my_slot],
                                         local_sems.at[a])
            mine.start()
            started.append(mine)
        sends = []
        for a, (src_ref, dst_ref) in enumerate(zip(src_refs, dst_refs)):
            for k, rel in enumerate(rels):
                peer = _flips(rel)
                part = src_ref.at[_slot_of(peer, by_chip)] if scatter else src_ref
                cp = pltpu.make_async_remote_copy(
                    src_ref=part, dst_ref=dst_ref.at[my_slot], send_sem=send_sems.at[a * nrel + k],
                    recv_sem=recv_sems.at[a * nrel + k], device_id=peer, device_id_type=pl.DeviceIdType.MESH)
                cp.start()
                sends.append(cp)
        for a, (src_ref, dst_ref) in enumerate(zip(src_refs, dst_refs)):
            for k, rel in enumerate(rels):
                peer = _flips(rel)
                part = src_ref.at[0] if scatter else src_ref
                pltpu.make_async_remote_copy(
                    src_ref=part, dst_ref=dst_ref.at[_slot_of(peer, by_chip)], send_sem=send_sems.at[a * nrel + k],
                    recv_sem=recv_sems.at[a * nrel + k], device_id=peer,
                    device_id_type=pl.DeviceIdType.MESH).wait_recv()
        for cp in sends:
            cp.wait_send()
        for mine in started:
            mine.wait()

    return pl.pallas_call(
        body, name=name,
        in_specs=[pl.BlockSpec(memory_space=pl.ANY)] * narr,
        out_specs=[pl.BlockSpec(memory_space=pl.ANY)] * narr,
        out_shape=[jax.ShapeDtypeStruct((nslot,) + s.shape[-2:], s.dtype) for s in srcs],
        scratch_shapes=[pltpu.SemaphoreType.DMA((narr * nrel,)), pltpu.SemaphoreType.DMA((narr * nrel,)),
                        pltpu.SemaphoreType.DMA((narr,))],
    )(*srcs)


def _sibling_swap(srcs, name):
    narr = len(srcs)

    def body(*refs):
        src_refs, dst_refs = refs[:narr], refs[narr:2 * narr]
        send_sems, recv_sems = refs[2 * narr:]
        peer = _flips((0, 0, 1))
        copies = [pltpu.make_async_remote_copy(src_ref=s, dst_ref=d, send_sem=send_sems.at[a], recv_sem=recv_sems.at[a],
                                               device_id=peer, device_id_type=pl.DeviceIdType.MESH)
                  for a, (s, d) in enumerate(zip(src_refs, dst_refs))]
        for cp in copies:
            cp.start()
        for cp in copies:
            cp.wait()

    return pl.pallas_call(
        body, name=name,
        in_specs=[pl.BlockSpec(memory_space=pl.ANY)] * narr,
        out_specs=[pl.BlockSpec(memory_space=pl.ANY)] * narr,
        out_shape=[jax.ShapeDtypeStruct(s.shape, s.dtype) for s in srcs],
        scratch_shapes=[pltpu.SemaphoreType.DMA((narr,)), pltpu.SemaphoreType.DMA((narr,))],
    )(*srcs)


def _sum_slots(parts, name):
    ns, rows, cols = parts.shape
    tr = _row_tile(rows, 256)

    def body(p_ref, o_ref):
        acc = p_ref[0].astype(F32)
        for s in range(1, ns):
            acc = acc + p_ref[s].astype(F32)
        o_ref[...] = acc

    return pl.pallas_call(
        body, name=name, grid=(rows // tr,),
        in_specs=[pl.BlockSpec((ns, tr, cols), lambda i: (0, i, 0))],
        out_specs=pl.BlockSpec((tr, cols), lambda i: (i, 0)),
        out_shape=jax.ShapeDtypeStruct((rows, cols), F32),
        compiler_params=_cparams(("parallel",)),
    )(parts)


def _adamw(w, g_parts, m, v, name, max_rows=256):
    rows, cols = w.shape
    tr = _row_tile(rows, max_rows)
    c1 = 1.0 / (1.0 - ADAM_B1 ** ADAM_STEP)
    c2 = 1.0 / (1.0 - ADAM_B2 ** ADAM_STEP)
    npart = len(g_parts)

    def body(*refs):
        w_ref, m_ref, v_ref = refs[:3]
        g_refs = refs[3:3 + npart]
        go_ref, d_ref, nm_ref, nv_ref = refs[3 + npart:]
        terms = []
        for g_ref in g_refs:
            terms += [g_ref[...]] if len(g_ref.shape) == 2 else [g_ref[s] for s in range(g_ref.shape[0])]
        g = terms[0]
        for term in terms[1:]:
            g = g + term
        nm = ADAM_B1 * m_ref[...] + (1.0 - ADAM_B1) * g
        nv = ADAM_B2 * v_ref[...] + (1.0 - ADAM_B2) * (g * g)
        d_ref[...] = -ADAM_LR * ((nm * c1) / (jnp.sqrt(nv * c2) + ADAM_EPS) + ADAM_WD * w_ref[...])
        go_ref[...] = g
        nm_ref[...] = nm
        nv_ref[...] = nv

    blk = pl.BlockSpec((tr, cols), lambda i: (i, 0))
    g_specs = [blk if p.ndim == 2 else pl.BlockSpec((p.shape[0], tr, cols), lambda i: (0, i, 0)) for p in g_parts]
    out = jax.ShapeDtypeStruct((rows, cols), F32)
    return pl.pallas_call(
        body, name=name, grid=(rows // tr,),
        in_specs=[blk, blk, blk] + g_specs,
        out_specs=[blk] * 4, out_shape=[out] * 4,
        compiler_params=_cparams(("parallel",)),
    )(w, m, v, *g_parts)


WIN_SHARD = 2052
CONV_SHARD = 768
ROW_SHARD = 256

SMALL = (("norm_pre", (DEPTH, D)), ("a_log", (DEPTH, NH)), ("dt_bias", (DEPTH, NH)), ("head_norm", (DEPTH, DH)),
         ("ssm_a_re", (DEPTH, NG, NS)), ("ssm_a_im", (DEPTH, NG, NS)), ("ssm_log_dt", (DEPTH, NG)),
         ("ssm_b_re", (DEPTH, NG, NS, GS)), ("ssm_b_im", (DEPTH, NG, NS, GS)),
         ("ssm_c_re", (DEPTH, NG, GS, NS)), ("ssm_c_im", (DEPTH, NG, GS, NS)), ("ssm_d", (DEPTH, D)),
         ("b_glu", (DEPTH, D)), ("norm_post", (DEPTH, D)))


def _pad_rows(flat, rows):
    return jnp.pad(flat, (0, rows * D - flat.shape[0])).reshape(rows, D)


def _rows_by_chip(a):
    nl, rows, cols = a.shape
    return a.reshape(nl, NCHIP, rows // NCHIP, cols).transpose(1, 0, 2, 3).reshape(NCHIP, -1, cols)


def _rows_from_chips(a):
    _, rows, cols = a.shape
    return a.reshape(NCHIP, DEPTH, rows // DEPTH, cols).transpose(1, 0, 2, 3).reshape(DEPTH, -1, cols)


def _cols_by_chip(a):
    nl, rows, cols = a.shape
    return a.reshape(nl, rows, NCHIP, cols // NCHIP).transpose(2, 0, 1, 3).reshape(NCHIP, nl * rows, -1)


def _cols_from_chips(a, nl):
    _, rows, cols = a.shape
    return a.reshape(NCHIP, nl, rows // nl, cols).transpose(1, 2, 0, 3).reshape(nl, rows // nl, NCHIP * cols)


SMALL_ROWS = sum(-(-math.prod(s) // (8 * D)) * 8 for _, s in SMALL)
CONV_ROWS = DEPTH * 4 * 3 * D // D


def _pack_small(vals, extra=()):
    parts = []
    for val in tuple(vals) + tuple(extra):
        n = val.size
        parts.append(_pad_rows(val.reshape(-1), -(-n // (8 * D)) * 8))
    return jnp.concatenate(parts, axis=0)


def _unpack_small(flat):
    outs, r0 = [], 0
    for _, shape in SMALL:
        n = math.prod(shape)
        rows = -(-n // (8 * D)) * 8
        outs.append(flat[r0:r0 + rows].reshape(-1)[:n].reshape(shape))
        r0 += rows
    return outs


def _rearrange_cols(w):
    pad = jnp.zeros(w.shape[:-1] + (NCOL - BD0 - 2 * NH,), w.dtype)
    return jnp.concatenate([w[..., :4 * D], w[..., 4 * D + 2 * NH:], w[..., 4 * D:4 * D + 2 * NH], pad], axis=-1)


def _restore_cols(w):
    return jnp.concatenate([w[..., :4 * D], w[..., BD0:BD0 + 2 * NH], w[..., 4 * D:BD0]], axis=-1)


def _block_diag_b(bb2):
    b = bb2.reshape(NCB, GPB, NS, GS)
    eye = jnp.eye(GPB, dtype=F32)
    return jnp.einsum("kgnc,gh->kgchn", b, eye).reshape(NCB, GPB * GS, SW)


def _block_diag_b_t(d):
    return jnp.einsum("kgchn,gh->kgnc", d.reshape(NCB, GPB, GS, GPB, NS), jnp.eye(GPB, dtype=F32)).reshape(NG, NS * GS)


def _block_diag_c(c):
    eye = jnp.eye(GPB, dtype=F32)
    return jnp.einsum("kgcn,gh->kgnhc", c.reshape(NCB, GPB, GS, NS), eye).reshape(NCB, SW, GPB * GS)


def _block_diag_c_t(d):
    return jnp.einsum("kgnhc,gh->kgcn", d.reshape(NCB, GPB, NS, GPB, GS), jnp.eye(GPB, dtype=F32)).reshape(NG, GS, NS)


def _local_step(x, target, wcat, conv, wglu, wout, small):
    ar, ai = small["ssm_a_re"], small["ssm_a_im"]
    ldt = small["ssm_log_dt"].reshape(DEPTH, NG, 1)
    br2 = small["ssm_b_re"].reshape(DEPTH, NG, NS * GS)
    bi2 = small["ssm_b_im"].reshape(DEPTH, NG, NS * GS)
    lr, li, bbr2, bbi2 = _s5_params(ar, ai, ldt, br2, bi2)

    def row(name, l, width):
        return small[name][l].reshape(1, width)

    saved = []
    for l in range(DEPTH):
        gvec = jnp.pad(jnp.stack([small["a_log"][l], small["dt_bias"][l]]), ((0, 6), (NH, DH - 2 * NH)))
        lam = jnp.stack([lr[l].reshape(1, NG * NS), li[l].reshape(1, NG * NS)])
        bblk = jnp.stack([_block_diag_b(bbr2[l]), _block_diag_b(bbi2[l])])
        cblk = jnp.stack([_block_diag_c(small["ssm_c_re"][l]), _block_diag_c(small["ssm_c_im"][l])])
        proj, h = _inproj_fwd(x, row("norm_pre", l, D), wcat, l)
        qkv = _prep_fwd(proj, conv[l])
        bb, gcb = _gates_fwd(proj, gvec)
        local, t_inv = _delta_local_fwd(qkv, bb, gcb)
        o, states = _delta_state_fwd(local, gcb)
        s5y, carries = _s5_fwd(proj, lam, bblk, cblk)
        xn = _mix_fwd(proj, o, s5y, x, row("head_norm", l, DH), row("ssm_d", l, D), wglu, row("b_glu", l, D),
                      wout, row("norm_post", l, D), l)
        saved.append((x, proj, h, qkv, bb, gcb, local, t_inv, o, states, s5y, carries, gvec, lam, bblk, cblk))
        x = xn

    dx, loss_part = _loss_grad(x, target)

    g = {k: [None] * DEPTH for k in ("wcat", "conv", "wglu", "wout", "norm_pre", "a_log", "dt_bias", "head_norm",
                                     "ssm_c_re", "ssm_c_im", "ssm_d", "b_glu", "norm_post", "lr", "li", "bbr", "bbi")}
    for l in reversed(range(DEPTH)):
        xl, proj, h, qkv, bb, gcb, local, t_inv, o, states, s5y, carries, gvec, lam, bblk, cblk = saved[l]
        (dza, du_skip, dzb, dra, drb, do, ds5y, dxres, dwg, dwo, dvecs, dhn) = _mix_bwd(
            proj, o, s5y, xl, row("head_norm", l, DH), row("ssm_d", l, D), wglu, row("b_glu", l, D), wout,
            row("norm_post", l, D), dx, l)
        du, dlam, dbblk, dcblk = _s5_bwd(proj, lam, bblk, cblk, carries, ds5y, du_skip)
        *dlocal, dgcb_state = _delta_state_bwd(local, gcb, states, do)
        dq, dk, dv, dbb, dgcb = _delta_local_bwd(qkv, bb, gcb, t_inv, dlocal, dgcb_state)
        dbd, dgvec = _gates_bwd(proj, gvec, dbb, dgcb)
        dqkv = jnp.concatenate([dq, dk, dv], axis=0)
        dpre, dconv = _prep_bwd(proj, conv[l], dqkv)
        dproj = jnp.concatenate([dpre, dza, du, dzb, dra, drb, dbd], axis=1)
        dx, dgain = _inproj_bwd_dx(dproj, wcat, xl, row("norm_pre", l, D), dxres, l)
        g["wcat"][l] = _inproj_bwd_dw(h, dproj)
        g["conv"][l], g["wglu"][l], g["wout"][l] = dconv, dwg, dwo
        g["norm_pre"][l] = dgain[0]
        g["a_log"][l], g["dt_bias"][l] = dgvec[0, NH:2 * NH], dgvec[1, NH:2 * NH]
        g["head_norm"][l] = dhn[0]
        g["ssm_d"][l], g["b_glu"][l], g["norm_post"][l] = dvecs[0], dvecs[1], dvecs[2]
        g["ssm_c_re"][l], g["ssm_c_im"][l] = _block_diag_c_t(dcblk[0]), _block_diag_c_t(dcblk[1])
        g["lr"][l], g["li"][l] = dlam[0].reshape(NG, NS), dlam[1].reshape(NG, NS)
        g["bbr"][l], g["bbi"][l] = _block_diag_b_t(dbblk[0]), _block_diag_b_t(dbblk[1])
    g = {k: jnp.stack(v) for k, v in g.items()}
    dar, dai, dldt, dbr2, dbi2 = _s5_params_bwd(ar, ai, ldt, br2, bi2, g["lr"], g["li"], g["bbr"], g["bbi"])
    g["ssm_a_re"], g["ssm_a_im"], g["ssm_log_dt"] = dar, dai, dldt.reshape(DEPTH, NG)
    g["ssm_b_re"] = dbr2.reshape(DEPTH, NG, NS, GS)
    g["ssm_b_im"] = dbi2.reshape(DEPTH, NG, NS, GS)
    return loss_part[0, 0], dx, g


def kernel(x, norm_pre, w_in, conv_w, a_log, dt_bias, head_norm, ssm_a_re, ssm_a_im, ssm_log_dt, ssm_b_re, ssm_b_im, ssm_c_re, ssm_c_im, ssm_d, w_glu, b_glu, w_out, norm_post, loss_target, m_norm_pre, m_w_in, m_conv_w, m_a_log, m_dt_bias, m_head_norm, m_ssm_a_re, m_ssm_a_im, m_ssm_log_dt, m_ssm_b_re, m_ssm_b_im, m_ssm_c_re, m_ssm_c_im, m_ssm_d, m_w_glu, m_b_glu, m_w_out, m_norm_post, v_norm_pre, v_w_in, v_conv_w, v_a_log, v_dt_bias, v_head_norm, v_ssm_a_re, v_ssm_a_im, v_ssm_log_dt, v_ssm_b_re, v_ssm_b_im, v_ssm_c_re, v_ssm_c_im, v_ssm_d, v_w_glu, v_b_glu, v_w_out, v_norm_post):
    args = dict(locals())
    small = {n: args[n] for n, _ in SMALL}

    def flat2(a):
        return a.reshape(-1, a.shape[-1])

    g_in, g_glu, g_out, g_conv = _exchange(
        [flat2(w_in).astype(BF16), flat2(w_glu).astype(BF16), flat2(w_out).astype(BF16), flat2(conv_w)],
        CHIP_RELS, True, False, "gather_weights")
    wcat = _rearrange_cols(_cols_from_chips(g_in, DEPTH))
    wglu, wout = _rows_from_chips(g_glu), _rows_from_chips(g_out)
    conv = _cols_from_chips(g_conv, DEPTH)

    loss_part, dx, g = _local_step(x[0], loss_target[0], wcat, conv, wglu, wout, small)
    loss = lax.psum(loss_part, ("x", "y", "c"))

    from_chips = _exchange(
        [_cols_by_chip(_restore_cols(g["wcat"])).astype(BF16), _rows_by_chip(g["wglu"]).astype(BF16),
         _rows_by_chip(g["wout"]).astype(BF16)], CHIP_RELS, True, True, "scatter_grads")
    core_sums = [_sum_slots(p, "sum_chips_" + n) for p, n in zip(from_chips, ("in", "glu", "out"))]
    others = _sibling_swap(core_sums, "swap_cores")
    sharded = {}
    for n, mine, other in zip(("w_in", "w_glu", "w_out"), core_sums, others):
        sharded[n] = _adamw(flat2(args[n]), [mine, other], flat2(args["m_" + n]), flat2(args["v_" + n]), "adamw_" + n,
                            max_rows=128)

    (small_parts,) = _exchange([_pack_small([g[n] for n, _ in SMALL], extra=[g["conv"]])], ALL_RELS, False, False,
                               "gather_small")
    small_out = _adamw(_pack_small([args[n] for n, _ in SMALL]), [small_parts],
                       _pack_small([args["m_" + n] for n, _ in SMALL]),
                       _pack_small([args["v_" + n] for n, _ in SMALL]), "adamw_small")
    chip = 2 * lax.axis_index("x") + lax.axis_index("y")
    conv_parts = small_parts[:, SMALL_ROWS:SMALL_ROWS + CONV_ROWS].reshape(NDEV, DEPTH * 4, 3 * D)
    conv_parts = lax.dynamic_slice_in_dim(conv_parts, chip * CONV_SHARD, CONV_SHARD, axis=2)
    sharded["conv_w"] = _adamw(flat2(conv_w), [conv_parts], flat2(m_conv_w), flat2(v_conv_w), "adamw_conv")

    names = ["norm_pre", "w_in", "conv_w", "a_log", "dt_bias", "head_norm", "ssm_a_re", "ssm_a_im", "ssm_log_dt",
             "ssm_b_re", "ssm_b_im", "ssm_c_re", "ssm_c_im", "ssm_d", "w_glu", "b_glu", "w_out", "norm_post"]
    outs = [loss, dx[None]]
    for i in range(4):
        sm = dict(zip([n for n, _ in SMALL], _unpack_small(small_out[i])))
        outs += [sharded[n][i].reshape(args[n].shape) if n in sharded else sm[n] for n in names]
    return tuple(outs)
```

```python
import functools
import math

import jax
import jax.numpy as jnp
from jax import lax
from jax.experimental import pallas as pl
from jax.experimental.pallas import tpu as pltpu

F32 = jnp.float32
BF16 = jnp.bfloat16
HI = lax.Precision.HIGHEST

D = 1024
NH = 8
DH = 128
CH = 128
NG = 64
GS = 16
NS = 64
GPB = 8
NCB = NG // GPB
SW = GPB * NS
NCOL = 8320
BD0 = 8192
EPS = 1e-6
DEPTH = 4
NCHIP = 4
NDEV = 8
VMEM_LIMIT = 56 * 1024 * 1024

ADAM_LR = 0.001
ADAM_B1 = 0.9
ADAM_B2 = 0.999
ADAM_EPS = 1e-08
ADAM_WD = 0.01
ADAM_STEP = 10


def _cparams(sem=None):
    return pltpu.CompilerParams(dimension_semantics=sem, vmem_limit_bytes=VMEM_LIMIT)


def _full(shape):
    nd = len(shape)
    return pl.BlockSpec(shape, lambda *_: (0,) * nd)


def _rms(x, gain):
    ms = jnp.mean(x * x, axis=-1, keepdims=True)
    return x * lax.rsqrt(ms + EPS) * gain


def _sigmoid(x):
    return 1.0 / (1.0 + jnp.exp(-x))


def _silu(x):
    return x * _sigmoid(x)


def _softplus(x):
    return jnp.maximum(x, 0.0) + jnp.log(1.0 + jnp.exp(-jnp.abs(x)))


def _gelu(x):
    return 0.5 * x * (1.0 + jnp.tanh(math.sqrt(2.0 / math.pi) * (x + 0.044715 * (x * x * x))))


def _dot_bf16(a, b, dims):
    return lax.dot_general(a.astype(BF16), b.astype(BF16), (dims, ((), ())), preferred_element_type=F32)


def _mm_nt(a, b):
    return _dot_bf16(a, b, ((1,), (1,)))


def _mm_tn(a, b):
    return _dot_bf16(a, b, ((0,), (0,)))


@jax.custom_vjp
def _mm(a, b):
    return _dot_bf16(a, b, ((1,), (0,)))


def _mm_fwd(a, b):
    return _dot_bf16(a, b, ((1,), (0,))), (a, b)


def _mm_bwd(res, ct):
    a, b = res
    return _mm_nt(ct, b).astype(a.dtype), _mm_tn(a, ct).astype(b.dtype)


_mm.defvjp(_mm_fwd, _mm_bwd)


@jax.custom_vjp
def _mm_nt_d(a, b):
    return _mm_nt(a, b)


def _mm_nt_d_bwd(res, ct):
    a, b = res
    return _dot_bf16(ct, b, ((1,), (0,))), _mm_tn(ct, a)


_mm_nt_d.defvjp(lambda a, b: (_mm_nt(a, b), (a, b)), _mm_nt_d_bwd)


@jax.custom_vjp
def _mm_tn_d(a, b):
    return _mm_tn(a, b)


def _mm_tn_d_bwd(res, ct):
    a, b = res
    return _mm_nt(b, ct), _dot_bf16(a, ct, ((1,), (0,)))


_mm_tn_d.defvjp(lambda a, b: (_mm_tn(a, b), (a, b)), _mm_tn_d_bwd)


def _split_bf16(a):
    hi = a.astype(BF16)
    return hi, (a - hi.astype(F32)).astype(BF16)


def _dot3(a, b, dims):
    ah, al = _split_bf16(a)
    bh, bl = _split_bf16(b)

    def dot(x, y):
        return lax.dot_general(x, y, (dims, ((), ())), preferred_element_type=F32)

    return dot(ah, bh) + (dot(ah, bl) + dot(al, bh))


@jax.custom_vjp
def _imm(a, b):
    return _dot3(a, b, ((1,), (0,)))


def _imm_bwd(res, ct):
    a, b = res
    return _dot3(ct, b, ((1,), (1,))), _dot3(a, ct, ((0,), (0,)))


_imm.defvjp(lambda a, b: (_dot3(a, b, ((1,), (0,))), (a, b)), _imm_bwd)


def _hmm(a, b):
    return jnp.dot(a, b, precision=HI, preferred_element_type=F32)


def _hmm_nt(a, b):
    return lax.dot_general(a, b, (((1,), (1,)), ((), ())), precision=HI, preferred_element_type=F32)


def _hmm_tn(a, b):
    return lax.dot_general(a, b, (((0,), (0,)), ((), ())), precision=HI, preferred_element_type=F32)


def _rows(shape):
    return lax.broadcasted_iota(jnp.int32, shape, 0)


def _cols(shape):
    return lax.broadcasted_iota(jnp.int32, shape, 1)


def _sd(x, s):
    return jnp.where(_rows(x.shape) >= s, pltpu.roll(x, s, axis=0), 0.0)


def _su(x, s):
    n = x.shape[0]
    return jnp.where(_rows(x.shape) < n - s, pltpu.roll(x, n - s, axis=0), 0.0)


@functools.partial(jax.custom_vjp, nondiff_argnums=(1,))
def _shift_down(x, s):
    return _sd(x, s)


def _shift_down_fwd(x, s):
    return _sd(x, s), None


def _shift_down_bwd(s, _, g):
    return (_su(g, s),)


_shift_down.defvjp(_shift_down_fwd, _shift_down_bwd)


def _last_row(x):
    n = x.shape[0]
    return jnp.sum(jnp.where(_rows(x.shape) == n - 1, x, 0.0), axis=0, keepdims=True)


def _prep_fn(p, w0, w1, w2, w3, qk):
    acc = w3 * p + w2 * _shift_down(p, 1) + w1 * _shift_down(p, 2) + w0 * _shift_down(p, 3)
    a = _silu(acc)
    nrm = lax.rsqrt(jnp.sum(a * a, axis=-1, keepdims=True) + EPS)
    return a * (nrm * qk + (1.0 - qk))


def _gates_fn(bd, av, bv):
    tm = bd.shape[0]
    beta_all = _sigmoid(bd)
    g_all = -jnp.exp(av) * _softplus(bd + bv)
    r, c = _rows((tm, tm)), _cols((tm, tm))
    tri = jnp.where((r // CH == c // CH) & (r >= c), 1.0, 0.0).astype(F32)
    gc_all = _hmm(tri, g_all)
    lane = _cols(bd.shape)
    outs = []
    for h in range(NH):
        b = jnp.sum(jnp.where(lane == h, beta_all, 0.0), axis=1, keepdims=True)
        outs.append(jnp.broadcast_to(b, bd.shape))
    for h in range(NH):
        g = jnp.sum(jnp.where(lane == NH + h, gc_all, 0.0), axis=1, keepdims=True)
        outs.append(jnp.broadcast_to(g, bd.shape))
    return tuple(outs)


def _unit_lower_inv(l_mat):
    n = l_mat.shape[0]
    eye = jnp.where(_rows((n, n)) == _cols((n, n)), 1.0, 0.0).astype(F32)
    p = -l_mat
    r = eye + p
    k = 1
    while 2 * k < n:
        p = _imm(p, p)
        r = r + _imm(r, p)
        k *= 2
    return r


@jax.custom_vjp
def _known_inverse(l_mat, t_inv):
    return t_inv


def _known_inverse_bwd(t_inv, ct):
    d_l = -_dot3(_dot3(t_inv, ct, ((0,), (0,))), t_inv, ((1,), (1,)))
    return d_l, jnp.zeros_like(t_inv)


_known_inverse.defvjp(lambda l_mat, t_inv: (t_inv, t_inv), _known_inverse_bwd)


def _chunk_local(q, k, v, bb, gcb, t_inv=None):
    qs = q * (DH ** -0.5)
    kb = k * bb
    eg = jnp.exp(gcb)
    ii, jj = _rows((CH, CH)), _cols((CH, CH))
    decay = jnp.exp(jnp.where(ii >= jj, gcb - gcb.T, -1e30))
    l_mat = jnp.where(ii > jj, _mm_nt_d(kb, k) * decay, 0.0)
    t_inv = _unit_lower_inv(l_mat) if t_inv is None else _known_inverse(l_mat, t_inv)
    u = _mm(t_inv, v * bb)
    w = _mm(t_inv, kb * eg)
    a_qk = _mm_nt_d(qs, k) * decay
    k_dec = k * jnp.exp(_last_row(gcb) - gcb)
    return (u, w, qs * eg, k_dec, a_qk), t_inv


def _state_step(u, w, q_dec, k_dec, a_qk, gcb, state):
    v_new = u - _mm(w, state)
    o = _mm(q_dec, state) + _mm(a_qk, v_new)
    new_state = state * jnp.exp(_last_row(gcb)) + _mm_tn_d(k_dec, v_new)
    return o, new_state


SUB = 8


def _cmul(ar, ai, br, bi):
    return ar * br - ai * bi, ar * bi + ai * br


def _shift_in_groups(x, s, reverse):
    n = x.shape[0]
    pos = jnp.bitwise_and(_rows(x.shape), SUB - 1)
    if reverse:
        return jnp.where(pos < SUB - s, pltpu.roll(x, n - s, axis=0), 0.0)
    return jnp.where(pos >= s, pltpu.roll(x, s, axis=0), 0.0)


def _scan_tile(xr, xi, mr, mi, hr_ref, hi_ref, cr_ref, ci_ref, reverse):
    n = xr.shape[0]
    ngroups = n // SUB
    shift_groups = _su if reverse else _sd
    pr, pi = mr, mi
    tr, ti = jnp.broadcast_to(mr, (SUB, mr.shape[1])), jnp.broadcast_to(mi, (SUB, mi.shape[1]))
    pos = _rows(tr.shape)
    s = 1
    while s < SUB:
        sr, si = _shift_in_groups(xr, s, reverse), _shift_in_groups(xi, s, reverse)
        dr, di = _cmul(pr, pi, sr, si)
        xr, xi = xr + dr, xi + di
        inside = pos < SUB - s if reverse else pos >= s
        shift = SUB - s if reverse else s
        er = jnp.where(inside, pltpu.roll(tr, shift, axis=0), 1.0)
        ei = jnp.where(inside, pltpu.roll(ti, shift, axis=0), 0.0)
        tr, ti = _cmul(tr, ti, er, ei)
        pr, pi = _cmul(pr, pi, pr, pi)
        s *= 2
    nlb = xr.shape[1] // DH

    def lanes(x, j):
        return x[:, j * DH:(j + 1) * DH]

    for j in range(nlb):
        hr_ref[j] = lanes(xr, j)
        hi_ref[j] = lanes(xi, j)
    edge = pl.ds(0 if reverse else SUB - 1, ngroups, stride=SUB)
    gr = jnp.concatenate([hr_ref.at[j][edge, :] for j in range(nlb)], axis=1)
    gi = jnp.concatenate([hi_ref.at[j][edge, :] for j in range(nlb)], axis=1)
    s = 1
    while s < ngroups:
        dr, di = _cmul(pr, pi, shift_groups(gr, s), shift_groups(gi, s))
        gr, gi = gr + dr, gi + di
        pr, pi = _cmul(pr, pi, pr, pi)
        s *= 2
    cr_ref[...] = shift_groups(gr, 1)
    ci_ref[...] = shift_groups(gi, 1)
    for g in range(ngroups):
        rows = slice(g * SUB, (g + 1) * SUB)
        dr, di = _cmul(tr, ti, cr_ref[g:g + 1, :], ci_ref[g:g + 1, :])
        for j in range(nlb):
            hr_ref[j, rows, :] += lanes(dr, j)
            hi_ref[j, rows, :] += lanes(di, j)
    return (jnp.concatenate([hr_ref[j] for j in range(nlb)], axis=1),
            jnp.concatenate([hi_ref[j] for j in range(nlb)], axis=1))


def _s5_states(u, lam_ref, b_ref, car_ref, hr_ref, hi_ref, cr_ref, ci_ref):
    lr, li = lam_ref[0], lam_ref[1]
    first = _rows((u.shape[0], SW)) == 0
    inr, ini = _cmul(lr, li, car_ref[0:1, :], car_ref[1:2, :])
    xr = _mm(u, b_ref[0]) + jnp.where(first, inr, 0.0)
    xi = _mm(u, b_ref[1]) + jnp.where(first, ini, 0.0)
    return _scan_tile(xr, xi, lr, li, hr_ref, hi_ref, cr_ref, ci_ref, False)


def _s5_params_fn(ar, ai, ldt, br2, bi2):
    dt = jnp.exp(ldt)
    mag = jnp.exp(ar * dt)
    lr, li = mag * jnp.cos(ai * dt), mag * jnp.sin(ai * dt)
    den = ar * ar + ai * ai
    fr = ((lr - 1.0) * ar + li * ai) / den
    fi = (li * ar - (lr - 1.0) * ai) / den
    expand = jnp.where(_cols((NS, NS * GS)) // GS == _rows((NS, NS * GS)), 1.0, 0.0).astype(F32)
    fr2, fi2 = _hmm(fr, expand), _hmm(fi, expand)
    return lr, li, fr2 * br2 - fi2 * bi2, fr2 * bi2 + fi2 * br2


def _head_norm(o, hn):
    parts = []
    for h in range(NH):
        oh = o[:, h * DH:(h + 1) * DH]
        parts.append(oh * lax.rsqrt(jnp.mean(oh * oh, axis=-1, keepdims=True) + EPS) * hn)
    return jnp.concatenate(parts, axis=1)


def _mix_pre(s5y, u, dvec):
    return _gelu(s5y + dvec * u)


def _mix_mid(o, za, y0, gl, zb, ra, rb, hn):
    ya = _head_norm(o, hn) * _silu(za)
    yb = y0 * _sigmoid(gl) * _silu(zb)
    return _sigmoid(ra) * ya + _sigmoid(rb) * yb


def _mix_post(x, out, npost):
    return x + _rms(out, npost)


def _tile(t, want):
    return min(t, want)


def _row_tile(rows, want):
    return max(r for r in range(16, want + 1, 16) if rows % r == 0)


def _inproj_fwd(x, gain, wcat, l):
    t = x.shape[0]
    tm, tn = _tile(t, 1024), 640

    def body(x_ref, g_ref, w_ref, o_ref, h_ref):
        @pl.when(pl.program_id(1) == 0)
        def _():
            h_ref[...] = _rms(x_ref[...], g_ref[...]).astype(h_ref.dtype)
        o_ref[...] = _dot_bf16(h_ref[...], w_ref[...], ((1,), (0,)))

    return pl.pallas_call(
        body, name="inproj_fwd", grid=(t // tm, NCOL // tn),
        in_specs=[pl.BlockSpec((tm, D), lambda i, j: (i, 0)), _full((1, D)),
                  pl.BlockSpec((None, D, tn), lambda i, j: (l, 0, j))],
        out_specs=[pl.BlockSpec((tm, tn), lambda i, j: (i, j)), pl.BlockSpec((tm, D), lambda i, j: (i, 0))],
        out_shape=[jax.ShapeDtypeStruct((t, NCOL), F32), jax.ShapeDtypeStruct((t, D), wcat.dtype)],
        compiler_params=_cparams(("parallel", "arbitrary")),
    )(x, gain, wcat)


def _inproj_bwd_dx(dproj, wcat, x, gain, dxres, l):
    t = x.shape[0]
    tm, tk = _tile(t, 1024), 640
    nk = NCOL // tk

    def body(dp_ref, w_ref, x_ref, g_ref, r_ref, dx_ref, dg_ref, acc_ref):
        i, k = pl.program_id(0), pl.program_id(1)

        @pl.when(k == 0)
        def _():
            acc_ref[...] = jnp.zeros_like(acc_ref)

        acc_ref[...] += _mm_nt(dp_ref[...], w_ref[...])

        @pl.when(k == nk - 1)
        def _():
            _, vjp = jax.vjp(_rms, x_ref[...], g_ref[...])
            dx, dg = vjp(acc_ref[...])
            dx_ref[...] = r_ref[...] + dx

            @pl.when(i == 0)
            def _():
                dg_ref[...] = dg

            @pl.when(i > 0)
            def _():
                dg_ref[...] += dg

    return pl.pallas_call(
        body, name="inproj_bwd_dx", grid=(t // tm, nk),
        in_specs=[pl.BlockSpec((tm, tk), lambda i, k: (i, k)), pl.BlockSpec((None, D, tk), lambda i, k: (l, 0, k)),
                  pl.BlockSpec((tm, D), lambda i, k: (i, 0)), _full((1, D)),
                  pl.BlockSpec((tm, D), lambda i, k: (i, 0))],
        out_specs=[pl.BlockSpec((tm, D), lambda i, k: (i, 0)), _full((1, D))],
        out_shape=[jax.ShapeDtypeStruct((t, D), F32), jax.ShapeDtypeStruct((1, D), F32)],
        scratch_shapes=[pltpu.VMEM((tm, D), F32)],
        compiler_params=_cparams(("arbitrary", "arbitrary")),
    )(dproj, wcat, x, gain, dxres)


def _inproj_bwd_dw(h, dproj):
    t = h.shape[0]
    tm, tn = _tile(t, 512), 1664

    def body(h_ref, dp_ref, o_ref):
        @pl.when(pl.program_id(1) == 0)
        def _():
            o_ref[...] = jnp.zeros_like(o_ref)

        o_ref[...] += _mm_tn(h_ref[...], dp_ref[...])

    return pl.pallas_call(
        body, name="inproj_bwd_dw", grid=(NCOL // tn, t // tm),
        in_specs=[pl.BlockSpec((tm, D), lambda j, i: (i, 0)), pl.BlockSpec((tm, tn), lambda j, i: (i, j))],
        out_specs=pl.BlockSpec((D, tn), lambda j, i: (0, j)),
        out_shape=jax.ShapeDtypeStruct((D, NCOL), F32),
        compiler_params=_cparams(("parallel", "arbitrary")),
    )(h, dproj)


def _prep_fwd(proj, cw):
    t = proj.shape[0]

    def body(p_ref, w_ref, o_ref):
        qk = (pl.program_id(0) < 2 * NH).astype(F32)
        o_ref[...] = _prep_fn(p_ref[...], w_ref[0:1, :], w_ref[1:2, :], w_ref[2:3, :], w_ref[3:4, :], qk)

    return pl.pallas_call(
        body, name="prep_fwd", grid=(3 * NH,),
        in_specs=[pl.BlockSpec((t, DH), lambda c: (0, c)), pl.BlockSpec((4, DH), lambda c: (0, c))],
        out_specs=pl.BlockSpec((None, t, DH), lambda c: (c, 0, 0)),
        out_shape=jax.ShapeDtypeStruct((3 * NH, t, DH), F32),
        compiler_params=_cparams(("parallel",)),
    )(proj, cw)


def _prep_bwd(proj, cw, dqkv):
    t = proj.shape[0]

    def body(p_ref, w_ref, d_ref, dp_ref, dw_ref):
        qk = (pl.program_id(0) < 2 * NH).astype(F32)
        _, vjp = jax.vjp(lambda p, w0, w1, w2, w3: _prep_fn(p, w0, w1, w2, w3, qk),
                         p_ref[...], w_ref[0:1, :], w_ref[1:2, :], w_ref[2:3, :], w_ref[3:4, :])
        dp, dw0, dw1, dw2, dw3 = vjp(d_ref[...])
        dp_ref[...] = dp
        dw_ref[0:1, :] = dw0
        dw_ref[1:2, :] = dw1
        dw_ref[2:3, :] = dw2
        dw_ref[3:4, :] = dw3

    return pl.pallas_call(
        body, name="prep_bwd", grid=(3 * NH,),
        in_specs=[pl.BlockSpec((t, DH), lambda c: (0, c)), pl.BlockSpec((4, DH), lambda c: (0, c)),
                  pl.BlockSpec((None, t, DH), lambda c: (c, 0, 0))],
        out_specs=[pl.BlockSpec((t, DH), lambda c: (0, c)), pl.BlockSpec((4, DH), lambda c: (0, c))],
        out_shape=[jax.ShapeDtypeStruct((t, 3 * D), F32), jax.ShapeDtypeStruct((4, 3 * D), F32)],
        compiler_params=_cparams(("parallel",)),
    )(proj, cw, dqkv)


def _gates_fwd(proj, gvec):
    t = proj.shape[0]
    tm = _tile(t, 512)

    def body(p_ref, gv_ref, b_ref, g_ref):
        outs = _gates_fn(p_ref[...], gv_ref[0:1, :], gv_ref[1:2, :])
        for h in range(NH):
            b_ref[h] = outs[h]
            g_ref[h] = outs[NH + h]

    spec = pl.BlockSpec((NH, tm, DH), lambda i: (0, i, 0))
    return pl.pallas_call(
        body, name="gates_fwd", grid=(t // tm,),
        in_specs=[pl.BlockSpec((tm, DH), lambda i: (i, BD0 // DH)), _full((8, DH))],
        out_specs=[spec, spec],
        out_shape=[jax.ShapeDtypeStruct((NH, t, DH), F32)] * 2,
        compiler_params=_cparams(("parallel",)),
    )(proj, gvec)


def _gates_bwd(proj, gvec, dbb, dgcb):
    t = proj.shape[0]
    tm = _tile(t, 512)

    def body(p_ref, gv_ref, db_ref, dg_ref, dp_ref, dgv_ref):
        _, vjp = jax.vjp(_gates_fn, p_ref[...], gv_ref[0:1, :], gv_ref[1:2, :])
        cts = tuple(db_ref[h] for h in range(NH)) + tuple(dg_ref[h] for h in range(NH))
        dp, da, db = vjp(cts)
        dp_ref[...] = dp

        @pl.when(pl.program_id(0) == 0)
        def _():
            dgv_ref[...] = jnp.zeros_like(dgv_ref)

        dgv_ref[0:1, :] += da
        dgv_ref[1:2, :] += db

    spec = pl.BlockSpec((NH, tm, DH), lambda i: (0, i, 0))
    return pl.pallas_call(
        body, name="gates_bwd", grid=(t // tm,),
        in_specs=[pl.BlockSpec((tm, DH), lambda i: (i, BD0 // DH)), _full((8, DH)), spec, spec],
        out_specs=[pl.BlockSpec((tm, DH), lambda i: (i, 0)), _full((8, DH))],
        out_shape=[jax.ShapeDtypeStruct((t, DH), F32), jax.ShapeDtypeStruct((8, DH), F32)],
        compiler_params=_cparams(("arbitrary",)),
    )(proj, gvec, dbb, dgcb)


def _chunks_per_step(nch):
    return 2 if nch % 2 == 0 else 1


def _delta_local_fwd(qkv, bb, gcb):
    t = qkv.shape[1]
    cps = _chunks_per_step(t // CH)
    rows = cps * CH

    def body(q_ref, k_ref, v_ref, b_ref, g_ref, *out_refs):
        for c in range(cps):
            sl = slice(c * CH, (c + 1) * CH)
            outs, t_inv = _chunk_local(q_ref[sl, :], k_ref[sl, :], v_ref[sl, :], b_ref[sl, :], g_ref[sl, :])
            for ref, val in zip(out_refs, outs + (t_inv,)):
                ref[sl, :] = val

    def blk(off):
        return pl.BlockSpec((None, rows, DH), lambda h, n: (h + off, n, 0))

    outs = pl.pallas_call(
        body, name="delta_local_fwd", grid=(NH, t // rows),
        in_specs=[blk(0), blk(NH), blk(2 * NH), blk(0), blk(0)],
        out_specs=[blk(0)] * 6,
        out_shape=[jax.ShapeDtypeStruct((NH, t, DH), F32)] * 6,
        compiler_params=_cparams(("parallel", "parallel")),
    )(qkv, qkv, qkv, bb, gcb)
    return outs[:5], outs[5]


def _delta_local_bwd(qkv, bb, gcb, t_inv, cts, dgcb_state):
    t = qkv.shape[1]
    cps = _chunks_per_step(t // CH)
    rows = cps * CH

    def body(q_ref, k_ref, v_ref, b_ref, g_ref, ti_ref, du_ref, dw_ref, dqd_ref, dkd_ref, da_ref, dgs_ref,
             dq_ref, dk_ref, dv_ref, db_ref, dg_ref):
        for c in range(cps):
            sl = slice(c * CH, (c + 1) * CH)
            t_inv_c = ti_ref[sl, :]
            _, vjp = jax.vjp(lambda *a: _chunk_local(*a, t_inv=t_inv_c)[0],
                             q_ref[sl, :], k_ref[sl, :], v_ref[sl, :], b_ref[sl, :], g_ref[sl, :])
            dq, dk, dv, db, dg = vjp((du_ref[sl, :], dw_ref[sl, :], dqd_ref[sl, :], dkd_ref[sl, :], da_ref[sl, :]))
            dq_ref[sl, :] = dq
            dk_ref[sl, :] = dk
            dv_ref[sl, :] = dv
            db_ref[sl, :] = db
            dg_ref[sl, :] = dg + dgs_ref[sl, :]

    def blk(off):
        return pl.BlockSpec((None, rows, DH), lambda h, n: (h + off, n, 0))

    return pl.pallas_call(
        body, name="delta_local_bwd", grid=(NH, t // rows),
        in_specs=[blk(0), blk(NH), blk(2 * NH)] + [blk(0)] * 9,
        out_specs=[blk(0)] * 5,
        out_shape=[jax.ShapeDtypeStruct((NH, t, DH), F32)] * 5,
        compiler_params=_cparams(("parallel", "parallel")),
    )(qkv, qkv, qkv, bb, gcb, t_inv, *cts, dgcb_state)


def _delta_state_fwd(local, gcb):
    t = gcb.shape[1]
    nch = t // CH

    def body(u_ref, w_ref, qd_ref, kd_ref, a_ref, g_ref, o_ref, s_ref, st_ref):
        @pl.when(pl.program_id(0) == 0)
        def _():
            st_ref[...] = jnp.zeros_like(st_ref)

        for h in range(NH):
            s_ref[h] = st_ref[h]
            o, ns = _state_step(u_ref[h], w_ref[h], qd_ref[h], kd_ref[h], a_ref[h], g_ref[h], st_ref[h])
            o_ref[:, h * DH:(h + 1) * DH] = o
            st_ref[h] = ns

    blk = pl.BlockSpec((NH, CH, DH), lambda n: (0, n, 0))
    return pl.pallas_call(
        body, name="delta_state_fwd", grid=(nch,),
        in_specs=[blk] * 6,
        out_specs=[pl.BlockSpec((CH, D), lambda n: (n, 0)),
                   pl.BlockSpec((NH, None, DH, DH), lambda n: (0, n, 0, 0))],
        out_shape=[jax.ShapeDtypeStruct((t, D), F32), jax.ShapeDtypeStruct((NH, nch, DH, DH), F32)],
        scratch_shapes=[pltpu.VMEM((NH, DH, DH), F32)],
        compiler_params=_cparams(("arbitrary",)),
    )(*local, gcb)


def _delta_state_bwd(local, gcb, states, do):
    t = gcb.shape[1]
    nch = t // CH

    def body(u_ref, w_ref, qd_ref, kd_ref, a_ref, g_ref, s_ref, do_ref,
             du_ref, dw_ref, dqd_ref, dkd_ref, da_ref, dg_ref, ds_ref):
        @pl.when(pl.program_id(0) == 0)
        def _():
            ds_ref[...] = jnp.zeros_like(ds_ref)

        for h in range(NH):
            _, vjp = jax.vjp(_state_step, u_ref[h], w_ref[h], qd_ref[h], kd_ref[h], a_ref[h], g_ref[h], s_ref[h])
            du, dw, dqd, dkd, da, dg, ds = vjp((do_ref[:, h * DH:(h + 1) * DH], ds_ref[h]))
            du_ref[h] = du
            dw_ref[h] = dw
            dqd_ref[h] = dqd
            dkd_ref[h] = dkd
            da_ref[h] = da
            dg_ref[h] = dg
            ds_ref[h] = ds

    blk = pl.BlockSpec((NH, CH, DH), lambda n: (0, nch - 1 - n, 0))
    return pl.pallas_call(
        body, name="delta_state_bwd", grid=(nch,),
        in_specs=[blk] * 6 + [pl.BlockSpec((NH, None, DH, DH), lambda n: (0, nch - 1 - n, 0, 0)),
                              pl.BlockSpec((CH, D), lambda n: (nch - 1 - n, 0))],
        out_specs=[blk] * 6,
        out_shape=[jax.ShapeDtypeStruct((NH, t, DH), F32)] * 6,
        scratch_shapes=[pltpu.VMEM((NH, DH, DH), F32)],
        compiler_params=_cparams(("arbitrary",)),
    )(*local, gcb, states, do)


def _s5_params(ar, ai, ldt, br2, bi2):
    def body(ar_ref, ai_ref, ld_ref, br_ref, bi_ref, lr_ref, li_ref, bbr_ref, bbi_ref):
        lr, li, bbr, bbi = _s5_params_fn(ar_ref[...], ai_ref[...], ld_ref[...], br_ref[...], bi_ref[...])
        lr_ref[...] = lr
        li_ref[...] = li
        bbr_ref[...] = bbr
        bbi_ref[...] = bbi

    sq = pl.BlockSpec((None, NG, NS), lambda l: (l, 0, 0))
    wide = pl.BlockSpec((None, NG, NS * GS), lambda l: (l, 0, 0))
    return pl.pallas_call(
        body, name="s5_params", grid=(DEPTH,),
        in_specs=[sq, sq, pl.BlockSpec((None, NG, 1), lambda l: (l, 0, 0)), wide, wide],
        out_specs=[sq, sq, wide, wide],
        out_shape=[jax.ShapeDtypeStruct((DEPTH, NG, NS), F32)] * 2
        + [jax.ShapeDtypeStruct((DEPTH, NG, NS * GS), F32)] * 2,
        compiler_params=_cparams(("parallel",)),
    )(ar, ai, ldt, br2, bi2)


def _s5_params_bwd(ar, ai, ldt, br2, bi2, dlr, dli, dbbr, dbbi):
    def body(ar_ref, ai_ref, ld_ref, br_ref, bi_ref, a_ref, b_ref, c_ref, d_ref,
             dar_ref, dai_ref, dld_ref, dbr_ref, dbi_ref):
        _, vjp = jax.vjp(_s5_params_fn, ar_ref[...], ai_ref[...], ld_ref[...], br_ref[...], bi_ref[...])
        dar, dai, dld, dbr, dbi = vjp((a_ref[...], b_ref[...], c_ref[...], d_ref[...]))
        dar_ref[...] = dar
        dai_ref[...] = dai
        dld_ref[...] = dld
        dbr_ref[...] = dbr
        dbi_ref[...] = dbi

    sq = pl.BlockSpec((None, NG, NS), lambda l: (l, 0, 0))
    col = pl.BlockSpec((None, NG, 1), lambda l: (l, 0, 0))
    wide = pl.BlockSpec((None, NG, NS * GS), lambda l: (l, 0, 0))
    return pl.pallas_call(
        body, name="s5_params_bwd", grid=(DEPTH,),
        in_specs=[sq, sq, col, wide, wide, sq, sq, wide, wide],
        out_specs=[sq, sq, col, wide, wide],
        out_shape=[jax.ShapeDtypeStruct((DEPTH, NG, NS), F32)] * 2 + [jax.ShapeDtypeStruct((DEPTH, NG, 1), F32)]
        + [jax.ShapeDtypeStruct((DEPTH, NG, NS * GS), F32)] * 2,
        compiler_params=_cparams(("parallel",)),
    )(ar, ai, ldt, br2, bi2, dlr, dli, dbbr, dbbi)


def _s5_tile_rows(t):
    return _tile(t // 2, 256)


def _s5_fwd(proj, lam, bblk, cblk):
    t = proj.shape[0]
    r = _s5_tile_rows(t)
    nt = t // r
    u0 = 4 * D // DH

    def body(u_ref, lam_ref, b_ref, c_ref, y_ref, car_ref, st_ref, hr_ref, hi_ref, cr_ref, ci_ref):
        @pl.when(pl.program_id(1) == 0)
        def _():
            st_ref[...] = jnp.zeros_like(st_ref)

        car_ref[...] = st_ref[...]
        hr, hi = _s5_states(u_ref[...], lam_ref, b_ref, st_ref, hr_ref, hi_ref, cr_ref, ci_ref)
        y_ref[...] = _mm(hr, c_ref[0]) - _mm(hi, c_ref[1])
        st_ref[0:1, :] = _last_row(hr)
        st_ref[1:2, :] = _last_row(hi)

    scratch = [pltpu.VMEM((8, SW), F32)] + [pltpu.VMEM((SW // DH, r, DH), F32)] * 2 + [pltpu.VMEM((r // SUB, SW), F32)] * 2
    return pl.pallas_call(
        body, name="s5_fwd", grid=(NCB, nt),
        in_specs=[pl.BlockSpec((r, DH), lambda c, i: (i, u0 + c)),
                  pl.BlockSpec((2, 1, SW), lambda c, i: (0, 0, c)),
                  pl.BlockSpec((2, None, DH, SW), lambda c, i: (0, c, 0, 0)),
                  pl.BlockSpec((2, None, SW, DH), lambda c, i: (0, c, 0, 0))],
        out_specs=[pl.BlockSpec((r, DH), lambda c, i: (i, c)),
                   pl.BlockSpec((None, 8, SW), lambda c, i: (i, 0, c))],
        out_shape=[jax.ShapeDtypeStruct((t, D), F32), jax.ShapeDtypeStruct((nt, 8, NG * NS), F32)],
        scratch_shapes=scratch,
        compiler_params=_cparams(("parallel", "arbitrary")),
    )(proj, lam, bblk, cblk)


def _s5_bwd(proj, lam, bblk, cblk, carries, dy, du_skip):
    t = proj.shape[0]
    r = _s5_tile_rows(t)
    nt = t // r
    u0 = 4 * D // DH

    def body(u_ref, lam_ref, b_ref, c_ref, car_ref, dy_ref, dus_ref, du_ref, dlam_ref, db_ref, dc_ref, dst_ref,
             hr_ref, hi_ref, ar_ref, ai_ref, cr_ref, ci_ref):
        first = pl.program_id(1) == 0

        @pl.when(first)
        def _():
            dst_ref[...] = jnp.zeros_like(dst_ref)

        u, dy = u_ref[...], dy_ref[...]
        lr, li = lam_ref[0], lam_ref[1]
        hr, hi = _s5_states(u, lam_ref, b_ref, car_ref, hr_ref, hi_ref, cr_ref, ci_ref)
        dcr2, dci2 = _mm_tn(hr, dy), -_mm_tn(hi, dy)
        last = _rows((r, SW)) == r - 1
        inr, ini = _cmul(lr, -li, dst_ref[0:1, :], dst_ref[1:2, :])
        dhr = _mm_nt(dy, c_ref[0]) + jnp.where(last, inr, 0.0)
        dhi = jnp.where(last, ini, 0.0) - _mm_nt(dy, c_ref[1])
        ar, ai = _scan_tile(dhr, dhi, lr, -li, ar_ref, ai_ref, cr_ref, ci_ref, True)
        top = _rows((r, SW)) == 0
        dst_ref[0:1, :] = jnp.sum(jnp.where(top, ar, 0.0), axis=0, keepdims=True)
        dst_ref[1:2, :] = jnp.sum(jnp.where(top, ai, 0.0), axis=0, keepdims=True)
        du_ref[...] = _mm_nt(ar, b_ref[0]) + _mm_nt(ai, b_ref[1]) + dus_ref[...]
        dbr, dbi = _mm_tn(u, ar), _mm_tn(u, ai)
        pr = _sd(hr, 1) + jnp.where(top, car_ref[0:1, :], 0.0)
        pi = _sd(hi, 1) + jnp.where(top, car_ref[1:2, :], 0.0)
        dlr = jnp.sum(ar * pr + ai * pi, axis=0, keepdims=True)
        dli = jnp.sum(ai * pr - ar * pi, axis=0, keepdims=True)

        @pl.when(first)
        def _():
            dlam_ref[0] = dlr
            dlam_ref[1] = dli
            db_ref[0] = dbr
            db_ref[1] = dbi
            dc_ref[0] = dcr2
            dc_ref[1] = dci2

        @pl.when(jnp.logical_not(first))
        def _():
            dlam_ref[0] += dlr
            dlam_ref[1] += dli
            db_ref[0] += dbr
            db_ref[1] += dbi
            dc_ref[0] += dcr2
            dc_ref[1] += dci2

    return pl.pallas_call(
        body, name="s5_bwd", grid=(NCB, nt),
        in_specs=[pl.BlockSpec((r, DH), lambda c, i: (nt - 1 - i, u0 + c)),
                  pl.BlockSpec((2, 1, SW), lambda c, i: (0, 0, c)),
                  pl.BlockSpec((2, None, DH, SW), lambda c, i: (0, c, 0, 0)),
                  pl.BlockSpec((2, None, SW, DH), lambda c, i: (0, c, 0, 0)),
                  pl.BlockSpec((None, 8, SW), lambda c, i: (nt - 1 - i, 0, c)),
                  pl.BlockSpec((r, DH), lambda c, i: (nt - 1 - i, c)),
                  pl.BlockSpec((r, DH), lambda c, i: (nt - 1 - i, c))],
        out_specs=[pl.BlockSpec((r, DH), lambda c, i: (nt - 1 - i, c)),
                   pl.BlockSpec((2, 1, SW), lambda c, i: (0, 0, c)),
                   pl.BlockSpec((2, None, DH, SW), lambda c, i: (0, c, 0, 0)),
                   pl.BlockSpec((2, None, SW, DH), lambda c, i: (0, c, 0, 0))],
        out_shape=[jax.ShapeDtypeStruct((t, D), F32), jax.ShapeDtypeStruct((2, 1, NG * NS), F32),
                   jax.ShapeDtypeStruct((2, NCB, DH, SW), F32), jax.ShapeDtypeStruct((2, NCB, SW, DH), F32)],
        scratch_shapes=[pltpu.VMEM((8, SW), F32)] + [pltpu.VMEM((SW // DH, r, DH), F32)] * 4
        + [pltpu.VMEM((r // SUB, SW), F32)] * 2,
        compiler_params=_cparams(("parallel", "arbitrary")),
    )(proj, lam, bblk, cblk, carries, dy, du_skip)


def _proj_spec(tm, col):
    return pl.BlockSpec((tm, D), lambda i: (i, col))


def _layer_mat(l):
    return pl.BlockSpec((None, D, D), lambda i: (l, 0, 0))


def _mix_fwd(proj, o, s5y, x, hn, dvec, wglu, bglu, wout, npost, l):
    t = x.shape[0]
    tm = _tile(t, 256)

    def body(za_ref, u_ref, zb_ref, ra_ref, rb_ref, o_ref, y_ref, x_ref, hn_ref, d_ref, wg_ref, bg_ref, wo_ref,
             np_ref, xn_ref):
        y0 = _mix_pre(y_ref[...], u_ref[...], d_ref[...])
        gl = _mm(y0, wg_ref[...]) + bg_ref[...]
        m = _mix_mid(o_ref[...], za_ref[...], y0, gl, zb_ref[...], ra_ref[...], rb_ref[...], hn_ref[...])
        out = _mm(m, wo_ref[...])
        xn_ref[...] = _mix_post(x_ref[...], out, np_ref[...])

    act = pl.BlockSpec((tm, D), lambda i: (i, 0))
    return pl.pallas_call(
        body, name="mix_fwd", grid=(t // tm,),
        in_specs=[_proj_spec(tm, 3), _proj_spec(tm, 4), _proj_spec(tm, 5), _proj_spec(tm, 6), _proj_spec(tm, 7),
                  act, act, act, _full((1, DH)), _full((1, D)), _layer_mat(l), _full((1, D)), _layer_mat(l),
                  _full((1, D))],
        out_specs=act,
        out_shape=jax.ShapeDtypeStruct((t, D), F32),
        compiler_params=_cparams(("parallel",)),
    )(proj, proj, proj, proj, proj, o, s5y, x, hn, dvec, wglu, bglu, wout, npost)


def _mix_bwd(proj, o, s5y, x, hn, dvec, wglu, bglu, wout, npost, dxn, l):
    t = x.shape[0]
    tm = _tile(t, 128)

    def body(za_ref, u_ref, zb_ref, ra_ref, rb_ref, o_ref, y_ref, x_ref, hn_ref, d_ref, wg_ref, bg_ref, wo_ref,
             np_ref, dxn_ref,
             dza_ref, du_ref, dzb_ref, dra_ref, drb_ref, do_ref, dy_ref, dx_ref,
             dwg_ref, dwo_ref, dvecs_ref, dhn_ref):
        y0, vjp_pre = jax.vjp(_mix_pre, y_ref[...], u_ref[...], d_ref[...])
        gl = _mm(y0, wg_ref[...]) + bg_ref[...]
        m, vjp_mid = jax.vjp(_mix_mid, o_ref[...], za_ref[...], y0, gl, zb_ref[...], ra_ref[...], rb_ref[...],
                             hn_ref[...])
        out = _mm(m, wo_ref[...])
        _, vjp_post = jax.vjp(_mix_post, x_ref[...], out, np_ref[...])
        dx, dout, dnp = vjp_post(dxn_ref[...])
        dm = _mm_nt(dout, wo_ref[...])
        dwo = _mm_tn(m, dout)
        do, dza, dy0, dgl, dzb, dra, drb, dhn = vjp_mid(dm)
        dwg = _mm_tn(y0, dgl)
        dbg = jnp.sum(dgl, axis=0, keepdims=True)
        dy0 = dy0 + _mm_nt(dgl, wg_ref[...])
        dy, du, dd = vjp_pre(dy0)
        dza_ref[...] = dza
        du_ref[...] = du
        dzb_ref[...] = dzb
        dra_ref[...] = dra
        drb_ref[...] = drb
        do_ref[...] = do
        dy_ref[...] = dy
        dx_ref[...] = dx
        first = pl.program_id(0) == 0

        @pl.when(first)
        def _():
            dwg_ref[...] = dwg
            dwo_ref[...] = dwo
            dvecs_ref[...] = jnp.zeros_like(dvecs_ref)
            dhn_ref[...] = jnp.zeros_like(dhn_ref)

        @pl.when(jnp.logical_not(first))
        def _():
            dwg_ref[...] += dwg
            dwo_ref[...] += dwo

        dvecs_ref[0:1, :] += dd
        dvecs_ref[1:2, :] += dbg
        dvecs_ref[2:3, :] += dnp
        dhn_ref[0:1, :] += dhn

    act = pl.BlockSpec((tm, D), lambda i: (i, 0))
    a = jax.ShapeDtypeStruct((t, D), F32)
    w = jax.ShapeDtypeStruct((D, D), F32)
    return pl.pallas_call(
        body, name="mix_bwd", grid=(t // tm,),
        in_specs=[_proj_spec(tm, 3), _proj_spec(tm, 4), _proj_spec(tm, 5), _proj_spec(tm, 6), _proj_spec(tm, 7),
                  act, act, act, _full((1, DH)), _full((1, D)), _layer_mat(l), _full((1, D)), _layer_mat(l),
                  _full((1, D)), act],
        out_specs=[act] * 8 + [_full((D, D)), _full((D, D)), _full((8, D)), _full((8, DH))],
        out_shape=[a] * 8 + [w, w, jax.ShapeDtypeStruct((8, D), F32), jax.ShapeDtypeStruct((8, DH), F32)],
        compiler_params=_cparams(("arbitrary",)),
    )(proj, proj, proj, proj, proj, o, s5y, x, hn, dvec, wglu, bglu, wout, npost, dxn)


def _loss_grad(y, target):
    t = y.shape[0]
    tm = _tile(t, 512)

    def body(y_ref, t_ref, dy_ref, l_ref):
        err = y_ref[...] - t_ref[...]
        dy_ref[...] = err * (1.0 / D)
        part = jnp.sum(jnp.sum(err * err, axis=1, keepdims=True), axis=0, keepdims=True) * (0.5 / D)
        part = jnp.broadcast_to(part, (8, DH))

        @pl.when(pl.program_id(0) == 0)
        def _():
            l_ref[...] = part

        @pl.when(pl.program_id(0) > 0)
        def _():
            l_ref[...] += part

    act = pl.BlockSpec((tm, D), lambda i: (i, 0))
    return pl.pallas_call(
        body, name="loss_grad", grid=(t // tm,),
        in_specs=[act, act], out_specs=[act, _full((8, DH))],
        out_shape=[jax.ShapeDtypeStruct((t, D), F32), jax.ShapeDtypeStruct((8, DH), F32)],
        compiler_params=_cparams(("arbitrary",)),
    )(y, target)


def _flips(rel):
    x, y, c = lax.axis_index("x"), lax.axis_index("y"), lax.axis_index("c")
    fx, fy, fc = rel
    return (x ^ fx if fx else x, y ^ fy if fy else y, c ^ fc if fc else c)


CHIP_RELS = ((1, 0, 0), (0, 1, 0), (1, 1, 0))
ALL_RELS = tuple((fx, fy, fc) for fx in (0, 1) for fy in (0, 1) for fc in (0, 1) if (fx, fy, fc) != (0, 0, 0))


def _slot_of(pos, by_chip):
    px, py, pc = pos
    return 2 * px + py if by_chip else 4 * px + 2 * py + pc


def _exchange(srcs, rels, by_chip, scatter, name):
    nslot = NCHIP if by_chip else NDEV
    narr, nrel = len(srcs), len(rels)

    def body(*refs):
        src_refs, dst_refs = refs[:narr], refs[narr:2 * narr]
        send_sems, recv_sems, local_sems = refs[2 * narr:]
        me = _flips((0, 0, 0))
        my_slot = _slot_of(me, by_chip)
        started = []
        for a, (src_ref, dst_ref) in enumerate(zip(src_refs, dst_refs)):
            mine = pltpu.make_async_copy(src_ref.at[my_slot] if scatter else src_ref, dst_ref.at[my_slot],
                                         local_sems.at[a])
            mine.start()
            started.append(mine)
        sends = []
        for a, (src_ref, dst_ref) in enumerate(zip(src_refs, dst_refs)):
            for k, rel in enumerate(rels):
                peer = _flips(rel)
                part = src_ref.at[_slot_of(peer, by_chip)] if scatter else src_ref
                cp = pltpu.make_async_remote_copy(
                    src_ref=part, dst_ref=dst_ref.at[my_slot], send_sem=send_sems.at[a * nrel + k],
                    recv_sem=recv_sems.at[a * nrel + k], device_id=peer, device_id_type=pl.DeviceIdType.MESH)
                cp.start()
                sends.append(cp)
        for a, (src_ref, dst_ref) in enumerate(zip(src_refs, dst_refs)):
            for k, rel in enumerate(rels):
                peer = _flips(rel)
                part = src_ref.at[0] if scatter else src_ref
                pltpu.make_async_remote_copy(
                    src_ref=part, dst_ref=dst_ref.at[_slot_of(peer, by_chip)], send_sem=send_sems.at[a * nrel + k],
                    recv_sem=recv_sems.at[a * nrel + k], device_id=peer,
                    device_id_type=pl.DeviceIdType.MESH).wait_recv()
        for cp in sends:
            cp.wait_send()
        for mine in started:
            mine.wait()

    return pl.pallas_call(
        body, name=name,
        in_specs=[pl.BlockSpec(memory_space=pl.ANY)] * narr,
        out_specs=[pl.BlockSpec(memory_space=pl.ANY)] * narr,
        out_shape=[jax.ShapeDtypeStruct((nslot,) + s.shape[-2:], s.dtype) for s in srcs],
        scratch_shapes=[pltpu.SemaphoreType.DMA((narr * nrel,)), pltpu.SemaphoreType.DMA((narr * nrel,)),
                        pltpu.SemaphoreType.DMA((narr,))],
    )(*srcs)


def _sibling_swap(srcs, name):
    narr = len(srcs)

    def body(*refs):
        src_refs, dst_refs = refs[:narr], refs[narr:2 * narr]
        send_sems, recv_sems = refs[2 * narr:]
        peer = _flips((0, 0, 1))
        copies = [pltpu.make_async_remote_copy(src_ref=s, dst_ref=d, send_sem=send_sems.at[a], recv_sem=recv_sems.at[a],
                                               device_id=peer, device_id_type=pl.DeviceIdType.MESH)
                  for a, (s, d) in enumerate(zip(src_refs, dst_refs))]
        for cp in copies:
            cp.start()
        for cp in copies:
            cp.wait()

    return pl.pallas_call(
        body, name=name,
        in_specs=[pl.BlockSpec(memory_space=pl.ANY)] * narr,
        out_specs=[pl.BlockSpec(memory_space=pl.ANY)] * narr,
        out_shape=[jax.ShapeDtypeStruct(s.shape, s.dtype) for s in srcs],
        scratch_shapes=[pltpu.SemaphoreType.DMA((narr,)), pltpu.SemaphoreType.DMA((narr,))],
    )(*srcs)


def _sum_slots(parts, name):
    ns, rows, cols = parts.shape
    tr = _row_tile(rows, 256)

    def body(p_ref, o_ref):
        acc = p_ref[0].astype(F32)
        for s in range(1, ns):
            acc = acc + p_ref[s].astype(F32)
        o_ref[...] = acc

    return pl.pallas_call(
        body, name=name, grid=(rows // tr,),
        in_specs=[pl.BlockSpec((ns, tr, cols), lambda i: (0, i, 0))],
        out_specs=pl.BlockSpec((tr, cols), lambda i: (i, 0)),
        out_shape=jax.ShapeDtypeStruct((rows, cols), F32),
        compiler_params=_cparams(("parallel",)),
    )(parts)


def _adamw(w, g_parts, m, v, name, max_rows=256):
    rows, cols = w.shape
    tr = _row_tile(rows, max_rows)
    c1 = 1.0 / (1.0 - ADAM_B1 ** ADAM_STEP)
    c2 = 1.0 / (1.0 - ADAM_B2 ** ADAM_STEP)
    npart = len(g_parts)

    def body(*refs):
        w_ref, m_ref, v_ref = refs[:3]
        g_refs = refs[3:3 + npart]
        go_ref, d_ref, nm_ref, nv_ref = refs[3 + npart:]
        terms = []
        for g_ref in g_refs:
            terms += [g_ref[...]] if len(g_ref.shape) == 2 else [g_ref[s] for s in range(g_ref.shape[0])]
        g = terms[0]
        for term in terms[1:]:
            g = g + term
        nm = ADAM_B1 * m_ref[...] + (1.0 - ADAM_B1) * g
        nv = ADAM_B2 * v_ref[...] + (1.0 - ADAM_B2) * (g * g)
        d_ref[...] = -ADAM_LR * ((nm * c1) / (jnp.sqrt(nv * c2) + ADAM_EPS) + ADAM_WD * w_ref[...])
        go_ref[...] = g
        nm_ref[...] = nm
        nv_ref[...] = nv

    blk = pl.BlockSpec((tr, cols), lambda i: (i, 0))
    g_specs = [blk if p.ndim == 2 else pl.BlockSpec((p.shape[0], tr, cols), lambda i: (0, i, 0)) for p in g_parts]
    out = jax.ShapeDtypeStruct((rows, cols), F32)
    return pl.pallas_call(
        body, name=name, grid=(rows // tr,),
        in_specs=[blk, blk, blk] + g_specs,
        out_specs=[blk] * 4, out_shape=[out] * 4,
        compiler_params=_cparams(("parallel",)),
    )(w, m, v, *g_parts)


WIN_SHARD = 2052
CONV_SHARD = 768
ROW_SHARD = 256

SMALL = (("norm_pre", (DEPTH, D)), ("a_log", (DEPTH, NH)), ("dt_bias", (DEPTH, NH)), ("head_norm", (DEPTH, DH)),
         ("ssm_a_re", (DEPTH, NG, NS)), ("ssm_a_im", (DEPTH, NG, NS)), ("ssm_log_dt", (DEPTH, NG)),
         ("ssm_b_re", (DEPTH, NG, NS, GS)), ("ssm_b_im", (DEPTH, NG, NS, GS)),
         ("ssm_c_re", (DEPTH, NG, GS, NS)), ("ssm_c_im", (DEPTH, NG, GS, NS)), ("ssm_d", (DEPTH, D)),
         ("b_glu", (DEPTH, D)), ("norm_post", (DEPTH, D)))


def _pad_rows(flat, rows):
    return jnp.pad(flat, (0, rows * D - flat.shape[0])).reshape(rows, D)


def _rows_by_chip(a):
    nl, rows, cols = a.shape
    return a.reshape(nl, NCHIP, rows // NCHIP, cols).transpose(1, 0, 2, 3).reshape(NCHIP, -1, cols)


def _rows_from_chips(a):
    _, rows, cols = a.shape
    return a.reshape(NCHIP, DEPTH, rows // DEPTH, cols).transpose(1, 0, 2, 3).reshape(DEPTH, -1, cols)


def _cols_by_chip(a):
    nl, rows, cols = a.shape
    return a.reshape(nl, rows, NCHIP, cols // NCHIP).transpose(2, 0, 1, 3).reshape(NCHIP, nl * rows, -1)


def _cols_from_chips(a, nl):
    _, rows, cols = a.shape
    return a.reshape(NCHIP, nl, rows // nl, cols).transpose(1, 2, 0, 3).reshape(nl, rows // nl, NCHIP * cols)


SMALL_ROWS = sum(-(-math.prod(s) // (8 * D)) * 8 for _, s in SMALL)
CONV_ROWS = DEPTH * 4 * 3 * D // D


def _pack_small(vals, extra=()):
    parts = []
    for val in tuple(vals) + tuple(extra):
        n = val.size
        parts.append(_pad_rows(val.reshape(-1), -(-n // (8 * D)) * 8))
    return jnp.concatenate(parts, axis=0)


def _unpack_small(flat):
    outs, r0 = [], 0
    for _, shape in SMALL:
        n = math.prod(shape)
        rows = -(-n // (8 * D)) * 8
        outs.append(flat[r0:r0 + rows].reshape(-1)[:n].reshape(shape))
        r0 += rows
    return outs


def _rearrange_cols(w):
    pad = jnp.zeros(w.shape[:-1] + (NCOL - BD0 - 2 * NH,), w.dtype)
    return jnp.concatenate([w[..., :4 * D], w[..., 4 * D + 2 * NH:], w[..., 4 * D:4 * D + 2 * NH], pad], axis=-1)


def _restore_cols(w):
    return jnp.concatenate([w[..., :4 * D], w[..., BD0:BD0 + 2 * NH], w[..., 4 * D:BD0]], axis=-1)


def _block_diag_b(bb2):
    b = bb2.reshape(NCB, GPB, NS, GS)
    eye = jnp.eye(GPB, dtype=F32)
    return jnp.einsum("kgnc,gh->kgchn", b, eye).reshape(NCB, GPB * GS, SW)


def _block_diag_b_t(d):
    return jnp.einsum("kgchn,gh->kgnc", d.reshape(NCB, GPB, GS, GPB, NS), jnp.eye(GPB, dtype=F32)).reshape(NG, NS * GS)


def _block_diag_c(c):
    eye = jnp.eye(GPB, dtype=F32)
    return jnp.einsum("kgcn,gh->kgnhc", c.reshape(NCB, GPB, GS, NS), eye).reshape(NCB, SW, GPB * GS)


def _block_diag_c_t(d):
    return jnp.einsum("kgnhc,gh->kgcn", d.reshape(NCB, GPB, NS, GPB, GS), jnp.eye(GPB, dtype=F32)).reshape(NG, GS, NS)


def _local_step(x, target, wcat, conv, wglu, wout, small):
    ar, ai = small["ssm_a_re"], small["ssm_a_im"]
    ldt = small["ssm_log_dt"].reshape(DEPTH, NG, 1)
    br2 = small["ssm_b_re"].reshape(DEPTH, NG, NS * GS)
    bi2 = small["ssm_b_im"].reshape(DEPTH, NG, NS * GS)
    lr, li, bbr2, bbi2 = _s5_params(ar, ai, ldt, br2, bi2)

    def row(name, l, width):
        return small[name][l].reshape(1, width)

    saved = []
    for l in range(DEPTH):
        gvec = jnp.pad(jnp.stack([small["a_log"][l], small["dt_bias"][l]]), ((0, 6), (NH, DH - 2 * NH)))
        lam = jnp.stack([lr[l].reshape(1, NG * NS), li[l].reshape(1, NG * NS)])
        bblk = jnp.stack([_block_diag_b(bbr2[l]), _block_diag_b(bbi2[l])])
        cblk = jnp.stack([_block_diag_c(small["ssm_c_re"][l]), _block_diag_c(small["ssm_c_im"][l])])
        proj, h = _inproj_fwd(x, row("norm_pre", l, D), wcat, l)
        qkv = _prep_fwd(proj, conv[l])
        bb, gcb = _gates_fwd(proj, gvec)
        local, t_inv = _delta_local_fwd(qkv, bb, gcb)
        o, states = _delta_state_fwd(local, gcb)
        s5y, carries = _s5_fwd(proj, lam, bblk, cblk)
        xn = _mix_fwd(proj, o, s5y, x, row("head_norm", l, DH), row("ssm_d", l, D), wglu, row("b_glu", l, D),
                      wout, row("norm_post", l, D), l)
        saved.append((x, proj, h, qkv, bb, gcb, local, t_inv, o, states, s5y, carries, gvec, lam, bblk, cblk))
        x = xn

    dx, loss_part = _loss_grad(x, target)

    g = {k: [None] * DEPTH for k in ("wcat", "conv", "wglu", "wout", "norm_pre", "a_log", "dt_bias", "head_norm",
                                     "ssm_c_re", "ssm_c_im", "ssm_d", "b_glu", "norm_post", "lr", "li", "bbr", "bbi")}
    for l in reversed(range(DEPTH)):
        xl, proj, h, qkv, bb, gcb, local, t_inv, o, states, s5y, carries, gvec, lam, bblk, cblk = saved[l]
        (dza, du_skip, dzb, dra, drb, do, ds5y, dxres, dwg, dwo, dvecs, dhn) = _mix_bwd(
            proj, o, s5y, xl, row("head_norm", l, DH), row("ssm_d", l, D), wglu, row("b_glu", l, D), wout,
            row("norm_post", l, D), dx, l)
        du, dlam, dbblk, dcblk = _s5_bwd(proj, lam, bblk, cblk, carries, ds5y, du_skip)
        *dlocal, dgcb_state = _delta_state_bwd(local, gcb, states, do)
        dq, dk, dv, dbb, dgcb = _delta_local_bwd(qkv, bb, gcb, t_inv, dlocal, dgcb_state)
        dbd, dgvec = _gates_bwd(proj, gvec, dbb, dgcb)
        dqkv = jnp.concatenate([dq, dk, dv], axis=0)
        dpre, dconv = _prep_bwd(proj, conv[l], dqkv)
        dproj = jnp.concatenate([dpre, dza, du, dzb, dra, drb, dbd], axis=1)
        dx, dgain = _inproj_bwd_dx(dproj, wcat, xl, row("norm_pre", l, D), dxres, l)
        g["wcat"][l] = _inproj_bwd_dw(h, dproj)
        g["conv"][l], g["wglu"][l], g["wout"][l] = dconv, dwg, dwo
        g["norm_pre"][l] = dgain[0]
        g["a_log"][l], g["dt_bias"][l] = dgvec[0, NH:2 * NH], dgvec[1, NH:2 * NH]
        g["head_norm"][l] = dhn[0]
        g["ssm_d"][l], g["b_glu"][l], g["norm_post"][l] = dvecs[0], dvecs[1], dvecs[2]
        g["ssm_c_re"][l], g["ssm_c_im"][l] = _block_diag_c_t(dcblk[0]), _block_diag_c_t(dcblk[1])
        g["lr"][l], g["li"][l] = dlam[0].reshape(NG, NS), dlam[1].reshape(NG, NS)
        g["bbr"][l], g["bbi"][l] = _block_diag_b_t(dbblk[0]), _block_diag_b_t(dbblk[1])
    g = {k: jnp.stack(v) for k, v in g.items()}
    dar, dai, dldt, dbr2, dbi2 = _s5_params_bwd(ar, ai, ldt, br2, bi2, g["lr"], g["li"], g["bbr"], g["bbi"])
    g["ssm_a_re"], g["ssm_a_im"], g["ssm_log_dt"] = dar, dai, dldt.reshape(DEPTH, NG)
    g["ssm_b_re"] = dbr2.reshape(DEPTH, NG, NS, GS)
    g["ssm_b_im"] = dbi2.reshape(DEPTH, NG, NS, GS)
    return loss_part[0, 0], dx, g


def kernel(x, norm_pre, w_in, conv_w, a_log, dt_bias, head_norm, ssm_a_re, ssm_a_im, ssm_log_dt, ssm_b_re, ssm_b_im, ssm_c_re, ssm_c_im, ssm_d, w_glu, b_glu, w_out, norm_post, loss_target, m_norm_pre, m_w_in, m_conv_w, m_a_log, m_dt_bias, m_head_norm, m_ssm_a_re, m_ssm_a_im, m_ssm_log_dt, m_ssm_b_re, m_ssm_b_im, m_ssm_c_re, m_ssm_c_im, m_ssm_d, m_w_glu, m_b_glu, m_w_out, m_norm_post, v_norm_pre, v_w_in, v_conv_w, v_a_log, v_dt_bias, v_head_norm, v_ssm_a_re, v_ssm_a_im, v_ssm_log_dt, v_ssm_b_re, v_ssm_b_im, v_ssm_c_re, v_ssm_c_im, v_ssm_d, v_w_glu, v_b_glu, v_w_out, v_norm_post):
    args = dict(locals())
    small = {n: args[n] for n, _ in SMALL}

    def flat2(a):
        return a.reshape(-1, a.shape[-1])

    g_in, g_glu, g_out, g_conv = _exchange(
        [flat2(w_in).astype(BF16), flat2(w_glu).astype(BF16), flat2(w_out).astype(BF16), flat2(conv_w)],
        CHIP_RELS, True, False, "gather_weights")
    wcat = _rearrange_cols(_cols_from_chips(g_in, DEPTH))
    wglu, wout = _rows_from_chips(g_glu), _rows_from_chips(g_out)
    conv = _cols_from_chips(g_conv, DEPTH)

    loss_part, dx, g = _local_step(x[0], loss_target[0], wcat, conv, wglu, wout, small)
    loss = lax.psum(loss_part, ("x", "y", "c"))

    from_chips = _exchange(
        [_cols_by_chip(_restore_cols(g["wcat"])).astype(BF16), _rows_by_chip(g["wglu"]).astype(BF16),
         _rows_by_chip(g["wout"]).astype(BF16)], CHIP_RELS, True, True, "scatter_grads")
    core_sums = [_sum_slots(p, "sum_chips_" + n) for p, n in zip(from_chips, ("in", "glu", "out"))]
    others = _sibling_swap(core_sums, "swap_cores")
    sharded = {}
    for n, mine, other in zip(("w_in", "w_glu", "w_out"), core_sums, others):
        sharded[n] = _adamw(flat2(args[n]), [mine, other], flat2(args["m_" + n]), flat2(args["v_" + n]), "adamw_" + n,
                            max_rows=128)

    (small_parts,) = _exchange([_pack_small([g[n] for n, _ in SMALL], extra=[g["conv"]])], ALL_RELS, False, False,
                               "gather_small")
    small_out = _adamw(_pack_small([args[n] for n, _ in SMALL]), [small_parts],
                       _pack_small([args["m_" + n] for n, _ in SMALL]),
                       _pack_small([args["v_" + n] for n, _ in SMALL]), "adamw_small")
    chip = 2 * lax.axis_index("x") + lax.axis_index("y")
    conv_parts = small_parts[:, SMALL_ROWS:SMALL_ROWS + CONV_ROWS].reshape(NDEV, DEPTH * 4, 3 * D)
    conv_parts = lax.dynamic_slice_in_dim(conv_parts, chip * CONV_SHARD, CONV_SHARD, axis=2)
    sharded["conv_w"] = _adamw(flat2(conv_w), [conv_parts], flat2(m_conv_w), flat2(v_conv_w), "adamw_conv")

    names = ["norm_pre", "w_in", "conv_w", "a_log", "dt_bias", "head_norm", "ssm_a_re", "ssm_a_im", "ssm_log_dt",
             "ssm_b_re", "ssm_b_im", "ssm_c_re", "ssm_c_im", "ssm_d", "w_glu", "b_glu", "w_out", "norm_post"]
    outs = [loss, dx[None]]
    for i in range(4):
        sm = dict(zip([n for n, _ in SMALL], _unpack_small(small_out[i])))
        outs += [sharded[n][i].reshape(args[n].shape) if n in sharded else sm[n] for n in names]
    return tuple(outs)
```

```python
import functools
import math

import jax
import jax.numpy as jnp
from jax import lax
from jax.experimental import pallas as pl
from jax.experimental.pallas import tpu as pltpu

F32 = jnp.float32
BF16 = jnp.bfloat16
HI = lax.Precision.HIGHEST

D = 1024
NH = 8
DH = 128
CH = 128
NG = 64
GS = 16
NS = 64
GPB = 8
NCB = NG // GPB
SW = GPB * NS
NCOL = 8320
BD0 = 8192
EPS = 1e-6
DEPTH = 4
NCHIP = 4
NDEV = 8
VMEM_LIMIT = 56 * 1024 * 1024

ADAM_LR = 0.001
ADAM_B1 = 0.9
ADAM_B2 = 0.999
ADAM_EPS = 1e-08
ADAM_WD = 0.01
ADAM_STEP = 10


def _cparams(sem=None):
    return pltpu.CompilerParams(dimension_semantics=sem, vmem_limit_bytes=VMEM_LIMIT)


def _full(shape):
    nd = len(shape)
    return pl.BlockSpec(shape, lambda *_: (0,) * nd)


def _rms(x, gain):
    ms = jnp.mean(x * x, axis=-1, keepdims=True)
    return x * lax.rsqrt(ms + EPS) * gain


def _sigmoid(x):
    return 1.0 / (1.0 + jnp.exp(-x))


def _silu(x):
    return x * _sigmoid(x)


def _softplus(x):
    return jnp.maximum(x, 0.0) + jnp.log(1.0 + jnp.exp(-jnp.abs(x)))


def _gelu(x):
    return 0.5 * x * (1.0 + jnp.tanh(math.sqrt(2.0 / math.pi) * (x + 0.044715 * (x * x * x))))


def _dot_bf16(a, b, dims):
    return lax.dot_general(a.astype(BF16), b.astype(BF16), (dims, ((), ())), preferred_element_type=F32)


def _mm_nt(a, b):
    return _dot_bf16(a, b, ((1,), (1,)))


def _mm_tn(a, b):
    return _dot_bf16(a, b, ((0,), (0,)))


@jax.custom_vjp
def _mm(a, b):
    return _dot_bf16(a, b, ((1,), (0,)))


def _mm_fwd(a, b):
    return _dot_bf16(a, b, ((1,), (0,))), (a, b)


def _mm_bwd(res, ct):
    a, b = res
    return _mm_nt(ct, b).astype(a.dtype), _mm_tn(a, ct).astype(b.dtype)


_mm.defvjp(_mm_fwd, _mm_bwd)


@jax.custom_vjp
def _mm_nt_d(a, b):
    return _mm_nt(a, b)


def _mm_nt_d_bwd(res, ct):
    a, b = res
    return _dot_bf16(ct, b, ((1,), (0,))), _mm_tn(ct, a)


_mm_nt_d.defvjp(lambda a, b: (_mm_nt(a, b), (a, b)), _mm_nt_d_bwd)


@jax.custom_vjp
def _mm_tn_d(a, b):
    return _mm_tn(a, b)


def _mm_tn_d_bwd(res, ct):
    a, b = res
    return _mm_nt(b, ct), _dot_bf16(a, ct, ((1,), (0,)))


_mm_tn_d.defvjp(lambda a, b: (_mm_tn(a, b), (a, b)), _mm_tn_d_bwd)


def _split_bf16(a):
    hi = a.astype(BF16)
    return hi, (a - hi.astype(F32)).astype(BF16)


def _dot3(a, b, dims):
    ah, al = _split_bf16(a)
    bh, bl = _split_bf16(b)

    def dot(x, y):
        return lax.dot_general(x, y, (dims, ((), ())), preferred_element_type=F32)

    return dot(ah, bh) + (dot(ah, bl) + dot(al, bh))


@jax.custom_vjp
def _imm(a, b):
    return _dot3(a, b, ((1,), (0,)))


def _imm_bwd(res, ct):
    a, b = res
    return _dot3(ct, b, ((1,), (1,))), _dot3(a, ct, ((0,), (0,)))


_imm.defvjp(lambda a, b: (_dot3(a, b, ((1,), (0,))), (a, b)), _imm_bwd)


def _hmm(a, b):
    return jnp.dot(a, b, precision=HI, preferred_element_type=F32)


def _hmm_nt(a, b):
    return lax.dot_general(a, b, (((1,), (1,)), ((), ())), precision=HI, preferred_element_type=F32)


def _hmm_tn(a, b):
    return lax.dot_general(a, b, (((0,), (0,)), ((), ())), precision=HI, preferred_element_type=F32)


def _rows(shape):
    return lax.broadcasted_iota(jnp.int32, shape, 0)


def _cols(shape):
    return lax.broadcasted_iota(jnp.int32, shape, 1)


def _sd(x, s):
    return jnp.where(_rows(x.shape) >= s, pltpu.roll(x, s, axis=0), 0.0)


def _su(x, s):
    n = x.shape[0]
    return jnp.where(_rows(x.shape) < n - s, pltpu.roll(x, n - s, axis=0), 0.0)


@functools.partial(jax.custom_vjp, nondiff_argnums=(1,))
def _shift_down(x, s):
    return _sd(x, s)


def _shift_down_fwd(x, s):
    return _sd(x, s), None


def _shift_down_bwd(s, _, g):
    return (_su(g, s),)


_shift_down.defvjp(_shift_down_fwd, _shift_down_bwd)


def _last_row(x):
    n = x.shape[0]
    return jnp.sum(jnp.where(_rows(x.shape) == n - 1, x, 0.0), axis=0, keepdims=True)


def _prep_fn(p, w0, w1, w2, w3, qk):
    acc = w3 * p + w2 * _shift_down(p, 1) + w1 * _shift_down(p, 2) + w0 * _shift_down(p, 3)
    a = _silu(acc)
    nrm = lax.rsqrt(jnp.sum(a * a, axis=-1, keepdims=True) + EPS)
    return a * (nrm * qk + (1.0 - qk))


def _gates_fn(bd, av, bv):
    tm = bd.shape[0]
    beta_all = _sigmoid(bd)
    g_all = -jnp.exp(av) * _softplus(bd + bv)
    r, c = _rows((tm, tm)), _cols((tm, tm))
    tri = jnp.where((r // CH == c // CH) & (r >= c), 1.0, 0.0).astype(F32)
    gc_all = _hmm(tri, g_all)
    lane = _cols(bd.shape)
    outs = []
    for h in range(NH):
        b = jnp.sum(jnp.where(lane == h, beta_all, 0.0), axis=1, keepdims=True)
        outs.append(jnp.broadcast_to(b, bd.shape))
    for h in range(NH):
        g = jnp.sum(jnp.where(lane == NH + h, gc_all, 0.0), axis=1, keepdims=True)
        outs.append(jnp.broadcast_to(g, bd.shape))
    return tuple(outs)


def _unit_lower_inv(l_mat):
    n = l_mat.shape[0]
    eye = jnp.where(_rows((n, n)) == _cols((n, n)), 1.0, 0.0).astype(F32)
    p = -l_mat
    r = eye + p
    k = 1
    while 2 * k < n:
        p = _imm(p, p)
        r = r + _imm(r, p)
        k *= 2
    return r


@jax.custom_vjp
def _known_inverse(l_mat, t_inv):
    return t_inv


def _known_inverse_bwd(t_inv, ct):
    d_l = -_dot3(_dot3(t_inv, ct, ((0,), (0,))), t_inv, ((1,), (1,)))
    return d_l, jnp.zeros_like(t_inv)


_known_inverse.defvjp(lambda l_mat, t_inv: (t_inv, t_inv), _known_inverse_bwd)


def _chunk_local(q, k, v, bb, gcb, t_inv=None):
    qs = q * (DH ** -0.5)
    kb = k * bb
    eg = jnp.exp(gcb)
    ii, jj = _rows((CH, CH)), _cols((CH, CH))
    decay = jnp.exp(jnp.where(ii >= jj, gcb - gcb.T, -1e30))
    l_mat = jnp.where(ii > jj, _mm_nt_d(kb, k) * decay, 0.0)
    t_inv = _unit_lower_inv(l_mat) if t_inv is None else _known_inverse(l_mat, t_inv)
    u = _mm(t_inv, v * bb)
    w = _mm(t_inv, kb * eg)
    a_qk = _mm_nt_d(qs, k) * decay
    k_dec = k * jnp.exp(_last_row(gcb) - gcb)
    return (u, w, qs * eg, k_dec, a_qk), t_inv


def _state_step(u, w, q_dec, k_dec, a_qk, gcb, state):
    v_new = u - _mm(w, state)
    o = _mm(q_dec, state) + _mm(a_qk, v_new)
    new_state = state * jnp.exp(_last_row(gcb)) + _mm_tn_d(k_dec, v_new)
    return o, new_state


SUB = 8


def _cmul(ar, ai, br, bi):
    return ar * br - ai * bi, ar * bi + ai * br


def _scan_tile(xr, xi, mr, mi, hr_ref, hi_ref, cr_ref, ci_ref, reverse):
    n, width = xr.shape
    ngroups = n // SUB
    shift_groups = _su if reverse else _sd
    xr, xi = xr.reshape(ngroups, SUB, width), xi.reshape(ngroups, SUB, width)
    pr, pi = mr, mi
    tr, ti = jnp.broadcast_to(mr, (SUB, width)), jnp.broadcast_to(mi, (SUB, width))
    pos = _rows(tr.shape)
    s = 1
    while s < SUB:
        inside = pos < SUB - s if reverse else pos >= s
        shift = SUB - s if reverse else s
        qr, qi = jnp.where(inside, pr, 0.0)[None], jnp.where(inside, pi, 0.0)[None]
        dr, di = _cmul(qr, qi, pltpu.roll(xr, shift, axis=1), pltpu.roll(xi, shift, axis=1))
        xr, xi = xr + dr, xi + di
        er = jnp.where(inside, pltpu.roll(tr, shift, axis=0), 1.0)
        ei = jnp.where(inside, pltpu.roll(ti, shift, axis=0), 0.0)
        tr, ti = _cmul(tr, ti, er, ei)
        pr, pi = _cmul(pr, pi, pr, pi)
        s *= 2
    xr, xi = xr.reshape(n, width), xi.reshape(n, width)
    nlb = width // DH

    def lanes(x, j):
        return x[:, j * DH:(j + 1) * DH]

    for j in range(nlb):
        hr_ref[j] = lanes(xr, j)
        hi_ref[j] = lanes(xi, j)
    edge = pl.ds(0 if reverse else SUB - 1, ngroups, stride=SUB)
    gr = jnp.concatenate([hr_ref.at[j][edge, :] for j in range(nlb)], axis=1)
    gi = jnp.concatenate([hi_ref.at[j][edge, :] for j in range(nlb)], axis=1)
    s = 1
    while s < ngroups:
        dr, di = _cmul(pr, pi, shift_groups(gr, s), shift_groups(gi, s))
        gr, gi = gr + dr, gi + di
        pr, pi = _cmul(pr, pi, pr, pi)
        s *= 2
    cr_ref[...] = shift_groups(gr, 1)
    ci_ref[...] = shift_groups(gi, 1)
    for g in range(ngroups):
        rows = slice(g * SUB, (g + 1) * SUB)
        dr, di = _cmul(tr, ti, cr_ref[g:g + 1, :], ci_ref[g:g + 1, :])
        for j in range(nlb):
            hr_ref[j, rows, :] += lanes(dr, j)
            hi_ref[j, rows, :] += lanes(di, j)
    return (jnp.concatenate([hr_ref[j] for j in range(nlb)], axis=1),
            jnp.concatenate([hi_ref[j] for j in range(nlb)], axis=1))


def _s5_states(u, lam_ref, b_ref, car_ref, hr_ref, hi_ref, cr_ref, ci_ref):
    lr, li = lam_ref[0], lam_ref[1]
    first = _rows((u.shape[0], SW)) == 0
    inr, ini = _cmul(lr, li, car_ref[0:1, :], car_ref[1:2, :])
    xr = _mm(u, b_ref[0]) + jnp.where(first, inr, 0.0)
    xi = _mm(u, b_ref[1]) + jnp.where(first, ini, 0.0)
    return _scan_tile(xr, xi, lr, li, hr_ref, hi_ref, cr_ref, ci_ref, False)


def _s5_params_fn(ar, ai, ldt, br2, bi2):
    dt = jnp.exp(ldt)
    mag = jnp.exp(ar * dt)
    lr, li = mag * jnp.cos(ai * dt), mag * jnp.sin(ai * dt)
    den = ar * ar + ai * ai
    fr = ((lr - 1.0) * ar + li * ai) / den
    fi = (li * ar - (lr - 1.0) * ai) / den
    expand = jnp.where(_cols((NS, NS * GS)) // GS == _rows((NS, NS * GS)), 1.0, 0.0).astype(F32)
    fr2, fi2 = _hmm(fr, expand), _hmm(fi, expand)
    return lr, li, fr2 * br2 - fi2 * bi2, fr2 * bi2 + fi2 * br2


def _head_norm(o, hn):
    parts = []
    for h in range(NH):
        oh = o[:, h * DH:(h + 1) * DH]
        parts.append(oh * lax.rsqrt(jnp.mean(oh * oh, axis=-1, keepdims=True) + EPS) * hn)
    return jnp.concatenate(parts, axis=1)


def _mix_pre(s5y, u, dvec):
    return _gelu(s5y + dvec * u)


def _mix_mid(o, za, y0, gl, zb, ra, rb, hn):
    ya = _head_norm(o, hn) * _silu(za)
    yb = y0 * _sigmoid(gl) * _silu(zb)
    return _sigmoid(ra) * ya + _sigmoid(rb) * yb


def _mix_post(x, out, npost):
    return x + _rms(out, npost)


def _tile(t, want):
    return min(t, want)


def _row_tile(rows, want):
    return max(r for r in range(16, want + 1, 16) if rows % r == 0)


def _inproj_fwd(x, gain, wcat, l):
    t = x.shape[0]
    tm, tn = _tile(t, 1024), 640

    def body(x_ref, g_ref, w_ref, o_ref, h_ref):
        @pl.when(pl.program_id(1) == 0)
        def _():
            h_ref[...] = _rms(x_ref[...], g_ref[...]).astype(h_ref.dtype)
        o_ref[...] = _dot_bf16(h_ref[...], w_ref[...], ((1,), (0,)))

    return pl.pallas_call(
        body, name="inproj_fwd", grid=(t // tm, NCOL // tn),
        in_specs=[pl.BlockSpec((tm, D), lambda i, j: (i, 0)), _full((1, D)),
                  pl.BlockSpec((None, D, tn), lambda i, j: (l, 0, j))],
        out_specs=[pl.BlockSpec((tm, tn), lambda i, j: (i, j)), pl.BlockSpec((tm, D), lambda i, j: (i, 0))],
        out_shape=[jax.ShapeDtypeStruct((t, NCOL), F32), jax.ShapeDtypeStruct((t, D), wcat.dtype)],
        compiler_params=_cparams(("parallel", "arbitrary")),
    )(x, gain, wcat)


def _inproj_bwd_dx(dproj, wcat, x, gain, dxres, l):
    t = x.shape[0]
    tm, tk = _tile(t, 1024), 640
    nk = NCOL // tk

    def body(dp_ref, w_ref, x_ref, g_ref, r_ref, dx_ref, dg_ref, acc_ref):
        i, k = pl.program_id(0), pl.program_id(1)

        @pl.when(k == 0)
        def _():
            acc_ref[...] = jnp.zeros_like(acc_ref)

        acc_ref[...] += _mm_nt(dp_ref[...], w_ref[...])

        @pl.when(k == nk - 1)
        def _():
            _, vjp = jax.vjp(_rms, x_ref[...], g_ref[...])
            dx, dg = vjp(acc_ref[...])
            dx_ref[...] = r_ref[...] + dx

            @pl.when(i == 0)
            def _():
                dg_ref[...] = dg

            @pl.when(i > 0)
            def _():
                dg_ref[...] += dg

    return pl.pallas_call(
        body, name="inproj_bwd_dx", grid=(t // tm, nk),
        in_specs=[pl.BlockSpec((tm, tk), lambda i, k: (i, k)), pl.BlockSpec((None, D, tk), lambda i, k: (l, 0, k)),
                  pl.BlockSpec((tm, D), lambda i, k: (i, 0)), _full((1, D)),
                  pl.BlockSpec((tm, D), lambda i, k: (i, 0))],
        out_specs=[pl.BlockSpec((tm, D), lambda i, k: (i, 0)), _full((1, D))],
        out_shape=[jax.ShapeDtypeStruct((t, D), F32), jax.ShapeDtypeStruct((1, D), F32)],
        scratch_shapes=[pltpu.VMEM((tm, D), F32)],
        compiler_params=_cparams(("arbitrary", "arbitrary")),
    )(dproj, wcat, x, gain, dxres)


def _inproj_bwd_dw(h, dproj):
    t = h.shape[0]
    tm, tn = _tile(t, 512), 1664

    def body(h_ref, dp_ref, o_ref):
        @pl.when(pl.program_id(1) == 0)
        def _():
            o_ref[...] = jnp.zeros_like(o_ref)

        o_ref[...] += _mm_tn(h_ref[...], dp_ref[...])

    return pl.pallas_call(
        body, name="inproj_bwd_dw", grid=(NCOL // tn, t // tm),
        in_specs=[pl.BlockSpec((tm, D), lambda j, i: (i, 0)), pl.BlockSpec((tm, tn), lambda j, i: (i, j))],
        out_specs=pl.BlockSpec((D, tn), lambda j, i: (0, j)),
        out_shape=jax.ShapeDtypeStruct((D, NCOL), F32),
        compiler_params=_cparams(("parallel", "arbitrary")),
    )(h, dproj)


def _prep_fwd(proj, cw):
    t = proj.shape[0]

    def body(p_ref, w_ref, o_ref):
        qk = (pl.program_id(0) < 2 * NH).astype(F32)
        o_ref[...] = _prep_fn(p_ref[...], w_ref[0:1, :], w_ref[1:2, :], w_ref[2:3, :], w_ref[3:4, :], qk)

    return pl.pallas_call(
        body, name="prep_fwd", grid=(3 * NH,),
        in_specs=[pl.BlockSpec((t, DH), lambda c: (0, c)), pl.BlockSpec((4, DH), lambda c: (0, c))],
        out_specs=pl.BlockSpec((None, t, DH), lambda c: (c, 0, 0)),
        out_shape=jax.ShapeDtypeStruct((3 * NH, t, DH), F32),
        compiler_params=_cparams(("parallel",)),
    )(proj, cw)


def _prep_bwd(proj, cw, dqkv):
    t = proj.shape[0]

    def body(p_ref, w_ref, d_ref, dp_ref, dw_ref):
        qk = (pl.program_id(0) < 2 * NH).astype(F32)
        _, vjp = jax.vjp(lambda p, w0, w1, w2, w3: _prep_fn(p, w0, w1, w2, w3, qk),
                         p_ref[...], w_ref[0:1, :], w_ref[1:2, :], w_ref[2:3, :], w_ref[3:4, :])
        dp, dw0, dw1, dw2, dw3 = vjp(d_ref[...])
        dp_ref[...] = dp
        dw_ref[0:1, :] = dw0
        dw_ref[1:2, :] = dw1
        dw_ref[2:3, :] = dw2
        dw_ref[3:4, :] = dw3

    return pl.pallas_call(
        body, name="prep_bwd", grid=(3 * NH,),
        in_specs=[pl.BlockSpec((t, DH), lambda c: (0, c)), pl.BlockSpec((4, DH), lambda c: (0, c)),
                  pl.BlockSpec((None, t, DH), lambda c: (c, 0, 0))],
        out_specs=[pl.BlockSpec((t, DH), lambda c: (0, c)), pl.BlockSpec((4, DH), lambda c: (0, c))],
        out_shape=[jax.ShapeDtypeStruct((t, 3 * D), F32), jax.ShapeDtypeStruct((4, 3 * D), F32)],
        compiler_params=_cparams(("parallel",)),
    )(proj, cw, dqkv)


def _gates_fwd(proj, gvec):
    t = proj.shape[0]
    tm = _tile(t, 512)

    def body(p_ref, gv_ref, b_ref, g_ref):
        outs = _gates_fn(p_ref[...], gv_ref[0:1, :], gv_ref[1:2, :])
        for h in range(NH):
            b_ref[h] = outs[h]
            g_ref[h] = outs[NH + h]

    spec = pl.BlockSpec((NH, tm, DH), lambda i: (0, i, 0))
    return pl.pallas_call(
        body, name="gates_fwd", grid=(t // tm,),
        in_specs=[pl.BlockSpec((tm, DH), lambda i: (i, BD0 // DH)), _full((8, DH))],
        out_specs=[spec, spec],
        out_shape=[jax.ShapeDtypeStruct((NH, t, DH), F32)] * 2,
        compiler_params=_cparams(("parallel",)),
    )(proj, gvec)


def _gates_bwd(proj, gvec, dbb, dgcb):
    t = proj.shape[0]
    tm = _tile(t, 512)

    def body(p_ref, gv_ref, db_ref, dg_ref, dp_ref, dgv_ref):
        _, vjp = jax.vjp(_gates_fn, p_ref[...], gv_ref[0:1, :], gv_ref[1:2, :])
        cts = tuple(db_ref[h] for h in range(NH)) + tuple(dg_ref[h] for h in range(NH))
        dp, da, db = vjp(cts)
        dp_ref[...] = dp

        @pl.when(pl.program_id(0) == 0)
        def _():
            dgv_ref[...] = jnp.zeros_like(dgv_ref)

        dgv_ref[0:1, :] += da
        dgv_ref[1:2, :] += db

    spec = pl.BlockSpec((NH, tm, DH), lambda i: (0, i, 0))
    return pl.pallas_call(
        body, name="gates_bwd", grid=(t // tm,),
        in_specs=[pl.BlockSpec((tm, DH), lambda i: (i, BD0 // DH)), _full((8, DH)), spec, spec],
        out_specs=[pl.BlockSpec((tm, DH), lambda i: (i, 0)), _full((8, DH))],
        out_shape=[jax.ShapeDtypeStruct((t, DH), F32), jax.ShapeDtypeStruct((8, DH), F32)],
        compiler_params=_cparams(("arbitrary",)),
    )(proj, gvec, dbb, dgcb)


def _chunks_per_step(nch):
    return 2 if nch % 2 == 0 else 1


def _delta_local_fwd(qkv, bb, gcb):
    t = qkv.shape[1]
    cps = _chunks_per_step(t // CH)
    rows = cps * CH

    def body(q_ref, k_ref, v_ref, b_ref, g_ref, *out_refs):
        for c in range(cps):
            sl = slice(c * CH, (c + 1) * CH)
            outs, t_inv = _chunk_local(q_ref[sl, :], k_ref[sl, :], v_ref[sl, :], b_ref[sl, :], g_ref[sl, :])
            for ref, val in zip(out_refs, outs + (t_inv,)):
                ref[sl, :] = val

    def blk(off):
        return pl.BlockSpec((None, rows, DH), lambda h, n: (h + off, n, 0))

    outs = pl.pallas_call(
        body, name="delta_local_fwd", grid=(NH, t // rows),
        in_specs=[blk(0), blk(NH), blk(2 * NH), blk(0), blk(0)],
        out_specs=[blk(0)] * 6,
        out_shape=[jax.ShapeDtypeStruct((NH, t, DH), F32)] * 6,
        compiler_params=_cparams(("parallel", "parallel")),
    )(qkv, qkv, qkv, bb, gcb)
    return outs[:5], outs[5]


def _delta_local_bwd(qkv, bb, gcb, t_inv, cts, dgcb_state):
    t = qkv.shape[1]
    cps = _chunks_per_step(t // CH)
    rows = cps * CH

    def body(q_ref, k_ref, v_ref, b_ref, g_ref, ti_ref, du_ref, dw_ref, dqd_ref, dkd_ref, da_ref, dgs_ref,
             dq_ref, dk_ref, dv_ref, db_ref, dg_ref):
        for c in range(cps):
            sl = slice(c * CH, (c + 1) * CH)
            t_inv_c = ti_ref[sl, :]
            _, vjp = jax.vjp(lambda *a: _chunk_local(*a, t_inv=t_inv_c)[0],
                             q_ref[sl, :], k_ref[sl, :], v_ref[sl, :], b_ref[sl, :], g_ref[sl, :])
            dq, dk, dv, db, dg = vjp((du_ref[sl, :], dw_ref[sl, :], dqd_ref[sl, :], dkd_ref[sl, :], da_ref[sl, :]))
            dq_ref[sl, :] = dq
            dk_ref[sl, :] = dk
            dv_ref[sl, :] = dv
            db_ref[sl, :] = db
            dg_ref[sl, :] = dg + dgs_ref[sl, :]

    def blk(off):
        return pl.BlockSpec((None, rows, DH), lambda h, n: (h + off, n, 0))

    return pl.pallas_call(
        body, name="delta_local_bwd", grid=(NH, t // rows),
        in_specs=[blk(0), blk(NH), blk(2 * NH)] + [blk(0)] * 9,
        out_specs=[blk(0)] * 5,
        out_shape=[jax.ShapeDtypeStruct((NH, t, DH), F32)] * 5,
        compiler_params=_cparams(("parallel", "parallel")),
    )(qkv, qkv, qkv, bb, gcb, t_inv, *cts, dgcb_state)


def _delta_state_fwd(local, gcb):
    t = gcb.shape[1]
    nch = t // CH

    def body(u_ref, w_ref, qd_ref, kd_ref, a_ref, g_ref, o_ref, s_ref, st_ref):
        @pl.when(pl.program_id(0) == 0)
        def _():
            st_ref[...] = jnp.zeros_like(st_ref)

        for h in range(NH):
            s_ref[h] = st_ref[h]
            o, ns = _state_step(u_ref[h], w_ref[h], qd_ref[h], kd_ref[h], a_ref[h], g_ref[h], st_ref[h])
            o_ref[:, h * DH:(h + 1) * DH] = o
            st_ref[h] = ns

    blk = pl.BlockSpec((NH, CH, DH), lambda n: (0, n, 0))
    return pl.pallas_call(
        body, name="delta_state_fwd", grid=(nch,),
        in_specs=[blk] * 6,
        out_specs=[pl.BlockSpec((CH, D), lambda n: (n, 0)),
                   pl.BlockSpec((NH, None, DH, DH), lambda n: (0, n, 0, 0))],
        out_shape=[jax.ShapeDtypeStruct((t, D), F32), jax.ShapeDtypeStruct((NH, nch, DH, DH), F32)],
        scratch_shapes=[pltpu.VMEM((NH, DH, DH), F32)],
        compiler_params=_cparams(("arbitrary",)),
    )(*local, gcb)


def _delta_state_bwd(local, gcb, states, do):
    t = gcb.shape[1]
    nch = t // CH

    def body(u_ref, w_ref, qd_ref, kd_ref, a_ref, g_ref, s_ref, do_ref,
             du_ref, dw_ref, dqd_ref, dkd_ref, da_ref, dg_ref, ds_ref):
        @pl.when(pl.program_id(0) == 0)
        def _():
            ds_ref[...] = jnp.zeros_like(ds_ref)

        for h in range(NH):
            _, vjp = jax.vjp(_state_step, u_ref[h], w_ref[h], qd_ref[h], kd_ref[h], a_ref[h], g_ref[h], s_ref[h])
            du, dw, dqd, dkd, da, dg, ds = vjp((do_ref[:, h * DH:(h + 1) * DH], ds_ref[h]))
            du_ref[h] = du
            dw_ref[h] = dw
            dqd_ref[h] = dqd
            dkd_ref[h] = dkd
            da_ref[h] = da
            dg_ref[h] = dg
            ds_ref[h] = ds

    blk = pl.BlockSpec((NH, CH, DH), lambda n: (0, nch - 1 - n, 0))
    return pl.pallas_call(
        body, name="delta_state_bwd", grid=(nch,),
        in_specs=[blk] * 6 + [pl.BlockSpec((NH, None, DH, DH), lambda n: (0, nch - 1 - n, 0, 0)),
                              pl.BlockSpec((CH, D), lambda n: (nch - 1 - n, 0))],
        out_specs=[blk] * 6,
        out_shape=[jax.ShapeDtypeStruct((NH, t, DH), F32)] * 6,
        scratch_shapes=[pltpu.VMEM((NH, DH, DH), F32)],
        compiler_params=_cparams(("arbitrary",)),
    )(*local, gcb, states, do)


def _s5_params(ar, ai, ldt, br2, bi2):
    def body(ar_ref, ai_ref, ld_ref, br_ref, bi_ref, lr_ref, li_ref, bbr_ref, bbi_ref):
        lr, li, bbr, bbi = _s5_params_fn(ar_ref[...], ai_ref[...], ld_ref[...], br_ref[...], bi_ref[...])
        lr_ref[...] = lr
        li_ref[...] = li
        bbr_ref[...] = bbr
        bbi_ref[...] = bbi

    sq = pl.BlockSpec((None, NG, NS), lambda l: (l, 0, 0))
    wide = pl.BlockSpec((None, NG, NS * GS), lambda l: (l, 0, 0))
    return pl.pallas_call(
        body, name="s5_params", grid=(DEPTH,),
        in_specs=[sq, sq, pl.BlockSpec((None, NG, 1), lambda l: (l, 0, 0)), wide, wide],
        out_specs=[sq, sq, wide, wide],
        out_shape=[jax.ShapeDtypeStruct((DEPTH, NG, NS), F32)] * 2
        + [jax.ShapeDtypeStruct((DEPTH, NG, NS * GS), F32)] * 2,
        compiler_params=_cparams(("parallel",)),
    )(ar, ai, ldt, br2, bi2)


def _s5_params_bwd(ar, ai, ldt, br2, bi2, dlr, dli, dbbr, dbbi):
    def body(ar_ref, ai_ref, ld_ref, br_ref, bi_ref, a_ref, b_ref, c_ref, d_ref,
             dar_ref, dai_ref, dld_ref, dbr_ref, dbi_ref):
        _, vjp = jax.vjp(_s5_params_fn, ar_ref[...], ai_ref[...], ld_ref[...], br_ref[...], bi_ref[...])
        dar, dai, dld, dbr, dbi = vjp((a_ref[...], b_ref[...], c_ref[...], d_ref[...]))
        dar_ref[...] = dar
        dai_ref[...] = dai
        dld_ref[...] = dld
        dbr_ref[...] = dbr
        dbi_ref[...] = dbi

    sq = pl.BlockSpec((None, NG, NS), lambda l: (l, 0, 0))
    col = pl.BlockSpec((None, NG, 1), lambda l: (l, 0, 0))
    wide = pl.BlockSpec((None, NG, NS * GS), lambda l: (l, 0, 0))
    return pl.pallas_call(
        body, name="s5_params_bwd", grid=(DEPTH,),
        in_specs=[sq, sq, col, wide, wide, sq, sq, wide, wide],
        out_specs=[sq, sq, col, wide, wide],
        out_shape=[jax.ShapeDtypeStruct((DEPTH, NG, NS), F32)] * 2 + [jax.ShapeDtypeStruct((DEPTH, NG, 1), F32)]
        + [jax.ShapeDtypeStruct((DEPTH, NG, NS * GS), F32)] * 2,
        compiler_params=_cparams(("parallel",)),
    )(ar, ai, ldt, br2, bi2, dlr, dli, dbbr, dbbi)


def _s5_tile_rows(t):
    return _tile(t // 2, 256)


def _s5_fwd(proj, lam, bblk, cblk):
    t = proj.shape[0]
    r = _s5_tile_rows(t)
    nt = t // r
    u0 = 4 * D // DH

    def body(u_ref, lam_ref, b_ref, c_ref, y_ref, car_ref, st_ref, hr_ref, hi_ref, cr_ref, ci_ref):
        @pl.when(pl.program_id(1) == 0)
        def _():
            st_ref[...] = jnp.zeros_like(st_ref)

        car_ref[...] = st_ref[...]
        hr, hi = _s5_states(u_ref[...], lam_ref, b_ref, st_ref, hr_ref, hi_ref, cr_ref, ci_ref)
        y_ref[...] = _mm(hr, c_ref[0]) - _mm(hi, c_ref[1])
        st_ref[0:1, :] = _last_row(hr)
        st_ref[1:2, :] = _last_row(hi)

    scratch = [pltpu.VMEM((8, SW), F32)] + [pltpu.VMEM((SW // DH, r, DH), F32)] * 2 + [pltpu.VMEM((r // SUB, SW), F32)] * 2
    return pl.pallas_call(
        body, name="s5_fwd", grid=(NCB, nt),
        in_specs=[pl.BlockSpec((r, DH), lambda c, i: (i, u0 + c)),
                  pl.BlockSpec((2, 1, SW), lambda c, i: (0, 0, c)),
                  pl.BlockSpec((2, None, DH, SW), lambda c, i: (0, c, 0, 0)),
                  pl.BlockSpec((2, None, SW, DH), lambda c, i: (0, c, 0, 0))],
        out_specs=[pl.BlockSpec((r, DH), lambda c, i: (i, c)),
                   pl.BlockSpec((None, 8, SW), lambda c, i: (i, 0, c))],
        out_shape=[jax.ShapeDtypeStruct((t, D), F32), jax.ShapeDtypeStruct((nt, 8, NG * NS), F32)],
        scratch_shapes=scratch,
        compiler_params=_cparams(("parallel", "arbitrary")),
    )(proj, lam, bblk, cblk)


def _s5_bwd(proj, lam, bblk, cblk, carries, dy, du_skip):
    t = proj.shape[0]
    r = _s5_tile_rows(t)
    nt = t // r
    u0 = 4 * D // DH

    def body(u_ref, lam_ref, b_ref, c_ref, car_ref, dy_ref, dus_ref, du_ref, dlam_ref, db_ref, dc_ref, dst_ref,
             hr_ref, hi_ref, ar_ref, ai_ref, cr_ref, ci_ref):
        first = pl.program_id(1) == 0

        @pl.when(first)
        def _():
            dst_ref[...] = jnp.zeros_like(dst_ref)

        u, dy = u_ref[...], dy_ref[...]
        lr, li = lam_ref[0], lam_ref[1]
        hr, hi = _s5_states(u, lam_ref, b_ref, car_ref, hr_ref, hi_ref, cr_ref, ci_ref)
        dcr2, dci2 = _mm_tn(hr, dy), -_mm_tn(hi, dy)
        last = _rows((r, SW)) == r - 1
        inr, ini = _cmul(lr, -li, dst_ref[0:1, :], dst_ref[1:2, :])
        dhr = _mm_nt(dy, c_ref[0]) + jnp.where(last, inr, 0.0)
        dhi = jnp.where(last, ini, 0.0) - _mm_nt(dy, c_ref[1])
        ar, ai = _scan_tile(dhr, dhi, lr, -li, ar_ref, ai_ref, cr_ref, ci_ref, True)
        top = _rows((r, SW)) == 0
        dst_ref[0:1, :] = jnp.sum(jnp.where(top, ar, 0.0), axis=0, keepdims=True)
        dst_ref[1:2, :] = jnp.sum(jnp.where(top, ai, 0.0), axis=0, keepdims=True)
        du_ref[...] = _mm_nt(ar, b_ref[0]) + _mm_nt(ai, b_ref[1]) + dus_ref[...]
        dbr, dbi = _mm_tn(u, ar), _mm_tn(u, ai)
        pr = _sd(hr, 1) + jnp.where(top, car_ref[0:1, :], 0.0)
        pi = _sd(hi, 1) + jnp.where(top, car_ref[1:2, :], 0.0)
        dlr = jnp.sum(ar * pr + ai * pi, axis=0, keepdims=True)
        dli = jnp.sum(ai * pr - ar * pi, axis=0, keepdims=True)

        @pl.when(first)
        def _():
            dlam_ref[0] = dlr
            dlam_ref[1] = dli
            db_ref[0] = dbr
            db_ref[1] = dbi
            dc_ref[0] = dcr2
            dc_ref[1] = dci2

        @pl.when(jnp.logical_not(first))
        def _():
            dlam_ref[0] += dlr
            dlam_ref[1] += dli
            db_ref[0] += dbr
            db_ref[1] += dbi
            dc_ref[0] += dcr2
            dc_ref[1] += dci2

    return pl.pallas_call(
        body, name="s5_bwd", grid=(NCB, nt),
        in_specs=[pl.BlockSpec((r, DH), lambda c, i: (nt - 1 - i, u0 + c)),
                  pl.BlockSpec((2, 1, SW), lambda c, i: (0, 0, c)),
                  pl.BlockSpec((2, None, DH, SW), lambda c, i: (0, c, 0, 0)),
                  pl.BlockSpec((2, None, SW, DH), lambda c, i: (0, c, 0, 0)),
                  pl.BlockSpec((None, 8, SW), lambda c, i: (nt - 1 - i, 0, c)),
                  pl.BlockSpec((r, DH), lambda c, i: (nt - 1 - i, c)),
                  pl.BlockSpec((r, DH), lambda c, i: (nt - 1 - i, c))],
        out_specs=[pl.BlockSpec((r, DH), lambda c, i: (nt - 1 - i, c)),
                   pl.BlockSpec((2, 1, SW), lambda c, i: (0, 0, c)),
                   pl.BlockSpec((2, None, DH, SW), lambda c, i: (0, c, 0, 0)),
                   pl.BlockSpec((2, None, SW, DH), lambda c, i: (0, c, 0, 0))],
        out_shape=[jax.ShapeDtypeStruct((t, D), F32), jax.ShapeDtypeStruct((2, 1, NG * NS), F32),
                   jax.ShapeDtypeStruct((2, NCB, DH, SW), F32), jax.ShapeDtypeStruct((2, NCB, SW, DH), F32)],
        scratch_shapes=[pltpu.VMEM((8, SW), F32)] + [pltpu.VMEM((SW // DH, r, DH), F32)] * 4
        + [pltpu.VMEM((r // SUB, SW), F32)] * 2,
        compiler_params=_cparams(("parallel", "arbitrary")),
    )(proj, lam, bblk, cblk, carries, dy, du_skip)


def _proj_spec(tm, col):
    return pl.BlockSpec((tm, D), lambda i: (i, col))


def _layer_mat(l):
    return pl.BlockSpec((None, D, D), lambda i: (l, 0, 0))


def _mix_fwd(proj, o, s5y, x, hn, dvec, wglu, bglu, wout, npost, l):
    t = x.shape[0]
    tm = _tile(t, 256)

    def body(za_ref, u_ref, zb_ref, ra_ref, rb_ref, o_ref, y_ref, x_ref, hn_ref, d_ref, wg_ref, bg_ref, wo_ref,
             np_ref, xn_ref):
        y0 = _mix_pre(y_ref[...], u_ref[...], d_ref[...])
        gl = _mm(y0, wg_ref[...]) + bg_ref[...]
        m = _mix_mid(o_ref[...], za_ref[...], y0, gl, zb_ref[...], ra_ref[...], rb_ref[...], hn_ref[...])
        out = _mm(m, wo_ref[...])
        xn_ref[...] = _mix_post(x_ref[...], out, np_ref[...])

    act = pl.BlockSpec((tm, D), lambda i: (i, 0))
    return pl.pallas_call(
        body, name="mix_fwd", grid=(t // tm,),
        in_specs=[_proj_spec(tm, 3), _proj_spec(tm, 4), _proj_spec(tm, 5), _proj_spec(tm, 6), _proj_spec(tm, 7),
                  act, act, act, _full((1, DH)), _full((1, D)), _layer_mat(l), _full((1, D)), _layer_mat(l),
                  _full((1, D))],
        out_specs=act,
        out_shape=jax.ShapeDtypeStruct((t, D), F32),
        compiler_params=_cparams(("parallel",)),
    )(proj, proj, proj, proj, proj, o, s5y, x, hn, dvec, wglu, bglu, wout, npost)


def _mix_bwd(proj, o, s5y, x, hn, dvec, wglu, bglu, wout, npost, dxn, l):
    t = x.shape[0]
    tm = _tile(t, 128)

    def body(za_ref, u_ref, zb_ref, ra_ref, rb_ref, o_ref, y_ref, x_ref, hn_ref, d_ref, wg_ref, bg_ref, wo_ref,
             np_ref, dxn_ref,
             dza_ref, du_ref, dzb_ref, dra_ref, drb_ref, do_ref, dy_ref, dx_ref,
             dwg_ref, dwo_ref, dvecs_ref, dhn_ref):
        y0, vjp_pre = jax.vjp(_mix_pre, y_ref[...], u_ref[...], d_ref[...])
        gl = _mm(y0, wg_ref[...]) + bg_ref[...]
        m, vjp_mid = jax.vjp(_mix_mid, o_ref[...], za_ref[...], y0, gl, zb_ref[...], ra_ref[...], rb_ref[...],
                             hn_ref[...])
        out = _mm(m, wo_ref[...])
        _, vjp_post = jax.vjp(_mix_post, x_ref[...], out, np_ref[...])
        dx, dout, dnp = vjp_post(dxn_ref[...])
        dm = _mm_nt(dout, wo_ref[...])
        dwo = _mm_tn(m, dout)
        do, dza, dy0, dgl, dzb, dra, drb, dhn = vjp_mid(dm)
        dwg = _mm_tn(y0, dgl)
        dbg = jnp.sum(dgl, axis=0, keepdims=True)
        dy0 = dy0 + _mm_nt(dgl, wg_ref[...])
        dy, du, dd = vjp_pre(dy0)
        dza_ref[...] = dza
        du_ref[...] = du
        dzb_ref[...] = dzb
        dra_ref[...] = dra
        drb_ref[...] = drb
        do_ref[...] = do
        dy_ref[...] = dy
        dx_ref[...] = dx
        first = pl.program_id(0) == 0

        @pl.when(first)
        def _():
            dwg_ref[...] = dwg
            dwo_ref[...] = dwo
            dvecs_ref[...] = jnp.zeros_like(dvecs_ref)
            dhn_ref[...] = jnp.zeros_like(dhn_ref)

        @pl.when(jnp.logical_not(first))
        def _():
            dwg_ref[...] += dwg
            dwo_ref[...] += dwo

        dvecs_ref[0:1, :] += dd
        dvecs_ref[1:2, :] += dbg
        dvecs_ref[2:3, :] += dnp
        dhn_ref[0:1, :] += dhn

    act = pl.BlockSpec((tm, D), lambda i: (i, 0))
    a = jax.ShapeDtypeStruct((t, D), F32)
    w = jax.ShapeDtypeStruct((D, D), F32)
    return pl.pallas_call(
        body, name="mix_bwd", grid=(t // tm,),
        in_specs=[_proj_spec(tm, 3), _proj_spec(tm, 4), _proj_spec(tm, 5), _proj_spec(tm, 6), _proj_spec(tm, 7),
                  act, act, act, _full((1, DH)), _full((1, D)), _layer_mat(l), _full((1, D)), _layer_mat(l),
                  _full((1, D)), act],
        out_specs=[act] * 8 + [_full((D, D)), _full((D, D)), _full((8, D)), _full((8, DH))],
        out_shape=[a] * 8 + [w, w, jax.ShapeDtypeStruct((8, D), F32), jax.ShapeDtypeStruct((8, DH), F32)],
        compiler_params=_cparams(("arbitrary",)),
    )(proj, proj, proj, proj, proj, o, s5y, x, hn, dvec, wglu, bglu, wout, npost, dxn)


def _loss_grad(y, target):
    t = y.shape[0]
    tm = _tile(t, 512)

    def body(y_ref, t_ref, dy_ref, l_ref):
        err = y_ref[...] - t_ref[...]
        dy_ref[...] = err * (1.0 / D)
        part = jnp.sum(jnp.sum(err * err, axis=1, keepdims=True), axis=0, keepdims=True) * (0.5 / D)
        part = jnp.broadcast_to(part, (8, DH))

        @pl.when(pl.program_id(0) == 0)
        def _():
            l_ref[...] = part

        @pl.when(pl.program_id(0) > 0)
        def _():
            l_ref[...] += part

    act = pl.BlockSpec((tm, D), lambda i: (i, 0))
    return pl.pallas_call(
        body, name="loss_grad", grid=(t // tm,),
        in_specs=[act, act], out_specs=[act, _full((8, DH))],
        out_shape=[jax.ShapeDtypeStruct((t, D), F32), jax.ShapeDtypeStruct((8, DH), F32)],
        compiler_params=_cparams(("arbitrary",)),
    )(y, target)


def _flips(rel):
    x, y, c = lax.axis_index("x"), lax.axis_index("y"), lax.axis_index("c")
    fx, fy, fc = rel
    return (x ^ fx if fx else x, y ^ fy if fy else y, c ^ fc if fc else c)


CHIP_RELS = ((1, 0, 0), (0, 1, 0), (1, 1, 0))
ALL_RELS = tuple((fx, fy, fc) for fx in (0, 1) for fy in (0, 1) for fc in (0, 1) if (fx, fy, fc) != (0, 0, 0))


def _slot_of(pos, by_chip):
    px, py, pc = pos
    return 2 * px + py if by_chip else 4 * px + 2 * py + pc


def _exchange(srcs, rels, by_chip, scatter, name):
    nslot = NCHIP if by_chip else NDEV
    narr, nrel = len(srcs), len(rels)

    def body(*refs):
        src_refs, dst_refs = refs[:narr], refs[narr:2 * narr]
        send_sems, recv_sems, local_sems = refs[2 * narr:]
        me = _flips((0, 0, 0))
        my_slot = _slot_of(me, by_chip)
        started = []
        for a, (src_ref, dst_ref) in enumerate(zip(src_refs, dst_refs)):
            mine = pltpu.make_async_copy(src_ref.at[my_slot] if scatter else src_ref, dst_ref.at[my_slot],
                                         local_sems.at[a])
            mine.start()
            started.append(mine)
        sends = []
        for a, (src_ref, dst_ref) in enumerate(zip(src_refs, dst_refs)):
            for k, rel in enumerate(rels):
                peer = _flips(rel)
                part = src_ref.at[_slot_of(peer, by_chip)] if scatter else src_ref
                cp = pltpu.make_async_remote_copy(
                    src_ref=part, dst_ref=dst_ref.at[my_slot], send_sem=send_sems.at[a * nrel + k],
                    recv_sem=recv_sems.at[a * nrel + k], device_id=peer, device_id_type=pl.DeviceIdType.MESH)
                cp.start()
                sends.append(cp)
        for a, (src_ref, dst_ref) in enumerate(zip(src_refs, dst_refs)):
            for k, rel in enumerate(rels):
                peer = _flips(rel)
                part = src_ref.at[0] if scatter else src_ref
                pltpu.make_async_remote_copy(
                    src_ref=part, dst_ref=dst_ref.at[_slot_of(peer, by_chip)], send_sem=send_sems.at[a * nrel + k],
                    recv_sem=recv_sems.at[a * nrel + k], device_id=peer,
                    device_id_type=pl.DeviceIdType.MESH).wait_recv()
        for cp in sends:
            cp.wait_send()
        for mine in started:
            mine.wait()

    return pl.pallas_call(
        body, name=name,
        in_specs=[pl.BlockSpec(memory_space=pl.ANY)] * narr,
        out_specs=[pl.BlockSpec(memory_space=pl.ANY)] * narr,
        out_shape=[jax.ShapeDtypeStruct((nslot,) + s.shape[-2:], s.dtype) for s in srcs],
        scratch_shapes=[pltpu.SemaphoreType.DMA((narr * nrel,)), pltpu.SemaphoreType.DMA((narr * nrel,)),
                        pltpu.SemaphoreType.DMA((narr,))],
    )(*srcs)


def _sibling_swap(srcs, name):
    narr = len(srcs)

    def body(*refs):
        src_refs, dst_refs = refs[:narr], refs[narr:2 * narr]
        send_sems, recv_sems = refs[2 * narr:]
        peer = _flips((0, 0, 1))
        copies = [pltpu.make_async_remote_copy(src_ref=s, dst_ref=d, send_sem=send_sems.at[a], recv_sem=recv_sems.at[a],
                                               device_id=peer, device_id_type=pl.DeviceIdType.MESH)
                  for a, (s, d) in enumerate(zip(src_refs, dst_refs))]
        for cp in copies:
            cp.start()
        for cp in copies:
            cp.wait()

    return pl.pallas_call(
        body, name=name,
        in_specs=[pl.BlockSpec(memory_space=pl.ANY)] * narr,
        out_specs=[pl.BlockSpec(memory_space=pl.ANY)] * narr,
        out_shape=[jax.ShapeDtypeStruct(s.shape, s.dtype) for s in srcs],
        scratch_shapes=[pltpu.SemaphoreType.DMA((narr,)), pltpu.SemaphoreType.DMA((narr,))],
    )(*srcs)


def _sum_slots(parts, name):
    ns, rows, cols = parts.shape
    tr = _row_tile(rows, 256)

    def body(p_ref, o_ref):
        acc = p_ref[0].astype(F32)
        for s in range(1, ns):
            acc = acc + p_ref[s].astype(F32)
        o_ref[...] = acc

    return pl.pallas_call(
        body, name=name, grid=(rows // tr,),
        in_specs=[pl.BlockSpec((ns, tr, cols), lambda i: (0, i, 0))],
        out_specs=pl.BlockSpec((tr, cols), lambda i: (i, 0)),
        out_shape=jax.ShapeDtypeStruct((rows, cols), F32),
        compiler_params=_cparams(("parallel",)),
    )(parts)


def _adamw(w, g_parts, m, v, name, max_rows=256):
    rows, cols = w.shape
    tr = _row_tile(rows, max_rows)
    c1 = 1.0 / (1.0 - ADAM_B1 ** ADAM_STEP)
    c2 = 1.0 / (1.0 - ADAM_B2 ** ADAM_STEP)
    npart = len(g_parts)

    def body(*refs):
        w_ref, m_ref, v_ref = refs[:3]
        g_refs = refs[3:3 + npart]
        go_ref, d_ref, nm_ref, nv_ref = refs[3 + npart:]
        terms = []
        for g_ref in g_refs:
            terms += [g_ref[...]] if len(g_ref.shape) == 2 else [g_ref[s] for s in range(g_ref.shape[0])]
        g = terms[0]
        for term in terms[1:]:
            g = g + term
        nm = ADAM_B1 * m_ref[...] + (1.0 - ADAM_B1) * g
        nv = ADAM_B2 * v_ref[...] + (1.0 - ADAM_B2) * (g * g)
        d_ref[...] = -ADAM_LR * ((nm * c1) / (jnp.sqrt(nv * c2) + ADAM_EPS) + ADAM_WD * w_ref[...])
        go_ref[...] = g
        nm_ref[...] = nm
        nv_ref[...] = nv

    blk = pl.BlockSpec((tr, cols), lambda i: (i, 0))
    g_specs = [blk if p.ndim == 2 else pl.BlockSpec((p.shape[0], tr, cols), lambda i: (0, i, 0)) for p in g_parts]
    out = jax.ShapeDtypeStruct((rows, cols), F32)
    return pl.pallas_call(
        body, name=name, grid=(rows // tr,),
        in_specs=[blk, blk, blk] + g_specs,
        out_specs=[blk] * 4, out_shape=[out] * 4,
        compiler_params=_cparams(("parallel",)),
    )(w, m, v, *g_parts)


WIN_SHARD = 2052
CONV_SHARD = 768
ROW_SHARD = 256

SMALL = (("norm_pre", (DEPTH, D)), ("a_log", (DEPTH, NH)), ("dt_bias", (DEPTH, NH)), ("head_norm", (DEPTH, DH)),
         ("ssm_a_re", (DEPTH, NG, NS)), ("ssm_a_im", (DEPTH, NG, NS)), ("ssm_log_dt", (DEPTH, NG)),
         ("ssm_b_re", (DEPTH, NG, NS, GS)), ("ssm_b_im", (DEPTH, NG, NS, GS)),
         ("ssm_c_re", (DEPTH, NG, GS, NS)), ("ssm_c_im", (DEPTH, NG, GS, NS)), ("ssm_d", (DEPTH, D)),
         ("b_glu", (DEPTH, D)), ("norm_post", (DEPTH, D)))


def _pad_rows(flat, rows):
    return jnp.pad(flat, (0, rows * D - flat.shape[0])).reshape(rows, D)


def _rows_by_chip(a):
    nl, rows, cols = a.shape
    return a.reshape(nl, NCHIP, rows // NCHIP, cols).transpose(1, 0, 2, 3).reshape(NCHIP, -1, cols)


def _rows_from_chips(a):
    _, rows, cols = a.shape
    return a.reshape(NCHIP, DEPTH, rows // DEPTH, cols).transpose(1, 0, 2, 3).reshape(DEPTH, -1, cols)


def _cols_by_chip(a):
    nl, rows, cols = a.shape
    return a.reshape(nl, rows, NCHIP, cols // NCHIP).transpose(2, 0, 1, 3).reshape(NCHIP, nl * rows, -1)


def _cols_from_chips(a, nl):
    _, rows, cols = a.shape
    return a.reshape(NCHIP, nl, rows // nl, cols).transpose(1, 2, 0, 3).reshape(nl, rows // nl, NCHIP * cols)


SMALL_ROWS = sum(-(-math.prod(s) // (8 * D)) * 8 for _, s in SMALL)
CONV_ROWS = DEPTH * 4 * 3 * D // D


def _pack_small(vals, extra=()):
    parts = []
    for val in tuple(vals) + tuple(extra):
        n = val.size
        parts.append(_pad_rows(val.reshape(-1), -(-n // (8 * D)) * 8))
    return jnp.concatenate(parts, axis=0)


def _unpack_small(flat):
    outs, r0 = [], 0
    for _, shape in SMALL:
        n = math.prod(shape)
        rows = -(-n // (8 * D)) * 8
        outs.append(flat[r0:r0 + rows].reshape(-1)[:n].reshape(shape))
        r0 += rows
    return outs


def _rearrange_cols(w):
    pad = jnp.zeros(w.shape[:-1] + (NCOL - BD0 - 2 * NH,), w.dtype)
    return jnp.concatenate([w[..., :4 * D], w[..., 4 * D + 2 * NH:], w[..., 4 * D:4 * D + 2 * NH], pad], axis=-1)


def _restore_cols(w):
    return jnp.concatenate([w[..., :4 * D], w[..., BD0:BD0 + 2 * NH], w[..., 4 * D:BD0]], axis=-1)


def _block_diag_b(bb2):
    b = bb2.reshape(NCB, GPB, NS, GS)
    eye = jnp.eye(GPB, dtype=F32)
    return jnp.einsum("kgnc,gh->kgchn", b, eye).reshape(NCB, GPB * GS, SW)


def _block_diag_b_t(d):
    return jnp.einsum("kgchn,gh->kgnc", d.reshape(NCB, GPB, GS, GPB, NS), jnp.eye(GPB, dtype=F32)).reshape(NG, NS * GS)


def _block_diag_c(c):
    eye = jnp.eye(GPB, dtype=F32)
    return jnp.einsum("kgcn,gh->kgnhc", c.reshape(NCB, GPB, GS, NS), eye).reshape(NCB, SW, GPB * GS)


def _block_diag_c_t(d):
    return jnp.einsum("kgnhc,gh->kgcn", d.reshape(NCB, GPB, NS, GPB, GS), jnp.eye(GPB, dtype=F32)).reshape(NG, GS, NS)


def _local_step(x, target, wcat, conv, wglu, wout, small):
    ar, ai = small["ssm_a_re"], small["ssm_a_im"]
    ldt = small["ssm_log_dt"].reshape(DEPTH, NG, 1)
    br2 = small["ssm_b_re"].reshape(DEPTH, NG, NS * GS)
    bi2 = small["ssm_b_im"].reshape(DEPTH, NG, NS * GS)
    lr, li, bbr2, bbi2 = _s5_params(ar, ai, ldt, br2, bi2)

    def row(name, l, width):
        return small[name][l].reshape(1, width)

    saved = []
    for l in range(DEPTH):
        gvec = jnp.pad(jnp.stack([small["a_log"][l], small["dt_bias"][l]]), ((0, 6), (NH, DH - 2 * NH)))
        lam = jnp.stack([lr[l].reshape(1, NG * NS), li[l].reshape(1, NG * NS)])
        bblk = jnp.stack([_block_diag_b(bbr2[l]), _block_diag_b(bbi2[l])])
        cblk = jnp.stack([_block_diag_c(small["ssm_c_re"][l]), _block_diag_c(small["ssm_c_im"][l])])
        proj, h = _inproj_fwd(x, row("norm_pre", l, D), wcat, l)
        qkv = _prep_fwd(proj, conv[l])
        bb, gcb = _gates_fwd(proj, gvec)
        local, t_inv = _delta_local_fwd(qkv, bb, gcb)
        o, states = _delta_state_fwd(local, gcb)
        s5y, carries = _s5_fwd(proj, lam, bblk, cblk)
        xn = _mix_fwd(proj, o, s5y, x, row("head_norm", l, DH), row("ssm_d", l, D), wglu, row("b_glu", l, D),
                      wout, row("norm_post", l, D), l)
        saved.append((x, proj, h, qkv, bb, gcb, local, t_inv, o, states, s5y, carries, gvec, lam, bblk, cblk))
        x = xn

    dx, loss_part = _loss_grad(x, target)

    g = {k: [None] * DEPTH for k in ("wcat", "conv", "wglu", "wout", "norm_pre", "a_log", "dt_bias", "head_norm",
                                     "ssm_c_re", "ssm_c_im", "ssm_d", "b_glu", "norm_post", "lr", "li", "bbr", "bbi")}
    for l in reversed(range(DEPTH)):
        xl, proj, h, qkv, bb, gcb, local, t_inv, o, states, s5y, carries, gvec, lam, bblk, cblk = saved[l]
        (dza, du_skip, dzb, dra, drb, do, ds5y, dxres, dwg, dwo, dvecs, dhn) = _mix_bwd(
            proj, o, s5y, xl, row("head_norm", l, DH), row("ssm_d", l, D), wglu, row("b_glu", l, D), wout,
            row("norm_post", l, D), dx, l)
        du, dlam, dbblk, dcblk = _s5_bwd(proj, lam, bblk, cblk, carries, ds5y, du_skip)
        *dlocal, dgcb_state = _delta_state_bwd(local, gcb, states, do)
        dq, dk, dv, dbb, dgcb = _delta_local_bwd(qkv, bb, gcb, t_inv, dlocal, dgcb_state)
        dbd, dgvec = _gates_bwd(proj, gvec, dbb, dgcb)
        dqkv = jnp.concatenate([dq, dk, dv], axis=0)
        dpre, dconv = _prep_bwd(proj, conv[l], dqkv)
        dproj = jnp.concatenate([dpre, dza, du, dzb, dra, drb, dbd], axis=1)
        dx, dgain = _inproj_bwd_dx(dproj, wcat, xl, row("norm_pre", l, D), dxres, l)
        g["wcat"][l] = _inproj_bwd_dw(h, dproj)
        g["conv"][l], g["wglu"][l], g["wout"][l] = dconv, dwg, dwo
        g["norm_pre"][l] = dgain[0]
        g["a_log"][l], g["dt_bias"][l] = dgvec[0, NH:2 * NH], dgvec[1, NH:2 * NH]
        g["head_norm"][l] = dhn[0]
        g["ssm_d"][l], g["b_glu"][l], g["norm_post"][l] = dvecs[0], dvecs[1], dvecs[2]
        g["ssm_c_re"][l], g["ssm_c_im"][l] = _block_diag_c_t(dcblk[0]), _block_diag_c_t(dcblk[1])
        g["lr"][l], g["li"][l] = dlam[0].reshape(NG, NS), dlam[1].reshape(NG, NS)
        g["bbr"][l], g["bbi"][l] = _block_diag_b_t(dbblk[0]), _block_diag_b_t(dbblk[1])
    g = {k: jnp.stack(v) for k, v in g.items()}
    dar, dai, dldt, dbr2, dbi2 = _s5_params_bwd(ar, ai, ldt, br2, bi2, g["lr"], g["li"], g["bbr"], g["bbi"])
    g["ssm_a_re"], g["ssm_a_im"], g["ssm_log_dt"] = dar, dai, dldt.reshape(DEPTH, NG)
    g["ssm_b_re"] = dbr2.reshape(DEPTH, NG, NS, GS)
    g["ssm_b_im"] = dbi2.reshape(DEPTH, NG, NS, GS)
    return loss_part[0, 0], dx, g


def kernel(x, norm_pre, w_in, conv_w, a_log, dt_bias, head_norm, ssm_a_re, ssm_a_im, ssm_log_dt, ssm_b_re, ssm_b_im, ssm_c_re, ssm_c_im, ssm_d, w_glu, b_glu, w_out, norm_post, loss_target, m_norm_pre, m_w_in, m_conv_w, m_a_log, m_dt_bias, m_head_norm, m_ssm_a_re, m_ssm_a_im, m_ssm_log_dt, m_ssm_b_re, m_ssm_b_im, m_ssm_c_re, m_ssm_c_im, m_ssm_d, m_w_glu, m_b_glu, m_w_out, m_norm_post, v_norm_pre, v_w_in, v_conv_w, v_a_log, v_dt_bias, v_head_norm, v_ssm_a_re, v_ssm_a_im, v_ssm_log_dt, v_ssm_b_re, v_ssm_b_im, v_ssm_c_re, v_ssm_c_im, v_ssm_d, v_w_glu, v_b_glu, v_w_out, v_norm_post):
    args = dict(locals())
    small = {n: args[n] for n, _ in SMALL}

    def flat2(a):
        return a.reshape(-1, a.shape[-1])

    g_in, g_glu, g_out, g_conv = _exchange(
        [flat2(w_in).astype(BF16), flat2(w_glu).astype(BF16), flat2(w_out).astype(BF16), flat2(conv_w)],
        CHIP_RELS, True, False, "gather_weights")
    wcat = _rearrange_cols(_cols_from_chips(g_in, DEPTH))
    wglu, wout = _rows_from_chips(g_glu), _rows_from_chips(g_out)
    conv = _cols_from_chips(g_conv, DEPTH)

    loss_part, dx, g = _local_step(x[0], loss_target[0], wcat, conv, wglu, wout, small)
    loss = lax.psum(loss_part, ("x", "y", "c"))

    from_chips = _exchange(
        [_cols_by_chip(_restore_cols(g["wcat"])).astype(BF16), _rows_by_chip(g["wglu"]).astype(BF16),
         _rows_by_chip(g["wout"]).astype(BF16)], CHIP_RELS, True, True, "scatter_grads")
    core_sums = [_sum_slots(p, "sum_chips_" + n) for p, n in zip(from_chips, ("in", "glu", "out"))]
    others = _sibling_swap(core_sums, "swap_cores")
    sharded = {}
    for n, mine, other in zip(("w_in", "w_glu", "w_out"), core_sums, others):
        sharded[n] = _adamw(flat2(args[n]), [mine, other], flat2(args["m_" + n]), flat2(args["v_" + n]), "adamw_" + n,
                            max_rows=128)

    (small_parts,) = _exchange([_pack_small([g[n] for n, _ in SMALL], extra=[g["conv"]])], ALL_RELS, False, False,
                               "gather_small")
    small_out = _adamw(_pack_small([args[n] for n, _ in SMALL]), [small_parts],
                       _pack_small([args["m_" + n] for n, _ in SMALL]),
                       _pack_small([args["v_" + n] for n, _ in SMALL]), "adamw_small")
    chip = 2 * lax.axis_index("x") + lax.axis_index("y")
    conv_parts = small_parts[:, SMALL_ROWS:SMALL_ROWS + CONV_ROWS].reshape(NDEV, DEPTH * 4, 3 * D)
    conv_parts = lax.dynamic_slice_in_dim(conv_parts, chip * CONV_SHARD, CONV_SHARD, axis=2)
    sharded["conv_w"] = _adamw(flat2(conv_w), [conv_parts], flat2(m_conv_w), flat2(v_conv_w), "adamw_conv")

    names = ["norm_pre", "w_in", "conv_w", "a_log", "dt_bias", "head_norm", "ssm_a_re", "ssm_a_im", "ssm_log_dt",
             "ssm_b_re", "ssm_b_im", "ssm_c_re", "ssm_c_im", "ssm_d", "w_glu", "b_glu", "w_out", "norm_post"]
    outs = [loss, dx[None]]
    for i in range(4):
        sm = dict(zip([n for n, _ in SMALL], _unpack_small(small_out[i])))
        outs += [sharded[n][i].reshape(args[n].shape) if n in sharded else sm[n] for n in names]
    return tuple(outs)
```

```python
import functools
import math

import jax
import jax.numpy as jnp
from jax import lax
from jax.experimental import pallas as pl
from jax.experimental.pallas import tpu as pltpu

F32 = jnp.float32
BF16 = jnp.bfloat16
HI = lax.Precision.HIGHEST

D = 1024
NH = 8
DH = 128
CH = 128
NG = 64
GS = 16
NS = 64
GPB = 8
NCB = NG // GPB
SW = GPB * NS
NCOL = 8320
BD0 = 8192
EPS = 1e-6
DEPTH = 4
NCHIP = 4
NDEV = 8
VMEM_LIMIT = 56 * 1024 * 1024

ADAM_LR = 0.001
ADAM_B1 = 0.9
ADAM_B2 = 0.999
ADAM_EPS = 1e-08
ADAM_WD = 0.01
ADAM_STEP = 10


def _cparams(sem=None):
    return pltpu.CompilerParams(dimension_semantics=sem, vmem_limit_bytes=VMEM_LIMIT)


def _full(shape):
    nd = len(shape)
    return pl.BlockSpec(shape, lambda *_: (0,) * nd)


def _rms(x, gain):
    ms = jnp.mean(x * x, axis=-1, keepdims=True)
    return x * lax.rsqrt(ms + EPS) * gain


def _sigmoid(x):
    return 1.0 / (1.0 + jnp.exp(-x))


def _silu(x):
    return x * _sigmoid(x)


def _softplus(x):
    return jnp.maximum(x, 0.0) + jnp.log(1.0 + jnp.exp(-jnp.abs(x)))


def _gelu(x):
    return 0.5 * x * (1.0 + jnp.tanh(math.sqrt(2.0 / math.pi) * (x + 0.044715 * (x * x * x))))


def _dot_bf16(a, b, dims):
    return lax.dot_general(a.astype(BF16), b.astype(BF16), (dims, ((), ())), preferred_element_type=F32)


def _mm_nt(a, b):
    return _dot_bf16(a, b, ((1,), (1,)))


def _mm_tn(a, b):
    return _dot_bf16(a, b, ((0,), (0,)))


@jax.custom_vjp
def _mm(a, b):
    return _dot_bf16(a, b, ((1,), (0,)))


def _mm_fwd(a, b):
    return _dot_bf16(a, b, ((1,), (0,))), (a, b)


def _mm_bwd(res, ct):
    a, b = res
    return _mm_nt(ct, b).astype(a.dtype), _mm_tn(a, ct).astype(b.dtype)


_mm.defvjp(_mm_fwd, _mm_bwd)


@jax.custom_vjp
def _mm_nt_d(a, b):
    return _mm_nt(a, b)


def _mm_nt_d_bwd(res, ct):
    a, b = res
    return _dot_bf16(ct, b, ((1,), (0,))), _mm_tn(ct, a)


_mm_nt_d.defvjp(lambda a, b: (_mm_nt(a, b), (a, b)), _mm_nt_d_bwd)


@jax.custom_vjp
def _mm_tn_d(a, b):
    return _mm_tn(a, b)


def _mm_tn_d_bwd(res, ct):
    a, b = res
    return _mm_nt(b, ct), _dot_bf16(a, ct, ((1,), (0,)))


_mm_tn_d.defvjp(lambda a, b: (_mm_tn(a, b), (a, b)), _mm_tn_d_bwd)


def _split_bf16(a):
    hi = a.astype(BF16)
    return hi, (a - hi.astype(F32)).astype(BF16)


def _dot3(a, b, dims):
    ah, al = _split_bf16(a)
    bh, bl = _split_bf16(b)

    def dot(x, y):
        return lax.dot_general(x, y, (dims, ((), ())), preferred_element_type=F32)

    return dot(ah, bh) + (dot(ah, bl) + dot(al, bh))


@jax.custom_vjp
def _imm(a, b):
    return _dot3(a, b, ((1,), (0,)))


def _imm_bwd(res, ct):
    a, b = res
    return _dot3(ct, b, ((1,), (1,))), _dot3(a, ct, ((0,), (0,)))


_imm.defvjp(lambda a, b: (_dot3(a, b, ((1,), (0,))), (a, b)), _imm_bwd)


def _hmm(a, b):
    return jnp.dot(a, b, precision=HI, preferred_element_type=F32)


def _hmm_nt(a, b):
    return lax.dot_general(a, b, (((1,), (1,)), ((), ())), precision=HI, preferred_element_type=F32)


def _hmm_tn(a, b):
    return lax.dot_general(a, b, (((0,), (0,)), ((), ())), precision=HI, preferred_element_type=F32)


def _rows(shape):
    return lax.broadcasted_iota(jnp.int32, shape, 0)


def _cols(shape):
    return lax.broadcasted_iota(jnp.int32, shape, 1)


def _sd(x, s):
    return jnp.where(_rows(x.shape) >= s, pltpu.roll(x, s, axis=0), 0.0)


def _su(x, s):
    n = x.shape[0]
    return jnp.where(_rows(x.shape) < n - s, pltpu.roll(x, n - s, axis=0), 0.0)


@functools.partial(jax.custom_vjp, nondiff_argnums=(1,))
def _shift_down(x, s):
    return _sd(x, s)


def _shift_down_fwd(x, s):
    return _sd(x, s), None


def _shift_down_bwd(s, _, g):
    return (_su(g, s),)


_shift_down.defvjp(_shift_down_fwd, _shift_down_bwd)


def _last_row(x):
    n = x.shape[0]
    return jnp.sum(jnp.where(_rows(x.shape) == n - 1, x, 0.0), axis=0, keepdims=True)


def _prep_fn(p, w0, w1, w2, w3, qk):
    acc = w3 * p + w2 * _shift_down(p, 1) + w1 * _shift_down(p, 2) + w0 * _shift_down(p, 3)
    a = _silu(acc)
    nrm = lax.rsqrt(jnp.sum(a * a, axis=-1, keepdims=True) + EPS)
    return a * (nrm * qk + (1.0 - qk))


def _gates_fn(bd, av, bv):
    tm = bd.shape[0]
    beta_all = _sigmoid(bd)
    g_all = -jnp.exp(av) * _softplus(bd + bv)
    r, c = _rows((tm, tm)), _cols((tm, tm))
    tri = jnp.where((r // CH == c // CH) & (r >= c), 1.0, 0.0).astype(F32)
    gc_all = _hmm(tri, g_all)
    lane = _cols(bd.shape)
    outs = []
    for h in range(NH):
        b = jnp.sum(jnp.where(lane == h, beta_all, 0.0), axis=1, keepdims=True)
        outs.append(jnp.broadcast_to(b, bd.shape))
    for h in range(NH):
        g = jnp.sum(jnp.where(lane == NH + h, gc_all, 0.0), axis=1, keepdims=True)
        outs.append(jnp.broadcast_to(g, bd.shape))
    return tuple(outs)


def _unit_lower_inv(l_mat):
    n = l_mat.shape[0]
    eye = jnp.where(_rows((n, n)) == _cols((n, n)), 1.0, 0.0).astype(F32)
    p = -l_mat
    r = eye + p
    k = 1
    while 2 * k < n:
        p = _imm(p, p)
        r = r + _imm(r, p)
        k *= 2
    return r


@jax.custom_vjp
def _known_inverse(l_mat, t_inv):
    return t_inv


def _known_inverse_bwd(t_inv, ct):
    d_l = -_dot3(_dot3(t_inv, ct, ((0,), (0,))), t_inv, ((1,), (1,)))
    return d_l, jnp.zeros_like(t_inv)


_known_inverse.defvjp(lambda l_mat, t_inv: (t_inv, t_inv), _known_inverse_bwd)


def _chunk_local(q, k, v, bb, gcb, t_inv=None):
    qs = q * (DH ** -0.5)
    kb = k * bb
    eg = jnp.exp(gcb)
    ii, jj = _rows((CH, CH)), _cols((CH, CH))
    decay = jnp.exp(jnp.where(ii >= jj, gcb - gcb.T, -1e30))
    l_mat = jnp.where(ii > jj, _mm_nt_d(kb, k) * decay, 0.0)
    t_inv = _unit_lower_inv(l_mat) if t_inv is None else _known_inverse(l_mat, t_inv)
    u = _mm(t_inv, v * bb)
    w = _mm(t_inv, kb * eg)
    a_qk = _mm_nt_d(qs, k) * decay
    k_dec = k * jnp.exp(_last_row(gcb) - gcb)
    return (u, w, qs * eg, k_dec, a_qk), t_inv


def _state_step(u, w, q_dec, k_dec, a_qk, gcb, state):
    v_new = u - _mm(w, state)
    o = _mm(q_dec, state) + _mm(a_qk, v_new)
    new_state = state * jnp.exp(_last_row(gcb)) + _mm_tn_d(k_dec, v_new)
    return o, new_state


SUB = 8


def _cmul(ar, ai, br, bi):
    return ar * br - ai * bi, ar * bi + ai * br


def _scan_tile(xr, xi, mr, mi, hr_ref, hi_ref, cr_ref, ci_ref, reverse):
    n, width = xr.shape
    ngroups = n // SUB
    shift_groups = _su if reverse else _sd
    xr, xi = xr.reshape(ngroups, SUB, width), xi.reshape(ngroups, SUB, width)
    pr, pi = mr, mi
    tr, ti = jnp.broadcast_to(mr, (SUB, width)), jnp.broadcast_to(mi, (SUB, width))
    pos = _rows(tr.shape)
    s = 1
    while s < SUB:
        inside = pos < SUB - s if reverse else pos >= s
        shift = SUB - s if reverse else s
        qr, qi = jnp.where(inside, pr, 0.0)[None], jnp.where(inside, pi, 0.0)[None]
        dr, di = _cmul(qr, qi, pltpu.roll(xr, shift, axis=1), pltpu.roll(xi, shift, axis=1))
        xr, xi = xr + dr, xi + di
        er = jnp.where(inside, pltpu.roll(tr, shift, axis=0), 1.0)
        ei = jnp.where(inside, pltpu.roll(ti, shift, axis=0), 0.0)
        tr, ti = _cmul(tr, ti, er, ei)
        pr, pi = _cmul(pr, pi, pr, pi)
        s *= 2
    xr, xi = xr.reshape(n, width), xi.reshape(n, width)
    nlb = width // DH

    def lanes(x, j):
        return x[:, j * DH:(j + 1) * DH]

    for j in range(nlb):
        hr_ref[j] = lanes(xr, j)
        hi_ref[j] = lanes(xi, j)
    edge = pl.ds(0 if reverse else SUB - 1, ngroups, stride=SUB)
    gr = jnp.concatenate([hr_ref.at[j][edge, :] for j in range(nlb)], axis=1)
    gi = jnp.concatenate([hi_ref.at[j][edge, :] for j in range(nlb)], axis=1)
    s = 1
    while s < ngroups:
        dr, di = _cmul(pr, pi, shift_groups(gr, s), shift_groups(gi, s))
        gr, gi = gr + dr, gi + di
        pr, pi = _cmul(pr, pi, pr, pi)
        s *= 2
    cr_ref[...] = shift_groups(gr, 1)
    ci_ref[...] = shift_groups(gi, 1)
    for g in range(ngroups):
        rows = slice(g * SUB, (g + 1) * SUB)
        dr, di = _cmul(tr, ti, cr_ref[g:g + 1, :], ci_ref[g:g + 1, :])
        for j in range(nlb):
            hr_ref[j, rows, :] += lanes(dr, j)
            hi_ref[j, rows, :] += lanes(di, j)
    return (jnp.concatenate([hr_ref[j] for j in range(nlb)], axis=1),
            jnp.concatenate([hi_ref[j] for j in range(nlb)], axis=1))


def _s5_states(u, lam_ref, b_ref, car_ref, hr_ref, hi_ref, cr_ref, ci_ref):
    lr, li = lam_ref[0], lam_ref[1]
    first = _rows((u.shape[0], SW)) == 0
    inr, ini = _cmul(lr, li, car_ref[0:1, :], car_ref[1:2, :])
    xr = _mm(u, b_ref[0]) + jnp.where(first, inr, 0.0)
    xi = _mm(u, b_ref[1]) + jnp.where(first, ini, 0.0)
    return _scan_tile(xr, xi, lr, li, hr_ref, hi_ref, cr_ref, ci_ref, False)


def _s5_params_fn(ar, ai, ldt, br2, bi2):
    dt = jnp.exp(ldt)
    mag = jnp.exp(ar * dt)
    lr, li = mag * jnp.cos(ai * dt), mag * jnp.sin(ai * dt)
    den = ar * ar + ai * ai
    fr = ((lr - 1.0) * ar + li * ai) / den
    fi = (li * ar - (lr - 1.0) * ai) / den
    expand = jnp.where(_cols((NS, NS * GS)) // GS == _rows((NS, NS * GS)), 1.0, 0.0).astype(F32)
    fr2, fi2 = _hmm(fr, expand), _hmm(fi, expand)
    return lr, li, fr2 * br2 - fi2 * bi2, fr2 * bi2 + fi2 * br2


def _head_norm(o, hn):
    parts = []
    for h in range(NH):
        oh = o[:, h * DH:(h + 1) * DH]
        parts.append(oh * lax.rsqrt(jnp.mean(oh * oh, axis=-1, keepdims=True) + EPS) * hn)
    return jnp.concatenate(parts, axis=1)


def _mix_pre(s5y, u, dvec):
    return _gelu(s5y + dvec * u)


def _mix_mid(o, za, y0, gl, zb, ra, rb, hn):
    ya = _head_norm(o, hn) * _silu(za)
    yb = y0 * _sigmoid(gl) * _silu(zb)
    return _sigmoid(ra) * ya + _sigmoid(rb) * yb


def _mix_post(x, out, npost):
    return x + _rms(out, npost)


def _tile(t, want):
    return min(t, want)


def _row_tile(rows, want):
    return max(r for r in range(16, want + 1, 16) if rows % r == 0)


def _inproj_fwd(x, gain, wcat, l):
    t = x.shape[0]
    tm, tn = _tile(t, 1024), 640

    def body(x_ref, g_ref, w_ref, o_ref, h_ref):
        @pl.when(pl.program_id(1) == 0)
        def _():
            h_ref[...] = _rms(x_ref[...], g_ref[...]).astype(h_ref.dtype)
        o_ref[...] = _dot_bf16(h_ref[...], w_ref[...], ((1,), (0,)))

    return pl.pallas_call(
        body, name="inproj_fwd", grid=(t // tm, NCOL // tn),
        in_specs=[pl.BlockSpec((tm, D), lambda i, j: (i, 0)), _full((1, D)),
                  pl.BlockSpec((None, D, tn), lambda i, j: (l, 0, j))],
        out_specs=[pl.BlockSpec((tm, tn), lambda i, j: (i, j)), pl.BlockSpec((tm, D), lambda i, j: (i, 0))],
        out_shape=[jax.ShapeDtypeStruct((t, NCOL), F32), jax.ShapeDtypeStruct((t, D), wcat.dtype)],
        compiler_params=_cparams(("parallel", "arbitrary")),
    )(x, gain, wcat)


def _inproj_bwd_dx(dproj, wcat, x, gain, dxres, l):
    t = x.shape[0]
    tm, tk = _tile(t, 1024), 640
    nk = NCOL // tk

    def body(dp_ref, w_ref, x_ref, g_ref, r_ref, dx_ref, dg_ref, acc_ref):
        i, k = pl.program_id(0), pl.program_id(1)

        @pl.when(k == 0)
        def _():
            acc_ref[...] = jnp.zeros_like(acc_ref)

        acc_ref[...] += _mm_nt(dp_ref[...], w_ref[...])

        @pl.when(k == nk - 1)
        def _():
            _, vjp = jax.vjp(_rms, x_ref[...], g_ref[...])
            dx, dg = vjp(acc_ref[...])
            dx_ref[...] = r_ref[...] + dx

            @pl.when(i == 0)
            def _():
                dg_ref[...] = dg

            @pl.when(i > 0)
            def _():
                dg_ref[...] += dg

    return pl.pallas_call(
        body, name="inproj_bwd_dx", grid=(t // tm, nk),
        in_specs=[pl.BlockSpec((tm, tk), lambda i, k: (i, k)), pl.BlockSpec((None, D, tk), lambda i, k: (l, 0, k)),
                  pl.BlockSpec((tm, D), lambda i, k: (i, 0)), _full((1, D)),
                  pl.BlockSpec((tm, D), lambda i, k: (i, 0))],
        out_specs=[pl.BlockSpec((tm, D), lambda i, k: (i, 0)), _full((1, D))],
        out_shape=[jax.ShapeDtypeStruct((t, D), F32), jax.ShapeDtypeStruct((1, D), F32)],
        scratch_shapes=[pltpu.VMEM((tm, D), F32)],
        compiler_params=_cparams(("arbitrary", "arbitrary")),
    )(dproj, wcat, x, gain, dxres)


def _inproj_bwd_dw(h, dproj):
    t = h.shape[0]
    tm, tn = _tile(t, 512), 1664

    def body(h_ref, dp_ref, o_ref):
        @pl.when(pl.program_id(1) == 0)
        def _():
            o_ref[...] = jnp.zeros_like(o_ref)

        o_ref[...] += _mm_tn(h_ref[...], dp_ref[...])

    return pl.pallas_call(
        body, name="inproj_bwd_dw", grid=(NCOL // tn, t // tm),
        in_specs=[pl.BlockSpec((tm, D), lambda j, i: (i, 0)), pl.BlockSpec((tm, tn), lambda j, i: (i, j))],
        out_specs=pl.BlockSpec((D, tn), lambda j, i: (0, j)),
        out_shape=jax.ShapeDtypeStruct((D, NCOL), F32),
        compiler_params=_cparams(("parallel", "arbitrary")),
    )(h, dproj)


def _prep_fwd(proj, cw):
    t = proj.shape[0]

    def body(p_ref, w_ref, o_ref):
        qk = (pl.program_id(0) < 2 * NH).astype(F32)
        o_ref[...] = _prep_fn(p_ref[...], w_ref[0:1, :], w_ref[1:2, :], w_ref[2:3, :], w_ref[3:4, :], qk)

    return pl.pallas_call(
        body, name="prep_fwd", grid=(3 * NH,),
        in_specs=[pl.BlockSpec((t, DH), lambda c: (0, c)), pl.BlockSpec((4, DH), lambda c: (0, c))],
        out_specs=pl.BlockSpec((None, t, DH), lambda c: (c, 0, 0)),
        out_shape=jax.ShapeDtypeStruct((3 * NH, t, DH), F32),
        compiler_params=_cparams(("parallel",)),
    )(proj, cw)


def _prep_bwd(proj, cw, dqkv):
    t = proj.shape[0]

    def body(p_ref, w_ref, d_ref, dp_ref, dw_ref):
        qk = (pl.program_id(0) < 2 * NH).astype(F32)
        _, vjp = jax.vjp(lambda p, w0, w1, w2, w3: _prep_fn(p, w0, w1, w2, w3, qk),
                         p_ref[...], w_ref[0:1, :], w_ref[1:2, :], w_ref[2:3, :], w_ref[3:4, :])
        dp, dw0, dw1, dw2, dw3 = vjp(d_ref[...])
        dp_ref[...] = dp
        dw_ref[0:1, :] = dw0
        dw_ref[1:2, :] = dw1
        dw_ref[2:3, :] = dw2
        dw_ref[3:4, :] = dw3

    return pl.pallas_call(
        body, name="prep_bwd", grid=(3 * NH,),
        in_specs=[pl.BlockSpec((t, DH), lambda c: (0, c)), pl.BlockSpec((4, DH), lambda c: (0, c)),
                  pl.BlockSpec((None, t, DH), lambda c: (c, 0, 0))],
        out_specs=[pl.BlockSpec((t, DH), lambda c: (0, c)), pl.BlockSpec((4, DH), lambda c: (0, c))],
        out_shape=[jax.ShapeDtypeStruct((t, 3 * D), F32), jax.ShapeDtypeStruct((4, 3 * D), F32)],
        compiler_params=_cparams(("parallel",)),
    )(proj, cw, dqkv)


def _gates_fwd(proj, gvec):
    t = proj.shape[0]
    tm = _tile(t, 512)

    def body(p_ref, gv_ref, b_ref, g_ref):
        outs = _gates_fn(p_ref[...], gv_ref[0:1, :], gv_ref[1:2, :])
        for h in range(NH):
            b_ref[h] = outs[h]
            g_ref[h] = outs[NH + h]

    spec = pl.BlockSpec((NH, tm, DH), lambda i: (0, i, 0))
    return pl.pallas_call(
        body, name="gates_fwd", grid=(t // tm,),
        in_specs=[pl.BlockSpec((tm, DH), lambda i: (i, BD0 // DH)), _full((8, DH))],
        out_specs=[spec, spec],
        out_shape=[jax.ShapeDtypeStruct((NH, t, DH), F32)] * 2,
        compiler_params=_cparams(("parallel",)),
    )(proj, gvec)


def _gates_bwd(proj, gvec, dbb, dgcb):
    t = proj.shape[0]
    tm = _tile(t, 512)

    def body(p_ref, gv_ref, db_ref, dg_ref, dp_ref, dgv_ref):
        _, vjp = jax.vjp(_gates_fn, p_ref[...], gv_ref[0:1, :], gv_ref[1:2, :])
        cts = tuple(db_ref[h] for h in range(NH)) + tuple(dg_ref[h] for h in range(NH))
        dp, da, db = vjp(cts)
        dp_ref[...] = dp

        @pl.when(pl.program_id(0) == 0)
        def _():
            dgv_ref[...] = jnp.zeros_like(dgv_ref)

        dgv_ref[0:1, :] += da
        dgv_ref[1:2, :] += db

    spec = pl.BlockSpec((NH, tm, DH), lambda i: (0, i, 0))
    return pl.pallas_call(
        body, name="gates_bwd", grid=(t // tm,),
        in_specs=[pl.BlockSpec((tm, DH), lambda i: (i, BD0 // DH)), _full((8, DH)), spec, spec],
        out_specs=[pl.BlockSpec((tm, DH), lambda i: (i, 0)), _full((8, DH))],
        out_shape=[jax.ShapeDtypeStruct((t, DH), F32), jax.ShapeDtypeStruct((8, DH), F32)],
        compiler_params=_cparams(("arbitrary",)),
    )(proj, gvec, dbb, dgcb)


def _chunks_per_step(nch):
    return 2 if nch % 2 == 0 else 1


def _grid_ends(grid):
    def first():
        return functools.reduce(jnp.logical_and, [pl.program_id(a) == 0 for a in range(len(grid))])

    def last():
        return functools.reduce(jnp.logical_and, [pl.program_id(a) == n - 1 for a, n in enumerate(grid)])

    return first, last


def _carry(body, n_in, n_out, n_scratch, exchange, grid):
    first, last = _grid_ends(grid)
    na = exchange.narr

    def wrapped(*refs):
        a, b = n_in, n_in + na
        c, d = b + n_out, b + n_out + na
        e = d + n_scratch
        srcs, dsts, sems = refs[a:b], refs[c:d], refs[e:]

        @pl.when(first())
        def _():
            exchange.start(srcs, dsts, sems)

        body(*(refs[:a] + refs[b:c] + refs[d:e]))

        @pl.when(last())
        def _():
            exchange.wait(srcs, dsts, sems)

    return wrapped


def _delta_local_fwd(qkv, bb, gcb, exchange=None):
    t = qkv.shape[1]
    cps = _chunks_per_step(t // CH)
    rows = cps * CH
    grid = (NH, t // rows)

    def body(q_ref, k_ref, v_ref, b_ref, g_ref, *out_refs):
        for c in range(cps):
            sl = slice(c * CH, (c + 1) * CH)
            outs, t_inv = _chunk_local(q_ref[sl, :], k_ref[sl, :], v_ref[sl, :], b_ref[sl, :], g_ref[sl, :])
            for ref, val in zip(out_refs, outs + (t_inv,)):
                ref[sl, :] = val

    def blk(off):
        return pl.BlockSpec((None, rows, DH), lambda h, n: (h + off, n, 0))

    in_specs = [blk(0), blk(NH), blk(2 * NH), blk(0), blk(0)]
    out_specs = [blk(0)] * 6
    out_shape = [jax.ShapeDtypeStruct((NH, t, DH), F32)] * 6
    args, scratch, sem = [qkv, qkv, qkv, bb, gcb], [], ("parallel", "parallel")
    if exchange is not None:
        body = _carry(body, 5, 6, 0, exchange, grid)
        in_specs, out_specs = in_specs + exchange.in_specs, out_specs + exchange.out_specs
        out_shape, scratch, args = out_shape + exchange.out_shape, exchange.scratch_shapes, args + exchange.srcs
        sem = ("arbitrary", "arbitrary")
    outs = pl.pallas_call(
        body, name="delta_local_fwd", grid=grid, in_specs=in_specs, out_specs=out_specs, out_shape=out_shape,
        scratch_shapes=scratch, compiler_params=_cparams(sem),
    )(*args)
    return outs[:5], outs[5], outs[6:]


def _delta_local_bwd(qkv, bb, gcb, t_inv, cts, dgcb_state):
    t = qkv.shape[1]
    cps = _chunks_per_step(t // CH)
    rows = cps * CH

    def body(q_ref, k_ref, v_ref, b_ref, g_ref, ti_ref, du_ref, dw_ref, dqd_ref, dkd_ref, da_ref, dgs_ref,
             dq_ref, dk_ref, dv_ref, db_ref, dg_ref):
        for c in range(cps):
            sl = slice(c * CH, (c + 1) * CH)
            t_inv_c = ti_ref[sl, :]
            _, vjp = jax.vjp(lambda *a: _chunk_local(*a, t_inv=t_inv_c)[0],
                             q_ref[sl, :], k_ref[sl, :], v_ref[sl, :], b_ref[sl, :], g_ref[sl, :])
            dq, dk, dv, db, dg = vjp((du_ref[sl, :], dw_ref[sl, :], dqd_ref[sl, :], dkd_ref[sl, :], da_ref[sl, :]))
            dq_ref[sl, :] = dq
            dk_ref[sl, :] = dk
            dv_ref[sl, :] = dv
            db_ref[sl, :] = db
            dg_ref[sl, :] = dg + dgs_ref[sl, :]

    def blk(off):
        return pl.BlockSpec((None, rows, DH), lambda h, n: (h + off, n, 0))

    return pl.pallas_call(
        body, name="delta_local_bwd", grid=(NH, t // rows),
        in_specs=[blk(0), blk(NH), blk(2 * NH)] + [blk(0)] * 9,
        out_specs=[blk(0)] * 5,
        out_shape=[jax.ShapeDtypeStruct((NH, t, DH), F32)] * 5,
        compiler_params=_cparams(("parallel", "parallel")),
    )(qkv, qkv, qkv, bb, gcb, t_inv, *cts, dgcb_state)


def _delta_state_fwd(local, gcb):
    t = gcb.shape[1]
    nch = t // CH

    def body(u_ref, w_ref, qd_ref, kd_ref, a_ref, g_ref, o_ref, s_ref, st_ref):
        @pl.when(pl.program_id(0) == 0)
        def _():
            st_ref[...] = jnp.zeros_like(st_ref)

        for h in range(NH):
            s_ref[h] = st_ref[h]
            o, ns = _state_step(u_ref[h], w_ref[h], qd_ref[h], kd_ref[h], a_ref[h], g_ref[h], st_ref[h])
            o_ref[:, h * DH:(h + 1) * DH] = o
            st_ref[h] = ns

    blk = pl.BlockSpec((NH, CH, DH), lambda n: (0, n, 0))
    return pl.pallas_call(
        body, name="delta_state_fwd", grid=(nch,),
        in_specs=[blk] * 6,
        out_specs=[pl.BlockSpec((CH, D), lambda n: (n, 0)),
                   pl.BlockSpec((NH, None, DH, DH), lambda n: (0, n, 0, 0))],
        out_shape=[jax.ShapeDtypeStruct((t, D), F32), jax.ShapeDtypeStruct((NH, nch, DH, DH), F32)],
        scratch_shapes=[pltpu.VMEM((NH, DH, DH), F32)],
        compiler_params=_cparams(("arbitrary",)),
    )(*local, gcb)


def _delta_state_bwd(local, gcb, states, do):
    t = gcb.shape[1]
    nch = t // CH

    def body(u_ref, w_ref, qd_ref, kd_ref, a_ref, g_ref, s_ref, do_ref,
             du_ref, dw_ref, dqd_ref, dkd_ref, da_ref, dg_ref, ds_ref):
        @pl.when(pl.program_id(0) == 0)
        def _():
            ds_ref[...] = jnp.zeros_like(ds_ref)

        for h in range(NH):
            _, vjp = jax.vjp(_state_step, u_ref[h], w_ref[h], qd_ref[h], kd_ref[h], a_ref[h], g_ref[h], s_ref[h])
            du, dw, dqd, dkd, da, dg, ds = vjp((do_ref[:, h * DH:(h + 1) * DH], ds_ref[h]))
            du_ref[h] = du
            dw_ref[h] = dw
            dqd_ref[h] = dqd
            dkd_ref[h] = dkd
            da_ref[h] = da
            dg_ref[h] = dg
            ds_ref[h] = ds

    blk = pl.BlockSpec((NH, CH, DH), lambda n: (0, nch - 1 - n, 0))
    return pl.pallas_call(
        body, name="delta_state_bwd", grid=(nch,),
        in_specs=[blk] * 6 + [pl.BlockSpec((NH, None, DH, DH), lambda n: (0, nch - 1 - n, 0, 0)),
                              pl.BlockSpec((CH, D), lambda n: (nch - 1 - n, 0))],
        out_specs=[blk] * 6,
        out_shape=[jax.ShapeDtypeStruct((NH, t, DH), F32)] * 6,
        scratch_shapes=[pltpu.VMEM((NH, DH, DH), F32)],
        compiler_params=_cparams(("arbitrary",)),
    )(*local, gcb, states, do)


def _s5_params(ar, ai, ldt, br2, bi2):
    def body(ar_ref, ai_ref, ld_ref, br_ref, bi_ref, lr_ref, li_ref, bbr_ref, bbi_ref):
        lr, li, bbr, bbi = _s5_params_fn(ar_ref[...], ai_ref[...], ld_ref[...], br_ref[...], bi_ref[...])
        lr_ref[...] = lr
        li_ref[...] = li
        bbr_ref[...] = bbr
        bbi_ref[...] = bbi

    sq = pl.BlockSpec((None, NG, NS), lambda l: (l, 0, 0))
    wide = pl.BlockSpec((None, NG, NS * GS), lambda l: (l, 0, 0))
    return pl.pallas_call(
        body, name="s5_params", grid=(DEPTH,),
        in_specs=[sq, sq, pl.BlockSpec((None, NG, 1), lambda l: (l, 0, 0)), wide, wide],
        out_specs=[sq, sq, wide, wide],
        out_shape=[jax.ShapeDtypeStruct((DEPTH, NG, NS), F32)] * 2
        + [jax.ShapeDtypeStruct((DEPTH, NG, NS * GS), F32)] * 2,
        compiler_params=_cparams(("parallel",)),
    )(ar, ai, ldt, br2, bi2)


def _s5_params_bwd(ar, ai, ldt, br2, bi2, dlr, dli, dbbr, dbbi):
    def body(ar_ref, ai_ref, ld_ref, br_ref, bi_ref, a_ref, b_ref, c_ref, d_ref,
             dar_ref, dai_ref, dld_ref, dbr_ref, dbi_ref):
        _, vjp = jax.vjp(_s5_params_fn, ar_ref[...], ai_ref[...], ld_ref[...], br_ref[...], bi_ref[...])
        dar, dai, dld, dbr, dbi = vjp((a_ref[...], b_ref[...], c_ref[...], d_ref[...]))
        dar_ref[...] = dar
        dai_ref[...] = dai
        dld_ref[...] = dld
        dbr_ref[...] = dbr
        dbi_ref[...] = dbi

    sq = pl.BlockSpec((None, NG, NS), lambda l: (l, 0, 0))
    col = pl.BlockSpec((None, NG, 1), lambda l: (l, 0, 0))
    wide = pl.BlockSpec((None, NG, NS * GS), lambda l: (l, 0, 0))
    return pl.pallas_call(
        body, name="s5_params_bwd", grid=(DEPTH,),
        in_specs=[sq, sq, col, wide, wide, sq, sq, wide, wide],
        out_specs=[sq, sq, col, wide, wide],
        out_shape=[jax.ShapeDtypeStruct((DEPTH, NG, NS), F32)] * 2 + [jax.ShapeDtypeStruct((DEPTH, NG, 1), F32)]
        + [jax.ShapeDtypeStruct((DEPTH, NG, NS * GS), F32)] * 2,
        compiler_params=_cparams(("parallel",)),
    )(ar, ai, ldt, br2, bi2, dlr, dli, dbbr, dbbi)


def _s5_tile_rows(t):
    return _tile(t // 2, 256)


def _s5_fwd(proj, lam, bblk, cblk):
    t = proj.shape[0]
    r = _s5_tile_rows(t)
    nt = t // r
    u0 = 4 * D // DH

    def body(u_ref, lam_ref, b_ref, c_ref, y_ref, car_ref, st_ref, hr_ref, hi_ref, cr_ref, ci_ref):
        @pl.when(pl.program_id(1) == 0)
        def _():
            st_ref[...] = jnp.zeros_like(st_ref)

        car_ref[...] = st_ref[...]
        hr, hi = _s5_states(u_ref[...], lam_ref, b_ref, st_ref, hr_ref, hi_ref, cr_ref, ci_ref)
        y_ref[...] = _mm(hr, c_ref[0]) - _mm(hi, c_ref[1])
        st_ref[0:1, :] = _last_row(hr)
        st_ref[1:2, :] = _last_row(hi)

    scratch = [pltpu.VMEM((8, SW), F32)] + [pltpu.VMEM((SW // DH, r, DH), F32)] * 2 + [pltpu.VMEM((r // SUB, SW), F32)] * 2
    return pl.pallas_call(
        body, name="s5_fwd", grid=(NCB, nt),
        in_specs=[pl.BlockSpec((r, DH), lambda c, i: (i, u0 + c)),
                  pl.BlockSpec((2, 1, SW), lambda c, i: (0, 0, c)),
                  pl.BlockSpec((2, None, DH, SW), lambda c, i: (0, c, 0, 0)),
                  pl.BlockSpec((2, None, SW, DH), lambda c, i: (0, c, 0, 0))],
        out_specs=[pl.BlockSpec((r, DH), lambda c, i: (i, c)),
                   pl.BlockSpec((None, 8, SW), lambda c, i: (i, 0, c))],
        out_shape=[jax.ShapeDtypeStruct((t, D), F32), jax.ShapeDtypeStruct((nt, 8, NG * NS), F32)],
        scratch_shapes=scratch,
        compiler_params=_cparams(("parallel", "arbitrary")),
    )(proj, lam, bblk, cblk)


def _s5_bwd(proj, lam, bblk, cblk, carries, dy, du_skip, exchange=None):
    t = proj.shape[0]
    r = _s5_tile_rows(t)
    nt = t // r
    u0 = 4 * D // DH

    def body(u_ref, lam_ref, b_ref, c_ref, car_ref, dy_ref, dus_ref, du_ref, dlam_ref, db_ref, dc_ref, dst_ref,
             hr_ref, hi_ref, ar_ref, ai_ref, cr_ref, ci_ref):
        first = pl.program_id(1) == 0

        @pl.when(first)
        def _():
            dst_ref[...] = jnp.zeros_like(dst_ref)

        u, dy = u_ref[...], dy_ref[...]
        lr, li = lam_ref[0], lam_ref[1]
        hr, hi = _s5_states(u, lam_ref, b_ref, car_ref, hr_ref, hi_ref, cr_ref, ci_ref)
        dcr2, dci2 = _mm_tn(hr, dy), -_mm_tn(hi, dy)
        last = _rows((r, SW)) == r - 1
        inr, ini = _cmul(lr, -li, dst_ref[0:1, :], dst_ref[1:2, :])
        dhr = _mm_nt(dy, c_ref[0]) + jnp.where(last, inr, 0.0)
        dhi = jnp.where(last, ini, 0.0) - _mm_nt(dy, c_ref[1])
        ar, ai = _scan_tile(dhr, dhi, lr, -li, ar_ref, ai_ref, cr_ref, ci_ref, True)
        top = _rows((r, SW)) == 0
        dst_ref[0:1, :] = jnp.sum(jnp.where(top, ar, 0.0), axis=0, keepdims=True)
        dst_ref[1:2, :] = jnp.sum(jnp.where(top, ai, 0.0), axis=0, keepdims=True)
        du_ref[...] = _mm_nt(ar, b_ref[0]) + _mm_nt(ai, b_ref[1]) + dus_ref[...]
        dbr, dbi = _mm_tn(u, ar), _mm_tn(u, ai)
        pr = _sd(hr, 1) + jnp.where(top, car_ref[0:1, :], 0.0)
        pi = _sd(hi, 1) + jnp.where(top, car_ref[1:2, :], 0.0)
        dlr = jnp.sum(ar * pr + ai * pi, axis=0, keepdims=True)
        dli = jnp.sum(ai * pr - ar * pi, axis=0, keepdims=True)

        @pl.when(first)
        def _():
            dlam_ref[0] = dlr
            dlam_ref[1] = dli
            db_ref[0] = dbr
            db_ref[1] = dbi
            dc_ref[0] = dcr2
            dc_ref[1] = dci2

        @pl.when(jnp.logical_not(first))
        def _():
            dlam_ref[0] += dlr
            dlam_ref[1] += dli
            db_ref[0] += dbr
            db_ref[1] += dbi
            dc_ref[0] += dcr2
            dc_ref[1] += dci2

    grid = (NCB, nt)
    in_specs = [pl.BlockSpec((r, DH), lambda c, i: (nt - 1 - i, u0 + c)),
                pl.BlockSpec((2, 1, SW), lambda c, i: (0, 0, c)),
                pl.BlockSpec((2, None, DH, SW), lambda c, i: (0, c, 0, 0)),
                pl.BlockSpec((2, None, SW, DH), lambda c, i: (0, c, 0, 0)),
                pl.BlockSpec((None, 8, SW), lambda c, i: (nt - 1 - i, 0, c)),
                pl.BlockSpec((r, DH), lambda c, i: (nt - 1 - i, c)),
                pl.BlockSpec((r, DH), lambda c, i: (nt - 1 - i, c))]
    out_specs = [pl.BlockSpec((r, DH), lambda c, i: (nt - 1 - i, c)),
                 pl.BlockSpec((2, 1, SW), lambda c, i: (0, 0, c)),
                 pl.BlockSpec((2, None, DH, SW), lambda c, i: (0, c, 0, 0)),
                 pl.BlockSpec((2, None, SW, DH), lambda c, i: (0, c, 0, 0))]
    out_shape = [jax.ShapeDtypeStruct((t, D), F32), jax.ShapeDtypeStruct((2, 1, NG * NS), F32),
                 jax.ShapeDtypeStruct((2, NCB, DH, SW), F32), jax.ShapeDtypeStruct((2, NCB, SW, DH), F32)]
    scratch = ([pltpu.VMEM((8, SW), F32)] + [pltpu.VMEM((SW // DH, r, DH), F32)] * 4
               + [pltpu.VMEM((r // SUB, SW), F32)] * 2)
    args, sem = [proj, lam, bblk, cblk, carries, dy, du_skip], ("parallel", "arbitrary")
    if exchange is not None:
        body = _carry(body, len(args), len(out_shape), len(scratch), exchange, grid)
        in_specs, out_specs = in_specs + exchange.in_specs, out_specs + exchange.out_specs
        out_shape, scratch, args = out_shape + exchange.out_shape, scratch + exchange.scratch_shapes, args + exchange.srcs
        sem = ("arbitrary", "arbitrary")
    outs = pl.pallas_call(
        body, name="s5_bwd", grid=grid, in_specs=in_specs, out_specs=out_specs, out_shape=out_shape,
        scratch_shapes=scratch, compiler_params=_cparams(sem),
    )(*args)
    return outs[:4], outs[4:]


def _proj_spec(tm, col):
    return pl.BlockSpec((tm, D), lambda i: (i, col))


def _layer_mat(l):
    return pl.BlockSpec((None, D, D), lambda i: (l, 0, 0))


def _mix_fwd(proj, o, s5y, x, hn, dvec, wglu, bglu, wout, npost, l):
    t = x.shape[0]
    tm = _tile(t, 256)

    def body(za_ref, u_ref, zb_ref, ra_ref, rb_ref, o_ref, y_ref, x_ref, hn_ref, d_ref, wg_ref, bg_ref, wo_ref,
             np_ref, xn_ref):
        y0 = _mix_pre(y_ref[...], u_ref[...], d_ref[...])
        gl = _mm(y0, wg_ref[...]) + bg_ref[...]
        m = _mix_mid(o_ref[...], za_ref[...], y0, gl, zb_ref[...], ra_ref[...], rb_ref[...], hn_ref[...])
        out = _mm(m, wo_ref[...])
        xn_ref[...] = _mix_post(x_ref[...], out, np_ref[...])

    act = pl.BlockSpec((tm, D), lambda i: (i, 0))
    return pl.pallas_call(
        body, name="mix_fwd", grid=(t // tm,),
        in_specs=[_proj_spec(tm, 3), _proj_spec(tm, 4), _proj_spec(tm, 5), _proj_spec(tm, 6), _proj_spec(tm, 7),
                  act, act, act, _full((1, DH)), _full((1, D)), _layer_mat(l), _full((1, D)), _layer_mat(l),
                  _full((1, D))],
        out_specs=act,
        out_shape=jax.ShapeDtypeStruct((t, D), F32),
        compiler_params=_cparams(("parallel",)),
    )(proj, proj, proj, proj, proj, o, s5y, x, hn, dvec, wglu, bglu, wout, npost)


def _mix_bwd(proj, o, s5y, x, hn, dvec, wglu, bglu, wout, npost, dxn, l):
    t = x.shape[0]
    tm = _tile(t, 128)

    def body(za_ref, u_ref, zb_ref, ra_ref, rb_ref, o_ref, y_ref, x_ref, hn_ref, d_ref, wg_ref, bg_ref, wo_ref,
             np_ref, dxn_ref,
             dza_ref, du_ref, dzb_ref, dra_ref, drb_ref, do_ref, dy_ref, dx_ref,
             dwg_ref, dwo_ref, dvecs_ref, dhn_ref):
        y0, vjp_pre = jax.vjp(_mix_pre, y_ref[...], u_ref[...], d_ref[...])
        gl = _mm(y0, wg_ref[...]) + bg_ref[...]
        m, vjp_mid = jax.vjp(_mix_mid, o_ref[...], za_ref[...], y0, gl, zb_ref[...], ra_ref[...], rb_ref[...],
                             hn_ref[...])
        out = _mm(m, wo_ref[...])
        _, vjp_post = jax.vjp(_mix_post, x_ref[...], out, np_ref[...])
        dx, dout, dnp = vjp_post(dxn_ref[...])
        dm = _mm_nt(dout, wo_ref[...])
        dwo = _mm_tn(m, dout)
        do, dza, dy0, dgl, dzb, dra, drb, dhn = vjp_mid(dm)
        dwg = _mm_tn(y0, dgl)
        dbg = jnp.sum(dgl, axis=0, keepdims=True)
        dy0 = dy0 + _mm_nt(dgl, wg_ref[...])
        dy, du, dd = vjp_pre(dy0)
        dza_ref[...] = dza
        du_ref[...] = du
        dzb_ref[...] = dzb
        dra_ref[...] = dra
        drb_ref[...] = drb
        do_ref[...] = do
        dy_ref[...] = dy
        dx_ref[...] = dx
        first = pl.program_id(0) == 0

        @pl.when(first)
        def _():
            dwg_ref[...] = dwg
            dwo_ref[...] = dwo
            dvecs_ref[...] = jnp.zeros_like(dvecs_ref)
            dhn_ref[...] = jnp.zeros_like(dhn_ref)

        @pl.when(jnp.logical_not(first))
        def _():
            dwg_ref[...] += dwg
            dwo_ref[...] += dwo

        dvecs_ref[0:1, :] += dd
        dvecs_ref[1:2, :] += dbg
        dvecs_ref[2:3, :] += dnp
        dhn_ref[0:1, :] += dhn

    act = pl.BlockSpec((tm, D), lambda i: (i, 0))
    a = jax.ShapeDtypeStruct((t, D), F32)
    w = jax.ShapeDtypeStruct((D, D), F32)
    return pl.pallas_call(
        body, name="mix_bwd", grid=(t // tm,),
        in_specs=[_proj_spec(tm, 3), _proj_spec(tm, 4), _proj_spec(tm, 5), _proj_spec(tm, 6), _proj_spec(tm, 7),
                  act, act, act, _full((1, DH)), _full((1, D)), _layer_mat(l), _full((1, D)), _layer_mat(l),
                  _full((1, D)), act],
        out_specs=[act] * 8 + [_full((D, D)), _full((D, D)), _full((8, D)), _full((8, DH))],
        out_shape=[a] * 8 + [w, w, jax.ShapeDtypeStruct((8, D), F32), jax.ShapeDtypeStruct((8, DH), F32)],
        compiler_params=_cparams(("arbitrary",)),
    )(proj, proj, proj, proj, proj, o, s5y, x, hn, dvec, wglu, bglu, wout, npost, dxn)


def _loss_grad(y, target):
    t = y.shape[0]
    tm = _tile(t, 512)

    def body(y_ref, t_ref, dy_ref, l_ref):
        err = y_ref[...] - t_ref[...]
        dy_ref[...] = err * (1.0 / D)
        part = jnp.sum(jnp.sum(err * err, axis=1, keepdims=True), axis=0, keepdims=True) * (0.5 / D)
        part = jnp.broadcast_to(part, (8, DH))

        @pl.when(pl.program_id(0) == 0)
        def _():
            l_ref[...] = part

        @pl.when(pl.program_id(0) > 0)
        def _():
            l_ref[...] += part

    act = pl.BlockSpec((tm, D), lambda i: (i, 0))
    return pl.pallas_call(
        body, name="loss_grad", grid=(t // tm,),
        in_specs=[act, act], out_specs=[act, _full((8, DH))],
        out_shape=[jax.ShapeDtypeStruct((t, D), F32), jax.ShapeDtypeStruct((8, DH), F32)],
        compiler_params=_cparams(("arbitrary",)),
    )(y, target)


def _flips(rel):
    x, y, c = lax.axis_index("x"), lax.axis_index("y"), lax.axis_index("c")
    fx, fy, fc = rel
    return (x ^ fx if fx else x, y ^ fy if fy else y, c ^ fc if fc else c)


CHIP_RELS = ((1, 0, 0), (0, 1, 0), (1, 1, 0))
ALL_RELS = tuple((fx, fy, fc) for fx in (0, 1) for fy in (0, 1) for fc in (0, 1) if (fx, fy, fc) != (0, 0, 0))


def _slot_of(pos, by_chip):
    px, py, pc = pos
    return 2 * px + py if by_chip else 4 * px + 2 * py + pc


class _Exchange:
    def __init__(self, srcs, rels, by_chip, scatter):
        self.srcs, self.rels, self.by_chip, self.scatter = list(srcs), rels, by_chip, scatter
        self.narr = len(self.srcs)
        nslot, nsem = NCHIP if by_chip else NDEV, self.narr * len(rels)
        self.in_specs = [pl.BlockSpec(memory_space=pl.ANY)] * self.narr
        self.out_specs = [pl.BlockSpec(memory_space=pl.ANY)] * self.narr
        self.out_shape = [jax.ShapeDtypeStruct((nslot,) + s.shape[-2:], s.dtype) for s in self.srcs]
        self.scratch_shapes = [pltpu.SemaphoreType.DMA((nsem,)), pltpu.SemaphoreType.DMA((nsem,)),
                               pltpu.SemaphoreType.DMA((self.narr,))]

    def _copies(self, src_refs, dst_refs, sems):
        send_sems, recv_sems, local_sems = sems
        my_slot = _slot_of(_flips((0, 0, 0)), self.by_chip)
        local, sends, arrivals = [], [], []
        for a, (src_ref, dst_ref) in enumerate(zip(src_refs, dst_refs)):
            local.append(pltpu.make_async_copy(src_ref.at[my_slot] if self.scatter else src_ref, dst_ref.at[my_slot],
                                               local_sems.at[a]))
            for k, rel in enumerate(self.rels):
                peer = _flips(rel)
                pair = dict(send_sem=send_sems.at[a * len(self.rels) + k], recv_sem=recv_sems.at[a * len(self.rels) + k],
                            device_id=peer, device_id_type=pl.DeviceIdType.MESH)
                part = src_ref.at[_slot_of(peer, self.by_chip)] if self.scatter else src_ref
                sends.append(pltpu.make_async_remote_copy(src_ref=part, dst_ref=dst_ref.at[my_slot], **pair))
                arrivals.append(pltpu.make_async_remote_copy(
                    src_ref=src_ref.at[0] if self.scatter else src_ref, dst_ref=dst_ref.at[_slot_of(peer, self.by_chip)],
                    **pair))
        return local, sends, arrivals

    def start(self, src_refs, dst_refs, sems):
        local, sends, _ = self._copies(src_refs, dst_refs, sems)
        for cp in local + sends:
            cp.start()

    def wait(self, src_refs, dst_refs, sems):
        local, sends, arrivals = self._copies(src_refs, dst_refs, sems)
        for cp in arrivals:
            cp.wait_recv()
        for cp in sends:
            cp.wait_send()
        for cp in local:
            cp.wait()


def _exchange(srcs, rels, by_chip, scatter, name):
    ex = _Exchange(srcs, rels, by_chip, scatter)

    def body(*refs):
        parts = refs[:ex.narr], refs[ex.narr:2 * ex.narr], refs[2 * ex.narr:]
        ex.start(*parts)
        ex.wait(*parts)

    return pl.pallas_call(body, name=name, in_specs=ex.in_specs, out_specs=ex.out_specs, out_shape=ex.out_shape,
                          scratch_shapes=ex.scratch_shapes)(*ex.srcs)


def _sibling_swap(srcs, name):
    narr = len(srcs)

    def body(*refs):
        src_refs, dst_refs = refs[:narr], refs[narr:2 * narr]
        send_sems, recv_sems = refs[2 * narr:]
        peer = _flips((0, 0, 1))
        copies = [pltpu.make_async_remote_copy(src_ref=s, dst_ref=d, send_sem=send_sems.at[a], recv_sem=recv_sems.at[a],
                                               device_id=peer, device_id_type=pl.DeviceIdType.MESH)
                  for a, (s, d) in enumerate(zip(src_refs, dst_refs))]
        for cp in copies:
            cp.start()
        for cp in copies:
            cp.wait()

    return pl.pallas_call(
        body, name=name,
        in_specs=[pl.BlockSpec(memory_space=pl.ANY)] * narr,
        out_specs=[pl.BlockSpec(memory_space=pl.ANY)] * narr,
        out_shape=[jax.ShapeDtypeStruct(s.shape, s.dtype) for s in srcs],
        scratch_shapes=[pltpu.SemaphoreType.DMA((narr,)), pltpu.SemaphoreType.DMA((narr,))],
    )(*srcs)


def _sum_slots(parts, name):
    ns, rows, cols = parts.shape
    tr = _row_tile(rows, 256)

    def body(p_ref, o_ref):
        acc = p_ref[0].astype(F32)
        for s in range(1, ns):
            acc = acc + p_ref[s].astype(F32)
        o_ref[...] = acc

    return pl.pallas_call(
        body, name=name, grid=(rows // tr,),
        in_specs=[pl.BlockSpec((ns, tr, cols), lambda i: (0, i, 0))],
        out_specs=pl.BlockSpec((tr, cols), lambda i: (i, 0)),
        out_shape=jax.ShapeDtypeStruct((rows, cols), F32),
        compiler_params=_cparams(("parallel",)),
    )(parts)


def _adamw(w, g_parts, m, v, name, max_rows=256):
    rows, cols = w.shape
    tr = _row_tile(rows, max_rows)
    c1 = 1.0 / (1.0 - ADAM_B1 ** ADAM_STEP)
    c2 = 1.0 / (1.0 - ADAM_B2 ** ADAM_STEP)
    npart = len(g_parts)

    def body(*refs):
        w_ref, m_ref, v_ref = refs[:3]
        g_refs = refs[3:3 + npart]
        go_ref, d_ref, nm_ref, nv_ref = refs[3 + npart:]
        terms = []
        for g_ref in g_refs:
            terms += [g_ref[...]] if len(g_ref.shape) == 2 else [g_ref[s] for s in range(g_ref.shape[0])]
        g = terms[0]
        for term in terms[1:]:
            g = g + term
        nm = ADAM_B1 * m_ref[...] + (1.0 - ADAM_B1) * g
        nv = ADAM_B2 * v_ref[...] + (1.0 - ADAM_B2) * (g * g)
        d_ref[...] = -ADAM_LR * ((nm * c1) / (jnp.sqrt(nv * c2) + ADAM_EPS) + ADAM_WD * w_ref[...])
        go_ref[...] = g
        nm_ref[...] = nm
        nv_ref[...] = nv

    blk = pl.BlockSpec((tr, cols), lambda i: (i, 0))
    g_specs = [blk if p.ndim == 2 else pl.BlockSpec((p.shape[0], tr, cols), lambda i: (0, i, 0)) for p in g_parts]
    out = jax.ShapeDtypeStruct((rows, cols), F32)
    return pl.pallas_call(
        body, name=name, grid=(rows // tr,),
        in_specs=[blk, blk, blk] + g_specs,
        out_specs=[blk] * 4, out_shape=[out] * 4,
        compiler_params=_cparams(("parallel",)),
    )(w, m, v, *g_parts)


WIN_SHARD = 2052
CONV_SHARD = 768
ROW_SHARD = 256

SMALL = (("norm_pre", (DEPTH, D)), ("a_log", (DEPTH, NH)), ("dt_bias", (DEPTH, NH)), ("head_norm", (DEPTH, DH)),
         ("ssm_a_re", (DEPTH, NG, NS)), ("ssm_a_im", (DEPTH, NG, NS)), ("ssm_log_dt", (DEPTH, NG)),
         ("ssm_b_re", (DEPTH, NG, NS, GS)), ("ssm_b_im", (DEPTH, NG, NS, GS)),
         ("ssm_c_re", (DEPTH, NG, GS, NS)), ("ssm_c_im", (DEPTH, NG, GS, NS)), ("ssm_d", (DEPTH, D)),
         ("b_glu", (DEPTH, D)), ("norm_post", (DEPTH, D)))


def _pad_rows(flat, rows):
    return jnp.pad(flat, (0, rows * D - flat.shape[0])).reshape(rows, D)


def _rows_by_chip(a):
    nl, rows, cols = a.shape
    return a.reshape(nl, NCHIP, rows // NCHIP, cols).transpose(1, 0, 2, 3).reshape(NCHIP, -1, cols)


def _rows_from_chips(a):
    _, rows, cols = a.shape
    return a.reshape(NCHIP, DEPTH, rows // DEPTH, cols).transpose(1, 0, 2, 3).reshape(DEPTH, -1, cols)


def _cols_by_chip(a):
    nl, rows, cols = a.shape
    return a.reshape(nl, rows, NCHIP, cols // NCHIP).transpose(2, 0, 1, 3).reshape(NCHIP, nl * rows, -1)


def _cols_from_chips(a, nl):
    _, rows, cols = a.shape
    return a.reshape(NCHIP, nl, rows // nl, cols).transpose(1, 2, 0, 3).reshape(nl, rows // nl, NCHIP * cols)


SMALL_ROWS = sum(-(-math.prod(s) // (8 * D)) * 8 for _, s in SMALL)
CONV_ROWS = DEPTH * 4 * 3 * D // D


def _pack_small(vals, extra=()):
    parts = []
    for val in tuple(vals) + tuple(extra):
        n = val.size
        parts.append(_pad_rows(val.reshape(-1), -(-n // (8 * D)) * 8))
    return jnp.concatenate(parts, axis=0)


def _unpack_small(flat):
    outs, r0 = [], 0
    for _, shape in SMALL:
        n = math.prod(shape)
        rows = -(-n // (8 * D)) * 8
        outs.append(flat[r0:r0 + rows].reshape(-1)[:n].reshape(shape))
        r0 += rows
    return outs


def _rearrange_cols(w):
    pad = jnp.zeros(w.shape[:-1] + (NCOL - BD0 - 2 * NH,), w.dtype)
    return jnp.concatenate([w[..., :4 * D], w[..., 4 * D + 2 * NH:], w[..., 4 * D:4 * D + 2 * NH], pad], axis=-1)


def _restore_cols(w):
    return jnp.concatenate([w[..., :4 * D], w[..., BD0:BD0 + 2 * NH], w[..., 4 * D:BD0]], axis=-1)


def _block_diag_b(bb2):
    b = bb2.reshape(NCB, GPB, NS, GS)
    eye = jnp.eye(GPB, dtype=F32)
    return jnp.einsum("kgnc,gh->kgchn", b, eye).reshape(NCB, GPB * GS, SW)


def _block_diag_b_t(d):
    return jnp.einsum("kgchn,gh->kgnc", d.reshape(NCB, GPB, GS, GPB, NS), jnp.eye(GPB, dtype=F32)).reshape(NG, NS * GS)


def _block_diag_c(c):
    eye = jnp.eye(GPB, dtype=F32)
    return jnp.einsum("kgcn,gh->kgnhc", c.reshape(NCB, GPB, GS, NS), eye).reshape(NCB, SW, GPB * GS)


def _block_diag_c_t(d):
    return jnp.einsum("kgnhc,gh->kgcn", d.reshape(NCB, GPB, NS, GPB, GS), jnp.eye(GPB, dtype=F32)).reshape(NG, GS, NS)


def _local_step(x, target, weights, conv, small, comm=None):
    weights = list(weights) + [None] * (DEPTH - len(weights))
    ar, ai = small["ssm_a_re"], small["ssm_a_im"]
    ldt = small["ssm_log_dt"].reshape(DEPTH, NG, 1)
    br2 = small["ssm_b_re"].reshape(DEPTH, NG, NS * GS)
    bi2 = small["ssm_b_im"].reshape(DEPTH, NG, NS * GS)
    lr, li, bbr2, bbi2 = _s5_params(ar, ai, ldt, br2, bi2)

    def row(name, l, width):
        return small[name][l].reshape(1, width)

    saved = []
    for l in range(DEPTH):
        gvec = jnp.pad(jnp.stack([small["a_log"][l], small["dt_bias"][l]]), ((0, 6), (NH, DH - 2 * NH)))
        lam = jnp.stack([lr[l].reshape(1, NG * NS), li[l].reshape(1, NG * NS)])
        bblk = jnp.stack([_block_diag_b(bbr2[l]), _block_diag_b(bbi2[l])])
        cblk = jnp.stack([_block_diag_c(small["ssm_c_re"][l]), _block_diag_c(small["ssm_c_im"][l])])
        wcat, wglu, wout = weights[l]
        proj, h = _inproj_fwd(x, row("norm_pre", l, D), wcat, 0)
        qkv = _prep_fwd(proj, conv[l])
        bb, gcb = _gates_fwd(proj, gvec)
        fetch = _Exchange(comm["weight_parts"](l + 1), CHIP_RELS, True, False) if comm and l + 1 < DEPTH else None
        local, t_inv, fetched = _delta_local_fwd(qkv, bb, gcb, fetch)
        if fetch is not None:
            weights[l + 1] = comm["weights_from"](fetched)
        o, states = _delta_state_fwd(local, gcb)
        s5y, carries = _s5_fwd(proj, lam, bblk, cblk)
        xn = _mix_fwd(proj, o, s5y, x, row("head_norm", l, DH), row("ssm_d", l, D), wglu, row("b_glu", l, D),
                      wout, row("norm_post", l, D), 0)
        saved.append((x, proj, h, qkv, bb, gcb, local, t_inv, o, states, s5y, carries, gvec, lam, bblk, cblk))
        x = xn

    dx, loss_part = _loss_grad(x, target)

    g = {k: [None] * DEPTH for k in ("wcat", "conv", "wglu", "wout", "norm_pre", "a_log", "dt_bias", "head_norm",
                                     "ssm_c_re", "ssm_c_im", "ssm_d", "b_glu", "norm_post", "lr", "li", "bbr", "bbi")}
    from_chips, send, send_layer = [None] * DEPTH, None, None
    for l in reversed(range(DEPTH)):
        xl, proj, h, qkv, bb, gcb, local, t_inv, o, states, s5y, carries, gvec, lam, bblk, cblk = saved[l]
        wcat, wglu, wout = weights[l]
        (dza, du_skip, dzb, dra, drb, do, ds5y, dxres, dwg, dwo, dvecs, dhn) = _mix_bwd(
            proj, o, s5y, xl, row("head_norm", l, DH), row("ssm_d", l, D), wglu, row("b_glu", l, D), wout,
            row("norm_post", l, D), dx, 0)
        (du, dlam, dbblk, dcblk), arrived = _s5_bwd(proj, lam, bblk, cblk, carries, ds5y, du_skip, send)
        if send is not None:
            from_chips[send_layer] = arrived
        *dlocal, dgcb_state = _delta_state_bwd(local, gcb, states, do)
        dq, dk, dv, dbb, dgcb = _delta_local_bwd(qkv, bb, gcb, t_inv, dlocal, dgcb_state)
        dbd, dgvec = _gates_bwd(proj, gvec, dbb, dgcb)
        dqkv = jnp.concatenate([dq, dk, dv], axis=0)
        dpre, dconv = _prep_bwd(proj, conv[l], dqkv)
        dproj = jnp.concatenate([dpre, dza, du, dzb, dra, drb, dbd], axis=1)
        dx, dgain = _inproj_bwd_dx(dproj, wcat, xl, row("norm_pre", l, D), dxres, 0)
        g["wcat"][l] = _inproj_bwd_dw(h, dproj)
        g["conv"][l], g["wglu"][l], g["wout"][l] = dconv, dwg, dwo
        if comm:
            send = _Exchange(comm["grad_parts"](g["wcat"][l], dwg, dwo), CHIP_RELS, True, True)
            send_layer = l
        g["norm_pre"][l] = dgain[0]
        g["a_log"][l], g["dt_bias"][l] = dgvec[0, NH:2 * NH], dgvec[1, NH:2 * NH]
        g["head_norm"][l] = dhn[0]
        g["ssm_d"][l], g["b_glu"][l], g["norm_post"][l] = dvecs[0], dvecs[1], dvecs[2]
        g["ssm_c_re"][l], g["ssm_c_im"][l] = _block_diag_c_t(dcblk[0]), _block_diag_c_t(dcblk[1])
        g["lr"][l], g["li"][l] = dlam[0].reshape(NG, NS), dlam[1].reshape(NG, NS)
        g["bbr"][l], g["bbi"][l] = _block_diag_b_t(dbblk[0]), _block_diag_b_t(dbblk[1])
    if comm:
        from_chips[send_layer] = _exchange(send.srcs, CHIP_RELS, True, True, "scatter_grads")
        for k in ("wcat", "wglu", "wout"):
            del g[k]
    g = {k: jnp.stack(v) for k, v in g.items()}
    g["from_chips"] = from_chips
    dar, dai, dldt, dbr2, dbi2 = _s5_params_bwd(ar, ai, ldt, br2, bi2, g["lr"], g["li"], g["bbr"], g["bbi"])
    g["ssm_a_re"], g["ssm_a_im"], g["ssm_log_dt"] = dar, dai, dldt.reshape(DEPTH, NG)
    g["ssm_b_re"] = dbr2.reshape(DEPTH, NG, NS, GS)
    g["ssm_b_im"] = dbi2.reshape(DEPTH, NG, NS, GS)
    return loss_part[0, 0], dx, g


def kernel(x, norm_pre, w_in, conv_w, a_log, dt_bias, head_norm, ssm_a_re, ssm_a_im, ssm_log_dt, ssm_b_re, ssm_b_im, ssm_c_re, ssm_c_im, ssm_d, w_glu, b_glu, w_out, norm_post, loss_target, m_norm_pre, m_w_in, m_conv_w, m_a_log, m_dt_bias, m_head_norm, m_ssm_a_re, m_ssm_a_im, m_ssm_log_dt, m_ssm_b_re, m_ssm_b_im, m_ssm_c_re, m_ssm_c_im, m_ssm_d, m_w_glu, m_b_glu, m_w_out, m_norm_post, v_norm_pre, v_w_in, v_conv_w, v_a_log, v_dt_bias, v_head_norm, v_ssm_a_re, v_ssm_a_im, v_ssm_log_dt, v_ssm_b_re, v_ssm_b_im, v_ssm_c_re, v_ssm_c_im, v_ssm_d, v_w_glu, v_b_glu, v_w_out, v_norm_post):
    args = dict(locals())
    small = {n: args[n] for n, _ in SMALL}

    def flat2(a):
        return a.reshape(-1, a.shape[-1])

    w_in16, w_glu16, w_out16 = w_in.astype(BF16), w_glu.astype(BF16), w_out.astype(BF16)

    def weight_parts(l):
        return [w_in16[l], w_glu16[l], w_out16[l]]

    def weights_from(parts):
        g_in, g_glu, g_out = parts[:3]
        return (_rearrange_cols(_cols_from_chips(g_in, 1)), g_glu.reshape(1, D, D), g_out.reshape(1, D, D))

    def grad_parts(gwcat, gwglu, gwout):
        return [_cols_by_chip(_restore_cols(gwcat[None])).astype(BF16), gwglu.reshape(NCHIP, ROW_SHARD, D).astype(BF16),
                gwout.reshape(NCHIP, ROW_SHARD, D).astype(BF16)]

    first = _exchange(weight_parts(0) + [flat2(conv_w)], CHIP_RELS, True, False, "gather_weights")
    conv = _cols_from_chips(first[3], DEPTH)
    comm = dict(weight_parts=weight_parts, weights_from=weights_from, grad_parts=grad_parts)
    loss_part, dx, g = _local_step(x[0], loss_target[0], [weights_from(first)], conv, small, comm)
    loss = lax.psum(loss_part, ("x", "y", "c"))

    from_chips = [jnp.concatenate([g["from_chips"][l][a] for l in range(DEPTH)], axis=1) for a in range(3)]
    core_sums = [_sum_slots(p, "sum_chips_" + n) for p, n in zip(from_chips, ("in", "glu", "out"))]
    others = _sibling_swap(core_sums, "swap_cores")
    sharded = {}
    for n, mine, other in zip(("w_in", "w_glu", "w_out"), core_sums, others):
        sharded[n] = _adamw(flat2(args[n]), [mine, other], flat2(args["m_" + n]), flat2(args["v_" + n]), "adamw_" + n,
                            max_rows=128)

    (small_parts,) = _exchange([_pack_small([g[n] for n, _ in SMALL], extra=[g["conv"]])], ALL_RELS, False, False,
                               "gather_small")
    small_out = _adamw(_pack_small([args[n] for n, _ in SMALL]), [small_parts],
                       _pack_small([args["m_" + n] for n, _ in SMALL]),
                       _pack_small([args["v_" + n] for n, _ in SMALL]), "adamw_small")
    chip = 2 * lax.axis_index("x") + lax.axis_index("y")
    conv_parts = small_parts[:, SMALL_ROWS:SMALL_ROWS + CONV_ROWS].reshape(NDEV, DEPTH * 4, 3 * D)
    conv_parts = lax.dynamic_slice_in_dim(conv_parts, chip * CONV_SHARD, CONV_SHARD, axis=2)
    sharded["conv_w"] = _adamw(flat2(conv_w), [conv_parts], flat2(m_conv_w), flat2(v_conv_w), "adamw_conv")

    names = ["norm_pre", "w_in", "conv_w", "a_log", "dt_bias", "head_norm", "ssm_a_re", "ssm_a_im", "ssm_log_dt",
             "ssm_b_re", "ssm_b_im", "ssm_c_re", "ssm_c_im", "ssm_d", "w_glu", "b_glu", "w_out", "norm_post"]
    outs = [loss, dx[None]]
    for i in range(4):
        sm = dict(zip([n for n, _ in SMALL], _unpack_small(small_out[i])))
        outs += [sharded[n][i].reshape(args[n].shape) if n in sharded else sm[n] for n in names]
    return tuple(outs)
```

```python
import functools
import math

import jax
import jax.numpy as jnp
from jax import lax
from jax.experimental import pallas as pl
from jax.experimental.pallas import tpu as pltpu

F32 = jnp.float32
BF16 = jnp.bfloat16
HI = lax.Precision.HIGHEST

D = 1024
NH = 8
DH = 128
CH = 128
NG = 64
GS = 16
NS = 64
GPB = 8
NCB = NG // GPB
SW = GPB * NS
NCOL = 8320
BD0 = 8192
EPS = 1e-6
DEPTH = 4
NCHIP = 4
NDEV = 8
VMEM_LIMIT = 56 * 1024 * 1024
GRAD_ACT = jnp.bfloat16

ADAM_LR = 0.001
ADAM_B1 = 0.9
ADAM_B2 = 0.999
ADAM_EPS = 1e-08
ADAM_WD = 0.01
ADAM_STEP = 10


def _cparams(sem=None):
    return pltpu.CompilerParams(dimension_semantics=sem, vmem_limit_bytes=VMEM_LIMIT)


def _full(shape):
    nd = len(shape)
    return pl.BlockSpec(shape, lambda *_: (0,) * nd)


def _rms(x, gain):
    ms = jnp.mean(x * x, axis=-1, keepdims=True)
    return x * lax.rsqrt(ms + EPS) * gain


def _sigmoid(x):
    return 1.0 / (1.0 + jnp.exp(-x))


def _silu(x):
    return x * _sigmoid(x)


def _softplus(x):
    return jnp.maximum(x, 0.0) + jnp.log(1.0 + jnp.exp(-jnp.abs(x)))


def _gelu(x):
    return 0.5 * x * (1.0 + jnp.tanh(math.sqrt(2.0 / math.pi) * (x + 0.044715 * (x * x * x))))


def _dot_bf16(a, b, dims):
    return lax.dot_general(a.astype(BF16), b.astype(BF16), (dims, ((), ())), preferred_element_type=F32)


def _mm_nt(a, b):
    return _dot_bf16(a, b, ((1,), (1,)))


def _mm_tn(a, b):
    return _dot_bf16(a, b, ((0,), (0,)))


@jax.custom_vjp
def _mm(a, b):
    return _dot_bf16(a, b, ((1,), (0,)))


def _mm_fwd(a, b):
    return _dot_bf16(a, b, ((1,), (0,))), (a, b)


def _mm_bwd(res, ct):
    a, b = res
    return _mm_nt(ct, b).astype(a.dtype), _mm_tn(a, ct).astype(b.dtype)


_mm.defvjp(_mm_fwd, _mm_bwd)


@jax.custom_vjp
def _mm_nt_d(a, b):
    return _mm_nt(a, b)


def _mm_nt_d_bwd(res, ct):
    a, b = res
    return _dot_bf16(ct, b, ((1,), (0,))), _mm_tn(ct, a)


_mm_nt_d.defvjp(lambda a, b: (_mm_nt(a, b), (a, b)), _mm_nt_d_bwd)


@jax.custom_vjp
def _mm_tn_d(a, b):
    return _mm_tn(a, b)


def _mm_tn_d_bwd(res, ct):
    a, b = res
    return _mm_nt(b, ct), _dot_bf16(a, ct, ((1,), (0,)))


_mm_tn_d.defvjp(lambda a, b: (_mm_tn(a, b), (a, b)), _mm_tn_d_bwd)


def _split_bf16(a):
    hi = a.astype(BF16)
    return hi, (a - hi.astype(F32)).astype(BF16)


def _dot3(a, b, dims):
    ah, al = _split_bf16(a)
    bh, bl = _split_bf16(b)

    def dot(x, y):
        return lax.dot_general(x, y, (dims, ((), ())), preferred_element_type=F32)

    return dot(ah, bh) + (dot(ah, bl) + dot(al, bh))


@jax.custom_vjp
def _imm(a, b):
    return _dot3(a, b, ((1,), (0,)))


def _imm_bwd(res, ct):
    a, b = res
    return _dot3(ct, b, ((1,), (1,))), _dot3(a, ct, ((0,), (0,)))


_imm.defvjp(lambda a, b: (_dot3(a, b, ((1,), (0,))), (a, b)), _imm_bwd)


def _hmm(a, b):
    return jnp.dot(a, b, precision=HI, preferred_element_type=F32)


def _hmm_nt(a, b):
    return lax.dot_general(a, b, (((1,), (1,)), ((), ())), precision=HI, preferred_element_type=F32)


def _hmm_tn(a, b):
    return lax.dot_general(a, b, (((0,), (0,)), ((), ())), precision=HI, preferred_element_type=F32)


def _rows(shape):
    return lax.broadcasted_iota(jnp.int32, shape, 0)


def _cols(shape):
    return lax.broadcasted_iota(jnp.int32, shape, 1)


def _sd(x, s):
    return jnp.where(_rows(x.shape) >= s, pltpu.roll(x, s, axis=0), 0.0)


def _su(x, s):
    n = x.shape[0]
    return jnp.where(_rows(x.shape) < n - s, pltpu.roll(x, n - s, axis=0), 0.0)


@functools.partial(jax.custom_vjp, nondiff_argnums=(1,))
def _shift_down(x, s):
    return _sd(x, s)


def _shift_down_fwd(x, s):
    return _sd(x, s), None


def _shift_down_bwd(s, _, g):
    return (_su(g, s),)


_shift_down.defvjp(_shift_down_fwd, _shift_down_bwd)


def _last_row(x):
    n = x.shape[0]
    return jnp.sum(jnp.where(_rows(x.shape) == n - 1, x, 0.0), axis=0, keepdims=True)


def _prep_fn(p, w0, w1, w2, w3, qk):
    acc = w3 * p + w2 * _shift_down(p, 1) + w1 * _shift_down(p, 2) + w0 * _shift_down(p, 3)
    a = _silu(acc)
    nrm = lax.rsqrt(jnp.sum(a * a, axis=-1, keepdims=True) + EPS)
    return a * (nrm * qk + (1.0 - qk))


def _gates_fn(bd, av, bv):
    tm = bd.shape[0]
    beta_all = _sigmoid(bd)
    g_all = -jnp.exp(av) * _softplus(bd + bv)
    r, c = _rows((tm, tm)), _cols((tm, tm))
    tri = jnp.where((r // CH == c // CH) & (r >= c), 1.0, 0.0).astype(F32)
    gc_all = _hmm(tri, g_all)
    lane = _cols(bd.shape)
    outs = []
    for h in range(NH):
        b = jnp.sum(jnp.where(lane == h, beta_all, 0.0), axis=1, keepdims=True)
        outs.append(jnp.broadcast_to(b, bd.shape))
    for h in range(NH):
        g = jnp.sum(jnp.where(lane == NH + h, gc_all, 0.0), axis=1, keepdims=True)
        outs.append(jnp.broadcast_to(g, bd.shape))
    return tuple(outs)


def _unit_lower_inv(l_mat):
    n = l_mat.shape[0]
    eye = jnp.where(_rows((n, n)) == _cols((n, n)), 1.0, 0.0).astype(F32)
    p = -l_mat
    r = eye + p
    k = 1
    while 2 * k < n:
        p = _imm(p, p)
        r = r + _imm(r, p)
        k *= 2
    return r


@jax.custom_vjp
def _known_inverse(l_mat, t_inv):
    return t_inv


def _known_inverse_bwd(t_inv, ct):
    d_l = -_dot3(_dot3(t_inv, ct, ((0,), (0,))), t_inv, ((1,), (1,)))
    return d_l, jnp.zeros_like(t_inv)


_known_inverse.defvjp(lambda l_mat, t_inv: (t_inv, t_inv), _known_inverse_bwd)


def _chunk_local(q, k, v, bb, gcb, t_inv=None):
    qs = q * (DH ** -0.5)
    kb = k * bb
    eg = jnp.exp(gcb)
    ii, jj = _rows((CH, CH)), _cols((CH, CH))
    decay = jnp.exp(jnp.where(ii >= jj, gcb - gcb.T, -1e30))
    l_mat = jnp.where(ii > jj, _mm_nt_d(kb, k) * decay, 0.0)
    t_inv = _unit_lower_inv(l_mat) if t_inv is None else _known_inverse(l_mat, t_inv)
    u = _mm(t_inv, v * bb)
    w = _mm(t_inv, kb * eg)
    a_qk = _mm_nt_d(qs, k) * decay
    k_dec = k * jnp.exp(_last_row(gcb) - gcb)
    return (u, w, qs * eg, k_dec, a_qk), t_inv


def _state_step(u, w, q_dec, k_dec, a_qk, gcb, state):
    v_new = u - _mm(w, state)
    o = _mm(q_dec, state) + _mm(a_qk, v_new)
    new_state = state * jnp.exp(_last_row(gcb)) + _mm_tn_d(k_dec, v_new)
    return o, new_state


SUB = 8


def _cmul(ar, ai, br, bi):
    return ar * br - ai * bi, ar * bi + ai * br


def _scan_tile(xr, xi, mr, mi, hr_ref, hi_ref, cr_ref, ci_ref, reverse):
    n, width = xr.shape
    ngroups = n // SUB
    shift_groups = _su if reverse else _sd
    xr, xi = xr.reshape(ngroups, SUB, width), xi.reshape(ngroups, SUB, width)
    pr, pi = mr, mi
    tr, ti = jnp.broadcast_to(mr, (SUB, width)), jnp.broadcast_to(mi, (SUB, width))
    pos = _rows(tr.shape)
    s = 1
    while s < SUB:
        inside = pos < SUB - s if reverse else pos >= s
        shift = SUB - s if reverse else s
        qr, qi = jnp.where(inside, pr, 0.0)[None], jnp.where(inside, pi, 0.0)[None]
        dr, di = _cmul(qr, qi, pltpu.roll(xr, shift, axis=1), pltpu.roll(xi, shift, axis=1))
        xr, xi = xr + dr, xi + di
        er = jnp.where(inside, pltpu.roll(tr, shift, axis=0), 1.0)
        ei = jnp.where(inside, pltpu.roll(ti, shift, axis=0), 0.0)
        tr, ti = _cmul(tr, ti, er, ei)
        pr, pi = _cmul(pr, pi, pr, pi)
        s *= 2
    xr, xi = xr.reshape(n, width), xi.reshape(n, width)
    nlb = width // DH

    def lanes(x, j):
        return x[:, j * DH:(j + 1) * DH]

    for j in range(nlb):
        hr_ref[j] = lanes(xr, j)
        hi_ref[j] = lanes(xi, j)
    edge = pl.ds(0 if reverse else SUB - 1, ngroups, stride=SUB)
    gr = jnp.concatenate([hr_ref.at[j][edge, :] for j in range(nlb)], axis=1)
    gi = jnp.concatenate([hi_ref.at[j][edge, :] for j in range(nlb)], axis=1)
    s = 1
    while s < ngroups:
        dr, di = _cmul(pr, pi, shift_groups(gr, s), shift_groups(gi, s))
        gr, gi = gr + dr, gi + di
        pr, pi = _cmul(pr, pi, pr, pi)
        s *= 2
    cr_ref[...] = shift_groups(gr, 1)
    ci_ref[...] = shift_groups(gi, 1)
    for g in range(ngroups):
        rows = slice(g * SUB, (g + 1) * SUB)
        dr, di = _cmul(tr, ti, cr_ref[g:g + 1, :], ci_ref[g:g + 1, :])
        for j in range(nlb):
            hr_ref[j, rows, :] += lanes(dr, j)
            hi_ref[j, rows, :] += lanes(di, j)
    return (jnp.concatenate([hr_ref[j] for j in range(nlb)], axis=1),
            jnp.concatenate([hi_ref[j] for j in range(nlb)], axis=1))


def _s5_states(u, lam_ref, b_ref, car_ref, hr_ref, hi_ref, cr_ref, ci_ref):
    lr, li = lam_ref[0], lam_ref[1]
    first = _rows((u.shape[0], SW)) == 0
    inr, ini = _cmul(lr, li, car_ref[0:1, :], car_ref[1:2, :])
    xr = _mm(u, b_ref[0]) + jnp.where(first, inr, 0.0)
    xi = _mm(u, b_ref[1]) + jnp.where(first, ini, 0.0)
    return _scan_tile(xr, xi, lr, li, hr_ref, hi_ref, cr_ref, ci_ref, False)


def _s5_params_fn(ar, ai, ldt, br2, bi2):
    dt = jnp.exp(ldt)
    mag = jnp.exp(ar * dt)
    lr, li = mag * jnp.cos(ai * dt), mag * jnp.sin(ai * dt)
    den = ar * ar + ai * ai
    fr = ((lr - 1.0) * ar + li * ai) / den
    fi = (li * ar - (lr - 1.0) * ai) / den
    expand = jnp.where(_cols((NS, NS * GS)) // GS == _rows((NS, NS * GS)), 1.0, 0.0).astype(F32)
    fr2, fi2 = _hmm(fr, expand), _hmm(fi, expand)
    return lr, li, fr2 * br2 - fi2 * bi2, fr2 * bi2 + fi2 * br2


def _head_norm(o, hn):
    parts = []
    for h in range(NH):
        oh = o[:, h * DH:(h + 1) * DH]
        parts.append(oh * lax.rsqrt(jnp.mean(oh * oh, axis=-1, keepdims=True) + EPS) * hn)
    return jnp.concatenate(parts, axis=1)


def _mix_pre(s5y, u, dvec):
    return _gelu(s5y + dvec * u)


def _mix_mid(o, za, y0, gl, zb, ra, rb, hn):
    ya = _head_norm(o, hn) * _silu(za)
    yb = y0 * _sigmoid(gl) * _silu(zb)
    return _sigmoid(ra) * ya + _sigmoid(rb) * yb


def _mix_post(x, out, npost):
    return x + _rms(out, npost)


def _tile(t, want):
    return min(t, want)


def _row_tile(rows, want):
    return max(r for r in range(16, want + 1, 16) if rows % r == 0)


def _inproj_fwd(x, gain, wcat, l):
    t = x.shape[0]
    tm, tn = _tile(t, 1024), 640

    def body(x_ref, g_ref, w_ref, o_ref, h_ref):
        @pl.when(pl.program_id(1) == 0)
        def _():
            h_ref[...] = _rms(x_ref[...], g_ref[...]).astype(h_ref.dtype)
        o_ref[...] = _dot_bf16(h_ref[...], w_ref[...], ((1,), (0,)))

    return pl.pallas_call(
        body, name="inproj_fwd", grid=(t // tm, NCOL // tn),
        in_specs=[pl.BlockSpec((tm, D), lambda i, j: (i, 0)), _full((1, D)),
                  pl.BlockSpec((None, D, tn), lambda i, j: (l, 0, j))],
        out_specs=[pl.BlockSpec((tm, tn), lambda i, j: (i, j)), pl.BlockSpec((tm, D), lambda i, j: (i, 0))],
        out_shape=[jax.ShapeDtypeStruct((t, NCOL), F32), jax.ShapeDtypeStruct((t, D), wcat.dtype)],
        compiler_params=_cparams(("parallel", "arbitrary")),
    )(x, gain, wcat)


def _inproj_bwd_dx(dproj, wcat, x, gain, dxres, l):
    t = x.shape[0]
    tm, tk = _tile(t, 1024), 640
    nk = NCOL // tk

    def body(dp_ref, w_ref, x_ref, g_ref, r_ref, dx_ref, dg_ref, acc_ref):
        i, k = pl.program_id(0), pl.program_id(1)

        @pl.when(k == 0)
        def _():
            acc_ref[...] = jnp.zeros_like(acc_ref)

        acc_ref[...] += _mm_nt(dp_ref[...], w_ref[...])

        @pl.when(k == nk - 1)
        def _():
            _, vjp = jax.vjp(_rms, x_ref[...], g_ref[...])
            dx, dg = vjp(acc_ref[...])
            dx_ref[...] = r_ref[...] + dx

            @pl.when(i == 0)
            def _():
                dg_ref[...] = dg

            @pl.when(i > 0)
            def _():
                dg_ref[...] += dg

    return pl.pallas_call(
        body, name="inproj_bwd_dx", grid=(t // tm, nk),
        in_specs=[pl.BlockSpec((tm, tk), lambda i, k: (i, k)), pl.BlockSpec((None, D, tk), lambda i, k: (l, 0, k)),
                  pl.BlockSpec((tm, D), lambda i, k: (i, 0)), _full((1, D)),
                  pl.BlockSpec((tm, D), lambda i, k: (i, 0))],
        out_specs=[pl.BlockSpec((tm, D), lambda i, k: (i, 0)), _full((1, D))],
        out_shape=[jax.ShapeDtypeStruct((t, D), F32), jax.ShapeDtypeStruct((1, D), F32)],
        scratch_shapes=[pltpu.VMEM((tm, D), F32)],
        compiler_params=_cparams(("arbitrary", "arbitrary")),
    )(dproj, wcat, x, gain, dxres)


def _inproj_bwd_dw(h, dproj):
    t = h.shape[0]
    tm, tn = _tile(t, 512), 1664

    def body(h_ref, dp_ref, o_ref):
        @pl.when(pl.program_id(1) == 0)
        def _():
            o_ref[...] = jnp.zeros_like(o_ref)

        o_ref[...] += _mm_tn(h_ref[...], dp_ref[...])

    return pl.pallas_call(
        body, name="inproj_bwd_dw", grid=(NCOL // tn, t // tm),
        in_specs=[pl.BlockSpec((tm, D), lambda j, i: (i, 0)), pl.BlockSpec((tm, tn), lambda j, i: (i, j))],
        out_specs=pl.BlockSpec((D, tn), lambda j, i: (0, j)),
        out_shape=jax.ShapeDtypeStruct((D, NCOL), F32),
        compiler_params=_cparams(("parallel", "arbitrary")),
    )(h, dproj)


def _prep_fwd(proj, cw):
    t = proj.shape[0]

    def body(p_ref, w_ref, o_ref):
        qk = (pl.program_id(0) < 2 * NH).astype(F32)
        o_ref[...] = _prep_fn(p_ref[...], w_ref[0:1, :], w_ref[1:2, :], w_ref[2:3, :], w_ref[3:4, :], qk)

    return pl.pallas_call(
        body, name="prep_fwd", grid=(3 * NH,),
        in_specs=[pl.BlockSpec((t, DH), lambda c: (0, c)), pl.BlockSpec((4, DH), lambda c: (0, c))],
        out_specs=pl.BlockSpec((None, t, DH), lambda c: (c, 0, 0)),
        out_shape=jax.ShapeDtypeStruct((3 * NH, t, DH), F32),
        compiler_params=_cparams(("parallel",)),
    )(proj, cw)


def _prep_bwd(proj, cw, dqkv):
    t = proj.shape[0]

    def body(p_ref, w_ref, d_ref, dp_ref, dw_ref):
        qk = (pl.program_id(0) < 2 * NH).astype(F32)
        _, vjp = jax.vjp(lambda p, w0, w1, w2, w3: _prep_fn(p, w0, w1, w2, w3, qk),
                         p_ref[...], w_ref[0:1, :], w_ref[1:2, :], w_ref[2:3, :], w_ref[3:4, :])
        dp, dw0, dw1, dw2, dw3 = vjp(d_ref[...])
        dp_ref[...] = dp.astype(dp_ref.dtype)
        dw_ref[0:1, :] = dw0
        dw_ref[1:2, :] = dw1
        dw_ref[2:3, :] = dw2
        dw_ref[3:4, :] = dw3

    return pl.pallas_call(
        body, name="prep_bwd", grid=(3 * NH,),
        in_specs=[pl.BlockSpec((t, DH), lambda c: (0, c)), pl.BlockSpec((4, DH), lambda c: (0, c)),
                  pl.BlockSpec((None, t, DH), lambda c: (c, 0, 0))],
        out_specs=[pl.BlockSpec((t, DH), lambda c: (0, c)), pl.BlockSpec((4, DH), lambda c: (0, c))],
        out_shape=[jax.ShapeDtypeStruct((t, 3 * D), GRAD_ACT), jax.ShapeDtypeStruct((4, 3 * D), F32)],
        compiler_params=_cparams(("parallel",)),
    )(proj, cw, dqkv)


def _gates_fwd(proj, gvec):
    t = proj.shape[0]
    tm = _tile(t, 512)

    def body(p_ref, gv_ref, b_ref, g_ref):
        outs = _gates_fn(p_ref[...], gv_ref[0:1, :], gv_ref[1:2, :])
        for h in range(NH):
            b_ref[h] = outs[h]
            g_ref[h] = outs[NH + h]

    spec = pl.BlockSpec((NH, tm, DH), lambda i: (0, i, 0))
    return pl.pallas_call(
        body, name="gates_fwd", grid=(t // tm,),
        in_specs=[pl.BlockSpec((tm, DH), lambda i: (i, BD0 // DH)), _full((8, DH))],
        out_specs=[spec, spec],
        out_shape=[jax.ShapeDtypeStruct((NH, t, DH), F32)] * 2,
        compiler_params=_cparams(("parallel",)),
    )(proj, gvec)


def _gates_bwd(proj, gvec, dbb, dgcb):
    t = proj.shape[0]
    tm = _tile(t, 512)

    def body(p_ref, gv_ref, db_ref, dg_ref, dp_ref, dgv_ref):
        _, vjp = jax.vjp(_gates_fn, p_ref[...], gv_ref[0:1, :], gv_ref[1:2, :])
        cts = tuple(db_ref[h] for h in range(NH)) + tuple(dg_ref[h] for h in range(NH))
        dp, da, db = vjp(cts)
        dp_ref[...] = dp.astype(dp_ref.dtype)

        @pl.when(pl.program_id(0) == 0)
        def _():
            dgv_ref[...] = jnp.zeros_like(dgv_ref)

        dgv_ref[0:1, :] += da
        dgv_ref[1:2, :] += db

    spec = pl.BlockSpec((NH, tm, DH), lambda i: (0, i, 0))
    return pl.pallas_call(
        body, name="gates_bwd", grid=(t // tm,),
        in_specs=[pl.BlockSpec((tm, DH), lambda i: (i, BD0 // DH)), _full((8, DH)), spec, spec],
        out_specs=[pl.BlockSpec((tm, DH), lambda i: (i, 0)), _full((8, DH))],
        out_shape=[jax.ShapeDtypeStruct((t, DH), GRAD_ACT), jax.ShapeDtypeStruct((8, DH), F32)],
        compiler_params=_cparams(("arbitrary",)),
    )(proj, gvec, dbb, dgcb)


def _chunks_per_step(nch):
    return 2 if nch % 2 == 0 else 1


def _grid_ends(grid):
    def first():
        return functools.reduce(jnp.logical_and, [pl.program_id(a) == 0 for a in range(len(grid))])

    def last():
        return functools.reduce(jnp.logical_and, [pl.program_id(a) == n - 1 for a, n in enumerate(grid)])

    return first, last


def _carry(body, n_in, n_out, n_scratch, exchange, grid):
    first, last = _grid_ends(grid)
    na = exchange.narr

    def wrapped(*refs):
        a, b = n_in, n_in + na
        c, d = b + n_out, b + n_out + na
        e = d + n_scratch
        srcs, dsts, sems = refs[a:b], refs[c:d], refs[e:]

        @pl.when(first())
        def _():
            exchange.start(srcs, dsts, sems)

        body(*(refs[:a] + refs[b:c] + refs[d:e]))

        @pl.when(last())
        def _():
            exchange.wait(srcs, dsts, sems)

    return wrapped


def _delta_local_fwd(qkv, bb, gcb, exchange=None):
    t = qkv.shape[1]
    cps = _chunks_per_step(t // CH)
    rows = cps * CH
    grid = (NH, t // rows)

    def body(q_ref, k_ref, v_ref, b_ref, g_ref, *out_refs):
        for c in range(cps):
            sl = slice(c * CH, (c + 1) * CH)
            outs, t_inv = _chunk_local(q_ref[sl, :], k_ref[sl, :], v_ref[sl, :], b_ref[sl, :], g_ref[sl, :])
            for ref, val in zip(out_refs, outs + (t_inv,)):
                ref[sl, :] = val

    def blk(off):
        return pl.BlockSpec((None, rows, DH), lambda h, n: (h + off, n, 0))

    in_specs = [blk(0), blk(NH), blk(2 * NH), blk(0), blk(0)]
    out_specs = [blk(0)] * 6
    out_shape = [jax.ShapeDtypeStruct((NH, t, DH), F32)] * 6
    args, scratch, sem = [qkv, qkv, qkv, bb, gcb], [], ("parallel", "parallel")
    if exchange is not None:
        body = _carry(body, 5, 6, 0, exchange, grid)
        in_specs, out_specs = in_specs + exchange.in_specs, out_specs + exchange.out_specs
        out_shape, scratch, args = out_shape + exchange.out_shape, exchange.scratch_shapes, args + exchange.srcs
        sem = ("arbitrary", "arbitrary")
    outs = pl.pallas_call(
        body, name="delta_local_fwd", grid=grid, in_specs=in_specs, out_specs=out_specs, out_shape=out_shape,
        scratch_shapes=scratch, compiler_params=_cparams(sem),
    )(*args)
    return outs[:5], outs[5], outs[6:]


def _delta_local_bwd(qkv, bb, gcb, t_inv, cts, dgcb_state):
    t = qkv.shape[1]
    cps = _chunks_per_step(t // CH)
    rows = cps * CH

    def body(q_ref, k_ref, v_ref, b_ref, g_ref, ti_ref, du_ref, dw_ref, dqd_ref, dkd_ref, da_ref, dgs_ref,
             dq_ref, dk_ref, dv_ref, db_ref, dg_ref):
        for c in range(cps):
            sl = slice(c * CH, (c + 1) * CH)
            t_inv_c = ti_ref[sl, :]
            _, vjp = jax.vjp(lambda *a: _chunk_local(*a, t_inv=t_inv_c)[0],
                             q_ref[sl, :], k_ref[sl, :], v_ref[sl, :], b_ref[sl, :], g_ref[sl, :])
            dq, dk, dv, db, dg = vjp((du_ref[sl, :], dw_ref[sl, :], dqd_ref[sl, :], dkd_ref[sl, :], da_ref[sl, :]))
            dq_ref[sl, :] = dq
            dk_ref[sl, :] = dk
            dv_ref[sl, :] = dv
            db_ref[sl, :] = db
            dg_ref[sl, :] = dg + dgs_ref[sl, :]

    def blk(off):
        return pl.BlockSpec((None, rows, DH), lambda h, n: (h + off, n, 0))

    return pl.pallas_call(
        body, name="delta_local_bwd", grid=(NH, t // rows),
        in_specs=[blk(0), blk(NH), blk(2 * NH)] + [blk(0)] * 9,
        out_specs=[blk(0)] * 5,
        out_shape=[jax.ShapeDtypeStruct((NH, t, DH), F32)] * 5,
        compiler_params=_cparams(("parallel", "parallel")),
    )(qkv, qkv, qkv, bb, gcb, t_inv, *cts, dgcb_state)


def _delta_state_fwd(local, gcb):
    t = gcb.shape[1]
    nch = t // CH

    def body(u_ref, w_ref, qd_ref, kd_ref, a_ref, g_ref, o_ref, s_ref, st_ref):
        @pl.when(pl.program_id(0) == 0)
        def _():
            st_ref[...] = jnp.zeros_like(st_ref)

        for h in range(NH):
            s_ref[h] = st_ref[h]
            o, ns = _state_step(u_ref[h], w_ref[h], qd_ref[h], kd_ref[h], a_ref[h], g_ref[h], st_ref[h])
            o_ref[:, h * DH:(h + 1) * DH] = o
            st_ref[h] = ns

    blk = pl.BlockSpec((NH, CH, DH), lambda n: (0, n, 0))
    return pl.pallas_call(
        body, name="delta_state_fwd", grid=(nch,),
        in_specs=[blk] * 6,
        out_specs=[pl.BlockSpec((CH, D), lambda n: (n, 0)),
                   pl.BlockSpec((NH, None, DH, DH), lambda n: (0, n, 0, 0))],
        out_shape=[jax.ShapeDtypeStruct((t, D), F32), jax.ShapeDtypeStruct((NH, nch, DH, DH), F32)],
        scratch_shapes=[pltpu.VMEM((NH, DH, DH), F32)],
        compiler_params=_cparams(("arbitrary",)),
    )(*local, gcb)


def _delta_state_bwd(local, gcb, states, do):
    t = gcb.shape[1]
    nch = t // CH

    def body(u_ref, w_ref, qd_ref, kd_ref, a_ref, g_ref, s_ref, do_ref,
             du_ref, dw_ref, dqd_ref, dkd_ref, da_ref, dg_ref, ds_ref):
        @pl.when(pl.program_id(0) == 0)
        def _():
            ds_ref[...] = jnp.zeros_like(ds_ref)

        for h in range(NH):
            _, vjp = jax.vjp(_state_step, u_ref[h], w_ref[h], qd_ref[h], kd_ref[h], a_ref[h], g_ref[h], s_ref[h])
            du, dw, dqd, dkd, da, dg, ds = vjp((do_ref[:, h * DH:(h + 1) * DH], ds_ref[h]))
            du_ref[h] = du
            dw_ref[h] = dw
            dqd_ref[h] = dqd
            dkd_ref[h] = dkd
            da_ref[h] = da
            dg_ref[h] = dg
            ds_ref[h] = ds

    blk = pl.BlockSpec((NH, CH, DH), lambda n: (0, nch - 1 - n, 0))
    return pl.pallas_call(
        body, name="delta_state_bwd", grid=(nch,),
        in_specs=[blk] * 6 + [pl.BlockSpec((NH, None, DH, DH), lambda n: (0, nch - 1 - n, 0, 0)),
                              pl.BlockSpec((CH, D), lambda n: (nch - 1 - n, 0))],
        out_specs=[blk] * 6,
        out_shape=[jax.ShapeDtypeStruct((NH, t, DH), F32)] * 6,
        scratch_shapes=[pltpu.VMEM((NH, DH, DH), F32)],
        compiler_params=_cparams(("arbitrary",)),
    )(*local, gcb, states, do)


def _s5_params(ar, ai, ldt, br2, bi2):
    def body(ar_ref, ai_ref, ld_ref, br_ref, bi_ref, lr_ref, li_ref, bbr_ref, bbi_ref):
        lr, li, bbr, bbi = _s5_params_fn(ar_ref[...], ai_ref[...], ld_ref[...], br_ref[...], bi_ref[...])
        lr_ref[...] = lr
        li_ref[...] = li
        bbr_ref[...] = bbr
        bbi_ref[...] = bbi

    sq = pl.BlockSpec((None, NG, NS), lambda l: (l, 0, 0))
    wide = pl.BlockSpec((None, NG, NS * GS), lambda l: (l, 0, 0))
    return pl.pallas_call(
        body, name="s5_params", grid=(DEPTH,),
        in_specs=[sq, sq, pl.BlockSpec((None, NG, 1), lambda l: (l, 0, 0)), wide, wide],
        out_specs=[sq, sq, wide, wide],
        out_shape=[jax.ShapeDtypeStruct((DEPTH, NG, NS), F32)] * 2
        + [jax.ShapeDtypeStruct((DEPTH, NG, NS * GS), F32)] * 2,
        compiler_params=_cparams(("parallel",)),
    )(ar, ai, ldt, br2, bi2)


def _s5_params_bwd(ar, ai, ldt, br2, bi2, dlr, dli, dbbr, dbbi):
    def body(ar_ref, ai_ref, ld_ref, br_ref, bi_ref, a_ref, b_ref, c_ref, d_ref,
             dar_ref, dai_ref, dld_ref, dbr_ref, dbi_ref):
        _, vjp = jax.vjp(_s5_params_fn, ar_ref[...], ai_ref[...], ld_ref[...], br_ref[...], bi_ref[...])
        dar, dai, dld, dbr, dbi = vjp((a_ref[...], b_ref[...], c_ref[...], d_ref[...]))
        dar_ref[...] = dar
        dai_ref[...] = dai
        dld_ref[...] = dld
        dbr_ref[...] = dbr
        dbi_ref[...] = dbi

    sq = pl.BlockSpec((None, NG, NS), lambda l: (l, 0, 0))
    col = pl.BlockSpec((None, NG, 1), lambda l: (l, 0, 0))
    wide = pl.BlockSpec((None, NG, NS * GS), lambda l: (l, 0, 0))
    return pl.pallas_call(
        body, name="s5_params_bwd", grid=(DEPTH,),
        in_specs=[sq, sq, col, wide, wide, sq, sq, wide, wide],
        out_specs=[sq, sq, col, wide, wide],
        out_shape=[jax.ShapeDtypeStruct((DEPTH, NG, NS), F32)] * 2 + [jax.ShapeDtypeStruct((DEPTH, NG, 1), F32)]
        + [jax.ShapeDtypeStruct((DEPTH, NG, NS * GS), F32)] * 2,
        compiler_params=_cparams(("parallel",)),
    )(ar, ai, ldt, br2, bi2, dlr, dli, dbbr, dbbi)


def _s5_tile_rows(t):
    return _tile(t // 2, 256)


def _s5_fwd(proj, lam, bblk, cblk):
    t = proj.shape[0]
    r = _s5_tile_rows(t)
    nt = t // r
    u0 = 4 * D // DH

    def body(u_ref, lam_ref, b_ref, c_ref, y_ref, car_ref, st_ref, hr_ref, hi_ref, cr_ref, ci_ref):
        @pl.when(pl.program_id(1) == 0)
        def _():
            st_ref[...] = jnp.zeros_like(st_ref)

        car_ref[...] = st_ref[...]
        hr, hi = _s5_states(u_ref[...], lam_ref, b_ref, st_ref, hr_ref, hi_ref, cr_ref, ci_ref)
        y_ref[...] = _mm(hr, c_ref[0]) - _mm(hi, c_ref[1])
        st_ref[0:1, :] = _last_row(hr)
        st_ref[1:2, :] = _last_row(hi)

    scratch = [pltpu.VMEM((8, SW), F32)] + [pltpu.VMEM((SW // DH, r, DH), F32)] * 2 + [pltpu.VMEM((r // SUB, SW), F32)] * 2
    return pl.pallas_call(
        body, name="s5_fwd", grid=(NCB, nt),
        in_specs=[pl.BlockSpec((r, DH), lambda c, i: (i, u0 + c)),
                  pl.BlockSpec((2, 1, SW), lambda c, i: (0, 0, c)),
                  pl.BlockSpec((2, None, DH, SW), lambda c, i: (0, c, 0, 0)),
                  pl.BlockSpec((2, None, SW, DH), lambda c, i: (0, c, 0, 0))],
        out_specs=[pl.BlockSpec((r, DH), lambda c, i: (i, c)),
                   pl.BlockSpec((None, 8, SW), lambda c, i: (i, 0, c))],
        out_shape=[jax.ShapeDtypeStruct((t, D), F32), jax.ShapeDtypeStruct((nt, 8, NG * NS), F32)],
        scratch_shapes=scratch,
        compiler_params=_cparams(("parallel", "arbitrary")),
    )(proj, lam, bblk, cblk)


def _s5_bwd(proj, lam, bblk, cblk, carries, dy, du_skip, exchange=None):
    t = proj.shape[0]
    r = _s5_tile_rows(t)
    nt = t // r
    u0 = 4 * D // DH

    def body(u_ref, lam_ref, b_ref, c_ref, car_ref, dy_ref, dus_ref, du_ref, dlam_ref, db_ref, dc_ref, dst_ref,
             hr_ref, hi_ref, ar_ref, ai_ref, cr_ref, ci_ref):
        first = pl.program_id(1) == 0

        @pl.when(first)
        def _():
            dst_ref[...] = jnp.zeros_like(dst_ref)

        u, dy = u_ref[...], dy_ref[...]
        lr, li = lam_ref[0], lam_ref[1]
        hr, hi = _s5_states(u, lam_ref, b_ref, car_ref, hr_ref, hi_ref, cr_ref, ci_ref)
        dcr2, dci2 = _mm_tn(hr, dy), -_mm_tn(hi, dy)
        last = _rows((r, SW)) == r - 1
        inr, ini = _cmul(lr, -li, dst_ref[0:1, :], dst_ref[1:2, :])
        dhr = _mm_nt(dy, c_ref[0]) + jnp.where(last, inr, 0.0)
        dhi = jnp.where(last, ini, 0.0) - _mm_nt(dy, c_ref[1])
        ar, ai = _scan_tile(dhr, dhi, lr, -li, ar_ref, ai_ref, cr_ref, ci_ref, True)
        top = _rows((r, SW)) == 0
        dst_ref[0:1, :] = jnp.sum(jnp.where(top, ar, 0.0), axis=0, keepdims=True)
        dst_ref[1:2, :] = jnp.sum(jnp.where(top, ai, 0.0), axis=0, keepdims=True)
        du_ref[...] = (_mm_nt(ar, b_ref[0]) + _mm_nt(ai, b_ref[1]) + dus_ref[...]).astype(du_ref.dtype)
        dbr, dbi = _mm_tn(u, ar), _mm_tn(u, ai)
        pr = _sd(hr, 1) + jnp.where(top, car_ref[0:1, :], 0.0)
        pi = _sd(hi, 1) + jnp.where(top, car_ref[1:2, :], 0.0)
        dlr = jnp.sum(ar * pr + ai * pi, axis=0, keepdims=True)
        dli = jnp.sum(ai * pr - ar * pi, axis=0, keepdims=True)

        @pl.when(first)
        def _():
            dlam_ref[0] = dlr
            dlam_ref[1] = dli
            db_ref[0] = dbr
            db_ref[1] = dbi
            dc_ref[0] = dcr2
            dc_ref[1] = dci2

        @pl.when(jnp.logical_not(first))
        def _():
            dlam_ref[0] += dlr
            dlam_ref[1] += dli
            db_ref[0] += dbr
            db_ref[1] += dbi
            dc_ref[0] += dcr2
            dc_ref[1] += dci2

    grid = (NCB, nt)
    in_specs = [pl.BlockSpec((r, DH), lambda c, i: (nt - 1 - i, u0 + c)),
                pl.BlockSpec((2, 1, SW), lambda c, i: (0, 0, c)),
                pl.BlockSpec((2, None, DH, SW), lambda c, i: (0, c, 0, 0)),
                pl.BlockSpec((2, None, SW, DH), lambda c, i: (0, c, 0, 0)),
                pl.BlockSpec((None, 8, SW), lambda c, i: (nt - 1 - i, 0, c)),
                pl.BlockSpec((r, DH), lambda c, i: (nt - 1 - i, c)),
                pl.BlockSpec((r, DH), lambda c, i: (nt - 1 - i, c))]
    out_specs = [pl.BlockSpec((r, DH), lambda c, i: (nt - 1 - i, c)),
                 pl.BlockSpec((2, 1, SW), lambda c, i: (0, 0, c)),
                 pl.BlockSpec((2, None, DH, SW), lambda c, i: (0, c, 0, 0)),
                 pl.BlockSpec((2, None, SW, DH), lambda c, i: (0, c, 0, 0))]
    out_shape = [jax.ShapeDtypeStruct((t, D), GRAD_ACT), jax.ShapeDtypeStruct((2, 1, NG * NS), F32),
                 jax.ShapeDtypeStruct((2, NCB, DH, SW), F32), jax.ShapeDtypeStruct((2, NCB, SW, DH), F32)]
    scratch = ([pltpu.VMEM((8, SW), F32)] + [pltpu.VMEM((SW // DH, r, DH), F32)] * 4
               + [pltpu.VMEM((r // SUB, SW), F32)] * 2)
    args, sem = [proj, lam, bblk, cblk, carries, dy, du_skip], ("parallel", "arbitrary")
    if exchange is not None:
        body = _carry(body, len(args), len(out_shape), len(scratch), exchange, grid)
        in_specs, out_specs = in_specs + exchange.in_specs, out_specs + exchange.out_specs
        out_shape, scratch, args = out_shape + exchange.out_shape, scratch + exchange.scratch_shapes, args + exchange.srcs
        sem = ("arbitrary", "arbitrary")
    outs = pl.pallas_call(
        body, name="s5_bwd", grid=grid, in_specs=in_specs, out_specs=out_specs, out_shape=out_shape,
        scratch_shapes=scratch, compiler_params=_cparams(sem),
    )(*args)
    return outs[:4], outs[4:]


def _proj_spec(tm, col):
    return pl.BlockSpec((tm, D), lambda i: (i, col))


def _layer_mat(l):
    return pl.BlockSpec((None, D, D), lambda i: (l, 0, 0))


def _mix_fwd(proj, o, s5y, x, hn, dvec, wglu, bglu, wout, npost, l):
    t = x.shape[0]
    tm = _tile(t, 256)

    def body(za_ref, u_ref, zb_ref, ra_ref, rb_ref, o_ref, y_ref, x_ref, hn_ref, d_ref, wg_ref, bg_ref, wo_ref,
             np_ref, xn_ref):
        y0 = _mix_pre(y_ref[...], u_ref[...], d_ref[...])
        gl = _mm(y0, wg_ref[...]) + bg_ref[...]
        m = _mix_mid(o_ref[...], za_ref[...], y0, gl, zb_ref[...], ra_ref[...], rb_ref[...], hn_ref[...])
        out = _mm(m, wo_ref[...])
        xn_ref[...] = _mix_post(x_ref[...], out, np_ref[...])

    act = pl.BlockSpec((tm, D), lambda i: (i, 0))
    return pl.pallas_call(
        body, name="mix_fwd", grid=(t // tm,),
        in_specs=[_proj_spec(tm, 3), _proj_spec(tm, 4), _proj_spec(tm, 5), _proj_spec(tm, 6), _proj_spec(tm, 7),
                  act, act, act, _full((1, DH)), _full((1, D)), _layer_mat(l), _full((1, D)), _layer_mat(l),
                  _full((1, D))],
        out_specs=act,
        out_shape=jax.ShapeDtypeStruct((t, D), F32),
        compiler_params=_cparams(("parallel",)),
    )(proj, proj, proj, proj, proj, o, s5y, x, hn, dvec, wglu, bglu, wout, npost)


def _mix_bwd(proj, o, s5y, x, hn, dvec, wglu, bglu, wout, npost, dxn, l):
    t = x.shape[0]
    tm = _tile(t, 128)

    def body(za_ref, u_ref, zb_ref, ra_ref, rb_ref, o_ref, y_ref, x_ref, hn_ref, d_ref, wg_ref, bg_ref, wo_ref,
             np_ref, dxn_ref,
             dza_ref, du_ref, dzb_ref, dra_ref, drb_ref, do_ref, dy_ref, dx_ref,
             dwg_ref, dwo_ref, dvecs_ref, dhn_ref):
        y0, vjp_pre = jax.vjp(_mix_pre, y_ref[...], u_ref[...], d_ref[...])
        gl = _mm(y0, wg_ref[...]) + bg_ref[...]
        m, vjp_mid = jax.vjp(_mix_mid, o_ref[...], za_ref[...], y0, gl, zb_ref[...], ra_ref[...], rb_ref[...],
                             hn_ref[...])
        out = _mm(m, wo_ref[...])
        _, vjp_post = jax.vjp(_mix_post, x_ref[...], out, np_ref[...])
        dx, dout, dnp = vjp_post(dxn_ref[...])
        dm = _mm_nt(dout, wo_ref[...])
        dwo = _mm_tn(m, dout)
        do, dza, dy0, dgl, dzb, dra, drb, dhn = vjp_mid(dm)
        dwg = _mm_tn(y0, dgl)
        dbg = jnp.sum(dgl, axis=0, keepdims=True)
        dy0 = dy0 + _mm_nt(dgl, wg_ref[...])
        dy, du, dd = vjp_pre(dy0)
        dza_ref[...] = dza.astype(dza_ref.dtype)
        du_ref[...] = du
        dzb_ref[...] = dzb.astype(dzb_ref.dtype)
        dra_ref[...] = dra.astype(dra_ref.dtype)
        drb_ref[...] = drb.astype(drb_ref.dtype)
        do_ref[...] = do
        dy_ref[...] = dy
        dx_ref[...] = dx
        first = pl.program_id(0) == 0

        @pl.when(first)
        def _():
            dwg_ref[...] = dwg
            dwo_ref[...] = dwo
            dvecs_ref[...] = jnp.zeros_like(dvecs_ref)
            dhn_ref[...] = jnp.zeros_like(dhn_ref)

        @pl.when(jnp.logical_not(first))
        def _():
            dwg_ref[...] += dwg
            dwo_ref[...] += dwo

        dvecs_ref[0:1, :] += dd
        dvecs_ref[1:2, :] += dbg
        dvecs_ref[2:3, :] += dnp
        dhn_ref[0:1, :] += dhn

    act = pl.BlockSpec((tm, D), lambda i: (i, 0))
    a, ga = jax.ShapeDtypeStruct((t, D), F32), jax.ShapeDtypeStruct((t, D), GRAD_ACT)
    w = jax.ShapeDtypeStruct((D, D), F32)
    return pl.pallas_call(
        body, name="mix_bwd", grid=(t // tm,),
        in_specs=[_proj_spec(tm, 3), _proj_spec(tm, 4), _proj_spec(tm, 5), _proj_spec(tm, 6), _proj_spec(tm, 7),
                  act, act, act, _full((1, DH)), _full((1, D)), _layer_mat(l), _full((1, D)), _layer_mat(l),
                  _full((1, D)), act],
        out_specs=[act] * 8 + [_full((D, D)), _full((D, D)), _full((8, D)), _full((8, DH))],
        out_shape=[ga, a, ga, ga, ga, a, a, a, w, w, jax.ShapeDtypeStruct((8, D), F32),
                   jax.ShapeDtypeStruct((8, DH), F32)],
        compiler_params=_cparams(("arbitrary",)),
    )(proj, proj, proj, proj, proj, o, s5y, x, hn, dvec, wglu, bglu, wout, npost, dxn)


def _loss_grad(y, target):
    t = y.shape[0]
    tm = _tile(t, 512)

    def body(y_ref, t_ref, dy_ref, l_ref):
        err = y_ref[...] - t_ref[...]
        dy_ref[...] = err * (1.0 / D)
        part = jnp.sum(jnp.sum(err * err, axis=1, keepdims=True), axis=0, keepdims=True) * (0.5 / D)
        part = jnp.broadcast_to(part, (8, DH))

        @pl.when(pl.program_id(0) == 0)
        def _():
            l_ref[...] = part

        @pl.when(pl.program_id(0) > 0)
        def _():
            l_ref[...] += part

    act = pl.BlockSpec((tm, D), lambda i: (i, 0))
    return pl.pallas_call(
        body, name="loss_grad", grid=(t // tm,),
        in_specs=[act, act], out_specs=[act, _full((8, DH))],
        out_shape=[jax.ShapeDtypeStruct((t, D), F32), jax.ShapeDtypeStruct((8, DH), F32)],
        compiler_params=_cparams(("arbitrary",)),
    )(y, target)


def _flips(rel):
    x, y, c = lax.axis_index("x"), lax.axis_index("y"), lax.axis_index("c")
    fx, fy, fc = rel
    return (x ^ fx if fx else x, y ^ fy if fy else y, c ^ fc if fc else c)


CHIP_RELS = ((1, 0, 0), (0, 1, 0), (1, 1, 0))
ALL_RELS = tuple((fx, fy, fc) for fx in (0, 1) for fy in (0, 1) for fc in (0, 1) if (fx, fy, fc) != (0, 0, 0))


def _slot_of(pos, by_chip):
    px, py, pc = pos
    return 2 * px + py if by_chip else 4 * px + 2 * py + pc


class _Exchange:
    def __init__(self, srcs, rels, by_chip, scatter):
        self.srcs, self.rels, self.by_chip, self.scatter = list(srcs), rels, by_chip, scatter
        self.narr = len(self.srcs)
        nslot, nsem = NCHIP if by_chip else NDEV, self.narr * len(rels)
        self.in_specs = [pl.BlockSpec(memory_space=pl.ANY)] * self.narr
        self.out_specs = [pl.BlockSpec(memory_space=pl.ANY)] * self.narr
        self.out_shape = [jax.ShapeDtypeStruct((nslot,) + s.shape[-2:], s.dtype) for s in self.srcs]
        self.scratch_shapes = [pltpu.SemaphoreType.DMA((nsem,)), pltpu.SemaphoreType.DMA((nsem,)),
                               pltpu.SemaphoreType.DMA((self.narr,))]

    def _copies(self, src_refs, dst_refs, sems):
        send_sems, recv_sems, local_sems = sems
        my_slot = _slot_of(_flips((0, 0, 0)), self.by_chip)
        local, sends, arrivals = [], [], []
        for a, (src_ref, dst_ref) in enumerate(zip(src_refs, dst_refs)):
            local.append(pltpu.make_async_copy(src_ref.at[my_slot] if self.scatter else src_ref, dst_ref.at[my_slot],
                                               local_sems.at[a]))
            for k, rel in enumerate(self.rels):
                peer = _flips(rel)
                pair = dict(send_sem=send_sems.at[a * len(self.rels) + k], recv_sem=recv_sems.at[a * len(self.rels) + k],
                            device_id=peer, device_id_type=pl.DeviceIdType.MESH)
                part = src_ref.at[_slot_of(peer, self.by_chip)] if self.scatter else src_ref
                sends.append(pltpu.make_async_remote_copy(src_ref=part, dst_ref=dst_ref.at[my_slot], **pair))
                arrivals.append(pltpu.make_async_remote_copy(
                    src_ref=src_ref.at[0] if self.scatter else src_ref, dst_ref=dst_ref.at[_slot_of(peer, self.by_chip)],
                    **pair))
        return local, sends, arrivals

    def start(self, src_refs, dst_refs, sems):
        local, sends, _ = self._copies(src_refs, dst_refs, sems)
        for cp in local + sends:
            cp.start()

    def wait(self, src_refs, dst_refs, sems):
        local, sends, arrivals = self._copies(src_refs, dst_refs, sems)
        for cp in arrivals:
            cp.wait_recv()
        for cp in sends:
            cp.wait_send()
        for cp in local:
            cp.wait()


def _exchange(srcs, rels, by_chip, scatter, name):
    ex = _Exchange(srcs, rels, by_chip, scatter)

    def body(*refs):
        parts = refs[:ex.narr], refs[ex.narr:2 * ex.narr], refs[2 * ex.narr:]
        ex.start(*parts)
        ex.wait(*parts)

    return pl.pallas_call(body, name=name, in_specs=ex.in_specs, out_specs=ex.out_specs, out_shape=ex.out_shape,
                          scratch_shapes=ex.scratch_shapes)(*ex.srcs)


def _sibling_swap(srcs, name):
    narr = len(srcs)

    def body(*refs):
        src_refs, dst_refs = refs[:narr], refs[narr:2 * narr]
        send_sems, recv_sems = refs[2 * narr:]
        peer = _flips((0, 0, 1))
        copies = [pltpu.make_async_remote_copy(src_ref=s, dst_ref=d, send_sem=send_sems.at[a], recv_sem=recv_sems.at[a],
                                               device_id=peer, device_id_type=pl.DeviceIdType.MESH)
                  for a, (s, d) in enumerate(zip(src_refs, dst_refs))]
        for cp in copies:
            cp.start()
        for cp in copies:
            cp.wait()

    return pl.pallas_call(
        body, name=name,
        in_specs=[pl.BlockSpec(memory_space=pl.ANY)] * narr,
        out_specs=[pl.BlockSpec(memory_space=pl.ANY)] * narr,
        out_shape=[jax.ShapeDtypeStruct(s.shape, s.dtype) for s in srcs],
        scratch_shapes=[pltpu.SemaphoreType.DMA((narr,)), pltpu.SemaphoreType.DMA((narr,))],
    )(*srcs)


def _all_reduce(src, name):
    rows, cols = src.shape
    r = rows // NDEV
    nrel = len(ALL_RELS)

    def body(src_ref, out_ref, parts_ref, mine_ref, send_sems, recv_sems):
        my_slot = _slot_of(_flips((0, 0, 0)), False)

        def piece(ref, slot):
            return ref.at[pl.ds(pl.multiple_of(slot * r, 8), r), :]

        def copies(phase):
            out = []
            for k, rel in enumerate(ALL_RELS):
                peer = _flips(rel)
                pair = dict(send_sem=send_sems.at[phase * nrel + k], recv_sem=recv_sems.at[phase * nrel + k],
                            device_id=peer, device_id_type=pl.DeviceIdType.MESH)
                if phase == 0:
                    out.append(pltpu.make_async_remote_copy(src_ref=piece(src_ref, _slot_of(peer, False)),
                                                            dst_ref=parts_ref.at[my_slot], **pair))
                else:
                    out.append(pltpu.make_async_remote_copy(src_ref=mine_ref, dst_ref=piece(out_ref, my_slot), **pair))
            return out

        first = copies(0)
        for cp in first:
            cp.start()
        parts_ref[my_slot] = piece(src_ref, my_slot)[...]
        for cp in first:
            cp.wait_recv()
        acc = parts_ref[0]
        for s in range(1, NDEV):
            acc = acc + parts_ref[s]
        mine_ref[...] = acc
        second = copies(1)
        for cp in second:
            cp.start()
        piece(out_ref, my_slot)[...] = acc
        for cp in second:
            cp.wait_recv()
        for cp in first + second:
            cp.wait_send()

    return pl.pallas_call(
        body, name=name,
        in_specs=[pl.BlockSpec(memory_space=pltpu.VMEM)], out_specs=pl.BlockSpec(memory_space=pltpu.VMEM),
        out_shape=jax.ShapeDtypeStruct(src.shape, src.dtype),
        scratch_shapes=[pltpu.VMEM((NDEV, r, cols), src.dtype), pltpu.VMEM((r, cols), src.dtype),
                        pltpu.SemaphoreType.DMA((2 * nrel,)), pltpu.SemaphoreType.DMA((2 * nrel,))],
        compiler_params=pltpu.CompilerParams(vmem_limit_bytes=VMEM_LIMIT),
    )(src)


def _sum_slots(parts, name):
    ns, rows, cols = parts.shape
    tr = _row_tile(rows, 256)

    def body(p_ref, o_ref):
        acc = p_ref[0].astype(F32)
        for s in range(1, ns):
            acc = acc + p_ref[s].astype(F32)
        o_ref[...] = acc

    return pl.pallas_call(
        body, name=name, grid=(rows // tr,),
        in_specs=[pl.BlockSpec((ns, tr, cols), lambda i: (0, i, 0))],
        out_specs=pl.BlockSpec((tr, cols), lambda i: (i, 0)),
        out_shape=jax.ShapeDtypeStruct((rows, cols), F32),
        compiler_params=_cparams(("parallel",)),
    )(parts)


def _adamw(w, g_parts, m, v, name, max_rows=256):
    rows, cols = w.shape
    tr = _row_tile(rows, max_rows)
    c1 = 1.0 / (1.0 - ADAM_B1 ** ADAM_STEP)
    c2 = 1.0 / (1.0 - ADAM_B2 ** ADAM_STEP)
    npart = len(g_parts)

    def body(*refs):
        w_ref, m_ref, v_ref = refs[:3]
        g_refs = refs[3:3 + npart]
        go_ref, d_ref, nm_ref, nv_ref = refs[3 + npart:]
        terms = []
        for g_ref in g_refs:
            terms += [g_ref[...]] if len(g_ref.shape) == 2 else [g_ref[s] for s in range(g_ref.shape[0])]
        g = terms[0]
        for term in terms[1:]:
            g = g + term
        nm = ADAM_B1 * m_ref[...] + (1.0 - ADAM_B1) * g
        nv = ADAM_B2 * v_ref[...] + (1.0 - ADAM_B2) * (g * g)
        d_ref[...] = -ADAM_LR * ((nm * c1) / (jnp.sqrt(nv * c2) + ADAM_EPS) + ADAM_WD * w_ref[...])
        go_ref[...] = g
        nm_ref[...] = nm
        nv_ref[...] = nv

    blk = pl.BlockSpec((tr, cols), lambda i: (i, 0))
    g_specs = [blk if p.ndim == 2 else pl.BlockSpec((p.shape[0], tr, cols), lambda i: (0, i, 0)) for p in g_parts]
    out = jax.ShapeDtypeStruct((rows, cols), F32)
    return pl.pallas_call(
        body, name=name, grid=(rows // tr,),
        in_specs=[blk, blk, blk] + g_specs,
        out_specs=[blk] * 4, out_shape=[out] * 4,
        compiler_params=_cparams(("parallel",)),
    )(w, m, v, *g_parts)


WIN_SHARD = 2052
CONV_SHARD = 768
ROW_SHARD = 256

SMALL = (("norm_pre", (DEPTH, D)), ("a_log", (DEPTH, NH)), ("dt_bias", (DEPTH, NH)), ("head_norm", (DEPTH, DH)),
         ("ssm_a_re", (DEPTH, NG, NS)), ("ssm_a_im", (DEPTH, NG, NS)), ("ssm_log_dt", (DEPTH, NG)),
         ("ssm_b_re", (DEPTH, NG, NS, GS)), ("ssm_b_im", (DEPTH, NG, NS, GS)),
         ("ssm_c_re", (DEPTH, NG, GS, NS)), ("ssm_c_im", (DEPTH, NG, GS, NS)), ("ssm_d", (DEPTH, D)),
         ("b_glu", (DEPTH, D)), ("norm_post", (DEPTH, D)))


def _pad_rows(flat, rows):
    return jnp.pad(flat, (0, rows * D - flat.shape[0])).reshape(rows, D)


def _rows_by_chip(a):
    nl, rows, cols = a.shape
    return a.reshape(nl, NCHIP, rows // NCHIP, cols).transpose(1, 0, 2, 3).reshape(NCHIP, -1, cols)


def _rows_from_chips(a):
    _, rows, cols = a.shape
    return a.reshape(NCHIP, DEPTH, rows // DEPTH, cols).transpose(1, 0, 2, 3).reshape(DEPTH, -1, cols)


def _cols_by_chip(a):
    nl, rows, cols = a.shape
    return a.reshape(nl, rows, NCHIP, cols // NCHIP).transpose(2, 0, 1, 3).reshape(NCHIP, nl * rows, -1)


def _cols_from_chips(a, nl):
    _, rows, cols = a.shape
    return a.reshape(NCHIP, nl, rows // nl, cols).transpose(1, 2, 0, 3).reshape(nl, rows // nl, NCHIP * cols)


SMALL_ROWS = sum(-(-math.prod(s) // (8 * D)) * 8 for _, s in SMALL)
CONV_ROWS = DEPTH * 4 * 3 * D // D


def _pack_small(vals, extra=()):
    parts = []
    for val in tuple(vals) + tuple(extra):
        n = val.size
        parts.append(_pad_rows(val.reshape(-1), -(-n // (8 * D)) * 8))
    return jnp.concatenate(parts, axis=0)


def _unpack_small(flat):
    outs, r0 = [], 0
    for _, shape in SMALL:
        n = math.prod(shape)
        rows = -(-n // (8 * D)) * 8
        outs.append(flat[r0:r0 + rows].reshape(-1)[:n].reshape(shape))
        r0 += rows
    return outs


def _rearrange_cols(w):
    pad = jnp.zeros(w.shape[:-1] + (NCOL - BD0 - 2 * NH,), w.dtype)
    return jnp.concatenate([w[..., :4 * D], w[..., 4 * D + 2 * NH:], w[..., 4 * D:4 * D + 2 * NH], pad], axis=-1)


def _restore_cols(w):
    return jnp.concatenate([w[..., :4 * D], w[..., BD0:BD0 + 2 * NH], w[..., 4 * D:BD0]], axis=-1)


def _block_diag_b(bb2):
    b = bb2.reshape(NCB, GPB, NS, GS)
    eye = jnp.eye(GPB, dtype=F32)
    return jnp.einsum("kgnc,gh->kgchn", b, eye).reshape(NCB, GPB * GS, SW)


def _block_diag_b_t(d):
    return jnp.einsum("kgchn,gh->kgnc", d.reshape(NCB, GPB, GS, GPB, NS), jnp.eye(GPB, dtype=F32)).reshape(NG, NS * GS)


def _block_diag_c(c):
    eye = jnp.eye(GPB, dtype=F32)
    return jnp.einsum("kgcn,gh->kgnhc", c.reshape(NCB, GPB, GS, NS), eye).reshape(NCB, SW, GPB * GS)


def _block_diag_c_t(d):
    return jnp.einsum("kgnhc,gh->kgcn", d.reshape(NCB, GPB, NS, GPB, GS), jnp.eye(GPB, dtype=F32)).reshape(NG, GS, NS)


def _local_step(x, target, weights, conv, small, comm=None):
    weights = list(weights) + [None] * (DEPTH - len(weights))
    ar, ai = small["ssm_a_re"], small["ssm_a_im"]
    ldt = small["ssm_log_dt"].reshape(DEPTH, NG, 1)
    br2 = small["ssm_b_re"].reshape(DEPTH, NG, NS * GS)
    bi2 = small["ssm_b_im"].reshape(DEPTH, NG, NS * GS)
    lr, li, bbr2, bbi2 = _s5_params(ar, ai, ldt, br2, bi2)

    def row(name, l, width):
        return small[name][l].reshape(1, width)

    saved = []
    for l in range(DEPTH):
        gvec = jnp.pad(jnp.stack([small["a_log"][l], small["dt_bias"][l]]), ((0, 6), (NH, DH - 2 * NH)))
        lam = jnp.stack([lr[l].reshape(1, NG * NS), li[l].reshape(1, NG * NS)])
        bblk = jnp.stack([_block_diag_b(bbr2[l]), _block_diag_b(bbi2[l])])
        cblk = jnp.stack([_block_diag_c(small["ssm_c_re"][l]), _block_diag_c(small["ssm_c_im"][l])])
        wcat, wglu, wout = weights[l]
        proj, h = _inproj_fwd(x, row("norm_pre", l, D), wcat, 0)
        qkv = _prep_fwd(proj, conv[l])
        bb, gcb = _gates_fwd(proj, gvec)
        fetch = _Exchange(comm["weight_parts"](l + 1), CHIP_RELS, True, False) if comm and l + 1 < DEPTH else None
        local, t_inv, fetched = _delta_local_fwd(qkv, bb, gcb, fetch)
        if fetch is not None:
            weights[l + 1] = comm["weights_from"](fetched)
        o, states = _delta_state_fwd(local, gcb)
        s5y, carries = _s5_fwd(proj, lam, bblk, cblk)
        xn = _mix_fwd(proj, o, s5y, x, row("head_norm", l, DH), row("ssm_d", l, D), wglu, row("b_glu", l, D),
                      wout, row("norm_post", l, D), 0)
        saved.append((x, proj, h, qkv, bb, gcb, local, t_inv, o, states, s5y, carries, gvec, lam, bblk, cblk))
        x = xn

    dx, loss_part = _loss_grad(x, target)

    g = {k: [None] * DEPTH for k in ("wcat", "conv", "wglu", "wout", "norm_pre", "a_log", "dt_bias", "head_norm",
                                     "ssm_c_re", "ssm_c_im", "ssm_d", "b_glu", "norm_post", "lr", "li", "bbr", "bbi")}
    from_chips, send, send_layer = [None] * DEPTH, None, None
    for l in reversed(range(DEPTH)):
        xl, proj, h, qkv, bb, gcb, local, t_inv, o, states, s5y, carries, gvec, lam, bblk, cblk = saved[l]
        wcat, wglu, wout = weights[l]
        (dza, du_skip, dzb, dra, drb, do, ds5y, dxres, dwg, dwo, dvecs, dhn) = _mix_bwd(
            proj, o, s5y, xl, row("head_norm", l, DH), row("ssm_d", l, D), wglu, row("b_glu", l, D), wout,
            row("norm_post", l, D), dx, 0)
        (du, dlam, dbblk, dcblk), arrived = _s5_bwd(proj, lam, bblk, cblk, carries, ds5y, du_skip, send)
        if send is not None:
            from_chips[send_layer] = arrived
        *dlocal, dgcb_state = _delta_state_bwd(local, gcb, states, do)
        dq, dk, dv, dbb, dgcb = _delta_local_bwd(qkv, bb, gcb, t_inv, dlocal, dgcb_state)
        dbd, dgvec = _gates_bwd(proj, gvec, dbb, dgcb)
        dqkv = jnp.concatenate([dq, dk, dv], axis=0)
        dpre, dconv = _prep_bwd(proj, conv[l], dqkv)
        dproj = jnp.concatenate([dpre, dza, du, dzb, dra, drb, dbd], axis=1)
        dx, dgain = _inproj_bwd_dx(dproj, wcat, xl, row("norm_pre", l, D), dxres, 0)
        g["wcat"][l] = _inproj_bwd_dw(h, dproj)
        g["conv"][l], g["wglu"][l], g["wout"][l] = dconv, dwg, dwo
        if comm:
            send = _Exchange(comm["grad_parts"](g["wcat"][l], dwg, dwo), CHIP_RELS, True, True)
            send_layer = l
        g["norm_pre"][l] = dgain[0]
        g["a_log"][l], g["dt_bias"][l] = dgvec[0, NH:2 * NH], dgvec[1, NH:2 * NH]
        g["head_norm"][l] = dhn[0]
        g["ssm_d"][l], g["b_glu"][l], g["norm_post"][l] = dvecs[0], dvecs[1], dvecs[2]
        g["ssm_c_re"][l], g["ssm_c_im"][l] = _block_diag_c_t(dcblk[0]), _block_diag_c_t(dcblk[1])
        g["lr"][l], g["li"][l] = dlam[0].reshape(NG, NS), dlam[1].reshape(NG, NS)
        g["bbr"][l], g["bbi"][l] = _block_diag_b_t(dbblk[0]), _block_diag_b_t(dbblk[1])
    if comm:
        from_chips[send_layer] = _exchange(send.srcs, CHIP_RELS, True, True, "scatter_grads")
        for k in ("wcat", "wglu", "wout"):
            del g[k]
    g = {k: jnp.stack(v) for k, v in g.items()}
    g["from_chips"] = from_chips
    dar, dai, dldt, dbr2, dbi2 = _s5_params_bwd(ar, ai, ldt, br2, bi2, g["lr"], g["li"], g["bbr"], g["bbi"])
    g["ssm_a_re"], g["ssm_a_im"], g["ssm_log_dt"] = dar, dai, dldt.reshape(DEPTH, NG)
    g["ssm_b_re"] = dbr2.reshape(DEPTH, NG, NS, GS)
    g["ssm_b_im"] = dbi2.reshape(DEPTH, NG, NS, GS)
    return loss_part[0, 0], dx, g


def kernel(x, norm_pre, w_in, conv_w, a_log, dt_bias, head_norm, ssm_a_re, ssm_a_im, ssm_log_dt, ssm_b_re, ssm_b_im, ssm_c_re, ssm_c_im, ssm_d, w_glu, b_glu, w_out, norm_post, loss_target, m_norm_pre, m_w_in, m_conv_w, m_a_log, m_dt_bias, m_head_norm, m_ssm_a_re, m_ssm_a_im, m_ssm_log_dt, m_ssm_b_re, m_ssm_b_im, m_ssm_c_re, m_ssm_c_im, m_ssm_d, m_w_glu, m_b_glu, m_w_out, m_norm_post, v_norm_pre, v_w_in, v_conv_w, v_a_log, v_dt_bias, v_head_norm, v_ssm_a_re, v_ssm_a_im, v_ssm_log_dt, v_ssm_b_re, v_ssm_b_im, v_ssm_c_re, v_ssm_c_im, v_ssm_d, v_w_glu, v_b_glu, v_w_out, v_norm_post):
    args = dict(locals())
    small = {n: args[n] for n, _ in SMALL}

    def flat2(a):
        return a.reshape(-1, a.shape[-1])

    w_in16, w_glu16, w_out16 = w_in.astype(BF16), w_glu.astype(BF16), w_out.astype(BF16)

    def weight_parts(l):
        return [w_in16[l], w_glu16[l], w_out16[l]]

    def weights_from(parts):
        g_in, g_glu, g_out = parts[:3]
        return (_rearrange_cols(_cols_from_chips(g_in, 1)), g_glu.reshape(1, D, D), g_out.reshape(1, D, D))

    def grad_parts(gwcat, gwglu, gwout):
        return [_cols_by_chip(_restore_cols(gwcat[None])).astype(BF16), gwglu.reshape(NCHIP, ROW_SHARD, D).astype(BF16),
                gwout.reshape(NCHIP, ROW_SHARD, D).astype(BF16)]

    first = _exchange(weight_parts(0) + [flat2(conv_w)], CHIP_RELS, True, False, "gather_weights")
    conv = _cols_from_chips(first[3], DEPTH)
    comm = dict(weight_parts=weight_parts, weights_from=weights_from, grad_parts=grad_parts)
    loss_part, dx, g = _local_step(x[0], loss_target[0], [weights_from(first)], conv, small, comm)
    loss = lax.psum(loss_part, ("x", "y", "c"))

    from_chips = [jnp.concatenate([g["from_chips"][l][a] for l in range(DEPTH)], axis=1) for a in range(3)]
    core_sums = [_sum_slots(p, "sum_chips_" + n) for p, n in zip(from_chips, ("in", "glu", "out"))]
    others = _sibling_swap(core_sums, "swap_cores")
    sharded = {}
    for n, mine, other in zip(("w_in", "w_glu", "w_out"), core_sums, others):
        sharded[n] = _adamw(flat2(args[n]), [mine, other], flat2(args["m_" + n]), flat2(args["v_" + n]), "adamw_" + n,
                            max_rows=128)

    pad = jnp.zeros(((-(SMALL_ROWS + CONV_ROWS)) % (8 * NDEV), D), F32)
    small_sum = _all_reduce(_pack_small([g[n] for n, _ in SMALL], extra=[g["conv"], pad]), "reduce_small")
    small_out = _adamw(_pack_small([args[n] for n, _ in SMALL]), [small_sum],
                       _pack_small([args["m_" + n] for n, _ in SMALL]),
                       _pack_small([args["v_" + n] for n, _ in SMALL]), "adamw_small")
    chip = 2 * lax.axis_index("x") + lax.axis_index("y")
    conv_sum = small_sum[SMALL_ROWS:SMALL_ROWS + CONV_ROWS].reshape(DEPTH * 4, 3 * D)
    conv_sum = lax.dynamic_slice_in_dim(conv_sum, chip * CONV_SHARD, CONV_SHARD, axis=1)
    sharded["conv_w"] = _adamw(flat2(conv_w), [conv_sum], flat2(m_conv_w), flat2(v_conv_w), "adamw_conv")

    names = ["norm_pre", "w_in", "conv_w", "a_log", "dt_bias", "head_norm", "ssm_a_re", "ssm_a_im", "ssm_log_dt",
             "ssm_b_re", "ssm_b_im", "ssm_c_re", "ssm_c_im", "ssm_d", "w_glu", "b_glu", "w_out", "norm_post"]
    outs = [loss, dx[None]]
    for i in range(4):
        sm = dict(zip([n for n, _ in SMALL], _unpack_small(small_out[i])))
        outs += [sharded[n][i].reshape(args[n].shape) if n in sharded else sm[n] for n in names]
    return tuple(outs)
```

```python
import functools
import math

import jax
import jax.numpy as jnp
from jax import lax
from jax.experimental import pallas as pl
from jax.experimental.pallas import tpu as pltpu

F32 = jnp.float32
BF16 = jnp.bfloat16
HI = lax.Precision.HIGHEST

D = 1024
NH = 8
DH = 128
CH = 128
NG = 64
GS = 16
NS = 64
GPB = 8
NCB = NG // GPB
SW = GPB * NS
NCOL = 8320
BD0 = 8192
EPS = 1e-6
DEPTH = 4
NCHIP = 4
NDEV = 8
VMEM_LIMIT = 56 * 1024 * 1024
GRAD_ACT = jnp.bfloat16

ADAM_LR = 0.001
ADAM_B1 = 0.9
ADAM_B2 = 0.999
ADAM_EPS = 1e-08
ADAM_WD = 0.01
ADAM_STEP = 10


def _cparams(sem=None):
    return pltpu.CompilerParams(dimension_semantics=sem, vmem_limit_bytes=VMEM_LIMIT)


def _full(shape):
    nd = len(shape)
    return pl.BlockSpec(shape, lambda *_: (0,) * nd)


def _rms(x, gain):
    ms = jnp.mean(x * x, axis=-1, keepdims=True)
    return x * lax.rsqrt(ms + EPS) * gain


def _sigmoid(x):
    return 1.0 / (1.0 + jnp.exp(-x))


def _silu(x):
    return x * _sigmoid(x)


def _softplus(x):
    return jnp.maximum(x, 0.0) + jnp.log(1.0 + jnp.exp(-jnp.abs(x)))


def _gelu(x):
    return 0.5 * x * (1.0 + jnp.tanh(math.sqrt(2.0 / math.pi) * (x + 0.044715 * (x * x * x))))


def _dot_bf16(a, b, dims):
    return lax.dot_general(a.astype(BF16), b.astype(BF16), (dims, ((), ())), preferred_element_type=F32)


def _mm_nt(a, b):
    return _dot_bf16(a, b, ((1,), (1,)))


def _mm_tn(a, b):
    return _dot_bf16(a, b, ((0,), (0,)))


@jax.custom_vjp
def _mm(a, b):
    return _dot_bf16(a, b, ((1,), (0,)))


def _mm_fwd(a, b):
    return _dot_bf16(a, b, ((1,), (0,))), (a, b)


def _mm_bwd(res, ct):
    a, b = res
    return _mm_nt(ct, b).astype(a.dtype), _mm_tn(a, ct).astype(b.dtype)


_mm.defvjp(_mm_fwd, _mm_bwd)


@jax.custom_vjp
def _mm_nt_d(a, b):
    return _mm_nt(a, b)


def _mm_nt_d_bwd(res, ct):
    a, b = res
    return _dot_bf16(ct, b, ((1,), (0,))), _mm_tn(ct, a)


_mm_nt_d.defvjp(lambda a, b: (_mm_nt(a, b), (a, b)), _mm_nt_d_bwd)


@jax.custom_vjp
def _mm_tn_d(a, b):
    return _mm_tn(a, b)


def _mm_tn_d_bwd(res, ct):
    a, b = res
    return _mm_nt(b, ct), _dot_bf16(a, ct, ((1,), (0,)))


_mm_tn_d.defvjp(lambda a, b: (_mm_tn(a, b), (a, b)), _mm_tn_d_bwd)


def _split_bf16(a):
    hi = a.astype(BF16)
    return hi, (a - hi.astype(F32)).astype(BF16)


def _dot3(a, b, dims):
    ah, al = _split_bf16(a)
    bh, bl = _split_bf16(b)

    def dot(x, y):
        return lax.dot_general(x, y, (dims, ((), ())), preferred_element_type=F32)

    return dot(ah, bh) + (dot(ah, bl) + dot(al, bh))


@jax.custom_vjp
def _imm(a, b):
    return _dot3(a, b, ((1,), (0,)))


def _imm_bwd(res, ct):
    a, b = res
    return _dot3(ct, b, ((1,), (1,))), _dot3(a, ct, ((0,), (0,)))


_imm.defvjp(lambda a, b: (_dot3(a, b, ((1,), (0,))), (a, b)), _imm_bwd)


def _hmm(a, b):
    return jnp.dot(a, b, precision=HI, preferred_element_type=F32)


def _hmm_nt(a, b):
    return lax.dot_general(a, b, (((1,), (1,)), ((), ())), precision=HI, preferred_element_type=F32)


def _hmm_tn(a, b):
    return lax.dot_general(a, b, (((0,), (0,)), ((), ())), precision=HI, preferred_element_type=F32)


def _rows(shape):
    return lax.broadcasted_iota(jnp.int32, shape, 0)


def _cols(shape):
    return lax.broadcasted_iota(jnp.int32, shape, 1)


def _sd(x, s):
    return jnp.where(_rows(x.shape) >= s, pltpu.roll(x, s, axis=0), 0.0)


def _su(x, s):
    n = x.shape[0]
    return jnp.where(_rows(x.shape) < n - s, pltpu.roll(x, n - s, axis=0), 0.0)


@functools.partial(jax.custom_vjp, nondiff_argnums=(1,))
def _shift_down(x, s):
    return _sd(x, s)


def _shift_down_fwd(x, s):
    return _sd(x, s), None


def _shift_down_bwd(s, _, g):
    return (_su(g, s),)


_shift_down.defvjp(_shift_down_fwd, _shift_down_bwd)


def _last_row(x):
    n = x.shape[0]
    return jnp.sum(jnp.where(_rows(x.shape) == n - 1, x, 0.0), axis=0, keepdims=True)


def _prep_fn(p, w0, w1, w2, w3, qk):
    acc = w3 * p + w2 * _shift_down(p, 1) + w1 * _shift_down(p, 2) + w0 * _shift_down(p, 3)
    a = _silu(acc)
    nrm = lax.rsqrt(jnp.sum(a * a, axis=-1, keepdims=True) + EPS)
    return a * (nrm * qk + (1.0 - qk))


def _gates_fn(bd, av, bv):
    tm = bd.shape[0]
    beta_all = _sigmoid(bd)
    g_all = -jnp.exp(av) * _softplus(bd + bv)
    r, c = _rows((tm, tm)), _cols((tm, tm))
    tri = jnp.where((r // CH == c // CH) & (r >= c), 1.0, 0.0).astype(F32)
    gc_all = _hmm(tri, g_all)
    lane = _cols(bd.shape)
    outs = []
    for h in range(NH):
        b = jnp.sum(jnp.where(lane == h, beta_all, 0.0), axis=1, keepdims=True)
        outs.append(jnp.broadcast_to(b, bd.shape))
    for h in range(NH):
        g = jnp.sum(jnp.where(lane == NH + h, gc_all, 0.0), axis=1, keepdims=True)
        outs.append(jnp.broadcast_to(g, bd.shape))
    return tuple(outs)


def _unit_lower_inv(l_mat):
    n = l_mat.shape[0]
    eye = jnp.where(_rows((n, n)) == _cols((n, n)), 1.0, 0.0).astype(F32)
    p = -l_mat
    r = eye + p
    k = 1
    while 2 * k < n:
        p = _imm(p, p)
        r = r + _imm(r, p)
        k *= 2
    return r


@jax.custom_vjp
def _known_inverse(l_mat, t_inv):
    return t_inv


def _known_inverse_bwd(t_inv, ct):
    d_l = -_dot3(_dot3(t_inv, ct, ((0,), (0,))), t_inv, ((1,), (1,)))
    return d_l, jnp.zeros_like(t_inv)


_known_inverse.defvjp(lambda l_mat, t_inv: (t_inv, t_inv), _known_inverse_bwd)


def _chunk_local(q, k, v, bb, gcb, t_inv=None):
    qs = q * (DH ** -0.5)
    kb = k * bb
    eg = jnp.exp(gcb)
    ii, jj = _rows((CH, CH)), _cols((CH, CH))
    decay = jnp.exp(jnp.where(ii >= jj, gcb - gcb.T, -1e30))
    l_mat = jnp.where(ii > jj, _mm_nt_d(kb, k) * decay, 0.0)
    t_inv = _unit_lower_inv(l_mat) if t_inv is None else _known_inverse(l_mat, t_inv)
    u = _mm(t_inv, v * bb)
    w = _mm(t_inv, kb * eg)
    a_qk = _mm_nt_d(qs, k) * decay
    k_dec = k * jnp.exp(_last_row(gcb) - gcb)
    return (u, w, qs * eg, k_dec, a_qk), t_inv


def _state_step(u, w, q_dec, k_dec, a_qk, gcb, state):
    v_new = u - _mm(w, state)
    o = _mm(q_dec, state) + _mm(a_qk, v_new)
    new_state = state * jnp.exp(_last_row(gcb)) + _mm_tn_d(k_dec, v_new)
    return o, new_state


SUB = 8


def _cmul(ar, ai, br, bi):
    return ar * br - ai * bi, ar * bi + ai * br


def _scan_tile(xr, xi, mr, mi, hr_ref, hi_ref, cr_ref, ci_ref, reverse):
    n, width = xr.shape
    ngroups = n // SUB
    shift_groups = _su if reverse else _sd
    xr, xi = xr.reshape(ngroups, SUB, width), xi.reshape(ngroups, SUB, width)
    pr, pi = mr, mi
    tr, ti = jnp.broadcast_to(mr, (SUB, width)), jnp.broadcast_to(mi, (SUB, width))
    pos = _rows(tr.shape)
    s = 1
    while s < SUB:
        inside = pos < SUB - s if reverse else pos >= s
        shift = SUB - s if reverse else s
        qr, qi = jnp.where(inside, pr, 0.0)[None], jnp.where(inside, pi, 0.0)[None]
        dr, di = _cmul(qr, qi, pltpu.roll(xr, shift, axis=1), pltpu.roll(xi, shift, axis=1))
        xr, xi = xr + dr, xi + di
        er = jnp.where(inside, pltpu.roll(tr, shift, axis=0), 1.0)
        ei = jnp.where(inside, pltpu.roll(ti, shift, axis=0), 0.0)
        tr, ti = _cmul(tr, ti, er, ei)
        pr, pi = _cmul(pr, pi, pr, pi)
        s *= 2
    xr, xi = xr.reshape(n, width), xi.reshape(n, width)
    nlb = width // DH

    def lanes(x, j):
        return x[:, j * DH:(j + 1) * DH]

    for j in range(nlb):
        hr_ref[j] = lanes(xr, j)
        hi_ref[j] = lanes(xi, j)
    edge = pl.ds(0 if reverse else SUB - 1, ngroups, stride=SUB)
    gr = jnp.concatenate([hr_ref.at[j][edge, :] for j in range(nlb)], axis=1)
    gi = jnp.concatenate([hi_ref.at[j][edge, :] for j in range(nlb)], axis=1)
    s = 1
    while s < ngroups:
        dr, di = _cmul(pr, pi, shift_groups(gr, s), shift_groups(gi, s))
        gr, gi = gr + dr, gi + di
        pr, pi = _cmul(pr, pi, pr, pi)
        s *= 2
    cr_ref[...] = shift_groups(gr, 1)
    ci_ref[...] = shift_groups(gi, 1)
    for g in range(ngroups):
        rows = slice(g * SUB, (g + 1) * SUB)
        dr, di = _cmul(tr, ti, cr_ref[g:g + 1, :], ci_ref[g:g + 1, :])
        for j in range(nlb):
            hr_ref[j, rows, :] += lanes(dr, j)
            hi_ref[j, rows, :] += lanes(di, j)
    return (jnp.concatenate([hr_ref[j] for j in range(nlb)], axis=1),
            jnp.concatenate([hi_ref[j] for j in range(nlb)], axis=1))


def _s5_states(u, lam_ref, b_ref, car_ref, hr_ref, hi_ref, cr_ref, ci_ref):
    lr, li = lam_ref[0], lam_ref[1]
    first = _rows((u.shape[0], SW)) == 0
    inr, ini = _cmul(lr, li, car_ref[0:1, :], car_ref[1:2, :])
    xr = _mm(u, b_ref[0]) + jnp.where(first, inr, 0.0)
    xi = _mm(u, b_ref[1]) + jnp.where(first, ini, 0.0)
    return _scan_tile(xr, xi, lr, li, hr_ref, hi_ref, cr_ref, ci_ref, False)


def _s5_params_fn(ar, ai, ldt, br2, bi2):
    dt = jnp.exp(ldt)
    mag = jnp.exp(ar * dt)
    lr, li = mag * jnp.cos(ai * dt), mag * jnp.sin(ai * dt)
    den = ar * ar + ai * ai
    fr = ((lr - 1.0) * ar + li * ai) / den
    fi = (li * ar - (lr - 1.0) * ai) / den
    expand = jnp.where(_cols((NS, NS * GS)) // GS == _rows((NS, NS * GS)), 1.0, 0.0).astype(F32)
    fr2, fi2 = _hmm(fr, expand), _hmm(fi, expand)
    return lr, li, fr2 * br2 - fi2 * bi2, fr2 * bi2 + fi2 * br2


def _head_norm(o, hn):
    parts = []
    for h in range(NH):
        oh = o[:, h * DH:(h + 1) * DH]
        parts.append(oh * lax.rsqrt(jnp.mean(oh * oh, axis=-1, keepdims=True) + EPS) * hn)
    return jnp.concatenate(parts, axis=1)


def _mix_pre(s5y, u, dvec):
    return _gelu(s5y + dvec * u)


def _mix_mid(o, za, y0, gl, zb, ra, rb, hn):
    ya = _head_norm(o, hn) * _silu(za)
    yb = y0 * _sigmoid(gl) * _silu(zb)
    return _sigmoid(ra) * ya + _sigmoid(rb) * yb


def _mix_post(x, out, npost):
    return x + _rms(out, npost)


def _tile(t, want):
    return min(t, want)


def _row_tile(rows, want):
    return max(r for r in range(16, want + 1, 16) if rows % r == 0)


def _inproj_fwd(x, gain, wcat, l):
    t = x.shape[0]
    tm, tn = _tile(t, 1024), 640

    def body(x_ref, g_ref, w_ref, o_ref, h_ref):
        @pl.when(pl.program_id(1) == 0)
        def _():
            h_ref[...] = _rms(x_ref[...], g_ref[...]).astype(h_ref.dtype)
        o_ref[...] = _dot_bf16(h_ref[...], w_ref[...], ((1,), (0,)))

    return pl.pallas_call(
        body, name="inproj_fwd", grid=(t // tm, NCOL // tn),
        in_specs=[pl.BlockSpec((tm, D), lambda i, j: (i, 0)), _full((1, D)),
                  pl.BlockSpec((None, D, tn), lambda i, j: (l, 0, j))],
        out_specs=[pl.BlockSpec((tm, tn), lambda i, j: (i, j)), pl.BlockSpec((tm, D), lambda i, j: (i, 0))],
        out_shape=[jax.ShapeDtypeStruct((t, NCOL), F32), jax.ShapeDtypeStruct((t, D), wcat.dtype)],
        compiler_params=_cparams(("parallel", "arbitrary")),
    )(x, gain, wcat)


def _inproj_bwd_dx(dproj, wcat, x, gain, dxres, l, exchange=None):
    t = x.shape[0]
    tm, tk = _tile(t, 1024), 640
    nk = NCOL // tk

    def body(dp_ref, w_ref, x_ref, g_ref, r_ref, dx_ref, dg_ref, acc_ref):
        i, k = pl.program_id(0), pl.program_id(1)

        @pl.when(k == 0)
        def _():
            acc_ref[...] = jnp.zeros_like(acc_ref)

        acc_ref[...] += _mm_nt(dp_ref[...], w_ref[...])

        @pl.when(k == nk - 1)
        def _():
            _, vjp = jax.vjp(_rms, x_ref[...], g_ref[...])
            dx, dg = vjp(acc_ref[...])
            dx_ref[...] = r_ref[...] + dx

            @pl.when(i == 0)
            def _():
                dg_ref[...] = dg

            @pl.when(i > 0)
            def _():
                dg_ref[...] += dg

    grid = (t // tm, nk)
    in_specs = [pl.BlockSpec((tm, tk), lambda i, k: (i, k)), pl.BlockSpec((None, D, tk), lambda i, k: (l, 0, k)),
                pl.BlockSpec((tm, D), lambda i, k: (i, 0)), _full((1, D)), pl.BlockSpec((tm, D), lambda i, k: (i, 0))]
    out_specs = [pl.BlockSpec((tm, D), lambda i, k: (i, 0)), _full((1, D))]
    out_shape = [jax.ShapeDtypeStruct((t, D), F32), jax.ShapeDtypeStruct((1, D), F32)]
    scratch, args = [pltpu.VMEM((tm, D), F32)], [dproj, wcat, x, gain, dxres]
    if exchange is not None:
        body = _carry(body, len(args), len(out_shape), len(scratch), exchange, grid)
        in_specs, out_specs = in_specs + exchange.in_specs, out_specs + exchange.out_specs
        out_shape, scratch, args = out_shape + exchange.out_shape, scratch + exchange.scratch_shapes, args + exchange.srcs
    outs = pl.pallas_call(
        body, name="inproj_bwd_dx", grid=grid, in_specs=in_specs, out_specs=out_specs, out_shape=out_shape,
        scratch_shapes=scratch, compiler_params=_cparams(("arbitrary", "arbitrary")),
    )(*args)
    return outs[0], outs[1], outs[2:]


def _inproj_bwd_dw(h, dproj):
    t = h.shape[0]
    tm, tn = _tile(t, 512), 1664

    def body(h_ref, dp_ref, o_ref):
        @pl.when(pl.program_id(1) == 0)
        def _():
            o_ref[...] = jnp.zeros_like(o_ref)

        o_ref[...] += _mm_tn(h_ref[...], dp_ref[...])

    return pl.pallas_call(
        body, name="inproj_bwd_dw", grid=(NCOL // tn, t // tm),
        in_specs=[pl.BlockSpec((tm, D), lambda j, i: (i, 0)), pl.BlockSpec((tm, tn), lambda j, i: (i, j))],
        out_specs=pl.BlockSpec((D, tn), lambda j, i: (0, j)),
        out_shape=jax.ShapeDtypeStruct((D, NCOL), F32),
        compiler_params=_cparams(("parallel", "arbitrary")),
    )(h, dproj)


def _prep_fwd(proj, cw):
    t = proj.shape[0]

    def body(p_ref, w_ref, o_ref):
        qk = (pl.program_id(0) < 2 * NH).astype(F32)
        o_ref[...] = _prep_fn(p_ref[...], w_ref[0:1, :], w_ref[1:2, :], w_ref[2:3, :], w_ref[3:4, :], qk)

    return pl.pallas_call(
        body, name="prep_fwd", grid=(3 * NH,),
        in_specs=[pl.BlockSpec((t, DH), lambda c: (0, c)), pl.BlockSpec((4, DH), lambda c: (0, c))],
        out_specs=pl.BlockSpec((None, t, DH), lambda c: (c, 0, 0)),
        out_shape=jax.ShapeDtypeStruct((3 * NH, t, DH), F32),
        compiler_params=_cparams(("parallel",)),
    )(proj, cw)


def _prep_bwd(proj, cw, dq, dk, dv):
    t = proj.shape[0]

    def body(p_ref, w_ref, dq_ref, dk_ref, dv_ref, dp_ref, dw_ref):
        c = pl.program_id(0)
        qk = (c < 2 * NH).astype(F32)
        _, vjp = jax.vjp(lambda p, w0, w1, w2, w3: _prep_fn(p, w0, w1, w2, w3, qk),
                         p_ref[...], w_ref[0:1, :], w_ref[1:2, :], w_ref[2:3, :], w_ref[3:4, :])
        d = jnp.where(c < NH, dq_ref[...], jnp.where(c < 2 * NH, dk_ref[...], dv_ref[...]))
        dp, dw0, dw1, dw2, dw3 = vjp(d)
        dp_ref[...] = dp.astype(dp_ref.dtype)
        dw_ref[0:1, :] = dw0
        dw_ref[1:2, :] = dw1
        dw_ref[2:3, :] = dw2
        dw_ref[3:4, :] = dw3

    return pl.pallas_call(
        body, name="prep_bwd", grid=(3 * NH,),
        in_specs=[pl.BlockSpec((t, DH), lambda c: (0, c)), pl.BlockSpec((4, DH), lambda c: (0, c))]
        + [pl.BlockSpec((None, t, DH), functools.partial(lambda c, off: (jnp.clip(c - off, 0, NH - 1), 0, 0), off=off))
           for off in (0, NH, 2 * NH)],
        out_specs=[pl.BlockSpec((t, DH), lambda c: (0, c)), pl.BlockSpec((4, DH), lambda c: (0, c))],
        out_shape=[jax.ShapeDtypeStruct((t, 3 * D), GRAD_ACT), jax.ShapeDtypeStruct((4, 3 * D), F32)],
        compiler_params=_cparams(("arbitrary",)),
    )(proj, cw, dq, dk, dv)


def _gates_fwd(proj, gvec):
    t = proj.shape[0]
    tm = _tile(t, 512)

    def body(p_ref, gv_ref, b_ref, g_ref):
        outs = _gates_fn(p_ref[...], gv_ref[0:1, :], gv_ref[1:2, :])
        for h in range(NH):
            b_ref[h] = outs[h]
            g_ref[h] = outs[NH + h]

    spec = pl.BlockSpec((NH, tm, DH), lambda i: (0, i, 0))
    return pl.pallas_call(
        body, name="gates_fwd", grid=(t // tm,),
        in_specs=[pl.BlockSpec((tm, DH), lambda i: (i, BD0 // DH)), _full((8, DH))],
        out_specs=[spec, spec],
        out_shape=[jax.ShapeDtypeStruct((NH, t, DH), F32)] * 2,
        compiler_params=_cparams(("parallel",)),
    )(proj, gvec)


def _gates_bwd(proj, gvec, dbb, dgcb):
    t = proj.shape[0]
    tm = _tile(t, 512)

    def body(p_ref, gv_ref, db_ref, dg_ref, dp_ref, dgv_ref):
        _, vjp = jax.vjp(_gates_fn, p_ref[...], gv_ref[0:1, :], gv_ref[1:2, :])
        cts = tuple(db_ref[h] for h in range(NH)) + tuple(dg_ref[h] for h in range(NH))
        dp, da, db = vjp(cts)
        dp_ref[...] = dp.astype(dp_ref.dtype)

        @pl.when(pl.program_id(0) == 0)
        def _():
            dgv_ref[...] = jnp.zeros_like(dgv_ref)

        dgv_ref[0:1, :] += da
        dgv_ref[1:2, :] += db

    spec = pl.BlockSpec((NH, tm, DH), lambda i: (0, i, 0))
    return pl.pallas_call(
        body, name="gates_bwd", grid=(t // tm,),
        in_specs=[pl.BlockSpec((tm, DH), lambda i: (i, BD0 // DH)), _full((8, DH)), spec, spec],
        out_specs=[pl.BlockSpec((tm, DH), lambda i: (i, 0)), _full((8, DH))],
        out_shape=[jax.ShapeDtypeStruct((t, DH), GRAD_ACT), jax.ShapeDtypeStruct((8, DH), F32)],
        compiler_params=_cparams(("arbitrary",)),
    )(proj, gvec, dbb, dgcb)


def _chunks_per_step(nch):
    return 2 if nch % 2 == 0 else 1


def _grid_ends(grid):
    def first():
        return functools.reduce(jnp.logical_and, [pl.program_id(a) == 0 for a in range(len(grid))])

    def last():
        return functools.reduce(jnp.logical_and, [pl.program_id(a) == n - 1 for a, n in enumerate(grid)])

    return first, last


def _carry(body, n_in, n_out, n_scratch, exchange, grid):
    first, last = _grid_ends(grid)
    na = exchange.narr

    def wrapped(*refs):
        a, b = n_in, n_in + na
        c, d = b + n_out, b + n_out + na
        e = d + n_scratch
        srcs, dsts, sems = refs[a:b], refs[c:d], refs[e:]

        @pl.when(first())
        def _():
            exchange.start(srcs, dsts, sems)

        body(*(refs[:a] + refs[b:c] + refs[d:e]))

        @pl.when(last())
        def _():
            exchange.wait(srcs, dsts, sems)

    return wrapped


def _delta_local_fwd(qkv, bb, gcb, exchange=None):
    t = qkv.shape[1]
    cps = _chunks_per_step(t // CH)
    rows = cps * CH
    grid = (NH, t // rows)

    def body(q_ref, k_ref, v_ref, b_ref, g_ref, *out_refs):
        for c in range(cps):
            sl = slice(c * CH, (c + 1) * CH)
            outs, t_inv = _chunk_local(q_ref[sl, :], k_ref[sl, :], v_ref[sl, :], b_ref[sl, :], g_ref[sl, :])
            for ref, val in zip(out_refs, outs + (t_inv,)):
                ref[sl, :] = val

    def blk(off):
        return pl.BlockSpec((None, rows, DH), lambda h, n: (h + off, n, 0))

    in_specs = [blk(0), blk(NH), blk(2 * NH), blk(0), blk(0)]
    out_specs = [blk(0)] * 6
    out_shape = [jax.ShapeDtypeStruct((NH, t, DH), F32)] * 6
    args, scratch, sem = [qkv, qkv, qkv, bb, gcb], [], ("parallel", "parallel")
    if exchange is not None:
        body = _carry(body, 5, 6, 0, exchange, grid)
        in_specs, out_specs = in_specs + exchange.in_specs, out_specs + exchange.out_specs
        out_shape, scratch, args = out_shape + exchange.out_shape, exchange.scratch_shapes, args + exchange.srcs
        sem = ("arbitrary", "arbitrary")
    outs = pl.pallas_call(
        body, name="delta_local_fwd", grid=grid, in_specs=in_specs, out_specs=out_specs, out_shape=out_shape,
        scratch_shapes=scratch, compiler_params=_cparams(sem),
    )(*args)
    return outs[:5], outs[5], outs[6:]


def _delta_local_bwd(qkv, bb, gcb, t_inv, cts, dgcb_state):
    t = qkv.shape[1]
    cps = _chunks_per_step(t // CH)
    rows = cps * CH

    def body(q_ref, k_ref, v_ref, b_ref, g_ref, ti_ref, du_ref, dw_ref, dqd_ref, dkd_ref, da_ref, dgs_ref,
             dq_ref, dk_ref, dv_ref, db_ref, dg_ref):
        for c in range(cps):
            sl = slice(c * CH, (c + 1) * CH)
            t_inv_c = ti_ref[sl, :]
            _, vjp = jax.vjp(lambda *a: _chunk_local(*a, t_inv=t_inv_c)[0],
                             q_ref[sl, :], k_ref[sl, :], v_ref[sl, :], b_ref[sl, :], g_ref[sl, :])
            dq, dk, dv, db, dg = vjp((du_ref[sl, :], dw_ref[sl, :], dqd_ref[sl, :], dkd_ref[sl, :], da_ref[sl, :]))
            dq_ref[sl, :] = dq
            dk_ref[sl, :] = dk
            dv_ref[sl, :] = dv
            db_ref[sl, :] = db
            dg_ref[sl, :] = dg + dgs_ref[sl, :]

    def blk(off):
        return pl.BlockSpec((None, rows, DH), lambda h, n: (h + off, n, 0))

    return pl.pallas_call(
        body, name="delta_local_bwd", grid=(NH, t // rows),
        in_specs=[blk(0), blk(NH), blk(2 * NH)] + [blk(0)] * 9,
        out_specs=[blk(0)] * 5,
        out_shape=[jax.ShapeDtypeStruct((NH, t, DH), F32)] * 5,
        compiler_params=_cparams(("parallel", "parallel")),
    )(qkv, qkv, qkv, bb, gcb, t_inv, *cts, dgcb_state)


def _delta_state_fwd(local, gcb):
    t = gcb.shape[1]
    nch = t // CH

    def body(u_ref, w_ref, qd_ref, kd_ref, a_ref, g_ref, o_ref, s_ref, st_ref):
        @pl.when(pl.program_id(0) == 0)
        def _():
            st_ref[...] = jnp.zeros_like(st_ref)

        for h in range(NH):
            s_ref[h] = st_ref[h]
            o, ns = _state_step(u_ref[h], w_ref[h], qd_ref[h], kd_ref[h], a_ref[h], g_ref[h], st_ref[h])
            o_ref[:, h * DH:(h + 1) * DH] = o
            st_ref[h] = ns

    blk = pl.BlockSpec((NH, CH, DH), lambda n: (0, n, 0))
    return pl.pallas_call(
        body, name="delta_state_fwd", grid=(nch,),
        in_specs=[blk] * 6,
        out_specs=[pl.BlockSpec((CH, D), lambda n: (n, 0)),
                   pl.BlockSpec((NH, None, DH, DH), lambda n: (0, n, 0, 0))],
        out_shape=[jax.ShapeDtypeStruct((t, D), F32), jax.ShapeDtypeStruct((NH, nch, DH, DH), F32)],
        scratch_shapes=[pltpu.VMEM((NH, DH, DH), F32)],
        compiler_params=_cparams(("arbitrary",)),
    )(*local, gcb)


def _delta_state_bwd(local, gcb, states, do):
    t = gcb.shape[1]
    nch = t // CH

    def body(u_ref, w_ref, qd_ref, kd_ref, a_ref, g_ref, s_ref, do_ref,
             du_ref, dw_ref, dqd_ref, dkd_ref, da_ref, dg_ref, ds_ref):
        @pl.when(pl.program_id(0) == 0)
        def _():
            ds_ref[...] = jnp.zeros_like(ds_ref)

        for h in range(NH):
            _, vjp = jax.vjp(_state_step, u_ref[h], w_ref[h], qd_ref[h], kd_ref[h], a_ref[h], g_ref[h], s_ref[h])
            du, dw, dqd, dkd, da, dg, ds = vjp((do_ref[:, h * DH:(h + 1) * DH], ds_ref[h]))
            du_ref[h] = du
            dw_ref[h] = dw
            dqd_ref[h] = dqd
            dkd_ref[h] = dkd
            da_ref[h] = da
            dg_ref[h] = dg
            ds_ref[h] = ds

    blk = pl.BlockSpec((NH, CH, DH), lambda n: (0, nch - 1 - n, 0))
    return pl.pallas_call(
        body, name="delta_state_bwd", grid=(nch,),
        in_specs=[blk] * 6 + [pl.BlockSpec((NH, None, DH, DH), lambda n: (0, nch - 1 - n, 0, 0)),
                              pl.BlockSpec((CH, D), lambda n: (nch - 1 - n, 0))],
        out_specs=[blk] * 6,
        out_shape=[jax.ShapeDtypeStruct((NH, t, DH), F32)] * 6,
        scratch_shapes=[pltpu.VMEM((NH, DH, DH), F32)],
        compiler_params=_cparams(("arbitrary",)),
    )(*local, gcb, states, do)


def _s5_params(ar, ai, ldt, br2, bi2):
    def body(ar_ref, ai_ref, ld_ref, br_ref, bi_ref, lr_ref, li_ref, bbr_ref, bbi_ref):
        lr, li, bbr, bbi = _s5_params_fn(ar_ref[...], ai_ref[...], ld_ref[...], br_ref[...], bi_ref[...])
        lr_ref[...] = lr
        li_ref[...] = li
        bbr_ref[...] = bbr
        bbi_ref[...] = bbi

    sq = pl.BlockSpec((None, NG, NS), lambda l: (l, 0, 0))
    wide = pl.BlockSpec((None, NG, NS * GS), lambda l: (l, 0, 0))
    return pl.pallas_call(
        body, name="s5_params", grid=(DEPTH,),
        in_specs=[sq, sq, pl.BlockSpec((None, NG, 1), lambda l: (l, 0, 0)), wide, wide],
        out_specs=[sq, sq, wide, wide],
        out_shape=[jax.ShapeDtypeStruct((DEPTH, NG, NS), F32)] * 2
        + [jax.ShapeDtypeStruct((DEPTH, NG, NS * GS), F32)] * 2,
        compiler_params=_cparams(("parallel",)),
    )(ar, ai, ldt, br2, bi2)


def _s5_params_bwd(ar, ai, ldt, br2, bi2, dlr, dli, dbbr, dbbi):
    def body(ar_ref, ai_ref, ld_ref, br_ref, bi_ref, a_ref, b_ref, c_ref, d_ref,
             dar_ref, dai_ref, dld_ref, dbr_ref, dbi_ref):
        _, vjp = jax.vjp(_s5_params_fn, ar_ref[...], ai_ref[...], ld_ref[...], br_ref[...], bi_ref[...])
        dar, dai, dld, dbr, dbi = vjp((a_ref[...], b_ref[...], c_ref[...], d_ref[...]))
        dar_ref[...] = dar
        dai_ref[...] = dai
        dld_ref[...] = dld
        dbr_ref[...] = dbr
        dbi_ref[...] = dbi

    sq = pl.BlockSpec((None, NG, NS), lambda l: (l, 0, 0))
    col = pl.BlockSpec((None, NG, 1), lambda l: (l, 0, 0))
    wide = pl.BlockSpec((None, NG, NS * GS), lambda l: (l, 0, 0))
    return pl.pallas_call(
        body, name="s5_params_bwd", grid=(DEPTH,),
        in_specs=[sq, sq, col, wide, wide, sq, sq, wide, wide],
        out_specs=[sq, sq, col, wide, wide],
        out_shape=[jax.ShapeDtypeStruct((DEPTH, NG, NS), F32)] * 2 + [jax.ShapeDtypeStruct((DEPTH, NG, 1), F32)]
        + [jax.ShapeDtypeStruct((DEPTH, NG, NS * GS), F32)] * 2,
        compiler_params=_cparams(("parallel",)),
    )(ar, ai, ldt, br2, bi2, dlr, dli, dbbr, dbbi)


def _s5_tile_rows(t):
    return _tile(t // 2, 512)


def _s5_fwd(proj, lam, bblk, cblk):
    t = proj.shape[0]
    r = _s5_tile_rows(t)
    nt = t // r
    u0 = 4 * D // DH

    def body(u_ref, lam_ref, b_ref, c_ref, y_ref, car_ref, st_ref, hr_ref, hi_ref, cr_ref, ci_ref):
        @pl.when(pl.program_id(1) == 0)
        def _():
            st_ref[...] = jnp.zeros_like(st_ref)

        car_ref[...] = st_ref[...]
        hr, hi = _s5_states(u_ref[...], lam_ref, b_ref, st_ref, hr_ref, hi_ref, cr_ref, ci_ref)
        y_ref[...] = _mm(hr, c_ref[0]) - _mm(hi, c_ref[1])
        st_ref[0:1, :] = _last_row(hr)
        st_ref[1:2, :] = _last_row(hi)

    scratch = ([pltpu.VMEM((8, SW), F32)] + [pltpu.VMEM((SW // DH, r, DH), F32)] * 2
               + [pltpu.VMEM((r // SUB, SW), F32)] * 2)
    return pl.pallas_call(
        body, name="s5_fwd", grid=(NCB, nt),
        in_specs=[pl.BlockSpec((r, DH), lambda c, i: (i, u0 + c)),
                  pl.BlockSpec((2, 1, SW), lambda c, i: (0, 0, c)),
                  pl.BlockSpec((2, None, DH, SW), lambda c, i: (0, c, 0, 0)),
                  pl.BlockSpec((2, None, SW, DH), lambda c, i: (0, c, 0, 0))],
        out_specs=[pl.BlockSpec((r, DH), lambda c, i: (i, c)),
                   pl.BlockSpec((None, 8, SW), lambda c, i: (i, 0, c))],
        out_shape=[jax.ShapeDtypeStruct((t, D), F32), jax.ShapeDtypeStruct((nt, 8, NG * NS), F32)],
        scratch_shapes=scratch,
        compiler_params=_cparams(("parallel", "arbitrary")),
    )(proj, lam, bblk, cblk)


def _s5_bwd(proj, lam, bblk, cblk, carries, dy, du_skip, exchange=None):
    t = proj.shape[0]
    r = _s5_tile_rows(t)
    nt = t // r
    u0 = 4 * D // DH

    def body(u_ref, lam_ref, b_ref, c_ref, car_ref, dy_ref, dus_ref, du_ref, dlam_ref, db_ref, dc_ref, dst_ref,
             hr_ref, hi_ref, ar_ref, ai_ref, cr_ref, ci_ref):
        first = pl.program_id(1) == 0

        @pl.when(first)
        def _():
            dst_ref[...] = jnp.zeros_like(dst_ref)

        u, dy = u_ref[...], dy_ref[...]
        lr, li = lam_ref[0], lam_ref[1]
        hr, hi = _s5_states(u, lam_ref, b_ref, car_ref, hr_ref, hi_ref, cr_ref, ci_ref)
        dcr2, dci2 = _mm_tn(hr, dy), -_mm_tn(hi, dy)
        last = _rows((r, SW)) == r - 1
        inr, ini = _cmul(lr, -li, dst_ref[0:1, :], dst_ref[1:2, :])
        dhr = _mm_nt(dy, c_ref[0]) + jnp.where(last, inr, 0.0)
        dhi = jnp.where(last, ini, 0.0) - _mm_nt(dy, c_ref[1])
        ar, ai = _scan_tile(dhr, dhi, lr, -li, ar_ref, ai_ref, cr_ref, ci_ref, True)
        top = _rows((r, SW)) == 0
        dst_ref[0:1, :] = jnp.sum(jnp.where(top, ar, 0.0), axis=0, keepdims=True)
        dst_ref[1:2, :] = jnp.sum(jnp.where(top, ai, 0.0), axis=0, keepdims=True)
        du_ref[...] = (_mm_nt(ar, b_ref[0]) + _mm_nt(ai, b_ref[1]) + dus_ref[...]).astype(du_ref.dtype)
        dbr, dbi = _mm_tn(u, ar), _mm_tn(u, ai)
        pr = _sd(hr, 1) + jnp.where(top, car_ref[0:1, :], 0.0)
        pi = _sd(hi, 1) + jnp.where(top, car_ref[1:2, :], 0.0)
        dlr = jnp.sum(ar * pr + ai * pi, axis=0, keepdims=True)
        dli = jnp.sum(ai * pr - ar * pi, axis=0, keepdims=True)

        @pl.when(first)
        def _():
            dlam_ref[0] = dlr
            dlam_ref[1] = dli
            db_ref[0] = dbr
            db_ref[1] = dbi
            dc_ref[0] = dcr2
            dc_ref[1] = dci2

        @pl.when(jnp.logical_not(first))
        def _():
            dlam_ref[0] += dlr
            dlam_ref[1] += dli
            db_ref[0] += dbr
            db_ref[1] += dbi
            dc_ref[0] += dcr2
            dc_ref[1] += dci2

    grid = (NCB, nt)
    in_specs = [pl.BlockSpec((r, DH), lambda c, i: (nt - 1 - i, u0 + c)),
                pl.BlockSpec((2, 1, SW), lambda c, i: (0, 0, c)),
                pl.BlockSpec((2, None, DH, SW), lambda c, i: (0, c, 0, 0)),
                pl.BlockSpec((2, None, SW, DH), lambda c, i: (0, c, 0, 0)),
                pl.BlockSpec((None, 8, SW), lambda c, i: (nt - 1 - i, 0, c)),
                pl.BlockSpec((r, DH), lambda c, i: (nt - 1 - i, c)),
                pl.BlockSpec((r, DH), lambda c, i: (nt - 1 - i, c))]
    out_specs = [pl.BlockSpec((r, DH), lambda c, i: (nt - 1 - i, c)),
                 pl.BlockSpec((2, 1, SW), lambda c, i: (0, 0, c)),
                 pl.BlockSpec((2, None, DH, SW), lambda c, i: (0, c, 0, 0)),
                 pl.BlockSpec((2, None, SW, DH), lambda c, i: (0, c, 0, 0))]
    out_shape = [jax.ShapeDtypeStruct((t, D), GRAD_ACT), jax.ShapeDtypeStruct((2, 1, NG * NS), F32),
                 jax.ShapeDtypeStruct((2, NCB, DH, SW), F32), jax.ShapeDtypeStruct((2, NCB, SW, DH), F32)]
    scratch = ([pltpu.VMEM((8, SW), F32)] + [pltpu.VMEM((SW // DH, r, DH), F32)] * 4
               + [pltpu.VMEM((r // SUB, SW), F32)] * 2)
    args, sem = [proj, lam, bblk, cblk, carries, dy, du_skip], ("parallel", "arbitrary")
    if exchange is not None:
        body = _carry(body, len(args), len(out_shape), len(scratch), exchange, grid)
        in_specs, out_specs = in_specs + exchange.in_specs, out_specs + exchange.out_specs
        out_shape, scratch, args = out_shape + exchange.out_shape, scratch + exchange.scratch_shapes, args + exchange.srcs
        sem = ("arbitrary", "arbitrary")
    outs = pl.pallas_call(
        body, name="s5_bwd", grid=grid, in_specs=in_specs, out_specs=out_specs, out_shape=out_shape,
        scratch_shapes=scratch, compiler_params=_cparams(sem),
    )(*args)
    return outs[:4], outs[4:]


def _proj_spec(tm, col):
    return pl.BlockSpec((tm, D), lambda i: (i, col))


def _layer_mat(l):
    return pl.BlockSpec((None, D, D), lambda i: (l, 0, 0))


def _mix_fwd(proj, o, s5y, x, hn, dvec, wglu, bglu, wout, npost, l):
    t = x.shape[0]
    tm = _tile(t, 256)

    def body(za_ref, u_ref, zb_ref, ra_ref, rb_ref, o_ref, y_ref, x_ref, hn_ref, d_ref, wg_ref, bg_ref, wo_ref,
             np_ref, xn_ref):
        y0 = _mix_pre(y_ref[...], u_ref[...], d_ref[...])
        gl = _mm(y0, wg_ref[...]) + bg_ref[...]
        m = _mix_mid(o_ref[...], za_ref[...], y0, gl, zb_ref[...], ra_ref[...], rb_ref[...], hn_ref[...])
        out = _mm(m, wo_ref[...])
        xn_ref[...] = _mix_post(x_ref[...], out, np_ref[...])

    act = pl.BlockSpec((tm, D), lambda i: (i, 0))
    return pl.pallas_call(
        body, name="mix_fwd", grid=(t // tm,),
        in_specs=[_proj_spec(tm, 3), _proj_spec(tm, 4), _proj_spec(tm, 5), _proj_spec(tm, 6), _proj_spec(tm, 7),
                  act, act, act, _full((1, DH)), _full((1, D)), _layer_mat(l), _full((1, D)), _layer_mat(l),
                  _full((1, D))],
        out_specs=act,
        out_shape=jax.ShapeDtypeStruct((t, D), F32),
        compiler_params=_cparams(("parallel",)),
    )(proj, proj, proj, proj, proj, o, s5y, x, hn, dvec, wglu, bglu, wout, npost)


def _mix_bwd(proj, o, s5y, x, hn, dvec, wglu, bglu, wout, npost, dxn, l):
    t = x.shape[0]
    tm = _tile(t, 128)

    def body(za_ref, u_ref, zb_ref, ra_ref, rb_ref, o_ref, y_ref, x_ref, hn_ref, d_ref, wg_ref, bg_ref, wo_ref,
             np_ref, dxn_ref,
             dza_ref, du_ref, dzb_ref, dra_ref, drb_ref, do_ref, dy_ref, dx_ref,
             dwg_ref, dwo_ref, dvecs_ref, dhn_ref):
        y0, vjp_pre = jax.vjp(_mix_pre, y_ref[...], u_ref[...], d_ref[...])
        gl = _mm(y0, wg_ref[...]) + bg_ref[...]
        m, vjp_mid = jax.vjp(_mix_mid, o_ref[...], za_ref[...], y0, gl, zb_ref[...], ra_ref[...], rb_ref[...],
                             hn_ref[...])
        out = _mm(m, wo_ref[...])
        _, vjp_post = jax.vjp(_mix_post, x_ref[...], out, np_ref[...])
        dx, dout, dnp = vjp_post(dxn_ref[...])
        dm = _mm_nt(dout, wo_ref[...])
        dwo = _mm_tn(m, dout)
        do, dza, dy0, dgl, dzb, dra, drb, dhn = vjp_mid(dm)
        dwg = _mm_tn(y0, dgl)
        dbg = jnp.sum(dgl, axis=0, keepdims=True)
        dy0 = dy0 + _mm_nt(dgl, wg_ref[...])
        dy, du, dd = vjp_pre(dy0)
        dza_ref[...] = dza.astype(dza_ref.dtype)
        du_ref[...] = du
        dzb_ref[...] = dzb.astype(dzb_ref.dtype)
        dra_ref[...] = dra.astype(dra_ref.dtype)
        drb_ref[...] = drb.astype(drb_ref.dtype)
        do_ref[...] = do
        dy_ref[...] = dy
        dx_ref[...] = dx
        first = pl.program_id(0) == 0

        @pl.when(first)
        def _():
            dwg_ref[...] = dwg
            dwo_ref[...] = dwo
            dvecs_ref[...] = jnp.zeros_like(dvecs_ref)
            dhn_ref[...] = jnp.zeros_like(dhn_ref)

        @pl.when(jnp.logical_not(first))
        def _():
            dwg_ref[...] += dwg
            dwo_ref[...] += dwo

        dvecs_ref[0:1, :] += dd
        dvecs_ref[1:2, :] += dbg
        dvecs_ref[2:3, :] += dnp
        dhn_ref[0:1, :] += dhn

    act = pl.BlockSpec((tm, D), lambda i: (i, 0))
    a, ga = jax.ShapeDtypeStruct((t, D), F32), jax.ShapeDtypeStruct((t, D), GRAD_ACT)
    w = jax.ShapeDtypeStruct((D, D), F32)
    return pl.pallas_call(
        body, name="mix_bwd", grid=(t // tm,),
        in_specs=[_proj_spec(tm, 3), _proj_spec(tm, 4), _proj_spec(tm, 5), _proj_spec(tm, 6), _proj_spec(tm, 7),
                  act, act, act, _full((1, DH)), _full((1, D)), _layer_mat(l), _full((1, D)), _layer_mat(l),
                  _full((1, D)), act],
        out_specs=[act] * 8 + [_full((D, D)), _full((D, D)), _full((8, D)), _full((8, DH))],
        out_shape=[ga, a, ga, ga, ga, a, a, a, w, w, jax.ShapeDtypeStruct((8, D), F32),
                   jax.ShapeDtypeStruct((8, DH), F32)],
        compiler_params=_cparams(("arbitrary",)),
    )(proj, proj, proj, proj, proj, o, s5y, x, hn, dvec, wglu, bglu, wout, npost, dxn)


def _loss_grad(y, target):
    t = y.shape[0]
    tm = _tile(t, 512)

    def body(y_ref, t_ref, dy_ref, l_ref):
        err = y_ref[...] - t_ref[...]
        dy_ref[...] = err * (1.0 / D)
        part = jnp.sum(jnp.sum(err * err, axis=1, keepdims=True), axis=0, keepdims=True) * (0.5 / D)
        part = jnp.broadcast_to(part, (8, DH))

        @pl.when(pl.program_id(0) == 0)
        def _():
            l_ref[...] = part

        @pl.when(pl.program_id(0) > 0)
        def _():
            l_ref[...] += part

    act = pl.BlockSpec((tm, D), lambda i: (i, 0))
    return pl.pallas_call(
        body, name="loss_grad", grid=(t // tm,),
        in_specs=[act, act], out_specs=[act, _full((8, DH))],
        out_shape=[jax.ShapeDtypeStruct((t, D), F32), jax.ShapeDtypeStruct((8, DH), F32)],
        compiler_params=_cparams(("arbitrary",)),
    )(y, target)


def _flips(rel):
    x, y, c = lax.axis_index("x"), lax.axis_index("y"), lax.axis_index("c")
    fx, fy, fc = rel
    return (x ^ fx if fx else x, y ^ fy if fy else y, c ^ fc if fc else c)


CHIP_RELS = ((1, 0, 0), (0, 1, 0), (1, 1, 0))
ALL_RELS = tuple((fx, fy, fc) for fx in (0, 1) for fy in (0, 1) for fc in (0, 1) if (fx, fy, fc) != (0, 0, 0))


def _slot_of(pos, by_chip):
    px, py, pc = pos
    return 2 * px + py if by_chip else 4 * px + 2 * py + pc


class _Exchange:
    def __init__(self, srcs, rels, by_chip, scatter):
        self.srcs, self.rels, self.by_chip, self.scatter = list(srcs), rels, by_chip, scatter
        self.narr = len(self.srcs)
        nslot, nsem = NCHIP if by_chip else NDEV, self.narr * len(rels)
        self.in_specs = [pl.BlockSpec(memory_space=pl.ANY)] * self.narr
        self.out_specs = [pl.BlockSpec(memory_space=pl.ANY)] * self.narr
        self.out_shape = [jax.ShapeDtypeStruct((nslot,) + s.shape[-2:], s.dtype) for s in self.srcs]
        self.scratch_shapes = [pltpu.SemaphoreType.DMA((nsem,)), pltpu.SemaphoreType.DMA((nsem,)),
                               pltpu.SemaphoreType.DMA((self.narr,))]

    def _copies(self, src_refs, dst_refs, sems):
        send_sems, recv_sems, local_sems = sems
        my_slot = _slot_of(_flips((0, 0, 0)), self.by_chip)
        local, sends, arrivals = [], [], []
        for a, (src_ref, dst_ref) in enumerate(zip(src_refs, dst_refs)):
            local.append(pltpu.make_async_copy(src_ref.at[my_slot] if self.scatter else src_ref, dst_ref.at[my_slot],
                                               local_sems.at[a]))
            for k, rel in enumerate(self.rels):
                peer = _flips(rel)
                pair = dict(send_sem=send_sems.at[a * len(self.rels) + k], recv_sem=recv_sems.at[a * len(self.rels) + k],
                            device_id=peer, device_id_type=pl.DeviceIdType.MESH)
                part = src_ref.at[_slot_of(peer, self.by_chip)] if self.scatter else src_ref
                sends.append(pltpu.make_async_remote_copy(src_ref=part, dst_ref=dst_ref.at[my_slot], **pair))
                arrivals.append(pltpu.make_async_remote_copy(
                    src_ref=src_ref.at[0] if self.scatter else src_ref, dst_ref=dst_ref.at[_slot_of(peer, self.by_chip)],
                    **pair))
        return local, sends, arrivals

    def start(self, src_refs, dst_refs, sems):
        local, sends, _ = self._copies(src_refs, dst_refs, sems)
        for cp in local + sends:
            cp.start()

    def wait(self, src_refs, dst_refs, sems):
        local, sends, arrivals = self._copies(src_refs, dst_refs, sems)
        for cp in arrivals:
            cp.wait_recv()
        for cp in sends:
            cp.wait_send()
        for cp in local:
            cp.wait()


def _exchange(srcs, rels, by_chip, scatter, name):
    ex = _Exchange(srcs, rels, by_chip, scatter)

    def body(*refs):
        parts = refs[:ex.narr], refs[ex.narr:2 * ex.narr], refs[2 * ex.narr:]
        ex.start(*parts)
        ex.wait(*parts)

    return pl.pallas_call(body, name=name, in_specs=ex.in_specs, out_specs=ex.out_specs, out_shape=ex.out_shape,
                          scratch_shapes=ex.scratch_shapes)(*ex.srcs)


def _sibling_swap(srcs, name):
    narr = len(srcs)

    def body(*refs):
        src_refs, dst_refs = refs[:narr], refs[narr:2 * narr]
        send_sems, recv_sems = refs[2 * narr:]
        peer = _flips((0, 0, 1))
        copies = [pltpu.make_async_remote_copy(src_ref=s, dst_ref=d, send_sem=send_sems.at[a], recv_sem=recv_sems.at[a],
                                               device_id=peer, device_id_type=pl.DeviceIdType.MESH)
                  for a, (s, d) in enumerate(zip(src_refs, dst_refs))]
        for cp in copies:
            cp.start()
        for cp in copies:
            cp.wait()

    return pl.pallas_call(
        body, name=name,
        in_specs=[pl.BlockSpec(memory_space=pl.ANY)] * narr,
        out_specs=[pl.BlockSpec(memory_space=pl.ANY)] * narr,
        out_shape=[jax.ShapeDtypeStruct(s.shape, s.dtype) for s in srcs],
        scratch_shapes=[pltpu.SemaphoreType.DMA((narr,)), pltpu.SemaphoreType.DMA((narr,))],
    )(*srcs)


def _all_reduce(src, name):
    rows, cols = src.shape
    r = rows // NDEV
    nrel = len(ALL_RELS)

    def body(src_ref, out_ref, parts_ref, mine_ref, send_sems, recv_sems):
        my_slot = _slot_of(_flips((0, 0, 0)), False)

        def piece(ref, slot):
            return ref.at[pl.ds(pl.multiple_of(slot * r, 8), r), :]

        def copies(phase):
            out = []
            for k, rel in enumerate(ALL_RELS):
                peer = _flips(rel)
                pair = dict(send_sem=send_sems.at[phase * nrel + k], recv_sem=recv_sems.at[phase * nrel + k],
                            device_id=peer, device_id_type=pl.DeviceIdType.MESH)
                if phase == 0:
                    out.append(pltpu.make_async_remote_copy(src_ref=piece(src_ref, _slot_of(peer, False)),
                                                            dst_ref=parts_ref.at[my_slot], **pair))
                else:
                    out.append(pltpu.make_async_remote_copy(src_ref=mine_ref, dst_ref=piece(out_ref, my_slot), **pair))
            return out

        first = copies(0)
        for cp in first:
            cp.start()
        parts_ref[my_slot] = piece(src_ref, my_slot)[...]
        for cp in first:
            cp.wait_recv()
        acc = parts_ref[0]
        for s in range(1, NDEV):
            acc = acc + parts_ref[s]
        mine_ref[...] = acc
        second = copies(1)
        for cp in second:
            cp.start()
        piece(out_ref, my_slot)[...] = acc
        for cp in second:
            cp.wait_recv()
        for cp in first + second:
            cp.wait_send()

    return pl.pallas_call(
        body, name=name,
        in_specs=[pl.BlockSpec(memory_space=pltpu.VMEM)], out_specs=pl.BlockSpec(memory_space=pltpu.VMEM),
        out_shape=jax.ShapeDtypeStruct(src.shape, src.dtype),
        scratch_shapes=[pltpu.VMEM((NDEV, r, cols), src.dtype), pltpu.VMEM((r, cols), src.dtype),
                        pltpu.SemaphoreType.DMA((2 * nrel,)), pltpu.SemaphoreType.DMA((2 * nrel,))],
        compiler_params=pltpu.CompilerParams(vmem_limit_bytes=VMEM_LIMIT),
    )(src)


def _sum_slots(parts, name):
    ns, rows, cols = parts.shape
    tr = _row_tile(rows, 256)

    def body(p_ref, o_ref):
        acc = p_ref[0].astype(F32)
        for s in range(1, ns):
            acc = acc + p_ref[s].astype(F32)
        o_ref[...] = acc

    return pl.pallas_call(
        body, name=name, grid=(rows // tr,),
        in_specs=[pl.BlockSpec((ns, tr, cols), lambda i: (0, i, 0))],
        out_specs=pl.BlockSpec((tr, cols), lambda i: (i, 0)),
        out_shape=jax.ShapeDtypeStruct((rows, cols), F32),
        compiler_params=_cparams(("parallel",)),
    )(parts)


def _adamw(w, g_parts, m, v, name, max_rows=256):
    rows, cols = w.shape
    tr = _row_tile(rows, max_rows)
    c1 = 1.0 / (1.0 - ADAM_B1 ** ADAM_STEP)
    c2 = 1.0 / (1.0 - ADAM_B2 ** ADAM_STEP)
    npart = len(g_parts)

    def body(*refs):
        w_ref, m_ref, v_ref = refs[:3]
        g_refs = refs[3:3 + npart]
        go_ref, d_ref, nm_ref, nv_ref = refs[3 + npart:]
        terms = []
        for g_ref in g_refs:
            terms += [g_ref[...]] if len(g_ref.shape) == 2 else [g_ref[s] for s in range(g_ref.shape[0])]
        g = terms[0]
        for term in terms[1:]:
            g = g + term
        nm = ADAM_B1 * m_ref[...] + (1.0 - ADAM_B1) * g
        nv = ADAM_B2 * v_ref[...] + (1.0 - ADAM_B2) * (g * g)
        d_ref[...] = -ADAM_LR * ((nm * c1) / (jnp.sqrt(nv * c2) + ADAM_EPS) + ADAM_WD * w_ref[...])
        go_ref[...] = g
        nm_ref[...] = nm
        nv_ref[...] = nv

    blk = pl.BlockSpec((tr, cols), lambda i: (i, 0))
    g_specs = [blk if p.ndim == 2 else pl.BlockSpec((p.shape[0], tr, cols), lambda i: (0, i, 0)) for p in g_parts]
    out = jax.ShapeDtypeStruct((rows, cols), F32)
    return pl.pallas_call(
        body, name=name, grid=(rows // tr,),
        in_specs=[blk, blk, blk] + g_specs,
        out_specs=[blk] * 4, out_shape=[out] * 4,
        compiler_params=_cparams(("parallel",)),
    )(w, m, v, *g_parts)


WIN_SHARD = 2052
CONV_SHARD = 768
ROW_SHARD = 256

SMALL = (("norm_pre", (DEPTH, D)), ("a_log", (DEPTH, NH)), ("dt_bias", (DEPTH, NH)), ("head_norm", (DEPTH, DH)),
         ("ssm_a_re", (DEPTH, NG, NS)), ("ssm_a_im", (DEPTH, NG, NS)), ("ssm_log_dt", (DEPTH, NG)),
         ("ssm_b_re", (DEPTH, NG, NS, GS)), ("ssm_b_im", (DEPTH, NG, NS, GS)),
         ("ssm_c_re", (DEPTH, NG, GS, NS)), ("ssm_c_im", (DEPTH, NG, GS, NS)), ("ssm_d", (DEPTH, D)),
         ("b_glu", (DEPTH, D)), ("norm_post", (DEPTH, D)))


def _pad_rows(flat, rows):
    return jnp.pad(flat, (0, rows * D - flat.shape[0])).reshape(rows, D)


def _rows_by_chip(a):
    nl, rows, cols = a.shape
    return a.reshape(nl, NCHIP, rows // NCHIP, cols).transpose(1, 0, 2, 3).reshape(NCHIP, -1, cols)


def _rows_from_chips(a):
    _, rows, cols = a.shape
    return a.reshape(NCHIP, DEPTH, rows // DEPTH, cols).transpose(1, 0, 2, 3).reshape(DEPTH, -1, cols)


def _cols_by_chip(a):
    nl, rows, cols = a.shape
    return a.reshape(nl, rows, NCHIP, cols // NCHIP).transpose(2, 0, 1, 3).reshape(NCHIP, nl * rows, -1)


def _cols_from_chips(a, nl):
    _, rows, cols = a.shape
    return a.reshape(NCHIP, nl, rows // nl, cols).transpose(1, 2, 0, 3).reshape(nl, rows // nl, NCHIP * cols)


SMALL_ROWS = sum(-(-math.prod(s) // (8 * D)) * 8 for _, s in SMALL)
CONV_ROWS = DEPTH * 4 * 3 * D // D


def _pack_small(vals, extra=()):
    parts = []
    for val in tuple(vals) + tuple(extra):
        n = val.size
        parts.append(_pad_rows(val.reshape(-1), -(-n // (8 * D)) * 8))
    return jnp.concatenate(parts, axis=0)


def _unpack_small(flat):
    outs, r0 = [], 0
    for _, shape in SMALL:
        n = math.prod(shape)
        rows = -(-n // (8 * D)) * 8
        outs.append(flat[r0:r0 + rows].reshape(-1)[:n].reshape(shape))
        r0 += rows
    return outs


def _rearrange_cols(w):
    pad = jnp.zeros(w.shape[:-1] + (NCOL - BD0 - 2 * NH,), w.dtype)
    return jnp.concatenate([w[..., :4 * D], w[..., 4 * D + 2 * NH:], w[..., 4 * D:4 * D + 2 * NH], pad], axis=-1)


def _restore_cols(w):
    return jnp.concatenate([w[..., :4 * D], w[..., BD0:BD0 + 2 * NH], w[..., 4 * D:BD0]], axis=-1)


def _block_diag_b(bb2):
    b = bb2.reshape(-1, NCB, GPB, NS, GS)
    return jnp.einsum("lkgnc,gh->lkgchn", b, jnp.eye(GPB, dtype=F32)).reshape(-1, NCB, GPB * GS, SW)


def _block_diag_b_t(d):
    blocks = jnp.einsum("lkgchn,gh->lkgnc", d.reshape(-1, NCB, GPB, GS, GPB, NS), jnp.eye(GPB, dtype=F32))
    return blocks.reshape(-1, NG, NS * GS)


def _block_diag_c(c):
    blocks = jnp.einsum("lkgcn,gh->lkgnhc", c.reshape(-1, NCB, GPB, GS, NS), jnp.eye(GPB, dtype=F32))
    return blocks.reshape(-1, NCB, SW, GPB * GS)


def _block_diag_c_t(d):
    blocks = jnp.einsum("lkgnhc,gh->lkgcn", d.reshape(-1, NCB, GPB, NS, GPB, GS), jnp.eye(GPB, dtype=F32))
    return blocks.reshape(-1, NG, GS, NS)


def _local_step(x, target, weights, conv, small, comm=None):
    weights = list(weights) + [None] * (DEPTH - len(weights))
    ar, ai = small["ssm_a_re"], small["ssm_a_im"]
    ldt = small["ssm_log_dt"].reshape(DEPTH, NG, 1)
    br2 = small["ssm_b_re"].reshape(DEPTH, NG, NS * GS)
    bi2 = small["ssm_b_im"].reshape(DEPTH, NG, NS * GS)
    lr, li, bbr2, bbi2 = _s5_params(ar, ai, ldt, br2, bi2)

    def row(name, l, width):
        return small[name][l].reshape(1, width)

    gvecs = jnp.pad(jnp.stack([small["a_log"], small["dt_bias"]], axis=1), ((0, 0), (0, 6), (NH, DH - 2 * NH)))
    lams = jnp.stack([lr.reshape(DEPTH, 1, NG * NS), li.reshape(DEPTH, 1, NG * NS)], axis=1)
    bblks = jnp.stack([_block_diag_b(bbr2), _block_diag_b(bbi2)], axis=1)
    cblks = jnp.stack([_block_diag_c(small["ssm_c_re"]), _block_diag_c(small["ssm_c_im"])], axis=1)
    saved = []
    for l in range(DEPTH):
        gvec, lam, bblk, cblk = gvecs[l], lams[l], bblks[l], cblks[l]
        wcat, wglu, wout = weights[l]
        proj, h = _inproj_fwd(x, row("norm_pre", l, D), wcat, 0)
        qkv = _prep_fwd(proj, conv[l])
        bb, gcb = _gates_fwd(proj, gvec)
        fetch = _Exchange(comm["weight_parts"](l + 1), CHIP_RELS, True, False) if comm and l + 1 < DEPTH else None
        local, t_inv, fetched = _delta_local_fwd(qkv, bb, gcb, fetch)
        if fetch is not None:
            weights[l + 1] = comm["weights_from"](fetched)
        o, states = _delta_state_fwd(local, gcb)
        s5y, carries = _s5_fwd(proj, lam, bblk, cblk)
        xn = _mix_fwd(proj, o, s5y, x, row("head_norm", l, DH), row("ssm_d", l, D), wglu, row("b_glu", l, D),
                      wout, row("norm_post", l, D), 0)
        saved.append((x, proj, h, qkv, bb, gcb, local, t_inv, o, states, s5y, carries, gvec, lam, bblk, cblk))
        x = xn

    dx, loss_part = _loss_grad(x, target)

    g = {k: [None] * DEPTH for k in ("wcat", "conv", "wglu", "wout", "norm_pre", "a_log", "dt_bias", "head_norm",
                                     "ssm_d", "b_glu", "norm_post", "dlam", "dbblk", "dcblk")}
    from_chips, send, send_layer = [None] * DEPTH, None, None
    for l in reversed(range(DEPTH)):
        xl, proj, h, qkv, bb, gcb, local, t_inv, o, states, s5y, carries, gvec, lam, bblk, cblk = saved[l]
        wcat, wglu, wout = weights[l]
        (dza, du_skip, dzb, dra, drb, do, ds5y, dxres, dwg, dwo, dvecs, dhn) = _mix_bwd(
            proj, o, s5y, xl, row("head_norm", l, DH), row("ssm_d", l, D), wglu, row("b_glu", l, D), wout,
            row("norm_post", l, D), dx, 0)
        (du, dlam, dbblk, dcblk), arrived = _s5_bwd(proj, lam, bblk, cblk, carries, ds5y, du_skip, send)
        if send is not None:
            from_chips[send_layer] = arrived
        *dlocal, dgcb_state = _delta_state_bwd(local, gcb, states, do)
        dq, dk, dv, dbb, dgcb = _delta_local_bwd(qkv, bb, gcb, t_inv, dlocal, dgcb_state)
        dbd, dgvec = _gates_bwd(proj, gvec, dbb, dgcb)
        dpre, dconv = _prep_bwd(proj, conv[l], dq, dk, dv)
        dproj = jnp.concatenate([dpre, dza, du, dzb, dra, drb, dbd], axis=1)
        g["wcat"][l] = _inproj_bwd_dw(h, dproj)
        g["conv"][l], g["wglu"][l], g["wout"][l] = dconv, dwg, dwo
        send = _Exchange(comm["grad_parts"](g["wcat"][l], dwg, dwo), CHIP_RELS, True, True) if comm else None
        dx, dgain, arrived = _inproj_bwd_dx(dproj, wcat, xl, row("norm_pre", l, D), dxres, 0, send if l == 0 else None)
        if comm and l == 0:
            from_chips[l] = arrived
        send_layer = l
        g["norm_pre"][l] = dgain[0]
        g["a_log"][l], g["dt_bias"][l] = dgvec[0, NH:2 * NH], dgvec[1, NH:2 * NH]
        g["head_norm"][l] = dhn[0]
        g["ssm_d"][l], g["b_glu"][l], g["norm_post"][l] = dvecs[0], dvecs[1], dvecs[2]
        g["dlam"][l], g["dbblk"][l], g["dcblk"][l] = dlam, dbblk, dcblk
    if comm:
        for k in ("wcat", "wglu", "wout"):
            del g[k]
    g = {k: jnp.stack(v) for k, v in g.items()}
    g["from_chips"] = from_chips
    dlam, dbblk, dcblk = g.pop("dlam"), g.pop("dbblk"), g.pop("dcblk")
    g["ssm_c_re"], g["ssm_c_im"] = _block_diag_c_t(dcblk[:, 0]), _block_diag_c_t(dcblk[:, 1])
    dar, dai, dldt, dbr2, dbi2 = _s5_params_bwd(
        ar, ai, ldt, br2, bi2, dlam[:, 0].reshape(DEPTH, NG, NS), dlam[:, 1].reshape(DEPTH, NG, NS),
        _block_diag_b_t(dbblk[:, 0]), _block_diag_b_t(dbblk[:, 1]))
    g["ssm_a_re"], g["ssm_a_im"], g["ssm_log_dt"] = dar, dai, dldt.reshape(DEPTH, NG)
    g["ssm_b_re"] = dbr2.reshape(DEPTH, NG, NS, GS)
    g["ssm_b_im"] = dbi2.reshape(DEPTH, NG, NS, GS)
    return loss_part[0, 0], dx, g


def kernel(x, norm_pre, w_in, conv_w, a_log, dt_bias, head_norm, ssm_a_re, ssm_a_im, ssm_log_dt, ssm_b_re, ssm_b_im, ssm_c_re, ssm_c_im, ssm_d, w_glu, b_glu, w_out, norm_post, loss_target, m_norm_pre, m_w_in, m_conv_w, m_a_log, m_dt_bias, m_head_norm, m_ssm_a_re, m_ssm_a_im, m_ssm_log_dt, m_ssm_b_re, m_ssm_b_im, m_ssm_c_re, m_ssm_c_im, m_ssm_d, m_w_glu, m_b_glu, m_w_out, m_norm_post, v_norm_pre, v_w_in, v_conv_w, v_a_log, v_dt_bias, v_head_norm, v_ssm_a_re, v_ssm_a_im, v_ssm_log_dt, v_ssm_b_re, v_ssm_b_im, v_ssm_c_re, v_ssm_c_im, v_ssm_d, v_w_glu, v_b_glu, v_w_out, v_norm_post):
    args = dict(locals())
    small = {n: args[n] for n, _ in SMALL}

    def flat2(a):
        return a.reshape(-1, a.shape[-1])

    w_in16, w_glu16, w_out16 = w_in.astype(BF16), w_glu.astype(BF16), w_out.astype(BF16)

    def weight_parts(l):
        return [w_in16[l], w_glu16[l], w_out16[l]]

    def weights_from(parts):
        g_in, g_glu, g_out = parts[:3]
        return (_rearrange_cols(_cols_from_chips(g_in, 1)), g_glu.reshape(1, D, D), g_out.reshape(1, D, D))

    def grad_parts(gwcat, gwglu, gwout):
        return [_cols_by_chip(_restore_cols(gwcat[None])).astype(BF16), gwglu.reshape(NCHIP, ROW_SHARD, D).astype(BF16),
                gwout.reshape(NCHIP, ROW_SHARD, D).astype(BF16)]

    first = _exchange(weight_parts(0) + [flat2(conv_w)], CHIP_RELS, True, False, "gather_weights")
    conv = _cols_from_chips(first[3], DEPTH)
    comm = dict(weight_parts=weight_parts, weights_from=weights_from, grad_parts=grad_parts)
    loss_part, dx, g = _local_step(x[0], loss_target[0], [weights_from(first)], conv, small, comm)
    loss = lax.psum(loss_part, ("x", "y", "c"))

    from_chips = [jnp.concatenate([g["from_chips"][l][a] for l in range(DEPTH)], axis=1) for a in range(3)]
    core_sums = [_sum_slots(p, "sum_chips_" + n) for p, n in zip(from_chips, ("in", "glu", "out"))]
    others = _sibling_swap(core_sums, "swap_cores")
    sharded = {}
    for n, mine, other in zip(("w_in", "w_glu", "w_out"), core_sums, others):
        sharded[n] = _adamw(flat2(args[n]), [mine, other], flat2(args["m_" + n]), flat2(args["v_" + n]), "adamw_" + n,
                            max_rows=128)

    pad = jnp.zeros(((-(SMALL_ROWS + CONV_ROWS)) % (8 * NDEV), D), F32)
    small_sum = _all_reduce(_pack_small([g[n] for n, _ in SMALL], extra=[g["conv"], pad]), "reduce_small")
    small_out = _adamw(_pack_small([args[n] for n, _ in SMALL]), [small_sum],
                       _pack_small([args["m_" + n] for n, _ in SMALL]),
                       _pack_small([args["v_" + n] for n, _ in SMALL]), "adamw_small")
    chip = 2 * lax.axis_index("x") + lax.axis_index("y")
    conv_sum = small_sum[SMALL_ROWS:SMALL_ROWS + CONV_ROWS].reshape(DEPTH * 4, 3 * D)
    conv_sum = lax.dynamic_slice_in_dim(conv_sum, chip * CONV_SHARD, CONV_SHARD, axis=1)
    sharded["conv_w"] = _adamw(flat2(conv_w), [conv_sum], flat2(m_conv_w), flat2(v_conv_w), "adamw_conv")

    names = ["norm_pre", "w_in", "conv_w", "a_log", "dt_bias", "head_norm", "ssm_a_re", "ssm_a_im", "ssm_log_dt",
             "ssm_b_re", "ssm_b_im", "ssm_c_re", "ssm_c_im", "ssm_d", "w_glu", "b_glu", "w_out", "norm_post"]
    outs = [loss, dx[None]]
    for i in range(4):
        sm = dict(zip([n for n, _ in SMALL], _unpack_small(small_out[i])))
        outs += [sharded[n][i].reshape(args[n].shape) if n in sharded else sm[n] for n in names]
    return tuple(outs)
```

```python
import functools
import math

import jax
import jax.numpy as jnp
from jax import lax
from jax.experimental import pallas as pl
from jax.experimental.pallas import tpu as pltpu

F32 = jnp.float32
BF16 = jnp.bfloat16
HI = lax.Precision.HIGHEST

D = 1024
NH = 8
DH = 128
CH = 128
NG = 64
GS = 16
NS = 64
GPB = 8
NCB = NG // GPB
SW = GPB * NS
NCOL = 8320
BD0 = 8192
EPS = 1e-6
DEPTH = 4
NCHIP = 4
NDEV = 8
VMEM_LIMIT = 56 * 1024 * 1024
GRAD_ACT = jnp.bfloat16

ADAM_LR = 0.001
ADAM_B1 = 0.9
ADAM_B2 = 0.999
ADAM_EPS = 1e-08
ADAM_WD = 0.01
ADAM_STEP = 10


def _cparams(sem=None):
    return pltpu.CompilerParams(dimension_semantics=sem, vmem_limit_bytes=VMEM_LIMIT)


def _full(shape):
    nd = len(shape)
    return pl.BlockSpec(shape, lambda *_: (0,) * nd)


def _rms(x, gain):
    ms = jnp.mean(x * x, axis=-1, keepdims=True)
    return x * lax.rsqrt(ms + EPS) * gain


def _sigmoid(x):
    return 1.0 / (1.0 + jnp.exp(-x))


def _silu(x):
    return x * _sigmoid(x)


def _softplus(x):
    return jnp.maximum(x, 0.0) + jnp.log(1.0 + jnp.exp(-jnp.abs(x)))


def _gelu(x):
    return 0.5 * x * (1.0 + jnp.tanh(math.sqrt(2.0 / math.pi) * (x + 0.044715 * (x * x * x))))


def _dot_bf16(a, b, dims):
    return lax.dot_general(a.astype(BF16), b.astype(BF16), (dims, ((), ())), preferred_element_type=F32)


def _mm_nt(a, b):
    return _dot_bf16(a, b, ((1,), (1,)))


def _mm_tn(a, b):
    return _dot_bf16(a, b, ((0,), (0,)))


@jax.custom_vjp
def _mm(a, b):
    return _dot_bf16(a, b, ((1,), (0,)))


def _mm_fwd(a, b):
    return _dot_bf16(a, b, ((1,), (0,))), (a, b)


def _mm_bwd(res, ct):
    a, b = res
    return _mm_nt(ct, b).astype(a.dtype), _mm_tn(a, ct).astype(b.dtype)


_mm.defvjp(_mm_fwd, _mm_bwd)


@jax.custom_vjp
def _mm_nt_d(a, b):
    return _mm_nt(a, b)


def _mm_nt_d_bwd(res, ct):
    a, b = res
    return _dot_bf16(ct, b, ((1,), (0,))), _mm_tn(ct, a)


_mm_nt_d.defvjp(lambda a, b: (_mm_nt(a, b), (a, b)), _mm_nt_d_bwd)


@jax.custom_vjp
def _mm_tn_d(a, b):
    return _mm_tn(a, b)


def _mm_tn_d_bwd(res, ct):
    a, b = res
    return _mm_nt(b, ct), _dot_bf16(a, ct, ((1,), (0,)))


_mm_tn_d.defvjp(lambda a, b: (_mm_tn(a, b), (a, b)), _mm_tn_d_bwd)


def _split_bf16(a):
    hi = a.astype(BF16)
    return hi, (a - hi.astype(F32)).astype(BF16)


def _dot3(a, b, dims):
    ah, al = _split_bf16(a)
    bh, bl = _split_bf16(b)

    def dot(x, y):
        return lax.dot_general(x, y, (dims, ((), ())), preferred_element_type=F32)

    return dot(ah, bh) + (dot(ah, bl) + dot(al, bh))


@jax.custom_vjp
def _imm(a, b):
    return _dot3(a, b, ((1,), (0,)))


def _imm_bwd(res, ct):
    a, b = res
    return _dot3(ct, b, ((1,), (1,))), _dot3(a, ct, ((0,), (0,)))


_imm.defvjp(lambda a, b: (_dot3(a, b, ((1,), (0,))), (a, b)), _imm_bwd)


def _hmm(a, b):
    return jnp.dot(a, b, precision=HI, preferred_element_type=F32)


def _hmm_nt(a, b):
    return lax.dot_general(a, b, (((1,), (1,)), ((), ())), precision=HI, preferred_element_type=F32)


def _hmm_tn(a, b):
    return lax.dot_general(a, b, (((0,), (0,)), ((), ())), precision=HI, preferred_element_type=F32)


def _rows(shape):
    return lax.broadcasted_iota(jnp.int32, shape, 0)


def _cols(shape):
    return lax.broadcasted_iota(jnp.int32, shape, 1)


def _sd(x, s):
    return jnp.where(_rows(x.shape) >= s, pltpu.roll(x, s, axis=0), 0.0)


def _su(x, s):
    n = x.shape[0]
    return jnp.where(_rows(x.shape) < n - s, pltpu.roll(x, n - s, axis=0), 0.0)


@functools.partial(jax.custom_vjp, nondiff_argnums=(1,))
def _shift_down(x, s):
    return _sd(x, s)


def _shift_down_fwd(x, s):
    return _sd(x, s), None


def _shift_down_bwd(s, _, g):
    return (_su(g, s),)


_shift_down.defvjp(_shift_down_fwd, _shift_down_bwd)


def _last_row(x):
    n = x.shape[0]
    return jnp.sum(jnp.where(_rows(x.shape) == n - 1, x, 0.0), axis=0, keepdims=True)


def _prep_fn(p, w0, w1, w2, w3, qk):
    acc = w3 * p + w2 * _shift_down(p, 1) + w1 * _shift_down(p, 2) + w0 * _shift_down(p, 3)
    a = _silu(acc)
    nrm = lax.rsqrt(jnp.sum(a * a, axis=-1, keepdims=True) + EPS)
    return a * (nrm * qk + (1.0 - qk))


def _gates_fn(bd, av, bv):
    tm = bd.shape[0]
    beta_all = _sigmoid(bd)
    g_all = -jnp.exp(av) * _softplus(bd + bv)
    r, c = _rows((tm, tm)), _cols((tm, tm))
    tri = jnp.where((r // CH == c // CH) & (r >= c), 1.0, 0.0).astype(F32)
    gc_all = _hmm(tri, g_all)
    lane = _cols(bd.shape)
    outs = []
    for h in range(NH):
        b = jnp.sum(jnp.where(lane == h, beta_all, 0.0), axis=1, keepdims=True)
        outs.append(jnp.broadcast_to(b, bd.shape))
    for h in range(NH):
        g = jnp.sum(jnp.where(lane == NH + h, gc_all, 0.0), axis=1, keepdims=True)
        outs.append(jnp.broadcast_to(g, bd.shape))
    return tuple(outs)


INV_BASE = 2


def _merge_mm(a, b):
    return _dot_bf16(a, b, ((1,), (0,)))


def _unit_lower_inv(l_mat):
    n = l_mat.shape[0]
    ii, jj = _rows((n, n)), _cols((n, n))
    p = -jnp.where(ii // INV_BASE == jj // INV_BASE, l_mat, 0.0)
    d = jnp.where(ii == jj, 1.0, 0.0).astype(F32) + p
    k = 1
    while 2 * k < INV_BASE:
        p = _imm(p, p)
        d = d + _imm(d, p)
        k *= 2
    m = INV_BASE
    while m < n:
        e = jnp.where((ii // (2 * m) == jj // (2 * m)) & (ii // m > jj // m), l_mat, 0.0)
        d = d - _merge_mm(_merge_mm(d, e), d)
        m *= 2
    return d


@jax.custom_vjp
def _known_inverse(l_mat, t_inv):
    return t_inv


def _known_inverse_bwd(t_inv, ct):
    d_l = -_dot3(_dot3(t_inv, ct, ((0,), (0,))), t_inv, ((1,), (1,)))
    return d_l, jnp.zeros_like(t_inv)


_known_inverse.defvjp(lambda l_mat, t_inv: (t_inv, t_inv), _known_inverse_bwd)


def _chunk_local(q, k, v, bb, gcb, t_inv=None):
    qs = q * (DH ** -0.5)
    kb = k * bb
    eg = jnp.exp(gcb)
    ii, jj = _rows((CH, CH)), _cols((CH, CH))
    decay = jnp.exp(jnp.where(ii >= jj, gcb - gcb.T, -1e30))
    l_mat = jnp.where(ii > jj, _mm_nt_d(kb, k) * decay, 0.0)
    t_inv = _unit_lower_inv(l_mat) if t_inv is None else _known_inverse(l_mat, t_inv)
    u = _mm(t_inv, v * bb)
    w = _mm(t_inv, kb * eg)
    a_qk = _mm_nt_d(qs, k) * decay
    k_dec = k * jnp.exp(_last_row(gcb) - gcb)
    return (u, w, qs * eg, k_dec, a_qk), t_inv


def _state_step(u, w, q_dec, k_dec, a_qk, gcb, state):
    v_new = u - _mm(w, state)
    o = _mm(q_dec, state) + _mm(a_qk, v_new)
    new_state = state * jnp.exp(_last_row(gcb)) + _mm_tn_d(k_dec, v_new)
    return o, new_state


SUB = 8


def _cmul(ar, ai, br, bi):
    return ar * br - ai * bi, ar * bi + ai * br


def _scan_tile(xr, xi, mr, mi, hr_ref, hi_ref, cr_ref, ci_ref, reverse):
    n, width = xr.shape
    ngroups = n // SUB
    shift_groups = _su if reverse else _sd
    xr, xi = xr.reshape(ngroups, SUB, width), xi.reshape(ngroups, SUB, width)
    pr, pi = mr, mi
    tr, ti = jnp.broadcast_to(mr, (SUB, width)), jnp.broadcast_to(mi, (SUB, width))
    pos = _rows(tr.shape)
    s = 1
    while s < SUB:
        inside = pos < SUB - s if reverse else pos >= s
        shift = SUB - s if reverse else s
        qr, qi = jnp.where(inside, pr, 0.0)[None], jnp.where(inside, pi, 0.0)[None]
        dr, di = _cmul(qr, qi, pltpu.roll(xr, shift, axis=1), pltpu.roll(xi, shift, axis=1))
        xr, xi = xr + dr, xi + di
        er = jnp.where(inside, pltpu.roll(tr, shift, axis=0), 1.0)
        ei = jnp.where(inside, pltpu.roll(ti, shift, axis=0), 0.0)
        tr, ti = _cmul(tr, ti, er, ei)
        pr, pi = _cmul(pr, pi, pr, pi)
        s *= 2
    xr, xi = xr.reshape(n, width), xi.reshape(n, width)
    nlb = width // DH

    def lanes(x, j):
        return x[:, j * DH:(j + 1) * DH]

    for j in range(nlb):
        hr_ref[j] = lanes(xr, j)
        hi_ref[j] = lanes(xi, j)
    edge = pl.ds(0 if reverse else SUB - 1, ngroups, stride=SUB)
    gr = jnp.concatenate([hr_ref.at[j][edge, :] for j in range(nlb)], axis=1)
    gi = jnp.concatenate([hi_ref.at[j][edge, :] for j in range(nlb)], axis=1)
    s = 1
    while s < ngroups:
        dr, di = _cmul(pr, pi, shift_groups(gr, s), shift_groups(gi, s))
        gr, gi = gr + dr, gi + di
        pr, pi = _cmul(pr, pi, pr, pi)
        s *= 2
    cr_ref[...] = shift_groups(gr, 1)
    ci_ref[...] = shift_groups(gi, 1)
    for g in range(ngroups):
        rows = slice(g * SUB, (g + 1) * SUB)
        dr, di = _cmul(tr, ti, cr_ref[g:g + 1, :], ci_ref[g:g + 1, :])
        for j in range(nlb):
            hr_ref[j, rows, :] += lanes(dr, j)
            hi_ref[j, rows, :] += lanes(di, j)
    return (jnp.concatenate([hr_ref[j] for j in range(nlb)], axis=1),
            jnp.concatenate([hi_ref[j] for j in range(nlb)], axis=1))


def _s5_states(u, lam_ref, b_ref, car_ref, hr_ref, hi_ref, cr_ref, ci_ref):
    lr, li = lam_ref[0], lam_ref[1]
    first = _rows((u.shape[0], SW)) == 0
    inr, ini = _cmul(lr, li, car_ref[0:1, :], car_ref[1:2, :])
    xr = _mm(u, b_ref[0]) + jnp.where(first, inr, 0.0)
    xi = _mm(u, b_ref[1]) + jnp.where(first, ini, 0.0)
    return _scan_tile(xr, xi, lr, li, hr_ref, hi_ref, cr_ref, ci_ref, False)


def _s5_params_fn(ar, ai, ldt, br2, bi2):
    dt = jnp.exp(ldt)
    mag = jnp.exp(ar * dt)
    lr, li = mag * jnp.cos(ai * dt), mag * jnp.sin(ai * dt)
    den = ar * ar + ai * ai
    fr = ((lr - 1.0) * ar + li * ai) / den
    fi = (li * ar - (lr - 1.0) * ai) / den
    expand = jnp.where(_cols((NS, NS * GS)) // GS == _rows((NS, NS * GS)), 1.0, 0.0).astype(F32)
    fr2, fi2 = _hmm(fr, expand), _hmm(fi, expand)
    return lr, li, fr2 * br2 - fi2 * bi2, fr2 * bi2 + fi2 * br2


def _head_norm(o, hn):
    parts = []
    for h in range(NH):
        oh = o[:, h * DH:(h + 1) * DH]
        parts.append(oh * lax.rsqrt(jnp.mean(oh * oh, axis=-1, keepdims=True) + EPS) * hn)
    return jnp.concatenate(parts, axis=1)


def _mix_pre(s5y, u, dvec):
    return _gelu(s5y + dvec * u)


def _mix_mid(o, za, y0, gl, zb, ra, rb, hn):
    ya = _head_norm(o, hn) * _silu(za)
    yb = y0 * _sigmoid(gl) * _silu(zb)
    return _sigmoid(ra) * ya + _sigmoid(rb) * yb


def _mix_post(x, out, npost):
    return x + _rms(out, npost)


def _tile(t, want):
    return min(t, want)


def _row_tile(rows, want):
    return max(r for r in range(16, want + 1, 16) if rows % r == 0)


def _inproj_fwd(x, gain, wcat, l):
    t = x.shape[0]
    tm, tn = _tile(t, 1024), 640

    def body(x_ref, g_ref, w_ref, o_ref, h_ref):
        @pl.when(pl.program_id(1) == 0)
        def _():
            h_ref[...] = _rms(x_ref[...], g_ref[...]).astype(h_ref.dtype)
        o_ref[...] = _dot_bf16(h_ref[...], w_ref[...], ((1,), (0,)))

    return pl.pallas_call(
        body, name="inproj_fwd", grid=(t // tm, NCOL // tn),
        in_specs=[pl.BlockSpec((tm, D), lambda i, j: (i, 0)), _full((1, D)),
                  pl.BlockSpec((None, D, tn), lambda i, j: (l, 0, j))],
        out_specs=[pl.BlockSpec((tm, tn), lambda i, j: (i, j)), pl.BlockSpec((tm, D), lambda i, j: (i, 0))],
        out_shape=[jax.ShapeDtypeStruct((t, NCOL), F32), jax.ShapeDtypeStruct((t, D), wcat.dtype)],
        compiler_params=_cparams(("parallel", "arbitrary")),
    )(x, gain, wcat)


def _inproj_bwd_dx(dproj, wcat, x, gain, dxres, l, exchange=None):
    t = x.shape[0]
    tm, tk = _tile(t, 1024), 640
    nk = NCOL // tk

    def body(dp_ref, w_ref, x_ref, g_ref, r_ref, dx_ref, dg_ref, acc_ref):
        i, k = pl.program_id(0), pl.program_id(1)

        @pl.when(k == 0)
        def _():
            acc_ref[...] = jnp.zeros_like(acc_ref)

        acc_ref[...] += _mm_nt(dp_ref[...], w_ref[...])

        @pl.when(k == nk - 1)
        def _():
            _, vjp = jax.vjp(_rms, x_ref[...], g_ref[...])
            dx, dg = vjp(acc_ref[...])
            dx_ref[...] = r_ref[...] + dx

            @pl.when(i == 0)
            def _():
                dg_ref[...] = dg

            @pl.when(i > 0)
            def _():
                dg_ref[...] += dg

    grid = (t // tm, nk)
    in_specs = [pl.BlockSpec((tm, tk), lambda i, k: (i, k)), pl.BlockSpec((None, D, tk), lambda i, k: (l, 0, k)),
                pl.BlockSpec((tm, D), lambda i, k: (i, 0)), _full((1, D)), pl.BlockSpec((tm, D), lambda i, k: (i, 0))]
    out_specs = [pl.BlockSpec((tm, D), lambda i, k: (i, 0)), _full((1, D))]
    out_shape = [jax.ShapeDtypeStruct((t, D), F32), jax.ShapeDtypeStruct((1, D), F32)]
    scratch, args = [pltpu.VMEM((tm, D), F32)], [dproj, wcat, x, gain, dxres]
    if exchange is not None:
        body = _carry(body, len(args), len(out_shape), len(scratch), exchange, grid)
        in_specs, out_specs = in_specs + exchange.in_specs, out_specs + exchange.out_specs
        out_shape, scratch, args = out_shape + exchange.out_shape, scratch + exchange.scratch_shapes, args + exchange.srcs
    outs = pl.pallas_call(
        body, name="inproj_bwd_dx", grid=grid, in_specs=in_specs, out_specs=out_specs, out_shape=out_shape,
        scratch_shapes=scratch, compiler_params=_cparams(("arbitrary", "arbitrary")),
    )(*args)
    return outs[0], outs[1], outs[2:]


def _inproj_bwd_dw(h, dproj):
    t = h.shape[0]
    tm, tn = _tile(t, 512), 1664

    def body(h_ref, dp_ref, o_ref):
        @pl.when(pl.program_id(1) == 0)
        def _():
            o_ref[...] = jnp.zeros_like(o_ref)

        o_ref[...] += _mm_tn(h_ref[...], dp_ref[...])

    return pl.pallas_call(
        body, name="inproj_bwd_dw", grid=(NCOL // tn, t // tm),
        in_specs=[pl.BlockSpec((tm, D), lambda j, i: (i, 0)), pl.BlockSpec((tm, tn), lambda j, i: (i, j))],
        out_specs=pl.BlockSpec((D, tn), lambda j, i: (0, j)),
        out_shape=jax.ShapeDtypeStruct((D, NCOL), F32),
        compiler_params=_cparams(("parallel", "arbitrary")),
    )(h, dproj)


def _prep_fwd(proj, cw):
    t = proj.shape[0]

    def body(p_ref, w_ref, o_ref):
        qk = (pl.program_id(0) < 2 * NH).astype(F32)
        o_ref[...] = _prep_fn(p_ref[...], w_ref[0:1, :], w_ref[1:2, :], w_ref[2:3, :], w_ref[3:4, :], qk)

    return pl.pallas_call(
        body, name="prep_fwd", grid=(3 * NH,),
        in_specs=[pl.BlockSpec((t, DH), lambda c: (0, c)), pl.BlockSpec((4, DH), lambda c: (0, c))],
        out_specs=pl.BlockSpec((None, t, DH), lambda c: (c, 0, 0)),
        out_shape=jax.ShapeDtypeStruct((3 * NH, t, DH), F32),
        compiler_params=_cparams(("parallel",)),
    )(proj, cw)


def _prep_bwd(proj, cw, dq, dk, dv):
    t = proj.shape[0]

    def body(p_ref, w_ref, dq_ref, dk_ref, dv_ref, dp_ref, dw_ref):
        c = pl.program_id(0)
        qk = (c < 2 * NH).astype(F32)
        _, vjp = jax.vjp(lambda p, w0, w1, w2, w3: _prep_fn(p, w0, w1, w2, w3, qk),
                         p_ref[...], w_ref[0:1, :], w_ref[1:2, :], w_ref[2:3, :], w_ref[3:4, :])
        d = jnp.where(c < NH, dq_ref[...], jnp.where(c < 2 * NH, dk_ref[...], dv_ref[...]))
        dp, dw0, dw1, dw2, dw3 = vjp(d)
        dp_ref[...] = dp.astype(dp_ref.dtype)
        dw_ref[0:1, :] = dw0
        dw_ref[1:2, :] = dw1
        dw_ref[2:3, :] = dw2
        dw_ref[3:4, :] = dw3

    return pl.pallas_call(
        body, name="prep_bwd", grid=(3 * NH,),
        in_specs=[pl.BlockSpec((t, DH), lambda c: (0, c)), pl.BlockSpec((4, DH), lambda c: (0, c))]
        + [pl.BlockSpec((None, t, DH), functools.partial(lambda c, off: (jnp.clip(c - off, 0, NH - 1), 0, 0), off=off))
           for off in (0, NH, 2 * NH)],
        out_specs=[pl.BlockSpec((t, DH), lambda c: (0, c)), pl.BlockSpec((4, DH), lambda c: (0, c))],
        out_shape=[jax.ShapeDtypeStruct((t, 3 * D), GRAD_ACT), jax.ShapeDtypeStruct((4, 3 * D), F32)],
        compiler_params=_cparams(("arbitrary",)),
    )(proj, cw, dq, dk, dv)


def _gates_fwd(proj, gvec):
    t = proj.shape[0]
    tm = _tile(t, 512)

    def body(p_ref, gv_ref, b_ref, g_ref):
        outs = _gates_fn(p_ref[...], gv_ref[0:1, :], gv_ref[1:2, :])
        for h in range(NH):
            b_ref[h] = outs[h]
            g_ref[h] = outs[NH + h]

    spec = pl.BlockSpec((NH, tm, DH), lambda i: (0, i, 0))
    return pl.pallas_call(
        body, name="gates_fwd", grid=(t // tm,),
        in_specs=[pl.BlockSpec((tm, DH), lambda i: (i, BD0 // DH)), _full((8, DH))],
        out_specs=[spec, spec],
        out_shape=[jax.ShapeDtypeStruct((NH, t, DH), F32)] * 2,
        compiler_params=_cparams(("parallel",)),
    )(proj, gvec)


def _gates_bwd(proj, gvec, dbb, dgcb):
    t = proj.shape[0]
    tm = _tile(t, 512)

    def body(p_ref, gv_ref, db_ref, dg_ref, dp_ref, dgv_ref):
        _, vjp = jax.vjp(_gates_fn, p_ref[...], gv_ref[0:1, :], gv_ref[1:2, :])
        cts = tuple(db_ref[h] for h in range(NH)) + tuple(dg_ref[h] for h in range(NH))
        dp, da, db = vjp(cts)
        dp_ref[...] = dp.astype(dp_ref.dtype)

        @pl.when(pl.program_id(0) == 0)
        def _():
            dgv_ref[...] = jnp.zeros_like(dgv_ref)

        dgv_ref[0:1, :] += da
        dgv_ref[1:2, :] += db

    spec = pl.BlockSpec((NH, tm, DH), lambda i: (0, i, 0))
    return pl.pallas_call(
        body, name="gates_bwd", grid=(t // tm,),
        in_specs=[pl.BlockSpec((tm, DH), lambda i: (i, BD0 // DH)), _full((8, DH)), spec, spec],
        out_specs=[pl.BlockSpec((tm, DH), lambda i: (i, 0)), _full((8, DH))],
        out_shape=[jax.ShapeDtypeStruct((t, DH), GRAD_ACT), jax.ShapeDtypeStruct((8, DH), F32)],
        compiler_params=_cparams(("arbitrary",)),
    )(proj, gvec, dbb, dgcb)


def _chunks_per_step(nch):
    return 2 if nch % 2 == 0 else 1


def _grid_ends(grid):
    def first():
        return functools.reduce(jnp.logical_and, [pl.program_id(a) == 0 for a in range(len(grid))])

    def last():
        return functools.reduce(jnp.logical_and, [pl.program_id(a) == n - 1 for a, n in enumerate(grid)])

    return first, last


def _carry(body, n_in, n_out, n_scratch, exchange, grid):
    first, last = _grid_ends(grid)
    na = exchange.narr

    def wrapped(*refs):
        a, b = n_in, n_in + na
        c, d = b + n_out, b + n_out + na
        e = d + n_scratch
        srcs, dsts, sems = refs[a:b], refs[c:d], refs[e:]

        @pl.when(first())
        def _():
            exchange.start(srcs, dsts, sems)

        body(*(refs[:a] + refs[b:c] + refs[d:e]))

        @pl.when(last())
        def _():
            exchange.wait(srcs, dsts, sems)

    return wrapped


def _delta_local_fwd(qkv, bb, gcb, exchange=None):
    t = qkv.shape[1]
    cps = _chunks_per_step(t // CH)
    rows = cps * CH
    grid = (NH, t // rows)

    def body(q_ref, k_ref, v_ref, b_ref, g_ref, *out_refs):
        for c in range(cps):
            sl = slice(c * CH, (c + 1) * CH)
            outs, t_inv = _chunk_local(q_ref[sl, :], k_ref[sl, :], v_ref[sl, :], b_ref[sl, :], g_ref[sl, :])
            for ref, val in zip(out_refs, outs + (t_inv,)):
                ref[sl, :] = val

    def blk(off):
        return pl.BlockSpec((None, rows, DH), lambda h, n: (h + off, n, 0))

    in_specs = [blk(0), blk(NH), blk(2 * NH), blk(0), blk(0)]
    out_specs = [blk(0)] * 6
    out_shape = [jax.ShapeDtypeStruct((NH, t, DH), F32)] * 6
    args, scratch, sem = [qkv, qkv, qkv, bb, gcb], [], ("parallel", "parallel")
    if exchange is not None:
        body = _carry(body, 5, 6, 0, exchange, grid)
        in_specs, out_specs = in_specs + exchange.in_specs, out_specs + exchange.out_specs
        out_shape, scratch, args = out_shape + exchange.out_shape, exchange.scratch_shapes, args + exchange.srcs
        sem = ("arbitrary", "arbitrary")
    outs = pl.pallas_call(
        body, name="delta_local_fwd", grid=grid, in_specs=in_specs, out_specs=out_specs, out_shape=out_shape,
        scratch_shapes=scratch, compiler_params=_cparams(sem),
    )(*args)
    return outs[:5], outs[5], outs[6:]


def _delta_local_bwd(qkv, bb, gcb, t_inv, cts, dgcb_state):
    t = qkv.shape[1]
    cps = _chunks_per_step(t // CH)
    rows = cps * CH

    def body(q_ref, k_ref, v_ref, b_ref, g_ref, ti_ref, du_ref, dw_ref, dqd_ref, dkd_ref, da_ref, dgs_ref,
             dq_ref, dk_ref, dv_ref, db_ref, dg_ref):
        for c in range(cps):
            sl = slice(c * CH, (c + 1) * CH)
            t_inv_c = ti_ref[sl, :]
            _, vjp = jax.vjp(lambda *a: _chunk_local(*a, t_inv=t_inv_c)[0],
                             q_ref[sl, :], k_ref[sl, :], v_ref[sl, :], b_ref[sl, :], g_ref[sl, :])
            dq, dk, dv, db, dg = vjp((du_ref[sl, :], dw_ref[sl, :], dqd_ref[sl, :], dkd_ref[sl, :], da_ref[sl, :]))
            dq_ref[sl, :] = dq
            dk_ref[sl, :] = dk
            dv_ref[sl, :] = dv
            db_ref[sl, :] = db
            dg_ref[sl, :] = dg + dgs_ref[sl, :]

    def blk(off):
        return pl.BlockSpec((None, rows, DH), lambda h, n: (h + off, n, 0))

    return pl.pallas_call(
        body, name="delta_local_bwd", grid=(NH, t // rows),
        in_specs=[blk(0), blk(NH), blk(2 * NH)] + [blk(0)] * 9,
        out_specs=[blk(0)] * 5,
        out_shape=[jax.ShapeDtypeStruct((NH, t, DH), F32)] * 5,
        compiler_params=_cparams(("parallel", "parallel")),
    )(qkv, qkv, qkv, bb, gcb, t_inv, *cts, dgcb_state)


def _delta_state_fwd(local, gcb):
    t = gcb.shape[1]
    nch = t // CH

    def body(u_ref, w_ref, qd_ref, kd_ref, a_ref, g_ref, o_ref, s_ref, st_ref):
        @pl.when(pl.program_id(0) == 0)
        def _():
            st_ref[...] = jnp.zeros_like(st_ref)

        for h in range(NH):
            s_ref[h] = st_ref[h]
            o, ns = _state_step(u_ref[h], w_ref[h], qd_ref[h], kd_ref[h], a_ref[h], g_ref[h], st_ref[h])
            o_ref[:, h * DH:(h + 1) * DH] = o
            st_ref[h] = ns

    blk = pl.BlockSpec((NH, CH, DH), lambda n: (0, n, 0))
    return pl.pallas_call(
        body, name="delta_state_fwd", grid=(nch,),
        in_specs=[blk] * 6,
        out_specs=[pl.BlockSpec((CH, D), lambda n: (n, 0)),
                   pl.BlockSpec((NH, None, DH, DH), lambda n: (0, n, 0, 0))],
        out_shape=[jax.ShapeDtypeStruct((t, D), F32), jax.ShapeDtypeStruct((NH, nch, DH, DH), F32)],
        scratch_shapes=[pltpu.VMEM((NH, DH, DH), F32)],
        compiler_params=_cparams(("arbitrary",)),
    )(*local, gcb)


def _delta_state_bwd(local, gcb, states, do):
    t = gcb.shape[1]
    nch = t // CH

    def body(u_ref, w_ref, qd_ref, kd_ref, a_ref, g_ref, s_ref, do_ref,
             du_ref, dw_ref, dqd_ref, dkd_ref, da_ref, dg_ref, ds_ref):
        @pl.when(pl.program_id(0) == 0)
        def _():
            ds_ref[...] = jnp.zeros_like(ds_ref)

        for h in range(NH):
            _, vjp = jax.vjp(_state_step, u_ref[h], w_ref[h], qd_ref[h], kd_ref[h], a_ref[h], g_ref[h], s_ref[h])
            du, dw, dqd, dkd, da, dg, ds = vjp((do_ref[:, h * DH:(h + 1) * DH], ds_ref[h]))
            du_ref[h] = du
            dw_ref[h] = dw
            dqd_ref[h] = dqd
            dkd_ref[h] = dkd
            da_ref[h] = da
            dg_ref[h] = dg
            ds_ref[h] = ds

    blk = pl.BlockSpec((NH, CH, DH), lambda n: (0, nch - 1 - n, 0))
    return pl.pallas_call(
        body, name="delta_state_bwd", grid=(nch,),
        in_specs=[blk] * 6 + [pl.BlockSpec((NH, None, DH, DH), lambda n: (0, nch - 1 - n, 0, 0)),
                              pl.BlockSpec((CH, D), lambda n: (nch - 1 - n, 0))],
        out_specs=[blk] * 6,
        out_shape=[jax.ShapeDtypeStruct((NH, t, DH), F32)] * 6,
        scratch_shapes=[pltpu.VMEM((NH, DH, DH), F32)],
        compiler_params=_cparams(("arbitrary",)),
    )(*local, gcb, states, do)


def _s5_params(ar, ai, ldt, br2, bi2):
    def body(ar_ref, ai_ref, ld_ref, br_ref, bi_ref, lr_ref, li_ref, bbr_ref, bbi_ref):
        lr, li, bbr, bbi = _s5_params_fn(ar_ref[...], ai_ref[...], ld_ref[...], br_ref[...], bi_ref[...])
        lr_ref[...] = lr
        li_ref[...] = li
        bbr_ref[...] = bbr
        bbi_ref[...] = bbi

    sq = pl.BlockSpec((None, NG, NS), lambda l: (l, 0, 0))
    wide = pl.BlockSpec((None, NG, NS * GS), lambda l: (l, 0, 0))
    return pl.pallas_call(
        body, name="s5_params", grid=(DEPTH,),
        in_specs=[sq, sq, pl.BlockSpec((None, NG, 1), lambda l: (l, 0, 0)), wide, wide],
        out_specs=[sq, sq, wide, wide],
        out_shape=[jax.ShapeDtypeStruct((DEPTH, NG, NS), F32)] * 2
        + [jax.ShapeDtypeStruct((DEPTH, NG, NS * GS), F32)] * 2,
        compiler_params=_cparams(("parallel",)),
    )(ar, ai, ldt, br2, bi2)


def _s5_params_bwd(ar, ai, ldt, br2, bi2, dlr, dli, dbbr, dbbi):
    def body(ar_ref, ai_ref, ld_ref, br_ref, bi_ref, a_ref, b_ref, c_ref, d_ref,
             dar_ref, dai_ref, dld_ref, dbr_ref, dbi_ref):
        _, vjp = jax.vjp(_s5_params_fn, ar_ref[...], ai_ref[...], ld_ref[...], br_ref[...], bi_ref[...])
        dar, dai, dld, dbr, dbi = vjp((a_ref[...], b_ref[...], c_ref[...], d_ref[...]))
        dar_ref[...] = dar
        dai_ref[...] = dai
        dld_ref[...] = dld
        dbr_ref[...] = dbr
        dbi_ref[...] = dbi

    sq = pl.BlockSpec((None, NG, NS), lambda l: (l, 0, 0))
    col = pl.BlockSpec((None, NG, 1), lambda l: (l, 0, 0))
    wide = pl.BlockSpec((None, NG, NS * GS), lambda l: (l, 0, 0))
    return pl.pallas_call(
        body, name="s5_params_bwd", grid=(DEPTH,),
        in_specs=[sq, sq, col, wide, wide, sq, sq, wide, wide],
        out_specs=[sq, sq, col, wide, wide],
        out_shape=[jax.ShapeDtypeStruct((DEPTH, NG, NS), F32)] * 2 + [jax.ShapeDtypeStruct((DEPTH, NG, 1), F32)]
        + [jax.ShapeDtypeStruct((DEPTH, NG, NS * GS), F32)] * 2,
        compiler_params=_cparams(("parallel",)),
    )(ar, ai, ldt, br2, bi2, dlr, dli, dbbr, dbbi)


def _s5_tile_rows(t):
    return _tile(t // 2, 1024)


def _s5_fwd(proj, lam, bblk, cblk):
    t = proj.shape[0]
    r = _s5_tile_rows(t)
    nt = t // r
    u0 = 4 * D // DH

    def body(u_ref, lam_ref, b_ref, c_ref, y_ref, car_ref, st_ref, hr_ref, hi_ref, cr_ref, ci_ref):
        @pl.when(pl.program_id(1) == 0)
        def _():
            st_ref[...] = jnp.zeros_like(st_ref)

        car_ref[...] = st_ref[...]
        hr, hi = _s5_states(u_ref[...], lam_ref, b_ref, st_ref, hr_ref, hi_ref, cr_ref, ci_ref)
        y_ref[...] = _mm(hr, c_ref[0]) - _mm(hi, c_ref[1])
        st_ref[0:1, :] = _last_row(hr)
        st_ref[1:2, :] = _last_row(hi)

    scratch = ([pltpu.VMEM((8, SW), F32)] + [pltpu.VMEM((SW // DH, r, DH), F32)] * 2
               + [pltpu.VMEM((r // SUB, SW), F32)] * 2)
    return pl.pallas_call(
        body, name="s5_fwd", grid=(NCB, nt),
        in_specs=[pl.BlockSpec((r, DH), lambda c, i: (i, u0 + c)),
                  pl.BlockSpec((2, 1, SW), lambda c, i: (0, 0, c)),
                  pl.BlockSpec((2, None, DH, SW), lambda c, i: (0, c, 0, 0)),
                  pl.BlockSpec((2, None, SW, DH), lambda c, i: (0, c, 0, 0))],
        out_specs=[pl.BlockSpec((r, DH), lambda c, i: (i, c)),
                   pl.BlockSpec((None, 8, SW), lambda c, i: (i, 0, c))],
        out_shape=[jax.ShapeDtypeStruct((t, D), F32), jax.ShapeDtypeStruct((nt, 8, NG * NS), F32)],
        scratch_shapes=scratch,
        compiler_params=_cparams(("parallel", "arbitrary")),
    )(proj, lam, bblk, cblk)


def _s5_bwd(proj, lam, bblk, cblk, carries, dy, du_skip, exchange=None):
    t = proj.shape[0]
    r = _s5_tile_rows(t)
    nt = t // r
    u0 = 4 * D // DH

    def body(u_ref, lam_ref, b_ref, c_ref, car_ref, dy_ref, dus_ref, du_ref, dlam_ref, db_ref, dc_ref, dst_ref,
             hr_ref, hi_ref, ar_ref, ai_ref, cr_ref, ci_ref):
        first = pl.program_id(1) == 0

        @pl.when(first)
        def _():
            dst_ref[...] = jnp.zeros_like(dst_ref)

        u, dy = u_ref[...], dy_ref[...]
        lr, li = lam_ref[0], lam_ref[1]
        hr, hi = _s5_states(u, lam_ref, b_ref, car_ref, hr_ref, hi_ref, cr_ref, ci_ref)
        dcr2, dci2 = _mm_tn(hr, dy), -_mm_tn(hi, dy)
        last = _rows((r, SW)) == r - 1
        inr, ini = _cmul(lr, -li, dst_ref[0:1, :], dst_ref[1:2, :])
        dhr = _mm_nt(dy, c_ref[0]) + jnp.where(last, inr, 0.0)
        dhi = jnp.where(last, ini, 0.0) - _mm_nt(dy, c_ref[1])
        ar, ai = _scan_tile(dhr, dhi, lr, -li, ar_ref, ai_ref, cr_ref, ci_ref, True)
        top = _rows((r, SW)) == 0
        dst_ref[0:1, :] = jnp.sum(jnp.where(top, ar, 0.0), axis=0, keepdims=True)
        dst_ref[1:2, :] = jnp.sum(jnp.where(top, ai, 0.0), axis=0, keepdims=True)
        du_ref[...] = (_mm_nt(ar, b_ref[0]) + _mm_nt(ai, b_ref[1]) + dus_ref[...]).astype(du_ref.dtype)
        dbr, dbi = _mm_tn(u, ar), _mm_tn(u, ai)
        pr = _sd(hr, 1) + jnp.where(top, car_ref[0:1, :], 0.0)
        pi = _sd(hi, 1) + jnp.where(top, car_ref[1:2, :], 0.0)
        dlr = jnp.sum(ar * pr + ai * pi, axis=0, keepdims=True)
        dli = jnp.sum(ai * pr - ar * pi, axis=0, keepdims=True)

        @pl.when(first)
        def _():
            dlam_ref[0] = dlr
            dlam_ref[1] = dli
            db_ref[0] = dbr
            db_ref[1] = dbi
            dc_ref[0] = dcr2
            dc_ref[1] = dci2

        @pl.when(jnp.logical_not(first))
        def _():
            dlam_ref[0] += dlr
            dlam_ref[1] += dli
            db_ref[0] += dbr
            db_ref[1] += dbi
            dc_ref[0] += dcr2
            dc_ref[1] += dci2

    grid = (NCB, nt)
    in_specs = [pl.BlockSpec((r, DH), lambda c, i: (nt - 1 - i, u0 + c)),
                pl.BlockSpec((2, 1, SW), lambda c, i: (0, 0, c)),
                pl.BlockSpec((2, None, DH, SW), lambda c, i: (0, c, 0, 0)),
                pl.BlockSpec((2, None, SW, DH), lambda c, i: (0, c, 0, 0)),
                pl.BlockSpec((None, 8, SW), lambda c, i: (nt - 1 - i, 0, c)),
                pl.BlockSpec((r, DH), lambda c, i: (nt - 1 - i, c)),
                pl.BlockSpec((r, DH), lambda c, i: (nt - 1 - i, c))]
    out_specs = [pl.BlockSpec((r, DH), lambda c, i: (nt - 1 - i, c)),
                 pl.BlockSpec((2, 1, SW), lambda c, i: (0, 0, c)),
                 pl.BlockSpec((2, None, DH, SW), lambda c, i: (0, c, 0, 0)),
                 pl.BlockSpec((2, None, SW, DH), lambda c, i: (0, c, 0, 0))]
    out_shape = [jax.ShapeDtypeStruct((t, D), GRAD_ACT), jax.ShapeDtypeStruct((2, 1, NG * NS), F32),
                 jax.ShapeDtypeStruct((2, NCB, DH, SW), F32), jax.ShapeDtypeStruct((2, NCB, SW, DH), F32)]
    scratch = ([pltpu.VMEM((8, SW), F32)] + [pltpu.VMEM((SW // DH, r, DH), F32)] * 4
               + [pltpu.VMEM((r // SUB, SW), F32)] * 2)
    args, sem = [proj, lam, bblk, cblk, carries, dy, du_skip], ("parallel", "arbitrary")
    if exchange is not None:
        body = _carry(body, len(args), len(out_shape), len(scratch), exchange, grid)
        in_specs, out_specs = in_specs + exchange.in_specs, out_specs + exchange.out_specs
        out_shape, scratch, args = out_shape + exchange.out_shape, scratch + exchange.scratch_shapes, args + exchange.srcs
        sem = ("arbitrary", "arbitrary")
    outs = pl.pallas_call(
        body, name="s5_bwd", grid=grid, in_specs=in_specs, out_specs=out_specs, out_shape=out_shape,
        scratch_shapes=scratch, compiler_params=_cparams(sem),
    )(*args)
    return outs[:4], outs[4:]


def _proj_spec(tm, col):
    return pl.BlockSpec((tm, D), lambda i: (i, col))


def _layer_mat(l):
    return pl.BlockSpec((None, D, D), lambda i: (l, 0, 0))


def _mix_fwd(proj, o, s5y, x, hn, dvec, wglu, bglu, wout, npost, l):
    t = x.shape[0]
    tm = _tile(t, 256)

    def body(za_ref, u_ref, zb_ref, ra_ref, rb_ref, o_ref, y_ref, x_ref, hn_ref, d_ref, wg_ref, bg_ref, wo_ref,
             np_ref, xn_ref):
        y0 = _mix_pre(y_ref[...], u_ref[...], d_ref[...])
        gl = _mm(y0, wg_ref[...]) + bg_ref[...]
        m = _mix_mid(o_ref[...], za_ref[...], y0, gl, zb_ref[...], ra_ref[...], rb_ref[...], hn_ref[...])
        out = _mm(m, wo_ref[...])
        xn_ref[...] = _mix_post(x_ref[...], out, np_ref[...])

    act = pl.BlockSpec((tm, D), lambda i: (i, 0))
    return pl.pallas_call(
        body, name="mix_fwd", grid=(t // tm,),
        in_specs=[_proj_spec(tm, 3), _proj_spec(tm, 4), _proj_spec(tm, 5), _proj_spec(tm, 6), _proj_spec(tm, 7),
                  act, act, act, _full((1, DH)), _full((1, D)), _layer_mat(l), _full((1, D)), _layer_mat(l),
                  _full((1, D))],
        out_specs=act,
        out_shape=jax.ShapeDtypeStruct((t, D), F32),
        compiler_params=_cparams(("parallel",)),
    )(proj, proj, proj, proj, proj, o, s5y, x, hn, dvec, wglu, bglu, wout, npost)


def _mix_bwd(proj, o, s5y, x, hn, dvec, wglu, bglu, wout, npost, dxn, l):
    t = x.shape[0]
    tm = _tile(t, 128)

    def body(za_ref, u_ref, zb_ref, ra_ref, rb_ref, o_ref, y_ref, x_ref, hn_ref, d_ref, wg_ref, bg_ref, wo_ref,
             np_ref, dxn_ref,
             dza_ref, du_ref, dzb_ref, dra_ref, drb_ref, do_ref, dy_ref, dx_ref,
             dwg_ref, dwo_ref, dvecs_ref, dhn_ref):
        y0, vjp_pre = jax.vjp(_mix_pre, y_ref[...], u_ref[...], d_ref[...])
        gl = _mm(y0, wg_ref[...]) + bg_ref[...]
        m, vjp_mid = jax.vjp(_mix_mid, o_ref[...], za_ref[...], y0, gl, zb_ref[...], ra_ref[...], rb_ref[...],
                             hn_ref[...])
        out = _mm(m, wo_ref[...])
        _, vjp_post = jax.vjp(_mix_post, x_ref[...], out, np_ref[...])
        dx, dout, dnp = vjp_post(dxn_ref[...])
        dm = _mm_nt(dout, wo_ref[...])
        dwo = _mm_tn(m, dout)
        do, dza, dy0, dgl, dzb, dra, drb, dhn = vjp_mid(dm)
        dwg = _mm_tn(y0, dgl)
        dbg = jnp.sum(dgl, axis=0, keepdims=True)
        dy0 = dy0 + _mm_nt(dgl, wg_ref[...])
        dy, du, dd = vjp_pre(dy0)
        dza_ref[...] = dza.astype(dza_ref.dtype)
        du_ref[...] = du
        dzb_ref[...] = dzb.astype(dzb_ref.dtype)
        dra_ref[...] = dra.astype(dra_ref.dtype)
        drb_ref[...] = drb.astype(drb_ref.dtype)
        do_ref[...] = do
        dy_ref[...] = dy
        dx_ref[...] = dx
        first = pl.program_id(0) == 0

        @pl.when(first)
        def _():
            dwg_ref[...] = dwg
            dwo_ref[...] = dwo
            dvecs_ref[...] = jnp.zeros_like(dvecs_ref)
            dhn_ref[...] = jnp.zeros_like(dhn_ref)

        @pl.when(jnp.logical_not(first))
        def _():
            dwg_ref[...] += dwg
            dwo_ref[...] += dwo

        dvecs_ref[0:1, :] += dd
        dvecs_ref[1:2, :] += dbg
        dvecs_ref[2:3, :] += dnp
        dhn_ref[0:1, :] += dhn

    act = pl.BlockSpec((tm, D), lambda i: (i, 0))
    a, ga = jax.ShapeDtypeStruct((t, D), F32), jax.ShapeDtypeStruct((t, D), GRAD_ACT)
    w = jax.ShapeDtypeStruct((D, D), F32)
    return pl.pallas_call(
        body, name="mix_bwd", grid=(t // tm,),
        in_specs=[_proj_spec(tm, 3), _proj_spec(tm, 4), _proj_spec(tm, 5), _proj_spec(tm, 6), _proj_spec(tm, 7),
                  act, act, act, _full((1, DH)), _full((1, D)), _layer_mat(l), _full((1, D)), _layer_mat(l),
                  _full((1, D)), act],
        out_specs=[act] * 8 + [_full((D, D)), _full((D, D)), _full((8, D)), _full((8, DH))],
        out_shape=[ga, a, ga, ga, ga, a, a, a, w, w, jax.ShapeDtypeStruct((8, D), F32),
                   jax.ShapeDtypeStruct((8, DH), F32)],
        compiler_params=_cparams(("arbitrary",)),
    )(proj, proj, proj, proj, proj, o, s5y, x, hn, dvec, wglu, bglu, wout, npost, dxn)


def _loss_grad(y, target):
    t = y.shape[0]
    tm = _tile(t, 512)

    def body(y_ref, t_ref, dy_ref, l_ref):
        err = y_ref[...] - t_ref[...]
        dy_ref[...] = err * (1.0 / D)
        part = jnp.sum(jnp.sum(err * err, axis=1, keepdims=True), axis=0, keepdims=True) * (0.5 / D)
        part = jnp.broadcast_to(part, (8, DH))

        @pl.when(pl.program_id(0) == 0)
        def _():
            l_ref[...] = part

        @pl.when(pl.program_id(0) > 0)
        def _():
            l_ref[...] += part

    act = pl.BlockSpec((tm, D), lambda i: (i, 0))
    return pl.pallas_call(
        body, name="loss_grad", grid=(t // tm,),
        in_specs=[act, act], out_specs=[act, _full((8, DH))],
        out_shape=[jax.ShapeDtypeStruct((t, D), F32), jax.ShapeDtypeStruct((8, DH), F32)],
        compiler_params=_cparams(("arbitrary",)),
    )(y, target)


def _flips(rel):
    x, y, c = lax.axis_index("x"), lax.axis_index("y"), lax.axis_index("c")
    fx, fy, fc = rel
    return (x ^ fx if fx else x, y ^ fy if fy else y, c ^ fc if fc else c)


CHIP_RELS = ((1, 0, 0), (0, 1, 0), (1, 1, 0))
ALL_RELS = tuple((fx, fy, fc) for fx in (0, 1) for fy in (0, 1) for fc in (0, 1) if (fx, fy, fc) != (0, 0, 0))


def _slot_of(pos, by_chip):
    px, py, pc = pos
    return 2 * px + py if by_chip else 4 * px + 2 * py + pc


class _Exchange:
    def __init__(self, srcs, rels, by_chip, scatter):
        self.srcs, self.rels, self.by_chip, self.scatter = list(srcs), rels, by_chip, scatter
        self.narr = len(self.srcs)
        nslot, nsem = NCHIP if by_chip else NDEV, self.narr * len(rels)
        self.in_specs = [pl.BlockSpec(memory_space=pl.ANY)] * self.narr
        self.out_specs = [pl.BlockSpec(memory_space=pl.ANY)] * self.narr
        self.out_shape = [jax.ShapeDtypeStruct((nslot,) + s.shape[-2:], s.dtype) for s in self.srcs]
        self.scratch_shapes = [pltpu.SemaphoreType.DMA((nsem,)), pltpu.SemaphoreType.DMA((nsem,)),
                               pltpu.SemaphoreType.DMA((self.narr,))]

    def _copies(self, src_refs, dst_refs, sems):
        send_sems, recv_sems, local_sems = sems
        my_slot = _slot_of(_flips((0, 0, 0)), self.by_chip)
        local, sends, arrivals = [], [], []
        for a, (src_ref, dst_ref) in enumerate(zip(src_refs, dst_refs)):
            local.append(pltpu.make_async_copy(src_ref.at[my_slot] if self.scatter else src_ref, dst_ref.at[my_slot],
                                               local_sems.at[a]))
            for k, rel in enumerate(self.rels):
                peer = _flips(rel)
                pair = dict(send_sem=send_sems.at[a * len(self.rels) + k], recv_sem=recv_sems.at[a * len(self.rels) + k],
                            device_id=peer, device_id_type=pl.DeviceIdType.MESH)
                part = src_ref.at[_slot_of(peer, self.by_chip)] if self.scatter else src_ref
                sends.append(pltpu.make_async_remote_copy(src_ref=part, dst_ref=dst_ref.at[my_slot], **pair))
                arrivals.append(pltpu.make_async_remote_copy(
                    src_ref=src_ref.at[0] if self.scatter else src_ref, dst_ref=dst_ref.at[_slot_of(peer, self.by_chip)],
                    **pair))
        return local, sends, arrivals

    def start(self, src_refs, dst_refs, sems):
        local, sends, _ = self._copies(src_refs, dst_refs, sems)
        for cp in local + sends:
            cp.start()

    def wait(self, src_refs, dst_refs, sems):
        local, sends, arrivals = self._copies(src_refs, dst_refs, sems)
        for cp in arrivals:
            cp.wait_recv()
        for cp in sends:
            cp.wait_send()
        for cp in local:
            cp.wait()


def _exchange(srcs, rels, by_chip, scatter, name):
    ex = _Exchange(srcs, rels, by_chip, scatter)

    def body(*refs):
        parts = refs[:ex.narr], refs[ex.narr:2 * ex.narr], refs[2 * ex.narr:]
        ex.start(*parts)
        ex.wait(*parts)

    return pl.pallas_call(body, name=name, in_specs=ex.in_specs, out_specs=ex.out_specs, out_shape=ex.out_shape,
                          scratch_shapes=ex.scratch_shapes)(*ex.srcs)


def _sibling_swap(srcs, name):
    narr = len(srcs)

    def body(*refs):
        src_refs, dst_refs = refs[:narr], refs[narr:2 * narr]
        send_sems, recv_sems = refs[2 * narr:]
        peer = _flips((0, 0, 1))
        copies = [pltpu.make_async_remote_copy(src_ref=s, dst_ref=d, send_sem=send_sems.at[a], recv_sem=recv_sems.at[a],
                                               device_id=peer, device_id_type=pl.DeviceIdType.MESH)
                  for a, (s, d) in enumerate(zip(src_refs, dst_refs))]
        for cp in copies:
            cp.start()
        for cp in copies:
            cp.wait()

    return pl.pallas_call(
        body, name=name,
        in_specs=[pl.BlockSpec(memory_space=pl.ANY)] * narr,
        out_specs=[pl.BlockSpec(memory_space=pl.ANY)] * narr,
        out_shape=[jax.ShapeDtypeStruct(s.shape, s.dtype) for s in srcs],
        scratch_shapes=[pltpu.SemaphoreType.DMA((narr,)), pltpu.SemaphoreType.DMA((narr,))],
    )(*srcs)


def _all_reduce(src, name):
    rows, cols = src.shape
    r = rows // NDEV
    nrel = len(ALL_RELS)

    def body(src_ref, out_ref, parts_ref, mine_ref, send_sems, recv_sems):
        my_slot = _slot_of(_flips((0, 0, 0)), False)

        def piece(ref, slot):
            return ref.at[pl.ds(pl.multiple_of(slot * r, 8), r), :]

        def copies(phase):
            out = []
            for k, rel in enumerate(ALL_RELS):
                peer = _flips(rel)
                pair = dict(send_sem=send_sems.at[phase * nrel + k], recv_sem=recv_sems.at[phase * nrel + k],
                            device_id=peer, device_id_type=pl.DeviceIdType.MESH)
                if phase == 0:
                    out.append(pltpu.make_async_remote_copy(src_ref=piece(src_ref, _slot_of(peer, False)),
                                                            dst_ref=parts_ref.at[my_slot], **pair))
                else:
                    out.append(pltpu.make_async_remote_copy(src_ref=mine_ref, dst_ref=piece(out_ref, my_slot), **pair))
            return out

        first = copies(0)
        for cp in first:
            cp.start()
        parts_ref[my_slot] = piece(src_ref, my_slot)[...]
        for cp in first:
            cp.wait_recv()
        acc = parts_ref[0]
        for s in range(1, NDEV):
            acc = acc + parts_ref[s]
        mine_ref[...] = acc
        second = copies(1)
        for cp in second:
            cp.start()
        piece(out_ref, my_slot)[...] = acc
        for cp in second:
            cp.wait_recv()
        for cp in first + second:
            cp.wait_send()

    return pl.pallas_call(
        body, name=name,
        in_specs=[pl.BlockSpec(memory_space=pltpu.VMEM)], out_specs=pl.BlockSpec(memory_space=pltpu.VMEM),
        out_shape=jax.ShapeDtypeStruct(src.shape, src.dtype),
        scratch_shapes=[pltpu.VMEM((NDEV, r, cols), src.dtype), pltpu.VMEM((r, cols), src.dtype),
                        pltpu.SemaphoreType.DMA((2 * nrel,)), pltpu.SemaphoreType.DMA((2 * nrel,))],
        compiler_params=pltpu.CompilerParams(vmem_limit_bytes=VMEM_LIMIT),
    )(src)


def _sum_slots(parts, name):
    ns, rows, cols = parts.shape
    tr = _row_tile(rows, 256)

    def body(p_ref, o_ref):
        acc = p_ref[0].astype(F32)
        for s in range(1, ns):
            acc = acc + p_ref[s].astype(F32)
        o_ref[...] = acc

    return pl.pallas_call(
        body, name=name, grid=(rows // tr,),
        in_specs=[pl.BlockSpec((ns, tr, cols), lambda i: (0, i, 0))],
        out_specs=pl.BlockSpec((tr, cols), lambda i: (i, 0)),
        out_shape=jax.ShapeDtypeStruct((rows, cols), F32),
        compiler_params=_cparams(("parallel",)),
    )(parts)


def _adamw(w, g_parts, m, v, name, max_rows=256):
    rows, cols = w.shape
    tr = _row_tile(rows, max_rows)
    c1 = 1.0 / (1.0 - ADAM_B1 ** ADAM_STEP)
    c2 = 1.0 / (1.0 - ADAM_B2 ** ADAM_STEP)
    npart = len(g_parts)

    def body(*refs):
        w_ref, m_ref, v_ref = refs[:3]
        g_refs = refs[3:3 + npart]
        go_ref, d_ref, nm_ref, nv_ref = refs[3 + npart:]
        terms = []
        for g_ref in g_refs:
            terms += [g_ref[...]] if len(g_ref.shape) == 2 else [g_ref[s] for s in range(g_ref.shape[0])]
        g = terms[0]
        for term in terms[1:]:
            g = g + term
        nm = ADAM_B1 * m_ref[...] + (1.0 - ADAM_B1) * g
        nv = ADAM_B2 * v_ref[...] + (1.0 - ADAM_B2) * (g * g)
        d_ref[...] = -ADAM_LR * ((nm * c1) / (jnp.sqrt(nv * c2) + ADAM_EPS) + ADAM_WD * w_ref[...])
        go_ref[...] = g
        nm_ref[...] = nm
        nv_ref[...] = nv

    blk = pl.BlockSpec((tr, cols), lambda i: (i, 0))
    g_specs = [blk if p.ndim == 2 else pl.BlockSpec((p.shape[0], tr, cols), lambda i: (0, i, 0)) for p in g_parts]
    out = jax.ShapeDtypeStruct((rows, cols), F32)
    return pl.pallas_call(
        body, name=name, grid=(rows // tr,),
        in_specs=[blk, blk, blk] + g_specs,
        out_specs=[blk] * 4, out_shape=[out] * 4,
        compiler_params=_cparams(("parallel",)),
    )(w, m, v, *g_parts)


WIN_SHARD = 2052
CONV_SHARD = 768
ROW_SHARD = 256

SMALL = (("norm_pre", (DEPTH, D)), ("a_log", (DEPTH, NH)), ("dt_bias", (DEPTH, NH)), ("head_norm", (DEPTH, DH)),
         ("ssm_a_re", (DEPTH, NG, NS)), ("ssm_a_im", (DEPTH, NG, NS)), ("ssm_log_dt", (DEPTH, NG)),
         ("ssm_b_re", (DEPTH, NG, NS, GS)), ("ssm_b_im", (DEPTH, NG, NS, GS)),
         ("ssm_c_re", (DEPTH, NG, GS, NS)), ("ssm_c_im", (DEPTH, NG, GS, NS)), ("ssm_d", (DEPTH, D)),
         ("b_glu", (DEPTH, D)), ("norm_post", (DEPTH, D)))


def _pad_rows(flat, rows):
    return jnp.pad(flat, (0, rows * D - flat.shape[0])).reshape(rows, D)


def _rows_by_chip(a):
    nl, rows, cols = a.shape
    return a.reshape(nl, NCHIP, rows // NCHIP, cols).transpose(1, 0, 2, 3).reshape(NCHIP, -1, cols)


def _rows_from_chips(a):
    _, rows, cols = a.shape
    return a.reshape(NCHIP, DEPTH, rows // DEPTH, cols).transpose(1, 0, 2, 3).reshape(DEPTH, -1, cols)


def _cols_by_chip(a):
    nl, rows, cols = a.shape
    return a.reshape(nl, rows, NCHIP, cols // NCHIP).transpose(2, 0, 1, 3).reshape(NCHIP, nl * rows, -1)


def _cols_from_chips(a, nl):
    _, rows, cols = a.shape
    return a.reshape(NCHIP, nl, rows // nl, cols).transpose(1, 2, 0, 3).reshape(nl, rows // nl, NCHIP * cols)


SMALL_ROWS = sum(-(-math.prod(s) // (8 * D)) * 8 for _, s in SMALL)
CONV_ROWS = DEPTH * 4 * 3 * D // D


def _pack_small(vals, extra=()):
    parts = []
    for val in tuple(vals) + tuple(extra):
        n = val.size
        parts.append(_pad_rows(val.reshape(-1), -(-n // (8 * D)) * 8))
    return jnp.concatenate(parts, axis=0)


def _unpack_small(flat):
    outs, r0 = [], 0
    for _, shape in SMALL:
        n = math.prod(shape)
        rows = -(-n // (8 * D)) * 8
        outs.append(flat[r0:r0 + rows].reshape(-1)[:n].reshape(shape))
        r0 += rows
    return outs


def _rearrange_cols(w):
    pad = jnp.zeros(w.shape[:-1] + (NCOL - BD0 - 2 * NH,), w.dtype)
    return jnp.concatenate([w[..., :4 * D], w[..., 4 * D + 2 * NH:], w[..., 4 * D:4 * D + 2 * NH], pad], axis=-1)


def _restore_cols(w):
    return jnp.concatenate([w[..., :4 * D], w[..., BD0:BD0 + 2 * NH], w[..., 4 * D:BD0]], axis=-1)


def _block_diag_b(bb2):
    b = bb2.reshape(-1, NCB, GPB, NS, GS)
    return jnp.einsum("lkgnc,gh->lkgchn", b, jnp.eye(GPB, dtype=F32)).reshape(-1, NCB, GPB * GS, SW)


def _block_diag_b_t(d):
    blocks = jnp.einsum("lkgchn,gh->lkgnc", d.reshape(-1, NCB, GPB, GS, GPB, NS), jnp.eye(GPB, dtype=F32))
    return blocks.reshape(-1, NG, NS * GS)


def _block_diag_c(c):
    blocks = jnp.einsum("lkgcn,gh->lkgnhc", c.reshape(-1, NCB, GPB, GS, NS), jnp.eye(GPB, dtype=F32))
    return blocks.reshape(-1, NCB, SW, GPB * GS)


def _block_diag_c_t(d):
    blocks = jnp.einsum("lkgnhc,gh->lkgcn", d.reshape(-1, NCB, GPB, NS, GPB, GS), jnp.eye(GPB, dtype=F32))
    return blocks.reshape(-1, NG, GS, NS)


def _local_step(x, target, weights, conv, small, comm=None):
    weights = list(weights) + [None] * (DEPTH - len(weights))
    ar, ai = small["ssm_a_re"], small["ssm_a_im"]
    ldt = small["ssm_log_dt"].reshape(DEPTH, NG, 1)
    br2 = small["ssm_b_re"].reshape(DEPTH, NG, NS * GS)
    bi2 = small["ssm_b_im"].reshape(DEPTH, NG, NS * GS)
    lr, li, bbr2, bbi2 = _s5_params(ar, ai, ldt, br2, bi2)

    def row(name, l, width):
        return small[name][l].reshape(1, width)

    gvecs = jnp.pad(jnp.stack([small["a_log"], small["dt_bias"]], axis=1), ((0, 0), (0, 6), (NH, DH - 2 * NH)))
    lams = jnp.stack([lr.reshape(DEPTH, 1, NG * NS), li.reshape(DEPTH, 1, NG * NS)], axis=1)
    bblks = jnp.stack([_block_diag_b(bbr2), _block_diag_b(bbi2)], axis=1)
    cblks = jnp.stack([_block_diag_c(small["ssm_c_re"]), _block_diag_c(small["ssm_c_im"])], axis=1)
    saved = []
    for l in range(DEPTH):
        gvec, lam, bblk, cblk = gvecs[l], lams[l], bblks[l], cblks[l]
        wcat, wglu, wout = weights[l]
        proj, h = _inproj_fwd(x, row("norm_pre", l, D), wcat, 0)
        qkv = _prep_fwd(proj, conv[l])
        bb, gcb = _gates_fwd(proj, gvec)
        fetch = _Exchange(comm["weight_parts"](l + 1), CHIP_RELS, True, False) if comm and l + 1 < DEPTH else None
        local, t_inv, fetched = _delta_local_fwd(qkv, bb, gcb, fetch)
        if fetch is not None:
            weights[l + 1] = comm["weights_from"](fetched)
        o, states = _delta_state_fwd(local, gcb)
        s5y, carries = _s5_fwd(proj, lam, bblk, cblk)
        xn = _mix_fwd(proj, o, s5y, x, row("head_norm", l, DH), row("ssm_d", l, D), wglu, row("b_glu", l, D),
                      wout, row("norm_post", l, D), 0)
        saved.append((x, proj, h, qkv, bb, gcb, local, t_inv, o, states, s5y, carries, gvec, lam, bblk, cblk))
        x = xn

    dx, loss_part = _loss_grad(x, target)

    g = {k: [None] * DEPTH for k in ("wcat", "conv", "wglu", "wout", "norm_pre", "a_log", "dt_bias", "head_norm",
                                     "ssm_d", "b_glu", "norm_post", "dlam", "dbblk", "dcblk")}
    from_chips, send, send_layer = [None] * DEPTH, None, None
    for l in reversed(range(DEPTH)):
        xl, proj, h, qkv, bb, gcb, local, t_inv, o, states, s5y, carries, gvec, lam, bblk, cblk = saved[l]
        wcat, wglu, wout = weights[l]
        (dza, du_skip, dzb, dra, drb, do, ds5y, dxres, dwg, dwo, dvecs, dhn) = _mix_bwd(
            proj, o, s5y, xl, row("head_norm", l, DH), row("ssm_d", l, D), wglu, row("b_glu", l, D), wout,
            row("norm_post", l, D), dx, 0)
        (du, dlam, dbblk, dcblk), arrived = _s5_bwd(proj, lam, bblk, cblk, carries, ds5y, du_skip, send)
        if send is not None:
            from_chips[send_layer] = arrived
        *dlocal, dgcb_state = _delta_state_bwd(local, gcb, states, do)
        dq, dk, dv, dbb, dgcb = _delta_local_bwd(qkv, bb, gcb, t_inv, dlocal, dgcb_state)
        dbd, dgvec = _gates_bwd(proj, gvec, dbb, dgcb)
        dpre, dconv = _prep_bwd(proj, conv[l], dq, dk, dv)
        dproj = jnp.concatenate([dpre, dza, du, dzb, dra, drb, dbd], axis=1)
        g["wcat"][l] = _inproj_bwd_dw(h, dproj)
        g["conv"][l], g["wglu"][l], g["wout"][l] = dconv, dwg, dwo
        send = _Exchange(comm["grad_parts"](g["wcat"][l], dwg, dwo), CHIP_RELS, True, True) if comm else None
        dx, dgain, arrived = _inproj_bwd_dx(dproj, wcat, xl, row("norm_pre", l, D), dxres, 0, send if l == 0 else None)
        if comm and l == 0:
            from_chips[l] = arrived
        send_layer = l
        g["norm_pre"][l] = dgain[0]
        g["a_log"][l], g["dt_bias"][l] = dgvec[0, NH:2 * NH], dgvec[1, NH:2 * NH]
        g["head_norm"][l] = dhn[0]
        g["ssm_d"][l], g["b_glu"][l], g["norm_post"][l] = dvecs[0], dvecs[1], dvecs[2]
        g["dlam"][l], g["dbblk"][l], g["dcblk"][l] = dlam, dbblk, dcblk
    if comm:
        for k in ("wcat", "wglu", "wout"):
            del g[k]
    g = {k: jnp.stack(v) for k, v in g.items()}
    g["from_chips"] = from_chips
    dlam, dbblk, dcblk = g.pop("dlam"), g.pop("dbblk"), g.pop("dcblk")
    g["ssm_c_re"], g["ssm_c_im"] = _block_diag_c_t(dcblk[:, 0]), _block_diag_c_t(dcblk[:, 1])
    dar, dai, dldt, dbr2, dbi2 = _s5_params_bwd(
        ar, ai, ldt, br2, bi2, dlam[:, 0].reshape(DEPTH, NG, NS), dlam[:, 1].reshape(DEPTH, NG, NS),
        _block_diag_b_t(dbblk[:, 0]), _block_diag_b_t(dbblk[:, 1]))
    g["ssm_a_re"], g["ssm_a_im"], g["ssm_log_dt"] = dar, dai, dldt.reshape(DEPTH, NG)
    g["ssm_b_re"] = dbr2.reshape(DEPTH, NG, NS, GS)
    g["ssm_b_im"] = dbi2.reshape(DEPTH, NG, NS, GS)
    return loss_part[0, 0], dx, g


def kernel(x, norm_pre, w_in, conv_w, a_log, dt_bias, head_norm, ssm_a_re, ssm_a_im, ssm_log_dt, ssm_b_re, ssm_b_im, ssm_c_re, ssm_c_im, ssm_d, w_glu, b_glu, w_out, norm_post, loss_target, m_norm_pre, m_w_in, m_conv_w, m_a_log, m_dt_bias, m_head_norm, m_ssm_a_re, m_ssm_a_im, m_ssm_log_dt, m_ssm_b_re, m_ssm_b_im, m_ssm_c_re, m_ssm_c_im, m_ssm_d, m_w_glu, m_b_glu, m_w_out, m_norm_post, v_norm_pre, v_w_in, v_conv_w, v_a_log, v_dt_bias, v_head_norm, v_ssm_a_re, v_ssm_a_im, v_ssm_log_dt, v_ssm_b_re, v_ssm_b_im, v_ssm_c_re, v_ssm_c_im, v_ssm_d, v_w_glu, v_b_glu, v_w_out, v_norm_post):
    args = dict(locals())
    small = {n: args[n] for n, _ in SMALL}

    def flat2(a):
        return a.reshape(-1, a.shape[-1])

    w_in16, w_glu16, w_out16 = w_in.astype(BF16), w_glu.astype(BF16), w_out.astype(BF16)

    def weight_parts(l):
        return [w_in16[l], w_glu16[l], w_out16[l]]

    def weights_from(parts):
        g_in, g_glu, g_out = parts[:3]
        return (_rearrange_cols(_cols_from_chips(g_in, 1)), g_glu.reshape(1, D, D), g_out.reshape(1, D, D))

    def grad_parts(gwcat, gwglu, gwout):
        return [_cols_by_chip(_restore_cols(gwcat[None])).astype(BF16), gwglu.reshape(NCHIP, ROW_SHARD, D).astype(BF16),
                gwout.reshape(NCHIP, ROW_SHARD, D).astype(BF16)]

    first = _exchange(weight_parts(0) + [flat2(conv_w)], CHIP_RELS, True, False, "gather_weights")
    conv = _cols_from_chips(first[3], DEPTH)
    comm = dict(weight_parts=weight_parts, weights_from=weights_from, grad_parts=grad_parts)
    loss_part, dx, g = _local_step(x[0], loss_target[0], [weights_from(first)], conv, small, comm)
    loss = lax.psum(loss_part, ("x", "y", "c"))

    from_chips = [jnp.concatenate([g["from_chips"][l][a] for l in range(DEPTH)], axis=1) for a in range(3)]
    core_sums = [_sum_slots(p, "sum_chips_" + n) for p, n in zip(from_chips, ("in", "glu", "out"))]
    others = _sibling_swap(core_sums, "swap_cores")
    sharded = {}
    for n, mine, other in zip(("w_in", "w_glu", "w_out"), core_sums, others):
        sharded[n] = _adamw(flat2(args[n]), [mine, other], flat2(args["m_" + n]), flat2(args["v_" + n]), "adamw_" + n,
                            max_rows=128)

    pad = jnp.zeros(((-(SMALL_ROWS + CONV_ROWS)) % (8 * NDEV), D), F32)
    small_sum = _all_reduce(_pack_small([g[n] for n, _ in SMALL], extra=[g["conv"], pad]), "reduce_small")
    small_out = _adamw(_pack_small([args[n] for n, _ in SMALL]), [small_sum],
                       _pack_small([args["m_" + n] for n, _ in SMALL]),
                       _pack_small([args["v_" + n] for n, _ in SMALL]), "adamw_small")
    chip = 2 * lax.axis_index("x") + lax.axis_index("y")
    conv_sum = small_sum[SMALL_ROWS:SMALL_ROWS + CONV_ROWS].reshape(DEPTH * 4, 3 * D)
    conv_sum = lax.dynamic_slice_in_dim(conv_sum, chip * CONV_SHARD, CONV_SHARD, axis=1)
    sharded["conv_w"] = _adamw(flat2(conv_w), [conv_sum], flat2(m_conv_w), flat2(v_conv_w), "adamw_conv")

    names = ["norm_pre", "w_in", "conv_w", "a_log", "dt_bias", "head_norm", "ssm_a_re", "ssm_a_im", "ssm_log_dt",
             "ssm_b_re", "ssm_b_im", "ssm_c_re", "ssm_c_im", "ssm_d", "w_glu", "b_glu", "w_out", "norm_post"]
    outs = [loss, dx[None]]
    for i in range(4):
        sm = dict(zip([n for n, _ in SMALL], _unpack_small(small_out[i])))
        outs += [sharded[n][i].reshape(args[n].shape) if n in sharded else sm[n] for n in names]
    return tuple(outs)
```

```python
import functools
import math

import jax
import jax.numpy as jnp
from jax import lax
from jax.experimental import pallas as pl
from jax.experimental.pallas import tpu as pltpu

F32 = jnp.float32
BF16 = jnp.bfloat16
HI = lax.Precision.HIGHEST

D = 1024
NH = 8
DH = 128
CH = 128
NG = 64
GS = 16
NS = 64
GPB = 8
NCB = NG // GPB
SW = GPB * NS
NCOL = 8320
BD0 = 8192
EPS = 1e-6
DEPTH = 4
NCHIP = 4
NDEV = 8
VMEM_LIMIT = 56 * 1024 * 1024
GRAD_ACT = jnp.bfloat16

ADAM_LR = 0.001
ADAM_B1 = 0.9
ADAM_B2 = 0.999
ADAM_EPS = 1e-08
ADAM_WD = 0.01
ADAM_STEP = 10


def _cparams(sem=None):
    return pltpu.CompilerParams(dimension_semantics=sem, vmem_limit_bytes=VMEM_LIMIT)


def _full(shape):
    nd = len(shape)
    return pl.BlockSpec(shape, lambda *_: (0,) * nd)


def _rms(x, gain):
    ms = jnp.mean(x * x, axis=-1, keepdims=True)
    return x * lax.rsqrt(ms + EPS) * gain


def _sigmoid(x):
    return 1.0 / (1.0 + jnp.exp(-x))


def _silu(x):
    return x * _sigmoid(x)


def _softplus(x):
    return jnp.maximum(x, 0.0) + jnp.log(1.0 + jnp.exp(-jnp.abs(x)))


def _gelu(x):
    return 0.5 * x * (1.0 + jnp.tanh(math.sqrt(2.0 / math.pi) * (x + 0.044715 * (x * x * x))))


def _dot_bf16(a, b, dims):
    return lax.dot_general(a.astype(BF16), b.astype(BF16), (dims, ((), ())), preferred_element_type=F32)


def _mm_nt(a, b):
    return _dot_bf16(a, b, ((1,), (1,)))


def _mm_tn(a, b):
    return _dot_bf16(a, b, ((0,), (0,)))


@jax.custom_vjp
def _mm(a, b):
    return _dot_bf16(a, b, ((1,), (0,)))


def _mm_fwd(a, b):
    return _dot_bf16(a, b, ((1,), (0,))), (a, b)


def _mm_bwd(res, ct):
    a, b = res
    return _mm_nt(ct, b).astype(a.dtype), _mm_tn(a, ct).astype(b.dtype)


_mm.defvjp(_mm_fwd, _mm_bwd)


@jax.custom_vjp
def _mm_nt_d(a, b):
    return _mm_nt(a, b)


def _mm_nt_d_bwd(res, ct):
    a, b = res
    return _dot_bf16(ct, b, ((1,), (0,))), _mm_tn(ct, a)


_mm_nt_d.defvjp(lambda a, b: (_mm_nt(a, b), (a, b)), _mm_nt_d_bwd)


@jax.custom_vjp
def _mm_tn_d(a, b):
    return _mm_tn(a, b)


def _mm_tn_d_bwd(res, ct):
    a, b = res
    return _mm_nt(b, ct), _dot_bf16(a, ct, ((1,), (0,)))


_mm_tn_d.defvjp(lambda a, b: (_mm_tn(a, b), (a, b)), _mm_tn_d_bwd)


def _split_bf16(a):
    hi = a.astype(BF16)
    return hi, (a - hi.astype(F32)).astype(BF16)


def _dot3(a, b, dims):
    ah, al = _split_bf16(a)
    bh, bl = _split_bf16(b)

    def dot(x, y):
        return lax.dot_general(x, y, (dims, ((), ())), preferred_element_type=F32)

    return dot(ah, bh) + (dot(ah, bl) + dot(al, bh))


@jax.custom_vjp
def _imm(a, b):
    return _dot3(a, b, ((1,), (0,)))


def _imm_bwd(res, ct):
    a, b = res
    return _dot3(ct, b, ((1,), (1,))), _dot3(a, ct, ((0,), (0,)))


_imm.defvjp(lambda a, b: (_dot3(a, b, ((1,), (0,))), (a, b)), _imm_bwd)


def _hmm(a, b):
    return jnp.dot(a, b, precision=HI, preferred_element_type=F32)


def _hmm_nt(a, b):
    return lax.dot_general(a, b, (((1,), (1,)), ((), ())), precision=HI, preferred_element_type=F32)


def _hmm_tn(a, b):
    return lax.dot_general(a, b, (((0,), (0,)), ((), ())), precision=HI, preferred_element_type=F32)


def _rows(shape):
    return lax.broadcasted_iota(jnp.int32, shape, 0)


def _cols(shape):
    return lax.broadcasted_iota(jnp.int32, shape, 1)


def _sd(x, s):
    return jnp.where(_rows(x.shape) >= s, pltpu.roll(x, s, axis=0), 0.0)


def _su(x, s):
    n = x.shape[0]
    return jnp.where(_rows(x.shape) < n - s, pltpu.roll(x, n - s, axis=0), 0.0)


@functools.partial(jax.custom_vjp, nondiff_argnums=(1,))
def _shift_down(x, s):
    return _sd(x, s)


def _shift_down_fwd(x, s):
    return _sd(x, s), None


def _shift_down_bwd(s, _, g):
    return (_su(g, s),)


_shift_down.defvjp(_shift_down_fwd, _shift_down_bwd)


def _last_row(x):
    n = x.shape[0]
    return jnp.sum(jnp.where(_rows(x.shape) == n - 1, x, 0.0), axis=0, keepdims=True)


def _prep_fn(p, w0, w1, w2, w3, qk):
    acc = w3 * p + w2 * _shift_down(p, 1) + w1 * _shift_down(p, 2) + w0 * _shift_down(p, 3)
    a = _silu(acc)
    nrm = lax.rsqrt(jnp.sum(a * a, axis=-1, keepdims=True) + EPS)
    return a * (nrm * qk + (1.0 - qk))


def _gates_fn(bd, av, bv):
    tm = bd.shape[0]
    beta_all = _sigmoid(bd)
    g_all = -jnp.exp(av) * _softplus(bd + bv)
    r, c = _rows((tm, tm)), _cols((tm, tm))
    tri = jnp.where((r // CH == c // CH) & (r >= c), 1.0, 0.0).astype(F32)
    gc_all = _hmm(tri, g_all)
    lane = _cols(bd.shape)
    outs = []
    for h in range(NH):
        b = jnp.sum(jnp.where(lane == h, beta_all, 0.0), axis=1, keepdims=True)
        outs.append(jnp.broadcast_to(b, bd.shape))
    for h in range(NH):
        g = jnp.sum(jnp.where(lane == NH + h, gc_all, 0.0), axis=1, keepdims=True)
        outs.append(jnp.broadcast_to(g, bd.shape))
    return tuple(outs)


INV_BASE = 2


def _merge_mm(a, b):
    return _dot_bf16(a, b, ((1,), (0,)))


def _unit_lower_inv(l_mats):
    n = l_mats[0].shape[0]
    ii, jj = _rows((n, n)), _cols((n, n))
    base = ii // INV_BASE == jj // INV_BASE
    ps = [-jnp.where(base, l_mat, 0.0) for l_mat in l_mats]
    eye = jnp.where(ii == jj, 1.0, 0.0).astype(F32)
    ds = [eye + p for p in ps]
    k = 1
    while 2 * k < INV_BASE:
        ps = [_imm(p, p) for p in ps]
        ds = [d + _imm(d, p) for d, p in zip(ds, ps)]
        k *= 2
    m = INV_BASE
    while m < n:
        pair = (ii // (2 * m) == jj // (2 * m)) & (ii // m > jj // m)
        des = [_merge_mm(d, jnp.where(pair, l_mat, 0.0)) for d, l_mat in zip(ds, l_mats)]
        ds = [d - _merge_mm(de, d) for d, de in zip(ds, des)]
        m *= 2
    return ds


@jax.custom_vjp
def _known_inverse(l_mat, t_inv):
    return t_inv


def _known_inverse_bwd(t_inv, ct):
    d_l = -_dot3(_dot3(t_inv, ct, ((0,), (0,))), t_inv, ((1,), (1,)))
    return d_l, jnp.zeros_like(t_inv)


_known_inverse.defvjp(lambda l_mat, t_inv: (t_inv, t_inv), _known_inverse_bwd)


def _chunk_system(q, k, v, bb, gcb):
    qs = q * (DH ** -0.5)
    kb = k * bb
    eg = jnp.exp(gcb)
    ii, jj = _rows((CH, CH)), _cols((CH, CH))
    decay = jnp.exp(jnp.where(ii >= jj, gcb - gcb.T, -1e30))
    l_mat = jnp.where(ii > jj, _mm_nt_d(kb, k) * decay, 0.0)
    a_qk = _mm_nt_d(qs, k) * decay
    k_dec = k * jnp.exp(_last_row(gcb) - gcb)
    return l_mat, (v * bb, kb * eg, qs * eg, k_dec, a_qk)


def _chunk_solve(t_inv, rest):
    vb, kbe, q_dec, k_dec, a_qk = rest
    return _mm(t_inv, vb), _mm(t_inv, kbe), q_dec, k_dec, a_qk


def _side_by_side_vjp(fn, items, cts):
    n = len(items[0])
    _, vjp = jax.vjp(lambda *flat: fn([flat[i * n:(i + 1) * n] for i in range(len(items))]),
                     *[a for item in items for a in item])
    grads = vjp(cts)
    return [grads[i * n:(i + 1) * n] for i in range(len(items))]


def _chunks_local_known(items):
    systems = [_chunk_system(*item[:5]) for item in items]
    t_invs = [_known_inverse(l_mat, item[5]) for (l_mat, _), item in zip(systems, items)]
    return [_chunk_solve(t_inv, rest) for t_inv, (_, rest) in zip(t_invs, systems)]


def _chunks_local(chunks):
    systems = [_chunk_system(*c) for c in chunks]
    t_invs = _unit_lower_inv([l_mat for l_mat, _ in systems])
    return [(_chunk_solve(t_inv, rest), t_inv) for t_inv, (_, rest) in zip(t_invs, systems)]


def _state_steps(items):
    v_news = [u - _mm(w, state) for u, w, _, _, _, _, state in items]
    outs = [_mm(q_dec, state) + _mm(a_qk, v_new) for (_, _, q_dec, _, a_qk, _, state), v_new in zip(items, v_news)]
    states = [state * jnp.exp(_last_row(gcb)) + _mm_tn_d(k_dec, v_new)
              for (_, _, _, k_dec, _, gcb, state), v_new in zip(items, v_news)]
    return list(zip(outs, states))


SUB = 8


def _cmul(ar, ai, br, bi):
    return ar * br - ai * bi, ar * bi + ai * br


def _scan_tile(xr, xi, mr, mi, hr_ref, hi_ref, cr_ref, ci_ref, reverse):
    n, width = xr.shape
    ngroups = n // SUB
    shift_groups = _su if reverse else _sd
    xr, xi = xr.reshape(ngroups, SUB, width), xi.reshape(ngroups, SUB, width)
    pr, pi = mr, mi
    tr, ti = jnp.broadcast_to(mr, (SUB, width)), jnp.broadcast_to(mi, (SUB, width))
    pos = _rows(tr.shape)
    s = 1
    while s < SUB:
        inside = pos < SUB - s if reverse else pos >= s
        shift = SUB - s if reverse else s
        qr, qi = jnp.where(inside, pr, 0.0)[None], jnp.where(inside, pi, 0.0)[None]
        dr, di = _cmul(qr, qi, pltpu.roll(xr, shift, axis=1), pltpu.roll(xi, shift, axis=1))
        xr, xi = xr + dr, xi + di
        er = jnp.where(inside, pltpu.roll(tr, shift, axis=0), 1.0)
        ei = jnp.where(inside, pltpu.roll(ti, shift, axis=0), 0.0)
        tr, ti = _cmul(tr, ti, er, ei)
        pr, pi = _cmul(pr, pi, pr, pi)
        s *= 2
    xr, xi = xr.reshape(n, width), xi.reshape(n, width)
    nlb = width // DH

    def lanes(x, j):
        return x[:, j * DH:(j + 1) * DH]

    for j in range(nlb):
        hr_ref[j] = lanes(xr, j)
        hi_ref[j] = lanes(xi, j)
    edge = pl.ds(0 if reverse else SUB - 1, ngroups, stride=SUB)
    gr = jnp.concatenate([hr_ref.at[j][edge, :] for j in range(nlb)], axis=1)
    gi = jnp.concatenate([hi_ref.at[j][edge, :] for j in range(nlb)], axis=1)
    s = 1
    while s < ngroups:
        dr, di = _cmul(pr, pi, shift_groups(gr, s), shift_groups(gi, s))
        gr, gi = gr + dr, gi + di
        pr, pi = _cmul(pr, pi, pr, pi)
        s *= 2
    cr_ref[...] = shift_groups(gr, 1)
    ci_ref[...] = shift_groups(gi, 1)
    for g in range(ngroups):
        rows = slice(g * SUB, (g + 1) * SUB)
        dr, di = _cmul(tr, ti, cr_ref[g:g + 1, :], ci_ref[g:g + 1, :])
        for j in range(nlb):
            hr_ref[j, rows, :] += lanes(dr, j)
            hi_ref[j, rows, :] += lanes(di, j)
    return (jnp.concatenate([hr_ref[j] for j in range(nlb)], axis=1),
            jnp.concatenate([hi_ref[j] for j in range(nlb)], axis=1))


def _s5_states(u, lam_ref, b_ref, car_ref, hr_ref, hi_ref, cr_ref, ci_ref):
    lr, li = lam_ref[0], lam_ref[1]
    first = _rows((u.shape[0], SW)) == 0
    inr, ini = _cmul(lr, li, car_ref[0:1, :], car_ref[1:2, :])
    xr = _mm(u, b_ref[0]) + jnp.where(first, inr, 0.0)
    xi = _mm(u, b_ref[1]) + jnp.where(first, ini, 0.0)
    return _scan_tile(xr, xi, lr, li, hr_ref, hi_ref, cr_ref, ci_ref, False)


def _s5_params_fn(ar, ai, ldt, br2, bi2):
    dt = jnp.exp(ldt)
    mag = jnp.exp(ar * dt)
    lr, li = mag * jnp.cos(ai * dt), mag * jnp.sin(ai * dt)
    den = ar * ar + ai * ai
    fr = ((lr - 1.0) * ar + li * ai) / den
    fi = (li * ar - (lr - 1.0) * ai) / den
    expand = jnp.where(_cols((NS, NS * GS)) // GS == _rows((NS, NS * GS)), 1.0, 0.0).astype(F32)
    fr2, fi2 = _hmm(fr, expand), _hmm(fi, expand)
    return lr, li, fr2 * br2 - fi2 * bi2, fr2 * bi2 + fi2 * br2


def _head_norm(o, hn):
    parts = []
    for h in range(NH):
        oh = o[:, h * DH:(h + 1) * DH]
        parts.append(oh * lax.rsqrt(jnp.mean(oh * oh, axis=-1, keepdims=True) + EPS) * hn)
    return jnp.concatenate(parts, axis=1)


def _mix_pre(s5y, u, dvec):
    return _gelu(s5y + dvec * u)


def _mix_mid(o, za, y0, gl, zb, ra, rb, hn):
    ya = _head_norm(o, hn) * _silu(za)
    yb = y0 * _sigmoid(gl) * _silu(zb)
    return _sigmoid(ra) * ya + _sigmoid(rb) * yb


def _mix_post(x, out, npost):
    return x + _rms(out, npost)


def _tile(t, want):
    return min(t, want)


def _row_tile(rows, want):
    return max(r for r in range(16, want + 1, 16) if rows % r == 0)


def _inproj_fwd(x, gain, wcat, l):
    t = x.shape[0]
    tm, tn = _tile(t, 1024), 640

    def body(x_ref, g_ref, w_ref, o_ref, h_ref):
        @pl.when(pl.program_id(1) == 0)
        def _():
            h_ref[...] = _rms(x_ref[...], g_ref[...]).astype(h_ref.dtype)
        o_ref[...] = _dot_bf16(h_ref[...], w_ref[...], ((1,), (0,)))

    return pl.pallas_call(
        body, name="inproj_fwd", grid=(t // tm, NCOL // tn),
        in_specs=[pl.BlockSpec((tm, D), lambda i, j: (i, 0)), _full((1, D)),
                  pl.BlockSpec((None, D, tn), lambda i, j: (l, 0, j))],
        out_specs=[pl.BlockSpec((tm, tn), lambda i, j: (i, j)), pl.BlockSpec((tm, D), lambda i, j: (i, 0))],
        out_shape=[jax.ShapeDtypeStruct((t, NCOL), F32), jax.ShapeDtypeStruct((t, D), wcat.dtype)],
        compiler_params=_cparams(("parallel", "arbitrary")),
    )(x, gain, wcat)


def _inproj_bwd_dx(dproj, wcat, x, gain, dxres, l, exchange=None):
    t = x.shape[0]
    tm, tk = _tile(t, 1024), 640
    nk = NCOL // tk

    def body(dp_ref, w_ref, x_ref, g_ref, r_ref, dx_ref, dg_ref, acc_ref):
        i, k = pl.program_id(0), pl.program_id(1)

        @pl.when(k == 0)
        def _():
            acc_ref[...] = jnp.zeros_like(acc_ref)

        acc_ref[...] += _mm_nt(dp_ref[...], w_ref[...])

        @pl.when(k == nk - 1)
        def _():
            _, vjp = jax.vjp(_rms, x_ref[...], g_ref[...])
            dx, dg = vjp(acc_ref[...])
            dx_ref[...] = r_ref[...] + dx

            @pl.when(i == 0)
            def _():
                dg_ref[...] = dg

            @pl.when(i > 0)
            def _():
                dg_ref[...] += dg

    grid = (t // tm, nk)
    in_specs = [pl.BlockSpec((tm, tk), lambda i, k: (i, k)), pl.BlockSpec((None, D, tk), lambda i, k: (l, 0, k)),
                pl.BlockSpec((tm, D), lambda i, k: (i, 0)), _full((1, D)), pl.BlockSpec((tm, D), lambda i, k: (i, 0))]
    out_specs = [pl.BlockSpec((tm, D), lambda i, k: (i, 0)), _full((1, D))]
    out_shape = [jax.ShapeDtypeStruct((t, D), F32), jax.ShapeDtypeStruct((1, D), F32)]
    scratch, args = [pltpu.VMEM((tm, D), F32)], [dproj, wcat, x, gain, dxres]
    if exchange is not None:
        body = _carry(body, len(args), len(out_shape), len(scratch), exchange, grid)
        in_specs, out_specs = in_specs + exchange.in_specs, out_specs + exchange.out_specs
        out_shape, scratch, args = out_shape + exchange.out_shape, scratch + exchange.scratch_shapes, args + exchange.srcs
    outs = pl.pallas_call(
        body, name="inproj_bwd_dx", grid=grid, in_specs=in_specs, out_specs=out_specs, out_shape=out_shape,
        scratch_shapes=scratch, compiler_params=_cparams(("arbitrary", "arbitrary")),
    )(*args)
    return outs[0], outs[1], outs[2:]


def _inproj_bwd_dw(h, dproj):
    t = h.shape[0]
    tm, tn = _tile(t, 512), 1664

    def body(h_ref, dp_ref, o_ref):
        @pl.when(pl.program_id(1) == 0)
        def _():
            o_ref[...] = jnp.zeros_like(o_ref)

        o_ref[...] += _mm_tn(h_ref[...], dp_ref[...])

    return pl.pallas_call(
        body, name="inproj_bwd_dw", grid=(NCOL // tn, t // tm),
        in_specs=[pl.BlockSpec((tm, D), lambda j, i: (i, 0)), pl.BlockSpec((tm, tn), lambda j, i: (i, j))],
        out_specs=pl.BlockSpec((D, tn), lambda j, i: (0, j)),
        out_shape=jax.ShapeDtypeStruct((D, NCOL), F32),
        compiler_params=_cparams(("parallel", "arbitrary")),
    )(h, dproj)


def _prep_fwd(proj, cw):
    t = proj.shape[0]

    def body(p_ref, w_ref, o_ref):
        qk = (pl.program_id(0) < 2 * NH).astype(F32)
        o_ref[...] = _prep_fn(p_ref[...], w_ref[0:1, :], w_ref[1:2, :], w_ref[2:3, :], w_ref[3:4, :], qk)

    return pl.pallas_call(
        body, name="prep_fwd", grid=(3 * NH,),
        in_specs=[pl.BlockSpec((t, DH), lambda c: (0, c)), pl.BlockSpec((4, DH), lambda c: (0, c))],
        out_specs=pl.BlockSpec((None, t, DH), lambda c: (c, 0, 0)),
        out_shape=jax.ShapeDtypeStruct((3 * NH, t, DH), F32),
        compiler_params=_cparams(("parallel",)),
    )(proj, cw)


def _prep_bwd(proj, cw, dq, dk, dv):
    t = proj.shape[0]

    def body(p_ref, w_ref, dq_ref, dk_ref, dv_ref, dp_ref, dw_ref):
        c = pl.program_id(0)
        qk = (c < 2 * NH).astype(F32)
        _, vjp = jax.vjp(lambda p, w0, w1, w2, w3: _prep_fn(p, w0, w1, w2, w3, qk),
                         p_ref[...], w_ref[0:1, :], w_ref[1:2, :], w_ref[2:3, :], w_ref[3:4, :])
        d = jnp.where(c < NH, dq_ref[...], jnp.where(c < 2 * NH, dk_ref[...], dv_ref[...]))
        dp, dw0, dw1, dw2, dw3 = vjp(d)
        dp_ref[...] = dp.astype(dp_ref.dtype)
        dw_ref[0:1, :] = dw0
        dw_ref[1:2, :] = dw1
        dw_ref[2:3, :] = dw2
        dw_ref[3:4, :] = dw3

    return pl.pallas_call(
        body, name="prep_bwd", grid=(3 * NH,),
        in_specs=[pl.BlockSpec((t, DH), lambda c: (0, c)), pl.BlockSpec((4, DH), lambda c: (0, c))]
        + [pl.BlockSpec((None, t, DH), functools.partial(lambda c, off: (jnp.clip(c - off, 0, NH - 1), 0, 0), off=off))
           for off in (0, NH, 2 * NH)],
        out_specs=[pl.BlockSpec((t, DH), lambda c: (0, c)), pl.BlockSpec((4, DH), lambda c: (0, c))],
        out_shape=[jax.ShapeDtypeStruct((t, 3 * D), GRAD_ACT), jax.ShapeDtypeStruct((4, 3 * D), F32)],
        compiler_params=_cparams(("arbitrary",)),
    )(proj, cw, dq, dk, dv)


def _gates_fwd(proj, gvec):
    t = proj.shape[0]
    tm = _tile(t, 512)

    def body(p_ref, gv_ref, b_ref, g_ref):
        outs = _gates_fn(p_ref[...], gv_ref[0:1, :], gv_ref[1:2, :])
        for h in range(NH):
            b_ref[h] = outs[h]
            g_ref[h] = outs[NH + h]

    spec = pl.BlockSpec((NH, tm, DH), lambda i: (0, i, 0))
    return pl.pallas_call(
        body, name="gates_fwd", grid=(t // tm,),
        in_specs=[pl.BlockSpec((tm, DH), lambda i: (i, BD0 // DH)), _full((8, DH))],
        out_specs=[spec, spec],
        out_shape=[jax.ShapeDtypeStruct((NH, t, DH), F32)] * 2,
        compiler_params=_cparams(("parallel",)),
    )(proj, gvec)


def _gates_bwd(proj, gvec, dbb, dgcb):
    t = proj.shape[0]
    tm = _tile(t, 512)

    def body(p_ref, gv_ref, db_ref, dg_ref, dp_ref, dgv_ref):
        _, vjp = jax.vjp(_gates_fn, p_ref[...], gv_ref[0:1, :], gv_ref[1:2, :])
        cts = tuple(db_ref[h] for h in range(NH)) + tuple(dg_ref[h] for h in range(NH))
        dp, da, db = vjp(cts)
        dp_ref[...] = dp.astype(dp_ref.dtype)

        @pl.when(pl.program_id(0) == 0)
        def _():
            dgv_ref[...] = jnp.zeros_like(dgv_ref)

        dgv_ref[0:1, :] += da
        dgv_ref[1:2, :] += db

    spec = pl.BlockSpec((NH, tm, DH), lambda i: (0, i, 0))
    return pl.pallas_call(
        body, name="gates_bwd", grid=(t // tm,),
        in_specs=[pl.BlockSpec((tm, DH), lambda i: (i, BD0 // DH)), _full((8, DH)), spec, spec],
        out_specs=[pl.BlockSpec((tm, DH), lambda i: (i, 0)), _full((8, DH))],
        out_shape=[jax.ShapeDtypeStruct((t, DH), GRAD_ACT), jax.ShapeDtypeStruct((8, DH), F32)],
        compiler_params=_cparams(("arbitrary",)),
    )(proj, gvec, dbb, dgcb)


def _chunks_per_step(nch):
    return max(c for c in (8, 4, 2, 1) if nch % c == 0)


def _grid_ends(grid):
    def first():
        return functools.reduce(jnp.logical_and, [pl.program_id(a) == 0 for a in range(len(grid))])

    def last():
        return functools.reduce(jnp.logical_and, [pl.program_id(a) == n - 1 for a, n in enumerate(grid)])

    return first, last


def _carry(body, n_in, n_out, n_scratch, exchange, grid):
    first, last = _grid_ends(grid)
    na = exchange.narr

    def wrapped(*refs):
        a, b = n_in, n_in + na
        c, d = b + n_out, b + n_out + na
        e = d + n_scratch
        srcs, dsts, sems = refs[a:b], refs[c:d], refs[e:]

        @pl.when(first())
        def _():
            exchange.start(srcs, dsts, sems)

        body(*(refs[:a] + refs[b:c] + refs[d:e]))

        @pl.when(last())
        def _():
            exchange.wait(srcs, dsts, sems)

    return wrapped


def _delta_local_fwd(qkv, bb, gcb, exchange=None):
    t = qkv.shape[1]
    cps = _chunks_per_step(t // CH)
    rows = cps * CH
    grid = (NH, t // rows)

    def body(q_ref, k_ref, v_ref, b_ref, g_ref, *out_refs):
        slices = [slice(c * CH, (c + 1) * CH) for c in range(cps)]
        results = _chunks_local([tuple(ref[sl, :] for ref in (q_ref, k_ref, v_ref, b_ref, g_ref)) for sl in slices])
        for sl, (outs, t_inv) in zip(slices, results):
            for ref, val in zip(out_refs, outs + (t_inv,)):
                ref[sl, :] = val

    def blk(off):
        return pl.BlockSpec((None, rows, DH), lambda h, n: (h + off, n, 0))

    in_specs = [blk(0), blk(NH), blk(2 * NH), blk(0), blk(0)]
    out_specs = [blk(0)] * 6
    out_shape = [jax.ShapeDtypeStruct((NH, t, DH), F32)] * 6
    args, scratch, sem = [qkv, qkv, qkv, bb, gcb], [], ("parallel", "parallel")
    if exchange is not None:
        body = _carry(body, 5, 6, 0, exchange, grid)
        in_specs, out_specs = in_specs + exchange.in_specs, out_specs + exchange.out_specs
        out_shape, scratch, args = out_shape + exchange.out_shape, exchange.scratch_shapes, args + exchange.srcs
        sem = ("arbitrary", "arbitrary")
    outs = pl.pallas_call(
        body, name="delta_local_fwd", grid=grid, in_specs=in_specs, out_specs=out_specs, out_shape=out_shape,
        scratch_shapes=scratch, compiler_params=_cparams(sem),
    )(*args)
    return outs[:5], outs[5], outs[6:]


def _delta_local_bwd(qkv, bb, gcb, t_inv, cts, dgcb_state):
    t = qkv.shape[1]
    cps = _chunks_per_step(t // CH)
    rows = cps * CH

    def body(q_ref, k_ref, v_ref, b_ref, g_ref, ti_ref, du_ref, dw_ref, dqd_ref, dkd_ref, da_ref, dgs_ref,
             dq_ref, dk_ref, dv_ref, db_ref, dg_ref):
        slices = [slice(c * CH, (c + 1) * CH) for c in range(cps)]
        items = [tuple(ref[sl, :] for ref in (q_ref, k_ref, v_ref, b_ref, g_ref, ti_ref)) for sl in slices]
        cts = [tuple(ref[sl, :] for ref in (du_ref, dw_ref, dqd_ref, dkd_ref, da_ref)) for sl in slices]
        for sl, (dq, dk, dv, db, dg, _) in zip(slices, _side_by_side_vjp(_chunks_local_known, items, cts)):
            dq_ref[sl, :] = dq
            dk_ref[sl, :] = dk
            dv_ref[sl, :] = dv
            db_ref[sl, :] = db
            dg_ref[sl, :] = dg + dgs_ref[sl, :]

    def blk(off):
        return pl.BlockSpec((None, rows, DH), lambda h, n: (h + off, n, 0))

    return pl.pallas_call(
        body, name="delta_local_bwd", grid=(NH, t // rows),
        in_specs=[blk(0), blk(NH), blk(2 * NH)] + [blk(0)] * 9,
        out_specs=[blk(0)] * 5,
        out_shape=[jax.ShapeDtypeStruct((NH, t, DH), F32)] * 5,
        compiler_params=_cparams(("parallel", "parallel")),
    )(qkv, qkv, qkv, bb, gcb, t_inv, *cts, dgcb_state)


def _delta_state_fwd(local, gcb):
    t = gcb.shape[1]
    nch = t // CH

    def body(u_ref, w_ref, qd_ref, kd_ref, a_ref, g_ref, o_ref, s_ref, st_ref):
        @pl.when(pl.program_id(0) == 0)
        def _():
            st_ref[...] = jnp.zeros_like(st_ref)

        s_ref[...] = st_ref[...]
        items = [tuple(ref[h] for ref in (u_ref, w_ref, qd_ref, kd_ref, a_ref, g_ref, st_ref)) for h in range(NH)]
        for h, (o, ns) in enumerate(_state_steps(items)):
            o_ref[:, h * DH:(h + 1) * DH] = o
            st_ref[h] = ns

    blk = pl.BlockSpec((NH, CH, DH), lambda n: (0, n, 0))
    return pl.pallas_call(
        body, name="delta_state_fwd", grid=(nch,),
        in_specs=[blk] * 6,
        out_specs=[pl.BlockSpec((CH, D), lambda n: (n, 0)),
                   pl.BlockSpec((NH, None, DH, DH), lambda n: (0, n, 0, 0))],
        out_shape=[jax.ShapeDtypeStruct((t, D), F32), jax.ShapeDtypeStruct((NH, nch, DH, DH), F32)],
        scratch_shapes=[pltpu.VMEM((NH, DH, DH), F32)],
        compiler_params=_cparams(("arbitrary",)),
    )(*local, gcb)


def _delta_state_bwd(local, gcb, states, do):
    t = gcb.shape[1]
    nch = t // CH

    def body(u_ref, w_ref, qd_ref, kd_ref, a_ref, g_ref, s_ref, do_ref,
             du_ref, dw_ref, dqd_ref, dkd_ref, da_ref, dg_ref, ds_ref):
        @pl.when(pl.program_id(0) == 0)
        def _():
            ds_ref[...] = jnp.zeros_like(ds_ref)

        items = [tuple(ref[h] for ref in (u_ref, w_ref, qd_ref, kd_ref, a_ref, g_ref, s_ref)) for h in range(NH)]
        cts = [(do_ref[:, h * DH:(h + 1) * DH], ds_ref[h]) for h in range(NH)]
        for h, (du, dw, dqd, dkd, da, dg, ds) in enumerate(_side_by_side_vjp(_state_steps, items, cts)):
            du_ref[h] = du
            dw_ref[h] = dw
            dqd_ref[h] = dqd
            dkd_ref[h] = dkd
            da_ref[h] = da
            dg_ref[h] = dg
            ds_ref[h] = ds

    blk = pl.BlockSpec((NH, CH, DH), lambda n: (0, nch - 1 - n, 0))
    return pl.pallas_call(
        body, name="delta_state_bwd", grid=(nch,),
        in_specs=[blk] * 6 + [pl.BlockSpec((NH, None, DH, DH), lambda n: (0, nch - 1 - n, 0, 0)),
                              pl.BlockSpec((CH, D), lambda n: (nch - 1 - n, 0))],
        out_specs=[blk] * 6,
        out_shape=[jax.ShapeDtypeStruct((NH, t, DH), F32)] * 6,
        scratch_shapes=[pltpu.VMEM((NH, DH, DH), F32)],
        compiler_params=_cparams(("arbitrary",)),
    )(*local, gcb, states, do)


def _s5_params(ar, ai, ldt, br2, bi2):
    def body(ar_ref, ai_ref, ld_ref, br_ref, bi_ref, lr_ref, li_ref, bbr_ref, bbi_ref):
        lr, li, bbr, bbi = _s5_params_fn(ar_ref[...], ai_ref[...], ld_ref[...], br_ref[...], bi_ref[...])
        lr_ref[...] = lr
        li_ref[...] = li
        bbr_ref[...] = bbr
        bbi_ref[...] = bbi

    sq = pl.BlockSpec((None, NG, NS), lambda l: (l, 0, 0))
    wide = pl.BlockSpec((None, NG, NS * GS), lambda l: (l, 0, 0))
    return pl.pallas_call(
        body, name="s5_params", grid=(DEPTH,),
        in_specs=[sq, sq, pl.BlockSpec((None, NG, 1), lambda l: (l, 0, 0)), wide, wide],
        out_specs=[sq, sq, wide, wide],
        out_shape=[jax.ShapeDtypeStruct((DEPTH, NG, NS), F32)] * 2
        + [jax.ShapeDtypeStruct((DEPTH, NG, NS * GS), F32)] * 2,
        compiler_params=_cparams(("parallel",)),
    )(ar, ai, ldt, br2, bi2)


def _s5_params_bwd(ar, ai, ldt, br2, bi2, dlr, dli, dbbr, dbbi):
    def body(ar_ref, ai_ref, ld_ref, br_ref, bi_ref, a_ref, b_ref, c_ref, d_ref,
             dar_ref, dai_ref, dld_ref, dbr_ref, dbi_ref):
        _, vjp = jax.vjp(_s5_params_fn, ar_ref[...], ai_ref[...], ld_ref[...], br_ref[...], bi_ref[...])
        dar, dai, dld, dbr, dbi = vjp((a_ref[...], b_ref[...], c_ref[...], d_ref[...]))
        dar_ref[...] = dar
        dai_ref[...] = dai
        dld_ref[...] = dld
        dbr_ref[...] = dbr
        dbi_ref[...] = dbi

    sq = pl.BlockSpec((None, NG, NS), lambda l: (l, 0, 0))
    col = pl.BlockSpec((None, NG, 1), lambda l: (l, 0, 0))
    wide = pl.BlockSpec((None, NG, NS * GS), lambda l: (l, 0, 0))
    return pl.pallas_call(
        body, name="s5_params_bwd", grid=(DEPTH,),
        in_specs=[sq, sq, col, wide, wide, sq, sq, wide, wide],
        out_specs=[sq, sq, col, wide, wide],
        out_shape=[jax.ShapeDtypeStruct((DEPTH, NG, NS), F32)] * 2 + [jax.ShapeDtypeStruct((DEPTH, NG, 1), F32)]
        + [jax.ShapeDtypeStruct((DEPTH, NG, NS * GS), F32)] * 2,
        compiler_params=_cparams(("parallel",)),
    )(ar, ai, ldt, br2, bi2, dlr, dli, dbbr, dbbi)


def _s5_tile_rows(t):
    return _tile(t // 2, 1024)


def _s5_fwd(proj, lam, bblk, cblk):
    t = proj.shape[0]
    r = _s5_tile_rows(t)
    nt = t // r
    u0 = 4 * D // DH

    def body(u_ref, lam_ref, b_ref, c_ref, y_ref, car_ref, st_ref, hr_ref, hi_ref, cr_ref, ci_ref):
        @pl.when(pl.program_id(1) == 0)
        def _():
            st_ref[...] = jnp.zeros_like(st_ref)

        car_ref[...] = st_ref[...]
        hr, hi = _s5_states(u_ref[...], lam_ref, b_ref, st_ref, hr_ref, hi_ref, cr_ref, ci_ref)
        y_ref[...] = _mm(hr, c_ref[0]) - _mm(hi, c_ref[1])
        st_ref[0:1, :] = _last_row(hr)
        st_ref[1:2, :] = _last_row(hi)

    scratch = ([pltpu.VMEM((8, SW), F32)] + [pltpu.VMEM((SW // DH, r, DH), F32)] * 2
               + [pltpu.VMEM((r // SUB, SW), F32)] * 2)
    return pl.pallas_call(
        body, name="s5_fwd", grid=(NCB, nt),
        in_specs=[pl.BlockSpec((r, DH), lambda c, i: (i, u0 + c)),
                  pl.BlockSpec((2, 1, SW), lambda c, i: (0, 0, c)),
                  pl.BlockSpec((2, None, DH, SW), lambda c, i: (0, c, 0, 0)),
                  pl.BlockSpec((2, None, SW, DH), lambda c, i: (0, c, 0, 0))],
        out_specs=[pl.BlockSpec((r, DH), lambda c, i: (i, c)),
                   pl.BlockSpec((None, 8, SW), lambda c, i: (i, 0, c))],
        out_shape=[jax.ShapeDtypeStruct((t, D), F32), jax.ShapeDtypeStruct((nt, 8, NG * NS), F32)],
        scratch_shapes=scratch,
        compiler_params=_cparams(("parallel", "arbitrary")),
    )(proj, lam, bblk, cblk)


def _s5_bwd(proj, lam, bblk, cblk, carries, dy, du_skip, exchange=None):
    t = proj.shape[0]
    r = _s5_tile_rows(t)
    nt = t // r
    u0 = 4 * D // DH

    def body(u_ref, lam_ref, b_ref, c_ref, car_ref, dy_ref, dus_ref, du_ref, dlam_ref, db_ref, dc_ref, dst_ref,
             hr_ref, hi_ref, ar_ref, ai_ref, cr_ref, ci_ref):
        first = pl.program_id(1) == 0

        @pl.when(first)
        def _():
            dst_ref[...] = jnp.zeros_like(dst_ref)

        u, dy = u_ref[...], dy_ref[...]
        lr, li = lam_ref[0], lam_ref[1]
        hr, hi = _s5_states(u, lam_ref, b_ref, car_ref, hr_ref, hi_ref, cr_ref, ci_ref)
        dcr2, dci2 = _mm_tn(hr, dy), -_mm_tn(hi, dy)
        last = _rows((r, SW)) == r - 1
        inr, ini = _cmul(lr, -li, dst_ref[0:1, :], dst_ref[1:2, :])
        dhr = _mm_nt(dy, c_ref[0]) + jnp.where(last, inr, 0.0)
        dhi = jnp.where(last, ini, 0.0) - _mm_nt(dy, c_ref[1])
        ar, ai = _scan_tile(dhr, dhi, lr, -li, ar_ref, ai_ref, cr_ref, ci_ref, True)
        top = _rows((r, SW)) == 0
        dst_ref[0:1, :] = jnp.sum(jnp.where(top, ar, 0.0), axis=0, keepdims=True)
        dst_ref[1:2, :] = jnp.sum(jnp.where(top, ai, 0.0), axis=0, keepdims=True)
        du_ref[...] = (_mm_nt(ar, b_ref[0]) + _mm_nt(ai, b_ref[1]) + dus_ref[...]).astype(du_ref.dtype)
        dbr, dbi = _mm_tn(u, ar), _mm_tn(u, ai)
        pr = _sd(hr, 1) + jnp.where(top, car_ref[0:1, :], 0.0)
        pi = _sd(hi, 1) + jnp.where(top, car_ref[1:2, :], 0.0)
        dlr = jnp.sum(ar * pr + ai * pi, axis=0, keepdims=True)
        dli = jnp.sum(ai * pr - ar * pi, axis=0, keepdims=True)

        @pl.when(first)
        def _():
            dlam_ref[0] = dlr
            dlam_ref[1] = dli
            db_ref[0] = dbr
            db_ref[1] = dbi
            dc_ref[0] = dcr2
            dc_ref[1] = dci2

        @pl.when(jnp.logical_not(first))
        def _():
            dlam_ref[0] += dlr
            dlam_ref[1] += dli
            db_ref[0] += dbr
            db_ref[1] += dbi
            dc_ref[0] += dcr2
            dc_ref[1] += dci2

    grid = (NCB, nt)
    in_specs = [pl.BlockSpec((r, DH), lambda c, i: (nt - 1 - i, u0 + c)),
                pl.BlockSpec((2, 1, SW), lambda c, i: (0, 0, c)),
                pl.BlockSpec((2, None, DH, SW), lambda c, i: (0, c, 0, 0)),
                pl.BlockSpec((2, None, SW, DH), lambda c, i: (0, c, 0, 0)),
                pl.BlockSpec((None, 8, SW), lambda c, i: (nt - 1 - i, 0, c)),
                pl.BlockSpec((r, DH), lambda c, i: (nt - 1 - i, c)),
                pl.BlockSpec((r, DH), lambda c, i: (nt - 1 - i, c))]
    out_specs = [pl.BlockSpec((r, DH), lambda c, i: (nt - 1 - i, c)),
                 pl.BlockSpec((2, 1, SW), lambda c, i: (0, 0, c)),
                 pl.BlockSpec((2, None, DH, SW), lambda c, i: (0, c, 0, 0)),
                 pl.BlockSpec((2, None, SW, DH), lambda c, i: (0, c, 0, 0))]
    out_shape = [jax.ShapeDtypeStruct((t, D), GRAD_ACT), jax.ShapeDtypeStruct((2, 1, NG * NS), F32),
                 jax.ShapeDtypeStruct((2, NCB, DH, SW), F32), jax.ShapeDtypeStruct((2, NCB, SW, DH), F32)]
    scratch = ([pltpu.VMEM((8, SW), F32)] + [pltpu.VMEM((SW // DH, r, DH), F32)] * 4
               + [pltpu.VMEM((r // SUB, SW), F32)] * 2)
    args, sem = [proj, lam, bblk, cblk, carries, dy, du_skip], ("parallel", "arbitrary")
    if exchange is not None:
        body = _carry(body, len(args), len(out_shape), len(scratch), exchange, grid)
        in_specs, out_specs = in_specs + exchange.in_specs, out_specs + exchange.out_specs
        out_shape, scratch, args = out_shape + exchange.out_shape, scratch + exchange.scratch_shapes, args + exchange.srcs
        sem = ("arbitrary", "arbitrary")
    outs = pl.pallas_call(
        body, name="s5_bwd", grid=grid, in_specs=in_specs, out_specs=out_specs, out_shape=out_shape,
        scratch_shapes=scratch, compiler_params=_cparams(sem),
    )(*args)
    return outs[:4], outs[4:]


def _proj_spec(tm, col):
    return pl.BlockSpec((tm, D), lambda i: (i, col))


def _layer_mat(l):
    return pl.BlockSpec((None, D, D), lambda i: (l, 0, 0))


def _mix_fwd(proj, o, s5y, x, hn, dvec, wglu, bglu, wout, npost, l):
    t = x.shape[0]
    tm = _tile(t, 256)

    def body(za_ref, u_ref, zb_ref, ra_ref, rb_ref, o_ref, y_ref, x_ref, hn_ref, d_ref, wg_ref, bg_ref, wo_ref,
             np_ref, xn_ref):
        y0 = _mix_pre(y_ref[...], u_ref[...], d_ref[...])
        gl = _mm(y0, wg_ref[...]) + bg_ref[...]
        m = _mix_mid(o_ref[...], za_ref[...], y0, gl, zb_ref[...], ra_ref[...], rb_ref[...], hn_ref[...])
        out = _mm(m, wo_ref[...])
        xn_ref[...] = _mix_post(x_ref[...], out, np_ref[...])

    act = pl.BlockSpec((tm, D), lambda i: (i, 0))
    return pl.pallas_call(
        body, name="mix_fwd", grid=(t // tm,),
        in_specs=[_proj_spec(tm, 3), _proj_spec(tm, 4), _proj_spec(tm, 5), _proj_spec(tm, 6), _proj_spec(tm, 7),
                  act, act, act, _full((1, DH)), _full((1, D)), _layer_mat(l), _full((1, D)), _layer_mat(l),
                  _full((1, D))],
        out_specs=act,
        out_shape=jax.ShapeDtypeStruct((t, D), F32),
        compiler_params=_cparams(("parallel",)),
    )(proj, proj, proj, proj, proj, o, s5y, x, hn, dvec, wglu, bglu, wout, npost)


def _mix_bwd(proj, o, s5y, x, hn, dvec, wglu, bglu, wout, npost, dxn, l):
    t = x.shape[0]
    tm = _tile(t, 128)

    def body(za_ref, u_ref, zb_ref, ra_ref, rb_ref, o_ref, y_ref, x_ref, hn_ref, d_ref, wg_ref, bg_ref, wo_ref,
             np_ref, dxn_ref,
             dza_ref, du_ref, dzb_ref, dra_ref, drb_ref, do_ref, dy_ref, dx_ref,
             dwg_ref, dwo_ref, dvecs_ref, dhn_ref):
        y0, vjp_pre = jax.vjp(_mix_pre, y_ref[...], u_ref[...], d_ref[...])
        gl = _mm(y0, wg_ref[...]) + bg_ref[...]
        m, vjp_mid = jax.vjp(_mix_mid, o_ref[...], za_ref[...], y0, gl, zb_ref[...], ra_ref[...], rb_ref[...],
                             hn_ref[...])
        out = _mm(m, wo_ref[...])
        _, vjp_post = jax.vjp(_mix_post, x_ref[...], out, np_ref[...])
        dx, dout, dnp = vjp_post(dxn_ref[...])
        dm = _mm_nt(dout, wo_ref[...])
        dwo = _mm_tn(m, dout)
        do, dza, dy0, dgl, dzb, dra, drb, dhn = vjp_mid(dm)
        dwg = _mm_tn(y0, dgl)
        dbg = jnp.sum(dgl, axis=0, keepdims=True)
        dy0 = dy0 + _mm_nt(dgl, wg_ref[...])
        dy, du, dd = vjp_pre(dy0)
        dza_ref[...] = dza.astype(dza_ref.dtype)
        du_ref[...] = du
        dzb_ref[...] = dzb.astype(dzb_ref.dtype)
        dra_ref[...] = dra.astype(dra_ref.dtype)
        drb_ref[...] = drb.astype(drb_ref.dtype)
        do_ref[...] = do
        dy_ref[...] = dy
        dx_ref[...] = dx
        first = pl.program_id(0) == 0

        @pl.when(first)
        def _():
            dwg_ref[...] = dwg
            dwo_ref[...] = dwo
            dvecs_ref[...] = jnp.zeros_like(dvecs_ref)
            dhn_ref[...] = jnp.zeros_like(dhn_ref)

        @pl.when(jnp.logical_not(first))
        def _():
            dwg_ref[...] += dwg
            dwo_ref[...] += dwo

        dvecs_ref[0:1, :] += dd
        dvecs_ref[1:2, :] += dbg
        dvecs_ref[2:3, :] += dnp
        dhn_ref[0:1, :] += dhn

    act = pl.BlockSpec((tm, D), lambda i: (i, 0))
    a, ga = jax.ShapeDtypeStruct((t, D), F32), jax.ShapeDtypeStruct((t, D), GRAD_ACT)
    w = jax.ShapeDtypeStruct((D, D), F32)
    return pl.pallas_call(
        body, name="mix_bwd", grid=(t // tm,),
        in_specs=[_proj_spec(tm, 3), _proj_spec(tm, 4), _proj_spec(tm, 5), _proj_spec(tm, 6), _proj_spec(tm, 7),
                  act, act, act, _full((1, DH)), _full((1, D)), _layer_mat(l), _full((1, D)), _layer_mat(l),
                  _full((1, D)), act],
        out_specs=[act] * 8 + [_full((D, D)), _full((D, D)), _full((8, D)), _full((8, DH))],
        out_shape=[ga, a, ga, ga, ga, a, a, a, w, w, jax.ShapeDtypeStruct((8, D), F32),
                   jax.ShapeDtypeStruct((8, DH), F32)],
        compiler_params=_cparams(("arbitrary",)),
    )(proj, proj, proj, proj, proj, o, s5y, x, hn, dvec, wglu, bglu, wout, npost, dxn)


def _loss_grad(y, target):
    t = y.shape[0]
    tm = _tile(t, 512)

    def body(y_ref, t_ref, dy_ref, l_ref):
        err = y_ref[...] - t_ref[...]
        dy_ref[...] = err * (1.0 / D)
        part = jnp.sum(jnp.sum(err * err, axis=1, keepdims=True), axis=0, keepdims=True) * (0.5 / D)
        part = jnp.broadcast_to(part, (8, DH))

        @pl.when(pl.program_id(0) == 0)
        def _():
            l_ref[...] = part

        @pl.when(pl.program_id(0) > 0)
        def _():
            l_ref[...] += part

    act = pl.BlockSpec((tm, D), lambda i: (i, 0))
    return pl.pallas_call(
        body, name="loss_grad", grid=(t // tm,),
        in_specs=[act, act], out_specs=[act, _full((8, DH))],
        out_shape=[jax.ShapeDtypeStruct((t, D), F32), jax.ShapeDtypeStruct((8, DH), F32)],
        compiler_params=_cparams(("arbitrary",)),
    )(y, target)


def _flips(rel):
    x, y, c = lax.axis_index("x"), lax.axis_index("y"), lax.axis_index("c")
    fx, fy, fc = rel
    return (x ^ fx if fx else x, y ^ fy if fy else y, c ^ fc if fc else c)


CHIP_RELS = ((1, 0, 0), (0, 1, 0), (1, 1, 0))
ALL_RELS = tuple((fx, fy, fc) for fx in (0, 1) for fy in (0, 1) for fc in (0, 1) if (fx, fy, fc) != (0, 0, 0))


def _slot_of(pos, by_chip):
    px, py, pc = pos
    return 2 * px + py if by_chip else 4 * px + 2 * py + pc


class _Exchange:
    def __init__(self, srcs, rels, by_chip, scatter):
        self.srcs, self.rels, self.by_chip, self.scatter = list(srcs), rels, by_chip, scatter
        self.narr = len(self.srcs)
        nslot, nsem = NCHIP if by_chip else NDEV, self.narr * len(rels)
        self.in_specs = [pl.BlockSpec(memory_space=pl.ANY)] * self.narr
        self.out_specs = [pl.BlockSpec(memory_space=pl.ANY)] * self.narr
        self.out_shape = [jax.ShapeDtypeStruct((nslot,) + s.shape[-2:], s.dtype) for s in self.srcs]
        self.scratch_shapes = [pltpu.SemaphoreType.DMA((nsem,)), pltpu.SemaphoreType.DMA((nsem,)),
                               pltpu.SemaphoreType.DMA((self.narr,))]

    def _copies(self, src_refs, dst_refs, sems):
        send_sems, recv_sems, local_sems = sems
        my_slot = _slot_of(_flips((0, 0, 0)), self.by_chip)
        local, sends, arrivals = [], [], []
        for a, (src_ref, dst_ref) in enumerate(zip(src_refs, dst_refs)):
            local.append(pltpu.make_async_copy(src_ref.at[my_slot] if self.scatter else src_ref, dst_ref.at[my_slot],
                                               local_sems.at[a]))
            for k, rel in enumerate(self.rels):
                peer = _flips(rel)
                pair = dict(send_sem=send_sems.at[a * len(self.rels) + k], recv_sem=recv_sems.at[a * len(self.rels) + k],
                            device_id=peer, device_id_type=pl.DeviceIdType.MESH)
                part = src_ref.at[_slot_of(peer, self.by_chip)] if self.scatter else src_ref
                sends.append(pltpu.make_async_remote_copy(src_ref=part, dst_ref=dst_ref.at[my_slot], **pair))
                arrivals.append(pltpu.make_async_remote_copy(
                    src_ref=src_ref.at[0] if self.scatter else src_ref, dst_ref=dst_ref.at[_slot_of(peer, self.by_chip)],
                    **pair))
        return local, sends, arrivals

    def start(self, src_refs, dst_refs, sems):
        local, sends, _ = self._copies(src_refs, dst_refs, sems)
        for cp in local + sends:
            cp.start()

    def wait(self, src_refs, dst_refs, sems):
        local, sends, arrivals = self._copies(src_refs, dst_refs, sems)
        for cp in arrivals:
            cp.wait_recv()
        for cp in sends:
            cp.wait_send()
        for cp in local:
            cp.wait()


def _exchange(srcs, rels, by_chip, scatter, name):
    ex = _Exchange(srcs, rels, by_chip, scatter)

    def body(*refs):
        parts = refs[:ex.narr], refs[ex.narr:2 * ex.narr], refs[2 * ex.narr:]
        ex.start(*parts)
        ex.wait(*parts)

    return pl.pallas_call(body, name=name, in_specs=ex.in_specs, out_specs=ex.out_specs, out_shape=ex.out_shape,
                          scratch_shapes=ex.scratch_shapes)(*ex.srcs)


def _sibling_swap(srcs, name):
    narr = len(srcs)

    def body(*refs):
        src_refs, dst_refs = refs[:narr], refs[narr:2 * narr]
        send_sems, recv_sems = refs[2 * narr:]
        peer = _flips((0, 0, 1))
        copies = [pltpu.make_async_remote_copy(src_ref=s, dst_ref=d, send_sem=send_sems.at[a], recv_sem=recv_sems.at[a],
                                               device_id=peer, device_id_type=pl.DeviceIdType.MESH)
                  for a, (s, d) in enumerate(zip(src_refs, dst_refs))]
        for cp in copies:
            cp.start()
        for cp in copies:
            cp.wait()

    return pl.pallas_call(
        body, name=name,
        in_specs=[pl.BlockSpec(memory_space=pl.ANY)] * narr,
        out_specs=[pl.BlockSpec(memory_space=pl.ANY)] * narr,
        out_shape=[jax.ShapeDtypeStruct(s.shape, s.dtype) for s in srcs],
        scratch_shapes=[pltpu.SemaphoreType.DMA((narr,)), pltpu.SemaphoreType.DMA((narr,))],
    )(*srcs)


def _all_reduce(src, name):
    rows, cols = src.shape
    r = rows // NDEV
    nrel = len(ALL_RELS)

    def body(src_ref, out_ref, parts_ref, mine_ref, send_sems, recv_sems):
        my_slot = _slot_of(_flips((0, 0, 0)), False)

        def piece(ref, slot):
            return ref.at[pl.ds(pl.multiple_of(slot * r, 8), r), :]

        def copies(phase):
            out = []
            for k, rel in enumerate(ALL_RELS):
                peer = _flips(rel)
                pair = dict(send_sem=send_sems.at[phase * nrel + k], recv_sem=recv_sems.at[phase * nrel + k],
                            device_id=peer, device_id_type=pl.DeviceIdType.MESH)
                if phase == 0:
                    out.append(pltpu.make_async_remote_copy(src_ref=piece(src_ref, _slot_of(peer, False)),
                                                            dst_ref=parts_ref.at[my_slot], **pair))
                else:
                    out.append(pltpu.make_async_remote_copy(src_ref=mine_ref, dst_ref=piece(out_ref, my_slot), **pair))
            return out

        first = copies(0)
        for cp in first:
            cp.start()
        parts_ref[my_slot] = piece(src_ref, my_slot)[...]
        for cp in first:
            cp.wait_recv()
        acc = parts_ref[0]
        for s in range(1, NDEV):
            acc = acc + parts_ref[s]
        mine_ref[...] = acc
        second = copies(1)
        for cp in second:
            cp.start()
        piece(out_ref, my_slot)[...] = acc
        for cp in second:
            cp.wait_recv()
        for cp in first + second:
            cp.wait_send()

    return pl.pallas_call(
        body, name=name,
        in_specs=[pl.BlockSpec(memory_space=pltpu.VMEM)], out_specs=pl.BlockSpec(memory_space=pltpu.VMEM),
        out_shape=jax.ShapeDtypeStruct(src.shape, src.dtype),
        scratch_shapes=[pltpu.VMEM((NDEV, r, cols), src.dtype), pltpu.VMEM((r, cols), src.dtype),
                        pltpu.SemaphoreType.DMA((2 * nrel,)), pltpu.SemaphoreType.DMA((2 * nrel,))],
        compiler_params=pltpu.CompilerParams(vmem_limit_bytes=VMEM_LIMIT),
    )(src)


def _sum_slots(parts, name):
    ns, rows, cols = parts.shape
    tr = _row_tile(rows, 256)

    def body(p_ref, o_ref):
        acc = p_ref[0].astype(F32)
        for s in range(1, ns):
            acc = acc + p_ref[s].astype(F32)
        o_ref[...] = acc

    return pl.pallas_call(
        body, name=name, grid=(rows // tr,),
        in_specs=[pl.BlockSpec((ns, tr, cols), lambda i: (0, i, 0))],
        out_specs=pl.BlockSpec((tr, cols), lambda i: (i, 0)),
        out_shape=jax.ShapeDtypeStruct((rows, cols), F32),
        compiler_params=_cparams(("parallel",)),
    )(parts)


def _adamw(w, g_parts, m, v, name, max_rows=256):
    rows, cols = w.shape
    tr = _row_tile(rows, max_rows)
    c1 = 1.0 / (1.0 - ADAM_B1 ** ADAM_STEP)
    c2 = 1.0 / (1.0 - ADAM_B2 ** ADAM_STEP)
    npart = len(g_parts)

    def body(*refs):
        w_ref, m_ref, v_ref = refs[:3]
        g_refs = refs[3:3 + npart]
        go_ref, d_ref, nm_ref, nv_ref = refs[3 + npart:]
        terms = []
        for g_ref in g_refs:
            terms += [g_ref[...]] if len(g_ref.shape) == 2 else [g_ref[s] for s in range(g_ref.shape[0])]
        g = terms[0]
        for term in terms[1:]:
            g = g + term
        nm = ADAM_B1 * m_ref[...] + (1.0 - ADAM_B1) * g
        nv = ADAM_B2 * v_ref[...] + (1.0 - ADAM_B2) * (g * g)
        d_ref[...] = -ADAM_LR * ((nm * c1) / (jnp.sqrt(nv * c2) + ADAM_EPS) + ADAM_WD * w_ref[...])
        go_ref[...] = g
        nm_ref[...] = nm
        nv_ref[...] = nv

    blk = pl.BlockSpec((tr, cols), lambda i: (i, 0))
    g_specs = [blk if p.ndim == 2 else pl.BlockSpec((p.shape[0], tr, cols), lambda i: (0, i, 0)) for p in g_parts]
    out = jax.ShapeDtypeStruct((rows, cols), F32)
    return pl.pallas_call(
        body, name=name, grid=(rows // tr,),
        in_specs=[blk, blk, blk] + g_specs,
        out_specs=[blk] * 4, out_shape=[out] * 4,
        compiler_params=_cparams(("parallel",)),
    )(w, m, v, *g_parts)


WIN_SHARD = 2052
CONV_SHARD = 768
ROW_SHARD = 256

SMALL = (("norm_pre", (DEPTH, D)), ("a_log", (DEPTH, NH)), ("dt_bias", (DEPTH, NH)), ("head_norm", (DEPTH, DH)),
         ("ssm_a_re", (DEPTH, NG, NS)), ("ssm_a_im", (DEPTH, NG, NS)), ("ssm_log_dt", (DEPTH, NG)),
         ("ssm_b_re", (DEPTH, NG, NS, GS)), ("ssm_b_im", (DEPTH, NG, NS, GS)),
         ("ssm_c_re", (DEPTH, NG, GS, NS)), ("ssm_c_im", (DEPTH, NG, GS, NS)), ("ssm_d", (DEPTH, D)),
         ("b_glu", (DEPTH, D)), ("norm_post", (DEPTH, D)))


def _pad_rows(flat, rows):
    return jnp.pad(flat, (0, rows * D - flat.shape[0])).reshape(rows, D)


def _rows_by_chip(a):
    nl, rows, cols = a.shape
    return a.reshape(nl, NCHIP, rows // NCHIP, cols).transpose(1, 0, 2, 3).reshape(NCHIP, -1, cols)


def _rows_from_chips(a):
    _, rows, cols = a.shape
    return a.reshape(NCHIP, DEPTH, rows // DEPTH, cols).transpose(1, 0, 2, 3).reshape(DEPTH, -1, cols)


def _cols_by_chip(a):
    nl, rows, cols = a.shape
    return a.reshape(nl, rows, NCHIP, cols // NCHIP).transpose(2, 0, 1, 3).reshape(NCHIP, nl * rows, -1)


def _cols_from_chips(a, nl):
    _, rows, cols = a.shape
    return a.reshape(NCHIP, nl, rows // nl, cols).transpose(1, 2, 0, 3).reshape(nl, rows // nl, NCHIP * cols)


SMALL_ROWS = sum(-(-math.prod(s) // (8 * D)) * 8 for _, s in SMALL)
CONV_ROWS = DEPTH * 4 * 3 * D // D


def _pack_small(vals, extra=()):
    parts = []
    for val in tuple(vals) + tuple(extra):
        n = val.size
        parts.append(_pad_rows(val.reshape(-1), -(-n // (8 * D)) * 8))
    return jnp.concatenate(parts, axis=0)


def _unpack_small(flat):
    outs, r0 = [], 0
    for _, shape in SMALL:
        n = math.prod(shape)
        rows = -(-n // (8 * D)) * 8
        outs.append(flat[r0:r0 + rows].reshape(-1)[:n].reshape(shape))
        r0 += rows
    return outs


def _rearrange_cols(w):
    pad = jnp.zeros(w.shape[:-1] + (NCOL - BD0 - 2 * NH,), w.dtype)
    return jnp.concatenate([w[..., :4 * D], w[..., 4 * D + 2 * NH:], w[..., 4 * D:4 * D + 2 * NH], pad], axis=-1)


def _restore_cols(w):
    return jnp.concatenate([w[..., :4 * D], w[..., BD0:BD0 + 2 * NH], w[..., 4 * D:BD0]], axis=-1)


def _block_diag_b(bb2):
    b = bb2.reshape(-1, NCB, GPB, NS, GS)
    return jnp.einsum("lkgnc,gh->lkgchn", b, jnp.eye(GPB, dtype=F32)).reshape(-1, NCB, GPB * GS, SW)


def _block_diag_b_t(d):
    blocks = jnp.einsum("lkgchn,gh->lkgnc", d.reshape(-1, NCB, GPB, GS, GPB, NS), jnp.eye(GPB, dtype=F32))
    return blocks.reshape(-1, NG, NS * GS)


def _block_diag_c(c):
    blocks = jnp.einsum("lkgcn,gh->lkgnhc", c.reshape(-1, NCB, GPB, GS, NS), jnp.eye(GPB, dtype=F32))
    return blocks.reshape(-1, NCB, SW, GPB * GS)


def _block_diag_c_t(d):
    blocks = jnp.einsum("lkgnhc,gh->lkgcn", d.reshape(-1, NCB, GPB, NS, GPB, GS), jnp.eye(GPB, dtype=F32))
    return blocks.reshape(-1, NG, GS, NS)


def _local_step(x, target, weights, conv, small, comm=None):
    weights = list(weights) + [None] * (DEPTH - len(weights))
    ar, ai = small["ssm_a_re"], small["ssm_a_im"]
    ldt = small["ssm_log_dt"].reshape(DEPTH, NG, 1)
    br2 = small["ssm_b_re"].reshape(DEPTH, NG, NS * GS)
    bi2 = small["ssm_b_im"].reshape(DEPTH, NG, NS * GS)
    lr, li, bbr2, bbi2 = _s5_params(ar, ai, ldt, br2, bi2)

    def row(name, l, width):
        return small[name][l].reshape(1, width)

    gvecs = jnp.pad(jnp.stack([small["a_log"], small["dt_bias"]], axis=1), ((0, 0), (0, 6), (NH, DH - 2 * NH)))
    lams = jnp.stack([lr.reshape(DEPTH, 1, NG * NS), li.reshape(DEPTH, 1, NG * NS)], axis=1)
    bblks = jnp.stack([_block_diag_b(bbr2), _block_diag_b(bbi2)], axis=1)
    cblks = jnp.stack([_block_diag_c(small["ssm_c_re"]), _block_diag_c(small["ssm_c_im"])], axis=1)
    saved = []
    for l in range(DEPTH):
        gvec, lam, bblk, cblk = gvecs[l], lams[l], bblks[l], cblks[l]
        wcat, wglu, wout = weights[l]
        proj, h = _inproj_fwd(x, row("norm_pre", l, D), wcat, 0)
        qkv = _prep_fwd(proj, conv[l])
        bb, gcb = _gates_fwd(proj, gvec)
        fetch = _Exchange(comm["weight_parts"](l + 1), CHIP_RELS, True, False) if comm and l + 1 < DEPTH else None
        local, t_inv, fetched = _delta_local_fwd(qkv, bb, gcb, fetch)
        if fetch is not None:
            weights[l + 1] = comm["weights_from"](fetched)
        o, states = _delta_state_fwd(local, gcb)
        s5y, carries = _s5_fwd(proj, lam, bblk, cblk)
        xn = _mix_fwd(proj, o, s5y, x, row("head_norm", l, DH), row("ssm_d", l, D), wglu, row("b_glu", l, D),
                      wout, row("norm_post", l, D), 0)
        saved.append((x, proj, h, qkv, bb, gcb, local, t_inv, o, states, s5y, carries, gvec, lam, bblk, cblk))
        x = xn

    dx, loss_part = _loss_grad(x, target)

    g = {k: [None] * DEPTH for k in ("wcat", "conv", "wglu", "wout", "norm_pre", "a_log", "dt_bias", "head_norm",
                                     "ssm_d", "b_glu", "norm_post", "dlam", "dbblk", "dcblk")}
    from_chips, send, send_layer = [None] * DEPTH, None, None
    for l in reversed(range(DEPTH)):
        xl, proj, h, qkv, bb, gcb, local, t_inv, o, states, s5y, carries, gvec, lam, bblk, cblk = saved[l]
        wcat, wglu, wout = weights[l]
        (dza, du_skip, dzb, dra, drb, do, ds5y, dxres, dwg, dwo, dvecs, dhn) = _mix_bwd(
            proj, o, s5y, xl, row("head_norm", l, DH), row("ssm_d", l, D), wglu, row("b_glu", l, D), wout,
            row("norm_post", l, D), dx, 0)
        (du, dlam, dbblk, dcblk), arrived = _s5_bwd(proj, lam, bblk, cblk, carries, ds5y, du_skip, send)
        if send is not None:
            from_chips[send_layer] = arrived
        *dlocal, dgcb_state = _delta_state_bwd(local, gcb, states, do)
        dq, dk, dv, dbb, dgcb = _delta_local_bwd(qkv, bb, gcb, t_inv, dlocal, dgcb_state)
        dbd, dgvec = _gates_bwd(proj, gvec, dbb, dgcb)
        dpre, dconv = _prep_bwd(proj, conv[l], dq, dk, dv)
        dproj = jnp.concatenate([dpre, dza, du, dzb, dra, drb, dbd], axis=1)
        g["wcat"][l] = _inproj_bwd_dw(h, dproj)
        g["conv"][l], g["wglu"][l], g["wout"][l] = dconv, dwg, dwo
        send = _Exchange(comm["grad_parts"](g["wcat"][l], dwg, dwo), CHIP_RELS, True, True) if comm else None
        dx, dgain, arrived = _inproj_bwd_dx(dproj, wcat, xl, row("norm_pre", l, D), dxres, 0, send if l == 0 else None)
        if comm and l == 0:
            from_chips[l] = arrived
        send_layer = l
        g["norm_pre"][l] = dgain[0]
        g["a_log"][l], g["dt_bias"][l] = dgvec[0, NH:2 * NH], dgvec[1, NH:2 * NH]
        g["head_norm"][l] = dhn[0]
        g["ssm_d"][l], g["b_glu"][l], g["norm_post"][l] = dvecs[0], dvecs[1], dvecs[2]
        g["dlam"][l], g["dbblk"][l], g["dcblk"][l] = dlam, dbblk, dcblk
    if comm:
        for k in ("wcat", "wglu", "wout"):
            del g[k]
    g = {k: jnp.stack(v) for k, v in g.items()}
    g["from_chips"] = from_chips
    dlam, dbblk, dcblk = g.pop("dlam"), g.pop("dbblk"), g.pop("dcblk")
    g["ssm_c_re"], g["ssm_c_im"] = _block_diag_c_t(dcblk[:, 0]), _block_diag_c_t(dcblk[:, 1])
    dar, dai, dldt, dbr2, dbi2 = _s5_params_bwd(
        ar, ai, ldt, br2, bi2, dlam[:, 0].reshape(DEPTH, NG, NS), dlam[:, 1].reshape(DEPTH, NG, NS),
        _block_diag_b_t(dbblk[:, 0]), _block_diag_b_t(dbblk[:, 1]))
    g["ssm_a_re"], g["ssm_a_im"], g["ssm_log_dt"] = dar, dai, dldt.reshape(DEPTH, NG)
    g["ssm_b_re"] = dbr2.reshape(DEPTH, NG, NS, GS)
    g["ssm_b_im"] = dbi2.reshape(DEPTH, NG, NS, GS)
    return loss_part[0, 0], dx, g


def kernel(x, norm_pre, w_in, conv_w, a_log, dt_bias, head_norm, ssm_a_re, ssm_a_im, ssm_log_dt, ssm_b_re, ssm_b_im, ssm_c_re, ssm_c_im, ssm_d, w_glu, b_glu, w_out, norm_post, loss_target, m_norm_pre, m_w_in, m_conv_w, m_a_log, m_dt_bias, m_head_norm, m_ssm_a_re, m_ssm_a_im, m_ssm_log_dt, m_ssm_b_re, m_ssm_b_im, m_ssm_c_re, m_ssm_c_im, m_ssm_d, m_w_glu, m_b_glu, m_w_out, m_norm_post, v_norm_pre, v_w_in, v_conv_w, v_a_log, v_dt_bias, v_head_norm, v_ssm_a_re, v_ssm_a_im, v_ssm_log_dt, v_ssm_b_re, v_ssm_b_im, v_ssm_c_re, v_ssm_c_im, v_ssm_d, v_w_glu, v_b_glu, v_w_out, v_norm_post):
    args = dict(locals())
    small = {n: args[n] for n, _ in SMALL}

    def flat2(a):
        return a.reshape(-1, a.shape[-1])

    w_in16, w_glu16, w_out16 = w_in.astype(BF16), w_glu.astype(BF16), w_out.astype(BF16)

    def weight_parts(l):
        return [w_in16[l], w_glu16[l], w_out16[l]]

    def weights_from(parts):
        g_in, g_glu, g_out = parts[:3]
        return (_rearrange_cols(_cols_from_chips(g_in, 1)), g_glu.reshape(1, D, D), g_out.reshape(1, D, D))

    def grad_parts(gwcat, gwglu, gwout):
        return [_cols_by_chip(_restore_cols(gwcat[None])).astype(BF16), gwglu.reshape(NCHIP, ROW_SHARD, D).astype(BF16),
                gwout.reshape(NCHIP, ROW_SHARD, D).astype(BF16)]

    first = _exchange(weight_parts(0) + [flat2(conv_w)], CHIP_RELS, True, False, "gather_weights")
    conv = _cols_from_chips(first[3], DEPTH)
    comm = dict(weight_parts=weight_parts, weights_from=weights_from, grad_parts=grad_parts)
    loss_part, dx, g = _local_step(x[0], loss_target[0], [weights_from(first)], conv, small, comm)
    loss = lax.psum(loss_part, ("x", "y", "c"))

    from_chips = [jnp.concatenate([g["from_chips"][l][a] for l in range(DEPTH)], axis=1) for a in range(3)]
    core_sums = [_sum_slots(p, "sum_chips_" + n) for p, n in zip(from_chips, ("in", "glu", "out"))]
    others = _sibling_swap(core_sums, "swap_cores")
    sharded = {}
    for n, mine, other in zip(("w_in", "w_glu", "w_out"), core_sums, others):
        sharded[n] = _adamw(flat2(args[n]), [mine, other], flat2(args["m_" + n]), flat2(args["v_" + n]), "adamw_" + n,
                            max_rows=128)

    pad = jnp.zeros(((-(SMALL_ROWS + CONV_ROWS)) % (8 * NDEV), D), F32)
    small_sum = _all_reduce(_pack_small([g[n] for n, _ in SMALL], extra=[g["conv"], pad]), "reduce_small")
    small_out = _adamw(_pack_small([args[n] for n, _ in SMALL]), [small_sum],
                       _pack_small([args["m_" + n] for n, _ in SMALL]),
                       _pack_small([args["v_" + n] for n, _ in SMALL]), "adamw_small")
    chip = 2 * lax.axis_index("x") + lax.axis_index("y")
    conv_sum = small_sum[SMALL_ROWS:SMALL_ROWS + CONV_ROWS].reshape(DEPTH * 4, 3 * D)
    conv_sum = lax.dynamic_slice_in_dim(conv_sum, chip * CONV_SHARD, CONV_SHARD, axis=1)
    sharded["conv_w"] = _adamw(flat2(conv_w), [conv_sum], flat2(m_conv_w), flat2(v_conv_w), "adamw_conv")

    names = ["norm_pre", "w_in", "conv_w", "a_log", "dt_bias", "head_norm", "ssm_a_re", "ssm_a_im", "ssm_log_dt",
             "ssm_b_re", "ssm_b_im", "ssm_c_re", "ssm_c_im", "ssm_d", "w_glu", "b_glu", "w_out", "norm_post"]
    outs = [loss, dx[None]]
    for i in range(4):
        sm = dict(zip([n for n, _ in SMALL], _unpack_small(small_out[i])))
        outs += [sharded[n][i].reshape(args[n].shape) if n in sharded else sm[n] for n in names]
    return tuple(outs)
```

```python
import functools
import math

import jax
import jax.numpy as jnp
from jax import lax
from jax.experimental import pallas as pl
from jax.experimental.pallas import tpu as pltpu

F32 = jnp.float32
BF16 = jnp.bfloat16
HI = lax.Precision.HIGHEST

D = 1024
NH = 8
DH = 128
CH = 128
NG = 64
GS = 16
NS = 64
GPB = 8
NCB = NG // GPB
SW = GPB * NS
NCOL = 8320
BD0 = 8192
EPS = 1e-6
DEPTH = 4
NCHIP = 4
NDEV = 8
VMEM_LIMIT = 56 * 1024 * 1024
GRAD_ACT = jnp.bfloat16

ADAM_LR = 0.001
ADAM_B1 = 0.9
ADAM_B2 = 0.999
ADAM_EPS = 1e-08
ADAM_WD = 0.01
ADAM_STEP = 10


def _cparams(sem=None):
    return pltpu.CompilerParams(dimension_semantics=sem, vmem_limit_bytes=VMEM_LIMIT)


def _full(shape):
    nd = len(shape)
    return pl.BlockSpec(shape, lambda *_: (0,) * nd)


def _rms(x, gain):
    ms = jnp.mean(x * x, axis=-1, keepdims=True)
    return x * lax.rsqrt(ms + EPS) * gain


def _sigmoid(x):
    return 1.0 / (1.0 + jnp.exp(-x))


def _silu(x):
    return x * _sigmoid(x)


def _softplus(x):
    return jnp.maximum(x, 0.0) + jnp.log(1.0 + jnp.exp(-jnp.abs(x)))


def _gelu(x):
    return 0.5 * x * (1.0 + jnp.tanh(math.sqrt(2.0 / math.pi) * (x + 0.044715 * (x * x * x))))


def _dot_bf16(a, b, dims):
    return lax.dot_general(a.astype(BF16), b.astype(BF16), (dims, ((), ())), preferred_element_type=F32)


def _mm_nt(a, b):
    return _dot_bf16(a, b, ((1,), (1,)))


def _mm_tn(a, b):
    return _dot_bf16(a, b, ((0,), (0,)))


@jax.custom_vjp
def _mm(a, b):
    return _dot_bf16(a, b, ((1,), (0,)))


def _mm_fwd(a, b):
    return _dot_bf16(a, b, ((1,), (0,))), (a, b)


def _mm_bwd(res, ct):
    a, b = res
    return _mm_nt(ct, b).astype(a.dtype), _mm_tn(a, ct).astype(b.dtype)


_mm.defvjp(_mm_fwd, _mm_bwd)


@jax.custom_vjp
def _mm_nt_d(a, b):
    return _mm_nt(a, b)


def _mm_nt_d_bwd(res, ct):
    a, b = res
    return _dot_bf16(ct, b, ((1,), (0,))), _mm_tn(ct, a)


_mm_nt_d.defvjp(lambda a, b: (_mm_nt(a, b), (a, b)), _mm_nt_d_bwd)


@jax.custom_vjp
def _mm_tn_d(a, b):
    return _mm_tn(a, b)


def _mm_tn_d_bwd(res, ct):
    a, b = res
    return _mm_nt(b, ct), _dot_bf16(a, ct, ((1,), (0,)))


_mm_tn_d.defvjp(lambda a, b: (_mm_tn(a, b), (a, b)), _mm_tn_d_bwd)


def _split_bf16(a):
    hi = a.astype(BF16)
    return hi, (a - hi.astype(F32)).astype(BF16)


def _dot3(a, b, dims):
    ah, al = _split_bf16(a)
    bh, bl = _split_bf16(b)

    def dot(x, y):
        return lax.dot_general(x, y, (dims, ((), ())), preferred_element_type=F32)

    return dot(ah, bh) + (dot(ah, bl) + dot(al, bh))


@jax.custom_vjp
def _imm(a, b):
    return _dot3(a, b, ((1,), (0,)))


def _imm_bwd(res, ct):
    a, b = res
    return _dot3(ct, b, ((1,), (1,))), _dot3(a, ct, ((0,), (0,)))


_imm.defvjp(lambda a, b: (_dot3(a, b, ((1,), (0,))), (a, b)), _imm_bwd)


def _hmm(a, b):
    return jnp.dot(a, b, precision=HI, preferred_element_type=F32)


def _hmm_nt(a, b):
    return lax.dot_general(a, b, (((1,), (1,)), ((), ())), precision=HI, preferred_element_type=F32)


def _hmm_tn(a, b):
    return lax.dot_general(a, b, (((0,), (0,)), ((), ())), precision=HI, preferred_element_type=F32)


def _rows(shape):
    return lax.broadcasted_iota(jnp.int32, shape, 0)


def _cols(shape):
    return lax.broadcasted_iota(jnp.int32, shape, 1)


def _sd(x, s):
    return jnp.where(_rows(x.shape) >= s, pltpu.roll(x, s, axis=0), 0.0)


def _su(x, s):
    n = x.shape[0]
    return jnp.where(_rows(x.shape) < n - s, pltpu.roll(x, n - s, axis=0), 0.0)


@functools.partial(jax.custom_vjp, nondiff_argnums=(1,))
def _shift_down(x, s):
    return _sd(x, s)


def _shift_down_fwd(x, s):
    return _sd(x, s), None


def _shift_down_bwd(s, _, g):
    return (_su(g, s),)


_shift_down.defvjp(_shift_down_fwd, _shift_down_bwd)


def _last_row(x):
    n = x.shape[0]
    return jnp.sum(jnp.where(_rows(x.shape) == n - 1, x, 0.0), axis=0, keepdims=True)


def _prep_fn(p, w0, w1, w2, w3, qk):
    acc = w3 * p + w2 * _shift_down(p, 1) + w1 * _shift_down(p, 2) + w0 * _shift_down(p, 3)
    a = _silu(acc)
    nrm = lax.rsqrt(jnp.sum(a * a, axis=-1, keepdims=True) + EPS)
    return a * (nrm * qk + (1.0 - qk))


def _gates_fn(bd, av, bv):
    tm = bd.shape[0]
    beta_all = _sigmoid(bd)
    g_all = -jnp.exp(av) * _softplus(bd + bv)
    r, c = _rows((tm, tm)), _cols((tm, tm))
    tri = jnp.where((r // CH == c // CH) & (r >= c), 1.0, 0.0).astype(F32)
    gc_all = _hmm(tri, g_all)
    lane = _cols(bd.shape)
    outs = []
    for h in range(NH):
        b = jnp.sum(jnp.where(lane == h, beta_all, 0.0), axis=1, keepdims=True)
        outs.append(jnp.broadcast_to(b, bd.shape))
    for h in range(NH):
        g = jnp.sum(jnp.where(lane == NH + h, gc_all, 0.0), axis=1, keepdims=True)
        outs.append(jnp.broadcast_to(g, bd.shape))
    return tuple(outs)


INV_BASE = 2


def _merge_mm(a, b):
    return _dot_bf16(a, b, ((1,), (0,)))


def _unit_lower_inv(l_mats):
    n = l_mats[0].shape[0]
    ii, jj = _rows((n, n)), _cols((n, n))
    base = ii // INV_BASE == jj // INV_BASE
    ps = [-jnp.where(base, l_mat, 0.0) for l_mat in l_mats]
    eye = jnp.where(ii == jj, 1.0, 0.0).astype(F32)
    ds = [eye + p for p in ps]
    k = 1
    while 2 * k < INV_BASE:
        ps = [_imm(p, p) for p in ps]
        ds = [d + _imm(d, p) for d, p in zip(ds, ps)]
        k *= 2
    m = INV_BASE
    while m < n:
        pair = (ii // (2 * m) == jj // (2 * m)) & (ii // m > jj // m)
        des = [_merge_mm(d, jnp.where(pair, l_mat, 0.0)) for d, l_mat in zip(ds, l_mats)]
        ds = [d - _merge_mm(de, d) for d, de in zip(ds, des)]
        m *= 2
    return ds


@jax.custom_vjp
def _known_inverse(l_mat, t_inv):
    return t_inv


def _known_inverse_bwd(t_inv, ct):
    d_l = -_dot3(_dot3(t_inv, ct, ((0,), (0,))), t_inv, ((1,), (1,)))
    return d_l, jnp.zeros_like(t_inv)


_known_inverse.defvjp(lambda l_mat, t_inv: (t_inv, t_inv), _known_inverse_bwd)


def _chunk_system(q, k, v, bb, gcb):
    qs = q * (DH ** -0.5)
    kb = k * bb
    eg = jnp.exp(gcb)
    ii, jj = _rows((CH, CH)), _cols((CH, CH))
    decay = jnp.exp(jnp.where(ii >= jj, gcb - gcb.T, -1e30))
    l_mat = jnp.where(ii > jj, _mm_nt_d(kb, k) * decay, 0.0)
    a_qk = _mm_nt_d(qs, k) * decay
    k_dec = k * jnp.exp(_last_row(gcb) - gcb)
    return l_mat, (v * bb, kb * eg, qs * eg, k_dec, a_qk)


def _chunk_solve(t_inv, rest):
    vb, kbe, q_dec, k_dec, a_qk = rest
    return _mm(t_inv, vb), _mm(t_inv, kbe), q_dec, k_dec, a_qk


def _side_by_side_vjp(fn, items, cts):
    n = len(items[0])
    _, vjp = jax.vjp(lambda *flat: fn([flat[i * n:(i + 1) * n] for i in range(len(items))]),
                     *[a for item in items for a in item])
    grads = vjp(cts)
    return [grads[i * n:(i + 1) * n] for i in range(len(items))]


def _chunks_local_known(items):
    systems = [_chunk_system(*item[:5]) for item in items]
    t_invs = [_known_inverse(l_mat, item[5]) for (l_mat, _), item in zip(systems, items)]
    return [_chunk_solve(t_inv, rest) for t_inv, (_, rest) in zip(t_invs, systems)]


def _chunks_local(chunks):
    systems = [_chunk_system(*c) for c in chunks]
    t_invs = _unit_lower_inv([l_mat for l_mat, _ in systems])
    return [(_chunk_solve(t_inv, rest), t_inv) for t_inv, (_, rest) in zip(t_invs, systems)]


def _state_steps(items):
    v_news = [u - _mm(w, state) for u, w, _, _, _, _, state in items]
    outs = [_mm(q_dec, state) + _mm(a_qk, v_new) for (_, _, q_dec, _, a_qk, _, state), v_new in zip(items, v_news)]
    states = [state * jnp.exp(_last_row(gcb)) + _mm_tn_d(k_dec, v_new)
              for (_, _, _, k_dec, _, gcb, state), v_new in zip(items, v_news)]
    return list(zip(outs, states))


SUB = 8


def _cmul(ar, ai, br, bi):
    return ar * br - ai * bi, ar * bi + ai * br


def _scan_tile(xr, xi, mr, mi, hr_ref, hi_ref, cr_ref, ci_ref, reverse):
    n, width = xr.shape
    ngroups = n // SUB
    shift_groups = _su if reverse else _sd
    xr, xi = xr.reshape(ngroups, SUB, width), xi.reshape(ngroups, SUB, width)
    pr, pi = mr, mi
    tr, ti = jnp.broadcast_to(mr, (SUB, width)), jnp.broadcast_to(mi, (SUB, width))
    pos = _rows(tr.shape)
    s = 1
    while s < SUB:
        inside = pos < SUB - s if reverse else pos >= s
        shift = SUB - s if reverse else s
        qr, qi = jnp.where(inside, pr, 0.0)[None], jnp.where(inside, pi, 0.0)[None]
        dr, di = _cmul(qr, qi, pltpu.roll(xr, shift, axis=1), pltpu.roll(xi, shift, axis=1))
        xr, xi = xr + dr, xi + di
        er = jnp.where(inside, pltpu.roll(tr, shift, axis=0), 1.0)
        ei = jnp.where(inside, pltpu.roll(ti, shift, axis=0), 0.0)
        tr, ti = _cmul(tr, ti, er, ei)
        pr, pi = _cmul(pr, pi, pr, pi)
        s *= 2
    xr, xi = xr.reshape(n, width), xi.reshape(n, width)
    nlb = width // DH

    def lanes(x, j):
        return x[:, j * DH:(j + 1) * DH]

    for j in range(nlb):
        hr_ref[j] = lanes(xr, j)
        hi_ref[j] = lanes(xi, j)
    edge = pl.ds(0 if reverse else SUB - 1, ngroups, stride=SUB)
    gr = jnp.concatenate([hr_ref.at[j][edge, :] for j in range(nlb)], axis=1)
    gi = jnp.concatenate([hi_ref.at[j][edge, :] for j in range(nlb)], axis=1)
    s = 1
    while s < ngroups:
        dr, di = _cmul(pr, pi, shift_groups(gr, s), shift_groups(gi, s))
        gr, gi = gr + dr, gi + di
        pr, pi = _cmul(pr, pi, pr, pi)
        s *= 2
    cr_ref[...] = shift_groups(gr, 1)
    ci_ref[...] = shift_groups(gi, 1)
    for g in range(ngroups):
        rows = slice(g * SUB, (g + 1) * SUB)
        dr, di = _cmul(tr, ti, cr_ref[g:g + 1, :], ci_ref[g:g + 1, :])
        for j in range(nlb):
            hr_ref[j, rows, :] += lanes(dr, j)
            hi_ref[j, rows, :] += lanes(di, j)
    return (jnp.concatenate([hr_ref[j] for j in range(nlb)], axis=1),
            jnp.concatenate([hi_ref[j] for j in range(nlb)], axis=1))


def _s5_states(u, lam_ref, b_ref, car_ref, hr_ref, hi_ref, cr_ref, ci_ref):
    lr, li = lam_ref[0], lam_ref[1]
    first = _rows((u.shape[0], SW)) == 0
    inr, ini = _cmul(lr, li, car_ref[0:1, :], car_ref[1:2, :])
    xr = _mm(u, b_ref[0]) + jnp.where(first, inr, 0.0)
    xi = _mm(u, b_ref[1]) + jnp.where(first, ini, 0.0)
    return _scan_tile(xr, xi, lr, li, hr_ref, hi_ref, cr_ref, ci_ref, False)


def _s5_params_fn(ar, ai, ldt, br2, bi2):
    dt = jnp.exp(ldt)
    mag = jnp.exp(ar * dt)
    lr, li = mag * jnp.cos(ai * dt), mag * jnp.sin(ai * dt)
    den = ar * ar + ai * ai
    fr = ((lr - 1.0) * ar + li * ai) / den
    fi = (li * ar - (lr - 1.0) * ai) / den
    expand = jnp.where(_cols((NS, NS * GS)) // GS == _rows((NS, NS * GS)), 1.0, 0.0).astype(F32)
    fr2, fi2 = _hmm(fr, expand), _hmm(fi, expand)
    return lr, li, fr2 * br2 - fi2 * bi2, fr2 * bi2 + fi2 * br2


def _head_norm(o, hn):
    parts = []
    for h in range(NH):
        oh = o[:, h * DH:(h + 1) * DH]
        parts.append(oh * lax.rsqrt(jnp.mean(oh * oh, axis=-1, keepdims=True) + EPS) * hn)
    return jnp.concatenate(parts, axis=1)


def _mix_pre(s5y, u, dvec):
    return _gelu(s5y + dvec * u)


def _mix_mid(o, za, y0, gl, zb, ra, rb, hn):
    ya = _head_norm(o, hn) * _silu(za)
    yb = y0 * _sigmoid(gl) * _silu(zb)
    return _sigmoid(ra) * ya + _sigmoid(rb) * yb


def _mix_post(x, out, npost):
    return x + _rms(out, npost)


def _tile(t, want):
    return min(t, want)


def _row_tile(rows, want):
    return max(r for r in range(16, want + 1, 16) if rows % r == 0)


def _call_carrying(body, name, grid, in_specs, out_specs, out_shape, scratch, args, semantics, exchange):
    n_out = len(out_shape)
    if exchange is not None:
        body = _carry(body, len(args), n_out, len(scratch), exchange, grid)
        in_specs, out_specs = in_specs + exchange.in_specs, out_specs + exchange.out_specs
        out_shape, scratch, args = out_shape + exchange.out_shape, scratch + exchange.scratch_shapes, args + exchange.srcs
        semantics = ("arbitrary",) * len(grid)
    outs = pl.pallas_call(body, name=name, grid=grid, in_specs=in_specs, out_specs=out_specs, out_shape=out_shape,
                          scratch_shapes=scratch, compiler_params=_cparams(semantics))(*args)
    return outs[:n_out], outs[n_out:]


def _inproj_fwd(x, gain, wcat, l, exchange=None):
    t = x.shape[0]
    tm, tn = _tile(t, 1024), 640

    def body(x_ref, g_ref, w_ref, o_ref, h_ref):
        @pl.when(pl.program_id(1) == 0)
        def _():
            h_ref[...] = _rms(x_ref[...], g_ref[...]).astype(h_ref.dtype)
        o_ref[...] = _dot_bf16(h_ref[...], w_ref[...], ((1,), (0,)))

    (proj, h), fetched = _call_carrying(
        body, "inproj_fwd", (t // tm, NCOL // tn),
        [pl.BlockSpec((tm, D), lambda i, j: (i, 0)), _full((1, D)), pl.BlockSpec((None, D, tn), lambda i, j: (l, 0, j))],
        [pl.BlockSpec((tm, tn), lambda i, j: (i, j)), pl.BlockSpec((tm, D), lambda i, j: (i, 0))],
        [jax.ShapeDtypeStruct((t, NCOL), F32), jax.ShapeDtypeStruct((t, D), wcat.dtype)],
        [], [x, gain, wcat], ("parallel", "arbitrary"), exchange)
    return proj, h, fetched


def _inproj_bwd_dx(dproj, wcat, x, gain, dxres, l, exchange=None):
    t = x.shape[0]
    tm, tk = _tile(t, 1024), 640
    nk = NCOL // tk

    def body(dp_ref, w_ref, x_ref, g_ref, r_ref, dx_ref, dg_ref, acc_ref):
        i, k = pl.program_id(0), pl.program_id(1)

        @pl.when(k == 0)
        def _():
            acc_ref[...] = jnp.zeros_like(acc_ref)

        acc_ref[...] += _mm_nt(dp_ref[...], w_ref[...])

        @pl.when(k == nk - 1)
        def _():
            _, vjp = jax.vjp(_rms, x_ref[...], g_ref[...])
            dx, dg = vjp(acc_ref[...])
            dx_ref[...] = r_ref[...] + dx

            @pl.when(i == 0)
            def _():
                dg_ref[...] = dg

            @pl.when(i > 0)
            def _():
                dg_ref[...] += dg

    grid = (t // tm, nk)
    in_specs = [pl.BlockSpec((tm, tk), lambda i, k: (i, k)), pl.BlockSpec((None, D, tk), lambda i, k: (l, 0, k)),
                pl.BlockSpec((tm, D), lambda i, k: (i, 0)), _full((1, D)), pl.BlockSpec((tm, D), lambda i, k: (i, 0))]
    out_specs = [pl.BlockSpec((tm, D), lambda i, k: (i, 0)), _full((1, D))]
    out_shape = [jax.ShapeDtypeStruct((t, D), F32), jax.ShapeDtypeStruct((1, D), F32)]
    scratch, args = [pltpu.VMEM((tm, D), F32)], [dproj, wcat, x, gain, dxres]
    if exchange is not None:
        body = _carry(body, len(args), len(out_shape), len(scratch), exchange, grid)
        in_specs, out_specs = in_specs + exchange.in_specs, out_specs + exchange.out_specs
        out_shape, scratch, args = out_shape + exchange.out_shape, scratch + exchange.scratch_shapes, args + exchange.srcs
    outs = pl.pallas_call(
        body, name="inproj_bwd_dx", grid=grid, in_specs=in_specs, out_specs=out_specs, out_shape=out_shape,
        scratch_shapes=scratch, compiler_params=_cparams(("arbitrary", "arbitrary")),
    )(*args)
    return outs[0], outs[1], outs[2:]


def _inproj_bwd_dw(h, dproj):
    t = h.shape[0]
    tm, tn = _tile(t, 512), 1664

    def body(h_ref, dp_ref, o_ref):
        @pl.when(pl.program_id(1) == 0)
        def _():
            o_ref[...] = jnp.zeros_like(o_ref)

        o_ref[...] += _mm_tn(h_ref[...], dp_ref[...])

    return pl.pallas_call(
        body, name="inproj_bwd_dw", grid=(NCOL // tn, t // tm),
        in_specs=[pl.BlockSpec((tm, D), lambda j, i: (i, 0)), pl.BlockSpec((tm, tn), lambda j, i: (i, j))],
        out_specs=pl.BlockSpec((D, tn), lambda j, i: (0, j)),
        out_shape=jax.ShapeDtypeStruct((D, NCOL), F32),
        compiler_params=_cparams(("parallel", "arbitrary")),
    )(h, dproj)


def _prep_fwd(proj, cw):
    t = proj.shape[0]

    def body(p_ref, w_ref, o_ref):
        qk = (pl.program_id(0) < 2 * NH).astype(F32)
        o_ref[...] = _prep_fn(p_ref[...], w_ref[0:1, :], w_ref[1:2, :], w_ref[2:3, :], w_ref[3:4, :], qk)

    return pl.pallas_call(
        body, name="prep_fwd", grid=(3 * NH,),
        in_specs=[pl.BlockSpec((t, DH), lambda c: (0, c)), pl.BlockSpec((4, DH), lambda c: (0, c))],
        out_specs=pl.BlockSpec((None, t, DH), lambda c: (c, 0, 0)),
        out_shape=jax.ShapeDtypeStruct((3 * NH, t, DH), F32),
        compiler_params=_cparams(("parallel",)),
    )(proj, cw)


def _prep_bwd(proj, cw, dq, dk, dv):
    t = proj.shape[0]

    def body(p_ref, w_ref, dq_ref, dk_ref, dv_ref, dp_ref, dw_ref):
        c = pl.program_id(0)
        qk = (c < 2 * NH).astype(F32)
        _, vjp = jax.vjp(lambda p, w0, w1, w2, w3: _prep_fn(p, w0, w1, w2, w3, qk),
                         p_ref[...], w_ref[0:1, :], w_ref[1:2, :], w_ref[2:3, :], w_ref[3:4, :])
        d = jnp.where(c < NH, dq_ref[...], jnp.where(c < 2 * NH, dk_ref[...], dv_ref[...]))
        dp, dw0, dw1, dw2, dw3 = vjp(d)
        dp_ref[...] = dp.astype(dp_ref.dtype)
        dw_ref[0:1, :] = dw0
        dw_ref[1:2, :] = dw1
        dw_ref[2:3, :] = dw2
        dw_ref[3:4, :] = dw3

    return pl.pallas_call(
        body, name="prep_bwd", grid=(3 * NH,),
        in_specs=[pl.BlockSpec((t, DH), lambda c: (0, c)), pl.BlockSpec((4, DH), lambda c: (0, c))]
        + [pl.BlockSpec((None, t, DH), functools.partial(lambda c, off: (jnp.clip(c - off, 0, NH - 1), 0, 0), off=off))
           for off in (0, NH, 2 * NH)],
        out_specs=[pl.BlockSpec((t, DH), lambda c: (0, c)), pl.BlockSpec((4, DH), lambda c: (0, c))],
        out_shape=[jax.ShapeDtypeStruct((t, 3 * D), GRAD_ACT), jax.ShapeDtypeStruct((4, 3 * D), F32)],
        compiler_params=_cparams(("arbitrary",)),
    )(proj, cw, dq, dk, dv)


def _gates_fwd(proj, gvec):
    t = proj.shape[0]
    tm = _tile(t, 512)

    def body(p_ref, gv_ref, b_ref, g_ref):
        outs = _gates_fn(p_ref[...], gv_ref[0:1, :], gv_ref[1:2, :])
        for h in range(NH):
            b_ref[h] = outs[h]
            g_ref[h] = outs[NH + h]

    spec = pl.BlockSpec((NH, tm, DH), lambda i: (0, i, 0))
    return pl.pallas_call(
        body, name="gates_fwd", grid=(t // tm,),
        in_specs=[pl.BlockSpec((tm, DH), lambda i: (i, BD0 // DH)), _full((8, DH))],
        out_specs=[spec, spec],
        out_shape=[jax.ShapeDtypeStruct((NH, t, DH), F32)] * 2,
        compiler_params=_cparams(("parallel",)),
    )(proj, gvec)


def _gates_bwd(proj, gvec, dbb, dgcb):
    t = proj.shape[0]
    tm = _tile(t, 512)

    def body(p_ref, gv_ref, db_ref, dg_ref, dp_ref, dgv_ref):
        _, vjp = jax.vjp(_gates_fn, p_ref[...], gv_ref[0:1, :], gv_ref[1:2, :])
        cts = tuple(db_ref[h] for h in range(NH)) + tuple(dg_ref[h] for h in range(NH))
        dp, da, db = vjp(cts)
        dp_ref[...] = dp.astype(dp_ref.dtype)

        @pl.when(pl.program_id(0) == 0)
        def _():
            dgv_ref[...] = jnp.zeros_like(dgv_ref)

        dgv_ref[0:1, :] += da
        dgv_ref[1:2, :] += db

    spec = pl.BlockSpec((NH, tm, DH), lambda i: (0, i, 0))
    return pl.pallas_call(
        body, name="gates_bwd", grid=(t // tm,),
        in_specs=[pl.BlockSpec((tm, DH), lambda i: (i, BD0 // DH)), _full((8, DH)), spec, spec],
        out_specs=[pl.BlockSpec((tm, DH), lambda i: (i, 0)), _full((8, DH))],
        out_shape=[jax.ShapeDtypeStruct((t, DH), GRAD_ACT), jax.ShapeDtypeStruct((8, DH), F32)],
        compiler_params=_cparams(("arbitrary",)),
    )(proj, gvec, dbb, dgcb)


def _chunks_per_step(nch):
    return max(c for c in (8, 4, 2, 1) if nch % c == 0)


def _grid_ends(grid):
    def first():
        return functools.reduce(jnp.logical_and, [pl.program_id(a) == 0 for a in range(len(grid))])

    def last():
        return functools.reduce(jnp.logical_and, [pl.program_id(a) == n - 1 for a, n in enumerate(grid)])

    return first, last


def _carry(body, n_in, n_out, n_scratch, exchange, grid):
    first, last = _grid_ends(grid)
    na = exchange.narr

    def wrapped(*refs):
        a, b = n_in, n_in + na
        c, d = b + n_out, b + n_out + na
        e = d + n_scratch
        srcs, dsts, sems = refs[a:b], refs[c:d], refs[e:]

        @pl.when(first())
        def _():
            exchange.start(srcs, dsts, sems)

        body(*(refs[:a] + refs[b:c] + refs[d:e]))

        @pl.when(last())
        def _():
            exchange.wait(srcs, dsts, sems)

    return wrapped


def _delta_local_fwd(qkv, bb, gcb, exchange=None):
    t = qkv.shape[1]
    cps = _chunks_per_step(t // CH)
    rows = cps * CH
    grid = (NH, t // rows)

    def body(q_ref, k_ref, v_ref, b_ref, g_ref, *out_refs):
        slices = [slice(c * CH, (c + 1) * CH) for c in range(cps)]
        results = _chunks_local([tuple(ref[sl, :] for ref in (q_ref, k_ref, v_ref, b_ref, g_ref)) for sl in slices])
        for sl, (outs, t_inv) in zip(slices, results):
            for ref, val in zip(out_refs, outs + (t_inv,)):
                ref[sl, :] = val

    def blk(off):
        return pl.BlockSpec((None, rows, DH), lambda h, n: (h + off, n, 0))

    in_specs = [blk(0), blk(NH), blk(2 * NH), blk(0), blk(0)]
    out_specs = [blk(0)] * 6
    out_shape = [jax.ShapeDtypeStruct((NH, t, DH), F32)] * 6
    args, scratch, sem = [qkv, qkv, qkv, bb, gcb], [], ("parallel", "parallel")
    if exchange is not None:
        body = _carry(body, 5, 6, 0, exchange, grid)
        in_specs, out_specs = in_specs + exchange.in_specs, out_specs + exchange.out_specs
        out_shape, scratch, args = out_shape + exchange.out_shape, exchange.scratch_shapes, args + exchange.srcs
        sem = ("arbitrary", "arbitrary")
    outs = pl.pallas_call(
        body, name="delta_local_fwd", grid=grid, in_specs=in_specs, out_specs=out_specs, out_shape=out_shape,
        scratch_shapes=scratch, compiler_params=_cparams(sem),
    )(*args)
    return outs[:5], outs[5], outs[6:]


def _delta_local_bwd(qkv, bb, gcb, t_inv, cts, dgcb_state):
    t = qkv.shape[1]
    cps = _chunks_per_step(t // CH)
    rows = cps * CH

    def body(q_ref, k_ref, v_ref, b_ref, g_ref, ti_ref, du_ref, dw_ref, dqd_ref, dkd_ref, da_ref, dgs_ref,
             dq_ref, dk_ref, dv_ref, db_ref, dg_ref):
        slices = [slice(c * CH, (c + 1) * CH) for c in range(cps)]
        items = [tuple(ref[sl, :] for ref in (q_ref, k_ref, v_ref, b_ref, g_ref, ti_ref)) for sl in slices]
        cts = [tuple(ref[sl, :] for ref in (du_ref, dw_ref, dqd_ref, dkd_ref, da_ref)) for sl in slices]
        for sl, (dq, dk, dv, db, dg, _) in zip(slices, _side_by_side_vjp(_chunks_local_known, items, cts)):
            dq_ref[sl, :] = dq
            dk_ref[sl, :] = dk
            dv_ref[sl, :] = dv
            db_ref[sl, :] = db
            dg_ref[sl, :] = dg + dgs_ref[sl, :]

    def blk(off):
        return pl.BlockSpec((None, rows, DH), lambda h, n: (h + off, n, 0))

    return pl.pallas_call(
        body, name="delta_local_bwd", grid=(NH, t // rows),
        in_specs=[blk(0), blk(NH), blk(2 * NH)] + [blk(0)] * 9,
        out_specs=[blk(0)] * 5,
        out_shape=[jax.ShapeDtypeStruct((NH, t, DH), F32)] * 5,
        compiler_params=_cparams(("parallel", "parallel")),
    )(qkv, qkv, qkv, bb, gcb, t_inv, *cts, dgcb_state)


def _delta_state_fwd(local, gcb):
    t = gcb.shape[1]
    nch = t // CH

    def body(u_ref, w_ref, qd_ref, kd_ref, a_ref, g_ref, o_ref, s_ref, st_ref):
        @pl.when(pl.program_id(0) == 0)
        def _():
            st_ref[...] = jnp.zeros_like(st_ref)

        s_ref[...] = st_ref[...]
        items = [tuple(ref[h] for ref in (u_ref, w_ref, qd_ref, kd_ref, a_ref, g_ref, st_ref)) for h in range(NH)]
        for h, (o, ns) in enumerate(_state_steps(items)):
            o_ref[:, h * DH:(h + 1) * DH] = o
            st_ref[h] = ns

    blk = pl.BlockSpec((NH, CH, DH), lambda n: (0, n, 0))
    return pl.pallas_call(
        body, name="delta_state_fwd", grid=(nch,),
        in_specs=[blk] * 6,
        out_specs=[pl.BlockSpec((CH, D), lambda n: (n, 0)),
                   pl.BlockSpec((NH, None, DH, DH), lambda n: (0, n, 0, 0))],
        out_shape=[jax.ShapeDtypeStruct((t, D), F32), jax.ShapeDtypeStruct((NH, nch, DH, DH), F32)],
        scratch_shapes=[pltpu.VMEM((NH, DH, DH), F32)],
        compiler_params=_cparams(("arbitrary",)),
    )(*local, gcb)


def _delta_state_bwd(local, gcb, states, do):
    t = gcb.shape[1]
    nch = t // CH

    def body(u_ref, w_ref, qd_ref, kd_ref, a_ref, g_ref, s_ref, do_ref,
             du_ref, dw_ref, dqd_ref, dkd_ref, da_ref, dg_ref, ds_ref):
        @pl.when(pl.program_id(0) == 0)
        def _():
            ds_ref[...] = jnp.zeros_like(ds_ref)

        items = [tuple(ref[h] for ref in (u_ref, w_ref, qd_ref, kd_ref, a_ref, g_ref, s_ref)) for h in range(NH)]
        cts = [(do_ref[:, h * DH:(h + 1) * DH], ds_ref[h]) for h in range(NH)]
        for h, (du, dw, dqd, dkd, da, dg, ds) in enumerate(_side_by_side_vjp(_state_steps, items, cts)):
            du_ref[h] = du
            dw_ref[h] = dw
            dqd_ref[h] = dqd
            dkd_ref[h] = dkd
            da_ref[h] = da
            dg_ref[h] = dg
            ds_ref[h] = ds

    blk = pl.BlockSpec((NH, CH, DH), lambda n: (0, nch - 1 - n, 0))
    return pl.pallas_call(
        body, name="delta_state_bwd", grid=(nch,),
        in_specs=[blk] * 6 + [pl.BlockSpec((NH, None, DH, DH), lambda n: (0, nch - 1 - n, 0, 0)),
                              pl.BlockSpec((CH, D), lambda n: (nch - 1 - n, 0))],
        out_specs=[blk] * 6,
        out_shape=[jax.ShapeDtypeStruct((NH, t, DH), F32)] * 6,
        scratch_shapes=[pltpu.VMEM((NH, DH, DH), F32)],
        compiler_params=_cparams(("arbitrary",)),
    )(*local, gcb, states, do)


def _s5_params(ar, ai, ldt, br2, bi2):
    def body(ar_ref, ai_ref, ld_ref, br_ref, bi_ref, lr_ref, li_ref, bbr_ref, bbi_ref):
        lr, li, bbr, bbi = _s5_params_fn(ar_ref[...], ai_ref[...], ld_ref[...], br_ref[...], bi_ref[...])
        lr_ref[...] = lr
        li_ref[...] = li
        bbr_ref[...] = bbr
        bbi_ref[...] = bbi

    sq = pl.BlockSpec((None, NG, NS), lambda l: (l, 0, 0))
    wide = pl.BlockSpec((None, NG, NS * GS), lambda l: (l, 0, 0))
    return pl.pallas_call(
        body, name="s5_params", grid=(DEPTH,),
        in_specs=[sq, sq, pl.BlockSpec((None, NG, 1), lambda l: (l, 0, 0)), wide, wide],
        out_specs=[sq, sq, wide, wide],
        out_shape=[jax.ShapeDtypeStruct((DEPTH, NG, NS), F32)] * 2
        + [jax.ShapeDtypeStruct((DEPTH, NG, NS * GS), F32)] * 2,
        compiler_params=_cparams(("parallel",)),
    )(ar, ai, ldt, br2, bi2)


def _s5_params_bwd(ar, ai, ldt, br2, bi2, dlr, dli, dbbr, dbbi):
    def body(ar_ref, ai_ref, ld_ref, br_ref, bi_ref, a_ref, b_ref, c_ref, d_ref,
             dar_ref, dai_ref, dld_ref, dbr_ref, dbi_ref):
        _, vjp = jax.vjp(_s5_params_fn, ar_ref[...], ai_ref[...], ld_ref[...], br_ref[...], bi_ref[...])
        dar, dai, dld, dbr, dbi = vjp((a_ref[...], b_ref[...], c_ref[...], d_ref[...]))
        dar_ref[...] = dar
        dai_ref[...] = dai
        dld_ref[...] = dld
        dbr_ref[...] = dbr
        dbi_ref[...] = dbi

    sq = pl.BlockSpec((None, NG, NS), lambda l: (l, 0, 0))
    col = pl.BlockSpec((None, NG, 1), lambda l: (l, 0, 0))
    wide = pl.BlockSpec((None, NG, NS * GS), lambda l: (l, 0, 0))
    return pl.pallas_call(
        body, name="s5_params_bwd", grid=(DEPTH,),
        in_specs=[sq, sq, col, wide, wide, sq, sq, wide, wide],
        out_specs=[sq, sq, col, wide, wide],
        out_shape=[jax.ShapeDtypeStruct((DEPTH, NG, NS), F32)] * 2 + [jax.ShapeDtypeStruct((DEPTH, NG, 1), F32)]
        + [jax.ShapeDtypeStruct((DEPTH, NG, NS * GS), F32)] * 2,
        compiler_params=_cparams(("parallel",)),
    )(ar, ai, ldt, br2, bi2, dlr, dli, dbbr, dbbi)


def _s5_tile_rows(t):
    return _tile(t // 2, 1024)


def _s5_fwd(proj, lam, bblk, cblk, exchange=None):
    t = proj.shape[0]
    r = _s5_tile_rows(t)
    nt = t // r
    u0 = 4 * D // DH

    def body(u_ref, lam_ref, b_ref, c_ref, y_ref, car_ref, st_ref, hr_ref, hi_ref, cr_ref, ci_ref):
        @pl.when(pl.program_id(1) == 0)
        def _():
            st_ref[...] = jnp.zeros_like(st_ref)

        car_ref[...] = st_ref[...]
        hr, hi = _s5_states(u_ref[...], lam_ref, b_ref, st_ref, hr_ref, hi_ref, cr_ref, ci_ref)
        y_ref[...] = _mm(hr, c_ref[0]) - _mm(hi, c_ref[1])
        st_ref[0:1, :] = _last_row(hr)
        st_ref[1:2, :] = _last_row(hi)

    scratch = ([pltpu.VMEM((8, SW), F32)] + [pltpu.VMEM((SW // DH, r, DH), F32)] * 2
               + [pltpu.VMEM((r // SUB, SW), F32)] * 2)
    (y, carries), fetched = _call_carrying(
        body, "s5_fwd", (NCB, nt),
        [pl.BlockSpec((r, DH), lambda c, i: (i, u0 + c)), pl.BlockSpec((2, 1, SW), lambda c, i: (0, 0, c)),
         pl.BlockSpec((2, None, DH, SW), lambda c, i: (0, c, 0, 0)),
         pl.BlockSpec((2, None, SW, DH), lambda c, i: (0, c, 0, 0))],
        [pl.BlockSpec((r, DH), lambda c, i: (i, c)), pl.BlockSpec((None, 8, SW), lambda c, i: (i, 0, c))],
        [jax.ShapeDtypeStruct((t, D), F32), jax.ShapeDtypeStruct((nt, 8, NG * NS), F32)],
        scratch, [proj, lam, bblk, cblk], ("parallel", "arbitrary"), exchange)
    return y, carries, fetched


def _s5_bwd(proj, lam, bblk, cblk, carries, dy, du_skip, exchange=None):
    t = proj.shape[0]
    r = _s5_tile_rows(t)
    nt = t // r
    u0 = 4 * D // DH

    def body(u_ref, lam_ref, b_ref, c_ref, car_ref, dy_ref, dus_ref, du_ref, dlam_ref, db_ref, dc_ref, dst_ref,
             hr_ref, hi_ref, ar_ref, ai_ref, cr_ref, ci_ref):
        first = pl.program_id(1) == 0

        @pl.when(first)
        def _():
            dst_ref[...] = jnp.zeros_like(dst_ref)

        u, dy = u_ref[...], dy_ref[...]
        lr, li = lam_ref[0], lam_ref[1]
        hr, hi = _s5_states(u, lam_ref, b_ref, car_ref, hr_ref, hi_ref, cr_ref, ci_ref)
        dcr2, dci2 = _mm_tn(hr, dy), -_mm_tn(hi, dy)
        last = _rows((r, SW)) == r - 1
        inr, ini = _cmul(lr, -li, dst_ref[0:1, :], dst_ref[1:2, :])
        dhr = _mm_nt(dy, c_ref[0]) + jnp.where(last, inr, 0.0)
        dhi = jnp.where(last, ini, 0.0) - _mm_nt(dy, c_ref[1])
        ar, ai = _scan_tile(dhr, dhi, lr, -li, ar_ref, ai_ref, cr_ref, ci_ref, True)
        top = _rows((r, SW)) == 0
        dst_ref[0:1, :] = jnp.sum(jnp.where(top, ar, 0.0), axis=0, keepdims=True)
        dst_ref[1:2, :] = jnp.sum(jnp.where(top, ai, 0.0), axis=0, keepdims=True)
        du_ref[...] = (_mm_nt(ar, b_ref[0]) + _mm_nt(ai, b_ref[1]) + dus_ref[...]).astype(du_ref.dtype)
        dbr, dbi = _mm_tn(u, ar), _mm_tn(u, ai)
        pr = _sd(hr, 1) + jnp.where(top, car_ref[0:1, :], 0.0)
        pi = _sd(hi, 1) + jnp.where(top, car_ref[1:2, :], 0.0)
        dlr = jnp.sum(ar * pr + ai * pi, axis=0, keepdims=True)
        dli = jnp.sum(ai * pr - ar * pi, axis=0, keepdims=True)

        @pl.when(first)
        def _():
            dlam_ref[0] = dlr
            dlam_ref[1] = dli
            db_ref[0] = dbr
            db_ref[1] = dbi
            dc_ref[0] = dcr2
            dc_ref[1] = dci2

        @pl.when(jnp.logical_not(first))
        def _():
            dlam_ref[0] += dlr
            dlam_ref[1] += dli
            db_ref[0] += dbr
            db_ref[1] += dbi
            dc_ref[0] += dcr2
            dc_ref[1] += dci2

    grid = (NCB, nt)
    in_specs = [pl.BlockSpec((r, DH), lambda c, i: (nt - 1 - i, u0 + c)),
                pl.BlockSpec((2, 1, SW), lambda c, i: (0, 0, c)),
                pl.BlockSpec((2, None, DH, SW), lambda c, i: (0, c, 0, 0)),
                pl.BlockSpec((2, None, SW, DH), lambda c, i: (0, c, 0, 0)),
                pl.BlockSpec((None, 8, SW), lambda c, i: (nt - 1 - i, 0, c)),
                pl.BlockSpec((r, DH), lambda c, i: (nt - 1 - i, c)),
                pl.BlockSpec((r, DH), lambda c, i: (nt - 1 - i, c))]
    out_specs = [pl.BlockSpec((r, DH), lambda c, i: (nt - 1 - i, c)),
                 pl.BlockSpec((2, 1, SW), lambda c, i: (0, 0, c)),
                 pl.BlockSpec((2, None, DH, SW), lambda c, i: (0, c, 0, 0)),
                 pl.BlockSpec((2, None, SW, DH), lambda c, i: (0, c, 0, 0))]
    out_shape = [jax.ShapeDtypeStruct((t, D), GRAD_ACT), jax.ShapeDtypeStruct((2, 1, NG * NS), F32),
                 jax.ShapeDtypeStruct((2, NCB, DH, SW), F32), jax.ShapeDtypeStruct((2, NCB, SW, DH), F32)]
    scratch = ([pltpu.VMEM((8, SW), F32)] + [pltpu.VMEM((SW // DH, r, DH), F32)] * 4
               + [pltpu.VMEM((r // SUB, SW), F32)] * 2)
    args, sem = [proj, lam, bblk, cblk, carries, dy, du_skip], ("parallel", "arbitrary")
    if exchange is not None:
        body = _carry(body, len(args), len(out_shape), len(scratch), exchange, grid)
        in_specs, out_specs = in_specs + exchange.in_specs, out_specs + exchange.out_specs
        out_shape, scratch, args = out_shape + exchange.out_shape, scratch + exchange.scratch_shapes, args + exchange.srcs
        sem = ("arbitrary", "arbitrary")
    outs = pl.pallas_call(
        body, name="s5_bwd", grid=grid, in_specs=in_specs, out_specs=out_specs, out_shape=out_shape,
        scratch_shapes=scratch, compiler_params=_cparams(sem),
    )(*args)
    return outs[:4], outs[4:]


def _proj_spec(tm, col):
    return pl.BlockSpec((tm, D), lambda i: (i, col))


def _layer_mat(l):
    return pl.BlockSpec((None, D, D), lambda i: (l, 0, 0))


def _mix_fwd(proj, o, s5y, x, hn, dvec, wglu, bglu, wout, npost, l):
    t = x.shape[0]
    tm = _tile(t, 256)

    def body(za_ref, u_ref, zb_ref, ra_ref, rb_ref, o_ref, y_ref, x_ref, hn_ref, d_ref, wg_ref, bg_ref, wo_ref,
             np_ref, xn_ref):
        y0 = _mix_pre(y_ref[...], u_ref[...], d_ref[...])
        gl = _mm(y0, wg_ref[...]) + bg_ref[...]
        m = _mix_mid(o_ref[...], za_ref[...], y0, gl, zb_ref[...], ra_ref[...], rb_ref[...], hn_ref[...])
        out = _mm(m, wo_ref[...])
        xn_ref[...] = _mix_post(x_ref[...], out, np_ref[...])

    act = pl.BlockSpec((tm, D), lambda i: (i, 0))
    return pl.pallas_call(
        body, name="mix_fwd", grid=(t // tm,),
        in_specs=[_proj_spec(tm, 3), _proj_spec(tm, 4), _proj_spec(tm, 5), _proj_spec(tm, 6), _proj_spec(tm, 7),
                  act, act, act, _full((1, DH)), _full((1, D)), _layer_mat(l), _full((1, D)), _layer_mat(l),
                  _full((1, D))],
        out_specs=act,
        out_shape=jax.ShapeDtypeStruct((t, D), F32),
        compiler_params=_cparams(("parallel",)),
    )(proj, proj, proj, proj, proj, o, s5y, x, hn, dvec, wglu, bglu, wout, npost)


def _mix_bwd(proj, o, s5y, x, hn, dvec, wglu, bglu, wout, npost, dxn, l):
    t = x.shape[0]
    tm = _tile(t, 128)

    def body(za_ref, u_ref, zb_ref, ra_ref, rb_ref, o_ref, y_ref, x_ref, hn_ref, d_ref, wg_ref, bg_ref, wo_ref,
             np_ref, dxn_ref,
             dza_ref, du_ref, dzb_ref, dra_ref, drb_ref, do_ref, dy_ref, dx_ref,
             dwg_ref, dwo_ref, dvecs_ref, dhn_ref):
        y0, vjp_pre = jax.vjp(_mix_pre, y_ref[...], u_ref[...], d_ref[...])
        gl = _mm(y0, wg_ref[...]) + bg_ref[...]
        m, vjp_mid = jax.vjp(_mix_mid, o_ref[...], za_ref[...], y0, gl, zb_ref[...], ra_ref[...], rb_ref[...],
                             hn_ref[...])
        out = _mm(m, wo_ref[...])
        _, vjp_post = jax.vjp(_mix_post, x_ref[...], out, np_ref[...])
        dx, dout, dnp = vjp_post(dxn_ref[...])
        dm = _mm_nt(dout, wo_ref[...])
        dwo = _mm_tn(m, dout)
        do, dza, dy0, dgl, dzb, dra, drb, dhn = vjp_mid(dm)
        dwg = _mm_tn(y0, dgl)
        dbg = jnp.sum(dgl, axis=0, keepdims=True)
        dy0 = dy0 + _mm_nt(dgl, wg_ref[...])
        dy, du, dd = vjp_pre(dy0)
        dza_ref[...] = dza.astype(dza_ref.dtype)
        du_ref[...] = du
        dzb_ref[...] = dzb.astype(dzb_ref.dtype)
        dra_ref[...] = dra.astype(dra_ref.dtype)
        drb_ref[...] = drb.astype(drb_ref.dtype)
        do_ref[...] = do
        dy_ref[...] = dy
        dx_ref[...] = dx
        first = pl.program_id(0) == 0

        @pl.when(first)
        def _():
            dwg_ref[...] = dwg
            dwo_ref[...] = dwo
            dvecs_ref[...] = jnp.zeros_like(dvecs_ref)
            dhn_ref[...] = jnp.zeros_like(dhn_ref)

        @pl.when(jnp.logical_not(first))
        def _():
            dwg_ref[...] += dwg
            dwo_ref[...] += dwo

        dvecs_ref[0:1, :] += dd
        dvecs_ref[1:2, :] += dbg
        dvecs_ref[2:3, :] += dnp
        dhn_ref[0:1, :] += dhn

    act = pl.BlockSpec((tm, D), lambda i: (i, 0))
    a, ga = jax.ShapeDtypeStruct((t, D), F32), jax.ShapeDtypeStruct((t, D), GRAD_ACT)
    w = jax.ShapeDtypeStruct((D, D), F32)
    return pl.pallas_call(
        body, name="mix_bwd", grid=(t // tm,),
        in_specs=[_proj_spec(tm, 3), _proj_spec(tm, 4), _proj_spec(tm, 5), _proj_spec(tm, 6), _proj_spec(tm, 7),
                  act, act, act, _full((1, DH)), _full((1, D)), _layer_mat(l), _full((1, D)), _layer_mat(l),
                  _full((1, D)), act],
        out_specs=[act] * 8 + [_full((D, D)), _full((D, D)), _full((8, D)), _full((8, DH))],
        out_shape=[ga, a, ga, ga, ga, a, a, a, w, w, jax.ShapeDtypeStruct((8, D), F32),
                   jax.ShapeDtypeStruct((8, DH), F32)],
        compiler_params=_cparams(("arbitrary",)),
    )(proj, proj, proj, proj, proj, o, s5y, x, hn, dvec, wglu, bglu, wout, npost, dxn)


def _loss_grad(y, target):
    t = y.shape[0]
    tm = _tile(t, 512)

    def body(y_ref, t_ref, dy_ref, l_ref):
        err = y_ref[...] - t_ref[...]
        dy_ref[...] = err * (1.0 / D)
        part = jnp.sum(jnp.sum(err * err, axis=1, keepdims=True), axis=0, keepdims=True) * (0.5 / D)
        part = jnp.broadcast_to(part, (8, DH))

        @pl.when(pl.program_id(0) == 0)
        def _():
            l_ref[...] = part

        @pl.when(pl.program_id(0) > 0)
        def _():
            l_ref[...] += part

    act = pl.BlockSpec((tm, D), lambda i: (i, 0))
    return pl.pallas_call(
        body, name="loss_grad", grid=(t // tm,),
        in_specs=[act, act], out_specs=[act, _full((8, DH))],
        out_shape=[jax.ShapeDtypeStruct((t, D), F32), jax.ShapeDtypeStruct((8, DH), F32)],
        compiler_params=_cparams(("arbitrary",)),
    )(y, target)


def _flips(rel):
    x, y, c = lax.axis_index("x"), lax.axis_index("y"), lax.axis_index("c")
    fx, fy, fc = rel
    return (x ^ fx if fx else x, y ^ fy if fy else y, c ^ fc if fc else c)


CHIP_RELS = ((1, 0, 0), (0, 1, 0), (1, 1, 0))
ALL_RELS = tuple((fx, fy, fc) for fx in (0, 1) for fy in (0, 1) for fc in (0, 1) if (fx, fy, fc) != (0, 0, 0))


def _slot_of(pos, by_chip):
    px, py, pc = pos
    return 2 * px + py if by_chip else 4 * px + 2 * py + pc


class _Exchange:
    def __init__(self, srcs, rels, by_chip, scatter):
        self.srcs, self.rels, self.by_chip, self.scatter = list(srcs), rels, by_chip, scatter
        self.narr = len(self.srcs)
        nslot, nsem = NCHIP if by_chip else NDEV, self.narr * len(rels)
        self.in_specs = [pl.BlockSpec(memory_space=pl.ANY)] * self.narr
        self.out_specs = [pl.BlockSpec(memory_space=pl.ANY)] * self.narr
        self.out_shape = [jax.ShapeDtypeStruct((nslot,) + s.shape[-2:], s.dtype) for s in self.srcs]
        self.scratch_shapes = [pltpu.SemaphoreType.DMA((nsem,)), pltpu.SemaphoreType.DMA((nsem,)),
                               pltpu.SemaphoreType.DMA((self.narr,))]

    def _copies(self, src_refs, dst_refs, sems):
        send_sems, recv_sems, local_sems = sems
        my_slot = _slot_of(_flips((0, 0, 0)), self.by_chip)
        local, sends, arrivals = [], [], []
        for a, (src_ref, dst_ref) in enumerate(zip(src_refs, dst_refs)):
            local.append(pltpu.make_async_copy(src_ref.at[my_slot] if self.scatter else src_ref, dst_ref.at[my_slot],
                                               local_sems.at[a]))
            for k, rel in enumerate(self.rels):
                peer = _flips(rel)
                pair = dict(send_sem=send_sems.at[a * len(self.rels) + k], recv_sem=recv_sems.at[a * len(self.rels) + k],
                            device_id=peer, device_id_type=pl.DeviceIdType.MESH)
                part = src_ref.at[_slot_of(peer, self.by_chip)] if self.scatter else src_ref
                sends.append(pltpu.make_async_remote_copy(src_ref=part, dst_ref=dst_ref.at[my_slot], **pair))
                arrivals.append(pltpu.make_async_remote_copy(
                    src_ref=src_ref.at[0] if self.scatter else src_ref, dst_ref=dst_ref.at[_slot_of(peer, self.by_chip)],
                    **pair))
        return local, sends, arrivals

    def start(self, src_refs, dst_refs, sems):
        local, sends, _ = self._copies(src_refs, dst_refs, sems)
        for cp in local + sends:
            cp.start()

    def wait(self, src_refs, dst_refs, sems):
        local, sends, arrivals = self._copies(src_refs, dst_refs, sems)
        for cp in arrivals:
            cp.wait_recv()
        for cp in sends:
            cp.wait_send()
        for cp in local:
            cp.wait()


def _exchange(srcs, rels, by_chip, scatter, name):
    ex = _Exchange(srcs, rels, by_chip, scatter)

    def body(*refs):
        parts = refs[:ex.narr], refs[ex.narr:2 * ex.narr], refs[2 * ex.narr:]
        ex.start(*parts)
        ex.wait(*parts)

    return pl.pallas_call(body, name=name, in_specs=ex.in_specs, out_specs=ex.out_specs, out_shape=ex.out_shape,
                          scratch_shapes=ex.scratch_shapes)(*ex.srcs)


def _sibling_swap(srcs, name):
    narr = len(srcs)

    def body(*refs):
        src_refs, dst_refs = refs[:narr], refs[narr:2 * narr]
        send_sems, recv_sems = refs[2 * narr:]
        peer = _flips((0, 0, 1))
        copies = [pltpu.make_async_remote_copy(src_ref=s, dst_ref=d, send_sem=send_sems.at[a], recv_sem=recv_sems.at[a],
                                               device_id=peer, device_id_type=pl.DeviceIdType.MESH)
                  for a, (s, d) in enumerate(zip(src_refs, dst_refs))]
        for cp in copies:
            cp.start()
        for cp in copies:
            cp.wait()

    return pl.pallas_call(
        body, name=name,
        in_specs=[pl.BlockSpec(memory_space=pl.ANY)] * narr,
        out_specs=[pl.BlockSpec(memory_space=pl.ANY)] * narr,
        out_shape=[jax.ShapeDtypeStruct(s.shape, s.dtype) for s in srcs],
        scratch_shapes=[pltpu.SemaphoreType.DMA((narr,)), pltpu.SemaphoreType.DMA((narr,))],
    )(*srcs)


def _all_reduce(src, name):
    rows, cols = src.shape
    r = rows // NDEV
    nrel = len(ALL_RELS)

    def body(src_ref, out_ref, parts_ref, mine_ref, send_sems, recv_sems):
        my_slot = _slot_of(_flips((0, 0, 0)), False)

        def piece(ref, slot):
            return ref.at[pl.ds(pl.multiple_of(slot * r, 8), r), :]

        def copies(phase):
            out = []
            for k, rel in enumerate(ALL_RELS):
                peer = _flips(rel)
                pair = dict(send_sem=send_sems.at[phase * nrel + k], recv_sem=recv_sems.at[phase * nrel + k],
                            device_id=peer, device_id_type=pl.DeviceIdType.MESH)
                if phase == 0:
                    out.append(pltpu.make_async_remote_copy(src_ref=piece(src_ref, _slot_of(peer, False)),
                                                            dst_ref=parts_ref.at[my_slot], **pair))
                else:
                    out.append(pltpu.make_async_remote_copy(src_ref=mine_ref, dst_ref=piece(out_ref, my_slot), **pair))
            return out

        first = copies(0)
        for cp in first:
            cp.start()
        parts_ref[my_slot] = piece(src_ref, my_slot)[...]
        for cp in first:
            cp.wait_recv()
        acc = parts_ref[0]
        for s in range(1, NDEV):
            acc = acc + parts_ref[s]
        mine_ref[...] = acc
        second = copies(1)
        for cp in second:
            cp.start()
        piece(out_ref, my_slot)[...] = acc
        for cp in second:
            cp.wait_recv()
        for cp in first + second:
            cp.wait_send()

    return pl.pallas_call(
        body, name=name,
        in_specs=[pl.BlockSpec(memory_space=pltpu.VMEM)], out_specs=pl.BlockSpec(memory_space=pltpu.VMEM),
        out_shape=jax.ShapeDtypeStruct(src.shape, src.dtype),
        scratch_shapes=[pltpu.VMEM((NDEV, r, cols), src.dtype), pltpu.VMEM((r, cols), src.dtype),
                        pltpu.SemaphoreType.DMA((2 * nrel,)), pltpu.SemaphoreType.DMA((2 * nrel,))],
        compiler_params=pltpu.CompilerParams(vmem_limit_bytes=VMEM_LIMIT),
    )(src)


def _sum_slots(parts, name):
    ns, rows, cols = parts.shape
    tr = _row_tile(rows, 256)

    def body(p_ref, o_ref):
        acc = p_ref[0].astype(F32)
        for s in range(1, ns):
            acc = acc + p_ref[s].astype(F32)
        o_ref[...] = acc

    return pl.pallas_call(
        body, name=name, grid=(rows // tr,),
        in_specs=[pl.BlockSpec((ns, tr, cols), lambda i: (0, i, 0))],
        out_specs=pl.BlockSpec((tr, cols), lambda i: (i, 0)),
        out_shape=jax.ShapeDtypeStruct((rows, cols), F32),
        compiler_params=_cparams(("parallel",)),
    )(parts)


def _adamw(w, g_parts, m, v, name, max_rows=256):
    rows, cols = w.shape
    tr = _row_tile(rows, max_rows)
    c1 = 1.0 / (1.0 - ADAM_B1 ** ADAM_STEP)
    c2 = 1.0 / (1.0 - ADAM_B2 ** ADAM_STEP)
    npart = len(g_parts)

    def body(*refs):
        w_ref, m_ref, v_ref = refs[:3]
        g_refs = refs[3:3 + npart]
        go_ref, d_ref, nm_ref, nv_ref = refs[3 + npart:]
        terms = []
        for g_ref in g_refs:
            terms += [g_ref[...]] if len(g_ref.shape) == 2 else [g_ref[s] for s in range(g_ref.shape[0])]
        g = terms[0]
        for term in terms[1:]:
            g = g + term
        nm = ADAM_B1 * m_ref[...] + (1.0 - ADAM_B1) * g
        nv = ADAM_B2 * v_ref[...] + (1.0 - ADAM_B2) * (g * g)
        d_ref[...] = -ADAM_LR * ((nm * c1) / (jnp.sqrt(nv * c2) + ADAM_EPS) + ADAM_WD * w_ref[...])
        go_ref[...] = g
        nm_ref[...] = nm
        nv_ref[...] = nv

    blk = pl.BlockSpec((tr, cols), lambda i: (i, 0))
    g_specs = [blk if p.ndim == 2 else pl.BlockSpec((p.shape[0], tr, cols), lambda i: (0, i, 0)) for p in g_parts]
    out = jax.ShapeDtypeStruct((rows, cols), F32)
    return pl.pallas_call(
        body, name=name, grid=(rows // tr,),
        in_specs=[blk, blk, blk] + g_specs,
        out_specs=[blk] * 4, out_shape=[out] * 4,
        compiler_params=_cparams(("parallel",)),
    )(w, m, v, *g_parts)


WEIGHT_SPLIT = (0, 384, 704, D)
WIN_SHARD = 2052
CONV_SHARD = 768
ROW_SHARD = 256

SMALL = (("norm_pre", (DEPTH, D)), ("a_log", (DEPTH, NH)), ("dt_bias", (DEPTH, NH)), ("head_norm", (DEPTH, DH)),
         ("ssm_a_re", (DEPTH, NG, NS)), ("ssm_a_im", (DEPTH, NG, NS)), ("ssm_log_dt", (DEPTH, NG)),
         ("ssm_b_re", (DEPTH, NG, NS, GS)), ("ssm_b_im", (DEPTH, NG, NS, GS)),
         ("ssm_c_re", (DEPTH, NG, GS, NS)), ("ssm_c_im", (DEPTH, NG, GS, NS)), ("ssm_d", (DEPTH, D)),
         ("b_glu", (DEPTH, D)), ("norm_post", (DEPTH, D)))


def _pad_rows(flat, rows):
    return jnp.pad(flat, (0, rows * D - flat.shape[0])).reshape(rows, D)


def _rows_by_chip(a):
    nl, rows, cols = a.shape
    return a.reshape(nl, NCHIP, rows // NCHIP, cols).transpose(1, 0, 2, 3).reshape(NCHIP, -1, cols)


def _rows_from_chips(a):
    _, rows, cols = a.shape
    return a.reshape(NCHIP, DEPTH, rows // DEPTH, cols).transpose(1, 0, 2, 3).reshape(DEPTH, -1, cols)


def _cols_by_chip(a):
    nl, rows, cols = a.shape
    return a.reshape(nl, rows, NCHIP, cols // NCHIP).transpose(2, 0, 1, 3).reshape(NCHIP, nl * rows, -1)


def _cols_from_chips(a, nl):
    _, rows, cols = a.shape
    return a.reshape(NCHIP, nl, rows // nl, cols).transpose(1, 2, 0, 3).reshape(nl, rows // nl, NCHIP * cols)


SMALL_ROWS = sum(-(-math.prod(s) // (8 * D)) * 8 for _, s in SMALL)
CONV_ROWS = DEPTH * 4 * 3 * D // D


def _pack_small(vals, extra=()):
    parts = []
    for val in tuple(vals) + tuple(extra):
        n = val.size
        parts.append(_pad_rows(val.reshape(-1), -(-n // (8 * D)) * 8))
    return jnp.concatenate(parts, axis=0)


def _unpack_small(flat):
    outs, r0 = [], 0
    for _, shape in SMALL:
        n = math.prod(shape)
        rows = -(-n // (8 * D)) * 8
        outs.append(flat[r0:r0 + rows].reshape(-1)[:n].reshape(shape))
        r0 += rows
    return outs


def _rearrange_cols(w):
    pad = jnp.zeros(w.shape[:-1] + (NCOL - BD0 - 2 * NH,), w.dtype)
    return jnp.concatenate([w[..., :4 * D], w[..., 4 * D + 2 * NH:], w[..., 4 * D:4 * D + 2 * NH], pad], axis=-1)


def _restore_cols(w):
    return jnp.concatenate([w[..., :4 * D], w[..., BD0:BD0 + 2 * NH], w[..., 4 * D:BD0]], axis=-1)


def _block_diag_b(bb2):
    b = bb2.reshape(-1, NCB, GPB, NS, GS)
    return jnp.einsum("lkgnc,gh->lkgchn", b, jnp.eye(GPB, dtype=F32)).reshape(-1, NCB, GPB * GS, SW)


def _block_diag_b_t(d):
    blocks = jnp.einsum("lkgchn,gh->lkgnc", d.reshape(-1, NCB, GPB, GS, GPB, NS), jnp.eye(GPB, dtype=F32))
    return blocks.reshape(-1, NG, NS * GS)


def _block_diag_c(c):
    blocks = jnp.einsum("lkgcn,gh->lkgnhc", c.reshape(-1, NCB, GPB, GS, NS), jnp.eye(GPB, dtype=F32))
    return blocks.reshape(-1, NCB, SW, GPB * GS)


def _block_diag_c_t(d):
    blocks = jnp.einsum("lkgnhc,gh->lkgcn", d.reshape(-1, NCB, GPB, NS, GPB, GS), jnp.eye(GPB, dtype=F32))
    return blocks.reshape(-1, NG, GS, NS)


def _local_step(x, target, weights, conv, small, comm=None):
    weights = list(weights) + [None] * (DEPTH - len(weights))
    ar, ai = small["ssm_a_re"], small["ssm_a_im"]
    ldt = small["ssm_log_dt"].reshape(DEPTH, NG, 1)
    br2 = small["ssm_b_re"].reshape(DEPTH, NG, NS * GS)
    bi2 = small["ssm_b_im"].reshape(DEPTH, NG, NS * GS)
    lr, li, bbr2, bbi2 = _s5_params(ar, ai, ldt, br2, bi2)

    def row(name, l, width):
        return small[name][l].reshape(1, width)

    gvecs = jnp.pad(jnp.stack([small["a_log"], small["dt_bias"]], axis=1), ((0, 0), (0, 6), (NH, DH - 2 * NH)))
    lams = jnp.stack([lr.reshape(DEPTH, 1, NG * NS), li.reshape(DEPTH, 1, NG * NS)], axis=1)
    bblks = jnp.stack([_block_diag_b(bbr2), _block_diag_b(bbi2)], axis=1)
    cblks = jnp.stack([_block_diag_c(small["ssm_c_re"]), _block_diag_c(small["ssm_c_im"])], axis=1)
    saved = []
    for l in range(DEPTH):
        gvec, lam, bblk, cblk = gvecs[l], lams[l], bblks[l], cblks[l]
        wcat, wglu, wout = weights[l]
        fetch = [None] * 3
        if comm and l + 1 < DEPTH:
            fetch = [_Exchange(comm["weight_parts"](l + 1, part), CHIP_RELS, True, False) for part in range(3)]
        proj, h, got0 = _inproj_fwd(x, row("norm_pre", l, D), wcat, 0, fetch[0])
        qkv = _prep_fwd(proj, conv[l])
        bb, gcb = _gates_fwd(proj, gvec)
        local, t_inv, got1 = _delta_local_fwd(qkv, bb, gcb, fetch[1])
        o, states = _delta_state_fwd(local, gcb)
        s5y, carries, got2 = _s5_fwd(proj, lam, bblk, cblk, fetch[2])
        if fetch[0] is not None:
            weights[l + 1] = comm["weights_from"]([got0, got1, got2])
        xn = _mix_fwd(proj, o, s5y, x, row("head_norm", l, DH), row("ssm_d", l, D), wglu, row("b_glu", l, D),
                      wout, row("norm_post", l, D), 0)
        saved.append((x, proj, h, qkv, bb, gcb, local, t_inv, o, states, s5y, carries, gvec, lam, bblk, cblk))
        x = xn

    dx, loss_part = _loss_grad(x, target)

    g = {k: [None] * DEPTH for k in ("wcat", "conv", "wglu", "wout", "norm_pre", "a_log", "dt_bias", "head_norm",
                                     "ssm_d", "b_glu", "norm_post", "dlam", "dbblk", "dcblk")}
    from_chips, send, send_layer = [None] * DEPTH, None, None
    for l in reversed(range(DEPTH)):
        xl, proj, h, qkv, bb, gcb, local, t_inv, o, states, s5y, carries, gvec, lam, bblk, cblk = saved[l]
        wcat, wglu, wout = weights[l]
        (dza, du_skip, dzb, dra, drb, do, ds5y, dxres, dwg, dwo, dvecs, dhn) = _mix_bwd(
            proj, o, s5y, xl, row("head_norm", l, DH), row("ssm_d", l, D), wglu, row("b_glu", l, D), wout,
            row("norm_post", l, D), dx, 0)
        (du, dlam, dbblk, dcblk), arrived = _s5_bwd(proj, lam, bblk, cblk, carries, ds5y, du_skip, send)
        if send is not None:
            from_chips[send_layer] = arrived
        *dlocal, dgcb_state = _delta_state_bwd(local, gcb, states, do)
        dq, dk, dv, dbb, dgcb = _delta_local_bwd(qkv, bb, gcb, t_inv, dlocal, dgcb_state)
        dbd, dgvec = _gates_bwd(proj, gvec, dbb, dgcb)
        dpre, dconv = _prep_bwd(proj, conv[l], dq, dk, dv)
        dproj = jnp.concatenate([dpre, dza, du, dzb, dra, drb, dbd], axis=1)
        g["wcat"][l] = _inproj_bwd_dw(h, dproj)
        g["conv"][l], g["wglu"][l], g["wout"][l] = dconv, dwg, dwo
        send = _Exchange(comm["grad_parts"](g["wcat"][l], dwg, dwo), CHIP_RELS, True, True) if comm else None
        dx, dgain, arrived = _inproj_bwd_dx(dproj, wcat, xl, row("norm_pre", l, D), dxres, 0, send if l == 0 else None)
        if comm and l == 0:
            from_chips[l] = arrived
        send_layer = l
        g["norm_pre"][l] = dgain[0]
        g["a_log"][l], g["dt_bias"][l] = dgvec[0, NH:2 * NH], dgvec[1, NH:2 * NH]
        g["head_norm"][l] = dhn[0]
        g["ssm_d"][l], g["b_glu"][l], g["norm_post"][l] = dvecs[0], dvecs[1], dvecs[2]
        g["dlam"][l], g["dbblk"][l], g["dcblk"][l] = dlam, dbblk, dcblk
    if comm:
        for k in ("wcat", "wglu", "wout"):
            del g[k]
    g = {k: jnp.stack(v) for k, v in g.items()}
    g["from_chips"] = from_chips
    dlam, dbblk, dcblk = g.pop("dlam"), g.pop("dbblk"), g.pop("dcblk")
    g["ssm_c_re"], g["ssm_c_im"] = _block_diag_c_t(dcblk[:, 0]), _block_diag_c_t(dcblk[:, 1])
    dar, dai, dldt, dbr2, dbi2 = _s5_params_bwd(
        ar, ai, ldt, br2, bi2, dlam[:, 0].reshape(DEPTH, NG, NS), dlam[:, 1].reshape(DEPTH, NG, NS),
        _block_diag_b_t(dbblk[:, 0]), _block_diag_b_t(dbblk[:, 1]))
    g["ssm_a_re"], g["ssm_a_im"], g["ssm_log_dt"] = dar, dai, dldt.reshape(DEPTH, NG)
    g["ssm_b_re"] = dbr2.reshape(DEPTH, NG, NS, GS)
    g["ssm_b_im"] = dbi2.reshape(DEPTH, NG, NS, GS)
    return loss_part[0, 0], dx, g


def kernel(x, norm_pre, w_in, conv_w, a_log, dt_bias, head_norm, ssm_a_re, ssm_a_im, ssm_log_dt, ssm_b_re, ssm_b_im, ssm_c_re, ssm_c_im, ssm_d, w_glu, b_glu, w_out, norm_post, loss_target, m_norm_pre, m_w_in, m_conv_w, m_a_log, m_dt_bias, m_head_norm, m_ssm_a_re, m_ssm_a_im, m_ssm_log_dt, m_ssm_b_re, m_ssm_b_im, m_ssm_c_re, m_ssm_c_im, m_ssm_d, m_w_glu, m_b_glu, m_w_out, m_norm_post, v_norm_pre, v_w_in, v_conv_w, v_a_log, v_dt_bias, v_head_norm, v_ssm_a_re, v_ssm_a_im, v_ssm_log_dt, v_ssm_b_re, v_ssm_b_im, v_ssm_c_re, v_ssm_c_im, v_ssm_d, v_w_glu, v_b_glu, v_w_out, v_norm_post):
    args = dict(locals())
    small = {n: args[n] for n, _ in SMALL}

    def flat2(a):
        return a.reshape(-1, a.shape[-1])

    w_in16, w_glu16, w_out16 = w_in.astype(BF16), w_glu.astype(BF16), w_out.astype(BF16)

    def weight_parts(l, part=None):
        if part is None:
            return [w_in16[l], w_glu16[l], w_out16[l]]
        lo, hi = WEIGHT_SPLIT[part], WEIGHT_SPLIT[part + 1]
        return [w_in16[l, lo:hi]] + [[w_glu16[l]], [w_out16[l]], []][part]

    def weights_from(parts):
        if len(parts) == 3 and isinstance(parts[0], (list, tuple)):
            parts = [jnp.concatenate([p[0] for p in parts], axis=1), parts[0][1], parts[1][1]]
        g_in, g_glu, g_out = parts[:3]
        return (_rearrange_cols(_cols_from_chips(g_in, 1)), g_glu.reshape(1, D, D), g_out.reshape(1, D, D))

    def grad_parts(gwcat, gwglu, gwout):
        return [_cols_by_chip(_restore_cols(gwcat[None])).astype(BF16), gwglu.reshape(NCHIP, ROW_SHARD, D).astype(BF16),
                gwout.reshape(NCHIP, ROW_SHARD, D).astype(BF16)]

    first = _exchange(weight_parts(0) + [flat2(conv_w)], CHIP_RELS, True, False, "gather_weights")
    conv = _cols_from_chips(first[3], DEPTH)
    comm = dict(weight_parts=weight_parts, weights_from=weights_from, grad_parts=grad_parts)
    loss_part, dx, g = _local_step(x[0], loss_target[0], [weights_from(first)], conv, small, comm)
    loss = lax.psum(loss_part, ("x", "y", "c"))

    from_chips = [jnp.concatenate([g["from_chips"][l][a] for l in range(DEPTH)], axis=1) for a in range(3)]
    core_sums = [_sum_slots(p, "sum_chips_" + n) for p, n in zip(from_chips, ("in", "glu", "out"))]
    others = _sibling_swap(core_sums, "swap_cores")
    sharded = {}
    for n, mine, other in zip(("w_in", "w_glu", "w_out"), core_sums, others):
        sharded[n] = _adamw(flat2(args[n]), [mine, other], flat2(args["m_" + n]), flat2(args["v_" + n]), "adamw_" + n,
                            max_rows=128)

    pad = jnp.zeros(((-(SMALL_ROWS + CONV_ROWS)) % (8 * NDEV), D), F32)
    small_sum = _all_reduce(_pack_small([g[n] for n, _ in SMALL], extra=[g["conv"], pad]), "reduce_small")
    small_out = _adamw(_pack_small([args[n] for n, _ in SMALL]), [small_sum],
                       _pack_small([args["m_" + n] for n, _ in SMALL]),
                       _pack_small([args["v_" + n] for n, _ in SMALL]), "adamw_small")
    chip = 2 * lax.axis_index("x") + lax.axis_index("y")
    conv_sum = small_sum[SMALL_ROWS:SMALL_ROWS + CONV_ROWS].reshape(DEPTH * 4, 3 * D)
    conv_sum = lax.dynamic_slice_in_dim(conv_sum, chip * CONV_SHARD, CONV_SHARD, axis=1)
    sharded["conv_w"] = _adamw(flat2(conv_w), [conv_sum], flat2(m_conv_w), flat2(v_conv_w), "adamw_conv")

    names = ["norm_pre", "w_in", "conv_w", "a_log", "dt_bias", "head_norm", "ssm_a_re", "ssm_a_im", "ssm_log_dt",
             "ssm_b_re", "ssm_b_im", "ssm_c_re", "ssm_c_im", "ssm_d", "w_glu", "b_glu", "w_out", "norm_post"]
    outs = [loss, dx[None]]
    for i in range(4):
        sm = dict(zip([n for n, _ in SMALL], _unpack_small(small_out[i])))
        outs += [sharded[n][i].reshape(args[n].shape) if n in sharded else sm[n] for n in names]
    return tuple(outs)
```

```python
import functools
import math

import jax
import jax.numpy as jnp
from jax import lax
from jax.experimental import pallas as pl
from jax.experimental.pallas import tpu as pltpu

F32 = jnp.float32
BF16 = jnp.bfloat16
HI = lax.Precision.HIGHEST

D = 1024
NH = 8
DH = 128
CH = 128
NG = 64
GS = 16
NS = 64
GPB = 8
NCB = NG // GPB
SW = GPB * NS
NCOL = 8320
BD0 = 8192
EPS = 1e-6
DEPTH = 4
NCHIP = 4
NDEV = 8
VMEM_LIMIT = 56 * 1024 * 1024
GRAD_ACT = jnp.bfloat16

ADAM_LR = 0.001
ADAM_B1 = 0.9
ADAM_B2 = 0.999
ADAM_EPS = 1e-08
ADAM_WD = 0.01
ADAM_STEP = 10


def _cparams(sem=None):
    return pltpu.CompilerParams(dimension_semantics=sem, vmem_limit_bytes=VMEM_LIMIT)


def _full(shape):
    nd = len(shape)
    return pl.BlockSpec(shape, lambda *_: (0,) * nd)


def _rms(x, gain):
    ms = jnp.mean(x * x, axis=-1, keepdims=True)
    return x * lax.rsqrt(ms + EPS) * gain


def _sigmoid(x):
    return 1.0 / (1.0 + jnp.exp(-x))


def _silu(x):
    return x * _sigmoid(x)


def _softplus(x):
    return jnp.maximum(x, 0.0) + jnp.log(1.0 + jnp.exp(-jnp.abs(x)))


def _gelu(x):
    return 0.5 * x * (1.0 + jnp.tanh(math.sqrt(2.0 / math.pi) * (x + 0.044715 * (x * x * x))))


def _dot_bf16(a, b, dims):
    return lax.dot_general(a.astype(BF16), b.astype(BF16), (dims, ((), ())), preferred_element_type=F32)


def _mm_nt(a, b):
    return _dot_bf16(a, b, ((1,), (1,)))


def _mm_tn(a, b):
    return _dot_bf16(a, b, ((0,), (0,)))


@jax.custom_vjp
def _mm(a, b):
    return _dot_bf16(a, b, ((1,), (0,)))


def _mm_fwd(a, b):
    return _dot_bf16(a, b, ((1,), (0,))), (a, b)


def _mm_bwd(res, ct):
    a, b = res
    return _mm_nt(ct, b).astype(a.dtype), _mm_tn(a, ct).astype(b.dtype)


_mm.defvjp(_mm_fwd, _mm_bwd)


@jax.custom_vjp
def _mm_nt_d(a, b):
    return _mm_nt(a, b)


def _mm_nt_d_bwd(res, ct):
    a, b = res
    return _dot_bf16(ct, b, ((1,), (0,))), _mm_tn(ct, a)


_mm_nt_d.defvjp(lambda a, b: (_mm_nt(a, b), (a, b)), _mm_nt_d_bwd)


@jax.custom_vjp
def _mm_tn_d(a, b):
    return _mm_tn(a, b)


def _mm_tn_d_bwd(res, ct):
    a, b = res
    return _mm_nt(b, ct), _dot_bf16(a, ct, ((1,), (0,)))


_mm_tn_d.defvjp(lambda a, b: (_mm_tn(a, b), (a, b)), _mm_tn_d_bwd)


def _split_bf16(a):
    hi = a.astype(BF16)
    return hi, (a - hi.astype(F32)).astype(BF16)


def _dot3(a, b, dims):
    ah, al = _split_bf16(a)
    bh, bl = _split_bf16(b)

    def dot(x, y):
        return lax.dot_general(x, y, (dims, ((), ())), preferred_element_type=F32)

    return dot(ah, bh) + (dot(ah, bl) + dot(al, bh))


@jax.custom_vjp
def _imm(a, b):
    return _dot3(a, b, ((1,), (0,)))


def _imm_bwd(res, ct):
    a, b = res
    return _dot3(ct, b, ((1,), (1,))), _dot3(a, ct, ((0,), (0,)))


_imm.defvjp(lambda a, b: (_dot3(a, b, ((1,), (0,))), (a, b)), _imm_bwd)


def _hmm(a, b):
    return jnp.dot(a, b, precision=HI, preferred_element_type=F32)


def _hmm_nt(a, b):
    return lax.dot_general(a, b, (((1,), (1,)), ((), ())), precision=HI, preferred_element_type=F32)


def _hmm_tn(a, b):
    return lax.dot_general(a, b, (((0,), (0,)), ((), ())), precision=HI, preferred_element_type=F32)


def _rows(shape):
    return lax.broadcasted_iota(jnp.int32, shape, 0)


def _cols(shape):
    return lax.broadcasted_iota(jnp.int32, shape, 1)


def _sd(x, s):
    return jnp.where(_rows(x.shape) >= s, pltpu.roll(x, s, axis=0), 0.0)


def _su(x, s):
    n = x.shape[0]
    return jnp.where(_rows(x.shape) < n - s, pltpu.roll(x, n - s, axis=0), 0.0)


@functools.partial(jax.custom_vjp, nondiff_argnums=(1,))
def _shift_down(x, s):
    return _sd(x, s)


def _shift_down_fwd(x, s):
    return _sd(x, s), None


def _shift_down_bwd(s, _, g):
    return (_su(g, s),)


_shift_down.defvjp(_shift_down_fwd, _shift_down_bwd)


def _last_row(x):
    n = x.shape[0]
    return jnp.sum(jnp.where(_rows(x.shape) == n - 1, x, 0.0), axis=0, keepdims=True)


def _prep_fn(p, w0, w1, w2, w3, qk):
    acc = w3 * p + w2 * _shift_down(p, 1) + w1 * _shift_down(p, 2) + w0 * _shift_down(p, 3)
    a = _silu(acc)
    nrm = lax.rsqrt(jnp.sum(a * a, axis=-1, keepdims=True) + EPS)
    return a * (nrm * qk + (1.0 - qk))


def _gates_fn(bd, av, bv):
    tm = bd.shape[0]
    beta_all = _sigmoid(bd)
    g_all = -jnp.exp(av) * _softplus(bd + bv)
    r, c = _rows((tm, tm)), _cols((tm, tm))
    tri = jnp.where((r // CH == c // CH) & (r >= c), 1.0, 0.0).astype(F32)
    gc_all = _hmm(tri, g_all)
    lane = _cols(bd.shape)
    outs = []
    for h in range(NH):
        b = jnp.sum(jnp.where(lane == h, beta_all, 0.0), axis=1, keepdims=True)
        outs.append(jnp.broadcast_to(b, bd.shape))
    for h in range(NH):
        g = jnp.sum(jnp.where(lane == NH + h, gc_all, 0.0), axis=1, keepdims=True)
        outs.append(jnp.broadcast_to(g, bd.shape))
    return tuple(outs)


INV_BASE = 2


def _merge_mm(a, b):
    return _dot_bf16(a, b, ((1,), (0,)))


def _unit_lower_inv(l_mats):
    n = l_mats[0].shape[0]
    ii, jj = _rows((n, n)), _cols((n, n))
    base = ii // INV_BASE == jj // INV_BASE
    ps = [-jnp.where(base, l_mat, 0.0) for l_mat in l_mats]
    eye = jnp.where(ii == jj, 1.0, 0.0).astype(F32)
    ds = [eye + p for p in ps]
    k = 1
    while 2 * k < INV_BASE:
        ps = [_imm(p, p) for p in ps]
        ds = [d + _imm(d, p) for d, p in zip(ds, ps)]
        k *= 2
    m = INV_BASE
    while m < n:
        pair = (ii // (2 * m) == jj // (2 * m)) & (ii // m > jj // m)
        des = [_merge_mm(d, jnp.where(pair, l_mat, 0.0)) for d, l_mat in zip(ds, l_mats)]
        ds = [d - _merge_mm(de, d) for d, de in zip(ds, des)]
        m *= 2
    return ds


@jax.custom_vjp
def _known_inverse(l_mat, t_inv):
    return t_inv


def _known_inverse_bwd(t_inv, ct):
    d_l = -_dot3(_dot3(t_inv, ct, ((0,), (0,))), t_inv, ((1,), (1,)))
    return d_l, jnp.zeros_like(t_inv)


_known_inverse.defvjp(lambda l_mat, t_inv: (t_inv, t_inv), _known_inverse_bwd)


def _chunk_system(q, k, v, bb, gcb):
    qs = q * (DH ** -0.5)
    kb = k * bb
    eg = jnp.exp(gcb)
    ii, jj = _rows((CH, CH)), _cols((CH, CH))
    decay = jnp.exp(jnp.where(ii >= jj, gcb - gcb.T, -1e30))
    l_mat = jnp.where(ii > jj, _mm_nt_d(kb, k) * decay, 0.0)
    a_qk = _mm_nt_d(qs, k) * decay
    k_dec = k * jnp.exp(_last_row(gcb) - gcb)
    return l_mat, (v * bb, kb * eg, qs * eg, k_dec, a_qk)


def _chunk_solve(t_inv, rest):
    vb, kbe, q_dec, k_dec, a_qk = rest
    return _mm(t_inv, vb), _mm(t_inv, kbe), q_dec, k_dec, a_qk


def _side_by_side_vjp(fn, items, cts):
    n = len(items[0])
    _, vjp = jax.vjp(lambda *flat: fn([flat[i * n:(i + 1) * n] for i in range(len(items))]),
                     *[a for item in items for a in item])
    grads = vjp(cts)
    return [grads[i * n:(i + 1) * n] for i in range(len(items))]


def _chunks_local_known(items):
    systems = [_chunk_system(*item[:5]) for item in items]
    t_invs = [_known_inverse(l_mat, item[5]) for (l_mat, _), item in zip(systems, items)]
    return [_chunk_solve(t_inv, rest) for t_inv, (_, rest) in zip(t_invs, systems)]


def _chunks_local(chunks):
    systems = [_chunk_system(*c) for c in chunks]
    t_invs = _unit_lower_inv([l_mat for l_mat, _ in systems])
    return [(_chunk_solve(t_inv, rest), t_inv) for t_inv, (_, rest) in zip(t_invs, systems)]


def _state_steps(items):
    v_news = [u - _mm(w, state) for u, w, _, _, _, _, state in items]
    outs = [_mm(q_dec, state) + _mm(a_qk, v_new) for (_, _, q_dec, _, a_qk, _, state), v_new in zip(items, v_news)]
    states = [state * jnp.exp(_last_row(gcb)) + _mm_tn_d(k_dec, v_new)
              for (_, _, _, k_dec, _, gcb, state), v_new in zip(items, v_news)]
    return list(zip(outs, states))


SUB = 8


def _cmul(ar, ai, br, bi):
    return ar * br - ai * bi, ar * bi + ai * br


def _scan_tile(xr, xi, mr, mi, hr_ref, hi_ref, cr_ref, ci_ref, reverse):
    n, width = xr.shape
    ngroups = n // SUB
    shift_groups = _su if reverse else _sd
    xr, xi = xr.reshape(ngroups, SUB, width), xi.reshape(ngroups, SUB, width)
    pr, pi = mr, mi
    tr, ti = jnp.broadcast_to(mr, (SUB, width)), jnp.broadcast_to(mi, (SUB, width))
    pos = _rows(tr.shape)
    s = 1
    while s < SUB:
        inside = pos < SUB - s if reverse else pos >= s
        shift = SUB - s if reverse else s
        qr, qi = jnp.where(inside, pr, 0.0)[None], jnp.where(inside, pi, 0.0)[None]
        dr, di = _cmul(qr, qi, pltpu.roll(xr, shift, axis=1), pltpu.roll(xi, shift, axis=1))
        xr, xi = xr + dr, xi + di
        er = jnp.where(inside, pltpu.roll(tr, shift, axis=0), 1.0)
        ei = jnp.where(inside, pltpu.roll(ti, shift, axis=0), 0.0)
        tr, ti = _cmul(tr, ti, er, ei)
        pr, pi = _cmul(pr, pi, pr, pi)
        s *= 2
    xr, xi = xr.reshape(n, width), xi.reshape(n, width)
    nlb = width // DH

    def lanes(x, j):
        return x[:, j * DH:(j + 1) * DH]

    for j in range(nlb):
        hr_ref[j] = lanes(xr, j)
        hi_ref[j] = lanes(xi, j)
    edge = pl.ds(0 if reverse else SUB - 1, ngroups, stride=SUB)
    gr = jnp.concatenate([hr_ref.at[j][edge, :] for j in range(nlb)], axis=1)
    gi = jnp.concatenate([hi_ref.at[j][edge, :] for j in range(nlb)], axis=1)
    s = 1
    while s < ngroups:
        dr, di = _cmul(pr, pi, shift_groups(gr, s), shift_groups(gi, s))
        gr, gi = gr + dr, gi + di
        pr, pi = _cmul(pr, pi, pr, pi)
        s *= 2
    cr_ref[...] = shift_groups(gr, 1)
    ci_ref[...] = shift_groups(gi, 1)
    for g in range(ngroups):
        rows = slice(g * SUB, (g + 1) * SUB)
        dr, di = _cmul(tr, ti, cr_ref[g:g + 1, :], ci_ref[g:g + 1, :])
        for j in range(nlb):
            hr_ref[j, rows, :] += lanes(dr, j)
            hi_ref[j, rows, :] += lanes(di, j)
    return (jnp.concatenate([hr_ref[j] for j in range(nlb)], axis=1),
            jnp.concatenate([hi_ref[j] for j in range(nlb)], axis=1))


def _s5_states(u, lam_ref, b_ref, car_ref, hr_ref, hi_ref, cr_ref, ci_ref):
    lr, li = lam_ref[0], lam_ref[1]
    first = _rows((u.shape[0], SW)) == 0
    inr, ini = _cmul(lr, li, car_ref[0:1, :], car_ref[1:2, :])
    xr = _mm(u, b_ref[0]) + jnp.where(first, inr, 0.0)
    xi = _mm(u, b_ref[1]) + jnp.where(first, ini, 0.0)
    return _scan_tile(xr, xi, lr, li, hr_ref, hi_ref, cr_ref, ci_ref, False)


def _s5_params_fn(ar, ai, ldt, br2, bi2):
    dt = jnp.exp(ldt)
    mag = jnp.exp(ar * dt)
    lr, li = mag * jnp.cos(ai * dt), mag * jnp.sin(ai * dt)
    den = ar * ar + ai * ai
    fr = ((lr - 1.0) * ar + li * ai) / den
    fi = (li * ar - (lr - 1.0) * ai) / den
    expand = jnp.where(_cols((NS, NS * GS)) // GS == _rows((NS, NS * GS)), 1.0, 0.0).astype(F32)
    fr2, fi2 = _hmm(fr, expand), _hmm(fi, expand)
    return lr, li, fr2 * br2 - fi2 * bi2, fr2 * bi2 + fi2 * br2


def _head_norm(o, hn):
    parts = []
    for h in range(NH):
        oh = o[:, h * DH:(h + 1) * DH]
        parts.append(oh * lax.rsqrt(jnp.mean(oh * oh, axis=-1, keepdims=True) + EPS) * hn)
    return jnp.concatenate(parts, axis=1)


def _mix_pre(s5y, u, dvec):
    return _gelu(s5y + dvec * u)


def _mix_mid(o, za, y0, gl, zb, ra, rb, hn):
    ya = _head_norm(o, hn) * _silu(za)
    yb = y0 * _sigmoid(gl) * _silu(zb)
    return _sigmoid(ra) * ya + _sigmoid(rb) * yb


def _mix_post(x, out, npost):
    return x + _rms(out, npost)


def _tile(t, want):
    return min(t, want)


def _row_tile(rows, want):
    return max(r for r in range(16, want + 1, 16) if rows % r == 0)


def _call_carrying(body, name, grid, in_specs, out_specs, out_shape, scratch, args, semantics, exchange):
    n_out = len(out_shape)
    if exchange is not None:
        body = _carry(body, len(args), n_out, len(scratch), exchange, grid)
        in_specs, out_specs = in_specs + exchange.in_specs, out_specs + exchange.out_specs
        out_shape, scratch, args = out_shape + exchange.out_shape, scratch + exchange.scratch_shapes, args + exchange.srcs
        semantics = ("arbitrary",) * len(grid)
    outs = pl.pallas_call(body, name=name, grid=grid, in_specs=in_specs, out_specs=out_specs, out_shape=out_shape,
                          scratch_shapes=scratch, compiler_params=_cparams(semantics))(*args)
    return outs[:n_out], outs[n_out:]


def _inproj_fwd(x, gain, wcat, l, exchange=None):
    t = x.shape[0]
    tm, tn = _tile(t, 1024), 640

    def body(x_ref, g_ref, w_ref, o_ref, h_ref):
        @pl.when(pl.program_id(1) == 0)
        def _():
            h_ref[...] = _rms(x_ref[...], g_ref[...]).astype(h_ref.dtype)
        o_ref[...] = _dot_bf16(h_ref[...], w_ref[...], ((1,), (0,)))

    (proj, h), fetched = _call_carrying(
        body, "inproj_fwd", (t // tm, NCOL // tn),
        [pl.BlockSpec((tm, D), lambda i, j: (i, 0)), _full((1, D)), pl.BlockSpec((None, D, tn), lambda i, j: (l, 0, j))],
        [pl.BlockSpec((tm, tn), lambda i, j: (i, j)), pl.BlockSpec((tm, D), lambda i, j: (i, 0))],
        [jax.ShapeDtypeStruct((t, NCOL), F32), jax.ShapeDtypeStruct((t, D), wcat.dtype)],
        [], [x, gain, wcat], ("parallel", "arbitrary"), exchange)
    return proj, h, fetched


def _inproj_bwd_dx(dproj, wcat, x, gain, dxres, l, exchange=None):
    t = x.shape[0]
    tm, tk = _tile(t, 1024), 640
    nk = NCOL // tk

    def body(dp_ref, w_ref, x_ref, g_ref, r_ref, dx_ref, dg_ref, acc_ref):
        i, k = pl.program_id(0), pl.program_id(1)

        @pl.when(k == 0)
        def _():
            acc_ref[...] = jnp.zeros_like(acc_ref)

        acc_ref[...] += _mm_nt(dp_ref[...], w_ref[...])

        @pl.when(k == nk - 1)
        def _():
            _, vjp = jax.vjp(_rms, x_ref[...], g_ref[...])
            dx, dg = vjp(acc_ref[...])
            dx_ref[...] = r_ref[...] + dx

            @pl.when(i == 0)
            def _():
                dg_ref[...] = dg

            @pl.when(i > 0)
            def _():
                dg_ref[...] += dg

    grid = (t // tm, nk)
    in_specs = [pl.BlockSpec((tm, tk), lambda i, k: (i, k)), pl.BlockSpec((None, D, tk), lambda i, k: (l, 0, k)),
                pl.BlockSpec((tm, D), lambda i, k: (i, 0)), _full((1, D)), pl.BlockSpec((tm, D), lambda i, k: (i, 0))]
    out_specs = [pl.BlockSpec((tm, D), lambda i, k: (i, 0)), _full((1, D))]
    out_shape = [jax.ShapeDtypeStruct((t, D), F32), jax.ShapeDtypeStruct((1, D), F32)]
    scratch, args = [pltpu.VMEM((tm, D), F32)], [dproj, wcat, x, gain, dxres]
    if exchange is not None:
        body = _carry(body, len(args), len(out_shape), len(scratch), exchange, grid)
        in_specs, out_specs = in_specs + exchange.in_specs, out_specs + exchange.out_specs
        out_shape, scratch, args = out_shape + exchange.out_shape, scratch + exchange.scratch_shapes, args + exchange.srcs
    outs = pl.pallas_call(
        body, name="inproj_bwd_dx", grid=grid, in_specs=in_specs, out_specs=out_specs, out_shape=out_shape,
        scratch_shapes=scratch, compiler_params=_cparams(("arbitrary", "arbitrary")),
    )(*args)
    return outs[0], outs[1], outs[2:]


def _inproj_bwd_dw(h, dproj):
    t = h.shape[0]
    tm, tn = _tile(t, 512), 1664

    def body(h_ref, dp_ref, o_ref):
        @pl.when(pl.program_id(1) == 0)
        def _():
            o_ref[...] = jnp.zeros_like(o_ref)

        o_ref[...] += _mm_tn(h_ref[...], dp_ref[...])

    return pl.pallas_call(
        body, name="inproj_bwd_dw", grid=(NCOL // tn, t // tm),
        in_specs=[pl.BlockSpec((tm, D), lambda j, i: (i, 0)), pl.BlockSpec((tm, tn), lambda j, i: (i, j))],
        out_specs=pl.BlockSpec((D, tn), lambda j, i: (0, j)),
        out_shape=jax.ShapeDtypeStruct((D, NCOL), F32),
        compiler_params=_cparams(("parallel", "arbitrary")),
    )(h, dproj)


def _prep_fwd(proj, cw):
    t = proj.shape[0]

    def body(p_ref, w_ref, o_ref):
        qk = (pl.program_id(0) < 2 * NH).astype(F32)
        o_ref[...] = _prep_fn(p_ref[...], w_ref[0:1, :], w_ref[1:2, :], w_ref[2:3, :], w_ref[3:4, :], qk)

    return pl.pallas_call(
        body, name="prep_fwd", grid=(3 * NH,),
        in_specs=[pl.BlockSpec((t, DH), lambda c: (0, c)), pl.BlockSpec((4, DH), lambda c: (0, c))],
        out_specs=pl.BlockSpec((None, t, DH), lambda c: (c, 0, 0)),
        out_shape=jax.ShapeDtypeStruct((3 * NH, t, DH), F32),
        compiler_params=_cparams(("parallel",)),
    )(proj, cw)


def _prep_bwd(proj, cw, dq, dk, dv):
    t = proj.shape[0]

    def body(p_ref, w_ref, dq_ref, dk_ref, dv_ref, dp_ref, dw_ref):
        c = pl.program_id(0)
        qk = (c < 2 * NH).astype(F32)
        _, vjp = jax.vjp(lambda p, w0, w1, w2, w3: _prep_fn(p, w0, w1, w2, w3, qk),
                         p_ref[...], w_ref[0:1, :], w_ref[1:2, :], w_ref[2:3, :], w_ref[3:4, :])
        d = jnp.where(c < NH, dq_ref[...], jnp.where(c < 2 * NH, dk_ref[...], dv_ref[...]))
        dp, dw0, dw1, dw2, dw3 = vjp(d)
        dp_ref[...] = dp.astype(dp_ref.dtype)
        dw_ref[0:1, :] = dw0
        dw_ref[1:2, :] = dw1
        dw_ref[2:3, :] = dw2
        dw_ref[3:4, :] = dw3

    return pl.pallas_call(
        body, name="prep_bwd", grid=(3 * NH,),
        in_specs=[pl.BlockSpec((t, DH), lambda c: (0, c)), pl.BlockSpec((4, DH), lambda c: (0, c))]
        + [pl.BlockSpec((None, t, DH), functools.partial(lambda c, off: (jnp.clip(c - off, 0, NH - 1), 0, 0), off=off))
           for off in (0, NH, 2 * NH)],
        out_specs=[pl.BlockSpec((t, DH), lambda c: (0, c)), pl.BlockSpec((4, DH), lambda c: (0, c))],
        out_shape=[jax.ShapeDtypeStruct((t, 3 * D), GRAD_ACT), jax.ShapeDtypeStruct((4, 3 * D), F32)],
        compiler_params=_cparams(("arbitrary",)),
    )(proj, cw, dq, dk, dv)


def _gates_fwd(proj, gvec):
    t = proj.shape[0]
    tm = _tile(t, 512)

    def body(p_ref, gv_ref, b_ref, g_ref):
        outs = _gates_fn(p_ref[...], gv_ref[0:1, :], gv_ref[1:2, :])
        for h in range(NH):
            b_ref[h] = outs[h]
            g_ref[h] = outs[NH + h]

    spec = pl.BlockSpec((NH, tm, DH), lambda i: (0, i, 0))
    return pl.pallas_call(
        body, name="gates_fwd", grid=(t // tm,),
        in_specs=[pl.BlockSpec((tm, DH), lambda i: (i, BD0 // DH)), _full((8, DH))],
        out_specs=[spec, spec],
        out_shape=[jax.ShapeDtypeStruct((NH, t, DH), F32)] * 2,
        compiler_params=_cparams(("parallel",)),
    )(proj, gvec)


def _gates_bwd(proj, gvec, dbb, dgcb):
    t = proj.shape[0]
    tm = _tile(t, 512)

    def body(p_ref, gv_ref, db_ref, dg_ref, dp_ref, dgv_ref):
        _, vjp = jax.vjp(_gates_fn, p_ref[...], gv_ref[0:1, :], gv_ref[1:2, :])
        cts = tuple(db_ref[h] for h in range(NH)) + tuple(dg_ref[h] for h in range(NH))
        dp, da, db = vjp(cts)
        dp_ref[...] = dp.astype(dp_ref.dtype)

        @pl.when(pl.program_id(0) == 0)
        def _():
            dgv_ref[...] = jnp.zeros_like(dgv_ref)

        dgv_ref[0:1, :] += da
        dgv_ref[1:2, :] += db

    spec = pl.BlockSpec((NH, tm, DH), lambda i: (0, i, 0))
    return pl.pallas_call(
        body, name="gates_bwd", grid=(t // tm,),
        in_specs=[pl.BlockSpec((tm, DH), lambda i: (i, BD0 // DH)), _full((8, DH)), spec, spec],
        out_specs=[pl.BlockSpec((tm, DH), lambda i: (i, 0)), _full((8, DH))],
        out_shape=[jax.ShapeDtypeStruct((t, DH), GRAD_ACT), jax.ShapeDtypeStruct((8, DH), F32)],
        compiler_params=_cparams(("arbitrary",)),
    )(proj, gvec, dbb, dgcb)


def _chunks_per_step(nch):
    return max(c for c in (8, 4, 2, 1) if nch % c == 0)


def _grid_ends(grid):
    def first():
        return functools.reduce(jnp.logical_and, [pl.program_id(a) == 0 for a in range(len(grid))])

    def last():
        return functools.reduce(jnp.logical_and, [pl.program_id(a) == n - 1 for a, n in enumerate(grid)])

    return first, last


def _carry(body, n_in, n_out, n_scratch, exchange, grid):
    first, last = _grid_ends(grid)
    na = exchange.narr

    def wrapped(*refs):
        a, b = n_in, n_in + na
        c, d = b + n_out, b + n_out + na
        e = d + n_scratch
        srcs, dsts, sems = refs[a:b], refs[c:d], refs[e:]

        @pl.when(first())
        def _():
            exchange.start(srcs, dsts, sems)

        body(*(refs[:a] + refs[b:c] + refs[d:e]))

        @pl.when(last())
        def _():
            exchange.wait(srcs, dsts, sems)

    return wrapped


def _delta_local_fwd(qkv, bb, gcb, exchange=None):
    t = qkv.shape[1]
    cps = _chunks_per_step(t // CH)
    rows = cps * CH
    grid = (NH, t // rows)

    def body(q_ref, k_ref, v_ref, b_ref, g_ref, *out_refs):
        slices = [slice(c * CH, (c + 1) * CH) for c in range(cps)]
        results = _chunks_local([tuple(ref[sl, :] for ref in (q_ref, k_ref, v_ref, b_ref, g_ref)) for sl in slices])
        for sl, (outs, t_inv) in zip(slices, results):
            for ref, val in zip(out_refs, outs + (t_inv,)):
                ref[sl, :] = val

    def blk(off):
        return pl.BlockSpec((None, rows, DH), lambda h, n: (h + off, n, 0))

    in_specs = [blk(0), blk(NH), blk(2 * NH), blk(0), blk(0)]
    out_specs = [blk(0)] * 6
    out_shape = [jax.ShapeDtypeStruct((NH, t, DH), F32)] * 6
    args, scratch, sem = [qkv, qkv, qkv, bb, gcb], [], ("parallel", "parallel")
    if exchange is not None:
        body = _carry(body, 5, 6, 0, exchange, grid)
        in_specs, out_specs = in_specs + exchange.in_specs, out_specs + exchange.out_specs
        out_shape, scratch, args = out_shape + exchange.out_shape, exchange.scratch_shapes, args + exchange.srcs
        sem = ("arbitrary", "arbitrary")
    outs = pl.pallas_call(
        body, name="delta_local_fwd", grid=grid, in_specs=in_specs, out_specs=out_specs, out_shape=out_shape,
        scratch_shapes=scratch, compiler_params=_cparams(sem),
    )(*args)
    return outs[:5], outs[5], outs[6:]


def _delta_local_bwd(qkv, bb, gcb, t_inv, cts, dgcb_state):
    t = qkv.shape[1]
    cps = _chunks_per_step(t // CH)
    rows = cps * CH

    def body(q_ref, k_ref, v_ref, b_ref, g_ref, ti_ref, du_ref, dw_ref, dqd_ref, dkd_ref, da_ref, dgs_ref,
             dq_ref, dk_ref, dv_ref, db_ref, dg_ref):
        slices = [slice(c * CH, (c + 1) * CH) for c in range(cps)]
        items = [tuple(ref[sl, :] for ref in (q_ref, k_ref, v_ref, b_ref, g_ref, ti_ref)) for sl in slices]
        cts = [tuple(ref[sl, :] for ref in (du_ref, dw_ref, dqd_ref, dkd_ref, da_ref)) for sl in slices]
        for sl, (dq, dk, dv, db, dg, _) in zip(slices, _side_by_side_vjp(_chunks_local_known, items, cts)):
            dq_ref[sl, :] = dq
            dk_ref[sl, :] = dk
            dv_ref[sl, :] = dv
            db_ref[sl, :] = db
            dg_ref[sl, :] = dg + dgs_ref[sl, :]

    def blk(off):
        return pl.BlockSpec((None, rows, DH), lambda h, n: (h + off, n, 0))

    return pl.pallas_call(
        body, name="delta_local_bwd", grid=(NH, t // rows),
        in_specs=[blk(0), blk(NH), blk(2 * NH)] + [blk(0)] * 9,
        out_specs=[blk(0)] * 5,
        out_shape=[jax.ShapeDtypeStruct((NH, t, DH), F32)] * 5,
        compiler_params=_cparams(("parallel", "parallel")),
    )(qkv, qkv, qkv, bb, gcb, t_inv, *cts, dgcb_state)


def _delta_state_fwd(local, gcb):
    t = gcb.shape[1]
    nch = t // CH

    def body(u_ref, w_ref, qd_ref, kd_ref, a_ref, g_ref, o_ref, s_ref, st_ref):
        @pl.when(pl.program_id(0) == 0)
        def _():
            st_ref[...] = jnp.zeros_like(st_ref)

        s_ref[...] = st_ref[...]
        items = [tuple(ref[h] for ref in (u_ref, w_ref, qd_ref, kd_ref, a_ref, g_ref, st_ref)) for h in range(NH)]
        for h, (o, ns) in enumerate(_state_steps(items)):
            o_ref[:, h * DH:(h + 1) * DH] = o
            st_ref[h] = ns

    blk = pl.BlockSpec((NH, CH, DH), lambda n: (0, n, 0))
    return pl.pallas_call(
        body, name="delta_state_fwd", grid=(nch,),
        in_specs=[blk] * 6,
        out_specs=[pl.BlockSpec((CH, D), lambda n: (n, 0)),
                   pl.BlockSpec((NH, None, DH, DH), lambda n: (0, n, 0, 0))],
        out_shape=[jax.ShapeDtypeStruct((t, D), F32), jax.ShapeDtypeStruct((NH, nch, DH, DH), F32)],
        scratch_shapes=[pltpu.VMEM((NH, DH, DH), F32)],
        compiler_params=_cparams(("arbitrary",)),
    )(*local, gcb)


def _delta_state_bwd(local, gcb, states, do):
    t = gcb.shape[1]
    nch = t // CH

    def body(u_ref, w_ref, qd_ref, kd_ref, a_ref, g_ref, s_ref, do_ref,
             du_ref, dw_ref, dqd_ref, dkd_ref, da_ref, dg_ref, ds_ref):
        @pl.when(pl.program_id(0) == 0)
        def _():
            ds_ref[...] = jnp.zeros_like(ds_ref)

        items = [tuple(ref[h] for ref in (u_ref, w_ref, qd_ref, kd_ref, a_ref, g_ref, s_ref)) for h in range(NH)]
        cts = [(do_ref[:, h * DH:(h + 1) * DH], ds_ref[h]) for h in range(NH)]
        for h, (du, dw, dqd, dkd, da, dg, ds) in enumerate(_side_by_side_vjp(_state_steps, items, cts)):
            du_ref[h] = du
            dw_ref[h] = dw
            dqd_ref[h] = dqd
            dkd_ref[h] = dkd
            da_ref[h] = da
            dg_ref[h] = dg
            ds_ref[h] = ds

    blk = pl.BlockSpec((NH, CH, DH), lambda n: (0, nch - 1 - n, 0))
    return pl.pallas_call(
        body, name="delta_state_bwd", grid=(nch,),
        in_specs=[blk] * 6 + [pl.BlockSpec((NH, None, DH, DH), lambda n: (0, nch - 1 - n, 0, 0)),
                              pl.BlockSpec((CH, D), lambda n: (nch - 1 - n, 0))],
        out_specs=[blk] * 6,
        out_shape=[jax.ShapeDtypeStruct((NH, t, DH), F32)] * 6,
        scratch_shapes=[pltpu.VMEM((NH, DH, DH), F32)],
        compiler_params=_cparams(("arbitrary",)),
    )(*local, gcb, states, do)


def _s5_params(ar, ai, ldt, br2, bi2):
    def body(ar_ref, ai_ref, ld_ref, br_ref, bi_ref, lr_ref, li_ref, bbr_ref, bbi_ref):
        lr, li, bbr, bbi = _s5_params_fn(ar_ref[...], ai_ref[...], ld_ref[...], br_ref[...], bi_ref[...])
        lr_ref[...] = lr
        li_ref[...] = li
        bbr_ref[...] = bbr
        bbi_ref[...] = bbi

    sq = pl.BlockSpec((None, NG, NS), lambda l: (l, 0, 0))
    wide = pl.BlockSpec((None, NG, NS * GS), lambda l: (l, 0, 0))
    return pl.pallas_call(
        body, name="s5_params", grid=(DEPTH,),
        in_specs=[sq, sq, pl.BlockSpec((None, NG, 1), lambda l: (l, 0, 0)), wide, wide],
        out_specs=[sq, sq, wide, wide],
        out_shape=[jax.ShapeDtypeStruct((DEPTH, NG, NS), F32)] * 2
        + [jax.ShapeDtypeStruct((DEPTH, NG, NS * GS), F32)] * 2,
        compiler_params=_cparams(("parallel",)),
    )(ar, ai, ldt, br2, bi2)


def _s5_params_bwd(ar, ai, ldt, br2, bi2, dlr, dli, dbbr, dbbi):
    def body(ar_ref, ai_ref, ld_ref, br_ref, bi_ref, a_ref, b_ref, c_ref, d_ref,
             dar_ref, dai_ref, dld_ref, dbr_ref, dbi_ref):
        _, vjp = jax.vjp(_s5_params_fn, ar_ref[...], ai_ref[...], ld_ref[...], br_ref[...], bi_ref[...])
        dar, dai, dld, dbr, dbi = vjp((a_ref[...], b_ref[...], c_ref[...], d_ref[...]))
        dar_ref[...] = dar
        dai_ref[...] = dai
        dld_ref[...] = dld
        dbr_ref[...] = dbr
        dbi_ref[...] = dbi

    sq = pl.BlockSpec((None, NG, NS), lambda l: (l, 0, 0))
    col = pl.BlockSpec((None, NG, 1), lambda l: (l, 0, 0))
    wide = pl.BlockSpec((None, NG, NS * GS), lambda l: (l, 0, 0))
    return pl.pallas_call(
        body, name="s5_params_bwd", grid=(DEPTH,),
        in_specs=[sq, sq, col, wide, wide, sq, sq, wide, wide],
        out_specs=[sq, sq, col, wide, wide],
        out_shape=[jax.ShapeDtypeStruct((DEPTH, NG, NS), F32)] * 2 + [jax.ShapeDtypeStruct((DEPTH, NG, 1), F32)]
        + [jax.ShapeDtypeStruct((DEPTH, NG, NS * GS), F32)] * 2,
        compiler_params=_cparams(("parallel",)),
    )(ar, ai, ldt, br2, bi2, dlr, dli, dbbr, dbbi)


def _s5_tile_rows(t):
    return _tile(t // 2, 1024)


def _s5_fwd(proj, lam, bblk, cblk, exchange=None):
    t = proj.shape[0]
    r = _s5_tile_rows(t)
    nt = t // r
    u0 = 4 * D // DH

    def body(u_ref, lam_ref, b_ref, c_ref, y_ref, car_ref, st_ref, hr_ref, hi_ref, cr_ref, ci_ref):
        @pl.when(pl.program_id(1) == 0)
        def _():
            st_ref[...] = jnp.zeros_like(st_ref)

        car_ref[...] = st_ref[...]
        hr, hi = _s5_states(u_ref[...], lam_ref, b_ref, st_ref, hr_ref, hi_ref, cr_ref, ci_ref)
        y_ref[...] = _mm(hr, c_ref[0]) - _mm(hi, c_ref[1])
        st_ref[0:1, :] = _last_row(hr)
        st_ref[1:2, :] = _last_row(hi)

    scratch = ([pltpu.VMEM((8, SW), F32)] + [pltpu.VMEM((SW // DH, r, DH), F32)] * 2
               + [pltpu.VMEM((r // SUB, SW), F32)] * 2)
    (y, carries), fetched = _call_carrying(
        body, "s5_fwd", (NCB, nt),
        [pl.BlockSpec((r, DH), lambda c, i: (i, u0 + c)), pl.BlockSpec((2, 1, SW), lambda c, i: (0, 0, c)),
         pl.BlockSpec((2, None, DH, SW), lambda c, i: (0, c, 0, 0)),
         pl.BlockSpec((2, None, SW, DH), lambda c, i: (0, c, 0, 0))],
        [pl.BlockSpec((r, DH), lambda c, i: (i, c)), pl.BlockSpec((None, 8, SW), lambda c, i: (i, 0, c))],
        [jax.ShapeDtypeStruct((t, D), F32), jax.ShapeDtypeStruct((nt, 8, NG * NS), F32)],
        scratch, [proj, lam, bblk, cblk], ("parallel", "arbitrary"), exchange)
    return y, carries, fetched


def _s5_bwd(proj, lam, bblk, cblk, carries, dy, du_skip, exchange=None):
    t = proj.shape[0]
    r = _s5_tile_rows(t)
    nt = t // r
    u0 = 4 * D // DH

    def body(u_ref, lam_ref, b_ref, c_ref, car_ref, dy_ref, dus_ref, du_ref, dlam_ref, db_ref, dc_ref, dst_ref,
             hr_ref, hi_ref, ar_ref, ai_ref, cr_ref, ci_ref):
        first = pl.program_id(1) == 0

        @pl.when(first)
        def _():
            dst_ref[...] = jnp.zeros_like(dst_ref)

        u, dy = u_ref[...], dy_ref[...]
        lr, li = lam_ref[0], lam_ref[1]
        hr, hi = _s5_states(u, lam_ref, b_ref, car_ref, hr_ref, hi_ref, cr_ref, ci_ref)
        dcr2, dci2 = _mm_tn(hr, dy), -_mm_tn(hi, dy)
        last = _rows((r, SW)) == r - 1
        inr, ini = _cmul(lr, -li, dst_ref[0:1, :], dst_ref[1:2, :])
        dhr = _mm_nt(dy, c_ref[0]) + jnp.where(last, inr, 0.0)
        dhi = jnp.where(last, ini, 0.0) - _mm_nt(dy, c_ref[1])
        ar, ai = _scan_tile(dhr, dhi, lr, -li, ar_ref, ai_ref, cr_ref, ci_ref, True)
        top = _rows((r, SW)) == 0
        dst_ref[0:1, :] = jnp.sum(jnp.where(top, ar, 0.0), axis=0, keepdims=True)
        dst_ref[1:2, :] = jnp.sum(jnp.where(top, ai, 0.0), axis=0, keepdims=True)
        du_ref[...] = (_mm_nt(ar, b_ref[0]) + _mm_nt(ai, b_ref[1]) + dus_ref[...]).astype(du_ref.dtype)
        dbr, dbi = _mm_tn(u, ar), _mm_tn(u, ai)
        pr = _sd(hr, 1) + jnp.where(top, car_ref[0:1, :], 0.0)
        pi = _sd(hi, 1) + jnp.where(top, car_ref[1:2, :], 0.0)
        dlr = jnp.sum(ar * pr + ai * pi, axis=0, keepdims=True)
        dli = jnp.sum(ai * pr - ar * pi, axis=0, keepdims=True)

        @pl.when(first)
        def _():
            dlam_ref[0] = dlr
            dlam_ref[1] = dli
            db_ref[0] = dbr
            db_ref[1] = dbi
            dc_ref[0] = dcr2
            dc_ref[1] = dci2

        @pl.when(jnp.logical_not(first))
        def _():
            dlam_ref[0] += dlr
            dlam_ref[1] += dli
            db_ref[0] += dbr
            db_ref[1] += dbi
            dc_ref[0] += dcr2
            dc_ref[1] += dci2

    grid = (NCB, nt)
    in_specs = [pl.BlockSpec((r, DH), lambda c, i: (nt - 1 - i, u0 + c)),
                pl.BlockSpec((2, 1, SW), lambda c, i: (0, 0, c)),
                pl.BlockSpec((2, None, DH, SW), lambda c, i: (0, c, 0, 0)),
                pl.BlockSpec((2, None, SW, DH), lambda c, i: (0, c, 0, 0)),
                pl.BlockSpec((None, 8, SW), lambda c, i: (nt - 1 - i, 0, c)),
                pl.BlockSpec((r, DH), lambda c, i: (nt - 1 - i, c)),
                pl.BlockSpec((r, DH), lambda c, i: (nt - 1 - i, c))]
    out_specs = [pl.BlockSpec((r, DH), lambda c, i: (nt - 1 - i, c)),
                 pl.BlockSpec((2, 1, SW), lambda c, i: (0, 0, c)),
                 pl.BlockSpec((2, None, DH, SW), lambda c, i: (0, c, 0, 0)),
                 pl.BlockSpec((2, None, SW, DH), lambda c, i: (0, c, 0, 0))]
    out_shape = [jax.ShapeDtypeStruct((t, D), GRAD_ACT), jax.ShapeDtypeStruct((2, 1, NG * NS), F32),
                 jax.ShapeDtypeStruct((2, NCB, DH, SW), F32), jax.ShapeDtypeStruct((2, NCB, SW, DH), F32)]
    scratch = ([pltpu.VMEM((8, SW), F32)] + [pltpu.VMEM((SW // DH, r, DH), F32)] * 4
               + [pltpu.VMEM((r // SUB, SW), F32)] * 2)
    args, sem = [proj, lam, bblk, cblk, carries, dy, du_skip], ("parallel", "arbitrary")
    if exchange is not None:
        body = _carry(body, len(args), len(out_shape), len(scratch), exchange, grid)
        in_specs, out_specs = in_specs + exchange.in_specs, out_specs + exchange.out_specs
        out_shape, scratch, args = out_shape + exchange.out_shape, scratch + exchange.scratch_shapes, args + exchange.srcs
        sem = ("arbitrary", "arbitrary")
    outs = pl.pallas_call(
        body, name="s5_bwd", grid=grid, in_specs=in_specs, out_specs=out_specs, out_shape=out_shape,
        scratch_shapes=scratch, compiler_params=_cparams(sem),
    )(*args)
    return outs[:4], outs[4:]


def _proj_spec(tm, col):
    return pl.BlockSpec((tm, D), lambda i: (i, col))


def _layer_mat(l):
    return pl.BlockSpec((None, D, D), lambda i: (l, 0, 0))


def _mix_fwd(proj, o, s5y, x, hn, dvec, wglu, bglu, wout, npost, l):
    t = x.shape[0]
    tm = _tile(t, 256)

    def body(za_ref, u_ref, zb_ref, ra_ref, rb_ref, o_ref, y_ref, x_ref, hn_ref, d_ref, wg_ref, bg_ref, wo_ref,
             np_ref, xn_ref):
        y0 = _mix_pre(y_ref[...], u_ref[...], d_ref[...])
        gl = _mm(y0, wg_ref[...]) + bg_ref[...]
        m = _mix_mid(o_ref[...], za_ref[...], y0, gl, zb_ref[...], ra_ref[...], rb_ref[...], hn_ref[...])
        out = _mm(m, wo_ref[...])
        xn_ref[...] = _mix_post(x_ref[...], out, np_ref[...])

    act = pl.BlockSpec((tm, D), lambda i: (i, 0))
    return pl.pallas_call(
        body, name="mix_fwd", grid=(t // tm,),
        in_specs=[_proj_spec(tm, 3), _proj_spec(tm, 4), _proj_spec(tm, 5), _proj_spec(tm, 6), _proj_spec(tm, 7),
                  act, act, act, _full((1, DH)), _full((1, D)), _layer_mat(l), _full((1, D)), _layer_mat(l),
                  _full((1, D))],
        out_specs=act,
        out_shape=jax.ShapeDtypeStruct((t, D), F32),
        compiler_params=_cparams(("parallel",)),
    )(proj, proj, proj, proj, proj, o, s5y, x, hn, dvec, wglu, bglu, wout, npost)


def _mix_bwd(proj, o, s5y, x, hn, dvec, wglu, bglu, wout, npost, dxn, l):
    t = x.shape[0]
    tm = _tile(t, 128)

    def body(za_ref, u_ref, zb_ref, ra_ref, rb_ref, o_ref, y_ref, x_ref, hn_ref, d_ref, wg_ref, bg_ref, wo_ref,
             np_ref, dxn_ref,
             dza_ref, du_ref, dzb_ref, dra_ref, drb_ref, do_ref, dy_ref, dx_ref,
             y0_ref, dgl_ref, m_ref, dout_ref, dvecs_ref, dhn_ref):
        y0, vjp_pre = jax.vjp(_mix_pre, y_ref[...], u_ref[...], d_ref[...])
        gl = _mm(y0, wg_ref[...]) + bg_ref[...]
        m, vjp_mid = jax.vjp(_mix_mid, o_ref[...], za_ref[...], y0, gl, zb_ref[...], ra_ref[...], rb_ref[...],
                             hn_ref[...])
        out = _mm(m, wo_ref[...])
        _, vjp_post = jax.vjp(_mix_post, x_ref[...], out, np_ref[...])
        dx, dout, dnp = vjp_post(dxn_ref[...])
        dm = _mm_nt(dout, wo_ref[...])
        do, dza, dy0, dgl, dzb, dra, drb, dhn = vjp_mid(dm)
        y0_ref[...] = y0.astype(BF16)
        dgl_ref[...] = dgl.astype(BF16)
        m_ref[...] = m.astype(BF16)
        dout_ref[...] = dout.astype(BF16)
        dbg = jnp.sum(dgl, axis=0, keepdims=True)
        dy0 = dy0 + _mm_nt(dgl, wg_ref[...])
        dy, du, dd = vjp_pre(dy0)
        dza_ref[...] = dza.astype(dza_ref.dtype)
        du_ref[...] = du
        dzb_ref[...] = dzb.astype(dzb_ref.dtype)
        dra_ref[...] = dra.astype(dra_ref.dtype)
        drb_ref[...] = drb.astype(drb_ref.dtype)
        do_ref[...] = do
        dy_ref[...] = dy
        dx_ref[...] = dx
        first = pl.program_id(0) == 0

        @pl.when(first)
        def _():
            dvecs_ref[...] = jnp.zeros_like(dvecs_ref)
            dhn_ref[...] = jnp.zeros_like(dhn_ref)

        dvecs_ref[0:1, :] += dd
        dvecs_ref[1:2, :] += dbg
        dvecs_ref[2:3, :] += dnp
        dhn_ref[0:1, :] += dhn

    act = pl.BlockSpec((tm, D), lambda i: (i, 0))
    a, ga = jax.ShapeDtypeStruct((t, D), F32), jax.ShapeDtypeStruct((t, D), GRAD_ACT)
    b16 = jax.ShapeDtypeStruct((t, D), BF16)
    outs = pl.pallas_call(
        body, name="mix_bwd", grid=(t // tm,),
        in_specs=[_proj_spec(tm, 3), _proj_spec(tm, 4), _proj_spec(tm, 5), _proj_spec(tm, 6), _proj_spec(tm, 7),
                  act, act, act, _full((1, DH)), _full((1, D)), _layer_mat(l), _full((1, D)), _layer_mat(l),
                  _full((1, D)), act],
        out_specs=[act] * 12 + [_full((8, D)), _full((8, DH))],
        out_shape=[ga, a, ga, ga, ga, a, a, a, b16, b16, b16, b16, jax.ShapeDtypeStruct((8, D), F32),
                   jax.ShapeDtypeStruct((8, DH), F32)],
        compiler_params=_cparams(("arbitrary",)),
    )(proj, proj, proj, proj, proj, o, s5y, x, hn, dvec, wglu, bglu, wout, npost, dxn)
    y0, dgl, m, dout = outs[8:12]
    return list(outs[:8]) + [_weight_grad(y0, dgl, "glu_dw"), _weight_grad(m, dout, "out_dw")] + list(outs[12:])


def _weight_grad(a, b, name):
    t = a.shape[0]
    tk = _tile(t, 1024)

    def body(a_ref, b_ref, o_ref):
        @pl.when(pl.program_id(0) == 0)
        def _():
            o_ref[...] = jnp.zeros_like(o_ref)

        o_ref[...] += _mm_tn(a_ref[...], b_ref[...])

    rows = pl.BlockSpec((tk, D), lambda i: (i, 0))
    return pl.pallas_call(
        body, name=name, grid=(t // tk,), in_specs=[rows, rows], out_specs=_full((D, D)),
        out_shape=jax.ShapeDtypeStruct((D, D), F32), compiler_params=_cparams(("arbitrary",)),
    )(a, b)


def _loss_grad(y, target):
    t = y.shape[0]
    tm = _tile(t, 512)

    def body(y_ref, t_ref, dy_ref, l_ref):
        err = y_ref[...] - t_ref[...]
        dy_ref[...] = err * (1.0 / D)
        part = jnp.sum(jnp.sum(err * err, axis=1, keepdims=True), axis=0, keepdims=True) * (0.5 / D)
        part = jnp.broadcast_to(part, (8, DH))

        @pl.when(pl.program_id(0) == 0)
        def _():
            l_ref[...] = part

        @pl.when(pl.program_id(0) > 0)
        def _():
            l_ref[...] += part

    act = pl.BlockSpec((tm, D), lambda i: (i, 0))
    return pl.pallas_call(
        body, name="loss_grad", grid=(t // tm,),
        in_specs=[act, act], out_specs=[act, _full((8, DH))],
        out_shape=[jax.ShapeDtypeStruct((t, D), F32), jax.ShapeDtypeStruct((8, DH), F32)],
        compiler_params=_cparams(("arbitrary",)),
    )(y, target)


def _flips(rel):
    x, y, c = lax.axis_index("x"), lax.axis_index("y"), lax.axis_index("c")
    fx, fy, fc = rel
    return (x ^ fx if fx else x, y ^ fy if fy else y, c ^ fc if fc else c)


CHIP_RELS = ((1, 0, 0), (0, 1, 0), (1, 1, 0))
ALL_RELS = tuple((fx, fy, fc) for fx in (0, 1) for fy in (0, 1) for fc in (0, 1) if (fx, fy, fc) != (0, 0, 0))


def _slot_of(pos, by_chip):
    px, py, pc = pos
    return 2 * px + py if by_chip else 4 * px + 2 * py + pc


class _Exchange:
    def __init__(self, srcs, rels, by_chip, scatter):
        self.srcs, self.rels, self.by_chip, self.scatter = list(srcs), rels, by_chip, scatter
        self.narr = len(self.srcs)
        nslot, nsem = NCHIP if by_chip else NDEV, self.narr * len(rels)
        self.in_specs = [pl.BlockSpec(memory_space=pl.ANY)] * self.narr
        self.out_specs = [pl.BlockSpec(memory_space=pl.ANY)] * self.narr
        self.out_shape = [jax.ShapeDtypeStruct((nslot,) + s.shape[-2:], s.dtype) for s in self.srcs]
        self.scratch_shapes = [pltpu.SemaphoreType.DMA((nsem,)), pltpu.SemaphoreType.DMA((nsem,)),
                               pltpu.SemaphoreType.DMA((self.narr,))]

    def _copies(self, src_refs, dst_refs, sems):
        send_sems, recv_sems, local_sems = sems
        my_slot = _slot_of(_flips((0, 0, 0)), self.by_chip)
        local, sends, arrivals = [], [], []
        for a, (src_ref, dst_ref) in enumerate(zip(src_refs, dst_refs)):
            local.append(pltpu.make_async_copy(src_ref.at[my_slot] if self.scatter else src_ref, dst_ref.at[my_slot],
                                               local_sems.at[a]))
            for k, rel in enumerate(self.rels):
                peer = _flips(rel)
                pair = dict(send_sem=send_sems.at[a * len(self.rels) + k], recv_sem=recv_sems.at[a * len(self.rels) + k],
                            device_id=peer, device_id_type=pl.DeviceIdType.MESH)
                part = src_ref.at[_slot_of(peer, self.by_chip)] if self.scatter else src_ref
                sends.append(pltpu.make_async_remote_copy(src_ref=part, dst_ref=dst_ref.at[my_slot], **pair))
                arrivals.append(pltpu.make_async_remote_copy(
                    src_ref=src_ref.at[0] if self.scatter else src_ref, dst_ref=dst_ref.at[_slot_of(peer, self.by_chip)],
                    **pair))
        return local, sends, arrivals

    def start(self, src_refs, dst_refs, sems):
        local, sends, _ = self._copies(src_refs, dst_refs, sems)
        for cp in local + sends:
            cp.start()

    def wait(self, src_refs, dst_refs, sems):
        local, sends, arrivals = self._copies(src_refs, dst_refs, sems)
        for cp in arrivals:
            cp.wait_recv()
        for cp in sends:
            cp.wait_send()
        for cp in local:
            cp.wait()


def _exchange(srcs, rels, by_chip, scatter, name):
    ex = _Exchange(srcs, rels, by_chip, scatter)

    def body(*refs):
        parts = refs[:ex.narr], refs[ex.narr:2 * ex.narr], refs[2 * ex.narr:]
        ex.start(*parts)
        ex.wait(*parts)

    return pl.pallas_call(body, name=name, in_specs=ex.in_specs, out_specs=ex.out_specs, out_shape=ex.out_shape,
                          scratch_shapes=ex.scratch_shapes)(*ex.srcs)


def _sibling_swap(srcs, name):
    narr = len(srcs)

    def body(*refs):
        src_refs, dst_refs = refs[:narr], refs[narr:2 * narr]
        send_sems, recv_sems = refs[2 * narr:]
        peer = _flips((0, 0, 1))
        copies = [pltpu.make_async_remote_copy(src_ref=s, dst_ref=d, send_sem=send_sems.at[a], recv_sem=recv_sems.at[a],
                                               device_id=peer, device_id_type=pl.DeviceIdType.MESH)
                  for a, (s, d) in enumerate(zip(src_refs, dst_refs))]
        for cp in copies:
            cp.start()
        for cp in copies:
            cp.wait()

    return pl.pallas_call(
        body, name=name,
        in_specs=[pl.BlockSpec(memory_space=pl.ANY)] * narr,
        out_specs=[pl.BlockSpec(memory_space=pl.ANY)] * narr,
        out_shape=[jax.ShapeDtypeStruct(s.shape, s.dtype) for s in srcs],
        scratch_shapes=[pltpu.SemaphoreType.DMA((narr,)), pltpu.SemaphoreType.DMA((narr,))],
    )(*srcs)


def _all_reduce(src, name):
    rows, cols = src.shape
    r = rows // NDEV
    nrel = len(ALL_RELS)

    def body(src_ref, out_ref, parts_ref, mine_ref, send_sems, recv_sems):
        my_slot = _slot_of(_flips((0, 0, 0)), False)

        def piece(ref, slot):
            return ref.at[pl.ds(pl.multiple_of(slot * r, 8), r), :]

        def copies(phase):
            out = []
            for k, rel in enumerate(ALL_RELS):
                peer = _flips(rel)
                pair = dict(send_sem=send_sems.at[phase * nrel + k], recv_sem=recv_sems.at[phase * nrel + k],
                            device_id=peer, device_id_type=pl.DeviceIdType.MESH)
                if phase == 0:
                    out.append(pltpu.make_async_remote_copy(src_ref=piece(src_ref, _slot_of(peer, False)),
                                                            dst_ref=parts_ref.at[my_slot], **pair))
                else:
                    out.append(pltpu.make_async_remote_copy(src_ref=mine_ref, dst_ref=piece(out_ref, my_slot), **pair))
            return out

        first = copies(0)
        for cp in first:
            cp.start()
        parts_ref[my_slot] = piece(src_ref, my_slot)[...]
        for cp in first:
            cp.wait_recv()
        acc = parts_ref[0]
        for s in range(1, NDEV):
            acc = acc + parts_ref[s]
        mine_ref[...] = acc
        second = copies(1)
        for cp in second:
            cp.start()
        piece(out_ref, my_slot)[...] = acc
        for cp in second:
            cp.wait_recv()
        for cp in first + second:
            cp.wait_send()

    return pl.pallas_call(
        body, name=name,
        in_specs=[pl.BlockSpec(memory_space=pltpu.VMEM)], out_specs=pl.BlockSpec(memory_space=pltpu.VMEM),
        out_shape=jax.ShapeDtypeStruct(src.shape, src.dtype),
        scratch_shapes=[pltpu.VMEM((NDEV, r, cols), src.dtype), pltpu.VMEM((r, cols), src.dtype),
                        pltpu.SemaphoreType.DMA((2 * nrel,)), pltpu.SemaphoreType.DMA((2 * nrel,))],
        compiler_params=pltpu.CompilerParams(vmem_limit_bytes=VMEM_LIMIT),
    )(src)


def _sum_slots(parts, name):
    ns, rows, cols = parts.shape
    tr = _row_tile(rows, 256)

    def body(p_ref, o_ref):
        acc = p_ref[0].astype(F32)
        for s in range(1, ns):
            acc = acc + p_ref[s].astype(F32)
        o_ref[...] = acc

    return pl.pallas_call(
        body, name=name, grid=(rows // tr,),
        in_specs=[pl.BlockSpec((ns, tr, cols), lambda i: (0, i, 0))],
        out_specs=pl.BlockSpec((tr, cols), lambda i: (i, 0)),
        out_shape=jax.ShapeDtypeStruct((rows, cols), F32),
        compiler_params=_cparams(("parallel",)),
    )(parts)


def _adamw(w, g_parts, m, v, name, max_rows=256):
    if w.ndim == 2:
        return [o[0] for o in _adamw(w[None], g_parts, m[None], v[None], name, max_rows)]
    nl, rows, cols = w.shape
    tr = _row_tile(rows, max_rows)
    per_layer = rows // tr
    c1 = 1.0 / (1.0 - ADAM_B1 ** ADAM_STEP)
    c2 = 1.0 / (1.0 - ADAM_B2 ** ADAM_STEP)
    npart = len(g_parts)

    def body(*refs):
        w_ref, m_ref, v_ref = refs[:3]
        g_refs = refs[3:3 + npart]
        go_ref, d_ref, nm_ref, nv_ref = refs[3 + npart:]
        terms = []
        for g_ref in g_refs:
            terms += [g_ref[...]] if len(g_ref.shape) == 2 else [g_ref[s] for s in range(g_ref.shape[0])]
        g = terms[0]
        for term in terms[1:]:
            g = g + term
        nm = ADAM_B1 * m_ref[...] + (1.0 - ADAM_B1) * g
        nv = ADAM_B2 * v_ref[...] + (1.0 - ADAM_B2) * (g * g)
        d_ref[...] = -ADAM_LR * ((nm * c1) / (jnp.sqrt(nv * c2) + ADAM_EPS) + ADAM_WD * w_ref[...])
        go_ref[...] = g
        nm_ref[...] = nm
        nv_ref[...] = nv

    blk = pl.BlockSpec((None, tr, cols), lambda l, i: (l, i, 0))
    g_specs = [pl.BlockSpec((tr, cols), lambda l, i: (l * per_layer + i, 0)) if p.ndim == 2 else
               pl.BlockSpec((p.shape[0], tr, cols), lambda l, i: (0, l * per_layer + i, 0)) for p in g_parts]
    out = jax.ShapeDtypeStruct((nl, rows, cols), F32)
    return pl.pallas_call(
        body, name=name, grid=(nl, per_layer),
        in_specs=[blk, blk, blk] + g_specs,
        out_specs=[blk] * 4, out_shape=[out] * 4,
        compiler_params=_cparams(("parallel", "parallel")),
    )(w, m, v, *g_parts)


WEIGHT_SPLIT = (0, 384, 704, D)
WIN_SHARD = 2052
CONV_SHARD = 768
ROW_SHARD = 256

SMALL = (("norm_pre", (DEPTH, D)), ("a_log", (DEPTH, NH)), ("dt_bias", (DEPTH, NH)), ("head_norm", (DEPTH, DH)),
         ("ssm_a_re", (DEPTH, NG, NS)), ("ssm_a_im", (DEPTH, NG, NS)), ("ssm_log_dt", (DEPTH, NG)),
         ("ssm_b_re", (DEPTH, NG, NS, GS)), ("ssm_b_im", (DEPTH, NG, NS, GS)),
         ("ssm_c_re", (DEPTH, NG, GS, NS)), ("ssm_c_im", (DEPTH, NG, GS, NS)), ("ssm_d", (DEPTH, D)),
         ("b_glu", (DEPTH, D)), ("norm_post", (DEPTH, D)))


def _pad_rows(flat, rows):
    return jnp.pad(flat, (0, rows * D - flat.shape[0])).reshape(rows, D)


def _rows_by_chip(a):
    nl, rows, cols = a.shape
    return a.reshape(nl, NCHIP, rows // NCHIP, cols).transpose(1, 0, 2, 3).reshape(NCHIP, -1, cols)


def _rows_from_chips(a):
    _, rows, cols = a.shape
    return a.reshape(NCHIP, DEPTH, rows // DEPTH, cols).transpose(1, 0, 2, 3).reshape(DEPTH, -1, cols)


def _cols_by_chip(a):
    nl, rows, cols = a.shape
    return a.reshape(nl, rows, NCHIP, cols // NCHIP).transpose(2, 0, 1, 3).reshape(NCHIP, nl * rows, -1)


def _cols_from_chips(a, nl):
    _, rows, cols = a.shape
    return a.reshape(NCHIP, nl, rows // nl, cols).transpose(1, 2, 0, 3).reshape(nl, rows // nl, NCHIP * cols)


SMALL_ROWS = sum(-(-math.prod(s) // (8 * D)) * 8 for _, s in SMALL)
CONV_ROWS = DEPTH * 4 * 3 * D // D


def _pack_small(vals, extra=()):
    parts = []
    for val in tuple(vals) + tuple(extra):
        n = val.size
        parts.append(_pad_rows(val.reshape(-1), -(-n // (8 * D)) * 8))
    return jnp.concatenate(parts, axis=0)


def _unpack_small(flat):
    outs, r0 = [], 0
    for _, shape in SMALL:
        n = math.prod(shape)
        rows = -(-n // (8 * D)) * 8
        outs.append(flat[r0:r0 + rows].reshape(-1)[:n].reshape(shape))
        r0 += rows
    return outs


def _rearrange_cols(w):
    pad = jnp.zeros(w.shape[:-1] + (NCOL - BD0 - 2 * NH,), w.dtype)
    return jnp.concatenate([w[..., :4 * D], w[..., 4 * D + 2 * NH:], w[..., 4 * D:4 * D + 2 * NH], pad], axis=-1)


def _restore_cols(w):
    return jnp.concatenate([w[..., :4 * D], w[..., BD0:BD0 + 2 * NH], w[..., 4 * D:BD0]], axis=-1)


LOGITS_IN_CHIP1 = 2 * WIN_SHARD - 4 * D
LOGITS_IN_CHIP2 = 2 * NH - LOGITS_IN_CHIP1


def _wcat_from_chips(g):
    before = WIN_SHARD - LOGITS_IN_CHIP1
    pad = jnp.zeros((D, NCOL - BD0 - 2 * NH), g.dtype)
    return jnp.concatenate([g[0], g[1][:, :before], g[2][:, LOGITS_IN_CHIP2:], g[3], g[1][:, before:],
                            g[2][:, :LOGITS_IN_CHIP2], pad], axis=1)[None]


def _wcat_grad_by_chip(gw):
    before, mid = WIN_SHARD - LOGITS_IN_CHIP1, 4 * D + WIN_SHARD - LOGITS_IN_CHIP2
    return jnp.stack([gw[:, :WIN_SHARD],
                      jnp.concatenate([gw[:, WIN_SHARD:4 * D], gw[:, BD0:BD0 + LOGITS_IN_CHIP1]], axis=1),
                      jnp.concatenate([gw[:, BD0 + LOGITS_IN_CHIP1:BD0 + 2 * NH], gw[:, 4 * D:mid]], axis=1),
                      gw[:, mid:BD0]])


def _block_diag_b(bb2):
    b = bb2.reshape(-1, NCB, GPB, NS, GS)
    return jnp.einsum("lkgnc,gh->lkgchn", b, jnp.eye(GPB, dtype=F32)).reshape(-1, NCB, GPB * GS, SW)


def _block_diag_b_t(d):
    blocks = jnp.einsum("lkgchn,gh->lkgnc", d.reshape(-1, NCB, GPB, GS, GPB, NS), jnp.eye(GPB, dtype=F32))
    return blocks.reshape(-1, NG, NS * GS)


def _block_diag_c(c):
    blocks = jnp.einsum("lkgcn,gh->lkgnhc", c.reshape(-1, NCB, GPB, GS, NS), jnp.eye(GPB, dtype=F32))
    return blocks.reshape(-1, NCB, SW, GPB * GS)


def _block_diag_c_t(d):
    blocks = jnp.einsum("lkgnhc,gh->lkgcn", d.reshape(-1, NCB, GPB, NS, GPB, GS), jnp.eye(GPB, dtype=F32))
    return blocks.reshape(-1, NG, GS, NS)


def _local_step(x, target, weights, conv, small, comm=None):
    weights = list(weights) + [None] * (DEPTH - len(weights))
    ar, ai = small["ssm_a_re"], small["ssm_a_im"]
    ldt = small["ssm_log_dt"].reshape(DEPTH, NG, 1)
    br2 = small["ssm_b_re"].reshape(DEPTH, NG, NS * GS)
    bi2 = small["ssm_b_im"].reshape(DEPTH, NG, NS * GS)
    lr, li, bbr2, bbi2 = _s5_params(ar, ai, ldt, br2, bi2)

    def row(name, l, width):
        return small[name][l].reshape(1, width)

    gvecs = jnp.pad(jnp.stack([small["a_log"], small["dt_bias"]], axis=1), ((0, 0), (0, 6), (NH, DH - 2 * NH)))
    lams = jnp.stack([lr.reshape(DEPTH, 1, NG * NS), li.reshape(DEPTH, 1, NG * NS)], axis=1)
    bblks = jnp.stack([_block_diag_b(bbr2), _block_diag_b(bbi2)], axis=1)
    cblks = jnp.stack([_block_diag_c(small["ssm_c_re"]), _block_diag_c(small["ssm_c_im"])], axis=1)
    saved = []
    for l in range(DEPTH):
        gvec, lam, bblk, cblk = gvecs[l], lams[l], bblks[l], cblks[l]
        wcat, wglu, wout = weights[l]
        fetch = [None] * 3
        if comm and l + 1 < DEPTH:
            fetch = [_Exchange(comm["weight_parts"](l + 1, part), CHIP_RELS, True, False) for part in range(3)]
        proj, h, got0 = _inproj_fwd(x, row("norm_pre", l, D), wcat, 0, fetch[0])
        qkv = _prep_fwd(proj, conv[l])
        bb, gcb = _gates_fwd(proj, gvec)
        local, t_inv, got1 = _delta_local_fwd(qkv, bb, gcb, fetch[1])
        o, states = _delta_state_fwd(local, gcb)
        s5y, carries, got2 = _s5_fwd(proj, lam, bblk, cblk, fetch[2])
        if fetch[0] is not None:
            weights[l + 1] = comm["weights_from"]([got0, got1, got2])
        xn = _mix_fwd(proj, o, s5y, x, row("head_norm", l, DH), row("ssm_d", l, D), wglu, row("b_glu", l, D),
                      wout, row("norm_post", l, D), 0)
        saved.append((x, proj, h, qkv, bb, gcb, local, t_inv, o, states, s5y, carries, gvec, lam, bblk, cblk))
        x = xn

    dx, loss_part = _loss_grad(x, target)

    g = {k: [None] * DEPTH for k in ("wcat", "conv", "wglu", "wout", "norm_pre", "a_log", "dt_bias", "head_norm",
                                     "ssm_d", "b_glu", "norm_post", "dlam", "dbblk", "dcblk")}
    from_chips, send, send_layer = [None] * DEPTH, None, None
    for l in reversed(range(DEPTH)):
        xl, proj, h, qkv, bb, gcb, local, t_inv, o, states, s5y, carries, gvec, lam, bblk, cblk = saved[l]
        wcat, wglu, wout = weights[l]
        (dza, du_skip, dzb, dra, drb, do, ds5y, dxres, dwg, dwo, dvecs, dhn) = _mix_bwd(
            proj, o, s5y, xl, row("head_norm", l, DH), row("ssm_d", l, D), wglu, row("b_glu", l, D), wout,
            row("norm_post", l, D), dx, 0)
        (du, dlam, dbblk, dcblk), arrived = _s5_bwd(proj, lam, bblk, cblk, carries, ds5y, du_skip, send)
        if send is not None:
            from_chips[send_layer] = arrived
        *dlocal, dgcb_state = _delta_state_bwd(local, gcb, states, do)
        dq, dk, dv, dbb, dgcb = _delta_local_bwd(qkv, bb, gcb, t_inv, dlocal, dgcb_state)
        dbd, dgvec = _gates_bwd(proj, gvec, dbb, dgcb)
        dpre, dconv = _prep_bwd(proj, conv[l], dq, dk, dv)
        dproj = jnp.concatenate([dpre, dza, du, dzb, dra, drb, dbd], axis=1)
        g["wcat"][l] = _inproj_bwd_dw(h, dproj)
        g["conv"][l], g["wglu"][l], g["wout"][l] = dconv, dwg, dwo
        send = _Exchange(comm["grad_parts"](g["wcat"][l], dwg, dwo), CHIP_RELS, True, True) if comm else None
        dx, dgain, arrived = _inproj_bwd_dx(dproj, wcat, xl, row("norm_pre", l, D), dxres, 0, send if l == 0 else None)
        if comm and l == 0:
            from_chips[l] = arrived
        send_layer = l
        g["norm_pre"][l] = dgain[0]
        g["a_log"][l], g["dt_bias"][l] = dgvec[0, NH:2 * NH], dgvec[1, NH:2 * NH]
        g["head_norm"][l] = dhn[0]
        g["ssm_d"][l], g["b_glu"][l], g["norm_post"][l] = dvecs[0], dvecs[1], dvecs[2]
        g["dlam"][l], g["dbblk"][l], g["dcblk"][l] = dlam, dbblk, dcblk
    if comm:
        for k in ("wcat", "wglu", "wout"):
            del g[k]
    g = {k: jnp.stack(v) for k, v in g.items()}
    g["from_chips"] = from_chips
    dlam, dbblk, dcblk = g.pop("dlam"), g.pop("dbblk"), g.pop("dcblk")
    g["ssm_c_re"], g["ssm_c_im"] = _block_diag_c_t(dcblk[:, 0]), _block_diag_c_t(dcblk[:, 1])
    dar, dai, dldt, dbr2, dbi2 = _s5_params_bwd(
        ar, ai, ldt, br2, bi2, dlam[:, 0].reshape(DEPTH, NG, NS), dlam[:, 1].reshape(DEPTH, NG, NS),
        _block_diag_b_t(dbblk[:, 0]), _block_diag_b_t(dbblk[:, 1]))
    g["ssm_a_re"], g["ssm_a_im"], g["ssm_log_dt"] = dar, dai, dldt.reshape(DEPTH, NG)
    g["ssm_b_re"] = dbr2.reshape(DEPTH, NG, NS, GS)
    g["ssm_b_im"] = dbi2.reshape(DEPTH, NG, NS, GS)
    return loss_part[0, 0], dx, g


def kernel(x, norm_pre, w_in, conv_w, a_log, dt_bias, head_norm, ssm_a_re, ssm_a_im, ssm_log_dt, ssm_b_re, ssm_b_im, ssm_c_re, ssm_c_im, ssm_d, w_glu, b_glu, w_out, norm_post, loss_target, m_norm_pre, m_w_in, m_conv_w, m_a_log, m_dt_bias, m_head_norm, m_ssm_a_re, m_ssm_a_im, m_ssm_log_dt, m_ssm_b_re, m_ssm_b_im, m_ssm_c_re, m_ssm_c_im, m_ssm_d, m_w_glu, m_b_glu, m_w_out, m_norm_post, v_norm_pre, v_w_in, v_conv_w, v_a_log, v_dt_bias, v_head_norm, v_ssm_a_re, v_ssm_a_im, v_ssm_log_dt, v_ssm_b_re, v_ssm_b_im, v_ssm_c_re, v_ssm_c_im, v_ssm_d, v_w_glu, v_b_glu, v_w_out, v_norm_post):
    args = dict(locals())
    small = {n: args[n] for n, _ in SMALL}

    def flat2(a):
        return a.reshape(-1, a.shape[-1])

    w_in16, w_glu16, w_out16 = w_in.astype(BF16), w_glu.astype(BF16), w_out.astype(BF16)

    def weight_parts(l, part=None):
        if part is None:
            return [w_in16[l], w_glu16[l], w_out16[l]]
        lo, hi = WEIGHT_SPLIT[part], WEIGHT_SPLIT[part + 1]
        return [w_in16[l, lo:hi]] + [[w_glu16[l]], [w_out16[l]], []][part]

    def weights_from(parts):
        if len(parts) == 3 and isinstance(parts[0], (list, tuple)):
            parts = [jnp.concatenate([p[0] for p in parts], axis=1), parts[0][1], parts[1][1]]
        g_in, g_glu, g_out = parts[:3]
        return _wcat_from_chips(g_in), g_glu.reshape(1, D, D), g_out.reshape(1, D, D)

    def grad_parts(gwcat, gwglu, gwout):
        return [_wcat_grad_by_chip(gwcat).astype(BF16), gwglu.reshape(NCHIP, ROW_SHARD, D).astype(BF16),
                gwout.reshape(NCHIP, ROW_SHARD, D).astype(BF16)]

    first = _exchange(weight_parts(0) + [flat2(conv_w)], CHIP_RELS, True, False, "gather_weights")
    conv = _cols_from_chips(first[3], DEPTH)
    comm = dict(weight_parts=weight_parts, weights_from=weights_from, grad_parts=grad_parts)
    loss_part, dx, g = _local_step(x[0], loss_target[0], [weights_from(first)], conv, small, comm)
    loss = lax.psum(loss_part, ("x", "y", "c"))

    from_chips = [jnp.concatenate([g["from_chips"][l][a] for l in range(DEPTH)], axis=1) for a in range(3)]
    core_sums = [_sum_slots(p, "sum_chips_" + n) for p, n in zip(from_chips, ("in", "glu", "out"))]
    others = _sibling_swap(core_sums, "swap_cores")
    sharded = {}
    for n, mine, other in zip(("w_in", "w_glu", "w_out"), core_sums, others):
        sharded[n] = _adamw(args[n], [mine, other], args["m_" + n], args["v_" + n], "adamw_" + n, max_rows=128)

    pad = jnp.zeros(((-(SMALL_ROWS + CONV_ROWS)) % (8 * NDEV), D), F32)
    small_sum = _all_reduce(_pack_small([g[n] for n, _ in SMALL], extra=[g["conv"], pad]), "reduce_small")
    small_out = _adamw(_pack_small([args[n] for n, _ in SMALL]), [small_sum],
                       _pack_small([args["m_" + n] for n, _ in SMALL]),
                       _pack_small([args["v_" + n] for n, _ in SMALL]), "adamw_small")
    chip = 2 * lax.axis_index("x") + lax.axis_index("y")
    conv_sum = small_sum[SMALL_ROWS:SMALL_ROWS + CONV_ROWS].reshape(DEPTH * 4, 3 * D)
    conv_sum = lax.dynamic_slice_in_dim(conv_sum, chip * CONV_SHARD, CONV_SHARD, axis=1)
    sharded["conv_w"] = _adamw(flat2(conv_w), [conv_sum], flat2(m_conv_w), flat2(v_conv_w), "adamw_conv")

    names = ["norm_pre", "w_in", "conv_w", "a_log", "dt_bias", "head_norm", "ssm_a_re", "ssm_a_im", "ssm_log_dt",
             "ssm_b_re", "ssm_b_im", "ssm_c_re", "ssm_c_im", "ssm_d", "w_glu", "b_glu", "w_out", "norm_post"]
    outs = [loss, dx[None]]
    for i in range(4):
        sm = dict(zip([n for n, _ in SMALL], _unpack_small(small_out[i])))
        outs += [sharded[n][i].reshape(args[n].shape) if n in sharded else sm[n] for n in names]
    return tuple(outs)
```

```python
import functools
import math

import jax
import jax.numpy as jnp
from jax import lax
from jax.experimental import pallas as pl
from jax.experimental.pallas import tpu as pltpu

F32 = jnp.float32
BF16 = jnp.bfloat16
HI = lax.Precision.HIGHEST

D = 1024
NH = 8
DH = 128
CH = 128
NG = 64
GS = 16
NS = 64
GPB = 8
NCB = NG // GPB
SW = GPB * NS
NCOL = 8320
BD0 = 8192
EPS = 1e-6
DEPTH = 4
NCHIP = 4
NDEV = 8
VMEM_LIMIT = 56 * 1024 * 1024
GRAD_ACT = jnp.bfloat16

ADAM_LR = 0.001
ADAM_B1 = 0.9
ADAM_B2 = 0.999
ADAM_EPS = 1e-08
ADAM_WD = 0.01
ADAM_STEP = 10


def _cparams(sem=None):
    return pltpu.CompilerParams(dimension_semantics=sem, vmem_limit_bytes=VMEM_LIMIT)


def _full(shape):
    nd = len(shape)
    return pl.BlockSpec(shape, lambda *_: (0,) * nd)


def _rms(x, gain):
    ms = jnp.mean(x * x, axis=-1, keepdims=True)
    return x * lax.rsqrt(ms + EPS) * gain


def _sigmoid(x):
    return 1.0 / (1.0 + jnp.exp(-x))


def _silu(x):
    return x * _sigmoid(x)


def _softplus(x):
    return jnp.maximum(x, 0.0) + jnp.log(1.0 + jnp.exp(-jnp.abs(x)))


def _gelu(x):
    return 0.5 * x * (1.0 + jnp.tanh(math.sqrt(2.0 / math.pi) * (x + 0.044715 * (x * x * x))))


def _dot_bf16(a, b, dims):
    return lax.dot_general(a.astype(BF16), b.astype(BF16), (dims, ((), ())), preferred_element_type=F32)


def _mm_nt(a, b):
    return _dot_bf16(a, b, ((1,), (1,)))


def _mm_tn(a, b):
    return _dot_bf16(a, b, ((0,), (0,)))


@jax.custom_vjp
def _mm(a, b):
    return _dot_bf16(a, b, ((1,), (0,)))


def _mm_fwd(a, b):
    return _dot_bf16(a, b, ((1,), (0,))), (a, b)


def _mm_bwd(res, ct):
    a, b = res
    return _mm_nt(ct, b).astype(a.dtype), _mm_tn(a, ct).astype(b.dtype)


_mm.defvjp(_mm_fwd, _mm_bwd)


@jax.custom_vjp
def _mm_nt_d(a, b):
    return _mm_nt(a, b)


def _mm_nt_d_bwd(res, ct):
    a, b = res
    return _dot_bf16(ct, b, ((1,), (0,))), _mm_tn(ct, a)


_mm_nt_d.defvjp(lambda a, b: (_mm_nt(a, b), (a, b)), _mm_nt_d_bwd)


@jax.custom_vjp
def _mm_tn_d(a, b):
    return _mm_tn(a, b)


def _mm_tn_d_bwd(res, ct):
    a, b = res
    return _mm_nt(b, ct), _dot_bf16(a, ct, ((1,), (0,)))


_mm_tn_d.defvjp(lambda a, b: (_mm_tn(a, b), (a, b)), _mm_tn_d_bwd)


def _split_bf16(a):
    hi = a.astype(BF16)
    return hi, (a - hi.astype(F32)).astype(BF16)


def _dot3(a, b, dims):
    ah, al = _split_bf16(a)
    bh, bl = _split_bf16(b)

    def dot(x, y):
        return lax.dot_general(x, y, (dims, ((), ())), preferred_element_type=F32)

    return dot(ah, bh) + (dot(ah, bl) + dot(al, bh))


@jax.custom_vjp
def _imm(a, b):
    return _dot3(a, b, ((1,), (0,)))


def _imm_bwd(res, ct):
    a, b = res
    return _dot3(ct, b, ((1,), (1,))), _dot3(a, ct, ((0,), (0,)))


_imm.defvjp(lambda a, b: (_dot3(a, b, ((1,), (0,))), (a, b)), _imm_bwd)


def _hmm(a, b):
    return jnp.dot(a, b, precision=HI, preferred_element_type=F32)


def _hmm_nt(a, b):
    return lax.dot_general(a, b, (((1,), (1,)), ((), ())), precision=HI, preferred_element_type=F32)


def _hmm_tn(a, b):
    return lax.dot_general(a, b, (((0,), (0,)), ((), ())), precision=HI, preferred_element_type=F32)


def _rows(shape):
    return lax.broadcasted_iota(jnp.int32, shape, 0)


def _cols(shape):
    return lax.broadcasted_iota(jnp.int32, shape, 1)


def _sd(x, s):
    return jnp.where(_rows(x.shape) >= s, pltpu.roll(x, s, axis=0), 0.0)


def _su(x, s):
    n = x.shape[0]
    return jnp.where(_rows(x.shape) < n - s, pltpu.roll(x, n - s, axis=0), 0.0)


@functools.partial(jax.custom_vjp, nondiff_argnums=(1,))
def _shift_down(x, s):
    return _sd(x, s)


def _shift_down_fwd(x, s):
    return _sd(x, s), None


def _shift_down_bwd(s, _, g):
    return (_su(g, s),)


_shift_down.defvjp(_shift_down_fwd, _shift_down_bwd)


def _last_row(x):
    n = x.shape[0]
    return jnp.sum(jnp.where(_rows(x.shape) == n - 1, x, 0.0), axis=0, keepdims=True)


def _prep_fn(p, w0, w1, w2, w3, qk):
    acc = w3 * p + w2 * _shift_down(p, 1) + w1 * _shift_down(p, 2) + w0 * _shift_down(p, 3)
    a = _silu(acc)
    nrm = lax.rsqrt(jnp.sum(a * a, axis=-1, keepdims=True) + EPS)
    return a * (nrm * qk + (1.0 - qk))


def _gates_fn(bd, av, bv):
    tm = bd.shape[0]
    beta_all = _sigmoid(bd)
    g_all = -jnp.exp(av) * _softplus(bd + bv)
    r, c = _rows((tm, tm)), _cols((tm, tm))
    tri = jnp.where((r // CH == c // CH) & (r >= c), 1.0, 0.0).astype(F32)
    gc_all = _hmm(tri, g_all)
    lane = _cols(bd.shape)
    outs = []
    for h in range(NH):
        b = jnp.sum(jnp.where(lane == h, beta_all, 0.0), axis=1, keepdims=True)
        outs.append(jnp.broadcast_to(b, bd.shape))
    for h in range(NH):
        g = jnp.sum(jnp.where(lane == NH + h, gc_all, 0.0), axis=1, keepdims=True)
        outs.append(jnp.broadcast_to(g, bd.shape))
    return tuple(outs)


INV_BASE = 2


def _merge_mm(a, b):
    return _dot_bf16(a, b, ((1,), (0,)))


def _unit_lower_inv(l_mats):
    n = l_mats[0].shape[0]
    ii, jj = _rows((n, n)), _cols((n, n))
    base = ii // INV_BASE == jj // INV_BASE
    ps = [-jnp.where(base, l_mat, 0.0) for l_mat in l_mats]
    eye = jnp.where(ii == jj, 1.0, 0.0).astype(F32)
    ds = [eye + p for p in ps]
    k = 1
    while 2 * k < INV_BASE:
        ps = [_imm(p, p) for p in ps]
        ds = [d + _imm(d, p) for d, p in zip(ds, ps)]
        k *= 2
    m = INV_BASE
    while m < n:
        pair = (ii // (2 * m) == jj // (2 * m)) & (ii // m > jj // m)
        des = [_merge_mm(d, jnp.where(pair, l_mat, 0.0)) for d, l_mat in zip(ds, l_mats)]
        ds = [d - _merge_mm(de, d) for d, de in zip(ds, des)]
        m *= 2
    return ds


@jax.custom_vjp
def _known_inverse(l_mat, t_inv):
    return t_inv


def _known_inverse_bwd(t_inv, ct):
    d_l = -_dot3(_dot3(t_inv, ct, ((0,), (0,))), t_inv, ((1,), (1,)))
    return d_l, jnp.zeros_like(t_inv)


_known_inverse.defvjp(lambda l_mat, t_inv: (t_inv, t_inv), _known_inverse_bwd)


def _chunk_system(q, k, v, bb, gcb):
    qs = q * (DH ** -0.5)
    kb = k * bb
    eg = jnp.exp(gcb)
    ii, jj = _rows((CH, CH)), _cols((CH, CH))
    decay = jnp.exp(jnp.where(ii >= jj, gcb - gcb.T, -1e30))
    l_mat = jnp.where(ii > jj, _mm_nt_d(kb, k) * decay, 0.0)
    a_qk = _mm_nt_d(qs, k) * decay
    k_dec = k * jnp.exp(_last_row(gcb) - gcb)
    return l_mat, (v * bb, kb * eg, qs * eg, k_dec, a_qk)


def _chunk_solve(t_inv, rest):
    vb, kbe, q_dec, k_dec, a_qk = rest
    return _mm(t_inv, vb), _mm(t_inv, kbe), q_dec, k_dec, a_qk


def _side_by_side_vjp(fn, items, cts):
    n = len(items[0])
    _, vjp = jax.vjp(lambda *flat: fn([flat[i * n:(i + 1) * n] for i in range(len(items))]),
                     *[a for item in items for a in item])
    grads = vjp(cts)
    return [grads[i * n:(i + 1) * n] for i in range(len(items))]


def _chunks_local_known(items):
    systems = [_chunk_system(*item[:5]) for item in items]
    t_invs = [_known_inverse(l_mat, item[5]) for (l_mat, _), item in zip(systems, items)]
    return [_chunk_solve(t_inv, rest) for t_inv, (_, rest) in zip(t_invs, systems)]


def _chunks_local(chunks):
    systems = [_chunk_system(*c) for c in chunks]
    t_invs = _unit_lower_inv([l_mat for l_mat, _ in systems])
    return [(_chunk_solve(t_inv, rest), t_inv) for t_inv, (_, rest) in zip(t_invs, systems)]


def _state_steps(items):
    v_news = [u - _mm(w, state) for u, w, _, _, _, _, state in items]
    outs = [_mm(q_dec, state) + _mm(a_qk, v_new) for (_, _, q_dec, _, a_qk, _, state), v_new in zip(items, v_news)]
    states = [state * jnp.exp(_last_row(gcb)) + _mm_tn_d(k_dec, v_new)
              for (_, _, _, k_dec, _, gcb, state), v_new in zip(items, v_news)]
    return list(zip(outs, states))


SUB = 8


SCAN_ROWS = 32


def _cmul(ar, ai, br, bi):
    return ar * br - ai * bi, ar * bi + ai * br


def _scan_tile(xr, xi, mr, mi, hr_ref, hi_ref, cr_ref, ci_ref, reverse, fold):
    n, width = xr.shape
    ngroups, nlb = n // SUB, width // DH
    shift_groups = _su if reverse else _sd

    def lanes(x, j):
        return x[..., j * DH:(j + 1) * DH]

    start = n - 1 if reverse else 0
    for j in range(nlb):
        hr_ref[j] = lanes(xr, j)
        hi_ref[j] = lanes(xi, j)
        hr_ref[j, start:start + 1, :] += lanes(fold[0], j)
        hi_ref[j, start:start + 1, :] += lanes(fold[1], j)
    pr, pi = mr, mi
    tr, ti = jnp.broadcast_to(mr, (SUB, width)), jnp.broadcast_to(mi, (SUB, width))
    pos = _rows(tr.shape)
    steps = []
    s = 1
    while s < SUB:
        inside = pos < SUB - s if reverse else pos >= s
        shift = SUB - s if reverse else s
        steps.append((shift, jnp.where(inside, pr, 0.0), jnp.where(inside, pi, 0.0)))
        er = jnp.where(inside, pltpu.roll(tr, shift, axis=0), 1.0)
        ei = jnp.where(inside, pltpu.roll(ti, shift, axis=0), 0.0)
        tr, ti = _cmul(tr, ti, er, ei)
        pr, pi = _cmul(pr, pi, pr, pi)
        s *= 2
    for b in range(0, n, SCAN_ROWS):
        rows = slice(b, b + SCAN_ROWS)
        for j in range(nlb):
            br = hr_ref[j, rows, :].reshape(SCAN_ROWS // SUB, SUB, DH)
            bi = hi_ref[j, rows, :].reshape(SCAN_ROWS // SUB, SUB, DH)
            for shift, qr, qi in steps:
                dr, di = _cmul(lanes(qr, j)[None], lanes(qi, j)[None],
                               pltpu.roll(br, shift, axis=1), pltpu.roll(bi, shift, axis=1))
                br, bi = br + dr, bi + di
            hr_ref[j, rows, :] = br.reshape(SCAN_ROWS, DH)
            hi_ref[j, rows, :] = bi.reshape(SCAN_ROWS, DH)
    edge = pl.ds(0 if reverse else SUB - 1, ngroups, stride=SUB)
    gr = jnp.concatenate([hr_ref.at[j][edge, :] for j in range(nlb)], axis=1)
    gi = jnp.concatenate([hi_ref.at[j][edge, :] for j in range(nlb)], axis=1)
    s = 1
    while s < ngroups:
        dr, di = _cmul(pr, pi, shift_groups(gr, s), shift_groups(gi, s))
        gr, gi = gr + dr, gi + di
        pr, pi = _cmul(pr, pi, pr, pi)
        s *= 2
    cr_ref[...] = shift_groups(gr, 1)
    ci_ref[...] = shift_groups(gi, 1)
    for g in range(ngroups):
        rows = slice(g * SUB, (g + 1) * SUB)
        dr, di = _cmul(tr, ti, cr_ref[g:g + 1, :], ci_ref[g:g + 1, :])
        for j in range(nlb):
            hr_ref[j, rows, :] += lanes(dr, j)
            hi_ref[j, rows, :] += lanes(di, j)
    return (jnp.concatenate([hr_ref[j] for j in range(nlb)], axis=1),
            jnp.concatenate([hi_ref[j] for j in range(nlb)], axis=1))


def _s5_states(u, lam_ref, b_ref, car_ref, hr_ref, hi_ref, cr_ref, ci_ref):
    lr, li = lam_ref[0], lam_ref[1]
    fold = _cmul(lr, li, car_ref[0:1, :], car_ref[1:2, :])
    return _scan_tile(_mm(u, b_ref[0]), _mm(u, b_ref[1]), lr, li, hr_ref, hi_ref, cr_ref, ci_ref, False, fold)


def _scratch_row(ref, row):
    return jnp.concatenate([ref[j, row:row + 1, :] for j in range(ref.shape[0])], axis=1)


def _s5_params_fn(ar, ai, ldt, br2, bi2):
    dt = jnp.exp(ldt)
    mag = jnp.exp(ar * dt)
    lr, li = mag * jnp.cos(ai * dt), mag * jnp.sin(ai * dt)
    den = ar * ar + ai * ai
    fr = ((lr - 1.0) * ar + li * ai) / den
    fi = (li * ar - (lr - 1.0) * ai) / den
    expand = jnp.where(_cols((NS, NS * GS)) // GS == _rows((NS, NS * GS)), 1.0, 0.0).astype(F32)
    fr2, fi2 = _hmm(fr, expand), _hmm(fi, expand)
    return lr, li, fr2 * br2 - fi2 * bi2, fr2 * bi2 + fi2 * br2


def _head_norm(o, hn):
    parts = []
    for h in range(NH):
        oh = o[:, h * DH:(h + 1) * DH]
        parts.append(oh * lax.rsqrt(jnp.mean(oh * oh, axis=-1, keepdims=True) + EPS) * hn)
    return jnp.concatenate(parts, axis=1)


def _mix_pre(s5y, u, dvec):
    return _gelu(s5y + dvec * u)


def _mix_mid(o, za, y0, gl, zb, ra, rb, hn):
    ya = _head_norm(o, hn) * _silu(za)
    yb = y0 * _sigmoid(gl) * _silu(zb)
    return _sigmoid(ra) * ya + _sigmoid(rb) * yb


def _mix_post(x, out, npost):
    return x + _rms(out, npost)


def _tile(t, want):
    return min(t, want)


def _row_tile(rows, want):
    return max(r for r in range(16, want + 1, 16) if rows % r == 0)


def _call_carrying(body, name, grid, in_specs, out_specs, out_shape, scratch, args, semantics, exchange):
    n_out = len(out_shape)
    if exchange is not None:
        body = _carry(body, len(args), n_out, len(scratch), exchange, grid)
        in_specs, out_specs = in_specs + exchange.in_specs, out_specs + exchange.out_specs
        out_shape, scratch, args = out_shape + exchange.out_shape, scratch + exchange.scratch_shapes, args + exchange.srcs
        semantics = ("arbitrary",) * len(grid)
    outs = pl.pallas_call(body, name=name, grid=grid, in_specs=in_specs, out_specs=out_specs, out_shape=out_shape,
                          scratch_shapes=scratch, compiler_params=_cparams(semantics))(*args)
    return outs[:n_out], outs[n_out:]


def _inproj_fwd(x, gain, wcat, l, exchange=None):
    t = x.shape[0]
    tm, tn = _tile(t, 1024), 1664

    def body(x_ref, g_ref, w_ref, o_ref, h_ref):
        @pl.when(pl.program_id(1) == 0)
        def _():
            h_ref[...] = _rms(x_ref[...], g_ref[...]).astype(h_ref.dtype)
        o_ref[...] = _dot_bf16(h_ref[...], w_ref[...], ((1,), (0,)))

    (proj, h), fetched = _call_carrying(
        body, "inproj_fwd", (t // tm, NCOL // tn),
        [pl.BlockSpec((tm, D), lambda i, j: (i, 0)), _full((1, D)), pl.BlockSpec((None, D, tn), lambda i, j: (l, 0, j))],
        [pl.BlockSpec((tm, tn), lambda i, j: (i, j)), pl.BlockSpec((tm, D), lambda i, j: (i, 0))],
        [jax.ShapeDtypeStruct((t, NCOL), F32), jax.ShapeDtypeStruct((t, D), wcat.dtype)],
        [], [x, gain, wcat], ("parallel", "arbitrary"), exchange)
    return proj, h, fetched


def _inproj_bwd_dx(dproj, wcat, x, gain, dxres, l, exchange=None):
    t = x.shape[0]
    tm, tk = _tile(t, 1024), 640
    nk = NCOL // tk

    def body(dp_ref, w_ref, x_ref, g_ref, r_ref, dx_ref, dg_ref, acc_ref):
        i, k = pl.program_id(0), pl.program_id(1)

        @pl.when(k == 0)
        def _():
            acc_ref[...] = jnp.zeros_like(acc_ref)

        acc_ref[...] += _mm_nt(dp_ref[...], w_ref[...])

        @pl.when(k == nk - 1)
        def _():
            _, vjp = jax.vjp(_rms, x_ref[...], g_ref[...])
            dx, dg = vjp(acc_ref[...])
            dx_ref[...] = r_ref[...] + dx

            @pl.when(i == 0)
            def _():
                dg_ref[...] = dg

            @pl.when(i > 0)
            def _():
                dg_ref[...] += dg

    grid = (t // tm, nk)
    in_specs = [pl.BlockSpec((tm, tk), lambda i, k: (i, k)), pl.BlockSpec((None, D, tk), lambda i, k: (l, 0, k)),
                pl.BlockSpec((tm, D), lambda i, k: (i, 0)), _full((1, D)), pl.BlockSpec((tm, D), lambda i, k: (i, 0))]
    out_specs = [pl.BlockSpec((tm, D), lambda i, k: (i, 0)), _full((1, D))]
    out_shape = [jax.ShapeDtypeStruct((t, D), F32), jax.ShapeDtypeStruct((1, D), F32)]
    scratch, args = [pltpu.VMEM((tm, D), F32)], [dproj, wcat, x, gain, dxres]
    if exchange is not None:
        body = _carry(body, len(args), len(out_shape), len(scratch), exchange, grid)
        in_specs, out_specs = in_specs + exchange.in_specs, out_specs + exchange.out_specs
        out_shape, scratch, args = out_shape + exchange.out_shape, scratch + exchange.scratch_shapes, args + exchange.srcs
    outs = pl.pallas_call(
        body, name="inproj_bwd_dx", grid=grid, in_specs=in_specs, out_specs=out_specs, out_shape=out_shape,
        scratch_shapes=scratch, compiler_params=_cparams(("arbitrary", "arbitrary")),
    )(*args)
    return outs[0], outs[1], outs[2:]


def _inproj_bwd_dw(h, dproj):
    t = h.shape[0]
    tm, tn = _tile(t, 512), 1664

    def body(h_ref, dp_ref, o_ref):
        @pl.when(pl.program_id(1) == 0)
        def _():
            o_ref[...] = jnp.zeros_like(o_ref)

        o_ref[...] += _mm_tn(h_ref[...], dp_ref[...])

    return pl.pallas_call(
        body, name="inproj_bwd_dw", grid=(NCOL // tn, t // tm),
        in_specs=[pl.BlockSpec((tm, D), lambda j, i: (i, 0)), pl.BlockSpec((tm, tn), lambda j, i: (i, j))],
        out_specs=pl.BlockSpec((D, tn), lambda j, i: (0, j)),
        out_shape=jax.ShapeDtypeStruct((D, NCOL), F32),
        compiler_params=_cparams(("parallel", "arbitrary")),
    )(h, dproj)


def _prep_fwd(proj, cw):
    t = proj.shape[0]

    def body(p_ref, w_ref, o_ref):
        qk = (pl.program_id(0) < 2 * NH).astype(F32)
        o_ref[...] = _prep_fn(p_ref[...], w_ref[0:1, :], w_ref[1:2, :], w_ref[2:3, :], w_ref[3:4, :], qk)

    return pl.pallas_call(
        body, name="prep_fwd", grid=(3 * NH,),
        in_specs=[pl.BlockSpec((t, DH), lambda c: (0, c)), pl.BlockSpec((4, DH), lambda c: (0, c))],
        out_specs=pl.BlockSpec((None, t, DH), lambda c: (c, 0, 0)),
        out_shape=jax.ShapeDtypeStruct((3 * NH, t, DH), F32),
        compiler_params=_cparams(("parallel",)),
    )(proj, cw)


def _prep_bwd(proj, cw, dq, dk, dv):
    t = proj.shape[0]

    def body(p_ref, w_ref, dq_ref, dk_ref, dv_ref, dp_ref, dw_ref):
        c = pl.program_id(0)
        qk = (c < 2 * NH).astype(F32)
        _, vjp = jax.vjp(lambda p, w0, w1, w2, w3: _prep_fn(p, w0, w1, w2, w3, qk),
                         p_ref[...], w_ref[0:1, :], w_ref[1:2, :], w_ref[2:3, :], w_ref[3:4, :])
        d = jnp.where(c < NH, dq_ref[...], jnp.where(c < 2 * NH, dk_ref[...], dv_ref[...]))
        dp, dw0, dw1, dw2, dw3 = vjp(d)
        dp_ref[...] = dp.astype(dp_ref.dtype)
        dw_ref[0:1, :] = dw0
        dw_ref[1:2, :] = dw1
        dw_ref[2:3, :] = dw2
        dw_ref[3:4, :] = dw3

    return pl.pallas_call(
        body, name="prep_bwd", grid=(3 * NH,),
        in_specs=[pl.BlockSpec((t, DH), lambda c: (0, c)), pl.BlockSpec((4, DH), lambda c: (0, c))]
        + [pl.BlockSpec((None, t, DH), functools.partial(lambda c, off: (jnp.clip(c - off, 0, NH - 1), 0, 0), off=off))
           for off in (0, NH, 2 * NH)],
        out_specs=[pl.BlockSpec((t, DH), lambda c: (0, c)), pl.BlockSpec((4, DH), lambda c: (0, c))],
        out_shape=[jax.ShapeDtypeStruct((t, 3 * D), GRAD_ACT), jax.ShapeDtypeStruct((4, 3 * D), F32)],
        compiler_params=_cparams(("arbitrary",)),
    )(proj, cw, dq, dk, dv)


def _gates_fwd(proj, gvec):
    t = proj.shape[0]
    tm = _tile(t, 512)

    def body(p_ref, gv_ref, b_ref, g_ref):
        outs = _gates_fn(p_ref[...], gv_ref[0:1, :], gv_ref[1:2, :])
        for h in range(NH):
            b_ref[h] = outs[h]
            g_ref[h] = outs[NH + h]

    spec = pl.BlockSpec((NH, tm, DH), lambda i: (0, i, 0))
    return pl.pallas_call(
        body, name="gates_fwd", grid=(t // tm,),
        in_specs=[pl.BlockSpec((tm, DH), lambda i: (i, BD0 // DH)), _full((8, DH))],
        out_specs=[spec, spec],
        out_shape=[jax.ShapeDtypeStruct((NH, t, DH), F32)] * 2,
        compiler_params=_cparams(("parallel",)),
    )(proj, gvec)


def _gates_bwd(proj, gvec, dbb, dgcb):
    t = proj.shape[0]
    tm = _tile(t, 512)

    def body(p_ref, gv_ref, db_ref, dg_ref, dp_ref, dgv_ref):
        _, vjp = jax.vjp(_gates_fn, p_ref[...], gv_ref[0:1, :], gv_ref[1:2, :])
        cts = tuple(db_ref[h] for h in range(NH)) + tuple(dg_ref[h] for h in range(NH))
        dp, da, db = vjp(cts)
        dp_ref[...] = dp.astype(dp_ref.dtype)

        @pl.when(pl.program_id(0) == 0)
        def _():
            dgv_ref[...] = jnp.zeros_like(dgv_ref)

        dgv_ref[0:1, :] += da
        dgv_ref[1:2, :] += db

    spec = pl.BlockSpec((NH, tm, DH), lambda i: (0, i, 0))
    return pl.pallas_call(
        body, name="gates_bwd", grid=(t // tm,),
        in_specs=[pl.BlockSpec((tm, DH), lambda i: (i, BD0 // DH)), _full((8, DH)), spec, spec],
        out_specs=[pl.BlockSpec((tm, DH), lambda i: (i, 0)), _full((8, DH))],
        out_shape=[jax.ShapeDtypeStruct((t, DH), GRAD_ACT), jax.ShapeDtypeStruct((8, DH), F32)],
        compiler_params=_cparams(("arbitrary",)),
    )(proj, gvec, dbb, dgcb)


def _chunks_per_step(nch):
    return max(c for c in (8, 4, 2, 1) if nch % c == 0)


def _grid_ends(grid):
    def first():
        return functools.reduce(jnp.logical_and, [pl.program_id(a) == 0 for a in range(len(grid))])

    def last():
        return functools.reduce(jnp.logical_and, [pl.program_id(a) == n - 1 for a, n in enumerate(grid)])

    return first, last


def _carry(body, n_in, n_out, n_scratch, exchange, grid):
    first, last = _grid_ends(grid)
    na = exchange.narr

    def wrapped(*refs):
        a, b = n_in, n_in + na
        c, d = b + n_out, b + n_out + na
        e = d + n_scratch
        srcs, dsts, sems = refs[a:b], refs[c:d], refs[e:]

        @pl.when(first())
        def _():
            exchange.start(srcs, dsts, sems)

        body(*(refs[:a] + refs[b:c] + refs[d:e]))

        @pl.when(last())
        def _():
            exchange.wait(srcs, dsts, sems)

    return wrapped


def _delta_local_fwd(qkv, bb, gcb, exchange=None):
    t = qkv.shape[1]
    cps = _chunks_per_step(t // CH)
    rows = cps * CH
    grid = (NH, t // rows)

    def body(q_ref, k_ref, v_ref, b_ref, g_ref, *out_refs):
        slices = [slice(c * CH, (c + 1) * CH) for c in range(cps)]
        results = _chunks_local([tuple(ref[sl, :] for ref in (q_ref, k_ref, v_ref, b_ref, g_ref)) for sl in slices])
        for sl, (outs, t_inv) in zip(slices, results):
            for ref, val in zip(out_refs, outs + (t_inv,)):
                ref[sl, :] = val.astype(ref.dtype)

    def blk(off):
        return pl.BlockSpec((None, rows, DH), lambda h, n: (h + off, n, 0))

    in_specs = [blk(0), blk(NH), blk(2 * NH), blk(0), blk(0)]
    out_specs = [blk(0)] * 6
    out_shape = [jax.ShapeDtypeStruct((NH, t, DH), dt) for dt in (F32, BF16, BF16, BF16, BF16, F32)]
    args, scratch, sem = [qkv, qkv, qkv, bb, gcb], [], ("parallel", "parallel")
    if exchange is not None:
        body = _carry(body, 5, 6, 0, exchange, grid)
        in_specs, out_specs = in_specs + exchange.in_specs, out_specs + exchange.out_specs
        out_shape, scratch, args = out_shape + exchange.out_shape, exchange.scratch_shapes, args + exchange.srcs
        sem = ("arbitrary", "arbitrary")
    outs = pl.pallas_call(
        body, name="delta_local_fwd", grid=grid, in_specs=in_specs, out_specs=out_specs, out_shape=out_shape,
        scratch_shapes=scratch, compiler_params=_cparams(sem),
    )(*args)
    return outs[:5], outs[5], outs[6:]


def _delta_local_bwd(qkv, bb, gcb, t_inv, cts, dgcb_state):
    t = qkv.shape[1]
    cps = _chunks_per_step(t // CH)
    rows = cps * CH

    def body(q_ref, k_ref, v_ref, b_ref, g_ref, ti_ref, du_ref, dw_ref, dqd_ref, dkd_ref, da_ref, dgs_ref,
             dq_ref, dk_ref, dv_ref, db_ref, dg_ref):
        slices = [slice(c * CH, (c + 1) * CH) for c in range(cps)]
        items = [tuple(ref[sl, :] for ref in (q_ref, k_ref, v_ref, b_ref, g_ref, ti_ref)) for sl in slices]
        cts = [tuple(ref[sl, :] for ref in (du_ref, dw_ref, dqd_ref, dkd_ref, da_ref)) for sl in slices]
        for sl, (dq, dk, dv, db, dg, _) in zip(slices, _side_by_side_vjp(_chunks_local_known, items, cts)):
            dq_ref[sl, :] = dq
            dk_ref[sl, :] = dk
            dv_ref[sl, :] = dv
            db_ref[sl, :] = db
            dg_ref[sl, :] = dg + dgs_ref[sl, :]

    def blk(off):
        return pl.BlockSpec((None, rows, DH), lambda h, n: (h + off, n, 0))

    return pl.pallas_call(
        body, name="delta_local_bwd", grid=(NH, t // rows),
        in_specs=[blk(0), blk(NH), blk(2 * NH)] + [blk(0)] * 9,
        out_specs=[blk(0)] * 5,
        out_shape=[jax.ShapeDtypeStruct((NH, t, DH), F32)] * 5,
        compiler_params=_cparams(("parallel", "parallel")),
    )(qkv, qkv, qkv, bb, gcb, t_inv, *cts, dgcb_state)


def _delta_state_fwd(local, gcb):
    t = gcb.shape[1]
    nch = t // CH

    def body(u_ref, w_ref, qd_ref, kd_ref, a_ref, g_ref, o_ref, s_ref, st_ref):
        @pl.when(pl.program_id(0) == 0)
        def _():
            st_ref[...] = jnp.zeros_like(st_ref)

        s_ref[...] = st_ref[...]
        items = [tuple(ref[h].astype(F32) for ref in (u_ref, w_ref, qd_ref, kd_ref, a_ref, g_ref, st_ref))
                 for h in range(NH)]
        for h, (o, ns) in enumerate(_state_steps(items)):
            o_ref[:, h * DH:(h + 1) * DH] = o
            st_ref[h] = ns

    blk = pl.BlockSpec((NH, CH, DH), lambda n: (0, n, 0))
    return pl.pallas_call(
        body, name="delta_state_fwd", grid=(nch,),
        in_specs=[blk] * 6,
        out_specs=[pl.BlockSpec((CH, D), lambda n: (n, 0)),
                   pl.BlockSpec((NH, None, DH, DH), lambda n: (0, n, 0, 0))],
        out_shape=[jax.ShapeDtypeStruct((t, D), F32), jax.ShapeDtypeStruct((NH, nch, DH, DH), F32)],
        scratch_shapes=[pltpu.VMEM((NH, DH, DH), F32)],
        compiler_params=_cparams(("arbitrary",)),
    )(*local, gcb)


def _delta_state_bwd(local, gcb, states, do):
    t = gcb.shape[1]
    nch = t // CH

    def body(u_ref, w_ref, qd_ref, kd_ref, a_ref, g_ref, s_ref, do_ref,
             du_ref, dw_ref, dqd_ref, dkd_ref, da_ref, dg_ref, ds_ref):
        @pl.when(pl.program_id(0) == 0)
        def _():
            ds_ref[...] = jnp.zeros_like(ds_ref)

        items = [tuple(ref[h].astype(F32) for ref in (u_ref, w_ref, qd_ref, kd_ref, a_ref, g_ref, s_ref))
                 for h in range(NH)]
        cts = [(do_ref[:, h * DH:(h + 1) * DH], ds_ref[h]) for h in range(NH)]
        for h, (du, dw, dqd, dkd, da, dg, ds) in enumerate(_side_by_side_vjp(_state_steps, items, cts)):
            du_ref[h] = du
            dw_ref[h] = dw
            dqd_ref[h] = dqd
            dkd_ref[h] = dkd
            da_ref[h] = da
            dg_ref[h] = dg
            ds_ref[h] = ds

    blk = pl.BlockSpec((NH, CH, DH), lambda n: (0, nch - 1 - n, 0))
    return pl.pallas_call(
        body, name="delta_state_bwd", grid=(nch,),
        in_specs=[blk] * 6 + [pl.BlockSpec((NH, None, DH, DH), lambda n: (0, nch - 1 - n, 0, 0)),
                              pl.BlockSpec((CH, D), lambda n: (nch - 1 - n, 0))],
        out_specs=[blk] * 6,
        out_shape=[jax.ShapeDtypeStruct((NH, t, DH), F32)] * 6,
        scratch_shapes=[pltpu.VMEM((NH, DH, DH), F32)],
        compiler_params=_cparams(("arbitrary",)),
    )(*local, gcb, states, do)


def _s5_params(ar, ai, ldt, br2, bi2):
    def body(ar_ref, ai_ref, ld_ref, br_ref, bi_ref, lr_ref, li_ref, bbr_ref, bbi_ref):
        lr, li, bbr, bbi = _s5_params_fn(ar_ref[...], ai_ref[...], ld_ref[...], br_ref[...], bi_ref[...])
        lr_ref[...] = lr
        li_ref[...] = li
        bbr_ref[...] = bbr
        bbi_ref[...] = bbi

    sq = pl.BlockSpec((None, NG, NS), lambda l: (l, 0, 0))
    wide = pl.BlockSpec((None, NG, NS * GS), lambda l: (l, 0, 0))
    return pl.pallas_call(
        body, name="s5_params", grid=(DEPTH,),
        in_specs=[sq, sq, pl.BlockSpec((None, NG, 1), lambda l: (l, 0, 0)), wide, wide],
        out_specs=[sq, sq, wide, wide],
        out_shape=[jax.ShapeDtypeStruct((DEPTH, NG, NS), F32)] * 2
        + [jax.ShapeDtypeStruct((DEPTH, NG, NS * GS), F32)] * 2,
        compiler_params=_cparams(("parallel",)),
    )(ar, ai, ldt, br2, bi2)


def _s5_params_bwd(ar, ai, ldt, br2, bi2, dlr, dli, dbbr, dbbi):
    def body(ar_ref, ai_ref, ld_ref, br_ref, bi_ref, a_ref, b_ref, c_ref, d_ref,
             dar_ref, dai_ref, dld_ref, dbr_ref, dbi_ref):
        _, vjp = jax.vjp(_s5_params_fn, ar_ref[...], ai_ref[...], ld_ref[...], br_ref[...], bi_ref[...])
        dar, dai, dld, dbr, dbi = vjp((a_ref[...], b_ref[...], c_ref[...], d_ref[...]))
        dar_ref[...] = dar
        dai_ref[...] = dai
        dld_ref[...] = dld
        dbr_ref[...] = dbr
        dbi_ref[...] = dbi

    sq = pl.BlockSpec((None, NG, NS), lambda l: (l, 0, 0))
    col = pl.BlockSpec((None, NG, 1), lambda l: (l, 0, 0))
    wide = pl.BlockSpec((None, NG, NS * GS), lambda l: (l, 0, 0))
    return pl.pallas_call(
        body, name="s5_params_bwd", grid=(DEPTH,),
        in_specs=[sq, sq, col, wide, wide, sq, sq, wide, wide],
        out_specs=[sq, sq, col, wide, wide],
        out_shape=[jax.ShapeDtypeStruct((DEPTH, NG, NS), F32)] * 2 + [jax.ShapeDtypeStruct((DEPTH, NG, 1), F32)]
        + [jax.ShapeDtypeStruct((DEPTH, NG, NS * GS), F32)] * 2,
        compiler_params=_cparams(("parallel",)),
    )(ar, ai, ldt, br2, bi2, dlr, dli, dbbr, dbbi)


def _s5_tile_rows(t):
    return _tile(t // 2, 1024)


def _s5_fwd(proj, lam, bblk, cblk, exchange=None):
    t = proj.shape[0]
    r = _s5_tile_rows(t)
    nt = t // r
    u0 = 4 * D // DH

    def body(u_ref, lam_ref, b_ref, c_ref, y_ref, car_ref, st_ref, hr_ref, hi_ref, cr_ref, ci_ref):
        @pl.when(pl.program_id(1) == 0)
        def _():
            st_ref[...] = jnp.zeros_like(st_ref)

        car_ref[...] = st_ref[...]
        hr, hi = _s5_states(u_ref[...], lam_ref, b_ref, st_ref, hr_ref, hi_ref, cr_ref, ci_ref)
        y_ref[...] = _mm(hr, c_ref[0]) - _mm(hi, c_ref[1])
        st_ref[0:1, :] = _scratch_row(hr_ref, r - 1)
        st_ref[1:2, :] = _scratch_row(hi_ref, r - 1)

    scratch = ([pltpu.VMEM((8, SW), F32)] + [pltpu.VMEM((SW // DH, r, DH), F32)] * 2
               + [pltpu.VMEM((r // SUB, SW), F32)] * 2)
    (y, carries), fetched = _call_carrying(
        body, "s5_fwd", (NCB, nt),
        [pl.BlockSpec((r, DH), lambda c, i: (i, u0 + c)), pl.BlockSpec((2, 1, SW), lambda c, i: (0, 0, c)),
         pl.BlockSpec((2, None, DH, SW), lambda c, i: (0, c, 0, 0)),
         pl.BlockSpec((2, None, SW, DH), lambda c, i: (0, c, 0, 0))],
        [pl.BlockSpec((r, DH), lambda c, i: (i, c)), pl.BlockSpec((None, 8, SW), lambda c, i: (i, 0, c))],
        [jax.ShapeDtypeStruct((t, D), F32), jax.ShapeDtypeStruct((nt, 8, NG * NS), F32)],
        scratch, [proj, lam, bblk, cblk], ("parallel", "arbitrary"), exchange)
    return y, carries, fetched


def _s5_bwd(proj, lam, bblk, cblk, carries, dy, du_skip, exchange=None):
    t = proj.shape[0]
    r = _s5_tile_rows(t)
    nt = t // r
    u0 = 4 * D // DH

    def body(u_ref, lam_ref, b_ref, c_ref, car_ref, dy_ref, dus_ref, du_ref, dlam_ref, db_ref, dc_ref, dst_ref,
             hr_ref, hi_ref, ar_ref, ai_ref, cr_ref, ci_ref):
        first = pl.program_id(1) == 0

        @pl.when(first)
        def _():
            dst_ref[...] = jnp.zeros_like(dst_ref)

        u, dy = u_ref[...], dy_ref[...]
        lr, li = lam_ref[0], lam_ref[1]
        hr, hi = _s5_states(u, lam_ref, b_ref, car_ref, hr_ref, hi_ref, cr_ref, ci_ref)
        dcr2, dci2 = _mm_tn(hr, dy), -_mm_tn(hi, dy)
        fold = _cmul(lr, -li, dst_ref[0:1, :], dst_ref[1:2, :])
        ar, ai = _scan_tile(_mm_nt(dy, c_ref[0]), -_mm_nt(dy, c_ref[1]), lr, -li, ar_ref, ai_ref, cr_ref, ci_ref,
                            True, fold)
        top = _rows((r, SW)) == 0
        dst_ref[0:1, :] = _scratch_row(ar_ref, 0)
        dst_ref[1:2, :] = _scratch_row(ai_ref, 0)
        du_ref[...] = (_mm_nt(ar, b_ref[0]) + _mm_nt(ai, b_ref[1]) + dus_ref[...]).astype(du_ref.dtype)
        dbr, dbi = _mm_tn(u, ar), _mm_tn(u, ai)
        pr = _sd(hr, 1) + jnp.where(top, car_ref[0:1, :], 0.0)
        pi = _sd(hi, 1) + jnp.where(top, car_ref[1:2, :], 0.0)
        dlr = jnp.sum(ar * pr + ai * pi, axis=0, keepdims=True)
        dli = jnp.sum(ai * pr - ar * pi, axis=0, keepdims=True)

        @pl.when(first)
        def _():
            dlam_ref[0] = dlr
            dlam_ref[1] = dli
            db_ref[0] = dbr
            db_ref[1] = dbi
            dc_ref[0] = dcr2
            dc_ref[1] = dci2

        @pl.when(jnp.logical_not(first))
        def _():
            dlam_ref[0] += dlr
            dlam_ref[1] += dli
            db_ref[0] += dbr
            db_ref[1] += dbi
            dc_ref[0] += dcr2
            dc_ref[1] += dci2

    grid = (NCB, nt)
    in_specs = [pl.BlockSpec((r, DH), lambda c, i: (nt - 1 - i, u0 + c)),
                pl.BlockSpec((2, 1, SW), lambda c, i: (0, 0, c)),
                pl.BlockSpec((2, None, DH, SW), lambda c, i: (0, c, 0, 0)),
                pl.BlockSpec((2, None, SW, DH), lambda c, i: (0, c, 0, 0)),
                pl.BlockSpec((None, 8, SW), lambda c, i: (nt - 1 - i, 0, c)),
                pl.BlockSpec((r, DH), lambda c, i: (nt - 1 - i, c)),
                pl.BlockSpec((r, DH), lambda c, i: (nt - 1 - i, c))]
    out_specs = [pl.BlockSpec((r, DH), lambda c, i: (nt - 1 - i, c)),
                 pl.BlockSpec((2, 1, SW), lambda c, i: (0, 0, c)),
                 pl.BlockSpec((2, None, DH, SW), lambda c, i: (0, c, 0, 0)),
                 pl.BlockSpec((2, None, SW, DH), lambda c, i: (0, c, 0, 0))]
    out_shape = [jax.ShapeDtypeStruct((t, D), GRAD_ACT), jax.ShapeDtypeStruct((2, 1, NG * NS), F32),
                 jax.ShapeDtypeStruct((2, NCB, DH, SW), F32), jax.ShapeDtypeStruct((2, NCB, SW, DH), F32)]
    scratch = ([pltpu.VMEM((8, SW), F32)] + [pltpu.VMEM((SW // DH, r, DH), F32)] * 4
               + [pltpu.VMEM((r // SUB, SW), F32)] * 2)
    args, sem = [proj, lam, bblk, cblk, carries, dy, du_skip], ("parallel", "arbitrary")
    if exchange is not None:
        body = _carry(body, len(args), len(out_shape), len(scratch), exchange, grid)
        in_specs, out_specs = in_specs + exchange.in_specs, out_specs + exchange.out_specs
        out_shape, scratch, args = out_shape + exchange.out_shape, scratch + exchange.scratch_shapes, args + exchange.srcs
        sem = ("arbitrary", "arbitrary")
    outs = pl.pallas_call(
        body, name="s5_bwd", grid=grid, in_specs=in_specs, out_specs=out_specs, out_shape=out_shape,
        scratch_shapes=scratch, compiler_params=_cparams(sem),
    )(*args)
    return outs[:4], outs[4:]


def _proj_spec(tm, col):
    return pl.BlockSpec((tm, D), lambda i: (i, col))


def _layer_mat(l):
    return pl.BlockSpec((None, D, D), lambda i: (l, 0, 0))


def _mix_fwd(proj, o, s5y, x, hn, dvec, wglu, bglu, wout, npost, l):
    t = x.shape[0]
    tm = _tile(t, 256)

    def body(za_ref, u_ref, zb_ref, ra_ref, rb_ref, o_ref, y_ref, x_ref, hn_ref, d_ref, wg_ref, bg_ref, wo_ref,
             np_ref, xn_ref):
        y0 = _mix_pre(y_ref[...], u_ref[...], d_ref[...])
        gl = _mm(y0, wg_ref[...]) + bg_ref[...]
        m = _mix_mid(o_ref[...], za_ref[...], y0, gl, zb_ref[...], ra_ref[...], rb_ref[...], hn_ref[...])
        out = _mm(m, wo_ref[...])
        xn_ref[...] = _mix_post(x_ref[...], out, np_ref[...])

    act = pl.BlockSpec((tm, D), lambda i: (i, 0))
    return pl.pallas_call(
        body, name="mix_fwd", grid=(t // tm,),
        in_specs=[_proj_spec(tm, 3), _proj_spec(tm, 4), _proj_spec(tm, 5), _proj_spec(tm, 6), _proj_spec(tm, 7),
                  act, act, act, _full((1, DH)), _full((1, D)), _layer_mat(l), _full((1, D)), _layer_mat(l),
                  _full((1, D))],
        out_specs=act,
        out_shape=jax.ShapeDtypeStruct((t, D), F32),
        compiler_params=_cparams(("parallel",)),
    )(proj, proj, proj, proj, proj, o, s5y, x, hn, dvec, wglu, bglu, wout, npost)


def _mix_bwd(proj, o, s5y, x, hn, dvec, wglu, bglu, wout, npost, dxn, l):
    t = x.shape[0]
    tm = _tile(t, 128)

    def body(za_ref, u_ref, zb_ref, ra_ref, rb_ref, o_ref, y_ref, x_ref, hn_ref, d_ref, wg_ref, bg_ref, wo_ref,
             np_ref, dxn_ref,
             dza_ref, du_ref, dzb_ref, dra_ref, drb_ref, do_ref, dy_ref, dx_ref,
             y0_ref, dgl_ref, m_ref, dout_ref, dvecs_ref, dhn_ref):
        y0, vjp_pre = jax.vjp(_mix_pre, y_ref[...], u_ref[...], d_ref[...])
        gl = _mm(y0, wg_ref[...]) + bg_ref[...]
        m, vjp_mid = jax.vjp(_mix_mid, o_ref[...], za_ref[...], y0, gl, zb_ref[...], ra_ref[...], rb_ref[...],
                             hn_ref[...])
        out = _mm(m, wo_ref[...])
        _, vjp_post = jax.vjp(_mix_post, x_ref[...], out, np_ref[...])
        dx, dout, dnp = vjp_post(dxn_ref[...])
        dm = _mm_nt(dout, wo_ref[...])
        do, dza, dy0, dgl, dzb, dra, drb, dhn = vjp_mid(dm)
        y0_ref[...] = y0.astype(BF16)
        dgl_ref[...] = dgl.astype(BF16)
        m_ref[...] = m.astype(BF16)
        dout_ref[...] = dout.astype(BF16)
        dbg = jnp.sum(dgl, axis=0, keepdims=True)
        dy0 = dy0 + _mm_nt(dgl, wg_ref[...])
        dy, du, dd = vjp_pre(dy0)
        dza_ref[...] = dza.astype(dza_ref.dtype)
        du_ref[...] = du
        dzb_ref[...] = dzb.astype(dzb_ref.dtype)
        dra_ref[...] = dra.astype(dra_ref.dtype)
        drb_ref[...] = drb.astype(drb_ref.dtype)
        do_ref[...] = do
        dy_ref[...] = dy
        dx_ref[...] = dx
        first = pl.program_id(0) == 0

        @pl.when(first)
        def _():
            dvecs_ref[...] = jnp.zeros_like(dvecs_ref)
            dhn_ref[...] = jnp.zeros_like(dhn_ref)

        dvecs_ref[0:1, :] += dd
        dvecs_ref[1:2, :] += dbg
        dvecs_ref[2:3, :] += dnp
        dhn_ref[0:1, :] += dhn

    act = pl.BlockSpec((tm, D), lambda i: (i, 0))
    a, ga = jax.ShapeDtypeStruct((t, D), F32), jax.ShapeDtypeStruct((t, D), GRAD_ACT)
    b16 = jax.ShapeDtypeStruct((t, D), BF16)
    outs = pl.pallas_call(
        body, name="mix_bwd", grid=(t // tm,),
        in_specs=[_proj_spec(tm, 3), _proj_spec(tm, 4), _proj_spec(tm, 5), _proj_spec(tm, 6), _proj_spec(tm, 7),
                  act, act, act, _full((1, DH)), _full((1, D)), _layer_mat(l), _full((1, D)), _layer_mat(l),
                  _full((1, D)), act],
        out_specs=[act] * 12 + [_full((8, D)), _full((8, DH))],
        out_shape=[ga, a, ga, ga, ga, a, a, a, b16, b16, b16, b16, jax.ShapeDtypeStruct((8, D), F32),
                   jax.ShapeDtypeStruct((8, DH), F32)],
        compiler_params=_cparams(("arbitrary",)),
    )(proj, proj, proj, proj, proj, o, s5y, x, hn, dvec, wglu, bglu, wout, npost, dxn)
    y0, dgl, m, dout = outs[8:12]
    return list(outs[:8]) + [_weight_grad(y0, dgl, "glu_dw"), _weight_grad(m, dout, "out_dw")] + list(outs[12:])


def _weight_grad(a, b, name):
    t = a.shape[0]
    tk = _tile(t, 1024)

    def body(a_ref, b_ref, o_ref):
        @pl.when(pl.program_id(0) == 0)
        def _():
            o_ref[...] = jnp.zeros_like(o_ref)

        o_ref[...] += _mm_tn(a_ref[...], b_ref[...])

    rows = pl.BlockSpec((tk, D), lambda i: (i, 0))
    return pl.pallas_call(
        body, name=name, grid=(t // tk,), in_specs=[rows, rows], out_specs=_full((D, D)),
        out_shape=jax.ShapeDtypeStruct((D, D), F32), compiler_params=_cparams(("arbitrary",)),
    )(a, b)


def _loss_grad(y, target):
    t = y.shape[0]
    tm = _tile(t, 512)

    def body(y_ref, t_ref, dy_ref, l_ref):
        err = y_ref[...] - t_ref[...]
        dy_ref[...] = err * (1.0 / D)
        part = jnp.sum(jnp.sum(err * err, axis=1, keepdims=True), axis=0, keepdims=True) * (0.5 / D)
        part = jnp.broadcast_to(part, (8, DH))

        @pl.when(pl.program_id(0) == 0)
        def _():
            l_ref[...] = part

        @pl.when(pl.program_id(0) > 0)
        def _():
            l_ref[...] += part

    act = pl.BlockSpec((tm, D), lambda i: (i, 0))
    return pl.pallas_call(
        body, name="loss_grad", grid=(t // tm,),
        in_specs=[act, act], out_specs=[act, _full((8, DH))],
        out_shape=[jax.ShapeDtypeStruct((t, D), F32), jax.ShapeDtypeStruct((8, DH), F32)],
        compiler_params=_cparams(("arbitrary",)),
    )(y, target)


def _flips(rel):
    x, y, c = lax.axis_index("x"), lax.axis_index("y"), lax.axis_index("c")
    fx, fy, fc = rel
    return (x ^ fx if fx else x, y ^ fy if fy else y, c ^ fc if fc else c)


CHIP_RELS = ((1, 0, 0), (0, 1, 0), (1, 1, 0))
ALL_RELS = tuple((fx, fy, fc) for fx in (0, 1) for fy in (0, 1) for fc in (0, 1) if (fx, fy, fc) != (0, 0, 0))


def _slot_of(pos, by_chip):
    px, py, pc = pos
    return 2 * px + py if by_chip else 4 * px + 2 * py + pc


class _Exchange:
    def __init__(self, srcs, rels, by_chip, scatter):
        self.srcs, self.rels, self.by_chip, self.scatter = list(srcs), rels, by_chip, scatter
        self.narr = len(self.srcs)
        nslot, nsem = NCHIP if by_chip else NDEV, self.narr * len(rels)
        self.in_specs = [pl.BlockSpec(memory_space=pl.ANY)] * self.narr
        self.out_specs = [pl.BlockSpec(memory_space=pl.ANY)] * self.narr
        self.out_shape = [jax.ShapeDtypeStruct((nslot,) + s.shape[-2:], s.dtype) for s in self.srcs]
        self.scratch_shapes = [pltpu.SemaphoreType.DMA((nsem,)), pltpu.SemaphoreType.DMA((nsem,)),
                               pltpu.SemaphoreType.DMA((self.narr,))]

    def _copies(self, src_refs, dst_refs, sems):
        send_sems, recv_sems, local_sems = sems
        my_slot = _slot_of(_flips((0, 0, 0)), self.by_chip)
        local, sends, arrivals = [], [], []
        for a, (src_ref, dst_ref) in enumerate(zip(src_refs, dst_refs)):
            local.append(pltpu.make_async_copy(src_ref.at[my_slot] if self.scatter else src_ref, dst_ref.at[my_slot],
                                               local_sems.at[a]))
            for k, rel in enumerate(self.rels):
                peer = _flips(rel)
                pair = dict(send_sem=send_sems.at[a * len(self.rels) + k], recv_sem=recv_sems.at[a * len(self.rels) + k],
                            device_id=peer, device_id_type=pl.DeviceIdType.MESH)
                part = src_ref.at[_slot_of(peer, self.by_chip)] if self.scatter else src_ref
                sends.append(pltpu.make_async_remote_copy(src_ref=part, dst_ref=dst_ref.at[my_slot], **pair))
                arrivals.append(pltpu.make_async_remote_copy(
                    src_ref=src_ref.at[0] if self.scatter else src_ref, dst_ref=dst_ref.at[_slot_of(peer, self.by_chip)],
                    **pair))
        return local, sends, arrivals

    def start(self, src_refs, dst_refs, sems):
        local, sends, _ = self._copies(src_refs, dst_refs, sems)
        for cp in local + sends:
            cp.start()

    def wait(self, src_refs, dst_refs, sems):
        local, sends, arrivals = self._copies(src_refs, dst_refs, sems)
        for cp in arrivals:
            cp.wait_recv()
        for cp in sends:
            cp.wait_send()
        for cp in local:
            cp.wait()


def _exchange(srcs, rels, by_chip, scatter, name):
    ex = _Exchange(srcs, rels, by_chip, scatter)

    def body(*refs):
        parts = refs[:ex.narr], refs[ex.narr:2 * ex.narr], refs[2 * ex.narr:]
        ex.start(*parts)
        ex.wait(*parts)

    return pl.pallas_call(body, name=name, in_specs=ex.in_specs, out_specs=ex.out_specs, out_shape=ex.out_shape,
                          scratch_shapes=ex.scratch_shapes)(*ex.srcs)


def _sibling_swap(srcs, name):
    narr = len(srcs)

    def body(*refs):
        src_refs, dst_refs = refs[:narr], refs[narr:2 * narr]
        send_sems, recv_sems = refs[2 * narr:]
        peer = _flips((0, 0, 1))
        copies = [pltpu.make_async_remote_copy(src_ref=s, dst_ref=d, send_sem=send_sems.at[a], recv_sem=recv_sems.at[a],
                                               device_id=peer, device_id_type=pl.DeviceIdType.MESH)
                  for a, (s, d) in enumerate(zip(src_refs, dst_refs))]
        for cp in copies:
            cp.start()
        for cp in copies:
            cp.wait()

    return pl.pallas_call(
        body, name=name,
        in_specs=[pl.BlockSpec(memory_space=pl.ANY)] * narr,
        out_specs=[pl.BlockSpec(memory_space=pl.ANY)] * narr,
        out_shape=[jax.ShapeDtypeStruct(s.shape, s.dtype) for s in srcs],
        scratch_shapes=[pltpu.SemaphoreType.DMA((narr,)), pltpu.SemaphoreType.DMA((narr,))],
    )(*srcs)


def _all_reduce(src, name):
    rows, cols = src.shape
    r = rows // NDEV
    nrel = len(ALL_RELS)

    def body(src_ref, out_ref, parts_ref, mine_ref, send_sems, recv_sems):
        my_slot = _slot_of(_flips((0, 0, 0)), False)

        def piece(ref, slot):
            return ref.at[pl.ds(pl.multiple_of(slot * r, 8), r), :]

        def copies(phase):
            out = []
            for k, rel in enumerate(ALL_RELS):
                peer = _flips(rel)
                pair = dict(send_sem=send_sems.at[phase * nrel + k], recv_sem=recv_sems.at[phase * nrel + k],
                            device_id=peer, device_id_type=pl.DeviceIdType.MESH)
                if phase == 0:
                    out.append(pltpu.make_async_remote_copy(src_ref=piece(src_ref, _slot_of(peer, False)),
                                                            dst_ref=parts_ref.at[my_slot], **pair))
                else:
                    out.append(pltpu.make_async_remote_copy(src_ref=mine_ref, dst_ref=piece(out_ref, my_slot), **pair))
            return out

        first = copies(0)
        for cp in first:
            cp.start()
        parts_ref[my_slot] = piece(src_ref, my_slot)[...]
        for cp in first:
            cp.wait_recv()
        acc = parts_ref[0]
        for s in range(1, NDEV):
            acc = acc + parts_ref[s]
        mine_ref[...] = acc
        second = copies(1)
        for cp in second:
            cp.start()
        piece(out_ref, my_slot)[...] = acc
        for cp in second:
            cp.wait_recv()
        for cp in first + second:
            cp.wait_send()

    return pl.pallas_call(
        body, name=name,
        in_specs=[pl.BlockSpec(memory_space=pltpu.VMEM)], out_specs=pl.BlockSpec(memory_space=pltpu.VMEM),
        out_shape=jax.ShapeDtypeStruct(src.shape, src.dtype),
        scratch_shapes=[pltpu.VMEM((NDEV, r, cols), src.dtype), pltpu.VMEM((r, cols), src.dtype),
                        pltpu.SemaphoreType.DMA((2 * nrel,)), pltpu.SemaphoreType.DMA((2 * nrel,))],
        compiler_params=pltpu.CompilerParams(vmem_limit_bytes=VMEM_LIMIT),
    )(src)


def _sum_slots(parts, name):
    ns, rows, cols = parts.shape
    tr = _row_tile(rows, 256)

    def body(p_ref, o_ref):
        acc = p_ref[0].astype(F32)
        for s in range(1, ns):
            acc = acc + p_ref[s].astype(F32)
        o_ref[...] = acc

    return pl.pallas_call(
        body, name=name, grid=(rows // tr,),
        in_specs=[pl.BlockSpec((ns, tr, cols), lambda i: (0, i, 0))],
        out_specs=pl.BlockSpec((tr, cols), lambda i: (i, 0)),
        out_shape=jax.ShapeDtypeStruct((rows, cols), F32),
        compiler_params=_cparams(("parallel",)),
    )(parts)


def _adamw(w, g_parts, m, v, name, max_rows=256):
    if w.ndim == 2:
        return [o[0] for o in _adamw(w[None], g_parts, m[None], v[None], name, max_rows)]
    nl, rows, cols = w.shape
    tr = _row_tile(rows, max_rows)
    per_layer = rows // tr
    c1 = 1.0 / (1.0 - ADAM_B1 ** ADAM_STEP)
    c2 = 1.0 / (1.0 - ADAM_B2 ** ADAM_STEP)
    npart = len(g_parts)

    def body(*refs):
        w_ref, m_ref, v_ref = refs[:3]
        g_refs = refs[3:3 + npart]
        go_ref, d_ref, nm_ref, nv_ref = refs[3 + npart:]
        terms = []
        for g_ref in g_refs:
            terms += [g_ref[...]] if len(g_ref.shape) == 2 else [g_ref[s] for s in range(g_ref.shape[0])]
        g = terms[0]
        for term in terms[1:]:
            g = g + term
        nm = ADAM_B1 * m_ref[...] + (1.0 - ADAM_B1) * g
        nv = ADAM_B2 * v_ref[...] + (1.0 - ADAM_B2) * (g * g)
        d_ref[...] = -ADAM_LR * ((nm * c1) / (jnp.sqrt(nv * c2) + ADAM_EPS) + ADAM_WD * w_ref[...])
        go_ref[...] = g
        nm_ref[...] = nm
        nv_ref[...] = nv

    blk = pl.BlockSpec((None, tr, cols), lambda l, i: (l, i, 0))
    g_specs = [pl.BlockSpec((tr, cols), lambda l, i: (l * per_layer + i, 0)) if p.ndim == 2 else
               pl.BlockSpec((p.shape[0], tr, cols), lambda l, i: (0, l * per_layer + i, 0)) for p in g_parts]
    out = jax.ShapeDtypeStruct((nl, rows, cols), F32)
    return pl.pallas_call(
        body, name=name, grid=(nl, per_layer),
        in_specs=[blk, blk, blk] + g_specs,
        out_specs=[blk] * 4, out_shape=[out] * 4,
        compiler_params=_cparams(("parallel", "parallel")),
    )(w, m, v, *g_parts)


WEIGHT_SPLIT = (0, 384, 704, D)
WIN_SHARD = 2052
CONV_SHARD = 768
ROW_SHARD = 256

SMALL = (("norm_pre", (DEPTH, D)), ("a_log", (DEPTH, NH)), ("dt_bias", (DEPTH, NH)), ("head_norm", (DEPTH, DH)),
         ("ssm_a_re", (DEPTH, NG, NS)), ("ssm_a_im", (DEPTH, NG, NS)), ("ssm_log_dt", (DEPTH, NG)),
         ("ssm_b_re", (DEPTH, NG, NS, GS)), ("ssm_b_im", (DEPTH, NG, NS, GS)),
         ("ssm_c_re", (DEPTH, NG, GS, NS)), ("ssm_c_im", (DEPTH, NG, GS, NS)), ("ssm_d", (DEPTH, D)),
         ("b_glu", (DEPTH, D)), ("norm_post", (DEPTH, D)))


def _pad_rows(flat, rows):
    return jnp.pad(flat, (0, rows * D - flat.shape[0])).reshape(rows, D)


def _rows_by_chip(a):
    nl, rows, cols = a.shape
    return a.reshape(nl, NCHIP, rows // NCHIP, cols).transpose(1, 0, 2, 3).reshape(NCHIP, -1, cols)


def _rows_from_chips(a):
    _, rows, cols = a.shape
    return a.reshape(NCHIP, DEPTH, rows // DEPTH, cols).transpose(1, 0, 2, 3).reshape(DEPTH, -1, cols)


def _cols_by_chip(a):
    nl, rows, cols = a.shape
    return a.reshape(nl, rows, NCHIP, cols // NCHIP).transpose(2, 0, 1, 3).reshape(NCHIP, nl * rows, -1)


def _cols_from_chips(a, nl):
    _, rows, cols = a.shape
    return a.reshape(NCHIP, nl, rows // nl, cols).transpose(1, 2, 0, 3).reshape(nl, rows // nl, NCHIP * cols)


SMALL_ROWS = sum(-(-math.prod(s) // (8 * D)) * 8 for _, s in SMALL)
CONV_ROWS = DEPTH * 4 * 3 * D // D


def _pack_small(vals, extra=()):
    parts = []
    for val in tuple(vals) + tuple(extra):
        n = val.size
        parts.append(_pad_rows(val.reshape(-1), -(-n // (8 * D)) * 8))
    return jnp.concatenate(parts, axis=0)


def _unpack_small(flat):
    outs, r0 = [], 0
    for _, shape in SMALL:
        n = math.prod(shape)
        rows = -(-n // (8 * D)) * 8
        outs.append(flat[r0:r0 + rows].reshape(-1)[:n].reshape(shape))
        r0 += rows
    return outs


def _rearrange_cols(w):
    pad = jnp.zeros(w.shape[:-1] + (NCOL - BD0 - 2 * NH,), w.dtype)
    return jnp.concatenate([w[..., :4 * D], w[..., 4 * D + 2 * NH:], w[..., 4 * D:4 * D + 2 * NH], pad], axis=-1)


def _restore_cols(w):
    return jnp.concatenate([w[..., :4 * D], w[..., BD0:BD0 + 2 * NH], w[..., 4 * D:BD0]], axis=-1)


LOGITS_IN_CHIP1 = 2 * WIN_SHARD - 4 * D
LOGITS_IN_CHIP2 = 2 * NH - LOGITS_IN_CHIP1


def _wcat_from_chips(g):
    before = WIN_SHARD - LOGITS_IN_CHIP1
    pad = jnp.zeros((D, NCOL - BD0 - 2 * NH), g.dtype)
    return jnp.concatenate([g[0], g[1][:, :before], g[2][:, LOGITS_IN_CHIP2:], g[3], g[1][:, before:],
                            g[2][:, :LOGITS_IN_CHIP2], pad], axis=1)[None]


def _wcat_grad_by_chip(gw):
    before, mid = WIN_SHARD - LOGITS_IN_CHIP1, 4 * D + WIN_SHARD - LOGITS_IN_CHIP2
    return jnp.stack([gw[:, :WIN_SHARD],
                      jnp.concatenate([gw[:, WIN_SHARD:4 * D], gw[:, BD0:BD0 + LOGITS_IN_CHIP1]], axis=1),
                      jnp.concatenate([gw[:, BD0 + LOGITS_IN_CHIP1:BD0 + 2 * NH], gw[:, 4 * D:mid]], axis=1),
                      gw[:, mid:BD0]])


def _block_diag_b(bb2):
    b = bb2.reshape(-1, NCB, GPB, NS, GS)
    return jnp.einsum("lkgnc,gh->lkgchn", b, jnp.eye(GPB, dtype=F32)).reshape(-1, NCB, GPB * GS, SW)


def _block_diag_b_t(d):
    blocks = jnp.einsum("lkgchn,gh->lkgnc", d.reshape(-1, NCB, GPB, GS, GPB, NS), jnp.eye(GPB, dtype=F32))
    return blocks.reshape(-1, NG, NS * GS)


def _block_diag_c(c):
    blocks = jnp.einsum("lkgcn,gh->lkgnhc", c.reshape(-1, NCB, GPB, GS, NS), jnp.eye(GPB, dtype=F32))
    return blocks.reshape(-1, NCB, SW, GPB * GS)


def _block_diag_c_t(d):
    blocks = jnp.einsum("lkgnhc,gh->lkgcn", d.reshape(-1, NCB, GPB, NS, GPB, GS), jnp.eye(GPB, dtype=F32))
    return blocks.reshape(-1, NG, GS, NS)


def _local_step(x, target, weights, conv, small, comm=None):
    weights = list(weights) + [None] * (DEPTH - len(weights))
    ar, ai = small["ssm_a_re"], small["ssm_a_im"]
    ldt = small["ssm_log_dt"].reshape(DEPTH, NG, 1)
    br2 = small["ssm_b_re"].reshape(DEPTH, NG, NS * GS)
    bi2 = small["ssm_b_im"].reshape(DEPTH, NG, NS * GS)
    lr, li, bbr2, bbi2 = _s5_params(ar, ai, ldt, br2, bi2)

    def row(name, l, width):
        return small[name][l].reshape(1, width)

    gvecs = jnp.pad(jnp.stack([small["a_log"], small["dt_bias"]], axis=1), ((0, 0), (0, 6), (NH, DH - 2 * NH)))
    lams = jnp.stack([lr.reshape(DEPTH, 1, NG * NS), li.reshape(DEPTH, 1, NG * NS)], axis=1)
    bblks = jnp.stack([_block_diag_b(bbr2), _block_diag_b(bbi2)], axis=1)
    cblks = jnp.stack([_block_diag_c(small["ssm_c_re"]), _block_diag_c(small["ssm_c_im"])], axis=1)
    saved = []
    for l in range(DEPTH):
        gvec, lam, bblk, cblk = gvecs[l], lams[l], bblks[l], cblks[l]
        wcat, wglu, wout = weights[l]
        fetch = [None] * 3
        if comm and l + 1 < DEPTH:
            fetch = [_Exchange(comm["weight_parts"](l + 1, part), CHIP_RELS, True, False) for part in range(3)]
        proj, h, got0 = _inproj_fwd(x, row("norm_pre", l, D), wcat, 0, fetch[0])
        qkv = _prep_fwd(proj, conv[l])
        bb, gcb = _gates_fwd(proj, gvec)
        local, t_inv, got1 = _delta_local_fwd(qkv, bb, gcb, fetch[1])
        o, states = _delta_state_fwd(local, gcb)
        s5y, carries, got2 = _s5_fwd(proj, lam, bblk, cblk, fetch[2])
        if fetch[0] is not None:
            weights[l + 1] = comm["weights_from"]([got0, got1, got2])
        xn = _mix_fwd(proj, o, s5y, x, row("head_norm", l, DH), row("ssm_d", l, D), wglu, row("b_glu", l, D),
                      wout, row("norm_post", l, D), 0)
        saved.append((x, proj, h, qkv, bb, gcb, local, t_inv, o, states, s5y, carries, gvec, lam, bblk, cblk))
        x = xn

    dx, loss_part = _loss_grad(x, target)

    g = {k: [None] * DEPTH for k in ("wcat", "conv", "wglu", "wout", "norm_pre", "a_log", "dt_bias", "head_norm",
                                     "ssm_d", "b_glu", "norm_post", "dlam", "dbblk", "dcblk")}
    from_chips, send, send_layer = [None] * DEPTH, None, None
    for l in reversed(range(DEPTH)):
        xl, proj, h, qkv, bb, gcb, local, t_inv, o, states, s5y, carries, gvec, lam, bblk, cblk = saved[l]
        wcat, wglu, wout = weights[l]
        (dza, du_skip, dzb, dra, drb, do, ds5y, dxres, dwg, dwo, dvecs, dhn) = _mix_bwd(
            proj, o, s5y, xl, row("head_norm", l, DH), row("ssm_d", l, D), wglu, row("b_glu", l, D), wout,
            row("norm_post", l, D), dx, 0)
        (du, dlam, dbblk, dcblk), arrived = _s5_bwd(proj, lam, bblk, cblk, carries, ds5y, du_skip, send)
        if send is not None:
            from_chips[send_layer] = arrived
        *dlocal, dgcb_state = _delta_state_bwd(local, gcb, states, do)
        dq, dk, dv, dbb, dgcb = _delta_local_bwd(qkv, bb, gcb, t_inv, dlocal, dgcb_state)
        dbd, dgvec = _gates_bwd(proj, gvec, dbb, dgcb)
        dpre, dconv = _prep_bwd(proj, conv[l], dq, dk, dv)
        dproj = jnp.concatenate([dpre, dza, du, dzb, dra, drb, dbd], axis=1)
        g["wcat"][l] = _inproj_bwd_dw(h, dproj)
        g["conv"][l], g["wglu"][l], g["wout"][l] = dconv, dwg, dwo
        send = _Exchange(comm["grad_parts"](g["wcat"][l], dwg, dwo), CHIP_RELS, True, True) if comm else None
        dx, dgain, arrived = _inproj_bwd_dx(dproj, wcat, xl, row("norm_pre", l, D), dxres, 0, send if l == 0 else None)
        if comm and l == 0:
            from_chips[l] = arrived
        send_layer = l
        g["norm_pre"][l] = dgain[0]
        g["a_log"][l], g["dt_bias"][l] = dgvec[0, NH:2 * NH], dgvec[1, NH:2 * NH]
        g["head_norm"][l] = dhn[0]
        g["ssm_d"][l], g["b_glu"][l], g["norm_post"][l] = dvecs[0], dvecs[1], dvecs[2]
        g["dlam"][l], g["dbblk"][l], g["dcblk"][l] = dlam, dbblk, dcblk
    if comm:
        for k in ("wcat", "wglu", "wout"):
            del g[k]
    g = {k: jnp.stack(v) for k, v in g.items()}
    g["from_chips"] = from_chips
    dlam, dbblk, dcblk = g.pop("dlam"), g.pop("dbblk"), g.pop("dcblk")
    g["ssm_c_re"], g["ssm_c_im"] = _block_diag_c_t(dcblk[:, 0]), _block_diag_c_t(dcblk[:, 1])
    dar, dai, dldt, dbr2, dbi2 = _s5_params_bwd(
        ar, ai, ldt, br2, bi2, dlam[:, 0].reshape(DEPTH, NG, NS), dlam[:, 1].reshape(DEPTH, NG, NS),
        _block_diag_b_t(dbblk[:, 0]), _block_diag_b_t(dbblk[:, 1]))
    g["ssm_a_re"], g["ssm_a_im"], g["ssm_log_dt"] = dar, dai, dldt.reshape(DEPTH, NG)
    g["ssm_b_re"] = dbr2.reshape(DEPTH, NG, NS, GS)
    g["ssm_b_im"] = dbi2.reshape(DEPTH, NG, NS, GS)
    return loss_part[0, 0], dx, g


def kernel(x, norm_pre, w_in, conv_w, a_log, dt_bias, head_norm, ssm_a_re, ssm_a_im, ssm_log_dt, ssm_b_re, ssm_b_im, ssm_c_re, ssm_c_im, ssm_d, w_glu, b_glu, w_out, norm_post, loss_target, m_norm_pre, m_w_in, m_conv_w, m_a_log, m_dt_bias, m_head_norm, m_ssm_a_re, m_ssm_a_im, m_ssm_log_dt, m_ssm_b_re, m_ssm_b_im, m_ssm_c_re, m_ssm_c_im, m_ssm_d, m_w_glu, m_b_glu, m_w_out, m_norm_post, v_norm_pre, v_w_in, v_conv_w, v_a_log, v_dt_bias, v_head_norm, v_ssm_a_re, v_ssm_a_im, v_ssm_log_dt, v_ssm_b_re, v_ssm_b_im, v_ssm_c_re, v_ssm_c_im, v_ssm_d, v_w_glu, v_b_glu, v_w_out, v_norm_post):
    args = dict(locals())
    small = {n: args[n] for n, _ in SMALL}

    def flat2(a):
        return a.reshape(-1, a.shape[-1])

    w_in16, w_glu16, w_out16 = w_in.astype(BF16), w_glu.astype(BF16), w_out.astype(BF16)

    def weight_parts(l, part=None):
        if part is None:
            return [w_in16[l], w_glu16[l], w_out16[l]]
        lo, hi = WEIGHT_SPLIT[part], WEIGHT_SPLIT[part + 1]
        return [w_in16[l, lo:hi]] + [[w_glu16[l]], [w_out16[l]], []][part]

    def weights_from(parts):
        if len(parts) == 3 and isinstance(parts[0], (list, tuple)):
            parts = [jnp.concatenate([p[0] for p in parts], axis=1), parts[0][1], parts[1][1]]
        g_in, g_glu, g_out = parts[:3]
        return _wcat_from_chips(g_in), g_glu.reshape(1, D, D), g_out.reshape(1, D, D)

    def grad_parts(gwcat, gwglu, gwout):
        return [_wcat_grad_by_chip(gwcat).astype(BF16), gwglu.reshape(NCHIP, ROW_SHARD, D).astype(BF16),
                gwout.reshape(NCHIP, ROW_SHARD, D).astype(BF16)]

    first = _exchange(weight_parts(0) + [flat2(conv_w)], CHIP_RELS, True, False, "gather_weights")
    conv = _cols_from_chips(first[3], DEPTH)
    comm = dict(weight_parts=weight_parts, weights_from=weights_from, grad_parts=grad_parts)
    loss_part, dx, g = _local_step(x[0], loss_target[0], [weights_from(first)], conv, small, comm)
    loss = lax.psum(loss_part, ("x", "y", "c"))

    from_chips = [jnp.concatenate([g["from_chips"][l][a] for l in range(DEPTH)], axis=1) for a in range(3)]
    core_sums = [_sum_slots(p, "sum_chips_" + n) for p, n in zip(from_chips, ("in", "glu", "out"))]
    others = _sibling_swap(core_sums, "swap_cores")
    sharded = {}
    for n, mine, other in zip(("w_in", "w_glu", "w_out"), core_sums, others):
        sharded[n] = _adamw(args[n], [mine, other], args["m_" + n], args["v_" + n], "adamw_" + n, max_rows=128)

    pad = jnp.zeros(((-(SMALL_ROWS + CONV_ROWS)) % (8 * NDEV), D), F32)
    small_sum = _all_reduce(_pack_small([g[n] for n, _ in SMALL], extra=[g["conv"], pad]), "reduce_small")
    small_out = _adamw(_pack_small([args[n] for n, _ in SMALL]), [small_sum],
                       _pack_small([args["m_" + n] for n, _ in SMALL]),
                       _pack_small([args["v_" + n] for n, _ in SMALL]), "adamw_small")
    chip = 2 * lax.axis_index("x") + lax.axis_index("y")
    conv_sum = small_sum[SMALL_ROWS:SMALL_ROWS + CONV_ROWS].reshape(DEPTH * 4, 3 * D)
    conv_sum = lax.dynamic_slice_in_dim(conv_sum, chip * CONV_SHARD, CONV_SHARD, axis=1)
    sharded["conv_w"] = _adamw(flat2(conv_w), [conv_sum], flat2(m_conv_w), flat2(v_conv_w), "adamw_conv")

    names = ["norm_pre", "w_in", "conv_w", "a_log", "dt_bias", "head_norm", "ssm_a_re", "ssm_a_im", "ssm_log_dt",
             "ssm_b_re", "ssm_b_im", "ssm_c_re", "ssm_c_im", "ssm_d", "w_glu", "b_glu", "w_out", "norm_post"]
    outs = [loss, dx[None]]
    for i in range(4):
        sm = dict(zip([n for n, _ in SMALL], _unpack_small(small_out[i])))
        outs += [sharded[n][i].reshape(args[n].shape) if n in sharded else sm[n] for n in names]
    return tuple(outs)
```

```python
import functools
import math

import jax
import jax.numpy as jnp
from jax import lax
from jax.experimental import pallas as pl
from jax.experimental.pallas import tpu as pltpu

F32 = jnp.float32
BF16 = jnp.bfloat16
HI = lax.Precision.HIGHEST

D = 1024
NH = 8
DH = 128
CH = 128
NG = 64
GS = 16
NS = 64
GPB = 8
NCB = NG // GPB
SW = GPB * NS
NCOL = 8320
BD0 = 8192
EPS = 1e-6
DEPTH = 4
NCHIP = 4
NDEV = 8
VMEM_LIMIT = 56 * 1024 * 1024
GRAD_ACT = jnp.bfloat16

ADAM_LR = 0.001
ADAM_B1 = 0.9
ADAM_B2 = 0.999
ADAM_EPS = 1e-08
ADAM_WD = 0.01
ADAM_STEP = 10


def _cparams(sem=None):
    return pltpu.CompilerParams(dimension_semantics=sem, vmem_limit_bytes=VMEM_LIMIT)


def _full(shape):
    nd = len(shape)
    return pl.BlockSpec(shape, lambda *_: (0,) * nd)


def _rms(x, gain):
    ms = jnp.mean(x * x, axis=-1, keepdims=True)
    return x * lax.rsqrt(ms + EPS) * gain


def _sigmoid(x):
    return 1.0 / (1.0 + jnp.exp(-x))


def _silu(x):
    return x * _sigmoid(x)


def _softplus(x):
    return jnp.maximum(x, 0.0) + jnp.log(1.0 + jnp.exp(-jnp.abs(x)))


def _gelu(x):
    return 0.5 * x * (1.0 + jnp.tanh(math.sqrt(2.0 / math.pi) * (x + 0.044715 * (x * x * x))))


def _dot_bf16(a, b, dims):
    return lax.dot_general(a.astype(BF16), b.astype(BF16), (dims, ((), ())), preferred_element_type=F32)


def _mm_nt(a, b):
    return _dot_bf16(a, b, ((1,), (1,)))


def _mm_tn(a, b):
    return _dot_bf16(a, b, ((0,), (0,)))


@jax.custom_vjp
def _mm(a, b):
    return _dot_bf16(a, b, ((1,), (0,)))


def _mm_fwd(a, b):
    return _dot_bf16(a, b, ((1,), (0,))), (a, b)


def _mm_bwd(res, ct):
    a, b = res
    return _mm_nt(ct, b).astype(a.dtype), _mm_tn(a, ct).astype(b.dtype)


_mm.defvjp(_mm_fwd, _mm_bwd)


@jax.custom_vjp
def _mm_nt_d(a, b):
    return _mm_nt(a, b)


def _mm_nt_d_bwd(res, ct):
    a, b = res
    return _dot_bf16(ct, b, ((1,), (0,))), _mm_tn(ct, a)


_mm_nt_d.defvjp(lambda a, b: (_mm_nt(a, b), (a, b)), _mm_nt_d_bwd)


@jax.custom_vjp
def _mm_tn_d(a, b):
    return _mm_tn(a, b)


def _mm_tn_d_bwd(res, ct):
    a, b = res
    return _mm_nt(b, ct), _dot_bf16(a, ct, ((1,), (0,)))


_mm_tn_d.defvjp(lambda a, b: (_mm_tn(a, b), (a, b)), _mm_tn_d_bwd)


def _split_bf16(a):
    hi = a.astype(BF16)
    return hi, (a - hi.astype(F32)).astype(BF16)


def _dot3(a, b, dims):
    ah, al = _split_bf16(a)
    bh, bl = _split_bf16(b)

    def dot(x, y):
        return lax.dot_general(x, y, (dims, ((), ())), preferred_element_type=F32)

    return dot(ah, bh) + (dot(ah, bl) + dot(al, bh))


@jax.custom_vjp
def _imm(a, b):
    return _dot3(a, b, ((1,), (0,)))


def _imm_bwd(res, ct):
    a, b = res
    return _dot3(ct, b, ((1,), (1,))), _dot3(a, ct, ((0,), (0,)))


_imm.defvjp(lambda a, b: (_dot3(a, b, ((1,), (0,))), (a, b)), _imm_bwd)


def _hmm(a, b):
    return jnp.dot(a, b, precision=HI, preferred_element_type=F32)


def _rows(shape):
    return lax.broadcasted_iota(jnp.int32, shape, 0)


def _cols(shape):
    return lax.broadcasted_iota(jnp.int32, shape, 1)


def _sd(x, s):
    return jnp.where(_rows(x.shape) >= s, pltpu.roll(x, s, axis=0), 0.0)


def _su(x, s):
    n = x.shape[0]
    return jnp.where(_rows(x.shape) < n - s, pltpu.roll(x, n - s, axis=0), 0.0)


@functools.partial(jax.custom_vjp, nondiff_argnums=(1,))
def _shift_down(x, s):
    return _sd(x, s)


def _shift_down_fwd(x, s):
    return _sd(x, s), None


def _shift_down_bwd(s, _, g):
    return (_su(g, s),)


_shift_down.defvjp(_shift_down_fwd, _shift_down_bwd)


def _last_row(x):
    n = x.shape[0]
    return jnp.sum(jnp.where(_rows(x.shape) == n - 1, x, 0.0), axis=0, keepdims=True)


def _prep_fn(p, w0, w1, w2, w3, qk):
    acc = w3 * p + w2 * _shift_down(p, 1) + w1 * _shift_down(p, 2) + w0 * _shift_down(p, 3)
    a = _silu(acc)
    nrm = lax.rsqrt(jnp.sum(a * a, axis=-1, keepdims=True) + EPS)
    return a * (nrm * qk + (1.0 - qk))


def _gates_fn(bd, av, bv):
    tm = bd.shape[0]
    beta_all = _sigmoid(bd)
    g_all = -jnp.exp(av) * _softplus(bd + bv)
    r, c = _rows((tm, tm)), _cols((tm, tm))
    tri = jnp.where((r // CH == c // CH) & (r >= c), 1.0, 0.0).astype(F32)
    gc_all = _hmm(tri, g_all)
    lane = _cols(bd.shape)
    outs = []
    for h in range(NH):
        b = jnp.sum(jnp.where(lane == h, beta_all, 0.0), axis=1, keepdims=True)
        outs.append(jnp.broadcast_to(b, bd.shape))
    for h in range(NH):
        g = jnp.sum(jnp.where(lane == NH + h, gc_all, 0.0), axis=1, keepdims=True)
        outs.append(jnp.broadcast_to(g, bd.shape))
    return tuple(outs)


INV_BASE = 2


def _merge_mm(a, b):
    return _dot_bf16(a, b, ((1,), (0,)))


def _unit_lower_inv(l_mats):
    n = l_mats[0].shape[0]
    ii, jj = _rows((n, n)), _cols((n, n))
    base = ii // INV_BASE == jj // INV_BASE
    ps = [-jnp.where(base, l_mat, 0.0) for l_mat in l_mats]
    eye = jnp.where(ii == jj, 1.0, 0.0).astype(F32)
    ds = [eye + p for p in ps]
    k = 1
    while 2 * k < INV_BASE:
        ps = [_imm(p, p) for p in ps]
        ds = [d + _imm(d, p) for d, p in zip(ds, ps)]
        k *= 2
    m = INV_BASE
    while m < n:
        pair = (ii // (2 * m) == jj // (2 * m)) & (ii // m > jj // m)
        des = [_merge_mm(d, jnp.where(pair, l_mat, 0.0)) for d, l_mat in zip(ds, l_mats)]
        ds = [d - _merge_mm(de, d) for d, de in zip(ds, des)]
        m *= 2
    return ds


@jax.custom_vjp
def _known_inverse(l_mat, t_inv):
    return t_inv


def _known_inverse_bwd(t_inv, ct):
    d_l = -_dot3(_dot3(t_inv, ct, ((0,), (0,))), t_inv, ((1,), (1,)))
    return d_l, jnp.zeros_like(t_inv)


_known_inverse.defvjp(lambda l_mat, t_inv: (t_inv, t_inv), _known_inverse_bwd)


def _chunk_system(q, k, v, bb, gcb):
    qs = q * (DH ** -0.5)
    kb = k * bb
    eg = jnp.exp(gcb)
    ii, jj = _rows((CH, CH)), _cols((CH, CH))
    decay = jnp.exp(jnp.where(ii >= jj, gcb - gcb.T, -1e30))
    l_mat = jnp.where(ii > jj, _mm_nt_d(kb, k) * decay, 0.0)
    a_qk = _mm_nt_d(qs, k) * decay
    k_dec = k * jnp.exp(_last_row(gcb) - gcb)
    return l_mat, (v * bb, kb * eg, qs * eg, k_dec, a_qk)


def _chunk_solve(t_inv, rest):
    vb, kbe, q_dec, k_dec, a_qk = rest
    return _mm(t_inv, vb), _mm(t_inv, kbe), q_dec, k_dec, a_qk


def _side_by_side_vjp(fn, items, cts):
    n = len(items[0])
    _, vjp = jax.vjp(lambda *flat: fn([flat[i * n:(i + 1) * n] for i in range(len(items))]),
                     *[a for item in items for a in item])
    grads = vjp(cts)
    return [grads[i * n:(i + 1) * n] for i in range(len(items))]


def _chunks_local_known(items):
    systems = [_chunk_system(*item[:5]) for item in items]
    t_invs = [_known_inverse(l_mat, item[5]) for (l_mat, _), item in zip(systems, items)]
    return [_chunk_solve(t_inv, rest) for t_inv, (_, rest) in zip(t_invs, systems)]


def _chunks_local(chunks):
    systems = [_chunk_system(*c) for c in chunks]
    t_invs = _unit_lower_inv([l_mat for l_mat, _ in systems])
    return [(_chunk_solve(t_inv, rest), t_inv) for t_inv, (_, rest) in zip(t_invs, systems)]


def _state_steps(items):
    v_news = [u - _mm(w, state) for u, w, _, _, _, _, state in items]
    outs = [_mm(q_dec, state) + _mm(a_qk, v_new) for (_, _, q_dec, _, a_qk, _, state), v_new in zip(items, v_news)]
    states = [state * jnp.exp(_last_row(gcb)) + _mm_tn_d(k_dec, v_new)
              for (_, _, _, k_dec, _, gcb, state), v_new in zip(items, v_news)]
    return list(zip(outs, states))


SUB = 8


SCAN_ROWS = 32


def _cmul(ar, ai, br, bi):
    return ar * br - ai * bi, ar * bi + ai * br


def _scan_tile(xr, xi, mr, mi, hr_ref, hi_ref, cr_ref, ci_ref, reverse, fold):
    n, width = xr.shape
    ngroups, nlb = n // SUB, width // DH
    shift_groups = _su if reverse else _sd

    def lanes(x, j):
        return x[..., j * DH:(j + 1) * DH]

    start = n - 1 if reverse else 0
    for j in range(nlb):
        hr_ref[j] = lanes(xr, j)
        hi_ref[j] = lanes(xi, j)
        hr_ref[j, start:start + 1, :] += lanes(fold[0], j)
        hi_ref[j, start:start + 1, :] += lanes(fold[1], j)
    pr, pi = mr, mi
    tr, ti = jnp.broadcast_to(mr, (SUB, width)), jnp.broadcast_to(mi, (SUB, width))
    pos = _rows(tr.shape)
    steps = []
    s = 1
    while s < SUB:
        inside = pos < SUB - s if reverse else pos >= s
        shift = SUB - s if reverse else s
        steps.append((shift, jnp.where(inside, pr, 0.0), jnp.where(inside, pi, 0.0)))
        er = jnp.where(inside, pltpu.roll(tr, shift, axis=0), 1.0)
        ei = jnp.where(inside, pltpu.roll(ti, shift, axis=0), 0.0)
        tr, ti = _cmul(tr, ti, er, ei)
        pr, pi = _cmul(pr, pi, pr, pi)
        s *= 2
    for b in range(0, n, SCAN_ROWS):
        rows = slice(b, b + SCAN_ROWS)
        for j in range(nlb):
            br = hr_ref[j, rows, :].reshape(SCAN_ROWS // SUB, SUB, DH)
            bi = hi_ref[j, rows, :].reshape(SCAN_ROWS // SUB, SUB, DH)
            for shift, qr, qi in steps:
                dr, di = _cmul(lanes(qr, j)[None], lanes(qi, j)[None],
                               pltpu.roll(br, shift, axis=1), pltpu.roll(bi, shift, axis=1))
                br, bi = br + dr, bi + di
            hr_ref[j, rows, :] = br.reshape(SCAN_ROWS, DH)
            hi_ref[j, rows, :] = bi.reshape(SCAN_ROWS, DH)
    edge = pl.ds(0 if reverse else SUB - 1, ngroups, stride=SUB)
    gr = jnp.concatenate([hr_ref.at[j][edge, :] for j in range(nlb)], axis=1)
    gi = jnp.concatenate([hi_ref.at[j][edge, :] for j in range(nlb)], axis=1)
    s = 1
    while s < ngroups:
        dr, di = _cmul(pr, pi, shift_groups(gr, s), shift_groups(gi, s))
        gr, gi = gr + dr, gi + di
        pr, pi = _cmul(pr, pi, pr, pi)
        s *= 2
    cr_ref[...] = shift_groups(gr, 1)
    ci_ref[...] = shift_groups(gi, 1)
    for g in range(ngroups):
        rows = slice(g * SUB, (g + 1) * SUB)
        dr, di = _cmul(tr, ti, cr_ref[g:g + 1, :], ci_ref[g:g + 1, :])
        for j in range(nlb):
            hr_ref[j, rows, :] += lanes(dr, j)
            hi_ref[j, rows, :] += lanes(di, j)
    return (jnp.concatenate([hr_ref[j] for j in range(nlb)], axis=1),
            jnp.concatenate([hi_ref[j] for j in range(nlb)], axis=1))


def _s5_states(u, lam_ref, b_ref, car_ref, hr_ref, hi_ref, cr_ref, ci_ref):
    lr, li = lam_ref[0], lam_ref[1]
    fold = _cmul(lr, li, car_ref[0:1, :], car_ref[1:2, :])
    return _scan_tile(_mm(u, b_ref[0]), _mm(u, b_ref[1]), lr, li, hr_ref, hi_ref, cr_ref, ci_ref, False, fold)


def _scratch_row(ref, row):
    return jnp.concatenate([ref[j, row:row + 1, :] for j in range(ref.shape[0])], axis=1)


def _s5_params_fn(ar, ai, ldt, br2, bi2):
    dt = jnp.exp(ldt)
    mag = jnp.exp(ar * dt)
    lr, li = mag * jnp.cos(ai * dt), mag * jnp.sin(ai * dt)
    den = ar * ar + ai * ai
    fr = ((lr - 1.0) * ar + li * ai) / den
    fi = (li * ar - (lr - 1.0) * ai) / den
    expand = jnp.where(_cols((NS, NS * GS)) // GS == _rows((NS, NS * GS)), 1.0, 0.0).astype(F32)
    fr2, fi2 = _hmm(fr, expand), _hmm(fi, expand)
    return lr, li, fr2 * br2 - fi2 * bi2, fr2 * bi2 + fi2 * br2


def _head_norm(o, hn):
    parts = []
    for h in range(NH):
        oh = o[:, h * DH:(h + 1) * DH]
        parts.append(oh * lax.rsqrt(jnp.mean(oh * oh, axis=-1, keepdims=True) + EPS) * hn)
    return jnp.concatenate(parts, axis=1)


def _mix_pre(s5y, u, dvec):
    return _gelu(s5y + dvec * u)


def _mix_mid(o, za, y0, gl, zb, ra, rb, hn):
    ya = _head_norm(o, hn) * _silu(za)
    yb = y0 * _sigmoid(gl) * _silu(zb)
    return _sigmoid(ra) * ya + _sigmoid(rb) * yb


def _mix_post(x, out, npost):
    return x + _rms(out, npost)


def _tile(t, want):
    return min(t, want)


def _row_tile(rows, want):
    return max(r for r in range(16, want + 1, 16) if rows % r == 0)


def _call_carrying(body, name, grid, in_specs, out_specs, out_shape, scratch, args, semantics, exchange):
    n_out = len(out_shape)
    if exchange is not None:
        body = _carry(body, len(args), n_out, len(scratch), exchange, grid)
        in_specs, out_specs = in_specs + exchange.in_specs, out_specs + exchange.out_specs
        out_shape, scratch, args = out_shape + exchange.out_shape, scratch + exchange.scratch_shapes, args + exchange.srcs
        semantics = ("arbitrary",) * len(grid)
    outs = pl.pallas_call(body, name=name, grid=grid, in_specs=in_specs, out_specs=out_specs, out_shape=out_shape,
                          scratch_shapes=scratch, compiler_params=_cparams(semantics))(*args)
    return outs[:n_out], outs[n_out:]


def _inproj_fwd(x, gain, wcat, l, exchange=None):
    t = x.shape[0]
    tm, tn = _tile(t, 1024), 1664

    def body(x_ref, g_ref, w_ref, o_ref, h_ref):
        @pl.when(pl.program_id(1) == 0)
        def _():
            h_ref[...] = _rms(x_ref[...], g_ref[...]).astype(h_ref.dtype)
        o_ref[...] = _dot_bf16(h_ref[...], w_ref[...], ((1,), (0,)))

    (proj, h), fetched = _call_carrying(
        body, "inproj_fwd", (t // tm, NCOL // tn),
        [pl.BlockSpec((tm, D), lambda i, j: (i, 0)), _full((1, D)), pl.BlockSpec((None, D, tn), lambda i, j: (l, 0, j))],
        [pl.BlockSpec((tm, tn), lambda i, j: (i, j)), pl.BlockSpec((tm, D), lambda i, j: (i, 0))],
        [jax.ShapeDtypeStruct((t, NCOL), F32), jax.ShapeDtypeStruct((t, D), wcat.dtype)],
        [], [x, gain, wcat], ("parallel", "arbitrary"), exchange)
    return proj, h, fetched


def _inproj_bwd_dx(dproj, wcat, x, gain, dxres, l, exchange=None):
    t = x.shape[0]
    tm, tk = _tile(t, 1024), 1664
    nk = NCOL // tk

    def body(dp_ref, w_ref, x_ref, g_ref, r_ref, dx_ref, dg_ref, acc_ref):
        i, k = pl.program_id(0), pl.program_id(1)

        @pl.when(k == 0)
        def _():
            acc_ref[...] = jnp.zeros_like(acc_ref)

        acc_ref[...] += _mm_nt(dp_ref[...], w_ref[...])

        @pl.when(k == nk - 1)
        def _():
            _, vjp = jax.vjp(_rms, x_ref[...], g_ref[...])
            dx, dg = vjp(acc_ref[...])
            dx_ref[...] = r_ref[...] + dx

            @pl.when(i == 0)
            def _():
                dg_ref[...] = dg

            @pl.when(i > 0)
            def _():
                dg_ref[...] += dg

    grid = (t // tm, nk)
    in_specs = [pl.BlockSpec((tm, tk), lambda i, k: (i, k)), pl.BlockSpec((None, D, tk), lambda i, k: (l, 0, k)),
                pl.BlockSpec((tm, D), lambda i, k: (i, 0)), _full((1, D)), pl.BlockSpec((tm, D), lambda i, k: (i, 0))]
    out_specs = [pl.BlockSpec((tm, D), lambda i, k: (i, 0)), _full((1, D))]
    out_shape = [jax.ShapeDtypeStruct((t, D), F32), jax.ShapeDtypeStruct((1, D), F32)]
    scratch, args = [pltpu.VMEM((tm, D), F32)], [dproj, wcat, x, gain, dxres]
    if exchange is not None:
        body = _carry(body, len(args), len(out_shape), len(scratch), exchange, grid)
        in_specs, out_specs = in_specs + exchange.in_specs, out_specs + exchange.out_specs
        out_shape, scratch, args = out_shape + exchange.out_shape, scratch + exchange.scratch_shapes, args + exchange.srcs
    outs = pl.pallas_call(
        body, name="inproj_bwd_dx", grid=grid, in_specs=in_specs, out_specs=out_specs, out_shape=out_shape,
        scratch_shapes=scratch, compiler_params=_cparams(("arbitrary", "arbitrary")),
    )(*args)
    return outs[0], outs[1], outs[2:]


def _inproj_bwd_dw(h, dproj):
    t = h.shape[0]
    tm, tn = _tile(t, 512), 1664

    def body(h_ref, dp_ref, o_ref):
        @pl.when(pl.program_id(1) == 0)
        def _():
            o_ref[...] = jnp.zeros_like(o_ref)

        o_ref[...] += _mm_tn(h_ref[...], dp_ref[...])

    return pl.pallas_call(
        body, name="inproj_bwd_dw", grid=(NCOL // tn, t // tm),
        in_specs=[pl.BlockSpec((tm, D), lambda j, i: (i, 0)), pl.BlockSpec((tm, tn), lambda j, i: (i, j))],
        out_specs=pl.BlockSpec((D, tn), lambda j, i: (0, j)),
        out_shape=jax.ShapeDtypeStruct((D, NCOL), F32),
        compiler_params=_cparams(("parallel", "arbitrary")),
    )(h, dproj)


def _prep_fwd(proj, cw):
    t = proj.shape[0]

    def body(p_ref, w_ref, o_ref):
        qk = (pl.program_id(0) < 2 * NH).astype(F32)
        o_ref[...] = _prep_fn(p_ref[...], w_ref[0:1, :], w_ref[1:2, :], w_ref[2:3, :], w_ref[3:4, :], qk)

    return pl.pallas_call(
        body, name="prep_fwd", grid=(3 * NH,),
        in_specs=[pl.BlockSpec((t, DH), lambda c: (0, c)), pl.BlockSpec((4, DH), lambda c: (0, c))],
        out_specs=pl.BlockSpec((None, t, DH), lambda c: (c, 0, 0)),
        out_shape=jax.ShapeDtypeStruct((3 * NH, t, DH), F32),
        compiler_params=_cparams(("parallel",)),
    )(proj, cw)


def _prep_bwd(proj, cw, dq, dk, dv):
    t = proj.shape[0]

    def body(p_ref, w_ref, dq_ref, dk_ref, dv_ref, dp_ref, dw_ref):
        c = pl.program_id(0)
        qk = (c < 2 * NH).astype(F32)
        _, vjp = jax.vjp(lambda p, w0, w1, w2, w3: _prep_fn(p, w0, w1, w2, w3, qk),
                         p_ref[...], w_ref[0:1, :], w_ref[1:2, :], w_ref[2:3, :], w_ref[3:4, :])
        d = jnp.where(c < NH, dq_ref[...], jnp.where(c < 2 * NH, dk_ref[...], dv_ref[...]))
        dp, dw0, dw1, dw2, dw3 = vjp(d)
        dp_ref[...] = dp.astype(dp_ref.dtype)
        dw_ref[0:1, :] = dw0
        dw_ref[1:2, :] = dw1
        dw_ref[2:3, :] = dw2
        dw_ref[3:4, :] = dw3

    return pl.pallas_call(
        body, name="prep_bwd", grid=(3 * NH,),
        in_specs=[pl.BlockSpec((t, DH), lambda c: (0, c)), pl.BlockSpec((4, DH), lambda c: (0, c))]
        + [pl.BlockSpec((None, t, DH), functools.partial(lambda c, off: (jnp.clip(c - off, 0, NH - 1), 0, 0), off=off))
           for off in (0, NH, 2 * NH)],
        out_specs=[pl.BlockSpec((t, DH), lambda c: (0, c)), pl.BlockSpec((4, DH), lambda c: (0, c))],
        out_shape=[jax.ShapeDtypeStruct((t, 3 * D), GRAD_ACT), jax.ShapeDtypeStruct((4, 3 * D), F32)],
        compiler_params=_cparams(("arbitrary",)),
    )(proj, cw, dq, dk, dv)


def _gates_fwd(proj, gvec):
    t = proj.shape[0]
    tm = _tile(t, 512)

    def body(p_ref, gv_ref, b_ref, g_ref):
        outs = _gates_fn(p_ref[...], gv_ref[0:1, :], gv_ref[1:2, :])
        for h in range(NH):
            b_ref[h] = outs[h]
            g_ref[h] = outs[NH + h]

    spec = pl.BlockSpec((NH, tm, DH), lambda i: (0, i, 0))
    return pl.pallas_call(
        body, name="gates_fwd", grid=(t // tm,),
        in_specs=[pl.BlockSpec((tm, DH), lambda i: (i, BD0 // DH)), _full((8, DH))],
        out_specs=[spec, spec],
        out_shape=[jax.ShapeDtypeStruct((NH, t, DH), F32)] * 2,
        compiler_params=_cparams(("parallel",)),
    )(proj, gvec)


def _gates_bwd(proj, gvec, dbb, dgcb):
    t = proj.shape[0]
    tm = _tile(t, 512)

    def body(p_ref, gv_ref, db_ref, dg_ref, dp_ref, dgv_ref):
        _, vjp = jax.vjp(_gates_fn, p_ref[...], gv_ref[0:1, :], gv_ref[1:2, :])
        cts = tuple(db_ref[h] for h in range(NH)) + tuple(dg_ref[h] for h in range(NH))
        dp, da, db = vjp(cts)
        dp_ref[...] = dp.astype(dp_ref.dtype)

        @pl.when(pl.program_id(0) == 0)
        def _():
            dgv_ref[...] = jnp.zeros_like(dgv_ref)

        dgv_ref[0:1, :] += da
        dgv_ref[1:2, :] += db

    spec = pl.BlockSpec((NH, tm, DH), lambda i: (0, i, 0))
    return pl.pallas_call(
        body, name="gates_bwd", grid=(t // tm,),
        in_specs=[pl.BlockSpec((tm, DH), lambda i: (i, BD0 // DH)), _full((8, DH)), spec, spec],
        out_specs=[pl.BlockSpec((tm, DH), lambda i: (i, 0)), _full((8, DH))],
        out_shape=[jax.ShapeDtypeStruct((t, DH), GRAD_ACT), jax.ShapeDtypeStruct((8, DH), F32)],
        compiler_params=_cparams(("arbitrary",)),
    )(proj, gvec, dbb, dgcb)


def _chunks_per_step(nch):
    return max(c for c in (8, 4, 2, 1) if nch % c == 0)


def _grid_ends(grid):
    def first():
        return functools.reduce(jnp.logical_and, [pl.program_id(a) == 0 for a in range(len(grid))])

    def last():
        return functools.reduce(jnp.logical_and, [pl.program_id(a) == n - 1 for a, n in enumerate(grid)])

    return first, last


def _carry(body, n_in, n_out, n_scratch, exchange, grid):
    first, last = _grid_ends(grid)
    na = exchange.narr

    def wrapped(*refs):
        a, b = n_in, n_in + na
        c, d = b + n_out, b + n_out + na
        e = d + n_scratch
        srcs, dsts, sems = refs[a:b], refs[c:d], refs[e:]

        @pl.when(first())
        def _():
            exchange.start(srcs, dsts, sems)

        body(*(refs[:a] + refs[b:c] + refs[d:e]))

        @pl.when(last())
        def _():
            exchange.wait(srcs, dsts, sems)

    return wrapped


def _delta_local_fwd(qkv, bb, gcb, exchange=None):
    t = qkv.shape[1]
    cps = _chunks_per_step(t // CH)
    rows = cps * CH
    grid = (NH, t // rows)

    def body(q_ref, k_ref, v_ref, b_ref, g_ref, *out_refs):
        slices = [slice(c * CH, (c + 1) * CH) for c in range(cps)]
        results = _chunks_local([tuple(ref[sl, :] for ref in (q_ref, k_ref, v_ref, b_ref, g_ref)) for sl in slices])
        for sl, (outs, t_inv) in zip(slices, results):
            for ref, val in zip(out_refs, outs + (t_inv,)):
                ref[sl, :] = val.astype(ref.dtype)

    def blk(off):
        return pl.BlockSpec((None, rows, DH), lambda h, n: (h + off, n, 0))

    in_specs = [blk(0), blk(NH), blk(2 * NH), blk(0), blk(0)]
    out_specs = [blk(0)] * 6
    out_shape = [jax.ShapeDtypeStruct((NH, t, DH), dt) for dt in (F32, BF16, BF16, BF16, BF16, F32)]
    args, scratch, sem = [qkv, qkv, qkv, bb, gcb], [], ("parallel", "parallel")
    if exchange is not None:
        body = _carry(body, 5, 6, 0, exchange, grid)
        in_specs, out_specs = in_specs + exchange.in_specs, out_specs + exchange.out_specs
        out_shape, scratch, args = out_shape + exchange.out_shape, exchange.scratch_shapes, args + exchange.srcs
        sem = ("arbitrary", "arbitrary")
    outs = pl.pallas_call(
        body, name="delta_local_fwd", grid=grid, in_specs=in_specs, out_specs=out_specs, out_shape=out_shape,
        scratch_shapes=scratch, compiler_params=_cparams(sem),
    )(*args)
    return outs[:5], outs[5], outs[6:]


def _delta_local_bwd(qkv, bb, gcb, t_inv, cts, dgcb_state):
    t = qkv.shape[1]
    cps = _chunks_per_step(t // CH)
    rows = cps * CH

    def body(q_ref, k_ref, v_ref, b_ref, g_ref, ti_ref, du_ref, dw_ref, dqd_ref, dkd_ref, da_ref, dgs_ref,
             dq_ref, dk_ref, dv_ref, db_ref, dg_ref):
        slices = [slice(c * CH, (c + 1) * CH) for c in range(cps)]
        items = [tuple(ref[sl, :] for ref in (q_ref, k_ref, v_ref, b_ref, g_ref, ti_ref)) for sl in slices]
        cts = [tuple(ref[sl, :] for ref in (du_ref, dw_ref, dqd_ref, dkd_ref, da_ref)) for sl in slices]
        for sl, (dq, dk, dv, db, dg, _) in zip(slices, _side_by_side_vjp(_chunks_local_known, items, cts)):
            dq_ref[sl, :] = dq
            dk_ref[sl, :] = dk
            dv_ref[sl, :] = dv
            db_ref[sl, :] = db
            dg_ref[sl, :] = dg + dgs_ref[sl, :]

    def blk(off):
        return pl.BlockSpec((None, rows, DH), lambda h, n: (h + off, n, 0))

    return pl.pallas_call(
        body, name="delta_local_bwd", grid=(NH, t // rows),
        in_specs=[blk(0), blk(NH), blk(2 * NH)] + [blk(0)] * 9,
        out_specs=[blk(0)] * 5,
        out_shape=[jax.ShapeDtypeStruct((NH, t, DH), F32)] * 5,
        compiler_params=_cparams(("parallel", "parallel")),
    )(qkv, qkv, qkv, bb, gcb, t_inv, *cts, dgcb_state)


def _delta_state_fwd(local, gcb):
    t = gcb.shape[1]
    nch = t // CH

    def body(u_ref, w_ref, qd_ref, kd_ref, a_ref, g_ref, o_ref, s_ref, st_ref):
        @pl.when(pl.program_id(0) == 0)
        def _():
            st_ref[...] = jnp.zeros_like(st_ref)

        s_ref[...] = st_ref[...]
        items = [tuple(ref[h].astype(F32) for ref in (u_ref, w_ref, qd_ref, kd_ref, a_ref, g_ref, st_ref))
                 for h in range(NH)]
        for h, (o, ns) in enumerate(_state_steps(items)):
            o_ref[:, h * DH:(h + 1) * DH] = o
            st_ref[h] = ns

    blk = pl.BlockSpec((NH, CH, DH), lambda n: (0, n, 0))
    return pl.pallas_call(
        body, name="delta_state_fwd", grid=(nch,),
        in_specs=[blk] * 6,
        out_specs=[pl.BlockSpec((CH, D), lambda n: (n, 0)),
                   pl.BlockSpec((NH, None, DH, DH), lambda n: (0, n, 0, 0))],
        out_shape=[jax.ShapeDtypeStruct((t, D), F32), jax.ShapeDtypeStruct((NH, nch, DH, DH), F32)],
        scratch_shapes=[pltpu.VMEM((NH, DH, DH), F32)],
        compiler_params=_cparams(("arbitrary",)),
    )(*local, gcb)


def _delta_state_bwd(local, gcb, states, do):
    t = gcb.shape[1]
    nch = t // CH

    def body(u_ref, w_ref, qd_ref, kd_ref, a_ref, g_ref, s_ref, do_ref,
             du_ref, dw_ref, dqd_ref, dkd_ref, da_ref, dg_ref, ds_ref):
        @pl.when(pl.program_id(0) == 0)
        def _():
            ds_ref[...] = jnp.zeros_like(ds_ref)

        items = [tuple(ref[h].astype(F32) for ref in (u_ref, w_ref, qd_ref, kd_ref, a_ref, g_ref, s_ref))
                 for h in range(NH)]
        cts = [(do_ref[:, h * DH:(h + 1) * DH], ds_ref[h]) for h in range(NH)]
        for h, (du, dw, dqd, dkd, da, dg, ds) in enumerate(_side_by_side_vjp(_state_steps, items, cts)):
            du_ref[h] = du
            dw_ref[h] = dw
            dqd_ref[h] = dqd
            dkd_ref[h] = dkd
            da_ref[h] = da
            dg_ref[h] = dg
            ds_ref[h] = ds

    blk = pl.BlockSpec((NH, CH, DH), lambda n: (0, nch - 1 - n, 0))
    return pl.pallas_call(
        body, name="delta_state_bwd", grid=(nch,),
        in_specs=[blk] * 6 + [pl.BlockSpec((NH, None, DH, DH), lambda n: (0, nch - 1 - n, 0, 0)),
                              pl.BlockSpec((CH, D), lambda n: (nch - 1 - n, 0))],
        out_specs=[blk] * 6,
        out_shape=[jax.ShapeDtypeStruct((NH, t, DH), F32)] * 6,
        scratch_shapes=[pltpu.VMEM((NH, DH, DH), F32)],
        compiler_params=_cparams(("arbitrary",)),
    )(*local, gcb, states, do)


def _s5_params(ar, ai, ldt, br2, bi2):
    def body(ar_ref, ai_ref, ld_ref, br_ref, bi_ref, lr_ref, li_ref, bbr_ref, bbi_ref):
        lr, li, bbr, bbi = _s5_params_fn(ar_ref[...], ai_ref[...], ld_ref[...], br_ref[...], bi_ref[...])
        lr_ref[...] = lr
        li_ref[...] = li
        bbr_ref[...] = bbr
        bbi_ref[...] = bbi

    sq = pl.BlockSpec((None, NG, NS), lambda l: (l, 0, 0))
    wide = pl.BlockSpec((None, NG, NS * GS), lambda l: (l, 0, 0))
    return pl.pallas_call(
        body, name="s5_params", grid=(DEPTH,),
        in_specs=[sq, sq, pl.BlockSpec((None, NG, 1), lambda l: (l, 0, 0)), wide, wide],
        out_specs=[sq, sq, wide, wide],
        out_shape=[jax.ShapeDtypeStruct((DEPTH, NG, NS), F32)] * 2
        + [jax.ShapeDtypeStruct((DEPTH, NG, NS * GS), F32)] * 2,
        compiler_params=_cparams(("parallel",)),
    )(ar, ai, ldt, br2, bi2)


def _s5_params_bwd(ar, ai, ldt, br2, bi2, dlr, dli, dbbr, dbbi):
    def body(ar_ref, ai_ref, ld_ref, br_ref, bi_ref, a_ref, b_ref, c_ref, d_ref,
             dar_ref, dai_ref, dld_ref, dbr_ref, dbi_ref):
        _, vjp = jax.vjp(_s5_params_fn, ar_ref[...], ai_ref[...], ld_ref[...], br_ref[...], bi_ref[...])
        dar, dai, dld, dbr, dbi = vjp((a_ref[...], b_ref[...], c_ref[...], d_ref[...]))
        dar_ref[...] = dar
        dai_ref[...] = dai
        dld_ref[...] = dld
        dbr_ref[...] = dbr
        dbi_ref[...] = dbi

    sq = pl.BlockSpec((None, NG, NS), lambda l: (l, 0, 0))
    col = pl.BlockSpec((None, NG, 1), lambda l: (l, 0, 0))
    wide = pl.BlockSpec((None, NG, NS * GS), lambda l: (l, 0, 0))
    return pl.pallas_call(
        body, name="s5_params_bwd", grid=(DEPTH,),
        in_specs=[sq, sq, col, wide, wide, sq, sq, wide, wide],
        out_specs=[sq, sq, col, wide, wide],
        out_shape=[jax.ShapeDtypeStruct((DEPTH, NG, NS), F32)] * 2 + [jax.ShapeDtypeStruct((DEPTH, NG, 1), F32)]
        + [jax.ShapeDtypeStruct((DEPTH, NG, NS * GS), F32)] * 2,
        compiler_params=_cparams(("parallel",)),
    )(ar, ai, ldt, br2, bi2, dlr, dli, dbbr, dbbi)


def _s5_tile_rows(t):
    return _tile(t // 2, 1024)


def _s5_fwd(proj, lam, bblk, cblk, exchange=None):
    t = proj.shape[0]
    r = _s5_tile_rows(t)
    nt = t // r
    u0 = 4 * D // DH

    def body(u_ref, lam_ref, b_ref, c_ref, y_ref, car_ref, st_ref, hr_ref, hi_ref, cr_ref, ci_ref):
        @pl.when(pl.program_id(1) == 0)
        def _():
            st_ref[...] = jnp.zeros_like(st_ref)

        car_ref[...] = st_ref[...]
        hr, hi = _s5_states(u_ref[...], lam_ref, b_ref, st_ref, hr_ref, hi_ref, cr_ref, ci_ref)
        y_ref[...] = _mm(hr, c_ref[0]) - _mm(hi, c_ref[1])
        st_ref[0:1, :] = _scratch_row(hr_ref, r - 1)
        st_ref[1:2, :] = _scratch_row(hi_ref, r - 1)

    scratch = ([pltpu.VMEM((8, SW), F32)] + [pltpu.VMEM((SW // DH, r, DH), F32)] * 2
               + [pltpu.VMEM((r // SUB, SW), F32)] * 2)
    (y, carries), fetched = _call_carrying(
        body, "s5_fwd", (NCB, nt),
        [pl.BlockSpec((r, DH), lambda c, i: (i, u0 + c)), pl.BlockSpec((2, 1, SW), lambda c, i: (0, 0, c)),
         pl.BlockSpec((2, None, DH, SW), lambda c, i: (0, c, 0, 0)),
         pl.BlockSpec((2, None, SW, DH), lambda c, i: (0, c, 0, 0))],
        [pl.BlockSpec((r, DH), lambda c, i: (i, c)), pl.BlockSpec((None, 8, SW), lambda c, i: (i, 0, c))],
        [jax.ShapeDtypeStruct((t, D), F32), jax.ShapeDtypeStruct((nt, 8, NG * NS), F32)],
        scratch, [proj, lam, bblk, cblk], ("parallel", "arbitrary"), exchange)
    return y, carries, fetched


def _s5_bwd(proj, lam, bblk, cblk, carries, dy, du_skip, exchange=None):
    t = proj.shape[0]
    r = _s5_tile_rows(t)
    nt = t // r
    u0 = 4 * D // DH

    def body(u_ref, lam_ref, b_ref, c_ref, car_ref, dy_ref, dus_ref, du_ref, dlam_ref, db_ref, dc_ref, dst_ref,
             hr_ref, hi_ref, ar_ref, ai_ref, cr_ref, ci_ref):
        first = pl.program_id(1) == 0

        @pl.when(first)
        def _():
            dst_ref[...] = jnp.zeros_like(dst_ref)

        u, dy = u_ref[...], dy_ref[...]
        lr, li = lam_ref[0], lam_ref[1]
        hr, hi = _s5_states(u, lam_ref, b_ref, car_ref, hr_ref, hi_ref, cr_ref, ci_ref)
        dcr2, dci2 = _mm_tn(hr, dy), -_mm_tn(hi, dy)
        fold = _cmul(lr, -li, dst_ref[0:1, :], dst_ref[1:2, :])
        ar, ai = _scan_tile(_mm_nt(dy, c_ref[0]), -_mm_nt(dy, c_ref[1]), lr, -li, ar_ref, ai_ref, cr_ref, ci_ref,
                            True, fold)
        top = _rows((r, SW)) == 0
        dst_ref[0:1, :] = _scratch_row(ar_ref, 0)
        dst_ref[1:2, :] = _scratch_row(ai_ref, 0)
        du_ref[...] = (_mm_nt(ar, b_ref[0]) + _mm_nt(ai, b_ref[1]) + dus_ref[...]).astype(du_ref.dtype)
        dbr, dbi = _mm_tn(u, ar), _mm_tn(u, ai)
        pr = _sd(hr, 1) + jnp.where(top, car_ref[0:1, :], 0.0)
        pi = _sd(hi, 1) + jnp.where(top, car_ref[1:2, :], 0.0)
        dlr = jnp.sum(ar * pr + ai * pi, axis=0, keepdims=True)
        dli = jnp.sum(ai * pr - ar * pi, axis=0, keepdims=True)

        @pl.when(first)
        def _():
            dlam_ref[0] = dlr
            dlam_ref[1] = dli
            db_ref[0] = dbr
            db_ref[1] = dbi
            dc_ref[0] = dcr2
            dc_ref[1] = dci2

        @pl.when(jnp.logical_not(first))
        def _():
            dlam_ref[0] += dlr
            dlam_ref[1] += dli
            db_ref[0] += dbr
            db_ref[1] += dbi
            dc_ref[0] += dcr2
            dc_ref[1] += dci2

    grid = (NCB, nt)
    in_specs = [pl.BlockSpec((r, DH), lambda c, i: (nt - 1 - i, u0 + c)),
                pl.BlockSpec((2, 1, SW), lambda c, i: (0, 0, c)),
                pl.BlockSpec((2, None, DH, SW), lambda c, i: (0, c, 0, 0)),
                pl.BlockSpec((2, None, SW, DH), lambda c, i: (0, c, 0, 0)),
                pl.BlockSpec((None, 8, SW), lambda c, i: (nt - 1 - i, 0, c)),
                pl.BlockSpec((r, DH), lambda c, i: (nt - 1 - i, c)),
                pl.BlockSpec((r, DH), lambda c, i: (nt - 1 - i, c))]
    out_specs = [pl.BlockSpec((r, DH), lambda c, i: (nt - 1 - i, c)),
                 pl.BlockSpec((2, 1, SW), lambda c, i: (0, 0, c)),
                 pl.BlockSpec((2, None, DH, SW), lambda c, i: (0, c, 0, 0)),
                 pl.BlockSpec((2, None, SW, DH), lambda c, i: (0, c, 0, 0))]
    out_shape = [jax.ShapeDtypeStruct((t, D), GRAD_ACT), jax.ShapeDtypeStruct((2, 1, NG * NS), F32),
                 jax.ShapeDtypeStruct((2, NCB, DH, SW), F32), jax.ShapeDtypeStruct((2, NCB, SW, DH), F32)]
    scratch = ([pltpu.VMEM((8, SW), F32)] + [pltpu.VMEM((SW // DH, r, DH), F32)] * 4
               + [pltpu.VMEM((r // SUB, SW), F32)] * 2)
    args, sem = [proj, lam, bblk, cblk, carries, dy, du_skip], ("parallel", "arbitrary")
    if exchange is not None:
        body = _carry(body, len(args), len(out_shape), len(scratch), exchange, grid)
        in_specs, out_specs = in_specs + exchange.in_specs, out_specs + exchange.out_specs
        out_shape, scratch, args = out_shape + exchange.out_shape, scratch + exchange.scratch_shapes, args + exchange.srcs
        sem = ("arbitrary", "arbitrary")
    outs = pl.pallas_call(
        body, name="s5_bwd", grid=grid, in_specs=in_specs, out_specs=out_specs, out_shape=out_shape,
        scratch_shapes=scratch, compiler_params=_cparams(sem),
    )(*args)
    return outs[:4], outs[4:]


def _proj_spec(tm, col):
    return pl.BlockSpec((tm, D), lambda i: (i, col))


def _layer_mat(l):
    return pl.BlockSpec((None, D, D), lambda i: (l, 0, 0))


def _mix_fwd(proj, o, s5y, x, hn, dvec, wglu, bglu, wout, npost, l):
    t = x.shape[0]
    tm = _tile(t, 256)

    def body(za_ref, u_ref, zb_ref, ra_ref, rb_ref, o_ref, y_ref, x_ref, hn_ref, d_ref, wg_ref, bg_ref, wo_ref,
             np_ref, xn_ref):
        y0 = _mix_pre(y_ref[...], u_ref[...], d_ref[...])
        gl = _mm(y0, wg_ref[...]) + bg_ref[...]
        m = _mix_mid(o_ref[...], za_ref[...], y0, gl, zb_ref[...], ra_ref[...], rb_ref[...], hn_ref[...])
        out = _mm(m, wo_ref[...])
        xn_ref[...] = _mix_post(x_ref[...], out, np_ref[...])

    act = pl.BlockSpec((tm, D), lambda i: (i, 0))
    return pl.pallas_call(
        body, name="mix_fwd", grid=(t // tm,),
        in_specs=[_proj_spec(tm, 3), _proj_spec(tm, 4), _proj_spec(tm, 5), _proj_spec(tm, 6), _proj_spec(tm, 7),
                  act, act, act, _full((1, DH)), _full((1, D)), _layer_mat(l), _full((1, D)), _layer_mat(l),
                  _full((1, D))],
        out_specs=act,
        out_shape=jax.ShapeDtypeStruct((t, D), F32),
        compiler_params=_cparams(("parallel",)),
    )(proj, proj, proj, proj, proj, o, s5y, x, hn, dvec, wglu, bglu, wout, npost)


def _mix_bwd(proj, o, s5y, x, hn, dvec, wglu, bglu, wout, npost, dxn, l):
    t = x.shape[0]
    tm = _tile(t, 128)

    def body(za_ref, u_ref, zb_ref, ra_ref, rb_ref, o_ref, y_ref, x_ref, hn_ref, d_ref, wg_ref, bg_ref, wo_ref,
             np_ref, dxn_ref,
             dza_ref, du_ref, dzb_ref, dra_ref, drb_ref, do_ref, dy_ref, dx_ref,
             y0_ref, dgl_ref, m_ref, dout_ref, dvecs_ref, dhn_ref):
        y0, vjp_pre = jax.vjp(_mix_pre, y_ref[...], u_ref[...], d_ref[...])
        gl = _mm(y0, wg_ref[...]) + bg_ref[...]
        m, vjp_mid = jax.vjp(_mix_mid, o_ref[...], za_ref[...], y0, gl, zb_ref[...], ra_ref[...], rb_ref[...],
                             hn_ref[...])
        out = _mm(m, wo_ref[...])
        _, vjp_post = jax.vjp(_mix_post, x_ref[...], out, np_ref[...])
        dx, dout, dnp = vjp_post(dxn_ref[...])
        dm = _mm_nt(dout, wo_ref[...])
        do, dza, dy0, dgl, dzb, dra, drb, dhn = vjp_mid(dm)
        y0_ref[...] = y0.astype(BF16)
        dgl_ref[...] = dgl.astype(BF16)
        m_ref[...] = m.astype(BF16)
        dout_ref[...] = dout.astype(BF16)
        dbg = jnp.sum(dgl, axis=0, keepdims=True)
        dy0 = dy0 + _mm_nt(dgl, wg_ref[...])
        dy, du, dd = vjp_pre(dy0)
        dza_ref[...] = dza.astype(dza_ref.dtype)
        du_ref[...] = du
        dzb_ref[...] = dzb.astype(dzb_ref.dtype)
        dra_ref[...] = dra.astype(dra_ref.dtype)
        drb_ref[...] = drb.astype(drb_ref.dtype)
        do_ref[...] = do
        dy_ref[...] = dy
        dx_ref[...] = dx
        first = pl.program_id(0) == 0

        @pl.when(first)
        def _():
            dvecs_ref[...] = jnp.zeros_like(dvecs_ref)
            dhn_ref[...] = jnp.zeros_like(dhn_ref)

        dvecs_ref[0:1, :] += dd
        dvecs_ref[1:2, :] += dbg
        dvecs_ref[2:3, :] += dnp
        dhn_ref[0:1, :] += dhn

    act = pl.BlockSpec((tm, D), lambda i: (i, 0))
    a, ga = jax.ShapeDtypeStruct((t, D), F32), jax.ShapeDtypeStruct((t, D), GRAD_ACT)
    b16 = jax.ShapeDtypeStruct((t, D), BF16)
    outs = pl.pallas_call(
        body, name="mix_bwd", grid=(t // tm,),
        in_specs=[_proj_spec(tm, 3), _proj_spec(tm, 4), _proj_spec(tm, 5), _proj_spec(tm, 6), _proj_spec(tm, 7),
                  act, act, act, _full((1, DH)), _full((1, D)), _layer_mat(l), _full((1, D)), _layer_mat(l),
                  _full((1, D)), act],
        out_specs=[act] * 12 + [_full((8, D)), _full((8, DH))],
        out_shape=[ga, a, ga, ga, ga, a, a, a, b16, b16, b16, b16, jax.ShapeDtypeStruct((8, D), F32),
                   jax.ShapeDtypeStruct((8, DH), F32)],
        compiler_params=_cparams(("arbitrary",)),
    )(proj, proj, proj, proj, proj, o, s5y, x, hn, dvec, wglu, bglu, wout, npost, dxn)
    y0, dgl, m, dout = outs[8:12]
    return list(outs[:8]) + [_weight_grad(y0, dgl, "glu_dw"), _weight_grad(m, dout, "out_dw")] + list(outs[12:])


def _weight_grad(a, b, name):
    t = a.shape[0]
    tk = _tile(t, 1024)

    def body(a_ref, b_ref, o_ref):
        @pl.when(pl.program_id(0) == 0)
        def _():
            o_ref[...] = jnp.zeros_like(o_ref)

        o_ref[...] += _mm_tn(a_ref[...], b_ref[...])

    rows = pl.BlockSpec((tk, D), lambda i: (i, 0))
    return pl.pallas_call(
        body, name=name, grid=(t // tk,), in_specs=[rows, rows], out_specs=_full((D, D)),
        out_shape=jax.ShapeDtypeStruct((D, D), F32), compiler_params=_cparams(("arbitrary",)),
    )(a, b)


def _loss_grad(y, target):
    t = y.shape[0]
    tm = _tile(t, 512)

    def body(y_ref, t_ref, dy_ref, l_ref):
        err = y_ref[...] - t_ref[...]
        dy_ref[...] = err * (1.0 / D)
        part = jnp.sum(jnp.sum(err * err, axis=1, keepdims=True), axis=0, keepdims=True) * (0.5 / D)
        part = jnp.broadcast_to(part, (8, DH))

        @pl.when(pl.program_id(0) == 0)
        def _():
            l_ref[...] = part

        @pl.when(pl.program_id(0) > 0)
        def _():
            l_ref[...] += part

    act = pl.BlockSpec((tm, D), lambda i: (i, 0))
    return pl.pallas_call(
        body, name="loss_grad", grid=(t // tm,),
        in_specs=[act, act], out_specs=[act, _full((8, DH))],
        out_shape=[jax.ShapeDtypeStruct((t, D), F32), jax.ShapeDtypeStruct((8, DH), F32)],
        compiler_params=_cparams(("arbitrary",)),
    )(y, target)


def _flips(rel):
    x, y, c = lax.axis_index("x"), lax.axis_index("y"), lax.axis_index("c")
    fx, fy, fc = rel
    return (x ^ fx if fx else x, y ^ fy if fy else y, c ^ fc if fc else c)


CHIP_RELS = ((1, 0, 0), (0, 1, 0), (1, 1, 0))
ALL_RELS = tuple((fx, fy, fc) for fx in (0, 1) for fy in (0, 1) for fc in (0, 1) if (fx, fy, fc) != (0, 0, 0))


def _slot_of(pos, by_chip):
    px, py, pc = pos
    return 2 * px + py if by_chip else 4 * px + 2 * py + pc


class _Exchange:
    def __init__(self, srcs, rels, by_chip, scatter):
        self.srcs, self.rels, self.by_chip, self.scatter = list(srcs), rels, by_chip, scatter
        self.narr = len(self.srcs)
        nslot, nsem = NCHIP if by_chip else NDEV, self.narr * len(rels)
        self.in_specs = [pl.BlockSpec(memory_space=pl.ANY)] * self.narr
        self.out_specs = [pl.BlockSpec(memory_space=pl.ANY)] * self.narr
        self.out_shape = [jax.ShapeDtypeStruct((nslot,) + s.shape[-2:], s.dtype) for s in self.srcs]
        self.scratch_shapes = [pltpu.SemaphoreType.DMA((nsem,)), pltpu.SemaphoreType.DMA((nsem,)),
                               pltpu.SemaphoreType.DMA((self.narr,))]

    def _copies(self, src_refs, dst_refs, sems):
        send_sems, recv_sems, local_sems = sems
        my_slot = _slot_of(_flips((0, 0, 0)), self.by_chip)
        local, sends, arrivals = [], [], []
        for a, (src_ref, dst_ref) in enumerate(zip(src_refs, dst_refs)):
            local.append(pltpu.make_async_copy(src_ref.at[my_slot] if self.scatter else src_ref, dst_ref.at[my_slot],
                                               local_sems.at[a]))
            for k, rel in enumerate(self.rels):
                peer = _flips(rel)
                pair = dict(send_sem=send_sems.at[a * len(self.rels) + k], recv_sem=recv_sems.at[a * len(self.rels) + k],
                            device_id=peer, device_id_type=pl.DeviceIdType.MESH)
                part = src_ref.at[_slot_of(peer, self.by_chip)] if self.scatter else src_ref
                sends.append(pltpu.make_async_remote_copy(src_ref=part, dst_ref=dst_ref.at[my_slot], **pair))
                arrivals.append(pltpu.make_async_remote_copy(
                    src_ref=src_ref.at[0] if self.scatter else src_ref, dst_ref=dst_ref.at[_slot_of(peer, self.by_chip)],
                    **pair))
        return local, sends, arrivals

    def start(self, src_refs, dst_refs, sems):
        local, sends, _ = self._copies(src_refs, dst_refs, sems)
        for cp in local + sends:
            cp.start()

    def wait(self, src_refs, dst_refs, sems):
        local, sends, arrivals = self._copies(src_refs, dst_refs, sems)
        for cp in arrivals:
            cp.wait_recv()
        for cp in sends:
            cp.wait_send()
        for cp in local:
            cp.wait()


def _exchange(srcs, rels, by_chip, scatter, name):
    ex = _Exchange(srcs, rels, by_chip, scatter)

    def body(*refs):
        parts = refs[:ex.narr], refs[ex.narr:2 * ex.narr], refs[2 * ex.narr:]
        ex.start(*parts)
        ex.wait(*parts)

    return pl.pallas_call(body, name=name, in_specs=ex.in_specs, out_specs=ex.out_specs, out_shape=ex.out_shape,
                          scratch_shapes=ex.scratch_shapes)(*ex.srcs)


def _sibling_swap(srcs, name):
    narr = len(srcs)

    def body(*refs):
        src_refs, dst_refs = refs[:narr], refs[narr:2 * narr]
        send_sems, recv_sems = refs[2 * narr:]
        peer = _flips((0, 0, 1))
        copies = [pltpu.make_async_remote_copy(src_ref=s, dst_ref=d, send_sem=send_sems.at[a], recv_sem=recv_sems.at[a],
                                               device_id=peer, device_id_type=pl.DeviceIdType.MESH)
                  for a, (s, d) in enumerate(zip(src_refs, dst_refs))]
        for cp in copies:
            cp.start()
        for cp in copies:
            cp.wait()

    return pl.pallas_call(
        body, name=name,
        in_specs=[pl.BlockSpec(memory_space=pl.ANY)] * narr,
        out_specs=[pl.BlockSpec(memory_space=pl.ANY)] * narr,
        out_shape=[jax.ShapeDtypeStruct(s.shape, s.dtype) for s in srcs],
        scratch_shapes=[pltpu.SemaphoreType.DMA((narr,)), pltpu.SemaphoreType.DMA((narr,))],
    )(*srcs)


def _all_reduce(src, name):
    rows, cols = src.shape
    r = rows // NDEV
    nrel = len(ALL_RELS)

    def body(src_ref, out_ref, parts_ref, mine_ref, send_sems, recv_sems):
        my_slot = _slot_of(_flips((0, 0, 0)), False)

        def piece(ref, slot):
            return ref.at[pl.ds(pl.multiple_of(slot * r, 8), r), :]

        def copies(phase):
            out = []
            for k, rel in enumerate(ALL_RELS):
                peer = _flips(rel)
                pair = dict(send_sem=send_sems.at[phase * nrel + k], recv_sem=recv_sems.at[phase * nrel + k],
                            device_id=peer, device_id_type=pl.DeviceIdType.MESH)
                if phase == 0:
                    out.append(pltpu.make_async_remote_copy(src_ref=piece(src_ref, _slot_of(peer, False)),
                                                            dst_ref=parts_ref.at[my_slot], **pair))
                else:
                    out.append(pltpu.make_async_remote_copy(src_ref=mine_ref, dst_ref=piece(out_ref, my_slot), **pair))
            return out

        first = copies(0)
        for cp in first:
            cp.start()
        parts_ref[my_slot] = piece(src_ref, my_slot)[...]
        for cp in first:
            cp.wait_recv()
        acc = parts_ref[0]
        for s in range(1, NDEV):
            acc = acc + parts_ref[s]
        mine_ref[...] = acc
        second = copies(1)
        for cp in second:
            cp.start()
        piece(out_ref, my_slot)[...] = acc
        for cp in second:
            cp.wait_recv()
        for cp in first + second:
            cp.wait_send()

    return pl.pallas_call(
        body, name=name,
        in_specs=[pl.BlockSpec(memory_space=pltpu.VMEM)], out_specs=pl.BlockSpec(memory_space=pltpu.VMEM),
        out_shape=jax.ShapeDtypeStruct(src.shape, src.dtype),
        scratch_shapes=[pltpu.VMEM((NDEV, r, cols), src.dtype), pltpu.VMEM((r, cols), src.dtype),
                        pltpu.SemaphoreType.DMA((2 * nrel,)), pltpu.SemaphoreType.DMA((2 * nrel,))],
        compiler_params=pltpu.CompilerParams(vmem_limit_bytes=VMEM_LIMIT),
    )(src)


def _sum_slots(parts, name):
    ns, rows, cols = parts.shape
    tr = _row_tile(rows, 256)

    def body(p_ref, o_ref):
        acc = p_ref[0].astype(F32)
        for s in range(1, ns):
            acc = acc + p_ref[s].astype(F32)
        o_ref[...] = acc

    return pl.pallas_call(
        body, name=name, grid=(rows // tr,),
        in_specs=[pl.BlockSpec((ns, tr, cols), lambda i: (0, i, 0))],
        out_specs=pl.BlockSpec((tr, cols), lambda i: (i, 0)),
        out_shape=jax.ShapeDtypeStruct((rows, cols), F32),
        compiler_params=_cparams(("parallel",)),
    )(parts)


def _adamw(w, g_parts, m, v, name, max_rows=256):
    if w.ndim == 2:
        return [o[0] for o in _adamw(w[None], g_parts, m[None], v[None], name, max_rows)]
    nl, rows, cols = w.shape
    tr = _row_tile(rows, max_rows)
    per_layer = rows // tr
    c1 = 1.0 / (1.0 - ADAM_B1 ** ADAM_STEP)
    c2 = 1.0 / (1.0 - ADAM_B2 ** ADAM_STEP)
    npart = len(g_parts)

    def body(*refs):
        w_ref, m_ref, v_ref = refs[:3]
        g_refs = refs[3:3 + npart]
        go_ref, d_ref, nm_ref, nv_ref = refs[3 + npart:]
        terms = []
        for g_ref in g_refs:
            terms += [g_ref[...]] if len(g_ref.shape) == 2 else [g_ref[s] for s in range(g_ref.shape[0])]
        g = terms[0]
        for term in terms[1:]:
            g = g + term
        nm = ADAM_B1 * m_ref[...] + (1.0 - ADAM_B1) * g
        nv = ADAM_B2 * v_ref[...] + (1.0 - ADAM_B2) * (g * g)
        d_ref[...] = -ADAM_LR * ((nm * c1) / (jnp.sqrt(nv * c2) + ADAM_EPS) + ADAM_WD * w_ref[...])
        go_ref[...] = g
        nm_ref[...] = nm
        nv_ref[...] = nv

    blk = pl.BlockSpec((None, tr, cols), lambda l, i: (l, i, 0))
    g_specs = [pl.BlockSpec((tr, cols), lambda l, i: (l * per_layer + i, 0)) if p.ndim == 2 else
               pl.BlockSpec((p.shape[0], tr, cols), lambda l, i: (0, l * per_layer + i, 0)) for p in g_parts]
    out = jax.ShapeDtypeStruct((nl, rows, cols), F32)
    return pl.pallas_call(
        body, name=name, grid=(nl, per_layer),
        in_specs=[blk, blk, blk] + g_specs,
        out_specs=[blk] * 4, out_shape=[out] * 4,
        compiler_params=_cparams(("parallel", "parallel")),
    )(w, m, v, *g_parts)


WEIGHT_SPLIT = (0, 384, 704, D)
WIN_SHARD = 2052
CONV_SHARD = 768
ROW_SHARD = 256

SMALL = (("norm_pre", (DEPTH, D)), ("a_log", (DEPTH, NH)), ("dt_bias", (DEPTH, NH)), ("head_norm", (DEPTH, DH)),
         ("ssm_a_re", (DEPTH, NG, NS)), ("ssm_a_im", (DEPTH, NG, NS)), ("ssm_log_dt", (DEPTH, NG)),
         ("ssm_b_re", (DEPTH, NG, NS, GS)), ("ssm_b_im", (DEPTH, NG, NS, GS)),
         ("ssm_c_re", (DEPTH, NG, GS, NS)), ("ssm_c_im", (DEPTH, NG, GS, NS)), ("ssm_d", (DEPTH, D)),
         ("b_glu", (DEPTH, D)), ("norm_post", (DEPTH, D)))


def _pad_rows(flat, rows):
    return jnp.pad(flat, (0, rows * D - flat.shape[0])).reshape(rows, D)


def _cols_from_chips(a, nl):
    _, rows, cols = a.shape
    return a.reshape(NCHIP, nl, rows // nl, cols).transpose(1, 2, 0, 3).reshape(nl, rows // nl, NCHIP * cols)


SMALL_ROWS = sum(-(-math.prod(s) // (8 * D)) * 8 for _, s in SMALL)
CONV_ROWS = DEPTH * 4 * 3 * D // D


def _pack_small(vals, extra=()):
    parts = []
    for val in tuple(vals) + tuple(extra):
        n = val.size
        parts.append(_pad_rows(val.reshape(-1), -(-n // (8 * D)) * 8))
    return jnp.concatenate(parts, axis=0)


def _unpack_small(flat):
    outs, r0 = [], 0
    for _, shape in SMALL:
        n = math.prod(shape)
        rows = -(-n // (8 * D)) * 8
        outs.append(flat[r0:r0 + rows].reshape(-1)[:n].reshape(shape))
        r0 += rows
    return outs


def _rearrange_cols(w):
    pad = jnp.zeros(w.shape[:-1] + (NCOL - BD0 - 2 * NH,), w.dtype)
    return jnp.concatenate([w[..., :4 * D], w[..., 4 * D + 2 * NH:], w[..., 4 * D:4 * D + 2 * NH], pad], axis=-1)


def _restore_cols(w):
    return jnp.concatenate([w[..., :4 * D], w[..., BD0:BD0 + 2 * NH], w[..., 4 * D:BD0]], axis=-1)


LOGITS_IN_CHIP1 = 2 * WIN_SHARD - 4 * D
LOGITS_IN_CHIP2 = 2 * NH - LOGITS_IN_CHIP1


def _wcat_from_chips(g):
    before = WIN_SHARD - LOGITS_IN_CHIP1
    pad = jnp.zeros((D, NCOL - BD0 - 2 * NH), g.dtype)
    return jnp.concatenate([g[0], g[1][:, :before], g[2][:, LOGITS_IN_CHIP2:], g[3], g[1][:, before:],
                            g[2][:, :LOGITS_IN_CHIP2], pad], axis=1)[None]


def _wcat_grad_by_chip(gw):
    before, mid = WIN_SHARD - LOGITS_IN_CHIP1, 4 * D + WIN_SHARD - LOGITS_IN_CHIP2
    return jnp.stack([gw[:, :WIN_SHARD],
                      jnp.concatenate([gw[:, WIN_SHARD:4 * D], gw[:, BD0:BD0 + LOGITS_IN_CHIP1]], axis=1),
                      jnp.concatenate([gw[:, BD0 + LOGITS_IN_CHIP1:BD0 + 2 * NH], gw[:, 4 * D:mid]], axis=1),
                      gw[:, mid:BD0]])


def _block_diag_b(bb2):
    b = bb2.reshape(-1, NCB, GPB, NS, GS)
    return jnp.einsum("lkgnc,gh->lkgchn", b, jnp.eye(GPB, dtype=F32)).reshape(-1, NCB, GPB * GS, SW)


def _block_diag_b_t(d):
    blocks = jnp.einsum("lkgchn,gh->lkgnc", d.reshape(-1, NCB, GPB, GS, GPB, NS), jnp.eye(GPB, dtype=F32))
    return blocks.reshape(-1, NG, NS * GS)


def _block_diag_c(c):
    blocks = jnp.einsum("lkgcn,gh->lkgnhc", c.reshape(-1, NCB, GPB, GS, NS), jnp.eye(GPB, dtype=F32))
    return blocks.reshape(-1, NCB, SW, GPB * GS)


def _block_diag_c_t(d):
    blocks = jnp.einsum("lkgnhc,gh->lkgcn", d.reshape(-1, NCB, GPB, NS, GPB, GS), jnp.eye(GPB, dtype=F32))
    return blocks.reshape(-1, NG, GS, NS)


def _local_step(x, target, weights, conv, small, comm=None):
    weights = list(weights) + [None] * (DEPTH - len(weights))
    ar, ai = small["ssm_a_re"], small["ssm_a_im"]
    ldt = small["ssm_log_dt"].reshape(DEPTH, NG, 1)
    br2 = small["ssm_b_re"].reshape(DEPTH, NG, NS * GS)
    bi2 = small["ssm_b_im"].reshape(DEPTH, NG, NS * GS)
    lr, li, bbr2, bbi2 = _s5_params(ar, ai, ldt, br2, bi2)

    def row(name, l, width):
        return small[name][l].reshape(1, width)

    gvecs = jnp.pad(jnp.stack([small["a_log"], small["dt_bias"]], axis=1), ((0, 0), (0, 6), (NH, DH - 2 * NH)))
    lams = jnp.stack([lr.reshape(DEPTH, 1, NG * NS), li.reshape(DEPTH, 1, NG * NS)], axis=1)
    bblks = jnp.stack([_block_diag_b(bbr2), _block_diag_b(bbi2)], axis=1)
    cblks = jnp.stack([_block_diag_c(small["ssm_c_re"]), _block_diag_c(small["ssm_c_im"])], axis=1)
    saved = []
    for l in range(DEPTH):
        gvec, lam, bblk, cblk = gvecs[l], lams[l], bblks[l], cblks[l]
        wcat, wglu, wout = weights[l]
        fetch = [None] * 3
        if comm and l + 1 < DEPTH:
            fetch = [_Exchange(comm["weight_parts"](l + 1, part), CHIP_RELS, True, False) for part in range(3)]
        proj, h, got0 = _inproj_fwd(x, row("norm_pre", l, D), wcat, 0, fetch[0])
        qkv = _prep_fwd(proj, conv[l])
        bb, gcb = _gates_fwd(proj, gvec)
        local, t_inv, got1 = _delta_local_fwd(qkv, bb, gcb, fetch[1])
        o, states = _delta_state_fwd(local, gcb)
        s5y, carries, got2 = _s5_fwd(proj, lam, bblk, cblk, fetch[2])
        if fetch[0] is not None:
            weights[l + 1] = comm["weights_from"]([got0, got1, got2])
        xn = _mix_fwd(proj, o, s5y, x, row("head_norm", l, DH), row("ssm_d", l, D), wglu, row("b_glu", l, D),
                      wout, row("norm_post", l, D), 0)
        saved.append((x, proj, h, qkv, bb, gcb, local, t_inv, o, states, s5y, carries, gvec, lam, bblk, cblk))
        x = xn

    dx, loss_part = _loss_grad(x, target)

    g = {k: [None] * DEPTH for k in ("wcat", "conv", "wglu", "wout", "norm_pre", "a_log", "dt_bias", "head_norm",
                                     "ssm_d", "b_glu", "norm_post", "dlam", "dbblk", "dcblk")}
    from_chips, send, send_layer = [None] * DEPTH, None, None
    for l in reversed(range(DEPTH)):
        xl, proj, h, qkv, bb, gcb, local, t_inv, o, states, s5y, carries, gvec, lam, bblk, cblk = saved[l]
        wcat, wglu, wout = weights[l]
        (dza, du_skip, dzb, dra, drb, do, ds5y, dxres, dwg, dwo, dvecs, dhn) = _mix_bwd(
            proj, o, s5y, xl, row("head_norm", l, DH), row("ssm_d", l, D), wglu, row("b_glu", l, D), wout,
            row("norm_post", l, D), dx, 0)
        (du, dlam, dbblk, dcblk), arrived = _s5_bwd(proj, lam, bblk, cblk, carries, ds5y, du_skip, send)
        if send is not None:
            from_chips[send_layer] = arrived
        *dlocal, dgcb_state = _delta_state_bwd(local, gcb, states, do)
        dq, dk, dv, dbb, dgcb = _delta_local_bwd(qkv, bb, gcb, t_inv, dlocal, dgcb_state)
        dbd, dgvec = _gates_bwd(proj, gvec, dbb, dgcb)
        dpre, dconv = _prep_bwd(proj, conv[l], dq, dk, dv)
        dproj = jnp.concatenate([dpre, dza, du, dzb, dra, drb, dbd], axis=1)
        g["wcat"][l] = _inproj_bwd_dw(h, dproj)
        g["conv"][l], g["wglu"][l], g["wout"][l] = dconv, dwg, dwo
        send = _Exchange(comm["grad_parts"](g["wcat"][l], dwg, dwo), CHIP_RELS, True, True) if comm else None
        dx, dgain, arrived = _inproj_bwd_dx(dproj, wcat, xl, row("norm_pre", l, D), dxres, 0, send if l == 0 else None)
        if comm and l == 0:
            from_chips[l] = arrived
        send_layer = l
        g["norm_pre"][l] = dgain[0]
        g["a_log"][l], g["dt_bias"][l] = dgvec[0, NH:2 * NH], dgvec[1, NH:2 * NH]
        g["head_norm"][l] = dhn[0]
        g["ssm_d"][l], g["b_glu"][l], g["norm_post"][l] = dvecs[0], dvecs[1], dvecs[2]
        g["dlam"][l], g["dbblk"][l], g["dcblk"][l] = dlam, dbblk, dcblk
    if comm:
        for k in ("wcat", "wglu", "wout"):
            del g[k]
    g = {k: jnp.stack(v) for k, v in g.items()}
    g["from_chips"] = from_chips
    dlam, dbblk, dcblk = g.pop("dlam"), g.pop("dbblk"), g.pop("dcblk")
    g["ssm_c_re"], g["ssm_c_im"] = _block_diag_c_t(dcblk[:, 0]), _block_diag_c_t(dcblk[:, 1])
    dar, dai, dldt, dbr2, dbi2 = _s5_params_bwd(
        ar, ai, ldt, br2, bi2, dlam[:, 0].reshape(DEPTH, NG, NS), dlam[:, 1].reshape(DEPTH, NG, NS),
        _block_diag_b_t(dbblk[:, 0]), _block_diag_b_t(dbblk[:, 1]))
    g["ssm_a_re"], g["ssm_a_im"], g["ssm_log_dt"] = dar, dai, dldt.reshape(DEPTH, NG)
    g["ssm_b_re"] = dbr2.reshape(DEPTH, NG, NS, GS)
    g["ssm_b_im"] = dbi2.reshape(DEPTH, NG, NS, GS)
    return loss_part[0, 0], dx, g


def kernel(x, norm_pre, w_in, conv_w, a_log, dt_bias, head_norm, ssm_a_re, ssm_a_im, ssm_log_dt, ssm_b_re, ssm_b_im, ssm_c_re, ssm_c_im, ssm_d, w_glu, b_glu, w_out, norm_post, loss_target, m_norm_pre, m_w_in, m_conv_w, m_a_log, m_dt_bias, m_head_norm, m_ssm_a_re, m_ssm_a_im, m_ssm_log_dt, m_ssm_b_re, m_ssm_b_im, m_ssm_c_re, m_ssm_c_im, m_ssm_d, m_w_glu, m_b_glu, m_w_out, m_norm_post, v_norm_pre, v_w_in, v_conv_w, v_a_log, v_dt_bias, v_head_norm, v_ssm_a_re, v_ssm_a_im, v_ssm_log_dt, v_ssm_b_re, v_ssm_b_im, v_ssm_c_re, v_ssm_c_im, v_ssm_d, v_w_glu, v_b_glu, v_w_out, v_norm_post):
    args = dict(locals())
    small = {n: args[n] for n, _ in SMALL}

    def flat2(a):
        return a.reshape(-1, a.shape[-1])

    w_in16, w_glu16, w_out16 = w_in.astype(BF16), w_glu.astype(BF16), w_out.astype(BF16)

    def weight_parts(l, part=None):
        if part is None:
            return [w_in16[l], w_glu16[l], w_out16[l]]
        lo, hi = WEIGHT_SPLIT[part], WEIGHT_SPLIT[part + 1]
        return [w_in16[l, lo:hi]] + [[w_glu16[l]], [w_out16[l]], []][part]

    def weights_from(parts):
        if len(parts) == 3 and isinstance(parts[0], (list, tuple)):
            parts = [jnp.concatenate([p[0] for p in parts], axis=1), parts[0][1], parts[1][1]]
        g_in, g_glu, g_out = parts[:3]
        return _wcat_from_chips(g_in), g_glu.reshape(1, D, D), g_out.reshape(1, D, D)

    def grad_parts(gwcat, gwglu, gwout):
        return [_wcat_grad_by_chip(gwcat).astype(BF16), gwglu.reshape(NCHIP, ROW_SHARD, D).astype(BF16),
                gwout.reshape(NCHIP, ROW_SHARD, D).astype(BF16)]

    first = _exchange(weight_parts(0) + [flat2(conv_w)], CHIP_RELS, True, False, "gather_weights")
    conv = _cols_from_chips(first[3], DEPTH)
    comm = dict(weight_parts=weight_parts, weights_from=weights_from, grad_parts=grad_parts)
    loss_part, dx, g = _local_step(x[0], loss_target[0], [weights_from(first)], conv, small, comm)
    loss = lax.psum(loss_part, ("x", "y", "c"))

    from_chips = [jnp.concatenate([g["from_chips"][l][a] for l in range(DEPTH)], axis=1) for a in range(3)]
    core_sums = [_sum_slots(p, "sum_chips_" + n) for p, n in zip(from_chips, ("in", "glu", "out"))]
    others = _sibling_swap(core_sums, "swap_cores")
    sharded = {}
    for n, mine, other in zip(("w_in", "w_glu", "w_out"), core_sums, others):
        sharded[n] = _adamw(args[n], [mine, other], args["m_" + n], args["v_" + n], "adamw_" + n, max_rows=128)

    pad = jnp.zeros(((-(SMALL_ROWS + CONV_ROWS)) % (8 * NDEV), D), F32)
    small_sum = _all_reduce(_pack_small([g[n] for n, _ in SMALL], extra=[g["conv"], pad]), "reduce_small")
    small_out = _adamw(_pack_small([args[n] for n, _ in SMALL]), [small_sum],
                       _pack_small([args["m_" + n] for n, _ in SMALL]),
                       _pack_small([args["v_" + n] for n, _ in SMALL]), "adamw_small")
    chip = 2 * lax.axis_index("x") + lax.axis_index("y")
    conv_sum = small_sum[SMALL_ROWS:SMALL_ROWS + CONV_ROWS].reshape(DEPTH * 4, 3 * D)
    conv_sum = lax.dynamic_slice_in_dim(conv_sum, chip * CONV_SHARD, CONV_SHARD, axis=1)
    sharded["conv_w"] = _adamw(flat2(conv_w), [conv_sum], flat2(m_conv_w), flat2(v_conv_w), "adamw_conv")

    names = ["norm_pre", "w_in", "conv_w", "a_log", "dt_bias", "head_norm", "ssm_a_re", "ssm_a_im", "ssm_log_dt",
             "ssm_b_re", "ssm_b_im", "ssm_c_re", "ssm_c_im", "ssm_d", "w_glu", "b_glu", "w_out", "norm_post"]
    outs = [loss, dx[None]]
    for i in range(4):
        sm = dict(zip([n for n, _ in SMALL], _unpack_small(small_out[i])))
        outs += [sharded[n][i].reshape(args[n].shape) if n in sharded else sm[n] for n in names]
    return tuple(outs)
```

```python
import functools
import math

import jax
import jax.numpy as jnp
from jax import lax
from jax.experimental import pallas as pl
from jax.experimental.pallas import tpu as pltpu

F32 = jnp.float32
BF16 = jnp.bfloat16
HI = lax.Precision.HIGHEST

D = 1024
NH = 8
DH = 128
CH = 128
NG = 64
GS = 16
NS = 64
GPB = 8
NCB = NG // GPB
SW = GPB * NS
NCOL = 8320
BD0 = 8192
EPS = 1e-6
DEPTH = 4
NCHIP = 4
NDEV = 8
VMEM_LIMIT = 56 * 1024 * 1024
GRAD_ACT = jnp.bfloat16

ADAM_LR = 0.001
ADAM_B1 = 0.9
ADAM_B2 = 0.999
ADAM_EPS = 1e-08
ADAM_WD = 0.01
ADAM_STEP = 10


def _cparams(sem=None):
    return pltpu.CompilerParams(dimension_semantics=sem, vmem_limit_bytes=VMEM_LIMIT)


def _full(shape):
    nd = len(shape)
    return pl.BlockSpec(shape, lambda *_: (0,) * nd)


def _rms(x, gain):
    ms = jnp.mean(x * x, axis=-1, keepdims=True)
    return x * lax.rsqrt(ms + EPS) * gain


def _sigmoid(x):
    return 1.0 / (1.0 + jnp.exp(-x))


def _silu(x):
    return x * _sigmoid(x)


def _softplus(x):
    return jnp.maximum(x, 0.0) + jnp.log(1.0 + jnp.exp(-jnp.abs(x)))


def _gelu(x):
    return 0.5 * x * (1.0 + jnp.tanh(math.sqrt(2.0 / math.pi) * (x + 0.044715 * (x * x * x))))


def _dot_bf16(a, b, dims):
    return lax.dot_general(a.astype(BF16), b.astype(BF16), (dims, ((), ())), preferred_element_type=F32)


def _mm_nt(a, b):
    return _dot_bf16(a, b, ((1,), (1,)))


def _mm_tn(a, b):
    return _dot_bf16(a, b, ((0,), (0,)))


@jax.custom_vjp
def _mm(a, b):
    return _dot_bf16(a, b, ((1,), (0,)))


def _mm_fwd(a, b):
    return _dot_bf16(a, b, ((1,), (0,))), (a, b)


def _mm_bwd(res, ct):
    a, b = res
    return _mm_nt(ct, b).astype(a.dtype), _mm_tn(a, ct).astype(b.dtype)


_mm.defvjp(_mm_fwd, _mm_bwd)


@jax.custom_vjp
def _mm_nt_d(a, b):
    return _mm_nt(a, b)


def _mm_nt_d_bwd(res, ct):
    a, b = res
    return _dot_bf16(ct, b, ((1,), (0,))), _mm_tn(ct, a)


_mm_nt_d.defvjp(lambda a, b: (_mm_nt(a, b), (a, b)), _mm_nt_d_bwd)


@jax.custom_vjp
def _mm_tn_d(a, b):
    return _mm_tn(a, b)


def _mm_tn_d_bwd(res, ct):
    a, b = res
    return _mm_nt(b, ct), _dot_bf16(a, ct, ((1,), (0,)))


_mm_tn_d.defvjp(lambda a, b: (_mm_tn(a, b), (a, b)), _mm_tn_d_bwd)


def _split_bf16(a):
    hi = a.astype(BF16)
    return hi, (a - hi.astype(F32)).astype(BF16)


def _dot3(a, b, dims):
    ah, al = _split_bf16(a)
    bh, bl = _split_bf16(b)

    def dot(x, y):
        return lax.dot_general(x, y, (dims, ((), ())), preferred_element_type=F32)

    return dot(ah, bh) + (dot(ah, bl) + dot(al, bh))


@jax.custom_vjp
def _imm(a, b):
    return _dot3(a, b, ((1,), (0,)))


def _imm_bwd(res, ct):
    a, b = res
    return _dot3(ct, b, ((1,), (1,))), _dot3(a, ct, ((0,), (0,)))


_imm.defvjp(lambda a, b: (_dot3(a, b, ((1,), (0,))), (a, b)), _imm_bwd)


def _hmm(a, b):
    return jnp.dot(a, b, precision=HI, preferred_element_type=F32)


def _rows(shape):
    return lax.broadcasted_iota(jnp.int32, shape, 0)


def _cols(shape):
    return lax.broadcasted_iota(jnp.int32, shape, 1)


def _sd(x, s):
    return jnp.where(_rows(x.shape) >= s, pltpu.roll(x, s, axis=0), 0.0)


def _su(x, s):
    n = x.shape[0]
    return jnp.where(_rows(x.shape) < n - s, pltpu.roll(x, n - s, axis=0), 0.0)


@functools.partial(jax.custom_vjp, nondiff_argnums=(1,))
def _shift_down(x, s):
    return _sd(x, s)


def _shift_down_fwd(x, s):
    return _sd(x, s), None


def _shift_down_bwd(s, _, g):
    return (_su(g, s),)


_shift_down.defvjp(_shift_down_fwd, _shift_down_bwd)


def _last_row(x):
    n = x.shape[0]
    return jnp.sum(jnp.where(_rows(x.shape) == n - 1, x, 0.0), axis=0, keepdims=True)


def _prep_fn(p, w0, w1, w2, w3, qk):
    acc = w3 * p + w2 * _shift_down(p, 1) + w1 * _shift_down(p, 2) + w0 * _shift_down(p, 3)
    a = _silu(acc)
    nrm = lax.rsqrt(jnp.sum(a * a, axis=-1, keepdims=True) + EPS)
    return a * (nrm * qk + (1.0 - qk))


def _gates_fn(bd, av, bv):
    tm = bd.shape[0]
    beta_all = _sigmoid(bd)
    g_all = -jnp.exp(av) * _softplus(bd + bv)
    r, c = _rows((tm, tm)), _cols((tm, tm))
    tri = jnp.where((r // CH == c // CH) & (r >= c), 1.0, 0.0).astype(F32)
    gc_all = _hmm(tri, g_all)
    lane = _cols(bd.shape)
    outs = []
    for h in range(NH):
        b = jnp.sum(jnp.where(lane == h, beta_all, 0.0), axis=1, keepdims=True)
        outs.append(jnp.broadcast_to(b, bd.shape))
    for h in range(NH):
        g = jnp.sum(jnp.where(lane == NH + h, gc_all, 0.0), axis=1, keepdims=True)
        outs.append(jnp.broadcast_to(g, bd.shape))
    return tuple(outs)


INV_BASE = 2


def _merge_mm(a, b):
    return _dot_bf16(a, b, ((1,), (0,)))


def _unit_lower_inv(l_mats):
    n = l_mats[0].shape[0]
    ii, jj = _rows((n, n)), _cols((n, n))
    base = ii // INV_BASE == jj // INV_BASE
    ps = [-jnp.where(base, l_mat, 0.0) for l_mat in l_mats]
    eye = jnp.where(ii == jj, 1.0, 0.0).astype(F32)
    ds = [eye + p for p in ps]
    k = 1
    while 2 * k < INV_BASE:
        ps = [_imm(p, p) for p in ps]
        ds = [d + _imm(d, p) for d, p in zip(ds, ps)]
        k *= 2
    m = INV_BASE
    while m < n:
        pair = (ii // (2 * m) == jj // (2 * m)) & (ii // m > jj // m)
        des = [_merge_mm(d, jnp.where(pair, l_mat, 0.0)) for d, l_mat in zip(ds, l_mats)]
        ds = [d - _merge_mm(de, d) for d, de in zip(ds, des)]
        m *= 2
    return ds


@jax.custom_vjp
def _known_inverse(l_mat, t_inv):
    return t_inv


def _known_inverse_bwd(t_inv, ct):
    d_l = -_dot3(_dot3(t_inv, ct, ((0,), (0,))), t_inv, ((1,), (1,)))
    return d_l, jnp.zeros_like(t_inv)


_known_inverse.defvjp(lambda l_mat, t_inv: (t_inv, t_inv), _known_inverse_bwd)


def _chunk_system(q, k, v, bb, gcb):
    qs = q * (DH ** -0.5)
    kb = k * bb
    eg = jnp.exp(gcb)
    ii, jj = _rows((CH, CH)), _cols((CH, CH))
    decay = jnp.exp(jnp.where(ii >= jj, gcb - gcb.T, -1e30))
    l_mat = jnp.where(ii > jj, _mm_nt_d(kb, k) * decay, 0.0)
    a_qk = _mm_nt_d(qs, k) * decay
    k_dec = k * jnp.exp(_last_row(gcb) - gcb)
    return l_mat, (v * bb, kb * eg, qs * eg, k_dec, a_qk)


def _chunk_solve(t_inv, rest):
    vb, kbe, q_dec, k_dec, a_qk = rest
    return _mm(t_inv, vb), _mm(t_inv, kbe), q_dec, k_dec, a_qk


def _side_by_side_vjp(fn, items, cts):
    n = len(items[0])
    _, vjp = jax.vjp(lambda *flat: fn([flat[i * n:(i + 1) * n] for i in range(len(items))]),
                     *[a for item in items for a in item])
    grads = vjp(cts)
    return [grads[i * n:(i + 1) * n] for i in range(len(items))]


def _chunks_local_known(items):
    systems = [_chunk_system(*item[:5]) for item in items]
    t_invs = [_known_inverse(l_mat, item[5]) for (l_mat, _), item in zip(systems, items)]
    return [_chunk_solve(t_inv, rest) for t_inv, (_, rest) in zip(t_invs, systems)]


def _chunks_local(chunks):
    systems = [_chunk_system(*c) for c in chunks]
    t_invs = _unit_lower_inv([l_mat for l_mat, _ in systems])
    return [(_chunk_solve(t_inv, rest), t_inv) for t_inv, (_, rest) in zip(t_invs, systems)]


def _state_steps(items):
    v_news = [u - _mm(w, state) for u, w, _, _, _, _, state in items]
    outs = [_mm(q_dec, state) + _mm(a_qk, v_new) for (_, _, q_dec, _, a_qk, _, state), v_new in zip(items, v_news)]
    states = [state * jnp.exp(_last_row(gcb)) + _mm_tn_d(k_dec, v_new)
              for (_, _, _, k_dec, _, gcb, state), v_new in zip(items, v_news)]
    return list(zip(outs, states))


SUB = 8


SCAN_ROWS = 32


def _cmul(ar, ai, br, bi):
    return ar * br - ai * bi, ar * bi + ai * br


def _scan_tile(xr, xi, mr, mi, hr_ref, hi_ref, cr_ref, ci_ref, reverse, fold):
    n, width = xr.shape
    ngroups, nlb = n // SUB, width // DH
    shift_groups = _su if reverse else _sd

    def lanes(x, j):
        return x[..., j * DH:(j + 1) * DH]

    start = n - 1 if reverse else 0
    for j in range(nlb):
        hr_ref[j] = lanes(xr, j)
        hi_ref[j] = lanes(xi, j)
        hr_ref[j, start:start + 1, :] += lanes(fold[0], j)
        hi_ref[j, start:start + 1, :] += lanes(fold[1], j)
    pr, pi = mr, mi
    tr, ti = jnp.broadcast_to(mr, (SUB, width)), jnp.broadcast_to(mi, (SUB, width))
    pos = _rows(tr.shape)
    steps = []
    s = 1
    while s < SUB:
        inside = pos < SUB - s if reverse else pos >= s
        shift = SUB - s if reverse else s
        steps.append((shift, jnp.where(inside, pr, 0.0), jnp.where(inside, pi, 0.0)))
        er = jnp.where(inside, pltpu.roll(tr, shift, axis=0), 1.0)
        ei = jnp.where(inside, pltpu.roll(ti, shift, axis=0), 0.0)
        tr, ti = _cmul(tr, ti, er, ei)
        pr, pi = _cmul(pr, pi, pr, pi)
        s *= 2
    for b in range(0, n, SCAN_ROWS):
        rows = slice(b, b + SCAN_ROWS)
        for j in range(nlb):
            br = hr_ref[j, rows, :].reshape(SCAN_ROWS // SUB, SUB, DH)
            bi = hi_ref[j, rows, :].reshape(SCAN_ROWS // SUB, SUB, DH)
            for shift, qr, qi in steps:
                dr, di = _cmul(lanes(qr, j)[None], lanes(qi, j)[None],
                               pltpu.roll(br, shift, axis=1), pltpu.roll(bi, shift, axis=1))
                br, bi = br + dr, bi + di
            hr_ref[j, rows, :] = br.reshape(SCAN_ROWS, DH)
            hi_ref[j, rows, :] = bi.reshape(SCAN_ROWS, DH)
    edge = pl.ds(0 if reverse else SUB - 1, ngroups, stride=SUB)
    gr = jnp.concatenate([hr_ref.at[j][edge, :] for j in range(nlb)], axis=1)
    gi = jnp.concatenate([hi_ref.at[j][edge, :] for j in range(nlb)], axis=1)
    s = 1
    while s < ngroups:
        dr, di = _cmul(pr, pi, shift_groups(gr, s), shift_groups(gi, s))
        gr, gi = gr + dr, gi + di
        pr, pi = _cmul(pr, pi, pr, pi)
        s *= 2
    cr_ref[...] = shift_groups(gr, 1)
    ci_ref[...] = shift_groups(gi, 1)
    for g in range(ngroups):
        rows = slice(g * SUB, (g + 1) * SUB)
        dr, di = _cmul(tr, ti, cr_ref[g:g + 1, :], ci_ref[g:g + 1, :])
        for j in range(nlb):
            hr_ref[j, rows, :] += lanes(dr, j)
            hi_ref[j, rows, :] += lanes(di, j)
    return (jnp.concatenate([hr_ref[j] for j in range(nlb)], axis=1),
            jnp.concatenate([hi_ref[j] for j in range(nlb)], axis=1))


def _s5_states(u, lam_ref, b_ref, car_ref, hr_ref, hi_ref, cr_ref, ci_ref):
    lr, li = lam_ref[0], lam_ref[1]
    fold = _cmul(lr, li, car_ref[0:1, :], car_ref[1:2, :])
    return _scan_tile(_mm(u, b_ref[0]), _mm(u, b_ref[1]), lr, li, hr_ref, hi_ref, cr_ref, ci_ref, False, fold)


def _scratch_row(ref, row):
    return jnp.concatenate([ref[j, row:row + 1, :] for j in range(ref.shape[0])], axis=1)


def _s5_params_fn(ar, ai, ldt, br2, bi2):
    dt = jnp.exp(ldt)
    mag = jnp.exp(ar * dt)
    lr, li = mag * jnp.cos(ai * dt), mag * jnp.sin(ai * dt)
    den = ar * ar + ai * ai
    fr = ((lr - 1.0) * ar + li * ai) / den
    fi = (li * ar - (lr - 1.0) * ai) / den
    expand = jnp.where(_cols((NS, NS * GS)) // GS == _rows((NS, NS * GS)), 1.0, 0.0).astype(F32)
    fr2, fi2 = _hmm(fr, expand), _hmm(fi, expand)
    return lr, li, fr2 * br2 - fi2 * bi2, fr2 * bi2 + fi2 * br2


def _head_norm(o, hn):
    parts = []
    for h in range(NH):
        oh = o[:, h * DH:(h + 1) * DH]
        parts.append(oh * lax.rsqrt(jnp.mean(oh * oh, axis=-1, keepdims=True) + EPS) * hn)
    return jnp.concatenate(parts, axis=1)


def _mix_pre(s5y, u, dvec):
    return _gelu(s5y + dvec * u)


def _mix_mid(o, za, y0, gl, zb, ra, rb, hn):
    ya = _head_norm(o, hn) * _silu(za)
    yb = y0 * _sigmoid(gl) * _silu(zb)
    return _sigmoid(ra) * ya + _sigmoid(rb) * yb


def _mix_post(x, out, npost):
    return x + _rms(out, npost)


def _tile(t, want):
    return min(t, want)


def _row_tile(rows, want):
    return max(r for r in range(16, want + 1, 16) if rows % r == 0)


def _call_carrying(body, name, grid, in_specs, out_specs, out_shape, scratch, args, semantics, exchange):
    n_out = len(out_shape)
    if exchange is not None:
        body = _carry(body, len(args), n_out, len(scratch), exchange, grid)
        in_specs, out_specs = in_specs + exchange.in_specs, out_specs + exchange.out_specs
        out_shape, scratch, args = out_shape + exchange.out_shape, scratch + exchange.scratch_shapes, args + exchange.srcs
        semantics = ("arbitrary",) * len(grid)
    outs = pl.pallas_call(body, name=name, grid=grid, in_specs=in_specs, out_specs=out_specs, out_shape=out_shape,
                          scratch_shapes=scratch, compiler_params=_cparams(semantics))(*args)
    return outs[:n_out], outs[n_out:]


def _inproj_fwd(x, gain, wcat, l, exchange=None):
    t = x.shape[0]
    tm, tn = _tile(t, 1024), 1664

    def body(x_ref, g_ref, w_ref, o_ref, h_ref):
        @pl.when(pl.program_id(1) == 0)
        def _():
            h_ref[...] = _rms(x_ref[...], g_ref[...]).astype(h_ref.dtype)
        o_ref[...] = _dot_bf16(h_ref[...], w_ref[...], ((1,), (0,)))

    (proj, h), fetched = _call_carrying(
        body, "inproj_fwd", (t // tm, NCOL // tn),
        [pl.BlockSpec((tm, D), lambda i, j: (i, 0)), _full((1, D)), pl.BlockSpec((None, D, tn), lambda i, j: (l, 0, j))],
        [pl.BlockSpec((tm, tn), lambda i, j: (i, j)), pl.BlockSpec((tm, D), lambda i, j: (i, 0))],
        [jax.ShapeDtypeStruct((t, NCOL), F32), jax.ShapeDtypeStruct((t, D), wcat.dtype)],
        [], [x, gain, wcat], ("parallel", "arbitrary"), exchange)
    return proj, h, fetched


def _inproj_bwd_dx(dproj, wcat, x, gain, dxres, l, exchange=None):
    t = x.shape[0]
    tm, tk = _tile(t, 1024), 1664
    nk = NCOL // tk

    def body(dp_ref, w_ref, x_ref, g_ref, r_ref, dx_ref, dg_ref, acc_ref):
        i, k = pl.program_id(0), pl.program_id(1)

        @pl.when(k == 0)
        def _():
            acc_ref[...] = jnp.zeros_like(acc_ref)

        acc_ref[...] += _mm_nt(dp_ref[...], w_ref[...])

        @pl.when(k == nk - 1)
        def _():
            _, vjp = jax.vjp(_rms, x_ref[...], g_ref[...])
            dx, dg = vjp(acc_ref[...])
            dx_ref[...] = r_ref[...] + dx

            @pl.when(i == 0)
            def _():
                dg_ref[...] = dg

            @pl.when(i > 0)
            def _():
                dg_ref[...] += dg

    grid = (t // tm, nk)
    in_specs = [pl.BlockSpec((tm, tk), lambda i, k: (i, k)), pl.BlockSpec((None, D, tk), lambda i, k: (l, 0, k)),
                pl.BlockSpec((tm, D), lambda i, k: (i, 0)), _full((1, D)), pl.BlockSpec((tm, D), lambda i, k: (i, 0))]
    out_specs = [pl.BlockSpec((tm, D), lambda i, k: (i, 0)), _full((1, D))]
    out_shape = [jax.ShapeDtypeStruct((t, D), F32), jax.ShapeDtypeStruct((1, D), F32)]
    scratch, args = [pltpu.VMEM((tm, D), F32)], [dproj, wcat, x, gain, dxres]
    if exchange is not None:
        body = _carry(body, len(args), len(out_shape), len(scratch), exchange, grid)
        in_specs, out_specs = in_specs + exchange.in_specs, out_specs + exchange.out_specs
        out_shape, scratch, args = out_shape + exchange.out_shape, scratch + exchange.scratch_shapes, args + exchange.srcs
    outs = pl.pallas_call(
        body, name="inproj_bwd_dx", grid=grid, in_specs=in_specs, out_specs=out_specs, out_shape=out_shape,
        scratch_shapes=scratch, compiler_params=_cparams(("arbitrary", "arbitrary")),
    )(*args)
    return outs[0], outs[1], outs[2:]


def _inproj_bwd_dw(h, dproj):
    t = h.shape[0]
    tm, tn = _tile(t, 1024), 1664

    def body(h_ref, dp_ref, o_ref):
        @pl.when(pl.program_id(1) == 0)
        def _():
            o_ref[...] = jnp.zeros_like(o_ref)

        o_ref[...] += _mm_tn(h_ref[...], dp_ref[...])

    return pl.pallas_call(
        body, name="inproj_bwd_dw", grid=(NCOL // tn, t // tm),
        in_specs=[pl.BlockSpec((tm, D), lambda j, i: (i, 0)), pl.BlockSpec((tm, tn), lambda j, i: (i, j))],
        out_specs=pl.BlockSpec((D, tn), lambda j, i: (0, j)),
        out_shape=jax.ShapeDtypeStruct((D, NCOL), F32),
        compiler_params=_cparams(("parallel", "arbitrary")),
    )(h, dproj)


def _prep_fwd(proj, cw):
    t = proj.shape[0]

    def body(p_ref, w_ref, o_ref):
        qk = (pl.program_id(0) < 2 * NH).astype(F32)
        o_ref[...] = _prep_fn(p_ref[...], w_ref[0:1, :], w_ref[1:2, :], w_ref[2:3, :], w_ref[3:4, :], qk)

    return pl.pallas_call(
        body, name="prep_fwd", grid=(3 * NH,),
        in_specs=[pl.BlockSpec((t, DH), lambda c: (0, c)), pl.BlockSpec((4, DH), lambda c: (0, c))],
        out_specs=pl.BlockSpec((None, t, DH), lambda c: (c, 0, 0)),
        out_shape=jax.ShapeDtypeStruct((3 * NH, t, DH), F32),
        compiler_params=_cparams(("parallel",)),
    )(proj, cw)


def _prep_bwd(proj, cw, dq, dk, dv):
    t = proj.shape[0]

    def body(p_ref, w_ref, dq_ref, dk_ref, dv_ref, dp_ref, dw_ref):
        c = pl.program_id(0)
        qk = (c < 2 * NH).astype(F32)
        _, vjp = jax.vjp(lambda p, w0, w1, w2, w3: _prep_fn(p, w0, w1, w2, w3, qk),
                         p_ref[...], w_ref[0:1, :], w_ref[1:2, :], w_ref[2:3, :], w_ref[3:4, :])
        d = jnp.where(c < NH, dq_ref[...], jnp.where(c < 2 * NH, dk_ref[...], dv_ref[...]))
        dp, dw0, dw1, dw2, dw3 = vjp(d)
        dp_ref[...] = dp.astype(dp_ref.dtype)
        dw_ref[0:1, :] = dw0
        dw_ref[1:2, :] = dw1
        dw_ref[2:3, :] = dw2
        dw_ref[3:4, :] = dw3

    return pl.pallas_call(
        body, name="prep_bwd", grid=(3 * NH,),
        in_specs=[pl.BlockSpec((t, DH), lambda c: (0, c)), pl.BlockSpec((4, DH), lambda c: (0, c))]
        + [pl.BlockSpec((None, t, DH), functools.partial(lambda c, off: (jnp.clip(c - off, 0, NH - 1), 0, 0), off=off))
           for off in (0, NH, 2 * NH)],
        out_specs=[pl.BlockSpec((t, DH), lambda c: (0, c)), pl.BlockSpec((4, DH), lambda c: (0, c))],
        out_shape=[jax.ShapeDtypeStruct((t, 3 * D), GRAD_ACT), jax.ShapeDtypeStruct((4, 3 * D), F32)],
        compiler_params=_cparams(("arbitrary",)),
    )(proj, cw, dq, dk, dv)


def _gates_fwd(proj, gvec):
    t = proj.shape[0]
    tm = _tile(t, 512)

    def body(p_ref, gv_ref, b_ref, g_ref):
        outs = _gates_fn(p_ref[...], gv_ref[0:1, :], gv_ref[1:2, :])
        for h in range(NH):
            b_ref[h] = outs[h]
            g_ref[h] = outs[NH + h]

    spec = pl.BlockSpec((NH, tm, DH), lambda i: (0, i, 0))
    return pl.pallas_call(
        body, name="gates_fwd", grid=(t // tm,),
        in_specs=[pl.BlockSpec((tm, DH), lambda i: (i, BD0 // DH)), _full((8, DH))],
        out_specs=[spec, spec],
        out_shape=[jax.ShapeDtypeStruct((NH, t, DH), F32)] * 2,
        compiler_params=_cparams(("parallel",)),
    )(proj, gvec)


def _gates_bwd(proj, gvec, dbb, dgcb):
    t = proj.shape[0]
    tm = _tile(t, 512)

    def body(p_ref, gv_ref, db_ref, dg_ref, dp_ref, dgv_ref):
        _, vjp = jax.vjp(_gates_fn, p_ref[...], gv_ref[0:1, :], gv_ref[1:2, :])
        cts = tuple(db_ref[h] for h in range(NH)) + tuple(dg_ref[h] for h in range(NH))
        dp, da, db = vjp(cts)
        dp_ref[...] = dp.astype(dp_ref.dtype)

        @pl.when(pl.program_id(0) == 0)
        def _():
            dgv_ref[...] = jnp.zeros_like(dgv_ref)

        dgv_ref[0:1, :] += da
        dgv_ref[1:2, :] += db

    spec = pl.BlockSpec((NH, tm, DH), lambda i: (0, i, 0))
    return pl.pallas_call(
        body, name="gates_bwd", grid=(t // tm,),
        in_specs=[pl.BlockSpec((tm, DH), lambda i: (i, BD0 // DH)), _full((8, DH)), spec, spec],
        out_specs=[pl.BlockSpec((tm, DH), lambda i: (i, 0)), _full((8, DH))],
        out_shape=[jax.ShapeDtypeStruct((t, DH), GRAD_ACT), jax.ShapeDtypeStruct((8, DH), F32)],
        compiler_params=_cparams(("arbitrary",)),
    )(proj, gvec, dbb, dgcb)


def _chunks_per_step(nch):
    return max(c for c in (8, 4, 2, 1) if nch % c == 0)


def _grid_ends(grid):
    def first():
        return functools.reduce(jnp.logical_and, [pl.program_id(a) == 0 for a in range(len(grid))])

    def last():
        return functools.reduce(jnp.logical_and, [pl.program_id(a) == n - 1 for a, n in enumerate(grid)])

    return first, last


def _carry(body, n_in, n_out, n_scratch, exchange, grid):
    first, last = _grid_ends(grid)
    na = exchange.narr

    def wrapped(*refs):
        a, b = n_in, n_in + na
        c, d = b + n_out, b + n_out + na
        e = d + n_scratch
        srcs, dsts, sems = refs[a:b], refs[c:d], refs[e:]

        @pl.when(first())
        def _():
            exchange.start(srcs, dsts, sems)

        body(*(refs[:a] + refs[b:c] + refs[d:e]))

        @pl.when(last())
        def _():
            exchange.wait(srcs, dsts, sems)

    return wrapped


def _delta_local_fwd(qkv, bb, gcb, exchange=None):
    t = qkv.shape[1]
    cps = _chunks_per_step(t // CH)
    rows = cps * CH
    grid = (NH, t // rows)

    def body(q_ref, k_ref, v_ref, b_ref, g_ref, *out_refs):
        slices = [slice(c * CH, (c + 1) * CH) for c in range(cps)]
        results = _chunks_local([tuple(ref[sl, :] for ref in (q_ref, k_ref, v_ref, b_ref, g_ref)) for sl in slices])
        for sl, (outs, t_inv) in zip(slices, results):
            for ref, val in zip(out_refs, outs + (t_inv,)):
                ref[sl, :] = val.astype(ref.dtype)

    def blk(off):
        return pl.BlockSpec((None, rows, DH), lambda h, n: (h + off, n, 0))

    in_specs = [blk(0), blk(NH), blk(2 * NH), blk(0), blk(0)]
    out_specs = [blk(0)] * 6
    out_shape = [jax.ShapeDtypeStruct((NH, t, DH), dt) for dt in (F32, BF16, BF16, BF16, BF16, F32)]
    args, scratch, sem = [qkv, qkv, qkv, bb, gcb], [], ("parallel", "parallel")
    if exchange is not None:
        body = _carry(body, 5, 6, 0, exchange, grid)
        in_specs, out_specs = in_specs + exchange.in_specs, out_specs + exchange.out_specs
        out_shape, scratch, args = out_shape + exchange.out_shape, exchange.scratch_shapes, args + exchange.srcs
        sem = ("arbitrary", "arbitrary")
    outs = pl.pallas_call(
        body, name="delta_local_fwd", grid=grid, in_specs=in_specs, out_specs=out_specs, out_shape=out_shape,
        scratch_shapes=scratch, compiler_params=_cparams(sem),
    )(*args)
    return outs[:5], outs[5], outs[6:]


def _delta_local_bwd(qkv, bb, gcb, t_inv, cts, dgcb_state):
    t = qkv.shape[1]
    cps = _chunks_per_step(t // CH)
    rows = cps * CH

    def body(q_ref, k_ref, v_ref, b_ref, g_ref, ti_ref, du_ref, dw_ref, dqd_ref, dkd_ref, da_ref, dgs_ref,
             dq_ref, dk_ref, dv_ref, db_ref, dg_ref):
        slices = [slice(c * CH, (c + 1) * CH) for c in range(cps)]
        items = [tuple(ref[sl, :] for ref in (q_ref, k_ref, v_ref, b_ref, g_ref, ti_ref)) for sl in slices]
        cts = [tuple(ref[sl, :] for ref in (du_ref, dw_ref, dqd_ref, dkd_ref, da_ref)) for sl in slices]
        for sl, (dq, dk, dv, db, dg, _) in zip(slices, _side_by_side_vjp(_chunks_local_known, items, cts)):
            dq_ref[sl, :] = dq
            dk_ref[sl, :] = dk
            dv_ref[sl, :] = dv
            db_ref[sl, :] = db
            dg_ref[sl, :] = dg + dgs_ref[sl, :]

    def blk(off):
        return pl.BlockSpec((None, rows, DH), lambda h, n: (h + off, n, 0))

    return pl.pallas_call(
        body, name="delta_local_bwd", grid=(NH, t // rows),
        in_specs=[blk(0), blk(NH), blk(2 * NH)] + [blk(0)] * 9,
        out_specs=[blk(0)] * 5,
        out_shape=[jax.ShapeDtypeStruct((NH, t, DH), F32)] * 5,
        compiler_params=_cparams(("parallel", "parallel")),
    )(qkv, qkv, qkv, bb, gcb, t_inv, *cts, dgcb_state)


def _delta_state_fwd(local, gcb):
    t = gcb.shape[1]
    nch = t // CH

    def body(u_ref, w_ref, qd_ref, kd_ref, a_ref, g_ref, o_ref, s_ref, st_ref):
        @pl.when(pl.program_id(0) == 0)
        def _():
            st_ref[...] = jnp.zeros_like(st_ref)

        s_ref[...] = st_ref[...]
        items = [tuple(ref[h].astype(F32) for ref in (u_ref, w_ref, qd_ref, kd_ref, a_ref, g_ref, st_ref))
                 for h in range(NH)]
        for h, (o, ns) in enumerate(_state_steps(items)):
            o_ref[:, h * DH:(h + 1) * DH] = o
            st_ref[h] = ns

    blk = pl.BlockSpec((NH, CH, DH), lambda n: (0, n, 0))
    return pl.pallas_call(
        body, name="delta_state_fwd", grid=(nch,),
        in_specs=[blk] * 6,
        out_specs=[pl.BlockSpec((CH, D), lambda n: (n, 0)),
                   pl.BlockSpec((NH, None, DH, DH), lambda n: (0, n, 0, 0))],
        out_shape=[jax.ShapeDtypeStruct((t, D), F32), jax.ShapeDtypeStruct((NH, nch, DH, DH), F32)],
        scratch_shapes=[pltpu.VMEM((NH, DH, DH), F32)],
        compiler_params=_cparams(("arbitrary",)),
    )(*local, gcb)


def _delta_state_bwd(local, gcb, states, do):
    t = gcb.shape[1]
    nch = t // CH

    def body(u_ref, w_ref, qd_ref, kd_ref, a_ref, g_ref, s_ref, do_ref,
             du_ref, dw_ref, dqd_ref, dkd_ref, da_ref, dg_ref, ds_ref):
        @pl.when(pl.program_id(0) == 0)
        def _():
            ds_ref[...] = jnp.zeros_like(ds_ref)

        items = [tuple(ref[h].astype(F32) for ref in (u_ref, w_ref, qd_ref, kd_ref, a_ref, g_ref, s_ref))
                 for h in range(NH)]
        cts = [(do_ref[:, h * DH:(h + 1) * DH], ds_ref[h]) for h in range(NH)]
        for h, (du, dw, dqd, dkd, da, dg, ds) in enumerate(_side_by_side_vjp(_state_steps, items, cts)):
            du_ref[h] = du
            dw_ref[h] = dw
            dqd_ref[h] = dqd
            dkd_ref[h] = dkd
            da_ref[h] = da
            dg_ref[h] = dg
            ds_ref[h] = ds

    blk = pl.BlockSpec((NH, CH, DH), lambda n: (0, nch - 1 - n, 0))
    return pl.pallas_call(
        body, name="delta_state_bwd", grid=(nch,),
        in_specs=[blk] * 6 + [pl.BlockSpec((NH, None, DH, DH), lambda n: (0, nch - 1 - n, 0, 0)),
                              pl.BlockSpec((CH, D), lambda n: (nch - 1 - n, 0))],
        out_specs=[blk] * 6,
        out_shape=[jax.ShapeDtypeStruct((NH, t, DH), F32)] * 6,
        scratch_shapes=[pltpu.VMEM((NH, DH, DH), F32)],
        compiler_params=_cparams(("arbitrary",)),
    )(*local, gcb, states, do)


def _s5_params(ar, ai, ldt, br2, bi2):
    def body(ar_ref, ai_ref, ld_ref, br_ref, bi_ref, lr_ref, li_ref, bbr_ref, bbi_ref):
        lr, li, bbr, bbi = _s5_params_fn(ar_ref[...], ai_ref[...], ld_ref[...], br_ref[...], bi_ref[...])
        lr_ref[...] = lr
        li_ref[...] = li
        bbr_ref[...] = bbr
        bbi_ref[...] = bbi

    sq = pl.BlockSpec((None, NG, NS), lambda l: (l, 0, 0))
    wide = pl.BlockSpec((None, NG, NS * GS), lambda l: (l, 0, 0))
    return pl.pallas_call(
        body, name="s5_params", grid=(DEPTH,),
        in_specs=[sq, sq, pl.BlockSpec((None, NG, 1), lambda l: (l, 0, 0)), wide, wide],
        out_specs=[sq, sq, wide, wide],
        out_shape=[jax.ShapeDtypeStruct((DEPTH, NG, NS), F32)] * 2
        + [jax.ShapeDtypeStruct((DEPTH, NG, NS * GS), F32)] * 2,
        compiler_params=_cparams(("parallel",)),
    )(ar, ai, ldt, br2, bi2)


def _s5_params_bwd(ar, ai, ldt, br2, bi2, dlr, dli, dbbr, dbbi):
    def body(ar_ref, ai_ref, ld_ref, br_ref, bi_ref, a_ref, b_ref, c_ref, d_ref,
             dar_ref, dai_ref, dld_ref, dbr_ref, dbi_ref):
        _, vjp = jax.vjp(_s5_params_fn, ar_ref[...], ai_ref[...], ld_ref[...], br_ref[...], bi_ref[...])
        dar, dai, dld, dbr, dbi = vjp((a_ref[...], b_ref[...], c_ref[...], d_ref[...]))
        dar_ref[...] = dar
        dai_ref[...] = dai
        dld_ref[...] = dld
        dbr_ref[...] = dbr
        dbi_ref[...] = dbi

    sq = pl.BlockSpec((None, NG, NS), lambda l: (l, 0, 0))
    col = pl.BlockSpec((None, NG, 1), lambda l: (l, 0, 0))
    wide = pl.BlockSpec((None, NG, NS * GS), lambda l: (l, 0, 0))
    return pl.pallas_call(
        body, name="s5_params_bwd", grid=(DEPTH,),
        in_specs=[sq, sq, col, wide, wide, sq, sq, wide, wide],
        out_specs=[sq, sq, col, wide, wide],
        out_shape=[jax.ShapeDtypeStruct((DEPTH, NG, NS), F32)] * 2 + [jax.ShapeDtypeStruct((DEPTH, NG, 1), F32)]
        + [jax.ShapeDtypeStruct((DEPTH, NG, NS * GS), F32)] * 2,
        compiler_params=_cparams(("parallel",)),
    )(ar, ai, ldt, br2, bi2, dlr, dli, dbbr, dbbi)


def _s5_tile_rows(t):
    return _tile(t // 2, 1024)


def _s5_fwd(proj, lam, bblk, cblk, exchange=None):
    t = proj.shape[0]
    r = _s5_tile_rows(t)
    nt = t // r
    u0 = 4 * D // DH

    def body(u_ref, lam_ref, b_ref, c_ref, y_ref, car_ref, st_ref, hr_ref, hi_ref, cr_ref, ci_ref):
        @pl.when(pl.program_id(1) == 0)
        def _():
            st_ref[...] = jnp.zeros_like(st_ref)

        car_ref[...] = st_ref[...]
        hr, hi = _s5_states(u_ref[...], lam_ref, b_ref, st_ref, hr_ref, hi_ref, cr_ref, ci_ref)
        y_ref[...] = _mm(hr, c_ref[0]) - _mm(hi, c_ref[1])
        st_ref[0:1, :] = _scratch_row(hr_ref, r - 1)
        st_ref[1:2, :] = _scratch_row(hi_ref, r - 1)

    scratch = ([pltpu.VMEM((8, SW), F32)] + [pltpu.VMEM((SW // DH, r, DH), F32)] * 2
               + [pltpu.VMEM((r // SUB, SW), F32)] * 2)
    (y, carries), fetched = _call_carrying(
        body, "s5_fwd", (NCB, nt),
        [pl.BlockSpec((r, DH), lambda c, i: (i, u0 + c)), pl.BlockSpec((2, 1, SW), lambda c, i: (0, 0, c)),
         pl.BlockSpec((2, None, DH, SW), lambda c, i: (0, c, 0, 0)),
         pl.BlockSpec((2, None, SW, DH), lambda c, i: (0, c, 0, 0))],
        [pl.BlockSpec((r, DH), lambda c, i: (i, c)), pl.BlockSpec((None, 8, SW), lambda c, i: (i, 0, c))],
        [jax.ShapeDtypeStruct((t, D), F32), jax.ShapeDtypeStruct((nt, 8, NG * NS), F32)],
        scratch, [proj, lam, bblk, cblk], ("parallel", "arbitrary"), exchange)
    return y, carries, fetched


def _s5_bwd(proj, lam, bblk, cblk, carries, dy, du_skip, exchange=None):
    t = proj.shape[0]
    r = _s5_tile_rows(t)
    nt = t // r
    u0 = 4 * D // DH

    def body(u_ref, lam_ref, b_ref, c_ref, car_ref, dy_ref, dus_ref, du_ref, dlam_ref, db_ref, dc_ref, dst_ref,
             hr_ref, hi_ref, ar_ref, ai_ref, cr_ref, ci_ref):
        first = pl.program_id(1) == 0

        @pl.when(first)
        def _():
            dst_ref[...] = jnp.zeros_like(dst_ref)

        u, dy = u_ref[...], dy_ref[...]
        lr, li = lam_ref[0], lam_ref[1]
        hr, hi = _s5_states(u, lam_ref, b_ref, car_ref, hr_ref, hi_ref, cr_ref, ci_ref)
        dcr2, dci2 = _mm_tn(hr, dy), -_mm_tn(hi, dy)
        fold = _cmul(lr, -li, dst_ref[0:1, :], dst_ref[1:2, :])
        ar, ai = _scan_tile(_mm_nt(dy, c_ref[0]), -_mm_nt(dy, c_ref[1]), lr, -li, ar_ref, ai_ref, cr_ref, ci_ref,
                            True, fold)
        top = _rows((r, SW)) == 0
        dst_ref[0:1, :] = _scratch_row(ar_ref, 0)
        dst_ref[1:2, :] = _scratch_row(ai_ref, 0)
        du_ref[...] = (_mm_nt(ar, b_ref[0]) + _mm_nt(ai, b_ref[1]) + dus_ref[...]).astype(du_ref.dtype)
        dbr, dbi = _mm_tn(u, ar), _mm_tn(u, ai)
        pr = _sd(hr, 1) + jnp.where(top, car_ref[0:1, :], 0.0)
        pi = _sd(hi, 1) + jnp.where(top, car_ref[1:2, :], 0.0)
        dlr = jnp.sum(ar * pr + ai * pi, axis=0, keepdims=True)
        dli = jnp.sum(ai * pr - ar * pi, axis=0, keepdims=True)

        @pl.when(first)
        def _():
            dlam_ref[0] = dlr
            dlam_ref[1] = dli
            db_ref[0] = dbr
            db_ref[1] = dbi
            dc_ref[0] = dcr2
            dc_ref[1] = dci2

        @pl.when(jnp.logical_not(first))
        def _():
            dlam_ref[0] += dlr
            dlam_ref[1] += dli
            db_ref[0] += dbr
            db_ref[1] += dbi
            dc_ref[0] += dcr2
            dc_ref[1] += dci2

    grid = (NCB, nt)
    in_specs = [pl.BlockSpec((r, DH), lambda c, i: (nt - 1 - i, u0 + c)),
                pl.BlockSpec((2, 1, SW), lambda c, i: (0, 0, c)),
                pl.BlockSpec((2, None, DH, SW), lambda c, i: (0, c, 0, 0)),
                pl.BlockSpec((2, None, SW, DH), lambda c, i: (0, c, 0, 0)),
                pl.BlockSpec((None, 8, SW), lambda c, i: (nt - 1 - i, 0, c)),
                pl.BlockSpec((r, DH), lambda c, i: (nt - 1 - i, c)),
                pl.BlockSpec((r, DH), lambda c, i: (nt - 1 - i, c))]
    out_specs = [pl.BlockSpec((r, DH), lambda c, i: (nt - 1 - i, c)),
                 pl.BlockSpec((2, 1, SW), lambda c, i: (0, 0, c)),
                 pl.BlockSpec((2, None, DH, SW), lambda c, i: (0, c, 0, 0)),
                 pl.BlockSpec((2, None, SW, DH), lambda c, i: (0, c, 0, 0))]
    out_shape = [jax.ShapeDtypeStruct((t, D), GRAD_ACT), jax.ShapeDtypeStruct((2, 1, NG * NS), F32),
                 jax.ShapeDtypeStruct((2, NCB, DH, SW), F32), jax.ShapeDtypeStruct((2, NCB, SW, DH), F32)]
    scratch = ([pltpu.VMEM((8, SW), F32)] + [pltpu.VMEM((SW // DH, r, DH), F32)] * 4
               + [pltpu.VMEM((r // SUB, SW), F32)] * 2)
    args, sem = [proj, lam, bblk, cblk, carries, dy, du_skip], ("parallel", "arbitrary")
    if exchange is not None:
        body = _carry(body, len(args), len(out_shape), len(scratch), exchange, grid)
        in_specs, out_specs = in_specs + exchange.in_specs, out_specs + exchange.out_specs
        out_shape, scratch, args = out_shape + exchange.out_shape, scratch + exchange.scratch_shapes, args + exchange.srcs
        sem = ("arbitrary", "arbitrary")
    outs = pl.pallas_call(
        body, name="s5_bwd", grid=grid, in_specs=in_specs, out_specs=out_specs, out_shape=out_shape,
        scratch_shapes=scratch, compiler_params=_cparams(sem),
    )(*args)
    return outs[:4], outs[4:]


def _proj_spec(tm, col):
    return pl.BlockSpec((tm, D), lambda i: (i, col))


def _layer_mat(l):
    return pl.BlockSpec((None, D, D), lambda i: (l, 0, 0))


def _mix_fwd(proj, o, s5y, x, hn, dvec, wglu, bglu, wout, npost, l):
    t = x.shape[0]
    tm = _tile(t, 256)

    def body(za_ref, u_ref, zb_ref, ra_ref, rb_ref, o_ref, y_ref, x_ref, hn_ref, d_ref, wg_ref, bg_ref, wo_ref,
             np_ref, xn_ref):
        y0 = _mix_pre(y_ref[...], u_ref[...], d_ref[...])
        gl = _mm(y0, wg_ref[...]) + bg_ref[...]
        m = _mix_mid(o_ref[...], za_ref[...], y0, gl, zb_ref[...], ra_ref[...], rb_ref[...], hn_ref[...])
        out = _mm(m, wo_ref[...])
        xn_ref[...] = _mix_post(x_ref[...], out, np_ref[...])

    act = pl.BlockSpec((tm, D), lambda i: (i, 0))
    return pl.pallas_call(
        body, name="mix_fwd", grid=(t // tm,),
        in_specs=[_proj_spec(tm, 3), _proj_spec(tm, 4), _proj_spec(tm, 5), _proj_spec(tm, 6), _proj_spec(tm, 7),
                  act, act, act, _full((1, DH)), _full((1, D)), _layer_mat(l), _full((1, D)), _layer_mat(l),
                  _full((1, D))],
        out_specs=act,
        out_shape=jax.ShapeDtypeStruct((t, D), F32),
        compiler_params=_cparams(("parallel",)),
    )(proj, proj, proj, proj, proj, o, s5y, x, hn, dvec, wglu, bglu, wout, npost)


def _mix_bwd(proj, o, s5y, x, hn, dvec, wglu, bglu, wout, npost, dxn, l):
    t = x.shape[0]
    tm = _tile(t, 128)

    def body(za_ref, u_ref, zb_ref, ra_ref, rb_ref, o_ref, y_ref, x_ref, hn_ref, d_ref, wg_ref, bg_ref, wo_ref,
             np_ref, dxn_ref,
             dza_ref, du_ref, dzb_ref, dra_ref, drb_ref, do_ref, dy_ref, dx_ref,
             y0_ref, dgl_ref, m_ref, dout_ref, dvecs_ref, dhn_ref):
        y0, vjp_pre = jax.vjp(_mix_pre, y_ref[...], u_ref[...], d_ref[...])
        gl = _mm(y0, wg_ref[...]) + bg_ref[...]
        m, vjp_mid = jax.vjp(_mix_mid, o_ref[...], za_ref[...], y0, gl, zb_ref[...], ra_ref[...], rb_ref[...],
                             hn_ref[...])
        out = _mm(m, wo_ref[...])
        _, vjp_post = jax.vjp(_mix_post, x_ref[...], out, np_ref[...])
        dx, dout, dnp = vjp_post(dxn_ref[...])
        dm = _mm_nt(dout, wo_ref[...])
        do, dza, dy0, dgl, dzb, dra, drb, dhn = vjp_mid(dm)
        y0_ref[...] = y0.astype(BF16)
        dgl_ref[...] = dgl.astype(BF16)
        m_ref[...] = m.astype(BF16)
        dout_ref[...] = dout.astype(BF16)
        dbg = jnp.sum(dgl, axis=0, keepdims=True)
        dy0 = dy0 + _mm_nt(dgl, wg_ref[...])
        dy, du, dd = vjp_pre(dy0)
        dza_ref[...] = dza.astype(dza_ref.dtype)
        du_ref[...] = du
        dzb_ref[...] = dzb.astype(dzb_ref.dtype)
        dra_ref[...] = dra.astype(dra_ref.dtype)
        drb_ref[...] = drb.astype(drb_ref.dtype)
        do_ref[...] = do
        dy_ref[...] = dy
        dx_ref[...] = dx
        first = pl.program_id(0) == 0

        @pl.when(first)
        def _():
            dvecs_ref[...] = jnp.zeros_like(dvecs_ref)
            dhn_ref[...] = jnp.zeros_like(dhn_ref)

        dvecs_ref[0:1, :] += dd
        dvecs_ref[1:2, :] += dbg
        dvecs_ref[2:3, :] += dnp
        dhn_ref[0:1, :] += dhn

    act = pl.BlockSpec((tm, D), lambda i: (i, 0))
    a, ga = jax.ShapeDtypeStruct((t, D), F32), jax.ShapeDtypeStruct((t, D), GRAD_ACT)
    b16 = jax.ShapeDtypeStruct((t, D), BF16)
    outs = pl.pallas_call(
        body, name="mix_bwd", grid=(t // tm,),
        in_specs=[_proj_spec(tm, 3), _proj_spec(tm, 4), _proj_spec(tm, 5), _proj_spec(tm, 6), _proj_spec(tm, 7),
                  act, act, act, _full((1, DH)), _full((1, D)), _layer_mat(l), _full((1, D)), _layer_mat(l),
                  _full((1, D)), act],
        out_specs=[act] * 12 + [_full((8, D)), _full((8, DH))],
        out_shape=[ga, a, ga, ga, ga, a, a, a, b16, b16, b16, b16, jax.ShapeDtypeStruct((8, D), F32),
                   jax.ShapeDtypeStruct((8, DH), F32)],
        compiler_params=_cparams(("arbitrary",)),
    )(proj, proj, proj, proj, proj, o, s5y, x, hn, dvec, wglu, bglu, wout, npost, dxn)
    y0, dgl, m, dout = outs[8:12]
    return list(outs[:8]) + [_weight_grad(y0, dgl, "glu_dw"), _weight_grad(m, dout, "out_dw")] + list(outs[12:])


def _weight_grad(a, b, name):
    t = a.shape[0]
    tk = _tile(t, 1024)

    def body(a_ref, b_ref, o_ref):
        @pl.when(pl.program_id(0) == 0)
        def _():
            o_ref[...] = jnp.zeros_like(o_ref)

        o_ref[...] += _mm_tn(a_ref[...], b_ref[...])

    rows = pl.BlockSpec((tk, D), lambda i: (i, 0))
    return pl.pallas_call(
        body, name=name, grid=(t // tk,), in_specs=[rows, rows], out_specs=_full((D, D)),
        out_shape=jax.ShapeDtypeStruct((D, D), F32), compiler_params=_cparams(("arbitrary",)),
    )(a, b)


def _loss_grad(y, target):
    t = y.shape[0]
    tm = _tile(t, 512)

    def body(y_ref, t_ref, dy_ref, l_ref):
        err = y_ref[...] - t_ref[...]
        dy_ref[...] = err * (1.0 / D)
        part = jnp.sum(jnp.sum(err * err, axis=1, keepdims=True), axis=0, keepdims=True) * (0.5 / D)
        part = jnp.broadcast_to(part, (8, DH))

        @pl.when(pl.program_id(0) == 0)
        def _():
            l_ref[...] = part

        @pl.when(pl.program_id(0) > 0)
        def _():
            l_ref[...] += part

    act = pl.BlockSpec((tm, D), lambda i: (i, 0))
    return pl.pallas_call(
        body, name="loss_grad", grid=(t // tm,),
        in_specs=[act, act], out_specs=[act, _full((8, DH))],
        out_shape=[jax.ShapeDtypeStruct((t, D), F32), jax.ShapeDtypeStruct((8, DH), F32)],
        compiler_params=_cparams(("arbitrary",)),
    )(y, target)


def _flips(rel):
    x, y, c = lax.axis_index("x"), lax.axis_index("y"), lax.axis_index("c")
    fx, fy, fc = rel
    return (x ^ fx if fx else x, y ^ fy if fy else y, c ^ fc if fc else c)


CHIP_RELS = ((1, 0, 0), (0, 1, 0), (1, 1, 0))
ALL_RELS = tuple((fx, fy, fc) for fx in (0, 1) for fy in (0, 1) for fc in (0, 1) if (fx, fy, fc) != (0, 0, 0))


def _slot_of(pos, by_chip):
    px, py, pc = pos
    return 2 * px + py if by_chip else 4 * px + 2 * py + pc


class _Exchange:
    def __init__(self, srcs, rels, by_chip, scatter):
        self.srcs, self.rels, self.by_chip, self.scatter = list(srcs), rels, by_chip, scatter
        self.narr = len(self.srcs)
        nslot, nsem = NCHIP if by_chip else NDEV, self.narr * len(rels)
        self.in_specs = [pl.BlockSpec(memory_space=pl.ANY)] * self.narr
        self.out_specs = [pl.BlockSpec(memory_space=pl.ANY)] * self.narr
        self.out_shape = [jax.ShapeDtypeStruct((nslot,) + s.shape[-2:], s.dtype) for s in self.srcs]
        self.scratch_shapes = [pltpu.SemaphoreType.DMA((nsem,)), pltpu.SemaphoreType.DMA((nsem,)),
                               pltpu.SemaphoreType.DMA((self.narr,))]

    def _copies(self, src_refs, dst_refs, sems):
        send_sems, recv_sems, local_sems = sems
        my_slot = _slot_of(_flips((0, 0, 0)), self.by_chip)
        local, sends, arrivals = [], [], []
        for a, (src_ref, dst_ref) in enumerate(zip(src_refs, dst_refs)):
            local.append(pltpu.make_async_copy(src_ref.at[my_slot] if self.scatter else src_ref, dst_ref.at[my_slot],
                                               local_sems.at[a]))
            for k, rel in enumerate(self.rels):
                peer = _flips(rel)
                pair = dict(send_sem=send_sems.at[a * len(self.rels) + k], recv_sem=recv_sems.at[a * len(self.rels) + k],
                            device_id=peer, device_id_type=pl.DeviceIdType.MESH)
                part = src_ref.at[_slot_of(peer, self.by_chip)] if self.scatter else src_ref
                sends.append(pltpu.make_async_remote_copy(src_ref=part, dst_ref=dst_ref.at[my_slot], **pair))
                arrivals.append(pltpu.make_async_remote_copy(
                    src_ref=src_ref.at[0] if self.scatter else src_ref, dst_ref=dst_ref.at[_slot_of(peer, self.by_chip)],
                    **pair))
        return local, sends, arrivals

    def start(self, src_refs, dst_refs, sems):
        local, sends, _ = self._copies(src_refs, dst_refs, sems)
        for cp in local + sends:
            cp.start()

    def wait(self, src_refs, dst_refs, sems):
        local, sends, arrivals = self._copies(src_refs, dst_refs, sems)
        for cp in arrivals:
            cp.wait_recv()
        for cp in sends:
            cp.wait_send()
        for cp in local:
            cp.wait()


def _exchange(srcs, rels, by_chip, scatter, name):
    ex = _Exchange(srcs, rels, by_chip, scatter)

    def body(*refs):
        parts = refs[:ex.narr], refs[ex.narr:2 * ex.narr], refs[2 * ex.narr:]
        ex.start(*parts)
        ex.wait(*parts)

    return pl.pallas_call(body, name=name, in_specs=ex.in_specs, out_specs=ex.out_specs, out_shape=ex.out_shape,
                          scratch_shapes=ex.scratch_shapes)(*ex.srcs)


def _sibling_swap(srcs, name):
    narr = len(srcs)

    def body(*refs):
        src_refs, dst_refs = refs[:narr], refs[narr:2 * narr]
        send_sems, recv_sems = refs[2 * narr:]
        peer = _flips((0, 0, 1))
        copies = [pltpu.make_async_remote_copy(src_ref=s, dst_ref=d, send_sem=send_sems.at[a], recv_sem=recv_sems.at[a],
                                               device_id=peer, device_id_type=pl.DeviceIdType.MESH)
                  for a, (s, d) in enumerate(zip(src_refs, dst_refs))]
        for cp in copies:
            cp.start()
        for cp in copies:
            cp.wait()

    return pl.pallas_call(
        body, name=name,
        in_specs=[pl.BlockSpec(memory_space=pl.ANY)] * narr,
        out_specs=[pl.BlockSpec(memory_space=pl.ANY)] * narr,
        out_shape=[jax.ShapeDtypeStruct(s.shape, s.dtype) for s in srcs],
        scratch_shapes=[pltpu.SemaphoreType.DMA((narr,)), pltpu.SemaphoreType.DMA((narr,))],
    )(*srcs)


def _all_reduce(src, name):
    rows, cols = src.shape
    r = rows // NDEV
    nrel = len(ALL_RELS)

    def body(src_ref, out_ref, parts_ref, mine_ref, send_sems, recv_sems):
        my_slot = _slot_of(_flips((0, 0, 0)), False)

        def piece(ref, slot):
            return ref.at[pl.ds(pl.multiple_of(slot * r, 8), r), :]

        def copies(phase):
            out = []
            for k, rel in enumerate(ALL_RELS):
                peer = _flips(rel)
                pair = dict(send_sem=send_sems.at[phase * nrel + k], recv_sem=recv_sems.at[phase * nrel + k],
                            device_id=peer, device_id_type=pl.DeviceIdType.MESH)
                if phase == 0:
                    out.append(pltpu.make_async_remote_copy(src_ref=piece(src_ref, _slot_of(peer, False)),
                                                            dst_ref=parts_ref.at[my_slot], **pair))
                else:
                    out.append(pltpu.make_async_remote_copy(src_ref=mine_ref, dst_ref=piece(out_ref, my_slot), **pair))
            return out

        first = copies(0)
        for cp in first:
            cp.start()
        parts_ref[my_slot] = piece(src_ref, my_slot)[...]
        for cp in first:
            cp.wait_recv()
        acc = parts_ref[0]
        for s in range(1, NDEV):
            acc = acc + parts_ref[s]
        mine_ref[...] = acc
        second = copies(1)
        for cp in second:
            cp.start()
        piece(out_ref, my_slot)[...] = acc
        for cp in second:
            cp.wait_recv()
        for cp in first + second:
            cp.wait_send()

    return pl.pallas_call(
        body, name=name,
        in_specs=[pl.BlockSpec(memory_space=pltpu.VMEM)], out_specs=pl.BlockSpec(memory_space=pltpu.VMEM),
        out_shape=jax.ShapeDtypeStruct(src.shape, src.dtype),
        scratch_shapes=[pltpu.VMEM((NDEV, r, cols), src.dtype), pltpu.VMEM((r, cols), src.dtype),
                        pltpu.SemaphoreType.DMA((2 * nrel,)), pltpu.SemaphoreType.DMA((2 * nrel,))],
        compiler_params=pltpu.CompilerParams(vmem_limit_bytes=VMEM_LIMIT),
    )(src)


def _sum_slots(parts, name):
    ns, rows, cols = parts.shape
    tr = _row_tile(rows, 256)

    def body(p_ref, o_ref):
        acc = p_ref[0].astype(F32)
        for s in range(1, ns):
            acc = acc + p_ref[s].astype(F32)
        o_ref[...] = acc

    return pl.pallas_call(
        body, name=name, grid=(rows // tr,),
        in_specs=[pl.BlockSpec((ns, tr, cols), lambda i: (0, i, 0))],
        out_specs=pl.BlockSpec((tr, cols), lambda i: (i, 0)),
        out_shape=jax.ShapeDtypeStruct((rows, cols), F32),
        compiler_params=_cparams(("parallel",)),
    )(parts)


def _adamw(w, g_parts, m, v, name, max_rows=256):
    if w.ndim == 2:
        return [o[0] for o in _adamw(w[None], g_parts, m[None], v[None], name, max_rows)]
    nl, rows, cols = w.shape
    tr = _row_tile(rows, max_rows)
    per_layer = rows // tr
    c1 = 1.0 / (1.0 - ADAM_B1 ** ADAM_STEP)
    c2 = 1.0 / (1.0 - ADAM_B2 ** ADAM_STEP)
    npart = len(g_parts)

    def body(*refs):
        w_ref, m_ref, v_ref = refs[:3]
        g_refs = refs[3:3 + npart]
        go_ref, d_ref, nm_ref, nv_ref = refs[3 + npart:]
        terms = []
        for g_ref in g_refs:
            terms += [g_ref[...]] if len(g_ref.shape) == 2 else [g_ref[s] for s in range(g_ref.shape[0])]
        g = terms[0]
        for term in terms[1:]:
            g = g + term
        nm = ADAM_B1 * m_ref[...] + (1.0 - ADAM_B1) * g
        nv = ADAM_B2 * v_ref[...] + (1.0 - ADAM_B2) * (g * g)
        d_ref[...] = -ADAM_LR * ((nm * c1) / (jnp.sqrt(nv * c2) + ADAM_EPS) + ADAM_WD * w_ref[...])
        go_ref[...] = g
        nm_ref[...] = nm
        nv_ref[...] = nv

    blk = pl.BlockSpec((None, tr, cols), lambda l, i: (l, i, 0))
    g_specs = [pl.BlockSpec((tr, cols), lambda l, i: (l * per_layer + i, 0)) if p.ndim == 2 else
               pl.BlockSpec((p.shape[0], tr, cols), lambda l, i: (0, l * per_layer + i, 0)) for p in g_parts]
    out = jax.ShapeDtypeStruct((nl, rows, cols), F32)
    return pl.pallas_call(
        body, name=name, grid=(nl, per_layer),
        in_specs=[blk, blk, blk] + g_specs,
        out_specs=[blk] * 4, out_shape=[out] * 4,
        compiler_params=_cparams(("parallel", "parallel")),
    )(w, m, v, *g_parts)


WEIGHT_SPLIT = (0, 384, 704, D)
WIN_SHARD = 2052
CONV_SHARD = 768
ROW_SHARD = 256

SMALL = (("norm_pre", (DEPTH, D)), ("a_log", (DEPTH, NH)), ("dt_bias", (DEPTH, NH)), ("head_norm", (DEPTH, DH)),
         ("ssm_a_re", (DEPTH, NG, NS)), ("ssm_a_im", (DEPTH, NG, NS)), ("ssm_log_dt", (DEPTH, NG)),
         ("ssm_b_re", (DEPTH, NG, NS, GS)), ("ssm_b_im", (DEPTH, NG, NS, GS)),
         ("ssm_c_re", (DEPTH, NG, GS, NS)), ("ssm_c_im", (DEPTH, NG, GS, NS)), ("ssm_d", (DEPTH, D)),
         ("b_glu", (DEPTH, D)), ("norm_post", (DEPTH, D)))


def _pad_rows(flat, rows):
    return jnp.pad(flat, (0, rows * D - flat.shape[0])).reshape(rows, D)


def _cols_from_chips(a, nl):
    _, rows, cols = a.shape
    return a.reshape(NCHIP, nl, rows // nl, cols).transpose(1, 2, 0, 3).reshape(nl, rows // nl, NCHIP * cols)


SMALL_ROWS = sum(-(-math.prod(s) // (8 * D)) * 8 for _, s in SMALL)
CONV_ROWS = DEPTH * 4 * 3 * D // D


def _pack_small(vals, extra=()):
    parts = []
    for val in tuple(vals) + tuple(extra):
        n = val.size
        parts.append(_pad_rows(val.reshape(-1), -(-n // (8 * D)) * 8))
    return jnp.concatenate(parts, axis=0)


def _unpack_small(flat):
    outs, r0 = [], 0
    for _, shape in SMALL:
        n = math.prod(shape)
        rows = -(-n // (8 * D)) * 8
        outs.append(flat[r0:r0 + rows].reshape(-1)[:n].reshape(shape))
        r0 += rows
    return outs


LOGITS_IN_CHIP1 = 2 * WIN_SHARD - 4 * D
LOGITS_IN_CHIP2 = 2 * NH - LOGITS_IN_CHIP1


def _wcat_from_chips(g):
    before = WIN_SHARD - LOGITS_IN_CHIP1
    pad = jnp.zeros((D, NCOL - BD0 - 2 * NH), g.dtype)
    return jnp.concatenate([g[0], g[1][:, :before], g[2][:, LOGITS_IN_CHIP2:], g[3], g[1][:, before:],
                            g[2][:, :LOGITS_IN_CHIP2], pad], axis=1)[None]


def _wcat_grad_by_chip(gw):
    before, mid = WIN_SHARD - LOGITS_IN_CHIP1, 4 * D + WIN_SHARD - LOGITS_IN_CHIP2
    return jnp.stack([gw[:, :WIN_SHARD],
                      jnp.concatenate([gw[:, WIN_SHARD:4 * D], gw[:, BD0:BD0 + LOGITS_IN_CHIP1]], axis=1),
                      jnp.concatenate([gw[:, BD0 + LOGITS_IN_CHIP1:BD0 + 2 * NH], gw[:, 4 * D:mid]], axis=1),
                      gw[:, mid:BD0]])


def _block_diag_b(bb2):
    b = bb2.reshape(-1, NCB, GPB, NS, GS)
    return jnp.einsum("lkgnc,gh->lkgchn", b, jnp.eye(GPB, dtype=F32)).reshape(-1, NCB, GPB * GS, SW)


def _block_diag_b_t(d):
    blocks = jnp.einsum("lkgchn,gh->lkgnc", d.reshape(-1, NCB, GPB, GS, GPB, NS), jnp.eye(GPB, dtype=F32))
    return blocks.reshape(-1, NG, NS * GS)


def _block_diag_c(c):
    blocks = jnp.einsum("lkgcn,gh->lkgnhc", c.reshape(-1, NCB, GPB, GS, NS), jnp.eye(GPB, dtype=F32))
    return blocks.reshape(-1, NCB, SW, GPB * GS)


def _block_diag_c_t(d):
    blocks = jnp.einsum("lkgnhc,gh->lkgcn", d.reshape(-1, NCB, GPB, NS, GPB, GS), jnp.eye(GPB, dtype=F32))
    return blocks.reshape(-1, NG, GS, NS)


def _local_step(x, target, weights, conv, small, comm=None):
    weights = list(weights) + [None] * (DEPTH - len(weights))
    ar, ai = small["ssm_a_re"], small["ssm_a_im"]
    ldt = small["ssm_log_dt"].reshape(DEPTH, NG, 1)
    br2 = small["ssm_b_re"].reshape(DEPTH, NG, NS * GS)
    bi2 = small["ssm_b_im"].reshape(DEPTH, NG, NS * GS)
    lr, li, bbr2, bbi2 = _s5_params(ar, ai, ldt, br2, bi2)

    def row(name, l, width):
        return small[name][l].reshape(1, width)

    gvecs = jnp.pad(jnp.stack([small["a_log"], small["dt_bias"]], axis=1), ((0, 0), (0, 6), (NH, DH - 2 * NH)))
    lams = jnp.stack([lr.reshape(DEPTH, 1, NG * NS), li.reshape(DEPTH, 1, NG * NS)], axis=1)
    bblks = jnp.stack([_block_diag_b(bbr2), _block_diag_b(bbi2)], axis=1)
    cblks = jnp.stack([_block_diag_c(small["ssm_c_re"]), _block_diag_c(small["ssm_c_im"])], axis=1)
    saved = []
    for l in range(DEPTH):
        gvec, lam, bblk, cblk = gvecs[l], lams[l], bblks[l], cblks[l]
        wcat, wglu, wout = weights[l]
        fetch = [None] * 3
        if comm and l + 1 < DEPTH:
            fetch = [_Exchange(comm["weight_parts"](l + 1, part), CHIP_RELS, True, False) for part in range(3)]
        proj, h, got0 = _inproj_fwd(x, row("norm_pre", l, D), wcat, 0, fetch[0])
        qkv = _prep_fwd(proj, conv[l])
        bb, gcb = _gates_fwd(proj, gvec)
        local, t_inv, got1 = _delta_local_fwd(qkv, bb, gcb, fetch[1])
        o, states = _delta_state_fwd(local, gcb)
        s5y, carries, got2 = _s5_fwd(proj, lam, bblk, cblk, fetch[2])
        if fetch[0] is not None:
            weights[l + 1] = comm["weights_from"]([got0, got1, got2])
        xn = _mix_fwd(proj, o, s5y, x, row("head_norm", l, DH), row("ssm_d", l, D), wglu, row("b_glu", l, D),
                      wout, row("norm_post", l, D), 0)
        saved.append((x, proj, h, qkv, bb, gcb, local, t_inv, o, states, s5y, carries, gvec, lam, bblk, cblk))
        x = xn

    dx, loss_part = _loss_grad(x, target)

    g = {k: [None] * DEPTH for k in ("wcat", "conv", "wglu", "wout", "norm_pre", "a_log", "dt_bias", "head_norm",
                                     "ssm_d", "b_glu", "norm_post", "dlam", "dbblk", "dcblk")}
    from_chips, send, send_layer = [None] * DEPTH, None, None
    for l in reversed(range(DEPTH)):
        xl, proj, h, qkv, bb, gcb, local, t_inv, o, states, s5y, carries, gvec, lam, bblk, cblk = saved[l]
        wcat, wglu, wout = weights[l]
        (dza, du_skip, dzb, dra, drb, do, ds5y, dxres, dwg, dwo, dvecs, dhn) = _mix_bwd(
            proj, o, s5y, xl, row("head_norm", l, DH), row("ssm_d", l, D), wglu, row("b_glu", l, D), wout,
            row("norm_post", l, D), dx, 0)
        (du, dlam, dbblk, dcblk), arrived = _s5_bwd(proj, lam, bblk, cblk, carries, ds5y, du_skip, send)
        if send is not None:
            from_chips[send_layer] = arrived
        *dlocal, dgcb_state = _delta_state_bwd(local, gcb, states, do)
        dq, dk, dv, dbb, dgcb = _delta_local_bwd(qkv, bb, gcb, t_inv, dlocal, dgcb_state)
        dbd, dgvec = _gates_bwd(proj, gvec, dbb, dgcb)
        dpre, dconv = _prep_bwd(proj, conv[l], dq, dk, dv)
        dproj = jnp.concatenate([dpre, dza, du, dzb, dra, drb, dbd], axis=1)
        g["wcat"][l] = _inproj_bwd_dw(h, dproj)
        g["conv"][l], g["wglu"][l], g["wout"][l] = dconv, dwg, dwo
        send = _Exchange(comm["grad_parts"](g["wcat"][l], dwg, dwo), CHIP_RELS, True, True) if comm else None
        dx, dgain, arrived = _inproj_bwd_dx(dproj, wcat, xl, row("norm_pre", l, D), dxres, 0, send if l == 0 else None)
        if comm and l == 0:
            from_chips[l] = arrived
        send_layer = l
        g["norm_pre"][l] = dgain[0]
        g["a_log"][l], g["dt_bias"][l] = dgvec[0, NH:2 * NH], dgvec[1, NH:2 * NH]
        g["head_norm"][l] = dhn[0]
        g["ssm_d"][l], g["b_glu"][l], g["norm_post"][l] = dvecs[0], dvecs[1], dvecs[2]
        g["dlam"][l], g["dbblk"][l], g["dcblk"][l] = dlam, dbblk, dcblk
    if comm:
        for k in ("wcat", "wglu", "wout"):
            del g[k]
    g = {k: jnp.stack(v) for k, v in g.items()}
    g["from_chips"] = from_chips
    dlam, dbblk, dcblk = g.pop("dlam"), g.pop("dbblk"), g.pop("dcblk")
    g["ssm_c_re"], g["ssm_c_im"] = _block_diag_c_t(dcblk[:, 0]), _block_diag_c_t(dcblk[:, 1])
    dar, dai, dldt, dbr2, dbi2 = _s5_params_bwd(
        ar, ai, ldt, br2, bi2, dlam[:, 0].reshape(DEPTH, NG, NS), dlam[:, 1].reshape(DEPTH, NG, NS),
        _block_diag_b_t(dbblk[:, 0]), _block_diag_b_t(dbblk[:, 1]))
    g["ssm_a_re"], g["ssm_a_im"], g["ssm_log_dt"] = dar, dai, dldt.reshape(DEPTH, NG)
    g["ssm_b_re"] = dbr2.reshape(DEPTH, NG, NS, GS)
    g["ssm_b_im"] = dbi2.reshape(DEPTH, NG, NS, GS)
    return loss_part[0, 0], dx, g


def kernel(x, norm_pre, w_in, conv_w, a_log, dt_bias, head_norm, ssm_a_re, ssm_a_im, ssm_log_dt, ssm_b_re, ssm_b_im, ssm_c_re, ssm_c_im, ssm_d, w_glu, b_glu, w_out, norm_post, loss_target, m_norm_pre, m_w_in, m_conv_w, m_a_log, m_dt_bias, m_head_norm, m_ssm_a_re, m_ssm_a_im, m_ssm_log_dt, m_ssm_b_re, m_ssm_b_im, m_ssm_c_re, m_ssm_c_im, m_ssm_d, m_w_glu, m_b_glu, m_w_out, m_norm_post, v_norm_pre, v_w_in, v_conv_w, v_a_log, v_dt_bias, v_head_norm, v_ssm_a_re, v_ssm_a_im, v_ssm_log_dt, v_ssm_b_re, v_ssm_b_im, v_ssm_c_re, v_ssm_c_im, v_ssm_d, v_w_glu, v_b_glu, v_w_out, v_norm_post):
    args = dict(locals())
    small = {n: args[n] for n, _ in SMALL}

    def flat2(a):
        return a.reshape(-1, a.shape[-1])

    w_in16, w_glu16, w_out16 = w_in.astype(BF16), w_glu.astype(BF16), w_out.astype(BF16)

    def weight_parts(l, part=None):
        if part is None:
            return [w_in16[l], w_glu16[l], w_out16[l]]
        lo, hi = WEIGHT_SPLIT[part], WEIGHT_SPLIT[part + 1]
        return [w_in16[l, lo:hi]] + [[w_glu16[l]], [w_out16[l]], []][part]

    def weights_from(parts):
        if len(parts) == 3 and isinstance(parts[0], (list, tuple)):
            parts = [jnp.concatenate([p[0] for p in parts], axis=1), parts[0][1], parts[1][1]]
        g_in, g_glu, g_out = parts[:3]
        return _wcat_from_chips(g_in), g_glu.reshape(1, D, D), g_out.reshape(1, D, D)

    def grad_parts(gwcat, gwglu, gwout):
        return [_wcat_grad_by_chip(gwcat).astype(BF16), gwglu.reshape(NCHIP, ROW_SHARD, D).astype(BF16),
                gwout.reshape(NCHIP, ROW_SHARD, D).astype(BF16)]

    first = _exchange(weight_parts(0) + [flat2(conv_w)], CHIP_RELS, True, False, "gather_weights")
    conv = _cols_from_chips(first[3], DEPTH)
    comm = dict(weight_parts=weight_parts, weights_from=weights_from, grad_parts=grad_parts)
    loss_part, dx, g = _local_step(x[0], loss_target[0], [weights_from(first)], conv, small, comm)
    loss = lax.psum(loss_part, ("x", "y", "c"))

    from_chips = [jnp.concatenate([g["from_chips"][l][a] for l in range(DEPTH)], axis=1) for a in range(3)]
    core_sums = [_sum_slots(p, "sum_chips_" + n) for p, n in zip(from_chips, ("in", "glu", "out"))]
    others = _sibling_swap(core_sums, "swap_cores")
    sharded = {}
    for n, mine, other in zip(("w_in", "w_glu", "w_out"), core_sums, others):
        sharded[n] = _adamw(args[n], [mine, other], args["m_" + n], args["v_" + n], "adamw_" + n, max_rows=128)

    pad = jnp.zeros(((-(SMALL_ROWS + CONV_ROWS)) % (8 * NDEV), D), F32)
    small_sum = _all_reduce(_pack_small([g[n] for n, _ in SMALL], extra=[g["conv"], pad]), "reduce_small")
    small_out = _adamw(_pack_small([args[n] for n, _ in SMALL]), [small_sum],
                       _pack_small([args["m_" + n] for n, _ in SMALL]),
                       _pack_small([args["v_" + n] for n, _ in SMALL]), "adamw_small")
    chip = 2 * lax.axis_index("x") + lax.axis_index("y")
    conv_sum = small_sum[SMALL_ROWS:SMALL_ROWS + CONV_ROWS].reshape(DEPTH * 4, 3 * D)
    conv_sum = lax.dynamic_slice_in_dim(conv_sum, chip * CONV_SHARD, CONV_SHARD, axis=1)
    sharded["conv_w"] = _adamw(flat2(conv_w), [conv_sum], flat2(m_conv_w), flat2(v_conv_w), "adamw_conv")

    names = ["norm_pre", "w_in", "conv_w", "a_log", "dt_bias", "head_norm", "ssm_a_re", "ssm_a_im", "ssm_log_dt",
             "ssm_b_re", "ssm_b_im", "ssm_c_re", "ssm_c_im", "ssm_d", "w_glu", "b_glu", "w_out", "norm_post"]
    outs = [loss, dx[None]]
    for i in range(4):
        sm = dict(zip([n for n, _ in SMALL], _unpack_small(small_out[i])))
        outs += [sharded[n][i].reshape(args[n].shape) if n in sharded else sm[n] for n in names]
    return tuple(outs)
```

```python
import functools
import math

import jax
import jax.numpy as jnp
from jax import lax
from jax.experimental import pallas as pl
from jax.experimental.pallas import tpu as pltpu

F32 = jnp.float32
BF16 = jnp.bfloat16
HI = lax.Precision.HIGHEST

D = 1024
NH = 8
DH = 128
CH = 128
NG = 64
GS = 16
NS = 64
GPB = 8
NCB = NG // GPB
SW = GPB * NS
NCOL = 8320
BD0 = 8192
EPS = 1e-6
DEPTH = 4
NCHIP = 4
NDEV = 8
VMEM_LIMIT = 56 * 1024 * 1024
GRAD_ACT = jnp.bfloat16

ADAM_LR = 0.001
ADAM_B1 = 0.9
ADAM_B2 = 0.999
ADAM_EPS = 1e-08
ADAM_WD = 0.01
ADAM_STEP = 10


def _cparams(sem=None):
    return pltpu.CompilerParams(dimension_semantics=sem, vmem_limit_bytes=VMEM_LIMIT)


def _full(shape):
    nd = len(shape)
    return pl.BlockSpec(shape, lambda *_: (0,) * nd)


def _rms(x, gain):
    ms = jnp.mean(x * x, axis=-1, keepdims=True)
    return x * lax.rsqrt(ms + EPS) * gain


def _sigmoid(x):
    return 1.0 / (1.0 + jnp.exp(-x))


def _silu(x):
    return x * _sigmoid(x)


def _softplus(x):
    return jnp.maximum(x, 0.0) + jnp.log(1.0 + jnp.exp(-jnp.abs(x)))


def _gelu(x):
    return 0.5 * x * (1.0 + jnp.tanh(math.sqrt(2.0 / math.pi) * (x + 0.044715 * (x * x * x))))


def _dot_bf16(a, b, dims):
    return lax.dot_general(a.astype(BF16), b.astype(BF16), (dims, ((), ())), preferred_element_type=F32)


def _mm_nt(a, b):
    return _dot_bf16(a, b, ((1,), (1,)))


def _mm_tn(a, b):
    return _dot_bf16(a, b, ((0,), (0,)))


@jax.custom_vjp
def _mm(a, b):
    return _dot_bf16(a, b, ((1,), (0,)))


def _mm_fwd(a, b):
    return _dot_bf16(a, b, ((1,), (0,))), (a, b)


def _mm_bwd(res, ct):
    a, b = res
    return _mm_nt(ct, b).astype(a.dtype), _mm_tn(a, ct).astype(b.dtype)


_mm.defvjp(_mm_fwd, _mm_bwd)


@jax.custom_vjp
def _mm_nt_d(a, b):
    return _mm_nt(a, b)


def _mm_nt_d_bwd(res, ct):
    a, b = res
    return _dot_bf16(ct, b, ((1,), (0,))), _mm_tn(ct, a)


_mm_nt_d.defvjp(lambda a, b: (_mm_nt(a, b), (a, b)), _mm_nt_d_bwd)


@jax.custom_vjp
def _mm_tn_d(a, b):
    return _mm_tn(a, b)


def _mm_tn_d_bwd(res, ct):
    a, b = res
    return _mm_nt(b, ct), _dot_bf16(a, ct, ((1,), (0,)))


_mm_tn_d.defvjp(lambda a, b: (_mm_tn(a, b), (a, b)), _mm_tn_d_bwd)


def _split_bf16(a):
    hi = a.astype(BF16)
    return hi, (a - hi.astype(F32)).astype(BF16)


def _dot3(a, b, dims):
    ah, al = _split_bf16(a)
    bh, bl = _split_bf16(b)

    def dot(x, y):
        return lax.dot_general(x, y, (dims, ((), ())), preferred_element_type=F32)

    return dot(ah, bh) + (dot(ah, bl) + dot(al, bh))


@jax.custom_vjp
def _imm(a, b):
    return _dot3(a, b, ((1,), (0,)))


def _imm_bwd(res, ct):
    a, b = res
    return _dot3(ct, b, ((1,), (1,))), _dot3(a, ct, ((0,), (0,)))


_imm.defvjp(lambda a, b: (_dot3(a, b, ((1,), (0,))), (a, b)), _imm_bwd)


def _hmm(a, b):
    return jnp.dot(a, b, precision=HI, preferred_element_type=F32)


def _rows(shape):
    return lax.broadcasted_iota(jnp.int32, shape, 0)


def _cols(shape):
    return lax.broadcasted_iota(jnp.int32, shape, 1)


def _sd(x, s):
    return jnp.where(_rows(x.shape) >= s, pltpu.roll(x, s, axis=0), 0.0)


def _su(x, s):
    n = x.shape[0]
    return jnp.where(_rows(x.shape) < n - s, pltpu.roll(x, n - s, axis=0), 0.0)


@functools.partial(jax.custom_vjp, nondiff_argnums=(1,))
def _shift_down(x, s):
    return _sd(x, s)


def _shift_down_fwd(x, s):
    return _sd(x, s), None


def _shift_down_bwd(s, _, g):
    return (_su(g, s),)


_shift_down.defvjp(_shift_down_fwd, _shift_down_bwd)


def _last_row(x):
    n = x.shape[0]
    return jnp.sum(jnp.where(_rows(x.shape) == n - 1, x, 0.0), axis=0, keepdims=True)


def _prep_fn(p, w0, w1, w2, w3, qk):
    acc = w3 * p + w2 * _shift_down(p, 1) + w1 * _shift_down(p, 2) + w0 * _shift_down(p, 3)
    a = _silu(acc)
    nrm = lax.rsqrt(jnp.sum(a * a, axis=-1, keepdims=True) + EPS)
    return a * (nrm * qk + (1.0 - qk))


def _gates_fn(bd, av, bv):
    tm = bd.shape[0]
    beta_all = _sigmoid(bd)
    g_all = -jnp.exp(av) * _softplus(bd + bv)
    r, c = _rows((tm, tm)), _cols((tm, tm))
    tri = jnp.where((r // CH == c // CH) & (r >= c), 1.0, 0.0).astype(F32)
    gc_all = _hmm(tri, g_all)
    lane = _cols(bd.shape)
    outs = []
    for h in range(NH):
        b = jnp.sum(jnp.where(lane == h, beta_all, 0.0), axis=1, keepdims=True)
        outs.append(jnp.broadcast_to(b, bd.shape))
    for h in range(NH):
        g = jnp.sum(jnp.where(lane == NH + h, gc_all, 0.0), axis=1, keepdims=True)
        outs.append(jnp.broadcast_to(g, bd.shape))
    return tuple(outs)


INV_BASE = 2


def _merge_mm(a, b):
    return _dot_bf16(a, b, ((1,), (0,)))


def _unit_lower_inv(l_mats):
    n = l_mats[0].shape[0]
    ii, jj = _rows((n, n)), _cols((n, n))
    base = ii // INV_BASE == jj // INV_BASE
    ps = [-jnp.where(base, l_mat, 0.0) for l_mat in l_mats]
    eye = jnp.where(ii == jj, 1.0, 0.0).astype(F32)
    ds = [eye + p for p in ps]
    k = 1
    while 2 * k < INV_BASE:
        ps = [_imm(p, p) for p in ps]
        ds = [d + _imm(d, p) for d, p in zip(ds, ps)]
        k *= 2
    m = INV_BASE
    while m < n:
        pair = (ii // (2 * m) == jj // (2 * m)) & (ii // m > jj // m)
        des = [_merge_mm(d, jnp.where(pair, l_mat, 0.0)) for d, l_mat in zip(ds, l_mats)]
        ds = [d - _merge_mm(de, d) for d, de in zip(ds, des)]
        m *= 2
    return ds


@jax.custom_vjp
def _known_inverse(l_mat, t_inv):
    return t_inv


def _known_inverse_bwd(t_inv, ct):
    d_l = -_dot3(_dot3(t_inv, ct, ((0,), (0,))), t_inv, ((1,), (1,)))
    return d_l, jnp.zeros_like(t_inv)


_known_inverse.defvjp(lambda l_mat, t_inv: (t_inv, t_inv), _known_inverse_bwd)


def _chunk_system(q, k, v, bb, gcb):
    qs = q * (DH ** -0.5)
    kb = k * bb
    eg = jnp.exp(gcb)
    ii, jj = _rows((CH, CH)), _cols((CH, CH))
    decay = jnp.exp(jnp.where(ii >= jj, gcb - gcb.T, -1e30))
    l_mat = jnp.where(ii > jj, _mm_nt_d(kb, k) * decay, 0.0)
    a_qk = _mm_nt_d(qs, k) * decay
    k_dec = k * jnp.exp(_last_row(gcb) - gcb)
    return l_mat, (v * bb, kb * eg, qs * eg, k_dec, a_qk)


def _chunk_solve(t_inv, rest):
    vb, kbe, q_dec, k_dec, a_qk = rest
    return _mm(t_inv, vb), _mm(t_inv, kbe), q_dec, k_dec, a_qk


def _side_by_side_vjp(fn, items, cts):
    n = len(items[0])
    _, vjp = jax.vjp(lambda *flat: fn([flat[i * n:(i + 1) * n] for i in range(len(items))]),
                     *[a for item in items for a in item])
    grads = vjp(cts)
    return [grads[i * n:(i + 1) * n] for i in range(len(items))]


def _chunks_local_known(items):
    systems = [_chunk_system(*item[:5]) for item in items]
    t_invs = [_known_inverse(l_mat, item[5]) for (l_mat, _), item in zip(systems, items)]
    return [_chunk_solve(t_inv, rest) for t_inv, (_, rest) in zip(t_invs, systems)]


def _chunks_local(chunks):
    systems = [_chunk_system(*c) for c in chunks]
    t_invs = _unit_lower_inv([l_mat for l_mat, _ in systems])
    return [(_chunk_solve(t_inv, rest), t_inv) for t_inv, (_, rest) in zip(t_invs, systems)]


def _state_steps(items):
    v_news = [u - _mm(w, state) for u, w, _, _, _, _, state in items]
    outs = [_mm(q_dec, state) + _mm(a_qk, v_new) for (_, _, q_dec, _, a_qk, _, state), v_new in zip(items, v_news)]
    states = [state * jnp.exp(_last_row(gcb)) + _mm_tn_d(k_dec, v_new)
              for (_, _, _, k_dec, _, gcb, state), v_new in zip(items, v_news)]
    return list(zip(outs, states))


SUB = 8


SCAN_ROWS = 32


def _cmul(ar, ai, br, bi):
    return ar * br - ai * bi, ar * bi + ai * br


def _scan_tile(xr, xi, mr, mi, hr_ref, hi_ref, cr_ref, ci_ref, reverse, fold):
    n, width = xr.shape
    ngroups, nlb = n // SUB, width // DH
    shift_groups = _su if reverse else _sd

    def lanes(x, j):
        return x[..., j * DH:(j + 1) * DH]

    start = n - 1 if reverse else 0
    for j in range(nlb):
        hr_ref[j] = lanes(xr, j)
        hi_ref[j] = lanes(xi, j)
        hr_ref[j, start:start + 1, :] += lanes(fold[0], j)
        hi_ref[j, start:start + 1, :] += lanes(fold[1], j)
    pr, pi = mr, mi
    tr, ti = jnp.broadcast_to(mr, (SUB, width)), jnp.broadcast_to(mi, (SUB, width))
    pos = _rows(tr.shape)
    steps = []
    s = 1
    while s < SUB:
        inside = pos < SUB - s if reverse else pos >= s
        shift = SUB - s if reverse else s
        steps.append((shift, jnp.where(inside, pr, 0.0), jnp.where(inside, pi, 0.0)))
        er = jnp.where(inside, pltpu.roll(tr, shift, axis=0), 1.0)
        ei = jnp.where(inside, pltpu.roll(ti, shift, axis=0), 0.0)
        tr, ti = _cmul(tr, ti, er, ei)
        pr, pi = _cmul(pr, pi, pr, pi)
        s *= 2
    for b in range(0, n, SCAN_ROWS):
        rows = slice(b, b + SCAN_ROWS)
        for j in range(nlb):
            br = hr_ref[j, rows, :].reshape(SCAN_ROWS // SUB, SUB, DH)
            bi = hi_ref[j, rows, :].reshape(SCAN_ROWS // SUB, SUB, DH)
            for shift, qr, qi in steps:
                dr, di = _cmul(lanes(qr, j)[None], lanes(qi, j)[None],
                               pltpu.roll(br, shift, axis=1), pltpu.roll(bi, shift, axis=1))
                br, bi = br + dr, bi + di
            hr_ref[j, rows, :] = br.reshape(SCAN_ROWS, DH)
            hi_ref[j, rows, :] = bi.reshape(SCAN_ROWS, DH)
    edge = pl.ds(0 if reverse else SUB - 1, ngroups, stride=SUB)
    gr = jnp.concatenate([hr_ref.at[j][edge, :] for j in range(nlb)], axis=1)
    gi = jnp.concatenate([hi_ref.at[j][edge, :] for j in range(nlb)], axis=1)
    s = 1
    while s < ngroups:
        dr, di = _cmul(pr, pi, shift_groups(gr, s), shift_groups(gi, s))
        gr, gi = gr + dr, gi + di
        pr, pi = _cmul(pr, pi, pr, pi)
        s *= 2
    cr_ref[...] = shift_groups(gr, 1)
    ci_ref[...] = shift_groups(gi, 1)
    for g in range(ngroups):
        rows = slice(g * SUB, (g + 1) * SUB)
        dr, di = _cmul(tr, ti, cr_ref[g:g + 1, :], ci_ref[g:g + 1, :])
        for j in range(nlb):
            hr_ref[j, rows, :] += lanes(dr, j)
            hi_ref[j, rows, :] += lanes(di, j)
    return (jnp.concatenate([hr_ref[j] for j in range(nlb)], axis=1),
            jnp.concatenate([hi_ref[j] for j in range(nlb)], axis=1))


def _s5_states(u, lam_ref, b_ref, car_ref, hr_ref, hi_ref, cr_ref, ci_ref):
    lr, li = lam_ref[0], lam_ref[1]
    fold = _cmul(lr, li, car_ref[0:1, :], car_ref[1:2, :])
    return _scan_tile(_mm(u, b_ref[0]), _mm(u, b_ref[1]), lr, li, hr_ref, hi_ref, cr_ref, ci_ref, False, fold)


def _scratch_row(ref, row):
    return jnp.concatenate([ref[j, row:row + 1, :] for j in range(ref.shape[0])], axis=1)


def _s5_params_fn(ar, ai, ldt, br2, bi2):
    dt = jnp.exp(ldt)
    mag = jnp.exp(ar * dt)
    lr, li = mag * jnp.cos(ai * dt), mag * jnp.sin(ai * dt)
    den = ar * ar + ai * ai
    fr = ((lr - 1.0) * ar + li * ai) / den
    fi = (li * ar - (lr - 1.0) * ai) / den
    expand = jnp.where(_cols((NS, NS * GS)) // GS == _rows((NS, NS * GS)), 1.0, 0.0).astype(F32)
    fr2, fi2 = _hmm(fr, expand), _hmm(fi, expand)
    return lr, li, fr2 * br2 - fi2 * bi2, fr2 * bi2 + fi2 * br2


def _head_norm(o, hn):
    parts = []
    for h in range(NH):
        oh = o[:, h * DH:(h + 1) * DH]
        parts.append(oh * lax.rsqrt(jnp.mean(oh * oh, axis=-1, keepdims=True) + EPS) * hn)
    return jnp.concatenate(parts, axis=1)


def _mix_pre(s5y, u, dvec):
    return _gelu(s5y + dvec * u)


def _mix_mid(o, za, y0, gl, zb, ra, rb, hn):
    ya = _head_norm(o, hn) * _silu(za)
    yb = y0 * _sigmoid(gl) * _silu(zb)
    return _sigmoid(ra) * ya + _sigmoid(rb) * yb


def _mix_post(x, out, npost):
    return x + _rms(out, npost)


def _tile(t, want):
    return min(t, want)


def _row_tile(rows, want):
    return max(r for r in range(16, want + 1, 16) if rows % r == 0)


def _call_carrying(body, name, grid, in_specs, out_specs, out_shape, scratch, args, semantics, exchange):
    n_out = len(out_shape)
    if exchange is not None:
        body = _carry(body, len(args), n_out, len(scratch), exchange, grid)
        in_specs, out_specs = in_specs + exchange.in_specs, out_specs + exchange.out_specs
        out_shape, scratch, args = out_shape + exchange.out_shape, scratch + exchange.scratch_shapes, args + exchange.srcs
        semantics = ("arbitrary",) * len(grid)
    outs = pl.pallas_call(body, name=name, grid=grid, in_specs=in_specs, out_specs=out_specs, out_shape=out_shape,
                          scratch_shapes=scratch, compiler_params=_cparams(semantics))(*args)
    return outs[:n_out], outs[n_out:]


def _inproj_fwd(x, gain, wcat, l, exchange=None):
    t = x.shape[0]
    tm, tn = _tile(t, 1024), 1664

    def body(x_ref, g_ref, w_ref, o_ref, h_ref):
        @pl.when(pl.program_id(1) == 0)
        def _():
            h_ref[...] = _rms(x_ref[...], g_ref[...]).astype(h_ref.dtype)
        o_ref[...] = _dot_bf16(h_ref[...], w_ref[...], ((1,), (0,)))

    (proj, h), fetched = _call_carrying(
        body, "inproj_fwd", (t // tm, NCOL // tn),
        [pl.BlockSpec((tm, D), lambda i, j: (i, 0)), _full((1, D)), pl.BlockSpec((None, D, tn), lambda i, j: (l, 0, j))],
        [pl.BlockSpec((tm, tn), lambda i, j: (i, j)), pl.BlockSpec((tm, D), lambda i, j: (i, 0))],
        [jax.ShapeDtypeStruct((t, NCOL), F32), jax.ShapeDtypeStruct((t, D), wcat.dtype)],
        [], [x, gain, wcat], ("parallel", "arbitrary"), exchange)
    return proj, h, fetched


def _inproj_bwd_dx(dproj, wcat, x, gain, dxres, l, exchange=None):
    t = x.shape[0]
    tm, tk = _tile(t, 1024), 1664
    nk = NCOL // tk

    def body(dp_ref, w_ref, x_ref, g_ref, r_ref, dx_ref, dg_ref, acc_ref):
        i, k = pl.program_id(0), pl.program_id(1)

        @pl.when(k == 0)
        def _():
            acc_ref[...] = jnp.zeros_like(acc_ref)

        acc_ref[...] += _mm_nt(dp_ref[...], w_ref[...])

        @pl.when(k == nk - 1)
        def _():
            _, vjp = jax.vjp(_rms, x_ref[...], g_ref[...])
            dx, dg = vjp(acc_ref[...])
            dx_ref[...] = r_ref[...] + dx

            @pl.when(i == 0)
            def _():
                dg_ref[...] = dg

            @pl.when(i > 0)
            def _():
                dg_ref[...] += dg

    grid = (t // tm, nk)
    in_specs = [pl.BlockSpec((tm, tk), lambda i, k: (i, k)), pl.BlockSpec((None, D, tk), lambda i, k: (l, 0, k)),
                pl.BlockSpec((tm, D), lambda i, k: (i, 0)), _full((1, D)), pl.BlockSpec((tm, D), lambda i, k: (i, 0))]
    out_specs = [pl.BlockSpec((tm, D), lambda i, k: (i, 0)), _full((1, D))]
    out_shape = [jax.ShapeDtypeStruct((t, D), F32), jax.ShapeDtypeStruct((1, D), F32)]
    scratch, args = [pltpu.VMEM((tm, D), F32)], [dproj, wcat, x, gain, dxres]
    if exchange is not None:
        body = _carry(body, len(args), len(out_shape), len(scratch), exchange, grid)
        in_specs, out_specs = in_specs + exchange.in_specs, out_specs + exchange.out_specs
        out_shape, scratch, args = out_shape + exchange.out_shape, scratch + exchange.scratch_shapes, args + exchange.srcs
    outs = pl.pallas_call(
        body, name="inproj_bwd_dx", grid=grid, in_specs=in_specs, out_specs=out_specs, out_shape=out_shape,
        scratch_shapes=scratch, compiler_params=_cparams(("arbitrary", "arbitrary")),
    )(*args)
    return outs[0], outs[1], outs[2:]


def _inproj_bwd_dw(h, dproj):
    t = h.shape[0]
    tm, tn = _tile(t, 1024), 1664

    def body(h_ref, dp_ref, o_ref):
        @pl.when(pl.program_id(1) == 0)
        def _():
            o_ref[...] = jnp.zeros_like(o_ref)

        o_ref[...] += _mm_tn(h_ref[...], dp_ref[...])

    return pl.pallas_call(
        body, name="inproj_bwd_dw", grid=(NCOL // tn, t // tm),
        in_specs=[pl.BlockSpec((tm, D), lambda j, i: (i, 0)), pl.BlockSpec((tm, tn), lambda j, i: (i, j))],
        out_specs=pl.BlockSpec((D, tn), lambda j, i: (0, j)),
        out_shape=jax.ShapeDtypeStruct((D, NCOL), F32),
        compiler_params=_cparams(("parallel", "arbitrary")),
    )(h, dproj)


def _prep_fwd(proj, cw):
    t = proj.shape[0]

    def body(p_ref, w_ref, o_ref):
        qk = (pl.program_id(0) < 2 * NH).astype(F32)
        o_ref[...] = _prep_fn(p_ref[...], w_ref[0:1, :], w_ref[1:2, :], w_ref[2:3, :], w_ref[3:4, :], qk)

    return pl.pallas_call(
        body, name="prep_fwd", grid=(3 * NH,),
        in_specs=[pl.BlockSpec((t, DH), lambda c: (0, c)), pl.BlockSpec((4, DH), lambda c: (0, c))],
        out_specs=pl.BlockSpec((None, t, DH), lambda c: (c, 0, 0)),
        out_shape=jax.ShapeDtypeStruct((3 * NH, t, DH), F32),
        compiler_params=_cparams(("parallel",)),
    )(proj, cw)


def _prep_bwd(proj, cw, dq, dk, dv):
    t = proj.shape[0]

    def body(p_ref, w_ref, dq_ref, dk_ref, dv_ref, dp_ref, dw_ref):
        c = pl.program_id(0)
        qk = (c < 2 * NH).astype(F32)
        _, vjp = jax.vjp(lambda p, w0, w1, w2, w3: _prep_fn(p, w0, w1, w2, w3, qk),
                         p_ref[...], w_ref[0:1, :], w_ref[1:2, :], w_ref[2:3, :], w_ref[3:4, :])
        d = jnp.where(c < NH, dq_ref[...], jnp.where(c < 2 * NH, dk_ref[...], dv_ref[...]))
        dp, dw0, dw1, dw2, dw3 = vjp(d)
        dp_ref[...] = dp.astype(dp_ref.dtype)
        dw_ref[0:1, :] = dw0
        dw_ref[1:2, :] = dw1
        dw_ref[2:3, :] = dw2
        dw_ref[3:4, :] = dw3

    return pl.pallas_call(
        body, name="prep_bwd", grid=(3 * NH,),
        in_specs=[pl.BlockSpec((t, DH), lambda c: (0, c)), pl.BlockSpec((4, DH), lambda c: (0, c))]
        + [pl.BlockSpec((None, t, DH), functools.partial(lambda c, off: (jnp.clip(c - off, 0, NH - 1), 0, 0), off=off))
           for off in (0, NH, 2 * NH)],
        out_specs=[pl.BlockSpec((t, DH), lambda c: (0, c)), pl.BlockSpec((4, DH), lambda c: (0, c))],
        out_shape=[jax.ShapeDtypeStruct((t, 3 * D), GRAD_ACT), jax.ShapeDtypeStruct((4, 3 * D), F32)],
        compiler_params=_cparams(("arbitrary",)),
    )(proj, cw, dq, dk, dv)


def _gates_fwd(proj, gvec):
    t = proj.shape[0]
    tm = _tile(t, 512)

    def body(p_ref, gv_ref, b_ref, g_ref):
        outs = _gates_fn(p_ref[...], gv_ref[0:1, :], gv_ref[1:2, :])
        for h in range(NH):
            b_ref[h] = outs[h]
            g_ref[h] = outs[NH + h]

    spec = pl.BlockSpec((NH, tm, DH), lambda i: (0, i, 0))
    return pl.pallas_call(
        body, name="gates_fwd", grid=(t // tm,),
        in_specs=[pl.BlockSpec((tm, DH), lambda i: (i, BD0 // DH)), _full((8, DH))],
        out_specs=[spec, spec],
        out_shape=[jax.ShapeDtypeStruct((NH, t, DH), F32)] * 2,
        compiler_params=_cparams(("parallel",)),
    )(proj, gvec)


def _gates_bwd(proj, gvec, dbb, dgcb):
    t = proj.shape[0]
    tm = _tile(t, 512)

    def body(p_ref, gv_ref, db_ref, dg_ref, dp_ref, dgv_ref):
        _, vjp = jax.vjp(_gates_fn, p_ref[...], gv_ref[0:1, :], gv_ref[1:2, :])
        cts = tuple(db_ref[h] for h in range(NH)) + tuple(dg_ref[h] for h in range(NH))
        dp, da, db = vjp(cts)
        dp_ref[...] = dp.astype(dp_ref.dtype)

        @pl.when(pl.program_id(0) == 0)
        def _():
            dgv_ref[...] = jnp.zeros_like(dgv_ref)

        dgv_ref[0:1, :] += da
        dgv_ref[1:2, :] += db

    spec = pl.BlockSpec((NH, tm, DH), lambda i: (0, i, 0))
    return pl.pallas_call(
        body, name="gates_bwd", grid=(t // tm,),
        in_specs=[pl.BlockSpec((tm, DH), lambda i: (i, BD0 // DH)), _full((8, DH)), spec, spec],
        out_specs=[pl.BlockSpec((tm, DH), lambda i: (i, 0)), _full((8, DH))],
        out_shape=[jax.ShapeDtypeStruct((t, DH), GRAD_ACT), jax.ShapeDtypeStruct((8, DH), F32)],
        compiler_params=_cparams(("arbitrary",)),
    )(proj, gvec, dbb, dgcb)


def _chunks_per_step(nch):
    return max(c for c in (8, 4, 2, 1) if nch % c == 0)


def _grid_ends(grid):
    def first():
        return functools.reduce(jnp.logical_and, [pl.program_id(a) == 0 for a in range(len(grid))])

    def last():
        return functools.reduce(jnp.logical_and, [pl.program_id(a) == n - 1 for a, n in enumerate(grid)])

    return first, last


def _carry(body, n_in, n_out, n_scratch, exchange, grid):
    first, last = _grid_ends(grid)
    na = exchange.narr

    def wrapped(*refs):
        a, b = n_in, n_in + na
        c, d = b + n_out, b + n_out + na
        e = d + n_scratch
        srcs, dsts, sems = refs[a:b], refs[c:d], refs[e:]

        @pl.when(first())
        def _():
            exchange.start(srcs, dsts, sems)

        body(*(refs[:a] + refs[b:c] + refs[d:e]))

        @pl.when(last())
        def _():
            exchange.wait(srcs, dsts, sems)

    return wrapped


def _delta_local_fwd(qkv, bb, gcb, exchange=None):
    t = qkv.shape[1]
    cps = _chunks_per_step(t // CH)
    rows = cps * CH
    grid = (NH, t // rows)

    def body(q_ref, k_ref, v_ref, b_ref, g_ref, *out_refs):
        slices = [slice(c * CH, (c + 1) * CH) for c in range(cps)]
        results = _chunks_local([tuple(ref[sl, :] for ref in (q_ref, k_ref, v_ref, b_ref, g_ref)) for sl in slices])
        for sl, (outs, t_inv) in zip(slices, results):
            for ref, val in zip(out_refs, outs + (t_inv,)):
                ref[sl, :] = val.astype(ref.dtype)

    def blk(off):
        return pl.BlockSpec((None, rows, DH), lambda h, n: (h + off, n, 0))

    in_specs = [blk(0), blk(NH), blk(2 * NH), blk(0), blk(0)]
    out_specs = [blk(0)] * 6
    out_shape = [jax.ShapeDtypeStruct((NH, t, DH), dt) for dt in (F32, BF16, BF16, BF16, BF16, F32)]
    args, scratch, sem = [qkv, qkv, qkv, bb, gcb], [], ("parallel", "parallel")
    if exchange is not None:
        body = _carry(body, 5, 6, 0, exchange, grid)
        in_specs, out_specs = in_specs + exchange.in_specs, out_specs + exchange.out_specs
        out_shape, scratch, args = out_shape + exchange.out_shape, exchange.scratch_shapes, args + exchange.srcs
        sem = ("arbitrary", "arbitrary")
    outs = pl.pallas_call(
        body, name="delta_local_fwd", grid=grid, in_specs=in_specs, out_specs=out_specs, out_shape=out_shape,
        scratch_shapes=scratch, compiler_params=_cparams(sem),
    )(*args)
    return outs[:5], outs[5], outs[6:]


def _delta_local_bwd(qkv, bb, gcb, t_inv, cts, dgcb_state):
    t = qkv.shape[1]
    cps = _chunks_per_step(t // CH)
    rows = cps * CH

    def body(q_ref, k_ref, v_ref, b_ref, g_ref, ti_ref, du_ref, dw_ref, dqd_ref, dkd_ref, da_ref, dgs_ref,
             dq_ref, dk_ref, dv_ref, db_ref, dg_ref):
        slices = [slice(c * CH, (c + 1) * CH) for c in range(cps)]
        items = [tuple(ref[sl, :] for ref in (q_ref, k_ref, v_ref, b_ref, g_ref, ti_ref)) for sl in slices]
        cts = [tuple(ref[sl, :] for ref in (du_ref, dw_ref, dqd_ref, dkd_ref, da_ref)) for sl in slices]
        for sl, (dq, dk, dv, db, dg, _) in zip(slices, _side_by_side_vjp(_chunks_local_known, items, cts)):
            dq_ref[sl, :] = dq
            dk_ref[sl, :] = dk
            dv_ref[sl, :] = dv
            db_ref[sl, :] = db
            dg_ref[sl, :] = dg + dgs_ref[sl, :]

    def blk(off):
        return pl.BlockSpec((None, rows, DH), lambda h, n: (h + off, n, 0))

    return pl.pallas_call(
        body, name="delta_local_bwd", grid=(NH, t // rows),
        in_specs=[blk(0), blk(NH), blk(2 * NH)] + [blk(0)] * 9,
        out_specs=[blk(0)] * 5,
        out_shape=[jax.ShapeDtypeStruct((NH, t, DH), F32)] * 5,
        compiler_params=_cparams(("parallel", "parallel")),
    )(qkv, qkv, qkv, bb, gcb, t_inv, *cts, dgcb_state)


def _delta_state_fwd(local, gcb):
    t = gcb.shape[1]
    nch = t // CH

    def body(u_ref, w_ref, qd_ref, kd_ref, a_ref, g_ref, o_ref, s_ref, st_ref):
        @pl.when(pl.program_id(0) == 0)
        def _():
            st_ref[...] = jnp.zeros_like(st_ref)

        s_ref[...] = st_ref[...]
        items = [tuple(ref[h].astype(F32) for ref in (u_ref, w_ref, qd_ref, kd_ref, a_ref, g_ref, st_ref))
                 for h in range(NH)]
        for h, (o, ns) in enumerate(_state_steps(items)):
            o_ref[:, h * DH:(h + 1) * DH] = o
            st_ref[h] = ns

    blk = pl.BlockSpec((NH, CH, DH), lambda n: (0, n, 0))
    return pl.pallas_call(
        body, name="delta_state_fwd", grid=(nch,),
        in_specs=[blk] * 6,
        out_specs=[pl.BlockSpec((CH, D), lambda n: (n, 0)),
                   pl.BlockSpec((NH, None, DH, DH), lambda n: (0, n, 0, 0))],
        out_shape=[jax.ShapeDtypeStruct((t, D), F32), jax.ShapeDtypeStruct((NH, nch, DH, DH), F32)],
        scratch_shapes=[pltpu.VMEM((NH, DH, DH), F32)],
        compiler_params=_cparams(("arbitrary",)),
    )(*local, gcb)


def _delta_state_bwd(local, gcb, states, do):
    t = gcb.shape[1]
    nch = t // CH

    def body(u_ref, w_ref, qd_ref, kd_ref, a_ref, g_ref, s_ref, do_ref,
             du_ref, dw_ref, dqd_ref, dkd_ref, da_ref, dg_ref, ds_ref):
        @pl.when(pl.program_id(0) == 0)
        def _():
            ds_ref[...] = jnp.zeros_like(ds_ref)

        items = [tuple(ref[h].astype(F32) for ref in (u_ref, w_ref, qd_ref, kd_ref, a_ref, g_ref, s_ref))
                 for h in range(NH)]
        cts = [(do_ref[:, h * DH:(h + 1) * DH], ds_ref[h]) for h in range(NH)]
        for h, (du, dw, dqd, dkd, da, dg, ds) in enumerate(_side_by_side_vjp(_state_steps, items, cts)):
            du_ref[h] = du
            dw_ref[h] = dw
            dqd_ref[h] = dqd
            dkd_ref[h] = dkd
            da_ref[h] = da
            dg_ref[h] = dg
            ds_ref[h] = ds

    blk = pl.BlockSpec((NH, CH, DH), lambda n: (0, nch - 1 - n, 0))
    return pl.pallas_call(
        body, name="delta_state_bwd", grid=(nch,),
        in_specs=[blk] * 6 + [pl.BlockSpec((NH, None, DH, DH), lambda n: (0, nch - 1 - n, 0, 0)),
                              pl.BlockSpec((CH, D), lambda n: (nch - 1 - n, 0))],
        out_specs=[blk] * 6,
        out_shape=[jax.ShapeDtypeStruct((NH, t, DH), F32)] * 6,
        scratch_shapes=[pltpu.VMEM((NH, DH, DH), F32)],
        compiler_params=_cparams(("arbitrary",)),
    )(*local, gcb, states, do)


def _s5_params(ar, ai, ldt, br2, bi2):
    def body(ar_ref, ai_ref, ld_ref, br_ref, bi_ref, lr_ref, li_ref, bbr_ref, bbi_ref):
        lr, li, bbr, bbi = _s5_params_fn(ar_ref[...], ai_ref[...], ld_ref[...], br_ref[...], bi_ref[...])
        lr_ref[...] = lr
        li_ref[...] = li
        bbr_ref[...] = bbr
        bbi_ref[...] = bbi

    sq = pl.BlockSpec((None, NG, NS), lambda l: (l, 0, 0))
    wide = pl.BlockSpec((None, NG, NS * GS), lambda l: (l, 0, 0))
    return pl.pallas_call(
        body, name="s5_params", grid=(DEPTH,),
        in_specs=[sq, sq, pl.BlockSpec((None, NG, 1), lambda l: (l, 0, 0)), wide, wide],
        out_specs=[sq, sq, wide, wide],
        out_shape=[jax.ShapeDtypeStruct((DEPTH, NG, NS), F32)] * 2
        + [jax.ShapeDtypeStruct((DEPTH, NG, NS * GS), F32)] * 2,
        compiler_params=_cparams(("parallel",)),
    )(ar, ai, ldt, br2, bi2)


def _s5_params_bwd(ar, ai, ldt, br2, bi2, dlr, dli, dbbr, dbbi):
    def body(ar_ref, ai_ref, ld_ref, br_ref, bi_ref, a_ref, b_ref, c_ref, d_ref,
             dar_ref, dai_ref, dld_ref, dbr_ref, dbi_ref):
        _, vjp = jax.vjp(_s5_params_fn, ar_ref[...], ai_ref[...], ld_ref[...], br_ref[...], bi_ref[...])
        dar, dai, dld, dbr, dbi = vjp((a_ref[...], b_ref[...], c_ref[...], d_ref[...]))
        dar_ref[...] = dar
        dai_ref[...] = dai
        dld_ref[...] = dld
        dbr_ref[...] = dbr
        dbi_ref[...] = dbi

    sq = pl.BlockSpec((None, NG, NS), lambda l: (l, 0, 0))
    col = pl.BlockSpec((None, NG, 1), lambda l: (l, 0, 0))
    wide = pl.BlockSpec((None, NG, NS * GS), lambda l: (l, 0, 0))
    return pl.pallas_call(
        body, name="s5_params_bwd", grid=(DEPTH,),
        in_specs=[sq, sq, col, wide, wide, sq, sq, wide, wide],
        out_specs=[sq, sq, col, wide, wide],
        out_shape=[jax.ShapeDtypeStruct((DEPTH, NG, NS), F32)] * 2 + [jax.ShapeDtypeStruct((DEPTH, NG, 1), F32)]
        + [jax.ShapeDtypeStruct((DEPTH, NG, NS * GS), F32)] * 2,
        compiler_params=_cparams(("parallel",)),
    )(ar, ai, ldt, br2, bi2, dlr, dli, dbbr, dbbi)


def _s5_tile_rows(t):
    return _tile(t // 2, 1024)


def _s5_fwd(proj, lam, bblk, cblk, exchange=None):
    t = proj.shape[0]
    r = _s5_tile_rows(t)
    nt = t // r
    u0 = 4 * D // DH

    def body(u_ref, lam_ref, b_ref, c_ref, y_ref, car_ref, st_ref, hr_ref, hi_ref, cr_ref, ci_ref):
        @pl.when(pl.program_id(1) == 0)
        def _():
            st_ref[...] = jnp.zeros_like(st_ref)

        car_ref[...] = st_ref[...]
        hr, hi = _s5_states(u_ref[...], lam_ref, b_ref, st_ref, hr_ref, hi_ref, cr_ref, ci_ref)
        y_ref[...] = _mm(hr, c_ref[0]) - _mm(hi, c_ref[1])
        st_ref[0:1, :] = _scratch_row(hr_ref, r - 1)
        st_ref[1:2, :] = _scratch_row(hi_ref, r - 1)

    scratch = ([pltpu.VMEM((8, SW), F32)] + [pltpu.VMEM((SW // DH, r, DH), F32)] * 2
               + [pltpu.VMEM((r // SUB, SW), F32)] * 2)
    (y, carries), fetched = _call_carrying(
        body, "s5_fwd", (NCB, nt),
        [pl.BlockSpec((r, DH), lambda c, i: (i, u0 + c)), pl.BlockSpec((2, 1, SW), lambda c, i: (0, 0, c)),
         pl.BlockSpec((2, None, DH, SW), lambda c, i: (0, c, 0, 0)),
         pl.BlockSpec((2, None, SW, DH), lambda c, i: (0, c, 0, 0))],
        [pl.BlockSpec((r, DH), lambda c, i: (i, c)), pl.BlockSpec((None, 8, SW), lambda c, i: (i, 0, c))],
        [jax.ShapeDtypeStruct((t, D), F32), jax.ShapeDtypeStruct((nt, 8, NG * NS), F32)],
        scratch, [proj, lam, bblk, cblk], ("parallel", "arbitrary"), exchange)
    return y, carries, fetched


def _s5_bwd(proj, lam, bblk, cblk, carries, dy, du_skip, exchange=None):
    t = proj.shape[0]
    r = _s5_tile_rows(t)
    nt = t // r
    u0 = 4 * D // DH

    def body(u_ref, lam_ref, b_ref, c_ref, car_ref, dy_ref, dus_ref, du_ref, dlam_ref, db_ref, dc_ref, dst_ref,
             hr_ref, hi_ref, ar_ref, ai_ref, cr_ref, ci_ref):
        first = pl.program_id(1) == 0

        @pl.when(first)
        def _():
            dst_ref[...] = jnp.zeros_like(dst_ref)

        u, dy = u_ref[...], dy_ref[...]
        lr, li = lam_ref[0], lam_ref[1]
        hr, hi = _s5_states(u, lam_ref, b_ref, car_ref, hr_ref, hi_ref, cr_ref, ci_ref)
        dcr2, dci2 = _mm_tn(hr, dy), -_mm_tn(hi, dy)
        fold = _cmul(lr, -li, dst_ref[0:1, :], dst_ref[1:2, :])
        ar, ai = _scan_tile(_mm_nt(dy, c_ref[0]), -_mm_nt(dy, c_ref[1]), lr, -li, ar_ref, ai_ref, cr_ref, ci_ref,
                            True, fold)
        top = _rows((r, SW)) == 0
        dst_ref[0:1, :] = _scratch_row(ar_ref, 0)
        dst_ref[1:2, :] = _scratch_row(ai_ref, 0)
        du_ref[...] = (_mm_nt(ar, b_ref[0]) + _mm_nt(ai, b_ref[1]) + dus_ref[...]).astype(du_ref.dtype)
        dbr, dbi = _mm_tn(u, ar), _mm_tn(u, ai)
        pr = _sd(hr, 1) + jnp.where(top, car_ref[0:1, :], 0.0)
        pi = _sd(hi, 1) + jnp.where(top, car_ref[1:2, :], 0.0)
        dlr = jnp.sum(ar * pr + ai * pi, axis=0, keepdims=True)
        dli = jnp.sum(ai * pr - ar * pi, axis=0, keepdims=True)

        @pl.when(first)
        def _():
            dlam_ref[0] = dlr
            dlam_ref[1] = dli
            db_ref[0] = dbr
            db_ref[1] = dbi
            dc_ref[0] = dcr2
            dc_ref[1] = dci2

        @pl.when(jnp.logical_not(first))
        def _():
            dlam_ref[0] += dlr
            dlam_ref[1] += dli
            db_ref[0] += dbr
            db_ref[1] += dbi
            dc_ref[0] += dcr2
            dc_ref[1] += dci2

    grid = (NCB, nt)
    in_specs = [pl.BlockSpec((r, DH), lambda c, i: (nt - 1 - i, u0 + c)),
                pl.BlockSpec((2, 1, SW), lambda c, i: (0, 0, c)),
                pl.BlockSpec((2, None, DH, SW), lambda c, i: (0, c, 0, 0)),
                pl.BlockSpec((2, None, SW, DH), lambda c, i: (0, c, 0, 0)),
                pl.BlockSpec((None, 8, SW), lambda c, i: (nt - 1 - i, 0, c)),
                pl.BlockSpec((r, DH), lambda c, i: (nt - 1 - i, c)),
                pl.BlockSpec((r, DH), lambda c, i: (nt - 1 - i, c))]
    out_specs = [pl.BlockSpec((r, DH), lambda c, i: (nt - 1 - i, c)),
                 pl.BlockSpec((2, 1, SW), lambda c, i: (0, 0, c)),
                 pl.BlockSpec((2, None, DH, SW), lambda c, i: (0, c, 0, 0)),
                 pl.BlockSpec((2, None, SW, DH), lambda c, i: (0, c, 0, 0))]
    out_shape = [jax.ShapeDtypeStruct((t, D), GRAD_ACT), jax.ShapeDtypeStruct((2, 1, NG * NS), F32),
                 jax.ShapeDtypeStruct((2, NCB, DH, SW), F32), jax.ShapeDtypeStruct((2, NCB, SW, DH), F32)]
    scratch = ([pltpu.VMEM((8, SW), F32)] + [pltpu.VMEM((SW // DH, r, DH), F32)] * 4
               + [pltpu.VMEM((r // SUB, SW), F32)] * 2)
    args, sem = [proj, lam, bblk, cblk, carries, dy, du_skip], ("parallel", "arbitrary")
    if exchange is not None:
        body = _carry(body, len(args), len(out_shape), len(scratch), exchange, grid)
        in_specs, out_specs = in_specs + exchange.in_specs, out_specs + exchange.out_specs
        out_shape, scratch, args = out_shape + exchange.out_shape, scratch + exchange.scratch_shapes, args + exchange.srcs
        sem = ("arbitrary", "arbitrary")
    outs = pl.pallas_call(
        body, name="s5_bwd", grid=grid, in_specs=in_specs, out_specs=out_specs, out_shape=out_shape,
        scratch_shapes=scratch, compiler_params=_cparams(sem),
    )(*args)
    return outs[:4], outs[4:]


def _proj_spec(tm, col):
    return pl.BlockSpec((tm, D), lambda i: (i, col))


def _layer_mat(l):
    return pl.BlockSpec((None, D, D), lambda i: (l, 0, 0))


def _mix_fwd(proj, o, s5y, x, hn, dvec, wglu, bglu, wout, npost, l):
    t = x.shape[0]
    tm = _tile(t, 256)

    def body(za_ref, u_ref, zb_ref, ra_ref, rb_ref, o_ref, y_ref, x_ref, hn_ref, d_ref, wg_ref, bg_ref, wo_ref,
             np_ref, xn_ref):
        y0 = _mix_pre(y_ref[...], u_ref[...], d_ref[...])
        gl = _mm(y0, wg_ref[...]) + bg_ref[...]
        m = _mix_mid(o_ref[...], za_ref[...], y0, gl, zb_ref[...], ra_ref[...], rb_ref[...], hn_ref[...])
        out = _mm(m, wo_ref[...])
        xn_ref[...] = _mix_post(x_ref[...], out, np_ref[...])

    act = pl.BlockSpec((tm, D), lambda i: (i, 0))
    return pl.pallas_call(
        body, name="mix_fwd", grid=(t // tm,),
        in_specs=[_proj_spec(tm, 3), _proj_spec(tm, 4), _proj_spec(tm, 5), _proj_spec(tm, 6), _proj_spec(tm, 7),
                  act, act, act, _full((1, DH)), _full((1, D)), _layer_mat(l), _full((1, D)), _layer_mat(l),
                  _full((1, D))],
        out_specs=act,
        out_shape=jax.ShapeDtypeStruct((t, D), F32),
        compiler_params=_cparams(("parallel",)),
    )(proj, proj, proj, proj, proj, o, s5y, x, hn, dvec, wglu, bglu, wout, npost)


def _mix_bwd(proj, o, s5y, x, hn, dvec, wglu, bglu, wout, npost, dxn, l):
    t = x.shape[0]
    tm = _tile(t, 128)

    def body(za_ref, u_ref, zb_ref, ra_ref, rb_ref, o_ref, y_ref, x_ref, hn_ref, d_ref, wg_ref, bg_ref, wo_ref,
             np_ref, dxn_ref,
             dza_ref, du_ref, dzb_ref, dra_ref, drb_ref, do_ref, dy_ref, dx_ref,
             y0_ref, dgl_ref, m_ref, dout_ref, dvecs_ref, dhn_ref):
        y0, vjp_pre = jax.vjp(_mix_pre, y_ref[...], u_ref[...], d_ref[...])
        gl = _mm(y0, wg_ref[...]) + bg_ref[...]
        m, vjp_mid = jax.vjp(_mix_mid, o_ref[...], za_ref[...], y0, gl, zb_ref[...], ra_ref[...], rb_ref[...],
                             hn_ref[...])
        out = _mm(m, wo_ref[...])
        _, vjp_post = jax.vjp(_mix_post, x_ref[...], out, np_ref[...])
        dx, dout, dnp = vjp_post(dxn_ref[...])
        dm = _mm_nt(dout, wo_ref[...])
        do, dza, dy0, dgl, dzb, dra, drb, dhn = vjp_mid(dm)
        y0_ref[...] = y0.astype(BF16)
        dgl_ref[...] = dgl.astype(BF16)
        m_ref[...] = m.astype(BF16)
        dout_ref[...] = dout.astype(BF16)
        dbg = jnp.sum(dgl, axis=0, keepdims=True)
        dy0 = dy0 + _mm_nt(dgl, wg_ref[...])
        dy, du, dd = vjp_pre(dy0)
        dza_ref[...] = dza.astype(dza_ref.dtype)
        du_ref[...] = du
        dzb_ref[...] = dzb.astype(dzb_ref.dtype)
        dra_ref[...] = dra.astype(dra_ref.dtype)
        drb_ref[...] = drb.astype(drb_ref.dtype)
        do_ref[...] = do
        dy_ref[...] = dy
        dx_ref[...] = dx
        first = pl.program_id(0) == 0

        @pl.when(first)
        def _():
            dvecs_ref[...] = jnp.zeros_like(dvecs_ref)
            dhn_ref[...] = jnp.zeros_like(dhn_ref)

        dvecs_ref[0:1, :] += dd
        dvecs_ref[1:2, :] += dbg
        dvecs_ref[2:3, :] += dnp
        dhn_ref[0:1, :] += dhn

    act = pl.BlockSpec((tm, D), lambda i: (i, 0))
    a, ga = jax.ShapeDtypeStruct((t, D), F32), jax.ShapeDtypeStruct((t, D), GRAD_ACT)
    b16 = jax.ShapeDtypeStruct((t, D), BF16)
    outs = pl.pallas_call(
        body, name="mix_bwd", grid=(t // tm,),
        in_specs=[_proj_spec(tm, 3), _proj_spec(tm, 4), _proj_spec(tm, 5), _proj_spec(tm, 6), _proj_spec(tm, 7),
                  act, act, act, _full((1, DH)), _full((1, D)), _layer_mat(l), _full((1, D)), _layer_mat(l),
                  _full((1, D)), act],
        out_specs=[act] * 12 + [_full((8, D)), _full((8, DH))],
        out_shape=[ga, a, ga, ga, ga, a, a, a, b16, b16, b16, b16, jax.ShapeDtypeStruct((8, D), F32),
                   jax.ShapeDtypeStruct((8, DH), F32)],
        compiler_params=_cparams(("arbitrary",)),
    )(proj, proj, proj, proj, proj, o, s5y, x, hn, dvec, wglu, bglu, wout, npost, dxn)
    y0, dgl, m, dout = outs[8:12]
    return list(outs[:8]) + [_weight_grad(y0, dgl, "glu_dw"), _weight_grad(m, dout, "out_dw")] + list(outs[12:])


def _weight_grad(a, b, name):
    t = a.shape[0]
    tk = _tile(t, 1024)

    def body(a_ref, b_ref, o_ref):
        @pl.when(pl.program_id(0) == 0)
        def _():
            o_ref[...] = jnp.zeros_like(o_ref)

        o_ref[...] += _mm_tn(a_ref[...], b_ref[...])

    rows = pl.BlockSpec((tk, D), lambda i: (i, 0))
    return pl.pallas_call(
        body, name=name, grid=(t // tk,), in_specs=[rows, rows], out_specs=_full((D, D)),
        out_shape=jax.ShapeDtypeStruct((D, D), F32), compiler_params=_cparams(("arbitrary",)),
    )(a, b)


def _loss_grad(y, target):
    t = y.shape[0]
    tm = _tile(t, 512)

    def body(y_ref, t_ref, dy_ref, l_ref):
        err = y_ref[...] - t_ref[...]
        dy_ref[...] = err * (1.0 / D)
        part = jnp.sum(jnp.sum(err * err, axis=1, keepdims=True), axis=0, keepdims=True) * (0.5 / D)
        part = jnp.broadcast_to(part, (8, DH))

        @pl.when(pl.program_id(0) == 0)
        def _():
            l_ref[...] = part

        @pl.when(pl.program_id(0) > 0)
        def _():
            l_ref[...] += part

    act = pl.BlockSpec((tm, D), lambda i: (i, 0))
    return pl.pallas_call(
        body, name="loss_grad", grid=(t // tm,),
        in_specs=[act, act], out_specs=[act, _full((8, DH))],
        out_shape=[jax.ShapeDtypeStruct((t, D), F32), jax.ShapeDtypeStruct((8, DH), F32)],
        compiler_params=_cparams(("arbitrary",)),
    )(y, target)


def _flips(rel):
    x, y, c = lax.axis_index("x"), lax.axis_index("y"), lax.axis_index("c")
    fx, fy, fc = rel
    return (x ^ fx if fx else x, y ^ fy if fy else y, c ^ fc if fc else c)


CHIP_RELS = ((1, 0, 0), (0, 1, 0), (1, 1, 0))
ALL_RELS = tuple((fx, fy, fc) for fx in (0, 1) for fy in (0, 1) for fc in (0, 1) if (fx, fy, fc) != (0, 0, 0))


def _slot_of(pos, by_chip):
    px, py, pc = pos
    return 2 * px + py if by_chip else 4 * px + 2 * py + pc


class _Exchange:
    def __init__(self, srcs, rels, by_chip, scatter):
        self.srcs, self.rels, self.by_chip, self.scatter = list(srcs), rels, by_chip, scatter
        self.narr = len(self.srcs)
        nslot, nsem = NCHIP if by_chip else NDEV, self.narr * len(rels)
        self.in_specs = [pl.BlockSpec(memory_space=pl.ANY)] * self.narr
        self.out_specs = [pl.BlockSpec(memory_space=pl.ANY)] * self.narr
        self.out_shape = [jax.ShapeDtypeStruct((nslot,) + s.shape[-2:], s.dtype) for s in self.srcs]
        self.scratch_shapes = [pltpu.SemaphoreType.DMA((nsem,)), pltpu.SemaphoreType.DMA((nsem,)),
                               pltpu.SemaphoreType.DMA((self.narr,))]

    def _copies(self, src_refs, dst_refs, sems):
        send_sems, recv_sems, local_sems = sems
        my_slot = _slot_of(_flips((0, 0, 0)), self.by_chip)
        local, sends, arrivals = [], [], []
        for a, (src_ref, dst_ref) in enumerate(zip(src_refs, dst_refs)):
            local.append(pltpu.make_async_copy(src_ref.at[my_slot] if self.scatter else src_ref, dst_ref.at[my_slot],
                                               local_sems.at[a]))
            for k, rel in enumerate(self.rels):
                peer = _flips(rel)
                pair = dict(send_sem=send_sems.at[a * len(self.rels) + k], recv_sem=recv_sems.at[a * len(self.rels) + k],
                            device_id=peer, device_id_type=pl.DeviceIdType.MESH)
                part = src_ref.at[_slot_of(peer, self.by_chip)] if self.scatter else src_ref
                sends.append(pltpu.make_async_remote_copy(src_ref=part, dst_ref=dst_ref.at[my_slot], **pair))
                arrivals.append(pltpu.make_async_remote_copy(
                    src_ref=src_ref.at[0] if self.scatter else src_ref, dst_ref=dst_ref.at[_slot_of(peer, self.by_chip)],
                    **pair))
        return local, sends, arrivals

    def start(self, src_refs, dst_refs, sems):
        local, sends, _ = self._copies(src_refs, dst_refs, sems)
        for cp in local + sends:
            cp.start()

    def wait(self, src_refs, dst_refs, sems):
        local, sends, arrivals = self._copies(src_refs, dst_refs, sems)
        for cp in arrivals:
            cp.wait_recv()
        for cp in sends:
            cp.wait_send()
        for cp in local:
            cp.wait()


def _exchange(srcs, rels, by_chip, scatter, name):
    ex = _Exchange(srcs, rels, by_chip, scatter)

    def body(*refs):
        parts = refs[:ex.narr], refs[ex.narr:2 * ex.narr], refs[2 * ex.narr:]
        ex.start(*parts)
        ex.wait(*parts)

    return pl.pallas_call(body, name=name, in_specs=ex.in_specs, out_specs=ex.out_specs, out_shape=ex.out_shape,
                          scratch_shapes=ex.scratch_shapes)(*ex.srcs)


def _sibling_swap(srcs, name):
    narr = len(srcs)

    def body(*refs):
        src_refs, dst_refs = refs[:narr], refs[narr:2 * narr]
        send_sems, recv_sems = refs[2 * narr:]
        peer = _flips((0, 0, 1))
        copies = [pltpu.make_async_remote_copy(src_ref=s, dst_ref=d, send_sem=send_sems.at[a], recv_sem=recv_sems.at[a],
                                               device_id=peer, device_id_type=pl.DeviceIdType.MESH)
                  for a, (s, d) in enumerate(zip(src_refs, dst_refs))]
        for cp in copies:
            cp.start()
        for cp in copies:
            cp.wait()

    return pl.pallas_call(
        body, name=name,
        in_specs=[pl.BlockSpec(memory_space=pl.ANY)] * narr,
        out_specs=[pl.BlockSpec(memory_space=pl.ANY)] * narr,
        out_shape=[jax.ShapeDtypeStruct(s.shape, s.dtype) for s in srcs],
        scratch_shapes=[pltpu.SemaphoreType.DMA((narr,)), pltpu.SemaphoreType.DMA((narr,))],
    )(*srcs)


def _all_reduce(src, name):
    rows, cols = src.shape
    r = rows // NDEV
    nrel = len(ALL_RELS)

    def body(src_ref, out_ref, parts_ref, mine_ref, send_sems, recv_sems):
        my_slot = _slot_of(_flips((0, 0, 0)), False)

        def piece(ref, slot):
            return ref.at[pl.ds(pl.multiple_of(slot * r, 8), r), :]

        def copies(phase):
            out = []
            for k, rel in enumerate(ALL_RELS):
                peer = _flips(rel)
                pair = dict(send_sem=send_sems.at[phase * nrel + k], recv_sem=recv_sems.at[phase * nrel + k],
                            device_id=peer, device_id_type=pl.DeviceIdType.MESH)
                if phase == 0:
                    out.append(pltpu.make_async_remote_copy(src_ref=piece(src_ref, _slot_of(peer, False)),
                                                            dst_ref=parts_ref.at[my_slot], **pair))
                else:
                    out.append(pltpu.make_async_remote_copy(src_ref=mine_ref, dst_ref=piece(out_ref, my_slot), **pair))
            return out

        first = copies(0)
        for cp in first:
            cp.start()
        parts_ref[my_slot] = piece(src_ref, my_slot)[...]
        for cp in first:
            cp.wait_recv()
        acc = parts_ref[0]
        for s in range(1, NDEV):
            acc = acc + parts_ref[s]
        mine_ref[...] = acc
        second = copies(1)
        for cp in second:
            cp.start()
        piece(out_ref, my_slot)[...] = acc
        for cp in second:
            cp.wait_recv()
        for cp in first + second:
            cp.wait_send()

    return pl.pallas_call(
        body, name=name,
        in_specs=[pl.BlockSpec(memory_space=pltpu.VMEM)], out_specs=pl.BlockSpec(memory_space=pltpu.VMEM),
        out_shape=jax.ShapeDtypeStruct(src.shape, src.dtype),
        scratch_shapes=[pltpu.VMEM((NDEV, r, cols), src.dtype), pltpu.VMEM((r, cols), src.dtype),
                        pltpu.SemaphoreType.DMA((2 * nrel,)), pltpu.SemaphoreType.DMA((2 * nrel,))],
        compiler_params=pltpu.CompilerParams(vmem_limit_bytes=VMEM_LIMIT),
    )(src)


def _sum_slots(parts, name):
    ns, rows, cols = parts.shape
    tr = _row_tile(rows, 256)

    def body(p_ref, o_ref):
        acc = p_ref[0].astype(F32)
        for s in range(1, ns):
            acc = acc + p_ref[s].astype(F32)
        o_ref[...] = acc

    return pl.pallas_call(
        body, name=name, grid=(rows // tr,),
        in_specs=[pl.BlockSpec((ns, tr, cols), lambda i: (0, i, 0))],
        out_specs=pl.BlockSpec((tr, cols), lambda i: (i, 0)),
        out_shape=jax.ShapeDtypeStruct((rows, cols), F32),
        compiler_params=_cparams(("parallel",)),
    )(parts)


def _adamw_update(w, g, m, v):
    c1 = 1.0 / (1.0 - ADAM_B1 ** ADAM_STEP)
    c2 = 1.0 / (1.0 - ADAM_B2 ** ADAM_STEP)
    nm = ADAM_B1 * m + (1.0 - ADAM_B1) * g
    nv = ADAM_B2 * v + (1.0 - ADAM_B2) * (g * g)
    return -ADAM_LR * ((nm * c1) / (jnp.sqrt(nv * c2) + ADAM_EPS) + ADAM_WD * w), nm, nv


def _adamw_layers(w, g, m, v, name):
    def body(w_ref, g_ref, m_ref, v_ref, d_ref, nm_ref, nv_ref):
        d_ref[...], nm_ref[...], nv_ref[...] = _adamw_update(w_ref[...], g_ref[...], m_ref[...], v_ref[...])

    blk = pl.BlockSpec((None,) + w.shape[1:], lambda l: (l,) + (0,) * (w.ndim - 1))
    return pl.pallas_call(
        body, name=name, grid=(w.shape[0],), in_specs=[blk] * 4, out_specs=[blk] * 3,
        out_shape=[jax.ShapeDtypeStruct(w.shape, F32)] * 3, compiler_params=_cparams(("parallel",)),
    )(w, g, m, v)


def _adamw(w, g_parts, m, v, name, max_rows=256):
    if w.ndim == 2:
        return [o[0] for o in _adamw(w[None], g_parts, m[None], v[None], name, max_rows)]
    nl, rows, cols = w.shape
    tr = _row_tile(rows, max_rows)
    per_layer = rows // tr
    c1 = 1.0 / (1.0 - ADAM_B1 ** ADAM_STEP)
    c2 = 1.0 / (1.0 - ADAM_B2 ** ADAM_STEP)
    npart = len(g_parts)

    def body(*refs):
        w_ref, m_ref, v_ref = refs[:3]
        g_refs = refs[3:3 + npart]
        go_ref, d_ref, nm_ref, nv_ref = refs[3 + npart:]
        terms = []
        for g_ref in g_refs:
            terms += [g_ref[...]] if len(g_ref.shape) == 2 else [g_ref[s] for s in range(g_ref.shape[0])]
        g = terms[0]
        for term in terms[1:]:
            g = g + term
        go_ref[...] = g
        d_ref[...], nm_ref[...], nv_ref[...] = _adamw_update(w_ref[...], g, m_ref[...], v_ref[...])

    blk = pl.BlockSpec((None, tr, cols), lambda l, i: (l, i, 0))
    g_specs = [pl.BlockSpec((tr, cols), lambda l, i: (l * per_layer + i, 0)) if p.ndim == 2 else
               pl.BlockSpec((p.shape[0], tr, cols), lambda l, i: (0, l * per_layer + i, 0)) for p in g_parts]
    out = jax.ShapeDtypeStruct((nl, rows, cols), F32)
    return pl.pallas_call(
        body, name=name, grid=(nl, per_layer),
        in_specs=[blk, blk, blk] + g_specs,
        out_specs=[blk] * 4, out_shape=[out] * 4,
        compiler_params=_cparams(("parallel", "parallel")),
    )(w, m, v, *g_parts)


WEIGHT_SPLIT = (0, 384, 704, D)
WIN_SHARD = 2052
CONV_SHARD = 768
ROW_SHARD = 256

SMALL_TINY = (("norm_pre", (DEPTH, D)), ("a_log", (DEPTH, NH)), ("dt_bias", (DEPTH, NH)), ("head_norm", (DEPTH, DH)),
              ("ssm_a_re", (DEPTH, NG, NS)), ("ssm_a_im", (DEPTH, NG, NS)), ("ssm_log_dt", (DEPTH, NG)),
              ("ssm_d", (DEPTH, D)), ("b_glu", (DEPTH, D)), ("norm_post", (DEPTH, D)))
SMALL_BIG = (("ssm_b_re", (DEPTH, NG, NS, GS)), ("ssm_b_im", (DEPTH, NG, NS, GS)),
             ("ssm_c_re", (DEPTH, NG, GS, NS)), ("ssm_c_im", (DEPTH, NG, GS, NS)))
SMALL = SMALL_TINY + SMALL_BIG


def _pad_rows(flat, rows):
    return jnp.pad(flat, (0, rows * D - flat.shape[0])).reshape(rows, D)


def _cols_from_chips(a, nl):
    _, rows, cols = a.shape
    return a.reshape(NCHIP, nl, rows // nl, cols).transpose(1, 2, 0, 3).reshape(nl, rows // nl, NCHIP * cols)


SMALL_ROWS = sum(-(-math.prod(s) // (8 * D)) * 8 for _, s in SMALL)
CONV_ROWS = DEPTH * 4 * 3 * D // D


def _pack_small(vals, extra=()):
    parts = []
    for val in tuple(vals) + tuple(extra):
        n = val.size
        parts.append(_pad_rows(val.reshape(-1), -(-n // (8 * D)) * 8))
    return jnp.concatenate(parts, axis=0)


def _packed_rows(entries):
    return sum(-(-math.prod(s) // (8 * D)) * 8 for _, s in entries)


def _unpack_small(flat, entries):
    outs, r0 = [], 0
    for _, shape in entries:
        n = math.prod(shape)
        rows = -(-n // (8 * D)) * 8
        outs.append(flat[r0:r0 + rows].reshape(-1)[:n].reshape(shape))
        r0 += rows
    return outs


LOGITS_IN_CHIP1 = 2 * WIN_SHARD - 4 * D
LOGITS_IN_CHIP2 = 2 * NH - LOGITS_IN_CHIP1


def _wcat_from_chips(g):
    before = WIN_SHARD - LOGITS_IN_CHIP1
    pad = jnp.zeros((D, NCOL - BD0 - 2 * NH), g.dtype)
    return jnp.concatenate([g[0], g[1][:, :before], g[2][:, LOGITS_IN_CHIP2:], g[3], g[1][:, before:],
                            g[2][:, :LOGITS_IN_CHIP2], pad], axis=1)[None]


def _wcat_grad_by_chip(gw):
    before, mid = WIN_SHARD - LOGITS_IN_CHIP1, 4 * D + WIN_SHARD - LOGITS_IN_CHIP2
    return jnp.stack([gw[:, :WIN_SHARD],
                      jnp.concatenate([gw[:, WIN_SHARD:4 * D], gw[:, BD0:BD0 + LOGITS_IN_CHIP1]], axis=1),
                      jnp.concatenate([gw[:, BD0 + LOGITS_IN_CHIP1:BD0 + 2 * NH], gw[:, 4 * D:mid]], axis=1),
                      gw[:, mid:BD0]])


def _block_diag_b(bb2):
    b = bb2.reshape(-1, NCB, GPB, NS, GS)
    return jnp.einsum("lkgnc,gh->lkgchn", b, jnp.eye(GPB, dtype=F32)).reshape(-1, NCB, GPB * GS, SW)


def _block_diag_b_t(d):
    blocks = jnp.einsum("lkgchn,gh->lkgnc", d.reshape(-1, NCB, GPB, GS, GPB, NS), jnp.eye(GPB, dtype=F32))
    return blocks.reshape(-1, NG, NS * GS)


def _block_diag_c(c):
    blocks = jnp.einsum("lkgcn,gh->lkgnhc", c.reshape(-1, NCB, GPB, GS, NS), jnp.eye(GPB, dtype=F32))
    return blocks.reshape(-1, NCB, SW, GPB * GS)


def _block_diag_c_t(d):
    blocks = jnp.einsum("lkgnhc,gh->lkgcn", d.reshape(-1, NCB, GPB, NS, GPB, GS), jnp.eye(GPB, dtype=F32))
    return blocks.reshape(-1, NG, GS, NS)


def _local_step(x, target, weights, conv, small, comm=None):
    weights = list(weights) + [None] * (DEPTH - len(weights))
    ar, ai = small["ssm_a_re"], small["ssm_a_im"]
    ldt = small["ssm_log_dt"].reshape(DEPTH, NG, 1)
    br2 = small["ssm_b_re"].reshape(DEPTH, NG, NS * GS)
    bi2 = small["ssm_b_im"].reshape(DEPTH, NG, NS * GS)
    lr, li, bbr2, bbi2 = _s5_params(ar, ai, ldt, br2, bi2)

    def row(name, l, width):
        return small[name][l].reshape(1, width)

    gvecs = jnp.pad(jnp.stack([small["a_log"], small["dt_bias"]], axis=1), ((0, 0), (0, 6), (NH, DH - 2 * NH)))
    lams = jnp.stack([lr.reshape(DEPTH, 1, NG * NS), li.reshape(DEPTH, 1, NG * NS)], axis=1)
    bblks = jnp.stack([_block_diag_b(bbr2), _block_diag_b(bbi2)], axis=1)
    cblks = jnp.stack([_block_diag_c(small["ssm_c_re"]), _block_diag_c(small["ssm_c_im"])], axis=1)
    saved = []
    for l in range(DEPTH):
        gvec, lam, bblk, cblk = gvecs[l], lams[l], bblks[l], cblks[l]
        wcat, wglu, wout = weights[l]
        fetch = [None] * 3
        if comm and l + 1 < DEPTH:
            fetch = [_Exchange(comm["weight_parts"](l + 1, part), CHIP_RELS, True, False) for part in range(3)]
        proj, h, got0 = _inproj_fwd(x, row("norm_pre", l, D), wcat, 0, fetch[0])
        qkv = _prep_fwd(proj, conv[l])
        bb, gcb = _gates_fwd(proj, gvec)
        local, t_inv, got1 = _delta_local_fwd(qkv, bb, gcb, fetch[1])
        o, states = _delta_state_fwd(local, gcb)
        s5y, carries, got2 = _s5_fwd(proj, lam, bblk, cblk, fetch[2])
        if fetch[0] is not None:
            weights[l + 1] = comm["weights_from"]([got0, got1, got2])
        xn = _mix_fwd(proj, o, s5y, x, row("head_norm", l, DH), row("ssm_d", l, D), wglu, row("b_glu", l, D),
                      wout, row("norm_post", l, D), 0)
        saved.append((x, proj, h, qkv, bb, gcb, local, t_inv, o, states, s5y, carries, gvec, lam, bblk, cblk))
        x = xn

    dx, loss_part = _loss_grad(x, target)

    g = {k: [None] * DEPTH for k in ("wcat", "conv", "wglu", "wout", "norm_pre", "a_log", "dt_bias", "head_norm",
                                     "ssm_d", "b_glu", "norm_post", "dlam", "dbblk", "dcblk")}
    from_chips, send, send_layer = [None] * DEPTH, None, None
    for l in reversed(range(DEPTH)):
        xl, proj, h, qkv, bb, gcb, local, t_inv, o, states, s5y, carries, gvec, lam, bblk, cblk = saved[l]
        wcat, wglu, wout = weights[l]
        (dza, du_skip, dzb, dra, drb, do, ds5y, dxres, dwg, dwo, dvecs, dhn) = _mix_bwd(
            proj, o, s5y, xl, row("head_norm", l, DH), row("ssm_d", l, D), wglu, row("b_glu", l, D), wout,
            row("norm_post", l, D), dx, 0)
        (du, dlam, dbblk, dcblk), arrived = _s5_bwd(proj, lam, bblk, cblk, carries, ds5y, du_skip, send)
        if send is not None:
            from_chips[send_layer] = arrived
        *dlocal, dgcb_state = _delta_state_bwd(local, gcb, states, do)
        dq, dk, dv, dbb, dgcb = _delta_local_bwd(qkv, bb, gcb, t_inv, dlocal, dgcb_state)
        dbd, dgvec = _gates_bwd(proj, gvec, dbb, dgcb)
        dpre, dconv = _prep_bwd(proj, conv[l], dq, dk, dv)
        dproj = jnp.concatenate([dpre, dza, du, dzb, dra, drb, dbd], axis=1)
        g["wcat"][l] = _inproj_bwd_dw(h, dproj)
        g["conv"][l], g["wglu"][l], g["wout"][l] = dconv, dwg, dwo
        send = _Exchange(comm["grad_parts"](g["wcat"][l], dwg, dwo), CHIP_RELS, True, True) if comm else None
        dx, dgain, arrived = _inproj_bwd_dx(dproj, wcat, xl, row("norm_pre", l, D), dxres, 0, send if l == 0 else None)
        if comm and l == 0:
            from_chips[l] = arrived
        send_layer = l
        g["norm_pre"][l] = dgain[0]
        g["a_log"][l], g["dt_bias"][l] = dgvec[0, NH:2 * NH], dgvec[1, NH:2 * NH]
        g["head_norm"][l] = dhn[0]
        g["ssm_d"][l], g["b_glu"][l], g["norm_post"][l] = dvecs[0], dvecs[1], dvecs[2]
        g["dlam"][l], g["dbblk"][l], g["dcblk"][l] = dlam, dbblk, dcblk
    if comm:
        for k in ("wcat", "wglu", "wout"):
            del g[k]
    g = {k: jnp.stack(v) for k, v in g.items()}
    g["from_chips"] = from_chips
    dlam, dbblk, dcblk = g.pop("dlam"), g.pop("dbblk"), g.pop("dcblk")
    g["ssm_c_re"], g["ssm_c_im"] = _block_diag_c_t(dcblk[:, 0]), _block_diag_c_t(dcblk[:, 1])
    dar, dai, dldt, dbr2, dbi2 = _s5_params_bwd(
        ar, ai, ldt, br2, bi2, dlam[:, 0].reshape(DEPTH, NG, NS), dlam[:, 1].reshape(DEPTH, NG, NS),
        _block_diag_b_t(dbblk[:, 0]), _block_diag_b_t(dbblk[:, 1]))
    g["ssm_a_re"], g["ssm_a_im"], g["ssm_log_dt"] = dar, dai, dldt.reshape(DEPTH, NG)
    g["ssm_b_re"] = dbr2.reshape(DEPTH, NG, NS, GS)
    g["ssm_b_im"] = dbi2.reshape(DEPTH, NG, NS, GS)
    return loss_part[0, 0], dx, g


def kernel(x, norm_pre, w_in, conv_w, a_log, dt_bias, head_norm, ssm_a_re, ssm_a_im, ssm_log_dt, ssm_b_re, ssm_b_im, ssm_c_re, ssm_c_im, ssm_d, w_glu, b_glu, w_out, norm_post, loss_target, m_norm_pre, m_w_in, m_conv_w, m_a_log, m_dt_bias, m_head_norm, m_ssm_a_re, m_ssm_a_im, m_ssm_log_dt, m_ssm_b_re, m_ssm_b_im, m_ssm_c_re, m_ssm_c_im, m_ssm_d, m_w_glu, m_b_glu, m_w_out, m_norm_post, v_norm_pre, v_w_in, v_conv_w, v_a_log, v_dt_bias, v_head_norm, v_ssm_a_re, v_ssm_a_im, v_ssm_log_dt, v_ssm_b_re, v_ssm_b_im, v_ssm_c_re, v_ssm_c_im, v_ssm_d, v_w_glu, v_b_glu, v_w_out, v_norm_post):
    args = dict(locals())
    small = {n: args[n] for n, _ in SMALL}

    def flat2(a):
        return a.reshape(-1, a.shape[-1])

    w_in16, w_glu16, w_out16 = w_in.astype(BF16), w_glu.astype(BF16), w_out.astype(BF16)

    def weight_parts(l, part=None):
        if part is None:
            return [w_in16[l], w_glu16[l], w_out16[l]]
        lo, hi = WEIGHT_SPLIT[part], WEIGHT_SPLIT[part + 1]
        return [w_in16[l, lo:hi]] + [[w_glu16[l]], [w_out16[l]], []][part]

    def weights_from(parts):
        if len(parts) == 3 and isinstance(parts[0], (list, tuple)):
            parts = [jnp.concatenate([p[0] for p in parts], axis=1), parts[0][1], parts[1][1]]
        g_in, g_glu, g_out = parts[:3]
        return _wcat_from_chips(g_in), g_glu.reshape(1, D, D), g_out.reshape(1, D, D)

    def grad_parts(gwcat, gwglu, gwout):
        return [_wcat_grad_by_chip(gwcat).astype(BF16), gwglu.reshape(NCHIP, ROW_SHARD, D).astype(BF16),
                gwout.reshape(NCHIP, ROW_SHARD, D).astype(BF16)]

    first = _exchange(weight_parts(0) + [flat2(conv_w)], CHIP_RELS, True, False, "gather_weights")
    conv = _cols_from_chips(first[3], DEPTH)
    comm = dict(weight_parts=weight_parts, weights_from=weights_from, grad_parts=grad_parts)
    loss_part, dx, g = _local_step(x[0], loss_target[0], [weights_from(first)], conv, small, comm)
    loss = lax.psum(loss_part, ("x", "y", "c"))

    from_chips = [jnp.concatenate([g["from_chips"][l][a] for l in range(DEPTH)], axis=1) for a in range(3)]
    core_sums = [_sum_slots(p, "sum_chips_" + n) for p, n in zip(from_chips, ("in", "glu", "out"))]
    others = _sibling_swap(core_sums, "swap_cores")
    sharded = {}
    for n, mine, other in zip(("w_in", "w_glu", "w_out"), core_sums, others):
        sharded[n] = _adamw(args[n], [mine, other], args["m_" + n], args["v_" + n], "adamw_" + n, max_rows=128)

    pad = jnp.zeros(((-(SMALL_ROWS + CONV_ROWS)) % (8 * NDEV), D), F32)
    small_sum = _all_reduce(_pack_small([g[n] for n, _ in SMALL], extra=[g["conv"], pad]), "reduce_small")
    small_out = _adamw(_pack_small([args[n] for n, _ in SMALL_TINY]), [small_sum],
                       _pack_small([args["m_" + n] for n, _ in SMALL_TINY]),
                       _pack_small([args["v_" + n] for n, _ in SMALL_TINY]), "adamw_small")
    row = _packed_rows(SMALL_TINY)
    for n, shape in SMALL_BIG:
        rows = _packed_rows([(n, shape)])
        g_own = small_sum[row:row + rows].reshape(shape)
        sharded[n] = [g_own] + list(_adamw_layers(args[n], g_own, args["m_" + n], args["v_" + n], "adamw_" + n))
        row += rows
    chip = 2 * lax.axis_index("x") + lax.axis_index("y")
    conv_sum = small_sum[SMALL_ROWS:SMALL_ROWS + CONV_ROWS].reshape(DEPTH * 4, 3 * D)
    conv_sum = lax.dynamic_slice_in_dim(conv_sum, chip * CONV_SHARD, CONV_SHARD, axis=1)
    sharded["conv_w"] = _adamw(flat2(conv_w), [conv_sum], flat2(m_conv_w), flat2(v_conv_w), "adamw_conv")

    names = ["norm_pre", "w_in", "conv_w", "a_log", "dt_bias", "head_norm", "ssm_a_re", "ssm_a_im", "ssm_log_dt",
             "ssm_b_re", "ssm_b_im", "ssm_c_re", "ssm_c_im", "ssm_d", "w_glu", "b_glu", "w_out", "norm_post"]
    outs = [loss, dx[None]]
    for i in range(4):
        sm = dict(zip([n for n, _ in SMALL_TINY], _unpack_small(small_out[i], SMALL_TINY)))
        outs += [sharded[n][i].reshape(args[n].shape) if n in sharded else sm[n] for n in names]
    return tuple(outs)
```

```python
import functools
import math

import jax
import jax.numpy as jnp
from jax import lax
from jax.experimental import pallas as pl
from jax.experimental.pallas import tpu as pltpu

F32 = jnp.float32
BF16 = jnp.bfloat16
HI = lax.Precision.HIGHEST

D = 1024
NH = 8
DH = 128
CH = 128
NG = 64
GS = 16
NS = 64
GPB = 8
NCB = NG // GPB
SW = GPB * NS
NCOL = 8320
BD0 = 8192
EPS = 1e-6
DEPTH = 4
NCHIP = 4
NDEV = 8
VMEM_LIMIT = 56 * 1024 * 1024
GRAD_ACT = jnp.bfloat16

ADAM_LR = 0.001
ADAM_B1 = 0.9
ADAM_B2 = 0.999
ADAM_EPS = 1e-08
ADAM_WD = 0.01
ADAM_STEP = 10


def _cparams(sem=None):
    return pltpu.CompilerParams(dimension_semantics=sem, vmem_limit_bytes=VMEM_LIMIT)


def _full(shape):
    nd = len(shape)
    return pl.BlockSpec(shape, lambda *_: (0,) * nd)


def _rms(x, gain):
    ms = jnp.mean(x * x, axis=-1, keepdims=True)
    return x * lax.rsqrt(ms + EPS) * gain


def _sigmoid(x):
    return 1.0 / (1.0 + jnp.exp(-x))


def _silu(x):
    return x * _sigmoid(x)


def _softplus(x):
    return jnp.maximum(x, 0.0) + jnp.log(1.0 + jnp.exp(-jnp.abs(x)))


def _gelu(x):
    return 0.5 * x * (1.0 + jnp.tanh(math.sqrt(2.0 / math.pi) * (x + 0.044715 * (x * x * x))))


def _dot_bf16(a, b, dims):
    return lax.dot_general(a.astype(BF16), b.astype(BF16), (dims, ((), ())), preferred_element_type=F32)


def _mm_nt(a, b):
    return _dot_bf16(a, b, ((1,), (1,)))


def _mm_tn(a, b):
    return _dot_bf16(a, b, ((0,), (0,)))


@jax.custom_vjp
def _mm(a, b):
    return _dot_bf16(a, b, ((1,), (0,)))


def _mm_fwd(a, b):
    return _dot_bf16(a, b, ((1,), (0,))), (a, b)


def _mm_bwd(res, ct):
    a, b = res
    return _mm_nt(ct, b).astype(a.dtype), _mm_tn(a, ct).astype(b.dtype)


_mm.defvjp(_mm_fwd, _mm_bwd)


@jax.custom_vjp
def _mm_nt_d(a, b):
    return _mm_nt(a, b)


def _mm_nt_d_bwd(res, ct):
    a, b = res
    return _dot_bf16(ct, b, ((1,), (0,))), _mm_tn(ct, a)


_mm_nt_d.defvjp(lambda a, b: (_mm_nt(a, b), (a, b)), _mm_nt_d_bwd)


@jax.custom_vjp
def _mm_tn_d(a, b):
    return _mm_tn(a, b)


def _mm_tn_d_bwd(res, ct):
    a, b = res
    return _mm_nt(b, ct), _dot_bf16(a, ct, ((1,), (0,)))


_mm_tn_d.defvjp(lambda a, b: (_mm_tn(a, b), (a, b)), _mm_tn_d_bwd)


def _split_bf16(a):
    hi = a.astype(BF16)
    return hi, (a - hi.astype(F32)).astype(BF16)


def _dot3(a, b, dims):
    ah, al = _split_bf16(a)
    bh, bl = _split_bf16(b)

    def dot(x, y):
        return lax.dot_general(x, y, (dims, ((), ())), preferred_element_type=F32)

    return dot(ah, bh) + (dot(ah, bl) + dot(al, bh))


@jax.custom_vjp
def _imm(a, b):
    return _dot3(a, b, ((1,), (0,)))


def _imm_bwd(res, ct):
    a, b = res
    return _dot3(ct, b, ((1,), (1,))), _dot3(a, ct, ((0,), (0,)))


_imm.defvjp(lambda a, b: (_dot3(a, b, ((1,), (0,))), (a, b)), _imm_bwd)


def _hmm(a, b):
    return jnp.dot(a, b, precision=HI, preferred_element_type=F32)


def _rows(shape):
    return lax.broadcasted_iota(jnp.int32, shape, 0)


def _cols(shape):
    return lax.broadcasted_iota(jnp.int32, shape, 1)


def _sd(x, s):
    return jnp.where(_rows(x.shape) >= s, pltpu.roll(x, s, axis=0), 0.0)


def _su(x, s):
    n = x.shape[0]
    return jnp.where(_rows(x.shape) < n - s, pltpu.roll(x, n - s, axis=0), 0.0)


@functools.partial(jax.custom_vjp, nondiff_argnums=(1,))
def _shift_down(x, s):
    return _sd(x, s)


def _shift_down_fwd(x, s):
    return _sd(x, s), None


def _shift_down_bwd(s, _, g):
    return (_su(g, s),)


_shift_down.defvjp(_shift_down_fwd, _shift_down_bwd)


def _last_row(x):
    n = x.shape[0]
    return jnp.sum(jnp.where(_rows(x.shape) == n - 1, x, 0.0), axis=0, keepdims=True)


def _prep_fn(p, w0, w1, w2, w3, qk):
    acc = w3 * p + w2 * _shift_down(p, 1) + w1 * _shift_down(p, 2) + w0 * _shift_down(p, 3)
    a = _silu(acc)
    nrm = lax.rsqrt(jnp.sum(a * a, axis=-1, keepdims=True) + EPS)
    return a * (nrm * qk + (1.0 - qk))


def _gates_fn(bd, av, bv):
    tm = bd.shape[0]
    beta_all = _sigmoid(bd)
    g_all = -jnp.exp(av) * _softplus(bd + bv)
    r, c = _rows((tm, tm)), _cols((tm, tm))
    tri = jnp.where((r // CH == c // CH) & (r >= c), 1.0, 0.0).astype(F32)
    gc_all = _hmm(tri, g_all)
    lane = _cols(bd.shape)
    outs = []
    for h in range(NH):
        b = jnp.sum(jnp.where(lane == h, beta_all, 0.0), axis=1, keepdims=True)
        outs.append(jnp.broadcast_to(b, bd.shape))
    for h in range(NH):
        g = jnp.sum(jnp.where(lane == NH + h, gc_all, 0.0), axis=1, keepdims=True)
        outs.append(jnp.broadcast_to(g, bd.shape))
    return tuple(outs)


INV_BASE = 2


def _merge_mm(a, b):
    return _dot_bf16(a, b, ((1,), (0,)))


def _unit_lower_inv(l_mats):
    n = l_mats[0].shape[0]
    ii, jj = _rows((n, n)), _cols((n, n))
    base = ii // INV_BASE == jj // INV_BASE
    ps = [-jnp.where(base, l_mat, 0.0) for l_mat in l_mats]
    eye = jnp.where(ii == jj, 1.0, 0.0).astype(F32)
    ds = [eye + p for p in ps]
    k = 1
    while 2 * k < INV_BASE:
        ps = [_imm(p, p) for p in ps]
        ds = [d + _imm(d, p) for d, p in zip(ds, ps)]
        k *= 2
    m = INV_BASE
    while m < n:
        pair = (ii // (2 * m) == jj // (2 * m)) & (ii // m > jj // m)
        des = [_merge_mm(d, jnp.where(pair, l_mat, 0.0)) for d, l_mat in zip(ds, l_mats)]
        ds = [d - _merge_mm(de, d) for d, de in zip(ds, des)]
        m *= 2
    return ds


@jax.custom_vjp
def _known_inverse(l_mat, t_inv):
    return t_inv


def _known_inverse_bwd(t_inv, ct):
    d_l = -_dot3(_dot3(t_inv, ct, ((0,), (0,))), t_inv, ((1,), (1,)))
    return d_l, jnp.zeros_like(t_inv)


_known_inverse.defvjp(lambda l_mat, t_inv: (t_inv, t_inv), _known_inverse_bwd)


def _chunk_system(q, k, v, bb, gcb):
    qs = q * (DH ** -0.5)
    kb = k * bb
    eg = jnp.exp(gcb)
    ii, jj = _rows((CH, CH)), _cols((CH, CH))
    decay = jnp.exp(jnp.where(ii >= jj, gcb - gcb.T, -1e30))
    l_mat = jnp.where(ii > jj, _mm_nt_d(kb, k) * decay, 0.0)
    a_qk = _mm_nt_d(qs, k) * decay
    k_dec = k * jnp.exp(_last_row(gcb) - gcb)
    return l_mat, (v * bb, kb * eg, qs * eg, k_dec, a_qk)


def _chunk_solve(t_inv, rest):
    vb, kbe, q_dec, k_dec, a_qk = rest
    return _mm(t_inv, vb), _mm(t_inv, kbe), q_dec, k_dec, a_qk


def _side_by_side_vjp(fn, items, cts):
    n = len(items[0])
    _, vjp = jax.vjp(lambda *flat: fn([flat[i * n:(i + 1) * n] for i in range(len(items))]),
                     *[a for item in items for a in item])
    grads = vjp(cts)
    return [grads[i * n:(i + 1) * n] for i in range(len(items))]


def _chunks_local_known(items):
    systems = [_chunk_system(*item[:5]) for item in items]
    t_invs = [_known_inverse(l_mat, item[5]) for (l_mat, _), item in zip(systems, items)]
    return [_chunk_solve(t_inv, rest) for t_inv, (_, rest) in zip(t_invs, systems)]


def _chunks_local(chunks):
    systems = [_chunk_system(*c) for c in chunks]
    t_invs = _unit_lower_inv([l_mat for l_mat, _ in systems])
    return [(_chunk_solve(t_inv, rest), t_inv) for t_inv, (_, rest) in zip(t_invs, systems)]


def _state_steps(items):
    v_news = [u - _mm(w, state) for u, w, _, _, _, _, state in items]
    outs = [_mm(q_dec, state) + _mm(a_qk, v_new) for (_, _, q_dec, _, a_qk, _, state), v_new in zip(items, v_news)]
    states = [state * jnp.exp(_last_row(gcb)) + _mm_tn_d(k_dec, v_new)
              for (_, _, _, k_dec, _, gcb, state), v_new in zip(items, v_news)]
    return list(zip(outs, states))


SUB = 8


SCAN_ROWS = 32


def _cmul(ar, ai, br, bi):
    return ar * br - ai * bi, ar * bi + ai * br


def _scan_tile(xr, xi, mr, mi, hr_ref, hi_ref, cr_ref, ci_ref, reverse, fold):
    n, width = xr.shape
    ngroups, nlb = n // SUB, width // DH
    shift_groups = _su if reverse else _sd

    def lanes(x, j):
        return x[..., j * DH:(j + 1) * DH]

    start = n - 1 if reverse else 0
    for j in range(nlb):
        hr_ref[j] = lanes(xr, j)
        hi_ref[j] = lanes(xi, j)
        hr_ref[j, start:start + 1, :] += lanes(fold[0], j)
        hi_ref[j, start:start + 1, :] += lanes(fold[1], j)
    pr, pi = mr, mi
    tr, ti = jnp.broadcast_to(mr, (SUB, width)), jnp.broadcast_to(mi, (SUB, width))
    pos = _rows(tr.shape)
    steps = []
    s = 1
    while s < SUB:
        inside = pos < SUB - s if reverse else pos >= s
        shift = SUB - s if reverse else s
        steps.append((shift, jnp.where(inside, pr, 0.0), jnp.where(inside, pi, 0.0)))
        er = jnp.where(inside, pltpu.roll(tr, shift, axis=0), 1.0)
        ei = jnp.where(inside, pltpu.roll(ti, shift, axis=0), 0.0)
        tr, ti = _cmul(tr, ti, er, ei)
        pr, pi = _cmul(pr, pi, pr, pi)
        s *= 2
    for b in range(0, n, SCAN_ROWS):
        rows = slice(b, b + SCAN_ROWS)
        for j in range(nlb):
            br = hr_ref[j, rows, :].reshape(SCAN_ROWS // SUB, SUB, DH)
            bi = hi_ref[j, rows, :].reshape(SCAN_ROWS // SUB, SUB, DH)
            for shift, qr, qi in steps:
                dr, di = _cmul(lanes(qr, j)[None], lanes(qi, j)[None],
                               pltpu.roll(br, shift, axis=1), pltpu.roll(bi, shift, axis=1))
                br, bi = br + dr, bi + di
            hr_ref[j, rows, :] = br.reshape(SCAN_ROWS, DH)
            hi_ref[j, rows, :] = bi.reshape(SCAN_ROWS, DH)
    edge = pl.ds(0 if reverse else SUB - 1, ngroups, stride=SUB)
    gr = jnp.concatenate([hr_ref.at[j][edge, :] for j in range(nlb)], axis=1)
    gi = jnp.concatenate([hi_ref.at[j][edge, :] for j in range(nlb)], axis=1)
    s = 1
    while s < ngroups:
        dr, di = _cmul(pr, pi, shift_groups(gr, s), shift_groups(gi, s))
        gr, gi = gr + dr, gi + di
        pr, pi = _cmul(pr, pi, pr, pi)
        s *= 2
    cr_ref[...] = shift_groups(gr, 1)
    ci_ref[...] = shift_groups(gi, 1)
    for g in range(ngroups):
        rows = slice(g * SUB, (g + 1) * SUB)
        dr, di = _cmul(tr, ti, cr_ref[g:g + 1, :], ci_ref[g:g + 1, :])
        for j in range(nlb):
            hr_ref[j, rows, :] += lanes(dr, j)
            hi_ref[j, rows, :] += lanes(di, j)
    return (jnp.concatenate([hr_ref[j] for j in range(nlb)], axis=1),
            jnp.concatenate([hi_ref[j] for j in range(nlb)], axis=1))


def _s5_states(u, lam_ref, b_ref, car_ref, hr_ref, hi_ref, cr_ref, ci_ref):
    lr, li = lam_ref[0], lam_ref[1]
    fold = _cmul(lr, li, car_ref[0:1, :], car_ref[1:2, :])
    return _scan_tile(_mm(u, b_ref[0]), _mm(u, b_ref[1]), lr, li, hr_ref, hi_ref, cr_ref, ci_ref, False, fold)


def _scratch_row(ref, row):
    return jnp.concatenate([ref[j, row:row + 1, :] for j in range(ref.shape[0])], axis=1)


def _s5_params_fn(ar, ai, ldt, br2, bi2):
    dt = jnp.exp(ldt)
    mag = jnp.exp(ar * dt)
    lr, li = mag * jnp.cos(ai * dt), mag * jnp.sin(ai * dt)
    den = ar * ar + ai * ai
    fr = ((lr - 1.0) * ar + li * ai) / den
    fi = (li * ar - (lr - 1.0) * ai) / den
    expand = jnp.where(_cols((NS, NS * GS)) // GS == _rows((NS, NS * GS)), 1.0, 0.0).astype(F32)
    fr2, fi2 = _hmm(fr, expand), _hmm(fi, expand)
    return lr, li, fr2 * br2 - fi2 * bi2, fr2 * bi2 + fi2 * br2


def _head_norm(o, hn):
    parts = []
    for h in range(NH):
        oh = o[:, h * DH:(h + 1) * DH]
        parts.append(oh * lax.rsqrt(jnp.mean(oh * oh, axis=-1, keepdims=True) + EPS) * hn)
    return jnp.concatenate(parts, axis=1)


def _mix_pre(s5y, u, dvec):
    return _gelu(s5y + dvec * u)


def _mix_mid(o, za, y0, gl, zb, ra, rb, hn):
    ya = _head_norm(o, hn) * _silu(za)
    yb = y0 * _sigmoid(gl) * _silu(zb)
    return _sigmoid(ra) * ya + _sigmoid(rb) * yb


def _mix_post(x, out, npost):
    return x + _rms(out, npost)


def _tile(t, want):
    return min(t, want)


def _row_tile(rows, want):
    return max(r for r in range(16, want + 1, 16) if rows % r == 0)


def _call_carrying(body, name, grid, in_specs, out_specs, out_shape, scratch, args, semantics, exchange):
    n_out = len(out_shape)
    if exchange is not None:
        body = _carry(body, len(args), n_out, len(scratch), exchange, grid)
        in_specs, out_specs = in_specs + exchange.in_specs, out_specs + exchange.out_specs
        out_shape, scratch, args = out_shape + exchange.out_shape, scratch + exchange.scratch_shapes, args + exchange.srcs
        semantics = ("arbitrary",) * len(grid)
    outs = pl.pallas_call(body, name=name, grid=grid, in_specs=in_specs, out_specs=out_specs, out_shape=out_shape,
                          scratch_shapes=scratch, compiler_params=_cparams(semantics))(*args)
    return outs[:n_out], outs[n_out:]


def _inproj_fwd(x, gain, wcat, l, exchange=None):
    t = x.shape[0]
    tm, tn = _tile(t, 1024), 1664

    def body(x_ref, g_ref, w_ref, o_ref, h_ref):
        @pl.when(pl.program_id(1) == 0)
        def _():
            h_ref[...] = _rms(x_ref[...], g_ref[...]).astype(h_ref.dtype)
        o_ref[...] = _dot_bf16(h_ref[...], w_ref[...], ((1,), (0,)))

    (proj, h), fetched = _call_carrying(
        body, "inproj_fwd", (t // tm, NCOL // tn),
        [pl.BlockSpec((tm, D), lambda i, j: (i, 0)), _full((1, D)), pl.BlockSpec((None, D, tn), lambda i, j: (l, 0, j))],
        [pl.BlockSpec((tm, tn), lambda i, j: (i, j)), pl.BlockSpec((tm, D), lambda i, j: (i, 0))],
        [jax.ShapeDtypeStruct((t, NCOL), F32), jax.ShapeDtypeStruct((t, D), wcat.dtype)],
        [], [x, gain, wcat], ("parallel", "arbitrary"), exchange)
    return proj, h, fetched


def _inproj_bwd_dx(dproj, wcat, x, gain, dxres, l, exchange=None):
    t = x.shape[0]
    tm, tk = _tile(t, 1024), 1664
    nk = NCOL // tk

    def body(dp_ref, w_ref, x_ref, g_ref, r_ref, dx_ref, dg_ref, acc_ref):
        i, k = pl.program_id(0), pl.program_id(1)

        @pl.when(k == 0)
        def _():
            acc_ref[...] = jnp.zeros_like(acc_ref)

        acc_ref[...] += _mm_nt(dp_ref[...], w_ref[...])

        @pl.when(k == nk - 1)
        def _():
            _, vjp = jax.vjp(_rms, x_ref[...], g_ref[...])
            dx, dg = vjp(acc_ref[...])
            dx_ref[...] = r_ref[...] + dx

            @pl.when(i == 0)
            def _():
                dg_ref[...] = dg

            @pl.when(i > 0)
            def _():
                dg_ref[...] += dg

    grid = (t // tm, nk)
    in_specs = [pl.BlockSpec((tm, tk), lambda i, k: (i, k)), pl.BlockSpec((None, D, tk), lambda i, k: (l, 0, k)),
                pl.BlockSpec((tm, D), lambda i, k: (i, 0)), _full((1, D)), pl.BlockSpec((tm, D), lambda i, k: (i, 0))]
    out_specs = [pl.BlockSpec((tm, D), lambda i, k: (i, 0)), _full((1, D))]
    out_shape = [jax.ShapeDtypeStruct((t, D), F32), jax.ShapeDtypeStruct((1, D), F32)]
    scratch, args = [pltpu.VMEM((tm, D), F32)], [dproj, wcat, x, gain, dxres]
    if exchange is not None:
        body = _carry(body, len(args), len(out_shape), len(scratch), exchange, grid)
        in_specs, out_specs = in_specs + exchange.in_specs, out_specs + exchange.out_specs
        out_shape, scratch, args = out_shape + exchange.out_shape, scratch + exchange.scratch_shapes, args + exchange.srcs
    outs = pl.pallas_call(
        body, name="inproj_bwd_dx", grid=grid, in_specs=in_specs, out_specs=out_specs, out_shape=out_shape,
        scratch_shapes=scratch, compiler_params=_cparams(("arbitrary", "arbitrary")),
    )(*args)
    return outs[0], outs[1], outs[2:]


def _inproj_bwd_dw(h, dproj):
    t = h.shape[0]
    tm, tn = _tile(t, 1024), 1664

    def body(h_ref, dp_ref, o_ref):
        @pl.when(pl.program_id(1) == 0)
        def _():
            o_ref[...] = jnp.zeros_like(o_ref)

        o_ref[...] += _mm_tn(h_ref[...], dp_ref[...])

    return pl.pallas_call(
        body, name="inproj_bwd_dw", grid=(NCOL // tn, t // tm),
        in_specs=[pl.BlockSpec((tm, D), lambda j, i: (i, 0)), pl.BlockSpec((tm, tn), lambda j, i: (i, j))],
        out_specs=pl.BlockSpec((D, tn), lambda j, i: (0, j)),
        out_shape=jax.ShapeDtypeStruct((D, NCOL), F32),
        compiler_params=_cparams(("parallel", "arbitrary")),
    )(h, dproj)


def _prep_fwd(proj, cw):
    t = proj.shape[0]

    def body(p_ref, w_ref, o_ref):
        qk = (pl.program_id(0) < 2 * NH).astype(F32)
        o_ref[...] = _prep_fn(p_ref[...], w_ref[0:1, :], w_ref[1:2, :], w_ref[2:3, :], w_ref[3:4, :], qk)

    return pl.pallas_call(
        body, name="prep_fwd", grid=(3 * NH,),
        in_specs=[pl.BlockSpec((t, DH), lambda c: (0, c)), pl.BlockSpec((4, DH), lambda c: (0, c))],
        out_specs=pl.BlockSpec((None, t, DH), lambda c: (c, 0, 0)),
        out_shape=jax.ShapeDtypeStruct((3 * NH, t, DH), F32),
        compiler_params=_cparams(("parallel",)),
    )(proj, cw)


def _prep_bwd(proj, cw, dq, dk, dv):
    t = proj.shape[0]

    def body(p_ref, w_ref, dq_ref, dk_ref, dv_ref, dp_ref, dw_ref):
        c = pl.program_id(0)
        qk = (c < 2 * NH).astype(F32)
        _, vjp = jax.vjp(lambda p, w0, w1, w2, w3: _prep_fn(p, w0, w1, w2, w3, qk),
                         p_ref[...], w_ref[0:1, :], w_ref[1:2, :], w_ref[2:3, :], w_ref[3:4, :])
        d = jnp.where(c < NH, dq_ref[...], jnp.where(c < 2 * NH, dk_ref[...], dv_ref[...]))
        dp, dw0, dw1, dw2, dw3 = vjp(d)
        dp_ref[...] = dp.astype(dp_ref.dtype)
        dw_ref[0:1, :] = dw0
        dw_ref[1:2, :] = dw1
        dw_ref[2:3, :] = dw2
        dw_ref[3:4, :] = dw3

    return pl.pallas_call(
        body, name="prep_bwd", grid=(3 * NH,),
        in_specs=[pl.BlockSpec((t, DH), lambda c: (0, c)), pl.BlockSpec((4, DH), lambda c: (0, c))]
        + [pl.BlockSpec((None, t, DH), functools.partial(lambda c, off: (jnp.clip(c - off, 0, NH - 1), 0, 0), off=off))
           for off in (0, NH, 2 * NH)],
        out_specs=[pl.BlockSpec((t, DH), lambda c: (0, c)), pl.BlockSpec((4, DH), lambda c: (0, c))],
        out_shape=[jax.ShapeDtypeStruct((t, 3 * D), GRAD_ACT), jax.ShapeDtypeStruct((4, 3 * D), F32)],
        compiler_params=_cparams(("arbitrary",)),
    )(proj, cw, dq, dk, dv)


def _gates_fwd(proj, gvec):
    t = proj.shape[0]
    tm = _tile(t, 512)

    def body(p_ref, gv_ref, b_ref, g_ref):
        outs = _gates_fn(p_ref[...], gv_ref[0:1, :], gv_ref[1:2, :])
        for h in range(NH):
            b_ref[h] = outs[h]
            g_ref[h] = outs[NH + h]

    spec = pl.BlockSpec((NH, tm, DH), lambda i: (0, i, 0))
    return pl.pallas_call(
        body, name="gates_fwd", grid=(t // tm,),
        in_specs=[pl.BlockSpec((tm, DH), lambda i: (i, BD0 // DH)), _full((8, DH))],
        out_specs=[spec, spec],
        out_shape=[jax.ShapeDtypeStruct((NH, t, DH), F32)] * 2,
        compiler_params=_cparams(("parallel",)),
    )(proj, gvec)


def _gates_bwd(proj, gvec, dbb, dgcb):
    t = proj.shape[0]
    tm = _tile(t, 512)

    def body(p_ref, gv_ref, db_ref, dg_ref, dp_ref, dgv_ref):
        _, vjp = jax.vjp(_gates_fn, p_ref[...], gv_ref[0:1, :], gv_ref[1:2, :])
        cts = tuple(db_ref[h] for h in range(NH)) + tuple(dg_ref[h] for h in range(NH))
        dp, da, db = vjp(cts)
        dp_ref[...] = dp.astype(dp_ref.dtype)

        @pl.when(pl.program_id(0) == 0)
        def _():
            dgv_ref[...] = jnp.zeros_like(dgv_ref)

        dgv_ref[0:1, :] += da
        dgv_ref[1:2, :] += db

    spec = pl.BlockSpec((NH, tm, DH), lambda i: (0, i, 0))
    return pl.pallas_call(
        body, name="gates_bwd", grid=(t // tm,),
        in_specs=[pl.BlockSpec((tm, DH), lambda i: (i, BD0 // DH)), _full((8, DH)), spec, spec],
        out_specs=[pl.BlockSpec((tm, DH), lambda i: (i, 0)), _full((8, DH))],
        out_shape=[jax.ShapeDtypeStruct((t, DH), GRAD_ACT), jax.ShapeDtypeStruct((8, DH), F32)],
        compiler_params=_cparams(("arbitrary",)),
    )(proj, gvec, dbb, dgcb)


def _chunks_per_step(nch):
    return max(c for c in (8, 4, 2, 1) if nch % c == 0)


def _grid_ends(grid):
    def first():
        return functools.reduce(jnp.logical_and, [pl.program_id(a) == 0 for a in range(len(grid))])

    def last():
        return functools.reduce(jnp.logical_and, [pl.program_id(a) == n - 1 for a, n in enumerate(grid)])

    return first, last


def _carry(body, n_in, n_out, n_scratch, exchange, grid):
    first, last = _grid_ends(grid)
    na = exchange.narr

    def wrapped(*refs):
        a, b = n_in, n_in + na
        c, d = b + n_out, b + n_out + na
        e = d + n_scratch
        srcs, dsts, sems = refs[a:b], refs[c:d], refs[e:]

        @pl.when(first())
        def _():
            exchange.start(srcs, dsts, sems)

        body(*(refs[:a] + refs[b:c] + refs[d:e]))

        @pl.when(last())
        def _():
            exchange.wait(srcs, dsts, sems)

    return wrapped


def _delta_local_fwd(qkv, bb, gcb, exchange=None):
    t = qkv.shape[1]
    cps = _chunks_per_step(t // CH)
    rows = cps * CH
    grid = (NH, t // rows)

    def body(q_ref, k_ref, v_ref, b_ref, g_ref, *out_refs):
        slices = [slice(c * CH, (c + 1) * CH) for c in range(cps)]
        results = _chunks_local([tuple(ref[sl, :] for ref in (q_ref, k_ref, v_ref, b_ref, g_ref)) for sl in slices])
        for sl, (outs, t_inv) in zip(slices, results):
            for ref, val in zip(out_refs, outs + (t_inv,)):
                ref[sl, :] = val.astype(ref.dtype)

    def blk(off):
        return pl.BlockSpec((None, rows, DH), lambda h, n: (h + off, n, 0))

    in_specs = [blk(0), blk(NH), blk(2 * NH), blk(0), blk(0)]
    out_specs = [blk(0)] * 6
    out_shape = [jax.ShapeDtypeStruct((NH, t, DH), dt) for dt in (F32, BF16, BF16, BF16, BF16, F32)]
    args, scratch, sem = [qkv, qkv, qkv, bb, gcb], [], ("parallel", "parallel")
    if exchange is not None:
        body = _carry(body, 5, 6, 0, exchange, grid)
        in_specs, out_specs = in_specs + exchange.in_specs, out_specs + exchange.out_specs
        out_shape, scratch, args = out_shape + exchange.out_shape, exchange.scratch_shapes, args + exchange.srcs
        sem = ("arbitrary", "arbitrary")
    outs = pl.pallas_call(
        body, name="delta_local_fwd", grid=grid, in_specs=in_specs, out_specs=out_specs, out_shape=out_shape,
        scratch_shapes=scratch, compiler_params=_cparams(sem),
    )(*args)
    return outs[:5], outs[5], outs[6:]


def _delta_local_bwd(qkv, bb, gcb, t_inv, cts, dgcb_state):
    t = qkv.shape[1]
    cps = _chunks_per_step(t // CH)
    rows = cps * CH

    def body(q_ref, k_ref, v_ref, b_ref, g_ref, ti_ref, du_ref, dw_ref, dqd_ref, dkd_ref, da_ref, dgs_ref,
             dq_ref, dk_ref, dv_ref, db_ref, dg_ref):
        slices = [slice(c * CH, (c + 1) * CH) for c in range(cps)]
        items = [tuple(ref[sl, :] for ref in (q_ref, k_ref, v_ref, b_ref, g_ref, ti_ref)) for sl in slices]
        cts = [tuple(ref[sl, :] for ref in (du_ref, dw_ref, dqd_ref, dkd_ref, da_ref)) for sl in slices]
        for sl, (dq, dk, dv, db, dg, _) in zip(slices, _side_by_side_vjp(_chunks_local_known, items, cts)):
            dq_ref[sl, :] = dq
            dk_ref[sl, :] = dk
            dv_ref[sl, :] = dv
            db_ref[sl, :] = db
            dg_ref[sl, :] = dg + dgs_ref[sl, :]

    def blk(off):
        return pl.BlockSpec((None, rows, DH), lambda h, n: (h + off, n, 0))

    return pl.pallas_call(
        body, name="delta_local_bwd", grid=(NH, t // rows),
        in_specs=[blk(0), blk(NH), blk(2 * NH)] + [blk(0)] * 9,
        out_specs=[blk(0)] * 5,
        out_shape=[jax.ShapeDtypeStruct((NH, t, DH), F32)] * 5,
        compiler_params=_cparams(("parallel", "parallel")),
    )(qkv, qkv, qkv, bb, gcb, t_inv, *cts, dgcb_state)


def _delta_state_fwd(local, gcb):
    t = gcb.shape[1]
    nch = t // CH

    def body(u_ref, w_ref, qd_ref, kd_ref, a_ref, g_ref, o_ref, s_ref, st_ref):
        @pl.when(pl.program_id(0) == 0)
        def _():
            st_ref[...] = jnp.zeros_like(st_ref)

        s_ref[...] = st_ref[...]
        items = [tuple(ref[h].astype(F32) for ref in (u_ref, w_ref, qd_ref, kd_ref, a_ref, g_ref, st_ref))
                 for h in range(NH)]
        for h, (o, ns) in enumerate(_state_steps(items)):
            o_ref[:, h * DH:(h + 1) * DH] = o
            st_ref[h] = ns

    blk = pl.BlockSpec((NH, CH, DH), lambda n: (0, n, 0))
    return pl.pallas_call(
        body, name="delta_state_fwd", grid=(nch,),
        in_specs=[blk] * 6,
        out_specs=[pl.BlockSpec((CH, D), lambda n: (n, 0)),
                   pl.BlockSpec((NH, None, DH, DH), lambda n: (0, n, 0, 0))],
        out_shape=[jax.ShapeDtypeStruct((t, D), F32), jax.ShapeDtypeStruct((NH, nch, DH, DH), F32)],
        scratch_shapes=[pltpu.VMEM((NH, DH, DH), F32)],
        compiler_params=_cparams(("arbitrary",)),
    )(*local, gcb)


def _delta_state_bwd(local, gcb, states, do):
    t = gcb.shape[1]
    nch = t // CH

    def body(u_ref, w_ref, qd_ref, kd_ref, a_ref, g_ref, s_ref, do_ref,
             du_ref, dw_ref, dqd_ref, dkd_ref, da_ref, dg_ref, ds_ref):
        @pl.when(pl.program_id(0) == 0)
        def _():
            ds_ref[...] = jnp.zeros_like(ds_ref)

        items = [tuple(ref[h].astype(F32) for ref in (u_ref, w_ref, qd_ref, kd_ref, a_ref, g_ref, s_ref))
                 for h in range(NH)]
        cts = [(do_ref[:, h * DH:(h + 1) * DH], ds_ref[h]) for h in range(NH)]
        for h, (du, dw, dqd, dkd, da, dg, ds) in enumerate(_side_by_side_vjp(_state_steps, items, cts)):
            du_ref[h] = du
            dw_ref[h] = dw
            dqd_ref[h] = dqd
            dkd_ref[h] = dkd
            da_ref[h] = da
            dg_ref[h] = dg
            ds_ref[h] = ds

    blk = pl.BlockSpec((NH, CH, DH), lambda n: (0, nch - 1 - n, 0))
    return pl.pallas_call(
        body, name="delta_state_bwd", grid=(nch,),
        in_specs=[blk] * 6 + [pl.BlockSpec((NH, None, DH, DH), lambda n: (0, nch - 1 - n, 0, 0)),
                              pl.BlockSpec((CH, D), lambda n: (nch - 1 - n, 0))],
        out_specs=[blk] * 6,
        out_shape=[jax.ShapeDtypeStruct((NH, t, DH), F32)] * 6,
        scratch_shapes=[pltpu.VMEM((NH, DH, DH), F32)],
        compiler_params=_cparams(("arbitrary",)),
    )(*local, gcb, states, do)


def _s5_params(ar, ai, ldt, br2, bi2):
    def body(ar_ref, ai_ref, ld_ref, br_ref, bi_ref, lr_ref, li_ref, bbr_ref, bbi_ref):
        lr, li, bbr, bbi = _s5_params_fn(ar_ref[...], ai_ref[...], ld_ref[...], br_ref[...], bi_ref[...])
        lr_ref[...] = lr
        li_ref[...] = li
        bbr_ref[...] = bbr
        bbi_ref[...] = bbi

    sq = pl.BlockSpec((None, NG, NS), lambda l: (l, 0, 0))
    wide = pl.BlockSpec((None, NG, NS * GS), lambda l: (l, 0, 0))
    return pl.pallas_call(
        body, name="s5_params", grid=(DEPTH,),
        in_specs=[sq, sq, pl.BlockSpec((None, NG, 1), lambda l: (l, 0, 0)), wide, wide],
        out_specs=[sq, sq, wide, wide],
        out_shape=[jax.ShapeDtypeStruct((DEPTH, NG, NS), F32)] * 2
        + [jax.ShapeDtypeStruct((DEPTH, NG, NS * GS), F32)] * 2,
        compiler_params=_cparams(("parallel",)),
    )(ar, ai, ldt, br2, bi2)


def _s5_params_bwd(ar, ai, ldt, br2, bi2, dlr, dli, dbbr, dbbi):
    def body(ar_ref, ai_ref, ld_ref, br_ref, bi_ref, a_ref, b_ref, c_ref, d_ref,
             dar_ref, dai_ref, dld_ref, dbr_ref, dbi_ref):
        _, vjp = jax.vjp(_s5_params_fn, ar_ref[...], ai_ref[...], ld_ref[...], br_ref[...], bi_ref[...])
        dar, dai, dld, dbr, dbi = vjp((a_ref[...], b_ref[...], c_ref[...], d_ref[...]))
        dar_ref[...] = dar
        dai_ref[...] = dai
        dld_ref[...] = dld
        dbr_ref[...] = dbr
        dbi_ref[...] = dbi

    sq = pl.BlockSpec((None, NG, NS), lambda l: (l, 0, 0))
    col = pl.BlockSpec((None, NG, 1), lambda l: (l, 0, 0))
    wide = pl.BlockSpec((None, NG, NS * GS), lambda l: (l, 0, 0))
    return pl.pallas_call(
        body, name="s5_params_bwd", grid=(DEPTH,),
        in_specs=[sq, sq, col, wide, wide, sq, sq, wide, wide],
        out_specs=[sq, sq, col, wide, wide],
        out_shape=[jax.ShapeDtypeStruct((DEPTH, NG, NS), F32)] * 2 + [jax.ShapeDtypeStruct((DEPTH, NG, 1), F32)]
        + [jax.ShapeDtypeStruct((DEPTH, NG, NS * GS), F32)] * 2,
        compiler_params=_cparams(("parallel",)),
    )(ar, ai, ldt, br2, bi2, dlr, dli, dbbr, dbbi)


def _s5_tile_rows(t):
    return _tile(t // 2, 1024)


def _s5_fwd(proj, lam, bblk, cblk, exchange=None):
    t = proj.shape[0]
    r = _s5_tile_rows(t)
    nt = t // r
    u0 = 4 * D // DH

    def body(u_ref, lam_ref, b_ref, c_ref, y_ref, car_ref, st_ref, hr_ref, hi_ref, cr_ref, ci_ref):
        @pl.when(pl.program_id(1) == 0)
        def _():
            st_ref[...] = jnp.zeros_like(st_ref)

        car_ref[...] = st_ref[...]
        hr, hi = _s5_states(u_ref[...], lam_ref, b_ref, st_ref, hr_ref, hi_ref, cr_ref, ci_ref)
        y_ref[...] = _mm(hr, c_ref[0]) - _mm(hi, c_ref[1])
        st_ref[0:1, :] = _scratch_row(hr_ref, r - 1)
        st_ref[1:2, :] = _scratch_row(hi_ref, r - 1)

    scratch = ([pltpu.VMEM((8, SW), F32)] + [pltpu.VMEM((SW // DH, r, DH), F32)] * 2
               + [pltpu.VMEM((r // SUB, SW), F32)] * 2)
    (y, carries), fetched = _call_carrying(
        body, "s5_fwd", (NCB, nt),
        [pl.BlockSpec((r, DH), lambda c, i: (i, u0 + c)), pl.BlockSpec((2, 1, SW), lambda c, i: (0, 0, c)),
         pl.BlockSpec((2, None, DH, SW), lambda c, i: (0, c, 0, 0)),
         pl.BlockSpec((2, None, SW, DH), lambda c, i: (0, c, 0, 0))],
        [pl.BlockSpec((r, DH), lambda c, i: (i, c)), pl.BlockSpec((None, 8, SW), lambda c, i: (i, 0, c))],
        [jax.ShapeDtypeStruct((t, D), F32), jax.ShapeDtypeStruct((nt, 8, NG * NS), F32)],
        scratch, [proj, lam, bblk, cblk], ("parallel", "arbitrary"), exchange)
    return y, carries, fetched


def _s5_bwd(proj, lam, bblk, cblk, carries, dy, du_skip, exchange=None):
    t = proj.shape[0]
    r = _s5_tile_rows(t)
    nt = t // r
    u0 = 4 * D // DH

    def body(u_ref, lam_ref, b_ref, c_ref, car_ref, dy_ref, dus_ref, du_ref, dlam_ref, db_ref, dc_ref, dst_ref,
             hr_ref, hi_ref, ar_ref, ai_ref, cr_ref, ci_ref):
        first = pl.program_id(1) == 0

        @pl.when(first)
        def _():
            dst_ref[...] = jnp.zeros_like(dst_ref)

        u, dy = u_ref[...], dy_ref[...]
        lr, li = lam_ref[0], lam_ref[1]
        hr, hi = _s5_states(u, lam_ref, b_ref, car_ref, hr_ref, hi_ref, cr_ref, ci_ref)
        dcr2, dci2 = _mm_tn(hr, dy), -_mm_tn(hi, dy)
        fold = _cmul(lr, -li, dst_ref[0:1, :], dst_ref[1:2, :])
        ar, ai = _scan_tile(_mm_nt(dy, c_ref[0]), -_mm_nt(dy, c_ref[1]), lr, -li, ar_ref, ai_ref, cr_ref, ci_ref,
                            True, fold)
        top = _rows((r, SW)) == 0
        dst_ref[0:1, :] = _scratch_row(ar_ref, 0)
        dst_ref[1:2, :] = _scratch_row(ai_ref, 0)
        du_ref[...] = (_mm_nt(ar, b_ref[0]) + _mm_nt(ai, b_ref[1]) + dus_ref[...]).astype(du_ref.dtype)
        dbr, dbi = _mm_tn(u, ar), _mm_tn(u, ai)
        pr = _sd(hr, 1) + jnp.where(top, car_ref[0:1, :], 0.0)
        pi = _sd(hi, 1) + jnp.where(top, car_ref[1:2, :], 0.0)
        dlr = jnp.sum(ar * pr + ai * pi, axis=0, keepdims=True)
        dli = jnp.sum(ai * pr - ar * pi, axis=0, keepdims=True)

        @pl.when(first)
        def _():
            dlam_ref[0] = dlr
            dlam_ref[1] = dli
            db_ref[0] = dbr
            db_ref[1] = dbi
            dc_ref[0] = dcr2
            dc_ref[1] = dci2

        @pl.when(jnp.logical_not(first))
        def _():
            dlam_ref[0] += dlr
            dlam_ref[1] += dli
            db_ref[0] += dbr
            db_ref[1] += dbi
            dc_ref[0] += dcr2
            dc_ref[1] += dci2

    grid = (NCB, nt)
    in_specs = [pl.BlockSpec((r, DH), lambda c, i: (nt - 1 - i, u0 + c)),
                pl.BlockSpec((2, 1, SW), lambda c, i: (0, 0, c)),
                pl.BlockSpec((2, None, DH, SW), lambda c, i: (0, c, 0, 0)),
                pl.BlockSpec((2, None, SW, DH), lambda c, i: (0, c, 0, 0)),
                pl.BlockSpec((None, 8, SW), lambda c, i: (nt - 1 - i, 0, c)),
                pl.BlockSpec((r, DH), lambda c, i: (nt - 1 - i, c)),
                pl.BlockSpec((r, DH), lambda c, i: (nt - 1 - i, c))]
    out_specs = [pl.BlockSpec((r, DH), lambda c, i: (nt - 1 - i, c)),
                 pl.BlockSpec((2, 1, SW), lambda c, i: (0, 0, c)),
                 pl.BlockSpec((2, None, DH, SW), lambda c, i: (0, c, 0, 0)),
                 pl.BlockSpec((2, None, SW, DH), lambda c, i: (0, c, 0, 0))]
    out_shape = [jax.ShapeDtypeStruct((t, D), GRAD_ACT), jax.ShapeDtypeStruct((2, 1, NG * NS), F32),
                 jax.ShapeDtypeStruct((2, NCB, DH, SW), F32), jax.ShapeDtypeStruct((2, NCB, SW, DH), F32)]
    scratch = ([pltpu.VMEM((8, SW), F32)] + [pltpu.VMEM((SW // DH, r, DH), F32)] * 4
               + [pltpu.VMEM((r // SUB, SW), F32)] * 2)
    args, sem = [proj, lam, bblk, cblk, carries, dy, du_skip], ("parallel", "arbitrary")
    if exchange is not None:
        body = _carry(body, len(args), len(out_shape), len(scratch), exchange, grid)
        in_specs, out_specs = in_specs + exchange.in_specs, out_specs + exchange.out_specs
        out_shape, scratch, args = out_shape + exchange.out_shape, scratch + exchange.scratch_shapes, args + exchange.srcs
        sem = ("arbitrary", "arbitrary")
    outs = pl.pallas_call(
        body, name="s5_bwd", grid=grid, in_specs=in_specs, out_specs=out_specs, out_shape=out_shape,
        scratch_shapes=scratch, compiler_params=_cparams(sem),
    )(*args)
    return outs[:4], outs[4:]


def _proj_spec(tm, col):
    return pl.BlockSpec((tm, D), lambda i: (i, col))


def _layer_mat(l):
    return pl.BlockSpec((None, D, D), lambda i: (l, 0, 0))


def _mix_fwd(proj, o, s5y, x, hn, dvec, wglu, bglu, wout, npost, l):
    t = x.shape[0]
    tm = _tile(t, 256)

    def body(za_ref, u_ref, zb_ref, ra_ref, rb_ref, o_ref, y_ref, x_ref, hn_ref, d_ref, wg_ref, bg_ref, wo_ref,
             np_ref, xn_ref):
        y0 = _mix_pre(y_ref[...], u_ref[...], d_ref[...])
        gl = _mm(y0, wg_ref[...]) + bg_ref[...]
        m = _mix_mid(o_ref[...], za_ref[...], y0, gl, zb_ref[...], ra_ref[...], rb_ref[...], hn_ref[...])
        out = _mm(m, wo_ref[...])
        xn_ref[...] = _mix_post(x_ref[...], out, np_ref[...])

    act = pl.BlockSpec((tm, D), lambda i: (i, 0))
    return pl.pallas_call(
        body, name="mix_fwd", grid=(t // tm,),
        in_specs=[_proj_spec(tm, 3), _proj_spec(tm, 4), _proj_spec(tm, 5), _proj_spec(tm, 6), _proj_spec(tm, 7),
                  act, act, act, _full((1, DH)), _full((1, D)), _layer_mat(l), _full((1, D)), _layer_mat(l),
                  _full((1, D))],
        out_specs=act,
        out_shape=jax.ShapeDtypeStruct((t, D), F32),
        compiler_params=_cparams(("parallel",)),
    )(proj, proj, proj, proj, proj, o, s5y, x, hn, dvec, wglu, bglu, wout, npost)


def _mix_bwd(proj, o, s5y, x, hn, dvec, wglu, bglu, wout, npost, dxn, l):
    t = x.shape[0]
    tm = _tile(t, 128)

    def body(za_ref, u_ref, zb_ref, ra_ref, rb_ref, o_ref, y_ref, x_ref, hn_ref, d_ref, wg_ref, bg_ref, wo_ref,
             np_ref, dxn_ref,
             dza_ref, du_ref, dzb_ref, dra_ref, drb_ref, do_ref, dy_ref, dx_ref,
             y0_ref, dgl_ref, m_ref, dout_ref, dvecs_ref, dhn_ref):
        y0, vjp_pre = jax.vjp(_mix_pre, y_ref[...], u_ref[...], d_ref[...])
        gl = _mm(y0, wg_ref[...]) + bg_ref[...]
        m, vjp_mid = jax.vjp(_mix_mid, o_ref[...], za_ref[...], y0, gl, zb_ref[...], ra_ref[...], rb_ref[...],
                             hn_ref[...])
        out = _mm(m, wo_ref[...])
        _, vjp_post = jax.vjp(_mix_post, x_ref[...], out, np_ref[...])
        dx, dout, dnp = vjp_post(dxn_ref[...])
        dm = _mm_nt(dout, wo_ref[...])
        do, dza, dy0, dgl, dzb, dra, drb, dhn = vjp_mid(dm)
        y0_ref[...] = y0.astype(BF16)
        dgl_ref[...] = dgl.astype(BF16)
        m_ref[...] = m.astype(BF16)
        dout_ref[...] = dout.astype(BF16)
        dbg = jnp.sum(dgl, axis=0, keepdims=True)
        dy0 = dy0 + _mm_nt(dgl, wg_ref[...])
        dy, du, dd = vjp_pre(dy0)
        dza_ref[...] = dza.astype(dza_ref.dtype)
        du_ref[...] = du
        dzb_ref[...] = dzb.astype(dzb_ref.dtype)
        dra_ref[...] = dra.astype(dra_ref.dtype)
        drb_ref[...] = drb.astype(drb_ref.dtype)
        do_ref[...] = do
        dy_ref[...] = dy
        dx_ref[...] = dx
        first = pl.program_id(0) == 0

        @pl.when(first)
        def _():
            dvecs_ref[...] = jnp.zeros_like(dvecs_ref)
            dhn_ref[...] = jnp.zeros_like(dhn_ref)

        dvecs_ref[0:1, :] += dd
        dvecs_ref[1:2, :] += dbg
        dvecs_ref[2:3, :] += dnp
        dhn_ref[0:1, :] += dhn

    act = pl.BlockSpec((tm, D), lambda i: (i, 0))
    a, ga = jax.ShapeDtypeStruct((t, D), F32), jax.ShapeDtypeStruct((t, D), GRAD_ACT)
    b16 = jax.ShapeDtypeStruct((t, D), BF16)
    outs = pl.pallas_call(
        body, name="mix_bwd", grid=(t // tm,),
        in_specs=[_proj_spec(tm, 3), _proj_spec(tm, 4), _proj_spec(tm, 5), _proj_spec(tm, 6), _proj_spec(tm, 7),
                  act, act, act, _full((1, DH)), _full((1, D)), _layer_mat(l), _full((1, D)), _layer_mat(l),
                  _full((1, D)), act],
        out_specs=[act] * 12 + [_full((8, D)), _full((8, DH))],
        out_shape=[ga, a, ga, ga, ga, a, a, a, b16, b16, b16, b16, jax.ShapeDtypeStruct((8, D), F32),
                   jax.ShapeDtypeStruct((8, DH), F32)],
        compiler_params=_cparams(("arbitrary",)),
    )(proj, proj, proj, proj, proj, o, s5y, x, hn, dvec, wglu, bglu, wout, npost, dxn)
    y0, dgl, m, dout = outs[8:12]
    return list(outs[:8]) + [_weight_grad(y0, dgl, "glu_dw"), _weight_grad(m, dout, "out_dw")] + list(outs[12:])


def _weight_grad(a, b, name):
    t = a.shape[0]
    tk = _tile(t, 1024)

    def body(a_ref, b_ref, o_ref):
        @pl.when(pl.program_id(0) == 0)
        def _():
            o_ref[...] = jnp.zeros_like(o_ref)

        o_ref[...] += _mm_tn(a_ref[...], b_ref[...])

    rows = pl.BlockSpec((tk, D), lambda i: (i, 0))
    return pl.pallas_call(
        body, name=name, grid=(t // tk,), in_specs=[rows, rows], out_specs=_full((D, D)),
        out_shape=jax.ShapeDtypeStruct((D, D), F32), compiler_params=_cparams(("arbitrary",)),
    )(a, b)


def _loss_grad(y, target):
    t = y.shape[0]
    tm = _tile(t, 512)

    def body(y_ref, t_ref, dy_ref, l_ref):
        err = y_ref[...] - t_ref[...]
        dy_ref[...] = err * (1.0 / D)
        part = jnp.sum(jnp.sum(err * err, axis=1, keepdims=True), axis=0, keepdims=True) * (0.5 / D)
        part = jnp.broadcast_to(part, (8, DH))

        @pl.when(pl.program_id(0) == 0)
        def _():
            l_ref[...] = part

        @pl.when(pl.program_id(0) > 0)
        def _():
            l_ref[...] += part

    act = pl.BlockSpec((tm, D), lambda i: (i, 0))
    return pl.pallas_call(
        body, name="loss_grad", grid=(t // tm,),
        in_specs=[act, act], out_specs=[act, _full((8, DH))],
        out_shape=[jax.ShapeDtypeStruct((t, D), F32), jax.ShapeDtypeStruct((8, DH), F32)],
        compiler_params=_cparams(("arbitrary",)),
    )(y, target)


def _flips(rel):
    x, y, c = lax.axis_index("x"), lax.axis_index("y"), lax.axis_index("c")
    fx, fy, fc = rel
    return (x ^ fx if fx else x, y ^ fy if fy else y, c ^ fc if fc else c)


CHIP_RELS = ((1, 0, 0), (0, 1, 0), (1, 1, 0))
ALL_RELS = tuple((fx, fy, fc) for fx in (0, 1) for fy in (0, 1) for fc in (0, 1) if (fx, fy, fc) != (0, 0, 0))


def _slot_of(pos, by_chip):
    px, py, pc = pos
    return 2 * px + py if by_chip else 4 * px + 2 * py + pc


class _Exchange:
    def __init__(self, srcs, rels, by_chip, scatter):
        self.srcs, self.rels, self.by_chip, self.scatter = list(srcs), rels, by_chip, scatter
        self.narr = len(self.srcs)
        nslot, nsem = NCHIP if by_chip else NDEV, self.narr * len(rels)
        self.in_specs = [pl.BlockSpec(memory_space=pl.ANY)] * self.narr
        self.out_specs = [pl.BlockSpec(memory_space=pl.ANY)] * self.narr
        self.out_shape = [jax.ShapeDtypeStruct((nslot,) + s.shape[-2:], s.dtype) for s in self.srcs]
        self.scratch_shapes = [pltpu.SemaphoreType.DMA((nsem,)), pltpu.SemaphoreType.DMA((nsem,)),
                               pltpu.SemaphoreType.DMA((self.narr,))]

    def _copies(self, src_refs, dst_refs, sems):
        send_sems, recv_sems, local_sems = sems
        my_slot = _slot_of(_flips((0, 0, 0)), self.by_chip)
        local, sends, arrivals = [], [], []
        for a, (src_ref, dst_ref) in enumerate(zip(src_refs, dst_refs)):
            local.append(pltpu.make_async_copy(src_ref.at[my_slot] if self.scatter else src_ref, dst_ref.at[my_slot],
                                               local_sems.at[a]))
            for k, rel in enumerate(self.rels):
                peer = _flips(rel)
                pair = dict(send_sem=send_sems.at[a * len(self.rels) + k], recv_sem=recv_sems.at[a * len(self.rels) + k],
                            device_id=peer, device_id_type=pl.DeviceIdType.MESH)
                part = src_ref.at[_slot_of(peer, self.by_chip)] if self.scatter else src_ref
                sends.append(pltpu.make_async_remote_copy(src_ref=part, dst_ref=dst_ref.at[my_slot], **pair))
                arrivals.append(pltpu.make_async_remote_copy(
                    src_ref=src_ref.at[0] if self.scatter else src_ref, dst_ref=dst_ref.at[_slot_of(peer, self.by_chip)],
                    **pair))
        return local, sends, arrivals

    def start(self, src_refs, dst_refs, sems):
        local, sends, _ = self._copies(src_refs, dst_refs, sems)
        for cp in local + sends:
            cp.start()

    def wait(self, src_refs, dst_refs, sems):
        local, sends, arrivals = self._copies(src_refs, dst_refs, sems)
        for cp in arrivals:
            cp.wait_recv()
        for cp in sends:
            cp.wait_send()
        for cp in local:
            cp.wait()


def _exchange(srcs, rels, by_chip, scatter, name):
    ex = _Exchange(srcs, rels, by_chip, scatter)

    def body(*refs):
        parts = refs[:ex.narr], refs[ex.narr:2 * ex.narr], refs[2 * ex.narr:]
        ex.start(*parts)
        ex.wait(*parts)

    return pl.pallas_call(body, name=name, in_specs=ex.in_specs, out_specs=ex.out_specs, out_shape=ex.out_shape,
                          scratch_shapes=ex.scratch_shapes)(*ex.srcs)


def _sibling_swap(srcs, name):
    narr = len(srcs)

    def body(*refs):
        src_refs, dst_refs = refs[:narr], refs[narr:2 * narr]
        send_sems, recv_sems = refs[2 * narr:]
        peer = _flips((0, 0, 1))
        copies = [pltpu.make_async_remote_copy(src_ref=s, dst_ref=d, send_sem=send_sems.at[a], recv_sem=recv_sems.at[a],
                                               device_id=peer, device_id_type=pl.DeviceIdType.MESH)
                  for a, (s, d) in enumerate(zip(src_refs, dst_refs))]
        for cp in copies:
            cp.start()
        for cp in copies:
            cp.wait()

    return pl.pallas_call(
        body, name=name,
        in_specs=[pl.BlockSpec(memory_space=pl.ANY)] * narr,
        out_specs=[pl.BlockSpec(memory_space=pl.ANY)] * narr,
        out_shape=[jax.ShapeDtypeStruct(s.shape, s.dtype) for s in srcs],
        scratch_shapes=[pltpu.SemaphoreType.DMA((narr,)), pltpu.SemaphoreType.DMA((narr,))],
    )(*srcs)


def _all_reduce(src, name):
    rows, cols = src.shape
    r = rows // NDEV
    nrel = len(ALL_RELS)

    def body(src_ref, out_ref, parts_ref, mine_ref, send_sems, recv_sems):
        my_slot = _slot_of(_flips((0, 0, 0)), False)

        def piece(ref, slot):
            return ref.at[pl.ds(pl.multiple_of(slot * r, 8), r), :]

        def copies(phase):
            out = []
            for k, rel in enumerate(ALL_RELS):
                peer = _flips(rel)
                pair = dict(send_sem=send_sems.at[phase * nrel + k], recv_sem=recv_sems.at[phase * nrel + k],
                            device_id=peer, device_id_type=pl.DeviceIdType.MESH)
                if phase == 0:
                    out.append(pltpu.make_async_remote_copy(src_ref=piece(src_ref, _slot_of(peer, False)),
                                                            dst_ref=parts_ref.at[my_slot], **pair))
                else:
                    out.append(pltpu.make_async_remote_copy(src_ref=mine_ref, dst_ref=piece(out_ref, my_slot), **pair))
            return out

        first = copies(0)
        for cp in first:
            cp.start()
        parts_ref[my_slot] = piece(src_ref, my_slot)[...]
        for cp in first:
            cp.wait_recv()
        acc = parts_ref[0]
        for s in range(1, NDEV):
            acc = acc + parts_ref[s]
        mine_ref[...] = acc
        second = copies(1)
        for cp in second:
            cp.start()
        piece(out_ref, my_slot)[...] = acc
        for cp in second:
            cp.wait_recv()
        for cp in first + second:
            cp.wait_send()

    return pl.pallas_call(
        body, name=name,
        in_specs=[pl.BlockSpec(memory_space=pltpu.VMEM)], out_specs=pl.BlockSpec(memory_space=pltpu.VMEM),
        out_shape=jax.ShapeDtypeStruct(src.shape, src.dtype),
        scratch_shapes=[pltpu.VMEM((NDEV, r, cols), src.dtype), pltpu.VMEM((r, cols), src.dtype),
                        pltpu.SemaphoreType.DMA((2 * nrel,)), pltpu.SemaphoreType.DMA((2 * nrel,))],
        compiler_params=pltpu.CompilerParams(vmem_limit_bytes=VMEM_LIMIT),
    )(src)


def _sum_slots(parts, name):
    ns, rows, cols = parts.shape
    tr = _row_tile(rows, 256)

    def body(p_ref, o_ref):
        acc = p_ref[0].astype(F32)
        for s in range(1, ns):
            acc = acc + p_ref[s].astype(F32)
        o_ref[...] = acc

    return pl.pallas_call(
        body, name=name, grid=(rows // tr,),
        in_specs=[pl.BlockSpec((ns, tr, cols), lambda i: (0, i, 0))],
        out_specs=pl.BlockSpec((tr, cols), lambda i: (i, 0)),
        out_shape=jax.ShapeDtypeStruct((rows, cols), F32),
        compiler_params=_cparams(("parallel",)),
    )(parts)


def _adamw_update(w, g, m, v):
    c1 = 1.0 / (1.0 - ADAM_B1 ** ADAM_STEP)
    c2 = 1.0 / (1.0 - ADAM_B2 ** ADAM_STEP)
    nm = ADAM_B1 * m + (1.0 - ADAM_B1) * g
    nv = ADAM_B2 * v + (1.0 - ADAM_B2) * (g * g)
    return -ADAM_LR * ((nm * c1) / (jnp.sqrt(nv * c2) + ADAM_EPS) + ADAM_WD * w), nm, nv


def _adamw_layers(w, g, m, v, name):
    def body(w_ref, g_ref, m_ref, v_ref, d_ref, nm_ref, nv_ref):
        d_ref[...], nm_ref[...], nv_ref[...] = _adamw_update(w_ref[...], g_ref[...], m_ref[...], v_ref[...])

    blk = pl.BlockSpec((None,) + w.shape[1:], lambda l: (l,) + (0,) * (w.ndim - 1))
    return pl.pallas_call(
        body, name=name, grid=(w.shape[0],), in_specs=[blk] * 4, out_specs=[blk] * 3,
        out_shape=[jax.ShapeDtypeStruct(w.shape, F32)] * 3, compiler_params=_cparams(("parallel",)),
    )(w, g, m, v)


def _adamw(w, g_parts, m, v, name, max_rows=256):
    if w.ndim == 2:
        return [o[0] for o in _adamw(w[None], g_parts, m[None], v[None], name, max_rows)]
    nl, rows, cols = w.shape
    tr = _row_tile(rows, max_rows)
    per_layer = rows // tr
    c1 = 1.0 / (1.0 - ADAM_B1 ** ADAM_STEP)
    c2 = 1.0 / (1.0 - ADAM_B2 ** ADAM_STEP)
    npart = len(g_parts)

    def body(*refs):
        w_ref, m_ref, v_ref = refs[:3]
        g_refs = refs[3:3 + npart]
        go_ref, d_ref, nm_ref, nv_ref = refs[3 + npart:]
        terms = []
        for g_ref in g_refs:
            terms += [g_ref[...]] if len(g_ref.shape) == 2 else [g_ref[s] for s in range(g_ref.shape[0])]
        g = terms[0]
        for term in terms[1:]:
            g = g + term
        go_ref[...] = g
        d_ref[...], nm_ref[...], nv_ref[...] = _adamw_update(w_ref[...], g, m_ref[...], v_ref[...])

    blk = pl.BlockSpec((None, tr, cols), lambda l, i: (l, i, 0))
    g_specs = [pl.BlockSpec((tr, cols), lambda l, i: (l * per_layer + i, 0)) if p.ndim == 2 else
               pl.BlockSpec((p.shape[0], tr, cols), lambda l, i: (0, l * per_layer + i, 0)) for p in g_parts]
    out = jax.ShapeDtypeStruct((nl, rows, cols), F32)
    return pl.pallas_call(
        body, name=name, grid=(nl, per_layer),
        in_specs=[blk, blk, blk] + g_specs,
        out_specs=[blk] * 4, out_shape=[out] * 4,
        compiler_params=_cparams(("parallel", "parallel")),
    )(w, m, v, *g_parts)


WEIGHT_SPLIT = (0, 384, 704, D)
WIN_SHARD = 2052
CONV_SHARD = 768
ROW_SHARD = 256

SMALL_TINY = (("norm_pre", (DEPTH, D)), ("a_log", (DEPTH, NH)), ("dt_bias", (DEPTH, NH)), ("head_norm", (DEPTH, DH)),
              ("ssm_a_re", (DEPTH, NG, NS)), ("ssm_a_im", (DEPTH, NG, NS)), ("ssm_log_dt", (DEPTH, NG)),
              ("ssm_d", (DEPTH, D)), ("b_glu", (DEPTH, D)), ("norm_post", (DEPTH, D)))
SMALL_BIG = (("ssm_b_re", (DEPTH, NG, NS, GS)), ("ssm_b_im", (DEPTH, NG, NS, GS)),
             ("ssm_c_re", (DEPTH, NG, GS, NS)), ("ssm_c_im", (DEPTH, NG, GS, NS)))
SMALL = SMALL_TINY + SMALL_BIG


def _pad_rows(flat, rows):
    return jnp.pad(flat, (0, rows * D - flat.shape[0])).reshape(rows, D)


def _cols_from_chips(a, nl):
    _, rows, cols = a.shape
    return a.reshape(NCHIP, nl, rows // nl, cols).transpose(1, 2, 0, 3).reshape(nl, rows // nl, NCHIP * cols)


SMALL_ROWS = sum(-(-math.prod(s) // (8 * D)) * 8 for _, s in SMALL)
CONV_ROWS = DEPTH * 4 * 3 * D // D


def _pack_small(vals, extra=()):
    parts = []
    for val in tuple(vals) + tuple(extra):
        n = val.size
        parts.append(_pad_rows(val.reshape(-1), -(-n // (8 * D)) * 8))
    return jnp.concatenate(parts, axis=0)


def _packed_rows(entries):
    return sum(-(-math.prod(s) // (8 * D)) * 8 for _, s in entries)


def _unpack_small(flat, entries):
    outs, r0 = [], 0
    for _, shape in entries:
        n = math.prod(shape)
        rows = -(-n // (8 * D)) * 8
        outs.append(flat[r0:r0 + rows].reshape(-1)[:n].reshape(shape))
        r0 += rows
    return outs


LOGITS_IN_CHIP1 = 2 * WIN_SHARD - 4 * D
LOGITS_IN_CHIP2 = 2 * NH - LOGITS_IN_CHIP1


def _wcat_from_chips(g):
    before = WIN_SHARD - LOGITS_IN_CHIP1
    pad = jnp.zeros((D, NCOL - BD0 - 2 * NH), g.dtype)
    return jnp.concatenate([g[0], g[1][:, :before], g[2][:, LOGITS_IN_CHIP2:], g[3], g[1][:, before:],
                            g[2][:, :LOGITS_IN_CHIP2], pad], axis=1)[None]


def _wcat_grad_by_chip(gw):
    before, mid = WIN_SHARD - LOGITS_IN_CHIP1, 4 * D + WIN_SHARD - LOGITS_IN_CHIP2
    return jnp.stack([gw[:, :WIN_SHARD],
                      jnp.concatenate([gw[:, WIN_SHARD:4 * D], gw[:, BD0:BD0 + LOGITS_IN_CHIP1]], axis=1),
                      jnp.concatenate([gw[:, BD0 + LOGITS_IN_CHIP1:BD0 + 2 * NH], gw[:, 4 * D:mid]], axis=1),
                      gw[:, mid:BD0]])


def _block_diag_b(bb2):
    b = bb2.reshape(-1, NCB, GPB, NS, GS)
    return jnp.einsum("lkgnc,gh->lkgchn", b, jnp.eye(GPB, dtype=F32)).reshape(-1, NCB, GPB * GS, SW)


def _block_diag_b_t(d):
    blocks = jnp.einsum("lkgchn,gh->lkgnc", d.reshape(-1, NCB, GPB, GS, GPB, NS), jnp.eye(GPB, dtype=F32))
    return blocks.reshape(-1, NG, NS * GS)


def _block_diag_c(c):
    blocks = jnp.einsum("lkgcn,gh->lkgnhc", c.reshape(-1, NCB, GPB, GS, NS), jnp.eye(GPB, dtype=F32))
    return blocks.reshape(-1, NCB, SW, GPB * GS)


def _block_diag_c_t(d):
    blocks = jnp.einsum("lkgnhc,gh->lkgcn", d.reshape(-1, NCB, GPB, NS, GPB, GS), jnp.eye(GPB, dtype=F32))
    return blocks.reshape(-1, NG, GS, NS)


def _local_step(x, target, weights, conv, small, comm=None):
    weights = list(weights) + [None] * (DEPTH - len(weights))
    ar, ai = small["ssm_a_re"], small["ssm_a_im"]
    ldt = small["ssm_log_dt"].reshape(DEPTH, NG, 1)
    br2 = small["ssm_b_re"].reshape(DEPTH, NG, NS * GS)
    bi2 = small["ssm_b_im"].reshape(DEPTH, NG, NS * GS)
    lr, li, bbr2, bbi2 = _s5_params(ar, ai, ldt, br2, bi2)

    def row(name, l, width):
        return small[name][l].reshape(1, width)

    gvecs = jnp.pad(jnp.stack([small["a_log"], small["dt_bias"]], axis=1), ((0, 0), (0, 6), (NH, DH - 2 * NH)))
    lams = jnp.stack([lr.reshape(DEPTH, 1, NG * NS), li.reshape(DEPTH, 1, NG * NS)], axis=1)
    bblks = jnp.stack([_block_diag_b(bbr2), _block_diag_b(bbi2)], axis=1)
    cblks = jnp.stack([_block_diag_c(small["ssm_c_re"]), _block_diag_c(small["ssm_c_im"])], axis=1)
    saved = []
    for l in range(DEPTH):
        gvec, lam, bblk, cblk = gvecs[l], lams[l], bblks[l], cblks[l]
        wcat, wglu, wout = weights[l]
        fetch = [None] * 3
        if comm and l + 1 < DEPTH:
            fetch = [_Exchange(comm["weight_parts"](l + 1, part), CHIP_RELS, True, False) for part in range(3)]
        proj, h, got0 = _inproj_fwd(x, row("norm_pre", l, D), wcat, 0, fetch[0])
        qkv = _prep_fwd(proj, conv[l])
        bb, gcb = _gates_fwd(proj, gvec)
        local, t_inv, got1 = _delta_local_fwd(qkv, bb, gcb, fetch[1])
        o, states = _delta_state_fwd(local, gcb)
        s5y, carries, got2 = _s5_fwd(proj, lam, bblk, cblk, fetch[2])
        if fetch[0] is not None:
            weights[l + 1] = comm["weights_from"]([got0, got1, got2])
        xn = _mix_fwd(proj, o, s5y, x, row("head_norm", l, DH), row("ssm_d", l, D), wglu, row("b_glu", l, D),
                      wout, row("norm_post", l, D), 0)
        saved.append((x, proj, h, qkv, bb, gcb, local, t_inv, o, states, s5y, carries, gvec, lam, bblk, cblk))
        x = xn

    dx, loss_part = _loss_grad(x, target)

    g = {k: [None] * DEPTH for k in ("wcat", "conv", "wglu", "wout", "norm_pre", "a_log", "dt_bias", "head_norm",
                                     "ssm_d", "b_glu", "norm_post", "dlam", "dbblk", "dcblk")}
    from_chips, send, send_layer = [None] * DEPTH, None, None
    for l in reversed(range(DEPTH)):
        xl, proj, h, qkv, bb, gcb, local, t_inv, o, states, s5y, carries, gvec, lam, bblk, cblk = saved[l]
        wcat, wglu, wout = weights[l]
        (dza, du_skip, dzb, dra, drb, do, ds5y, dxres, dwg, dwo, dvecs, dhn) = _mix_bwd(
            proj, o, s5y, xl, row("head_norm", l, DH), row("ssm_d", l, D), wglu, row("b_glu", l, D), wout,
            row("norm_post", l, D), dx, 0)
        (du, dlam, dbblk, dcblk), arrived = _s5_bwd(proj, lam, bblk, cblk, carries, ds5y, du_skip, send)
        if send is not None:
            from_chips[send_layer] = arrived
        *dlocal, dgcb_state = _delta_state_bwd(local, gcb, states, do)
        dq, dk, dv, dbb, dgcb = _delta_local_bwd(qkv, bb, gcb, t_inv, dlocal, dgcb_state)
        dbd, dgvec = _gates_bwd(proj, gvec, dbb, dgcb)
        dpre, dconv = _prep_bwd(proj, conv[l], dq, dk, dv)
        dproj = jnp.concatenate([dpre, dza, du, dzb, dra, drb, dbd], axis=1)
        g["wcat"][l] = _inproj_bwd_dw(h, dproj)
        g["conv"][l], g["wglu"][l], g["wout"][l] = dconv, dwg, dwo
        send = _Exchange(comm["grad_parts"](g["wcat"][l], dwg, dwo), CHIP_RELS, True, True) if comm else None
        dx, dgain, arrived = _inproj_bwd_dx(dproj, wcat, xl, row("norm_pre", l, D), dxres, 0, send if l == 0 else None)
        if comm and l == 0:
            from_chips[l] = arrived
        send_layer = l
        g["norm_pre"][l] = dgain[0]
        g["a_log"][l], g["dt_bias"][l] = dgvec[0, NH:2 * NH], dgvec[1, NH:2 * NH]
        g["head_norm"][l] = dhn[0]
        g["ssm_d"][l], g["b_glu"][l], g["norm_post"][l] = dvecs[0], dvecs[1], dvecs[2]
        g["dlam"][l], g["dbblk"][l], g["dcblk"][l] = dlam, dbblk, dcblk
    if comm:
        for k in ("wcat", "wglu", "wout"):
            del g[k]
    g = {k: jnp.stack(v) for k, v in g.items()}
    g["from_chips"] = from_chips
    dlam, dbblk, dcblk = g.pop("dlam"), g.pop("dbblk"), g.pop("dcblk")
    g["ssm_c_re"], g["ssm_c_im"] = _block_diag_c_t(dcblk[:, 0]), _block_diag_c_t(dcblk[:, 1])
    dar, dai, dldt, dbr2, dbi2 = _s5_params_bwd(
        ar, ai, ldt, br2, bi2, dlam[:, 0].reshape(DEPTH, NG, NS), dlam[:, 1].reshape(DEPTH, NG, NS),
        _block_diag_b_t(dbblk[:, 0]), _block_diag_b_t(dbblk[:, 1]))
    g["ssm_a_re"], g["ssm_a_im"], g["ssm_log_dt"] = dar, dai, dldt.reshape(DEPTH, NG)
    g["ssm_b_re"] = dbr2.reshape(DEPTH, NG, NS, GS)
    g["ssm_b_im"] = dbi2.reshape(DEPTH, NG, NS, GS)
    return loss_part[0, 0], dx, g


def kernel(x, norm_pre, w_in, conv_w, a_log, dt_bias, head_norm, ssm_a_re, ssm_a_im, ssm_log_dt, ssm_b_re, ssm_b_im, ssm_c_re, ssm_c_im, ssm_d, w_glu, b_glu, w_out, norm_post, loss_target, m_norm_pre, m_w_in, m_conv_w, m_a_log, m_dt_bias, m_head_norm, m_ssm_a_re, m_ssm_a_im, m_ssm_log_dt, m_ssm_b_re, m_ssm_b_im, m_ssm_c_re, m_ssm_c_im, m_ssm_d, m_w_glu, m_b_glu, m_w_out, m_norm_post, v_norm_pre, v_w_in, v_conv_w, v_a_log, v_dt_bias, v_head_norm, v_ssm_a_re, v_ssm_a_im, v_ssm_log_dt, v_ssm_b_re, v_ssm_b_im, v_ssm_c_re, v_ssm_c_im, v_ssm_d, v_w_glu, v_b_glu, v_w_out, v_norm_post):
    args = dict(locals())
    small = {n: args[n] for n, _ in SMALL}

    def flat2(a):
        return a.reshape(-1, a.shape[-1])

    w_in16, w_glu16, w_out16 = w_in.astype(BF16), w_glu.astype(BF16), w_out.astype(BF16)

    def weight_parts(l, part=None):
        if part is None:
            return [w_in16[l], w_glu16[l], w_out16[l]]
        lo, hi = WEIGHT_SPLIT[part], WEIGHT_SPLIT[part + 1]
        return [w_in16[l, lo:hi]] + [[w_glu16[l]], [w_out16[l]], []][part]

    def weights_from(parts):
        if len(parts) == 3 and isinstance(parts[0], (list, tuple)):
            parts = [jnp.concatenate([p[0] for p in parts], axis=1), parts[0][1], parts[1][1]]
        g_in, g_glu, g_out = parts[:3]
        return _wcat_from_chips(g_in), g_glu.reshape(1, D, D), g_out.reshape(1, D, D)

    def grad_parts(gwcat, gwglu, gwout):
        return [_wcat_grad_by_chip(gwcat).astype(BF16), gwglu.reshape(NCHIP, ROW_SHARD, D).astype(BF16),
                gwout.reshape(NCHIP, ROW_SHARD, D).astype(BF16)]

    core = lax.axis_index("c")

    def my_half(a):
        return lax.dynamic_slice_in_dim(a, core * (a.shape[0] // 2), a.shape[0] // 2, axis=0)

    def join(mine, theirs):
        return jnp.concatenate([jnp.where(core == 0, mine, theirs), jnp.where(core == 0, theirs, mine)], axis=1)

    halves = _exchange([my_half(p) for p in weight_parts(0) + [flat2(conv_w)]], CHIP_RELS, True, False,
                       "gather_weights")
    first = [join(m, t) for m, t in zip(halves, _sibling_swap(halves, "swap_weights"))]
    conv = _cols_from_chips(first[3], DEPTH)
    comm = dict(weight_parts=weight_parts, weights_from=weights_from, grad_parts=grad_parts)
    loss_part, dx, g = _local_step(x[0], loss_target[0], [weights_from(first)], conv, small, comm)
    loss = lax.psum(loss_part, ("x", "y", "c"))

    from_chips = [jnp.concatenate([g["from_chips"][l][a] for l in range(DEPTH)], axis=1) for a in range(3)]
    core_sums = [_sum_slots(p, "sum_chips_" + n) for p, n in zip(from_chips, ("in", "glu", "out"))]
    others = _sibling_swap(core_sums, "swap_cores")
    sharded = {}
    for n, mine, other in zip(("w_in", "w_glu", "w_out"), core_sums, others):
        sharded[n] = _adamw(args[n], [mine, other], args["m_" + n], args["v_" + n], "adamw_" + n, max_rows=128)

    pad = jnp.zeros(((-(SMALL_ROWS + CONV_ROWS)) % (8 * NDEV), D), F32)
    small_sum = _all_reduce(_pack_small([g[n] for n, _ in SMALL], extra=[g["conv"], pad]), "reduce_small")
    small_out = _adamw(_pack_small([args[n] for n, _ in SMALL_TINY]), [small_sum],
                       _pack_small([args["m_" + n] for n, _ in SMALL_TINY]),
                       _pack_small([args["v_" + n] for n, _ in SMALL_TINY]), "adamw_small")
    row = _packed_rows(SMALL_TINY)
    for n, shape in SMALL_BIG:
        rows = _packed_rows([(n, shape)])
        g_own = small_sum[row:row + rows].reshape(shape)
        sharded[n] = [g_own] + list(_adamw_layers(args[n], g_own, args["m_" + n], args["v_" + n], "adamw_" + n))
        row += rows
    chip = 2 * lax.axis_index("x") + lax.axis_index("y")
    conv_sum = small_sum[SMALL_ROWS:SMALL_ROWS + CONV_ROWS].reshape(DEPTH * 4, 3 * D)
    conv_sum = lax.dynamic_slice_in_dim(conv_sum, chip * CONV_SHARD, CONV_SHARD, axis=1)
    sharded["conv_w"] = _adamw(flat2(conv_w), [conv_sum], flat2(m_conv_w), flat2(v_conv_w), "adamw_conv")

    names = ["norm_pre", "w_in", "conv_w", "a_log", "dt_bias", "head_norm", "ssm_a_re", "ssm_a_im", "ssm_log_dt",
             "ssm_b_re", "ssm_b_im", "ssm_c_re", "ssm_c_im", "ssm_d", "w_glu", "b_glu", "w_out", "norm_post"]
    outs = [loss, dx[None]]
    for i in range(4):
        sm = dict(zip([n for n, _ in SMALL_TINY], _unpack_small(small_out[i], SMALL_TINY)))
        outs += [sharded[n][i].reshape(args[n].shape) if n in sharded else sm[n] for n in names]
    return tuple(outs)
```
